```python
import math
import jax, jax.numpy as jnp
from jax import lax
import numpy as np

D_MODEL = 1024
BATCH = 8
SEQ = 8192
DEPTH = 2

N_META = 16
N_A = DEPTH // 2
N_B = DEPTH - N_A
N_HEADS = 8
HEAD_DIM = D_MODEL // N_HEADS
D_FF = 2816
CONV_WIDTH = 3
BLOCK = 128
PAD = (-N_META) % BLOCK
LN_EPS = 1e-5
DEEPNORM_ALPHA = (2 * DEPTH) ** 0.25
DEEPNORM_BETA = (8 * DEPTH) ** -0.25
NEG_INF = -1e30

kernel_name = "yoco_shortconv_fox_macaron_deepnorm"


def layer_norm(x, g, b):
    xf = x.astype(jnp.float32)
    mu = jnp.mean(xf, axis=-1, keepdims=True)
    var = jnp.mean(jnp.square(xf - mu), axis=-1, keepdims=True)
    y = (xf - mu) * lax.rsqrt(var + LN_EPS) * g.astype(jnp.float32) + b.astype(jnp.float32)
    return y.astype(x.dtype)


def swiglu(x, wg, wu, wd):
    return (jax.nn.silu(x @ wg) * (x @ wu)) @ wd


def short_conv(x, w_in, w_conv, w_out):
    bgate, cgate, val = jnp.split(x @ w_in, 3, axis=-1)
    u = cgate * val
    y = lax.conv_general_dilated(
        u, w_conv[:, None, :].astype(u.dtype),
        window_strides=(1,), padding=[(CONV_WIDTH - 1, 0)],
        dimension_numbers=("NWC", "WIO", "NWC"),
        feature_group_count=D_MODEL)
    return (bgate * y) @ w_out


def shared_kv(h, kv_w, f_bias):
    bsz, L, _ = h.shape
    kvf = h @ kv_w
    k = kvf[..., :D_MODEL].reshape(bsz, L, N_HEADS, HEAD_DIM)
    v = kvf[..., D_MODEL:2 * D_MODEL].reshape(bsz, L, N_HEADS, HEAD_DIM)
    f_logit = (kvf[..., 2 * D_MODEL:] + f_bias).astype(jnp.float32)
    log_f = jax.nn.log_sigmoid(f_logit)
    pad4 = ((0, 0), (PAD, 0), (0, 0), (0, 0))
    k = jnp.pad(k, pad4).transpose(0, 2, 1, 3)
    v = jnp.pad(v, pad4).transpose(0, 2, 1, 3)
    log_f = jnp.pad(log_f, ((0, 0), (PAD, 0), (0, 0)))
    c = jnp.cumsum(log_f, axis=1).transpose(0, 2, 1)
    return k, v, c


def forgetting_attention(h, w_q, w_o, k, v, c):
    bsz, L, _ = h.shape
    Lp = L + PAD
    n_blocks = Lp // BLOCK
    scale = 1.0 / math.sqrt(HEAD_DIM)
    q = (h @ w_q).reshape(bsz, L, N_HEADS, HEAD_DIM)
    q = jnp.pad(q, ((0, 0), (PAD, 0), (0, 0), (0, 0))).transpose(0, 2, 1, 3)
    k_pos = jnp.arange(Lp)

    def one_block(i):
        start = i * BLOCK
        qb = lax.dynamic_slice_in_dim(q, start, BLOCK, axis=2)
        cq = lax.dynamic_slice_in_dim(c, start, BLOCK, axis=2)
        s = jnp.einsum("bhqd,bhkd->bhqk", qb, k).astype(jnp.float32) * scale
        s = s + cq[..., :, None] - c[..., None, :]
        q_pos = start + jnp.arange(BLOCK)
        mask = (k_pos[None, :] <= q_pos[:, None]) & (k_pos[None, :] >= PAD)
        s = jnp.where(mask, s, NEG_INF)
        p = jax.nn.softmax(s, axis=-1)
        return jnp.einsum("bhqk,bhkd->bhqd", p.astype(v.dtype), v)

    o = lax.map(one_block, jnp.arange(n_blocks))
    o = o.transpose(1, 0, 3, 2, 4).reshape(bsz, Lp, D_MODEL)[:, PAD:]
    return o @ w_o


def _fwd_setup_inputs(seed: int = 0) -> dict:
    key = jax.random.key(seed)
    ks = jax.random.split(key, 20)
    f32 = jnp.float32

    def nrm(k, shape, scale):
        return jax.random.normal(k, shape, f32) * scale

    d_s = D_MODEL ** -0.5
    f_s = D_FF ** -0.5
    x = nrm(ks[0], (BATCH, SEQ, D_MODEL), 1.0)
    meta = nrm(ks[1], (N_META, D_MODEL), 1.0)
    ffn1_wg = nrm(ks[2], (DEPTH, D_MODEL, D_FF), d_s)
    ffn1_wu = nrm(ks[3], (DEPTH, D_MODEL, D_FF), d_s)
    ffn1_wd = nrm(ks[4], (DEPTH, D_FF, D_MODEL), f_s * DEEPNORM_BETA)
    ffn2_wg = nrm(ks[5], (DEPTH, D_MODEL, D_FF), d_s)
    ffn2_wu = nrm(ks[6], (DEPTH, D_MODEL, D_FF), d_s)
    ffn2_wd = nrm(ks[7], (DEPTH, D_FF, D_MODEL), f_s * DEEPNORM_BETA)
    ln_gain = 1.0 + nrm(ks[8], (DEPTH, 3, D_MODEL), 0.02)
    ln_bias = nrm(ks[9], (DEPTH, 3, D_MODEL), 0.02)
    conv_w_in = nrm(ks[10], (N_A, D_MODEL, 3 * D_MODEL), d_s)
    conv_w = nrm(ks[11], (N_A, CONV_WIDTH, D_MODEL), CONV_WIDTH ** -0.5)
    conv_w_out = nrm(ks[12], (N_A, D_MODEL, D_MODEL), d_s * DEEPNORM_BETA)
    w_k = nrm(ks[13], (D_MODEL, D_MODEL), d_s)
    w_v = nrm(ks[14], (D_MODEL, D_MODEL), d_s * DEEPNORM_BETA)
    w_f = nrm(ks[15], (D_MODEL, N_HEADS), d_s * 0.5)
    kv_w = jnp.concatenate([w_k, w_v, w_f], axis=1)
    f_bias = 3.0 + nrm(ks[16], (N_HEADS,), 0.5)
    attn_w_q = nrm(ks[17], (N_B, D_MODEL, D_MODEL), d_s)
    attn_w_o = nrm(ks[18], (N_B, D_MODEL, D_MODEL), d_s * DEEPNORM_BETA)
    return {"x": x, "meta": meta,
            "ffn1_wg": ffn1_wg, "ffn1_wu": ffn1_wu, "ffn1_wd": ffn1_wd,
            "ffn2_wg": ffn2_wg, "ffn2_wu": ffn2_wu, "ffn2_wd": ffn2_wd,
            "ln_gain": ln_gain, "ln_bias": ln_bias,
            "conv_w_in": conv_w_in, "conv_w": conv_w, "conv_w_out": conv_w_out,
            "kv_w": kv_w, "f_bias": f_bias,
            "attn_w_q": attn_w_q, "attn_w_o": attn_w_o}


def _fwd_reference(x, meta, ffn1_wg, ffn1_wu, ffn1_wd, ffn2_wg, ffn2_wu, ffn2_wd,
              ln_gain, ln_bias, conv_w_in, conv_w, conv_w_out, kv_w, f_bias,
              attn_w_q, attn_w_o):
    bsz = x.shape[0]
    h = jnp.concatenate(
        [jnp.broadcast_to(meta.astype(x.dtype)[None], (bsz, N_META, D_MODEL)), x], axis=1)
    k_sh = v_sh = c_sh = None
    for l in range(DEPTH):
        h = layer_norm(DEEPNORM_ALPHA * h + 0.5 * swiglu(h, ffn1_wg[l], ffn1_wu[l], ffn1_wd[l]),
                       ln_gain[l, 0], ln_bias[l, 0])
        if l < N_A:
            mix = short_conv(h, conv_w_in[l], conv_w[l], conv_w_out[l])
        else:
            j = l - N_A
            mix = forgetting_attention(h, attn_w_q[j], attn_w_o[j], k_sh, v_sh, c_sh)
        h = layer_norm(DEEPNORM_ALPHA * h + mix, ln_gain[l, 1], ln_bias[l, 1])
        h = layer_norm(DEEPNORM_ALPHA * h + 0.5 * swiglu(h, ffn2_wg[l], ffn2_wu[l], ffn2_wd[l]),
                       ln_gain[l, 2], ln_bias[l, 2])
        if l == N_A - 1:
            k_sh, v_sh, c_sh = shared_kv(h, kv_w, f_bias)
    return h[:, N_META:]


import jax as _jax
import jax.numpy as _jnp

TWIN_FORMAT = 'train_step'
FWD_PARAMS = ['x', 'meta', 'ffn1_wg', 'ffn1_wu', 'ffn1_wd', 'ffn2_wg', 'ffn2_wu', 'ffn2_wd', 'ln_gain', 'ln_bias', 'conv_w_in', 'conv_w', 'conv_w_out', 'kv_w', 'f_bias', 'attn_w_q', 'attn_w_o']
TWIN_WEIGHTS = ['meta', 'ffn1_wg', 'ffn1_wu', 'ffn1_wd', 'ffn2_wg', 'ffn2_wu', 'ffn2_wd', 'ln_gain', 'ln_bias', 'conv_w_in', 'conv_w', 'conv_w_out', 'kv_w', 'f_bias', 'attn_w_q', 'attn_w_o']
TWIN_DIFF_INPUT = 'x'
TWIN_INPUTS = ['x', 'meta', 'ffn1_wg', 'ffn1_wu', 'ffn1_wd', 'ffn2_wg', 'ffn2_wu', 'ffn2_wd', 'ln_gain', 'ln_bias', 'conv_w_in', 'conv_w', 'conv_w_out', 'kv_w', 'f_bias', 'attn_w_q', 'attn_w_o', 'loss_target', 'm_meta', 'm_ffn1_wg', 'm_ffn1_wu', 'm_ffn1_wd', 'm_ffn2_wg', 'm_ffn2_wu', 'm_ffn2_wd', 'm_ln_gain', 'm_ln_bias', 'm_conv_w_in', 'm_conv_w', 'm_conv_w_out', 'm_kv_w', 'm_f_bias', 'm_attn_w_q', 'm_attn_w_o', 'v_meta', 'v_ffn1_wg', 'v_ffn1_wu', 'v_ffn1_wd', 'v_ffn2_wg', 'v_ffn2_wu', 'v_ffn2_wd', 'v_ln_gain', 'v_ln_bias', 'v_conv_w_in', 'v_conv_w', 'v_conv_w_out', 'v_kv_w', 'v_f_bias', 'v_attn_w_q', 'v_attn_w_o']
TWIN_OUTPUTS = ['loss', 'grad_x', 'grad_meta', 'grad_ffn1_wg', 'grad_ffn1_wu', 'grad_ffn1_wd', 'grad_ffn2_wg', 'grad_ffn2_wu', 'grad_ffn2_wd', 'grad_ln_gain', 'grad_ln_bias', 'grad_conv_w_in', 'grad_conv_w', 'grad_conv_w_out', 'grad_kv_w', 'grad_f_bias', 'grad_attn_w_q', 'grad_attn_w_o', 'delta_meta', 'delta_ffn1_wg', 'delta_ffn1_wu', 'delta_ffn1_wd', 'delta_ffn2_wg', 'delta_ffn2_wu', 'delta_ffn2_wd', 'delta_ln_gain', 'delta_ln_bias', 'delta_conv_w_in', 'delta_conv_w', 'delta_conv_w_out', 'delta_kv_w', 'delta_f_bias', 'delta_attn_w_q', 'delta_attn_w_o', 'new_m_meta', 'new_m_ffn1_wg', 'new_m_ffn1_wu', 'new_m_ffn1_wd', 'new_m_ffn2_wg', 'new_m_ffn2_wu', 'new_m_ffn2_wd', 'new_m_ln_gain', 'new_m_ln_bias', 'new_m_conv_w_in', 'new_m_conv_w', 'new_m_conv_w_out', 'new_m_kv_w', 'new_m_f_bias', 'new_m_attn_w_q', 'new_m_attn_w_o', 'new_v_meta', 'new_v_ffn1_wg', 'new_v_ffn1_wu', 'new_v_ffn1_wd', 'new_v_ffn2_wg', 'new_v_ffn2_wu', 'new_v_ffn2_wd', 'new_v_ln_gain', 'new_v_ln_bias', 'new_v_conv_w_in', 'new_v_conv_w', 'new_v_conv_w_out', 'new_v_kv_w', 'new_v_f_bias', 'new_v_attn_w_q', 'new_v_attn_w_o']
TWIN_LEAF_KINDS = {'loss': 'loss', 'grad_x': 'grad_x', 'grad_meta': 'grad_w', 'grad_ffn1_wg': 'grad_w', 'grad_ffn1_wu': 'grad_w', 'grad_ffn1_wd': 'grad_w', 'grad_ffn2_wg': 'grad_w', 'grad_ffn2_wu': 'grad_w', 'grad_ffn2_wd': 'grad_w', 'grad_ln_gain': 'grad_w', 'grad_ln_bias': 'grad_w', 'grad_conv_w_in': 'grad_w', 'grad_conv_w': 'grad_w', 'grad_conv_w_out': 'grad_w', 'grad_kv_w': 'grad_w', 'grad_f_bias': 'grad_w', 'grad_attn_w_q': 'grad_w', 'grad_attn_w_o': 'grad_w', 'delta_meta': 'delta_w', 'delta_ffn1_wg': 'delta_w', 'delta_ffn1_wu': 'delta_w', 'delta_ffn1_wd': 'delta_w', 'delta_ffn2_wg': 'delta_w', 'delta_ffn2_wu': 'delta_w', 'delta_ffn2_wd': 'delta_w', 'delta_ln_gain': 'delta_w', 'delta_ln_bias': 'delta_w', 'delta_conv_w_in': 'delta_w', 'delta_conv_w': 'delta_w', 'delta_conv_w_out': 'delta_w', 'delta_kv_w': 'delta_w', 'delta_f_bias': 'delta_w', 'delta_attn_w_q': 'delta_w', 'delta_attn_w_o': 'delta_w', 'new_m_meta': 'new_m', 'new_m_ffn1_wg': 'new_m', 'new_m_ffn1_wu': 'new_m', 'new_m_ffn1_wd': 'new_m', 'new_m_ffn2_wg': 'new_m', 'new_m_ffn2_wu': 'new_m', 'new_m_ffn2_wd': 'new_m', 'new_m_ln_gain': 'new_m', 'new_m_ln_bias': 'new_m', 'new_m_conv_w_in': 'new_m', 'new_m_conv_w': 'new_m', 'new_m_conv_w_out': 'new_m', 'new_m_kv_w': 'new_m', 'new_m_f_bias': 'new_m', 'new_m_attn_w_q': 'new_m', 'new_m_attn_w_o': 'new_m', 'new_v_meta': 'new_v', 'new_v_ffn1_wg': 'new_v', 'new_v_ffn1_wu': 'new_v', 'new_v_ffn1_wd': 'new_v', 'new_v_ffn2_wg': 'new_v', 'new_v_ffn2_wu': 'new_v', 'new_v_ffn2_wd': 'new_v', 'new_v_ln_gain': 'new_v', 'new_v_ln_bias': 'new_v', 'new_v_conv_w_in': 'new_v', 'new_v_conv_w': 'new_v', 'new_v_conv_w_out': 'new_v', 'new_v_kv_w': 'new_v', 'new_v_f_bias': 'new_v', 'new_v_attn_w_q': 'new_v', 'new_v_attn_w_o': 'new_v'}


def _forward(args):
    return _fwd_reference(*[args[k] for k in FWD_PARAMS])


def _output_shape():
    def fwd():
        inp = _fwd_setup_inputs(0)
        return _fwd_reference(*[inp[k] for k in FWD_PARAMS])
    out = _jax.eval_shape(fwd)
    return out.shape, out.dtype

N_MICROBATCH = 1
ADAM_LR = 0.001
ADAM_B1 = 0.9
ADAM_B2 = 0.999
ADAM_EPS = 1e-08
ADAM_WD = 0.01
ADAM_STEP = 10
PER_EXAMPLE_BATCH_AXIS = {'x': 0, 'loss_target': 0}
SHARED_INPUTS = []
_WEIGHT_DTYPES = {'meta': _jnp.float32, 'ffn1_wg': _jnp.float32, 'ffn1_wu': _jnp.float32, 'ffn1_wd': _jnp.float32, 'ffn2_wg': _jnp.float32, 'ffn2_wu': _jnp.float32, 'ffn2_wd': _jnp.float32, 'ln_gain': _jnp.float32, 'ln_bias': _jnp.float32, 'conv_w_in': _jnp.float32, 'conv_w': _jnp.float32, 'conv_w_out': _jnp.float32, 'kv_w': _jnp.float32, 'f_bias': _jnp.float32, 'attn_w_q': _jnp.float32, 'attn_w_o': _jnp.float32}
MOMENT_SCALE = {'meta': 1.931412e-03, 'ffn1_wg': 1.743039e-02, 'ffn1_wu': 1.687885e-02, 'ffn1_wd': 5.600281e-02, 'ffn2_wg': 1.646940e-02, 'ffn2_wu': 1.596226e-02, 'ffn2_wd': 5.306398e-02, 'ln_gain': 2.625985e+01, 'ln_bias': 1.416781e+00, 'conv_w_in': 8.396252e-02, 'conv_w': 8.496935e-02, 'conv_w_out': 1.676336e-01, 'kv_w': 1.967252e-02, 'f_bias': 1.744556e-01, 'attn_w_q': 1.084443e-02, 'attn_w_o': 2.562877e-02}


def _to_microbatches(a, axis):
    t = _jnp.moveaxis(a, axis, 0)
    t = t.reshape((N_MICROBATCH, t.shape[0] // N_MICROBATCH) + t.shape[1:])
    return _jnp.moveaxis(t, 1, axis + 1)


def setup_inputs(seed: int = 0) -> dict:
    inp = _fwd_setup_inputs(seed)
    key = _jax.random.fold_in(_jax.random.key(seed), 7919)
    shape, _ = _output_shape()
    out = dict(inp)
    out["loss_target"] = _jax.random.normal(_jax.random.fold_in(key, 0), shape, _jnp.float32)
    for i, name in enumerate(TWIN_WEIGHTS):
        w = inp[name].astype(_jnp.float32)
        if MOMENT_SCALE is None:
            s = _jnp.sqrt(_jnp.mean(_jnp.square(w)) + 1e-30)
        else:
            s = MOMENT_SCALE[name]
        km, kv = _jax.random.split(_jax.random.fold_in(key, i + 1))
        out[name] = w
        out["m_" + name] = s * _jax.random.normal(km, w.shape, _jnp.float32)
        out["v_" + name] = (s * s) * _jax.random.uniform(kv, w.shape, _jnp.float32, 0.5, 1.5)
    if N_MICROBATCH > 1:
        for name, axis in PER_EXAMPLE_BATCH_AXIS.items():
            out[name] = _to_microbatches(out[name], axis)
    return {'x': out['x'], 'meta': out['meta'], 'ffn1_wg': out['ffn1_wg'], 'ffn1_wu': out['ffn1_wu'], 'ffn1_wd': out['ffn1_wd'], 'ffn2_wg': out['ffn2_wg'], 'ffn2_wu': out['ffn2_wu'], 'ffn2_wd': out['ffn2_wd'], 'ln_gain': out['ln_gain'], 'ln_bias': out['ln_bias'], 'conv_w_in': out['conv_w_in'], 'conv_w': out['conv_w'], 'conv_w_out': out['conv_w_out'], 'kv_w': out['kv_w'], 'f_bias': out['f_bias'], 'attn_w_q': out['attn_w_q'], 'attn_w_o': out['attn_w_o'], 'loss_target': out['loss_target'], 'm_meta': out['m_meta'], 'm_ffn1_wg': out['m_ffn1_wg'], 'm_ffn1_wu': out['m_ffn1_wu'], 'm_ffn1_wd': out['m_ffn1_wd'], 'm_ffn2_wg': out['m_ffn2_wg'], 'm_ffn2_wu': out['m_ffn2_wu'], 'm_ffn2_wd': out['m_ffn2_wd'], 'm_ln_gain': out['m_ln_gain'], 'm_ln_bias': out['m_ln_bias'], 'm_conv_w_in': out['m_conv_w_in'], 'm_conv_w': out['m_conv_w'], 'm_conv_w_out': out['m_conv_w_out'], 'm_kv_w': out['m_kv_w'], 'm_f_bias': out['m_f_bias'], 'm_attn_w_q': out['m_attn_w_q'], 'm_attn_w_o': out['m_attn_w_o'], 'v_meta': out['v_meta'], 'v_ffn1_wg': out['v_ffn1_wg'], 'v_ffn1_wu': out['v_ffn1_wu'], 'v_ffn1_wd': out['v_ffn1_wd'], 'v_ffn2_wg': out['v_ffn2_wg'], 'v_ffn2_wu': out['v_ffn2_wu'], 'v_ffn2_wd': out['v_ffn2_wd'], 'v_ln_gain': out['v_ln_gain'], 'v_ln_bias': out['v_ln_bias'], 'v_conv_w_in': out['v_conv_w_in'], 'v_conv_w': out['v_conv_w'], 'v_conv_w_out': out['v_conv_w_out'], 'v_kv_w': out['v_kv_w'], 'v_f_bias': out['v_f_bias'], 'v_attn_w_q': out['v_attn_w_q'], 'v_attn_w_o': out['v_attn_w_o']}


def _loss(weights, diff, rest, loss_target):
    with _jax.named_scope("forward"):
        args = {**rest, TWIN_DIFF_INPUT: diff, **{k: w.astype(_WEIGHT_DTYPES[k]) for k, w in weights.items()}}
        y = _forward(args)
    with _jax.named_scope("loss_head"):
        err = _jnp.square(y.astype(_jnp.float32) - loss_target)
        return 0.5 * _jnp.sum(_jnp.mean(err, axis=-1)) if err.ndim else 0.5 * err


def _adamw(w, g, m, v):
    m = ADAM_B1 * m + (1.0 - ADAM_B1) * g
    v = ADAM_B2 * v + (1.0 - ADAM_B2) * _jnp.square(g)
    m_hat = m / (1.0 - ADAM_B1 ** ADAM_STEP)
    v_hat = v / (1.0 - ADAM_B2 ** ADAM_STEP)
    delta = -ADAM_LR * (m_hat / (_jnp.sqrt(v_hat) + ADAM_EPS) + ADAM_WD * w)
    return delta, m, v


def reference(x, meta, ffn1_wg, ffn1_wu, ffn1_wd, ffn2_wg, ffn2_wu, ffn2_wd, ln_gain, ln_bias, conv_w_in, conv_w, conv_w_out, kv_w, f_bias, attn_w_q, attn_w_o, loss_target, m_meta, m_ffn1_wg, m_ffn1_wu, m_ffn1_wd, m_ffn2_wg, m_ffn2_wu, m_ffn2_wd, m_ln_gain, m_ln_bias, m_conv_w_in, m_conv_w, m_conv_w_out, m_kv_w, m_f_bias, m_attn_w_q, m_attn_w_o, v_meta, v_ffn1_wg, v_ffn1_wu, v_ffn1_wd, v_ffn2_wg, v_ffn2_wu, v_ffn2_wd, v_ln_gain, v_ln_bias, v_conv_w_in, v_conv_w, v_conv_w_out, v_kv_w, v_f_bias, v_attn_w_q, v_attn_w_o):
    given = dict(x=x, meta=meta, ffn1_wg=ffn1_wg, ffn1_wu=ffn1_wu, ffn1_wd=ffn1_wd, ffn2_wg=ffn2_wg, ffn2_wu=ffn2_wu, ffn2_wd=ffn2_wd, ln_gain=ln_gain, ln_bias=ln_bias, conv_w_in=conv_w_in, conv_w=conv_w, conv_w_out=conv_w_out, kv_w=kv_w, f_bias=f_bias, attn_w_q=attn_w_q, attn_w_o=attn_w_o, loss_target=loss_target, m_meta=m_meta, m_ffn1_wg=m_ffn1_wg, m_ffn1_wu=m_ffn1_wu, m_ffn1_wd=m_ffn1_wd, m_ffn2_wg=m_ffn2_wg, m_ffn2_wu=m_ffn2_wu, m_ffn2_wd=m_ffn2_wd, m_ln_gain=m_ln_gain, m_ln_bias=m_ln_bias, m_conv_w_in=m_conv_w_in, m_conv_w=m_conv_w, m_conv_w_out=m_conv_w_out, m_kv_w=m_kv_w, m_f_bias=m_f_bias, m_attn_w_q=m_attn_w_q, m_attn_w_o=m_attn_w_o, v_meta=v_meta, v_ffn1_wg=v_ffn1_wg, v_ffn1_wu=v_ffn1_wu, v_ffn1_wd=v_ffn1_wd, v_ffn2_wg=v_ffn2_wg, v_ffn2_wu=v_ffn2_wu, v_ffn2_wd=v_ffn2_wd, v_ln_gain=v_ln_gain, v_ln_bias=v_ln_bias, v_conv_w_in=v_conv_w_in, v_conv_w=v_conv_w, v_conv_w_out=v_conv_w_out, v_kv_w=v_kv_w, v_f_bias=v_f_bias, v_attn_w_q=v_attn_w_q, v_attn_w_o=v_attn_w_o)
    weights = {n: given[n] for n in TWIN_WEIGHTS}
    shared = {n: given[n] for n in SHARED_INPUTS}
    per_example = {n: given[n] for n in ['x']}
    grad_fn = _jax.value_and_grad(_loss, argnums=(0, 1))

    def one_microbatch(ex, loss_target):
        ex = dict(ex)
        diff = ex.pop(TWIN_DIFF_INPUT)
        return grad_fn(weights, diff, {**shared, **ex}, loss_target)

    if N_MICROBATCH == 1:
        loss, (grad_w, grad_x) = one_microbatch(per_example, given["loss_target"])
    else:
        def body(carry, xs):
            loss_sum, grad_sum = carry
            l_k, (gw_k, gx_k) = one_microbatch(xs[0], xs[1])
            with _jax.named_scope("update"):
                return (loss_sum + l_k, _jax.tree.map(_jnp.add, grad_sum, gw_k)), gx_k

        init = (_jnp.zeros((), _jnp.float32), _jax.tree.map(_jnp.zeros_like, weights))
        (loss, grad_w), grad_x = _jax.lax.scan(body, init, (per_example, given["loss_target"]))
    with _jax.named_scope("update"):
        delta_w, new_m, new_v = {}, {}, {}
        for n in TWIN_WEIGHTS:
            delta_w[n], new_m[n], new_v[n] = _adamw(weights[n], grad_w[n], given["m_" + n], given["v_" + n])
    return (loss, grad_x, *[grad_w[n] for n in TWIN_WEIGHTS], *[delta_w[n] for n in TWIN_WEIGHTS],
            *[new_m[n] for n in TWIN_WEIGHTS], *[new_v[n] for n in TWIN_WEIGHTS])
```

```python
import functools
import math

import jax
import jax.numpy as jnp
from jax import lax
from jax.experimental import pallas as pl
from jax.experimental.pallas import tpu as pltpu

F32 = jnp.float32
BF16 = jnp.bfloat16

N_DEV = 8
N_HEADS = 8
N_META = 16
PAD = 112
ROW0 = PAD + N_META
LN_EPS = 1e-5
NEG_INF = -1e30
LANES = 128
FFN_CHUNK = 256

ADAM_LR = 0.001
ADAM_B1 = 0.9
ADAM_B2 = 0.999
ADAM_EPS = 1e-08
ADAM_WD = 0.01
ADAM_STEP = 10

ROW_TILES = (640, 128)
LOSS_TILE = 128
VMEM_BIG = 56 << 20
VMEM_MID = 40 << 20

ANY = pl.BlockSpec(memory_space=pl.ANY)
MESH = pl.DeviceIdType.MESH


def _row_tile(t):
    for c in ROW_TILES:
        if t % c == 0:
            return c
    raise ValueError(f"no row tile for {t}")


def _dot(a, b):
    return jnp.dot(a, b, preferred_element_type=F32)


def _dot_nt(a, b):
    return lax.dot_general(a, b, (((1,), (1,)), ((), ())), preferred_element_type=F32)


def _dot_tn(a, b):
    return lax.dot_general(a, b, (((0,), (0,)), ((), ())), preferred_element_type=F32)


def _params(sem, vmem):
    return pltpu.CompilerParams(dimension_semantics=sem, vmem_limit_bytes=vmem)


def _ln_fwd(z):
    mu = jnp.mean(z, axis=-1, keepdims=True)
    zc = z - mu
    var = jnp.mean(zc * zc, axis=-1, keepdims=True)
    rstd = lax.rsqrt(var + LN_EPS)
    return zc * rstd, rstd


def _ln_bwd(dh, xhat, rstd, gain):
    dxh = dh * gain
    m1 = jnp.mean(dxh, axis=-1, keepdims=True)
    m2 = jnp.mean(dxh * xhat, axis=-1, keepdims=True)
    dz = rstd * (dxh - m1 - xhat * m2)
    return dz, jnp.sum(dh * xhat, axis=0, keepdims=True), jnp.sum(dh, axis=0, keepdims=True)


def _load_resident(pairs, sems):
    cps = [pltpu.make_async_copy(src, dst, sems.at[k]) for k, (src, dst) in enumerate(pairs)]
    for cp in cps:
        cp.start()
    for cp in cps:
        cp.wait()


def _ffn_fwd(xh, gi, bi, wg, wu, wd, go, bo, alpha, name):
    t, d = xh.shape
    nc, _, fc = wg.shape
    f = nc * fc
    tm = _row_tile(t)
    nt = t // tm

    def body(xh_ref, gi_ref, bi_ref, wg_hbm, wu_hbm, wd_hbm, go_ref, bo_ref,
             xo_ref, rs_ref, hb_ref, g_ref, u_ref,
             wg_v, wu_v, wd_v, acc, hbs, sems):
        i = pl.program_id(0)
        c = pl.program_id(1)

        @pl.when((i == 0) & (c == 0))
        def _():
            _load_resident([(wg_hbm, wg_v), (wu_hbm, wu_v), (wd_hbm, wd_v)], sems)

        @pl.when(c == 0)
        def _():
            h = xh_ref[...] * gi_ref[...] + bi_ref[...]
            hbs[...] = h.astype(BF16)
            acc[...] = jnp.zeros_like(acc)

        hb = hbs[...]
        g = _dot(hb, wg_v[c])
        u = _dot(hb, wu_v[c])
        a = (g * jax.nn.sigmoid(g)) * u
        g_ref[...] = g.astype(BF16)
        u_ref[...] = u.astype(BF16)
        acc[...] += _dot(a.astype(BF16), wd_v[c])

        @pl.when(c == nc - 1)
        def _():
            h = xh_ref[...] * gi_ref[...] + bi_ref[...]
            xhat, rstd = _ln_fwd(alpha * h + 0.5 * acc[...])
            xo_ref[...] = xhat
            rs_ref[...] = rstd
            hb_ref[...] = (xhat * go_ref[...] + bo_ref[...]).astype(BF16)

    row = pl.BlockSpec((tm, d), lambda i, c: (i, 0))
    vec = pl.BlockSpec((1, d), lambda i, c: (0, 0))
    chunk = pl.BlockSpec((tm, fc), lambda i, c: (i, c))
    return pl.pallas_call(
        body, name=name, grid=(nt, nc),
        in_specs=[row, vec, vec, ANY, ANY, ANY, vec, vec],
        out_specs=[row, pl.BlockSpec((tm, 1), lambda i, c: (i, 0)), row, chunk, chunk],
        out_shape=[jax.ShapeDtypeStruct((t, d), F32), jax.ShapeDtypeStruct((t, 1), F32),
                   jax.ShapeDtypeStruct((t, d), BF16), jax.ShapeDtypeStruct((t, f), BF16),
                   jax.ShapeDtypeStruct((t, f), BF16)],
        scratch_shapes=[pltpu.VMEM((nc, d, fc), BF16), pltpu.VMEM((nc, d, fc), BF16),
                        pltpu.VMEM((nc, fc, d), BF16), pltpu.VMEM((tm, d), F32),
                        pltpu.VMEM((tm, d), BF16), pltpu.SemaphoreType.DMA((3,))],
        compiler_params=_params(("arbitrary", "arbitrary"), VMEM_BIG),
    )(xh, gi, bi, wg, wu, wd, go, bo)


def _ffn_bwd(dh, xo, rs, go, gs, us, wg, wu, wd, alpha, name):
    t, d = dh.shape
    nc, _, fc = wg.shape
    f = nc * fc
    tm = _row_tile(t)
    nt = t // tm

    def body(dh_ref, xo_ref, rs_ref, go_ref, g_ref, u_ref, wg_hbm, wu_hbm, wd_hbm,
             dhin_ref, do_ref, dg_ref, du_ref, a_ref, dgain_ref, dbias_ref,
             wg_v, wu_v, wd_v, acc, sems):
        i = pl.program_id(0)
        c = pl.program_id(1)

        @pl.when((i == 0) & (c == 0))
        def _():
            _load_resident([(wg_hbm, wg_v), (wu_hbm, wu_v), (wd_hbm, wd_v)], sems)
            dgain_ref[...] = jnp.zeros_like(dgain_ref)
            dbias_ref[...] = jnp.zeros_like(dbias_ref)

        @pl.when(c == 0)
        def _():
            dz, dgp, dbp = _ln_bwd(dh_ref[...], xo_ref[...], rs_ref[...], go_ref[...])
            dgain_ref[...] += dgp
            dbias_ref[...] += dbp
            do_ref[...] = (0.5 * dz).astype(BF16)
            acc[...] = alpha * dz

        do = do_ref[...]
        g = g_ref[...].astype(F32)
        u = u_ref[...].astype(F32)
        sg = jax.nn.sigmoid(g)
        sl = g * sg
        da = _dot_nt(do, wd_v[c])
        dgb = (da * u * (sg * (1.0 + g * (1.0 - sg)))).astype(BF16)
        dub = (da * sl).astype(BF16)
        a_ref[...] = (sl * u).astype(BF16)
        dg_ref[...] = dgb
        du_ref[...] = dub
        acc[...] += _dot_nt(dgb, wg_v[c]) + _dot_nt(dub, wu_v[c])

        @pl.when(c == nc - 1)
        def _():
            dhin_ref[...] = acc[...]

    row = pl.BlockSpec((tm, d), lambda i, c: (i, 0))
    vec = pl.BlockSpec((1, d), lambda i, c: (0, 0))
    chunk = pl.BlockSpec((tm, fc), lambda i, c: (i, c))
    return pl.pallas_call(
        body, name=name, grid=(nt, nc),
        in_specs=[row, row, pl.BlockSpec((tm, 1), lambda i, c: (i, 0)), vec, chunk, chunk,
                  ANY, ANY, ANY],
        out_specs=[row, row, chunk, chunk, chunk, vec, vec],
        out_shape=[jax.ShapeDtypeStruct((t, d), F32), jax.ShapeDtypeStruct((t, d), BF16),
                   jax.ShapeDtypeStruct((t, f), BF16), jax.ShapeDtypeStruct((t, f), BF16),
                   jax.ShapeDtypeStruct((t, f), BF16), jax.ShapeDtypeStruct((1, d), F32),
                   jax.ShapeDtypeStruct((1, d), F32)],
        scratch_shapes=[pltpu.VMEM((nc, d, fc), BF16), pltpu.VMEM((nc, d, fc), BF16),
                        pltpu.VMEM((nc, fc, d), BF16), pltpu.VMEM((tm, d), F32),
                        pltpu.SemaphoreType.DMA((3,))],
        compiler_params=_params(("arbitrary", "arbitrary"), VMEM_BIG),
    )(dh, xo, rs, go, gs, us, wg, wu, wd)


def _wgrad(x, ys, name):
    t, m = x.shape
    n = ys[0].shape[1]
    tn = min(n, FFN_CHUNK)
    tk = _row_tile(t)
    nk = t // tk
    ny = len(ys)

    def body(*refs):
        x_hbm = refs[0]
        y_refs = refs[1:1 + ny]
        o_refs = refs[1 + ny:1 + 2 * ny]
        xv, sems = refs[1 + 2 * ny:]

        @pl.when(pl.program_id(0) == 0)
        def _():
            _load_resident([(x_hbm, xv)], sems)

        for y_ref, o_ref in zip(y_refs, o_refs):
            def step(k, acc, y_ref=y_ref):
                rows = pl.ds(pl.multiple_of(k * tk, tk), tk)
                return acc + _dot_tn(xv[rows, :], y_ref[rows, :].astype(BF16))
            o_ref[...] = lax.fori_loop(0, nk, step, jnp.zeros((m, tn), F32))

    return pl.pallas_call(
        body, name=name, grid=(n // tn,),
        in_specs=[ANY] + [pl.BlockSpec((t, tn), lambda c: (0, c)) for _ in ys],
        out_specs=[pl.BlockSpec((m, tn), lambda c: (0, c)) for _ in ys],
        out_shape=[jax.ShapeDtypeStruct((m, n), F32) for _ in ys],
        scratch_shapes=[pltpu.VMEM((t, m), BF16), pltpu.SemaphoreType.DMA((1,))],
        compiler_params=_params(("arbitrary",), VMEM_BIG),
    )(x, *ys)


def _wgrad_xt(xt, y, name):
    m, t = xt.shape
    n = y.shape[1]
    tk = _row_tile(t)

    def body(xt_ref, y_ref, o_ref):
        @pl.when(pl.program_id(0) == 0)
        def _():
            o_ref[...] = jnp.zeros_like(o_ref)
        o_ref[...] += _dot(xt_ref[...], y_ref[...])

    return pl.pallas_call(
        body, name=name, grid=(t // tk,),
        in_specs=[pl.BlockSpec((m, tk), lambda k: (0, k)), pl.BlockSpec((tk, n), lambda k: (k, 0))],
        out_specs=pl.BlockSpec((m, n), lambda k: (0, 0)),
        out_shape=jax.ShapeDtypeStruct((m, n), F32),
        compiler_params=_params(("arbitrary",), VMEM_MID),
    )(xt, y)


def _shift_rows(u, halo, tm):
    r = lax.broadcasted_iota(jnp.int32, (tm, 1), 0)
    u1 = jnp.where(r == 0, halo[7:8], pltpu.roll(u, 1, 0))
    u2 = jnp.where(r == 0, halo[6:7], jnp.where(r == 1, halo[7:8], pltpu.roll(u, 2, 0)))
    return u1, u2


def _conv_fwd(xh, gi, bi, w_in, cw, w_out, go, bo, alpha, name):
    t, d = xh.shape
    tm = _row_tile(t)
    nt = t // tm

    def body(xh_ref, gi_ref, bi_ref, win_ref, cw_ref, wout_ref, go_ref, bo_ref,
             xo_ref, rs_ref, hb_ref, p_ref, m_ref, halo):
        i = pl.program_id(0)

        @pl.when(i == 0)
        def _():
            halo[...] = jnp.zeros_like(halo)

        h = xh_ref[...] * gi_ref[...] + bi_ref[...]
        hb = h.astype(BF16)
        bg = _dot(hb, win_ref[:, 0:d])
        cg = _dot(hb, win_ref[:, d:2 * d])
        val = _dot(hb, win_ref[:, 2 * d:3 * d])
        p_ref[:, 0:d] = bg.astype(BF16)
        p_ref[:, d:2 * d] = cg.astype(BF16)
        p_ref[:, 2 * d:3 * d] = val.astype(BF16)
        rows = i * tm + lax.broadcasted_iota(jnp.int32, (tm, 1), 0)
        u = jnp.where(rows >= PAD, cg * val, 0.0)
        u1, u2 = _shift_rows(u, halo[...], tm)
        halo[...] = u[tm - 8:tm]
        y = cw_ref[0:1] * u2 + cw_ref[1:2] * u1 + cw_ref[2:3] * u
        mb = (bg * y).astype(BF16)
        m_ref[...] = mb
        xhat, rstd = _ln_fwd(alpha * h + _dot(mb, wout_ref[...]))
        xo_ref[...] = xhat
        rs_ref[...] = rstd
        hb_ref[...] = (xhat * go_ref[...] + bo_ref[...]).astype(BF16)

    row = pl.BlockSpec((tm, d), lambda i: (i, 0))
    vec = pl.BlockSpec((1, d), lambda i: (0, 0))
    return pl.pallas_call(
        body, name=name, grid=(nt,),
        in_specs=[row, vec, vec, pl.BlockSpec((d, 3 * d), lambda i: (0, 0)),
                  pl.BlockSpec((3, d), lambda i: (0, 0)), pl.BlockSpec((d, d), lambda i: (0, 0)),
                  vec, vec],
        out_specs=[row, pl.BlockSpec((tm, 1), lambda i: (i, 0)), row,
                   pl.BlockSpec((tm, 3 * d), lambda i: (i, 0)), row],
        out_shape=[jax.ShapeDtypeStruct((t, d), F32), jax.ShapeDtypeStruct((t, 1), F32),
                   jax.ShapeDtypeStruct((t, d), BF16), jax.ShapeDtypeStruct((t, 3 * d), BF16),
                   jax.ShapeDtypeStruct((t, d), BF16)],
        scratch_shapes=[pltpu.VMEM((8, d), F32)],
        compiler_params=_params(("arbitrary",), VMEM_BIG),
    )(xh, gi, bi, w_in, cw, w_out, go, bo)


def _conv_bwd(dh, xo, rs, go, p, cw, w_in, w_out, alpha, name):
    t, d = dh.shape
    tm = _row_tile(t)
    nt = t // tm
    tb = tm // 8

    def body(dh_ref, xo_ref, rs_ref, go_ref, p_ref, ph_ref, cw_ref, win_ref, wout_ref,
             dhin_ref, dmix_ref, dp_ref, dcw_ref, dgain_ref, dbias_ref, carry):
        i = pl.program_id(0)
        tile = nt - 1 - i

        @pl.when(i == 0)
        def _():
            carry[...] = jnp.zeros_like(carry)
            dcw_ref[...] = jnp.zeros_like(dcw_ref)
            dgain_ref[...] = jnp.zeros_like(dgain_ref)
            dbias_ref[...] = jnp.zeros_like(dbias_ref)

        dz, dgp, dbp = _ln_bwd(dh_ref[...], xo_ref[...], rs_ref[...], go_ref[...])
        dgain_ref[...] += dgp
        dbias_ref[...] += dbp
        dmixb = dz.astype(BF16)
        dmix_ref[...] = dmixb
        dm = _dot_nt(dmixb, wout_ref[...])

        bg = p_ref[:, 0:d].astype(F32)
        cg = p_ref[:, d:2 * d].astype(F32)
        val = p_ref[:, 2 * d:3 * d].astype(F32)
        rows = tile * tm + lax.broadcasted_iota(jnp.int32, (tm, 1), 0)
        valid = rows >= PAD
        u = jnp.where(valid, cg * val, 0.0)
        hrows = tile * tm - 8 + lax.broadcasted_iota(jnp.int32, (8, 1), 0)
        hu = jnp.where((hrows >= PAD) & (tile > 0),
                       ph_ref[:, d:2 * d].astype(F32) * ph_ref[:, 2 * d:3 * d].astype(F32), 0.0)
        u1, u2 = _shift_rows(u, hu, tm)
        w0, w1, w2 = cw_ref[0:1], cw_ref[1:2], cw_ref[2:3]
        y = w0 * u2 + w1 * u1 + w2 * u
        dbg = dm * y
        dy = dm * bg
        dcw_ref[0:1] += jnp.sum(dy * u2, axis=0, keepdims=True)
        dcw_ref[1:2] += jnp.sum(dy * u1, axis=0, keepdims=True)
        dcw_ref[2:3] += jnp.sum(dy * u, axis=0, keepdims=True)

        nxt = carry[...]
        r = lax.broadcasted_iota(jnp.int32, (tm, 1), 0)
        dy1 = jnp.where(r == tm - 1, nxt[0:1], pltpu.roll(dy, tm - 1, 0))
        dy2 = jnp.where(r == tm - 2, nxt[0:1],
                        jnp.where(r == tm - 1, nxt[1:2], pltpu.roll(dy, tm - 2, 0)))
        carry[...] = dy[0:8]
        du = jnp.where(valid, w2 * dy + w1 * dy1 + w0 * dy2, 0.0)
        dbgb = dbg.astype(BF16)
        dcgb = (du * val).astype(BF16)
        dvalb = (du * cg).astype(BF16)
        dp_ref[:, 0:d] = dbgb
        dp_ref[:, d:2 * d] = dcgb
        dp_ref[:, 2 * d:3 * d] = dvalb
        dhin_ref[...] = (alpha * dz + _dot_nt(dbgb, win_ref[:, 0:d])
                         + _dot_nt(dcgb, win_ref[:, d:2 * d]) + _dot_nt(dvalb, win_ref[:, 2 * d:3 * d]))

    row = pl.BlockSpec((tm, d), lambda i: (nt - 1 - i, 0))
    vec = pl.BlockSpec((1, d), lambda i: (0, 0))
    prow = pl.BlockSpec((tm, 3 * d), lambda i: (nt - 1 - i, 0))
    return pl.pallas_call(
        body, name=name, grid=(nt,),
        in_specs=[row, row, pl.BlockSpec((tm, 1), lambda i: (nt - 1 - i, 0)), vec, prow,
                  pl.BlockSpec((8, 3 * d), lambda i: (jnp.maximum((nt - 1 - i) * tb - 1, 0), 0)),
                  pl.BlockSpec((3, d), lambda i: (0, 0)),
                  pl.BlockSpec((d, 3 * d), lambda i: (0, 0)), pl.BlockSpec((d, d), lambda i: (0, 0))],
        out_specs=[row, row, prow, pl.BlockSpec((3, d), lambda i: (0, 0)), vec, vec],
        out_shape=[jax.ShapeDtypeStruct((t, d), F32), jax.ShapeDtypeStruct((t, d), BF16),
                   jax.ShapeDtypeStruct((t, 3 * d), BF16), jax.ShapeDtypeStruct((3, d), F32),
                   jax.ShapeDtypeStruct((1, d), F32), jax.ShapeDtypeStruct((1, d), F32)],
        scratch_shapes=[pltpu.VMEM((8, d), F32)],
        compiler_params=_params(("arbitrary",), VMEM_BIG),
    )(dh, xo, rs, go, p, p, cw, w_in, w_out)


def _kv_fwd(hb, wk, wv, wf, fb, name):
    t, d = hb.shape
    tm = _row_tile(t)
    nt = t // tm

    def body(hb_ref, wk_ref, wv_ref, wf_ref, fb_ref, k_ref, v_ref, lg_ref, c_ref, ct_ref, run):
        i = pl.program_id(0)

        @pl.when(i == 0)
        def _():
            run[...] = jnp.zeros_like(run)

        x = hb_ref[...]
        k_ref[...] = _dot(x, wk_ref[...]).astype(BF16)
        v_ref[...] = _dot(x, wv_ref[...]).astype(BF16)
        logit = _dot(x, wf_ref[...]) + fb_ref[...]
        lg_ref[...] = logit
        logf = jnp.minimum(logit, 0.0) - jnp.log(1.0 + jnp.exp(-jnp.abs(logit)))
        rows = i * tm + lax.broadcasted_iota(jnp.int32, (tm, 1), 0)
        logf = jnp.where(rows >= PAD, logf, 0.0)
        tri = (lax.broadcasted_iota(jnp.int32, (tm, tm), 0)
               >= lax.broadcasted_iota(jnp.int32, (tm, tm), 1)).astype(F32)
        cs = jnp.dot(tri, logf, precision=lax.Precision.HIGHEST, preferred_element_type=F32) + run[...]
        run[...] = cs[tm - 1:tm]
        c_ref[...] = cs
        ct_ref[...] = cs.T

    row = pl.BlockSpec((tm, d), lambda i: (i, 0))
    gate = pl.BlockSpec((tm, LANES), lambda i: (i, 0))
    sq = pl.BlockSpec((d, d), lambda i: (0, 0))
    return pl.pallas_call(
        body, name=name, grid=(nt,),
        in_specs=[row, sq, sq, pl.BlockSpec((d, LANES), lambda i: (0, 0)),
                  pl.BlockSpec((1, LANES), lambda i: (0, 0))],
        out_specs=[row, row, gate, gate, pl.BlockSpec((LANES, tm), lambda i: (0, i))],
        out_shape=[jax.ShapeDtypeStruct((t, d), BF16), jax.ShapeDtypeStruct((t, d), BF16),
                   jax.ShapeDtypeStruct((t, LANES), F32), jax.ShapeDtypeStruct((t, LANES), F32),
                   jax.ShapeDtypeStruct((LANES, t), F32)],
        scratch_shapes=[pltpu.VMEM((1, LANES), F32)],
        compiler_params=_params(("arbitrary",), VMEM_MID),
    )(hb, wk, wv, wf, fb)


def _kv_bwd(dk, dv, dcs, dcq, logit, dh_other, wk, wv, wf, name):
    t, d = dk.shape
    tm = _row_tile(t)
    nt = t // tm

    def body(dk_ref, dv_ref, dcs_ref, dcq_ref, lg_ref, oth_ref, wk_ref, wv_ref, wf_ref,
             dh_ref, dl_ref, dfb_ref, run):
        i = pl.program_id(0)
        tile = nt - 1 - i

        @pl.when(i == 0)
        def _():
            run[...] = jnp.zeros_like(run)
            dfb_ref[...] = jnp.zeros_like(dfb_ref)

        lane = lax.broadcasted_iota(jnp.int32, (tm, LANES), 1)
        dc = dcq_ref[...]
        for hh in range(N_HEADS):
            dc = dc + jnp.where(lane == hh, jnp.sum(dcs_ref[hh], axis=1, keepdims=True), 0.0)
        tri = (lax.broadcasted_iota(jnp.int32, (tm, tm), 0)
               <= lax.broadcasted_iota(jnp.int32, (tm, tm), 1)).astype(F32)
        dlf = jnp.dot(tri, dc, precision=lax.Precision.HIGHEST, preferred_element_type=F32) + run[...]
        run[...] = dlf[0:1]
        rows = tile * tm + lax.broadcasted_iota(jnp.int32, (tm, 1), 0)
        dlogit = jnp.where(rows >= PAD, dlf * jax.nn.sigmoid(-lg_ref[...]), 0.0)
        dfb_ref[...] += jnp.sum(dlogit, axis=0, keepdims=True)
        dlb = dlogit.astype(BF16)
        dl_ref[...] = dlb
        dh_ref[...] = (oth_ref[...] + _dot_nt(dk_ref[...], wk_ref[...])
                       + _dot_nt(dv_ref[...], wv_ref[...]) + _dot_nt(dlb, wf_ref[...]))

    row = pl.BlockSpec((tm, d), lambda i: (nt - 1 - i, 0))
    gate = pl.BlockSpec((tm, LANES), lambda i: (nt - 1 - i, 0))
    sq = pl.BlockSpec((d, d), lambda i: (0, 0))
    return pl.pallas_call(
        body, name=name, grid=(nt,),
        in_specs=[row, row, pl.BlockSpec((N_HEADS, tm, LANES), lambda i: (0, nt - 1 - i, 0)), gate, gate, row,
                  sq, sq, pl.BlockSpec((d, LANES), lambda i: (0, 0))],
        out_specs=[row, gate, pl.BlockSpec((1, LANES), lambda i: (0, 0))],
        out_shape=[jax.ShapeDtypeStruct((t, d), F32), jax.ShapeDtypeStruct((t, LANES), BF16),
                   jax.ShapeDtypeStruct((1, LANES), F32)],
        scratch_shapes=[pltpu.VMEM((1, LANES), F32)],
        compiler_params=_params(("arbitrary",), VMEM_MID),
    )(dk, dv, dcs, dcq, logit, dh_other, wk, wv, wf)


def _proj(x, w, name):
    t, k = x.shape
    n = w.shape[1]
    tm = _row_tile(t)

    def body(x_ref, w_ref, o_ref):
        o_ref[...] = _dot(x_ref[...], w_ref[...]).astype(BF16)

    return pl.pallas_call(
        body, name=name, grid=(t // tm,),
        in_specs=[pl.BlockSpec((tm, k), lambda i: (i, 0)), pl.BlockSpec((k, n), lambda i: (0, 0))],
        out_specs=pl.BlockSpec((tm, n), lambda i: (i, 0)),
        out_shape=jax.ShapeDtypeStruct((t, n), BF16),
        compiler_params=_params(("arbitrary",), VMEM_MID),
    )(x, w)


def _add_proj_nt(base, y, w, name):
    t, n = y.shape
    k = w.shape[0]
    tm = _row_tile(t)

    def body(b_ref, y_ref, w_ref, o_ref):
        o_ref[...] = b_ref[...] + _dot_nt(y_ref[...].astype(BF16), w_ref[...])

    return pl.pallas_call(
        body, name=name, grid=(t // tm,),
        in_specs=[pl.BlockSpec((tm, k), lambda i: (i, 0)), pl.BlockSpec((tm, n), lambda i: (i, 0)),
                  pl.BlockSpec((k, n), lambda i: (0, 0))],
        out_specs=pl.BlockSpec((tm, k), lambda i: (i, 0)),
        out_shape=jax.ShapeDtypeStruct((t, k), F32),
        compiler_params=_params(("arbitrary",), VMEM_MID),
    )(base, y, w)


def _attn_out_fwd(ot, xh, gi, bi, w_o, go, bo, alpha, name):
    t, d = xh.shape
    tm = _row_tile(t)

    def body(ot_ref, xh_ref, gi_ref, bi_ref, wo_ref, go_ref, bo_ref, xo_ref, rs_ref, hb_ref):
        h = xh_ref[...] * gi_ref[...] + bi_ref[...]
        xhat, rstd = _ln_fwd(alpha * h + _dot_tn(ot_ref[...], wo_ref[...]))
        xo_ref[...] = xhat
        rs_ref[...] = rstd
        hb_ref[...] = (xhat * go_ref[...] + bo_ref[...]).astype(BF16)

    row = pl.BlockSpec((tm, d), lambda i: (i, 0))
    vec = pl.BlockSpec((1, d), lambda i: (0, 0))
    return pl.pallas_call(
        body, name=name, grid=(t // tm,),
        in_specs=[pl.BlockSpec((d, tm), lambda i: (0, i)), row, vec, vec,
                  pl.BlockSpec((d, d), lambda i: (0, 0)), vec, vec],
        out_specs=[row, pl.BlockSpec((tm, 1), lambda i: (i, 0)), row],
        out_shape=[jax.ShapeDtypeStruct((t, d), F32), jax.ShapeDtypeStruct((t, 1), F32),
                   jax.ShapeDtypeStruct((t, d), BF16)],
        compiler_params=_params(("arbitrary",), VMEM_MID),
    )(ot, xh, gi, bi, w_o, go, bo)


def _attn_out_bwd(dh, xo, rs, go, ot, w_o, alpha, name):
    t, d = dh.shape
    tm = _row_tile(t)
    hd = d // N_HEADS

    def body(dh_ref, xo_ref, rs_ref, go_ref, ot_ref, wo_ref,
             dres_ref, dmix_ref, dot_ref, delta_ref, dgain_ref, dbias_ref):
        @pl.when(pl.program_id(0) == 0)
        def _():
            dgain_ref[...] = jnp.zeros_like(dgain_ref)
            dbias_ref[...] = jnp.zeros_like(dbias_ref)

        dz, dgp, dbp = _ln_bwd(dh_ref[...], xo_ref[...], rs_ref[...], go_ref[...])
        dgain_ref[...] += dgp
        dbias_ref[...] += dbp
        dres_ref[...] = alpha * dz
        dmixb = dz.astype(BF16)
        dmix_ref[...] = dmixb
        dot_t = _dot_nt(wo_ref[...], dmixb)
        dot_ref[...] = dot_t.astype(BF16)
        prod = dot_t * ot_ref[...].astype(F32)
        delta_ref[...] = jnp.sum(prod.reshape(N_HEADS, hd, tm), axis=1)

    row = pl.BlockSpec((tm, d), lambda i: (i, 0))
    vec = pl.BlockSpec((1, d), lambda i: (0, 0))
    col = pl.BlockSpec((d, tm), lambda i: (0, i))
    return pl.pallas_call(
        body, name=name, grid=(t // tm,),
        in_specs=[row, row, pl.BlockSpec((tm, 1), lambda i: (i, 0)), vec, col,
                  pl.BlockSpec((d, d), lambda i: (0, 0))],
        out_specs=[row, row, col, pl.BlockSpec((N_HEADS, tm), lambda i: (0, i)), vec, vec],
        out_shape=[jax.ShapeDtypeStruct((t, d), F32), jax.ShapeDtypeStruct((t, d), BF16),
                   jax.ShapeDtypeStruct((d, t), BF16), jax.ShapeDtypeStruct((N_HEADS, t), F32),
                   jax.ShapeDtypeStruct((1, d), F32), jax.ShapeDtypeStruct((1, d), F32)],
        compiler_params=_params(("arbitrary",), VMEM_MID),
    )(dh, xo, rs, go, ot, w_o)


def _scores_t(k, q, ct_ref, c_ref, h, i, j, tq, tk, scale):
    st = _dot_nt(k, q) * scale
    sub = lax.broadcasted_iota(jnp.int32, (8, tq), 0)
    cq = jnp.sum(jnp.where(sub == h, ct_ref[...], 0.0), axis=0, keepdims=True)
    lane = lax.broadcasted_iota(jnp.int32, (tk, LANES), 1)
    ck = jnp.sum(jnp.where(lane == h, c_ref[...], 0.0), axis=1, keepdims=True)
    kpos = j * tk + lax.broadcasted_iota(jnp.int32, (tk, 1), 0)
    qpos = i * tq + lax.broadcasted_iota(jnp.int32, (1, tq), 1)
    mask = (kpos <= qpos) & (kpos >= PAD)
    return jnp.where(mask, st + cq - ck, NEG_INF)


def _attn_fwd(q, k, v, c, ct, name):
    t, d = q.shape
    hd = d // N_HEADS
    tq = tk = _row_tile(t)
    nq = t // tq
    scale = 1.0 / math.sqrt(hd)

    def body(q_ref, k_ref, v_ref, c_ref, ct_ref, ot_ref, lse_ref, m_s, l_s, acc):
        h, i, j = pl.program_id(0), pl.program_id(1), pl.program_id(2)

        @pl.when(j == 0)
        def _():
            m_s[...] = jnp.full_like(m_s, NEG_INF)
            l_s[...] = jnp.zeros_like(l_s)
            acc[...] = jnp.zeros_like(acc)

        @pl.when(j <= i)
        def _():
            st = _scores_t(k_ref[...], q_ref[...], ct_ref, c_ref, h, i, j, tq, tk, scale)
            m_new = jnp.maximum(m_s[...], jnp.max(st, axis=0, keepdims=True))
            a = jnp.exp(m_s[...] - m_new)
            p = jnp.exp(st - m_new)
            l_s[...] = a * l_s[...] + jnp.sum(p, axis=0, keepdims=True)
            acc[...] = a * acc[...] + _dot_tn(v_ref[...], p.astype(BF16))
            m_s[...] = m_new

        @pl.when(j == i)
        def _():
            ot_ref[...] = (acc[...] / l_s[...]).astype(BF16)
            lse_ref[0] = m_s[...] + jnp.log(l_s[...])

    kv = pl.BlockSpec((tk, hd), lambda h, i, j: (jnp.minimum(j, i), h))
    return pl.pallas_call(
        body, name=name, grid=(N_HEADS, nq, nq),
        in_specs=[pl.BlockSpec((tq, hd), lambda h, i, j: (i, h)), kv, kv,
                  pl.BlockSpec((tk, LANES), lambda h, i, j: (jnp.minimum(j, i), 0)),
                  pl.BlockSpec((8, tq), lambda h, i, j: (0, i))],
        out_specs=[pl.BlockSpec((hd, tq), lambda h, i, j: (h, i)),
                   pl.BlockSpec((1, 1, tq), lambda h, i, j: (h, 0, i))],
        out_shape=[jax.ShapeDtypeStruct((d, t), BF16), jax.ShapeDtypeStruct((N_HEADS, 1, t), F32)],
        scratch_shapes=[pltpu.VMEM((1, tq), F32), pltpu.VMEM((1, tq), F32), pltpu.VMEM((hd, tq), F32)],
        compiler_params=_params(("arbitrary", "arbitrary", "arbitrary"), VMEM_MID),
    )(q, k, v, c, ct)


def _attn_bwd(q, k, v, c, ct, lse, delta, dot_t, name):
    t, d = q.shape
    hd = d // N_HEADS
    tq = tk = _row_tile(t)
    nq = t // tq
    scale = 1.0 / math.sqrt(hd)

    def body(q_ref, k_ref, v_ref, c_ref, ct_ref, lse_ref, delta_ref, dot_ref,
             dq_ref, dk_ref, dv_ref, dcs_ref, drow_ref, dk_acc, dv_acc, dc_acc):
        h, j, i = pl.program_id(0), pl.program_id(1), pl.program_id(2)

        @pl.when((j == 0) & (i == 0))
        def _():
            dq_ref[...] = jnp.zeros_like(dq_ref)
            drow_ref[...] = jnp.zeros_like(drow_ref)

        @pl.when(i == 0)
        def _():
            dk_acc[...] = jnp.zeros_like(dk_acc)
            dv_acc[...] = jnp.zeros_like(dv_acc)
            dc_acc[...] = jnp.zeros_like(dc_acc)

        @pl.when(i >= j)
        def _():
            qv, kv_, vv = q_ref[...], k_ref[...], v_ref[...]
            st = _scores_t(kv_, qv, ct_ref, c_ref, h, i, j, tq, tk, scale)
            p = jnp.exp(st - lse_ref[0])
            do_t = dot_ref[...]
            dp = _dot(vv, do_t)
            sub = lax.broadcasted_iota(jnp.int32, (8, tq), 0)
            dl = jnp.sum(jnp.where(sub == h, delta_ref[...], 0.0), axis=0, keepdims=True)
            ds = p * (dp - dl)
            dsb = ds.astype(BF16)
            dv_acc[...] += _dot_nt(p.astype(BF16), do_t)
            dk_acc[...] += _dot(dsb, qv) * scale
            rows = pl.ds(pl.multiple_of(i * tq, tq), tq)
            dq_ref[rows, :] += _dot_tn(dsb, kv_) * scale
            part = ds[:, 0:LANES]
            for g in range(1, tq // LANES):
                part = part + ds[:, g * LANES:(g + 1) * LANES]
            dc_acc[...] += part
            drow_ref[0, i] += jnp.broadcast_to(jnp.sum(ds, axis=0, keepdims=True), (8, tq))

        @pl.when(i == nq - 1)
        def _():
            dk_ref[...] = dk_acc[...].astype(BF16)
            dv_ref[...] = dv_acc[...].astype(BF16)
            dcs_ref[0] = -dc_acc[...]

    qi = lambda h, j, i: jnp.maximum(i, j)
    kv = pl.BlockSpec((tk, hd), lambda h, j, i: (j, h))
    return pl.pallas_call(
        body, name=name, grid=(N_HEADS, nq, nq),
        in_specs=[pl.BlockSpec((tq, hd), lambda h, j, i: (qi(h, j, i), h)), kv, kv,
                  pl.BlockSpec((tk, LANES), lambda h, j, i: (j, 0)),
                  pl.BlockSpec((8, tq), lambda h, j, i: (0, qi(h, j, i))),
                  pl.BlockSpec((1, 1, tq), lambda h, j, i: (h, 0, qi(h, j, i))),
                  pl.BlockSpec((N_HEADS, tq), lambda h, j, i: (0, qi(h, j, i))),
                  pl.BlockSpec((hd, tq), lambda h, j, i: (h, qi(h, j, i)))],
        out_specs=[pl.BlockSpec((t, hd), lambda h, j, i: (0, h)), kv, kv,
                   pl.BlockSpec((1, tk, LANES), lambda h, j, i: (h, j, 0)),
                   pl.BlockSpec((1, nq, 8, tq), lambda h, j, i: (h, 0, 0, 0))],
        out_shape=[jax.ShapeDtypeStruct((t, d), F32), jax.ShapeDtypeStruct((t, d), BF16),
                   jax.ShapeDtypeStruct((t, d), BF16), jax.ShapeDtypeStruct((N_HEADS, t, LANES), F32),
                   jax.ShapeDtypeStruct((N_HEADS, nq, 8, tq), F32)],
        scratch_shapes=[pltpu.VMEM((tk, hd), F32), pltpu.VMEM((tk, hd), F32), pltpu.VMEM((tk, LANES), F32)],
        compiler_params=_params(("arbitrary", "arbitrary", "arbitrary"), VMEM_MID),
    )(q, k, v, c, ct, lse, delta, dot_t)


def _loss_head(xh, g, b, target, name):
    t, d = xh.shape
    tm = LOSS_TILE
    nt = t // tm
    lead = ROW0 // tm

    def body(xh_ref, g_ref, b_ref, tg_ref, dh_ref, loss_ref, part):
        i = pl.program_id(0)

        @pl.when(i == 0)
        def _():
            part[...] = jnp.zeros_like(part)

        @pl.when(i < lead)
        def _():
            dh_ref[...] = jnp.zeros_like(dh_ref)

        @pl.when(i >= lead)
        def _():
            e = xh_ref[...] * g_ref[...] + b_ref[...] - tg_ref[...]
            dh_ref[...] = e * (1.0 / d)
            part[...] += jnp.sum(e * e, axis=0, keepdims=True)

        @pl.when(i == nt - 1)
        def _():
            loss_ref[...] = jnp.full((1, LANES), 0.5 / d, F32) * jnp.sum(part[...])

    return pl.pallas_call(
        body, name=name, grid=(nt,),
        in_specs=[pl.BlockSpec((tm, d), lambda i: (i, 0)), pl.BlockSpec((1, d), lambda i: (0, 0)),
                  pl.BlockSpec((1, d), lambda i: (0, 0)),
                  pl.BlockSpec((tm, d), lambda i: (jnp.maximum(i - lead, 0), 0))],
        out_specs=[pl.BlockSpec((tm, d), lambda i: (i, 0)), pl.BlockSpec((1, LANES), lambda i: (0, 0))],
        out_shape=[jax.ShapeDtypeStruct((t, d), F32), jax.ShapeDtypeStruct((1, LANES), F32)],
        scratch_shapes=[pltpu.VMEM((1, d), F32)],
        compiler_params=_params(("arbitrary",), VMEM_MID),
    )(xh, g, b, target)


def _adamw(w, g, m, v, name):
    r, c = w.shape
    tr = r
    for cand in (256, 128, 64, 32, 16, 8):
        if r % cand == 0 and r > cand:
            tr = cand
            break
    bc1 = 1.0 - ADAM_B1 ** ADAM_STEP
    bc2 = 1.0 - ADAM_B2 ** ADAM_STEP

    def body(w_ref, g_ref, m_ref, v_ref, d_ref, nm_ref, nv_ref):
        gg = g_ref[...]
        nm = ADAM_B1 * m_ref[...] + (1.0 - ADAM_B1) * gg
        nv = ADAM_B2 * v_ref[...] + (1.0 - ADAM_B2) * (gg * gg)
        d_ref[...] = -ADAM_LR * ((nm / bc1) / (jnp.sqrt(nv / bc2) + ADAM_EPS) + ADAM_WD * w_ref[...])
        nm_ref[...] = nm
        nv_ref[...] = nv

    blk = pl.BlockSpec((tr, c), lambda i: (i, 0))
    shp = jax.ShapeDtypeStruct((r, c), F32)
    return pl.pallas_call(
        body, name=name, grid=(r // tr,), in_specs=[blk] * 4, out_specs=[blk] * 3,
        out_shape=[shp] * 3, compiler_params=_params(("arbitrary",), VMEM_MID),
    )(w, g, m, v)


def _sum_sources(r, name):
    n, rows, c = r.shape
    tr = rows
    for cand in (752, 512, 256, 128, 64, 32, 16):
        if rows % cand == 0 and rows > cand:
            tr = cand
            break

    def body(r_ref, o_ref):
        acc = r_ref[0].astype(F32)
        for s in range(1, n):
            acc = acc + r_ref[s].astype(F32)
        o_ref[...] = acc

    return pl.pallas_call(
        body, name=name, grid=(rows // tr,),
        in_specs=[pl.BlockSpec((n, tr, c), lambda i: (0, i, 0))],
        out_specs=pl.BlockSpec((tr, c), lambda i: (i, 0)),
        out_shape=jax.ShapeDtypeStruct((rows, c), F32),
        compiler_params=_params(("arbitrary",), VMEM_MID),
    )(r)


def _all_gather(x, name):
    rows, cols = x.shape

    def body(x_ref, out_ref, send_sems, recv_sems, local_sem):
        mx, my, mc = lax.axis_index("x"), lax.axis_index("y"), lax.axis_index("c")
        me, sibling = (mx, my, mc), (mx, my, 1 - mc)
        chips = [(1 - mx, my), (mx, 1 - my), (1 - mx, 1 - my)]

        def slot(px, py, pc):
            return out_ref.at[4 * px + 2 * py + pc]

        def copy(k, block, to, src=None):
            return pltpu.make_async_remote_copy(
                src_ref=slot(*block) if src is None else src, dst_ref=slot(*block),
                send_sem=send_sems.at[k], recv_sem=recv_sems.at[k],
                device_id=to, device_id_type=MESH)

        mine = pltpu.make_async_copy(x_ref, slot(*me), local_sem)
        mine.start()
        first = [copy(0, me, sibling, src=x_ref)]
        first += [copy(1 + n, me, (*chip, mc), src=x_ref) for n, chip in enumerate(chips)]
        for cp in first:
            cp.start()
        passed = [copy(4 + n, (*chip, mc), sibling) for n, chip in enumerate(chips)]
        for n, chip in enumerate(chips):
            copy(1 + n, (*chip, mc), me).wait_recv()
            passed[n].start()
        copy(0, sibling, me).wait_recv()
        for n, chip in enumerate(chips):
            copy(4 + n, (*chip, 1 - mc), me).wait_recv()
        for cp in first + passed:
            cp.wait_send()
        mine.wait()

    return pl.pallas_call(
        body, name=name, in_specs=[ANY], out_specs=ANY,
        out_shape=jax.ShapeDtypeStruct((N_DEV, rows, cols), x.dtype),
        scratch_shapes=[pltpu.SemaphoreType.DMA((7,)), pltpu.SemaphoreType.DMA((7,)),
                        pltpu.SemaphoreType.DMA],
    )(x)


def _exchange(s, name):
    _, rows, cols = s.shape

    def body(s_ref, r_ref, send_sems, recv_sems, local_sem):
        mx, my, mc = lax.axis_index("x"), lax.axis_index("y"), lax.axis_index("c")
        me_id = 4 * mx + 2 * my + mc
        mine = pltpu.make_async_copy(s_ref.at[me_id], r_ref.at[me_id], local_sem)
        mine.start()
        sends, recvs = [], []
        for kk in range(1, N_DEV):
            px = 1 - mx if (kk >> 2) & 1 else mx
            py = 1 - my if (kk >> 1) & 1 else my
            pc = 1 - mc if kk & 1 else mc
            pid = 4 * px + 2 * py + pc
            sends.append(pltpu.make_async_remote_copy(
                src_ref=s_ref.at[pid], dst_ref=r_ref.at[me_id],
                send_sem=send_sems.at[kk - 1], recv_sem=recv_sems.at[kk - 1],
                device_id=(px, py, pc), device_id_type=MESH))
            recvs.append(pltpu.make_async_remote_copy(
                src_ref=s_ref.at[pid], dst_ref=r_ref.at[pid],
                send_sem=send_sems.at[kk - 1], recv_sem=recv_sems.at[kk - 1],
                device_id=(px, py, pc), device_id_type=MESH))
        for cp in sends:
            cp.start()
        for cp in recvs:
            cp.wait_recv()
        for cp in sends:
            cp.wait_send()
        mine.wait()

    return pl.pallas_call(
        body, name=name, in_specs=[ANY], out_specs=ANY,
        out_shape=jax.ShapeDtypeStruct(s.shape, s.dtype),
        scratch_shapes=[pltpu.SemaphoreType.DMA((7,)), pltpu.SemaphoreType.DMA((7,)),
                        pltpu.SemaphoreType.DMA],
    )(s)


def _pack_rows(parts, width, mult):
    out = []
    for a in parts:
        flat = a.reshape(-1)
        per = width * mult
        padn = (-flat.shape[0]) % per
        if padn:
            flat = jnp.concatenate([flat, jnp.zeros((padn,), flat.dtype)])
        out.append(flat.reshape(-1, width))
    return jnp.concatenate(out, axis=0)


def _rows_of(shape, width, mult):
    n = math.prod(shape)
    per = width * mult
    return ((n + per - 1) // per) * mult


def _unpack_rows(buf, shapes, width, mult):
    lead = buf.shape[:-2]
    out, off = [], 0
    for shp in shapes:
        r = _rows_of(shp, width, mult)
        flat = buf[..., off:off + r, :].reshape(lead + (r * width,))
        out.append(flat[..., :math.prod(shp)].reshape(lead + tuple(shp)))
        off += r
    return out


def _cols_from_devices(g):
    nd = g.ndim
    perm = tuple(range(1, nd - 1)) + (0, nd - 1)
    t = jnp.transpose(g, perm)
    return t.reshape(t.shape[:-2] + (t.shape[-2] * t.shape[-1],))


def _cols_to_devices(a):
    c = a.shape[-1] // N_DEV
    t = a.reshape(a.shape[:-1] + (N_DEV, c))
    nd = t.ndim
    perm = (nd - 2,) + tuple(range(0, nd - 2)) + (nd - 1,)
    return jnp.transpose(t, perm)


def _rows_from_devices(g):
    t = jnp.transpose(g, (1, 0, 2, 3))
    return t.reshape(t.shape[0], t.shape[1] * t.shape[2], t.shape[3])


def _rows_to_devices(a):
    l, r, c = a.shape
    return jnp.transpose(a.reshape(l, N_DEV, r // N_DEV, c), (1, 0, 2, 3))


WIDTH = 1024
BF16_ROWS = 16
F32_ROWS = 8


def kernel(x, meta, ffn1_wg, ffn1_wu, ffn1_wd, ffn2_wg, ffn2_wu, ffn2_wd, ln_gain, ln_bias, conv_w_in, conv_w, conv_w_out, kv_w, f_bias, attn_w_q, attn_w_o, loss_target, m_meta, m_ffn1_wg, m_ffn1_wu, m_ffn1_wd, m_ffn2_wg, m_ffn2_wu, m_ffn2_wd, m_ln_gain, m_ln_bias, m_conv_w_in, m_conv_w, m_conv_w_out, m_kv_w, m_f_bias, m_attn_w_q, m_attn_w_o, v_meta, v_ffn1_wg, v_ffn1_wu, v_ffn1_wd, v_ffn2_wg, v_ffn2_wu, v_ffn2_wd, v_ln_gain, v_ln_bias, v_conv_w_in, v_conv_w, v_conv_w_out, v_kv_w, v_f_bias, v_attn_w_q, v_attn_w_o):
    depth = ln_gain.shape[0]
    alpha = float((2 * depth) ** 0.25)
    d = x.shape[-1]
    seq = x.shape[1]
    t = ROW0 + seq
    fsh = ffn1_wg.shape[-1]
    f = fsh * N_DEV
    nc = f // FFN_CHUNK
    me = 4 * lax.axis_index("x") + 2 * lax.axis_index("y") + lax.axis_index("c")

    big = [ffn1_wg, ffn1_wu, ffn1_wd, ffn2_wg, ffn2_wu, ffn2_wd, conv_w_in, conv_w_out,
           attn_w_q, attn_w_o, kv_w]
    big_shapes = [a.shape for a in big]
    small = [meta, ln_gain, ln_bias, conv_w]
    small_shapes = [a.shape for a in small]
    gbig = _all_gather(_pack_rows([a.astype(BF16) for a in big], WIDTH, BF16_ROWS), "ag_weights")
    gsmall = _all_gather(_pack_rows(small, WIDTH, F32_ROWS), "ag_small")
    (g1g, g1u, g1d, g2g, g2u, g2d, gcin, gcout, gwq, gwo, gkv) = _unpack_rows(gbig, big_shapes, WIDTH, BF16_ROWS)
    gmeta, ggain, gbias, gcw = _unpack_rows(gsmall, small_shapes, WIDTH, F32_ROWS)

    def up_chunks(g, l):
        full = _cols_from_devices(g[:, l])
        return jnp.transpose(full.reshape(d, nc, FFN_CHUNK), (1, 0, 2))

    def down_chunks(g, l):
        return _rows_from_devices(g)[l].reshape(nc, FFN_CHUNK, d)

    w_in = _cols_from_devices(gcin)[0]
    w_out = _rows_from_devices(gcout)[0]
    w_q = _rows_from_devices(gwq)[0]
    w_o = _rows_from_devices(gwo)[0]
    kvw = _cols_from_devices(gkv)
    wk, wv = kvw[:, :d], kvw[:, d:2 * d]
    wf = jnp.pad(kvw[:, 2 * d:], ((0, 0), (0, LANES - N_HEADS)))
    fb = jnp.pad(f_bias, (0, LANES - N_HEADS)).reshape(1, LANES)
    meta_f = _cols_from_devices(gmeta)
    gain_f = _cols_from_devices(ggain)
    bias_f = _cols_from_devices(gbias)
    cw_f = _cols_from_devices(gcw)[0]

    def gb(l, n):
        return gain_f[l, n].reshape(1, d), bias_f[l, n].reshape(1, d)

    ones = jnp.ones((1, d), F32)
    zeros = jnp.zeros((1, d), F32)

    h0 = jnp.concatenate([jnp.zeros((PAD, d), F32), meta_f, x[0]], axis=0)
    hb0 = h0.astype(BF16)

    w1 = (up_chunks(g1g, 0), up_chunks(g1u, 0), down_chunks(g1d, 0))
    g00, b00 = gb(0, 0)
    xh1, rs1, hb1, gg1, uu1 = _ffn_fwd(h0, ones, zeros, *w1, g00, b00, alpha, "ffn_fwd_0a")
    g01, b01 = gb(0, 1)
    xh2, rs2, hb2, pp, mb = _conv_fwd(xh1, g00, b00, w_in, cw_f, w_out, g01, b01, alpha, "conv_fwd")
    w2 = (up_chunks(g2g, 0), up_chunks(g2u, 0), down_chunks(g2d, 0))
    g02, b02 = gb(0, 2)
    xh3, rs3, hb3, gg3, uu3 = _ffn_fwd(xh2, g01, b01, *w2, g02, b02, alpha, "ffn_fwd_0b")
    kk, vv, logit, cc, cct = _kv_fwd(hb3, wk, wv, wf, fb, "kv_fwd")

    w3 = (up_chunks(g1g, 1), up_chunks(g1u, 1), down_chunks(g1d, 1))
    g10, b10 = gb(1, 0)
    xh4, rs4, hb4, gg4, uu4 = _ffn_fwd(xh3, g02, b02, *w3, g10, b10, alpha, "ffn_fwd_1a")
    qq = _proj(hb4, w_q, "q_proj")
    ot, lse = _attn_fwd(qq, kk, vv, cc, cct, "attn_fwd")
    g11, b11 = gb(1, 1)
    xh5, rs5, hb5 = _attn_out_fwd(ot, xh4, g10, b10, w_o, g11, b11, alpha, "attn_out_fwd")
    w4 = (up_chunks(g2g, 1), up_chunks(g2u, 1), down_chunks(g2d, 1))
    g12, b12 = gb(1, 2)
    xh6, rs6, _, gg6, uu6 = _ffn_fwd(xh5, g11, b11, *w4, g12, b12, alpha, "ffn_fwd_1b")

    dh6, loss_l = _loss_head(xh6, g12, b12, loss_target[0], "loss_head")
    loss = lax.psum(loss_l[0, 0], ("x", "y", "c"))

    dgain = [[None] * 3 for _ in range(depth)]
    dbias = [[None] * 3 for _ in range(depth)]

    dh5, do6, dg6, du6, a6, dgain[1][2], dbias[1][2] = _ffn_bwd(dh6, xh6, rs6, g12, gg6, uu6, *w4, alpha, "ffn_bwd_1b")
    dw2g_1, dw2u_1 = _wgrad(hb5, [dg6, du6], "wgrad_up_1b")
    (dw2dT_1,) = _wgrad(do6, [a6], "wgrad_down_1b")

    dres4, dmix5, dot_t, delta, dgain[1][1], dbias[1][1] = _attn_out_bwd(dh5, xh5, rs5, g11, ot, w_o, alpha, "attn_out_bwd")
    dwo = _wgrad_xt(ot, dmix5, "wgrad_wo")
    dq, dkk, dvv, dcs, drow = _attn_bwd(qq, kk, vv, cc, cct, lse, delta, dot_t, "attn_bwd")
    dh4 = _add_proj_nt(dres4, dq, w_q, "q_bwd")
    (dwq,) = _wgrad(hb4, [dq], "wgrad_wq")

    dh3a, do4, dg4, du4, a4, dgain[1][0], dbias[1][0] = _ffn_bwd(dh4, xh4, rs4, g10, gg4, uu4, *w3, alpha, "ffn_bwd_1a")
    dw1g_1, dw1u_1 = _wgrad(hb3, [dg4, du4], "wgrad_up_1a")
    (dw1dT_1,) = _wgrad(do4, [a4], "wgrad_down_1a")

    dcq = jnp.pad(drow[:, :, 0, :].reshape(N_HEADS, t).T, ((0, 0), (0, LANES - N_HEADS)))
    dh3, dlogit, dfb = _kv_bwd(dkk, dvv, dcs, dcq, logit, dh3a, wk, wv, wf, "kv_bwd")
    dwk, dwv = _wgrad(hb3, [dkk, dvv], "wgrad_kv")
    (dwf,) = _wgrad(hb3, [dlogit], "wgrad_f")

    dh2, do3, dg3, du3, a3, dgain[0][2], dbias[0][2] = _ffn_bwd(dh3, xh3, rs3, g02, gg3, uu3, *w2, alpha, "ffn_bwd_0b")
    dw2g_0, dw2u_0 = _wgrad(hb2, [dg3, du3], "wgrad_up_0b")
    (dw2dT_0,) = _wgrad(do3, [a3], "wgrad_down_0b")

    dh1, dmix2, dpp, dcw, dgain[0][1], dbias[0][1] = _conv_bwd(dh2, xh2, rs2, g01, pp, cw_f, w_in, w_out, alpha, "conv_bwd")
    (dwin,) = _wgrad(hb1, [dpp], "wgrad_conv_in")
    (dwout,) = _wgrad(mb, [dmix2], "wgrad_conv_out")

    dh0, do1, dg1, du1, a1, dgain[0][0], dbias[0][0] = _ffn_bwd(dh1, xh1, rs1, g00, gg1, uu1, *w1, alpha, "ffn_bwd_0a")
    dw1g_0, dw1u_0 = _wgrad(hb0, [dg1, du1], "wgrad_up_0a")
    (dw1dT_0,) = _wgrad(do1, [a1], "wgrad_down_0a")

    grad_x = dh0[ROW0:].reshape(1, seq, d)

    def up_send(g0, g1):
        return _cols_to_devices(jnp.stack([g0, g1]))

    def down_send(g0t, g1t):
        return _rows_to_devices(jnp.stack([g0t.T, g1t.T]))

    dkv = jnp.concatenate([dwk, dwv, dwf[:, :N_HEADS]], axis=1)
    send_big = [up_send(dw1g_0, dw1g_1), up_send(dw1u_0, dw1u_1), down_send(dw1dT_0, dw1dT_1),
                up_send(dw2g_0, dw2g_1), up_send(dw2u_0, dw2u_1), down_send(dw2dT_0, dw2dT_1),
                _cols_to_devices(dwin[None]), _rows_to_devices(dwout[None]),
                _rows_to_devices(dwq[None]), _rows_to_devices(dwo[None]), _cols_to_devices(dkv)]
    sbuf = jnp.stack([_pack_rows([p[dev].astype(BF16) for p in send_big], WIDTH, BF16_ROWS)
                      for dev in range(N_DEV)])
    rbig = _sum_sources(_exchange(sbuf, "rs_weights"), "rs_sum")
    grads_big = _unpack_rows(rbig, big_shapes, WIDTH, BF16_ROWS)

    dmeta = dh0[PAD:ROW0]
    dgain_f = jnp.stack([jnp.concatenate(r, axis=0) for r in dgain])
    dbias_f = jnp.stack([jnp.concatenate(r, axis=0) for r in dbias])
    small_full = [dmeta, dgain_f, dbias_f, dcw[None], dfb]
    small_full_shapes = [a.shape for a in small_full]
    rsmall = _sum_sources(_all_gather(_pack_rows(small_full, WIDTH, F32_ROWS), "ag_small_grads"), "small_sum")
    smeta, sgain, sbias, scw, sfb = _unpack_rows(rsmall, small_full_shapes, WIDTH, F32_ROWS)
    csh = d // N_DEV

    def my_cols(a):
        return lax.dynamic_slice_in_dim(a, me * csh, csh, axis=a.ndim - 1)

    grads = {
        "meta": my_cols(smeta), "ffn1_wg": grads_big[0], "ffn1_wu": grads_big[1], "ffn1_wd": grads_big[2],
        "ffn2_wg": grads_big[3], "ffn2_wu": grads_big[4], "ffn2_wd": grads_big[5],
        "ln_gain": my_cols(sgain), "ln_bias": my_cols(sbias), "conv_w_in": grads_big[6],
        "conv_w": my_cols(scw), "conv_w_out": grads_big[7], "kv_w": grads_big[10],
        "f_bias": sfb[0, :N_HEADS], "attn_w_q": grads_big[8], "attn_w_o": grads_big[9],
    }
    weights = dict(meta=meta, ffn1_wg=ffn1_wg, ffn1_wu=ffn1_wu, ffn1_wd=ffn1_wd, ffn2_wg=ffn2_wg,
                   ffn2_wu=ffn2_wu, ffn2_wd=ffn2_wd, ln_gain=ln_gain, ln_bias=ln_bias,
                   conv_w_in=conv_w_in, conv_w=conv_w, conv_w_out=conv_w_out, kv_w=kv_w,
                   f_bias=f_bias, attn_w_q=attn_w_q, attn_w_o=attn_w_o)
    moms = dict(meta=(m_meta, v_meta), ffn1_wg=(m_ffn1_wg, v_ffn1_wg), ffn1_wu=(m_ffn1_wu, v_ffn1_wu),
                ffn1_wd=(m_ffn1_wd, v_ffn1_wd), ffn2_wg=(m_ffn2_wg, v_ffn2_wg), ffn2_wu=(m_ffn2_wu, v_ffn2_wu),
                ffn2_wd=(m_ffn2_wd, v_ffn2_wd), ln_gain=(m_ln_gain, v_ln_gain), ln_bias=(m_ln_bias, v_ln_bias),
                conv_w_in=(m_conv_w_in, v_conv_w_in), conv_w=(m_conv_w, v_conv_w),
                conv_w_out=(m_conv_w_out, v_conv_w_out), kv_w=(m_kv_w, v_kv_w), f_bias=(m_f_bias, v_f_bias),
                attn_w_q=(m_attn_w_q, v_attn_w_q), attn_w_o=(m_attn_w_o, v_attn_w_o))

    names = list(weights)
    g_out, d_out, m_out, v_out = [], [], [], []
    for n in names:
        w = weights[n]
        shp = w.shape
        two = (1, shp[0]) if w.ndim == 1 else (math.prod(shp[:-1]), shp[-1])
        g = grads[n].reshape(shp)
        mm, vv_ = moms[n]
        dl, nm, nv = _adamw(w.reshape(two), g.reshape(two), mm.reshape(two), vv_.reshape(two), "adamw_" + n)
        g_out.append(g)
        d_out.append(dl.reshape(shp))
        m_out.append(nm.reshape(shp))
        v_out.append(nv.reshape(shp))
    return (loss, grad_x, *g_out, *d_out, *m_out, *v_out)
```

```python
import functools
import math

import jax
import jax.numpy as jnp
from jax import lax
from jax.experimental import pallas as pl
from jax.experimental.pallas import tpu as pltpu

F32 = jnp.float32
BF16 = jnp.bfloat16

N_DEV = 8
N_HEADS = 8
N_META = 16
PAD = 112
ROW0 = PAD + N_META
LN_EPS = 1e-5
NEG_INF = -1e30
LANES = 128
FFN_CHUNK = 256

ADAM_LR = 0.001
ADAM_B1 = 0.9
ADAM_B2 = 0.999
ADAM_EPS = 1e-08
ADAM_WD = 0.01
ADAM_STEP = 10

ROW_TILES = (640, 128)
LOSS_TILE = 128
VMEM_BIG = 56 << 20
VMEM_MID = 40 << 20

ANY = pl.BlockSpec(memory_space=pl.ANY)
MESH = pl.DeviceIdType.MESH


def _row_tile(t):
    for c in ROW_TILES:
        if t % c == 0:
            return c
    raise ValueError(f"no row tile for {t}")


def _dot(a, b):
    return jnp.dot(a, b, preferred_element_type=F32)


def _dot_nt(a, b):
    return lax.dot_general(a, b, (((1,), (1,)), ((), ())), preferred_element_type=F32)


def _dot_tn(a, b):
    return lax.dot_general(a, b, (((0,), (0,)), ((), ())), preferred_element_type=F32)


def _params(sem, vmem):
    return pltpu.CompilerParams(dimension_semantics=sem, vmem_limit_bytes=vmem)


def _ln_fwd(z):
    mu = jnp.mean(z, axis=-1, keepdims=True)
    zc = z - mu
    var = jnp.mean(zc * zc, axis=-1, keepdims=True)
    rstd = lax.rsqrt(var + LN_EPS)
    return zc * rstd, rstd


def _ln_bwd(dh, xhat, rstd, gain):
    dxh = dh * gain
    m1 = jnp.mean(dxh, axis=-1, keepdims=True)
    m2 = jnp.mean(dxh * xhat, axis=-1, keepdims=True)
    dz = rstd * (dxh - m1 - xhat * m2)
    return dz, jnp.sum(dh * xhat, axis=0, keepdims=True), jnp.sum(dh, axis=0, keepdims=True)


def _load_resident(pairs, sems):
    cps = [pltpu.make_async_copy(src, dst, sems.at[k]) for k, (src, dst) in enumerate(pairs)]
    for cp in cps:
        cp.start()
    for cp in cps:
        cp.wait()


def _ffn_fwd(xh, gi, bi, wg, wu, wd, go, bo, alpha, name):
    t, d = xh.shape
    nc, _, fc = wg.shape
    f = nc * fc
    tm = _row_tile(t)
    nt = t // tm

    def body(xh_ref, gi_ref, bi_ref, wg_hbm, wu_hbm, wd_hbm, go_ref, bo_ref,
             xo_ref, rs_ref, hb_ref, g_ref, u_ref,
             wg_v, wu_v, wd_v, acc, hbs, sems):
        i = pl.program_id(0)
        c = pl.program_id(1)

        @pl.when((i == 0) & (c == 0))
        def _():
            _load_resident([(wg_hbm, wg_v), (wu_hbm, wu_v), (wd_hbm, wd_v)], sems)

        @pl.when(c == 0)
        def _():
            h = xh_ref[...] * gi_ref[...] + bi_ref[...]
            hbs[...] = h.astype(BF16)
            acc[...] = jnp.zeros_like(acc)

        hb = hbs[...]
        g = _dot(hb, wg_v[c])
        u = _dot(hb, wu_v[c])
        a = (g * jax.nn.sigmoid(g)) * u
        g_ref[...] = g.astype(BF16)
        u_ref[...] = u.astype(BF16)
        acc[...] += _dot(a.astype(BF16), wd_v[c])

        @pl.when(c == nc - 1)
        def _():
            h = xh_ref[...] * gi_ref[...] + bi_ref[...]
            xhat, rstd = _ln_fwd(alpha * h + 0.5 * acc[...])
            xo_ref[...] = xhat
            rs_ref[...] = rstd
            hb_ref[...] = (xhat * go_ref[...] + bo_ref[...]).astype(BF16).T

    row = pl.BlockSpec((tm, d), lambda i, c: (i, 0))
    vec = pl.BlockSpec((1, d), lambda i, c: (0, 0))
    chunk = pl.BlockSpec((tm, fc), lambda i, c: (i, c))
    return pl.pallas_call(
        body, name=name, grid=(nt, nc),
        in_specs=[row, vec, vec, ANY, ANY, ANY, vec, vec],
        out_specs=[row, pl.BlockSpec((tm, 1), lambda i, c: (i, 0)),
                   pl.BlockSpec((d, tm), lambda i, c: (0, i)), chunk, chunk],
        out_shape=[jax.ShapeDtypeStruct((t, d), F32), jax.ShapeDtypeStruct((t, 1), F32),
                   jax.ShapeDtypeStruct((d, t), BF16), jax.ShapeDtypeStruct((t, f), BF16),
                   jax.ShapeDtypeStruct((t, f), BF16)],
        scratch_shapes=[pltpu.VMEM((nc, d, fc), BF16), pltpu.VMEM((nc, d, fc), BF16),
                        pltpu.VMEM((nc, fc, d), BF16), pltpu.VMEM((tm, d), F32),
                        pltpu.VMEM((tm, d), BF16), pltpu.SemaphoreType.DMA((3,))],
        compiler_params=_params(("arbitrary", "arbitrary"), VMEM_BIG),
    )(xh, gi, bi, wg, wu, wd, go, bo)


def _ffn_bwd(dh, xo, rs, go, gs, us, wg, wu, wd, alpha, name):
    t, d = dh.shape
    nc, _, fc = wg.shape
    f = nc * fc
    tm = _row_tile(t)
    nt = t // tm

    def body(dh_ref, xo_ref, rs_ref, go_ref, g_ref, u_ref, wg_hbm, wu_hbm, wd_hbm,
             dhin_ref, dot_ref, dg_ref, du_ref, a_ref, dgain_ref, dbias_ref,
             wg_v, wu_v, wd_v, acc, do_ref, sems):
        i = pl.program_id(0)
        c = pl.program_id(1)

        @pl.when((i == 0) & (c == 0))
        def _():
            _load_resident([(wg_hbm, wg_v), (wu_hbm, wu_v), (wd_hbm, wd_v)], sems)
            dgain_ref[...] = jnp.zeros_like(dgain_ref)
            dbias_ref[...] = jnp.zeros_like(dbias_ref)

        @pl.when(c == 0)
        def _():
            dz, dgp, dbp = _ln_bwd(dh_ref[...], xo_ref[...], rs_ref[...], go_ref[...])
            dgain_ref[...] += dgp
            dbias_ref[...] += dbp
            dob = (0.5 * dz).astype(BF16)
            do_ref[...] = dob
            dot_ref[...] = dob.T
            acc[...] = alpha * dz

        do = do_ref[...]
        g = g_ref[...].astype(F32)
        u = u_ref[...].astype(F32)
        sg = jax.nn.sigmoid(g)
        sl = g * sg
        da = _dot_nt(do, wd_v[c])
        dgb = (da * u * (sg * (1.0 + g * (1.0 - sg)))).astype(BF16)
        dub = (da * sl).astype(BF16)
        a_ref[...] = (sl * u).astype(BF16)
        dg_ref[...] = dgb
        du_ref[...] = dub
        acc[...] += _dot_nt(dgb, wg_v[c]) + _dot_nt(dub, wu_v[c])

        @pl.when(c == nc - 1)
        def _():
            dhin_ref[...] = acc[...]

    row = pl.BlockSpec((tm, d), lambda i, c: (i, 0))
    vec = pl.BlockSpec((1, d), lambda i, c: (0, 0))
    chunk = pl.BlockSpec((tm, fc), lambda i, c: (i, c))
    return pl.pallas_call(
        body, name=name, grid=(nt, nc),
        in_specs=[row, row, pl.BlockSpec((tm, 1), lambda i, c: (i, 0)), vec, chunk, chunk,
                  ANY, ANY, ANY],
        out_specs=[row, pl.BlockSpec((d, tm), lambda i, c: (0, i)), chunk, chunk, chunk, vec, vec],
        out_shape=[jax.ShapeDtypeStruct((t, d), F32), jax.ShapeDtypeStruct((d, t), BF16),
                   jax.ShapeDtypeStruct((t, f), BF16), jax.ShapeDtypeStruct((t, f), BF16),
                   jax.ShapeDtypeStruct((t, f), BF16), jax.ShapeDtypeStruct((1, d), F32),
                   jax.ShapeDtypeStruct((1, d), F32)],
        scratch_shapes=[pltpu.VMEM((nc, d, fc), BF16), pltpu.VMEM((nc, d, fc), BF16),
                        pltpu.VMEM((nc, fc, d), BF16), pltpu.VMEM((tm, d), F32),
                        pltpu.VMEM((tm, d), BF16), pltpu.SemaphoreType.DMA((3,))],
        compiler_params=_params(("arbitrary", "arbitrary"), VMEM_BIG),
    )(dh, xo, rs, go, gs, us, wg, wu, wd)


def _wgrad(xt, ys, name):
    m, t = xt.shape
    n = ys[0].shape[1]
    tn = min(n, FFN_CHUNK)
    ny = len(ys)

    def body(*refs):
        x_hbm = refs[0]
        y_refs = refs[1:1 + ny]
        o_refs = refs[1 + ny:1 + 2 * ny]
        xv, sems = refs[1 + 2 * ny:]

        @pl.when(pl.program_id(0) == 0)
        def _():
            _load_resident([(x_hbm, xv)], sems)

        for y_ref, o_ref in zip(y_refs, o_refs):
            o_ref[...] = _dot(xv[...], y_ref[...].astype(BF16))

    return pl.pallas_call(
        body, name=name, grid=(n // tn,),
        in_specs=[ANY] + [pl.BlockSpec((t, tn), lambda c: (0, c)) for _ in ys],
        out_specs=[pl.BlockSpec((m, tn), lambda c: (0, c)) for _ in ys],
        out_shape=[jax.ShapeDtypeStruct((m, n), F32) for _ in ys],
        scratch_shapes=[pltpu.VMEM((m, t), BF16), pltpu.SemaphoreType.DMA((1,))],
        compiler_params=_params(("arbitrary",), VMEM_BIG),
    )(xt, *ys)


def _cast_t(h, name):
    t, d = h.shape
    tm = _row_tile(t)

    def body(h_ref, o_ref):
        o_ref[...] = h_ref[...].astype(BF16).T

    return pl.pallas_call(
        body, name=name, grid=(t // tm,),
        in_specs=[pl.BlockSpec((tm, d), lambda i: (i, 0))],
        out_specs=pl.BlockSpec((d, tm), lambda i: (0, i)),
        out_shape=jax.ShapeDtypeStruct((d, t), BF16),
        compiler_params=_params(("arbitrary",), VMEM_MID),
    )(h)


def _shift_rows(u, halo, tm):
    r = lax.broadcasted_iota(jnp.int32, (tm, 1), 0)
    u1 = jnp.where(r == 0, halo[7:8], pltpu.roll(u, 1, 0))
    u2 = jnp.where(r == 0, halo[6:7], jnp.where(r == 1, halo[7:8], pltpu.roll(u, 2, 0)))
    return u1, u2


def _conv_fwd(xh, gi, bi, w_in, cw, w_out, go, bo, alpha, name):
    t, d = xh.shape
    tm = _row_tile(t)
    nt = t // tm

    def body(xh_ref, gi_ref, bi_ref, win_ref, cw_ref, wout_ref, go_ref, bo_ref,
             xo_ref, rs_ref, hb_ref, p_ref, m_ref, halo):
        i = pl.program_id(0)

        @pl.when(i == 0)
        def _():
            halo[...] = jnp.zeros_like(halo)

        h = xh_ref[...] * gi_ref[...] + bi_ref[...]
        hb = h.astype(BF16)
        bg = _dot(hb, win_ref[:, 0:d])
        cg = _dot(hb, win_ref[:, d:2 * d])
        val = _dot(hb, win_ref[:, 2 * d:3 * d])
        p_ref[:, 0:d] = bg.astype(BF16)
        p_ref[:, d:2 * d] = cg.astype(BF16)
        p_ref[:, 2 * d:3 * d] = val.astype(BF16)
        rows = i * tm + lax.broadcasted_iota(jnp.int32, (tm, 1), 0)
        u = jnp.where(rows >= PAD, cg * val, 0.0)
        u1, u2 = _shift_rows(u, halo[...], tm)
        halo[...] = u[tm - 8:tm]
        y = cw_ref[0:1] * u2 + cw_ref[1:2] * u1 + cw_ref[2:3] * u
        mb = (bg * y).astype(BF16)
        m_ref[...] = mb.T
        xhat, rstd = _ln_fwd(alpha * h + _dot(mb, wout_ref[...]))
        xo_ref[...] = xhat
        rs_ref[...] = rstd
        hb_ref[...] = (xhat * go_ref[...] + bo_ref[...]).astype(BF16).T

    row = pl.BlockSpec((tm, d), lambda i: (i, 0))
    col = pl.BlockSpec((d, tm), lambda i: (0, i))
    vec = pl.BlockSpec((1, d), lambda i: (0, 0))
    return pl.pallas_call(
        body, name=name, grid=(nt,),
        in_specs=[row, vec, vec, pl.BlockSpec((d, 3 * d), lambda i: (0, 0)),
                  pl.BlockSpec((3, d), lambda i: (0, 0)), pl.BlockSpec((d, d), lambda i: (0, 0)),
                  vec, vec],
        out_specs=[row, pl.BlockSpec((tm, 1), lambda i: (i, 0)), col,
                   pl.BlockSpec((tm, 3 * d), lambda i: (i, 0)), col],
        out_shape=[jax.ShapeDtypeStruct((t, d), F32), jax.ShapeDtypeStruct((t, 1), F32),
                   jax.ShapeDtypeStruct((d, t), BF16), jax.ShapeDtypeStruct((t, 3 * d), BF16),
                   jax.ShapeDtypeStruct((d, t), BF16)],
        scratch_shapes=[pltpu.VMEM((8, d), F32)],
        compiler_params=_params(("arbitrary",), VMEM_BIG),
    )(xh, gi, bi, w_in, cw, w_out, go, bo)


def _conv_bwd(dh, xo, rs, go, p, cw, w_in, w_out, alpha, name):
    t, d = dh.shape
    tm = _row_tile(t)
    nt = t // tm
    tb = tm // 8

    def body(dh_ref, xo_ref, rs_ref, go_ref, p_ref, ph_ref, cw_ref, win_ref, wout_ref,
             dhin_ref, dmix_ref, dp_ref, dcw_ref, dgain_ref, dbias_ref, carry):
        i = pl.program_id(0)
        tile = nt - 1 - i

        @pl.when(i == 0)
        def _():
            carry[...] = jnp.zeros_like(carry)
            dcw_ref[...] = jnp.zeros_like(dcw_ref)
            dgain_ref[...] = jnp.zeros_like(dgain_ref)
            dbias_ref[...] = jnp.zeros_like(dbias_ref)

        dz, dgp, dbp = _ln_bwd(dh_ref[...], xo_ref[...], rs_ref[...], go_ref[...])
        dgain_ref[...] += dgp
        dbias_ref[...] += dbp
        dmixb = dz.astype(BF16)
        dmix_ref[...] = dmixb
        dm = _dot_nt(dmixb, wout_ref[...])

        bg = p_ref[:, 0:d].astype(F32)
        cg = p_ref[:, d:2 * d].astype(F32)
        val = p_ref[:, 2 * d:3 * d].astype(F32)
        rows = tile * tm + lax.broadcasted_iota(jnp.int32, (tm, 1), 0)
        valid = rows >= PAD
        u = jnp.where(valid, cg * val, 0.0)
        hrows = tile * tm - 8 + lax.broadcasted_iota(jnp.int32, (8, 1), 0)
        hu = jnp.where((hrows >= PAD) & (tile > 0),
                       ph_ref[:, d:2 * d].astype(F32) * ph_ref[:, 2 * d:3 * d].astype(F32), 0.0)
        u1, u2 = _shift_rows(u, hu, tm)
        w0, w1, w2 = cw_ref[0:1], cw_ref[1:2], cw_ref[2:3]
        y = w0 * u2 + w1 * u1 + w2 * u
        dbg = dm * y
        dy = dm * bg
        dcw_ref[0:1] += jnp.sum(dy * u2, axis=0, keepdims=True)
        dcw_ref[1:2] += jnp.sum(dy * u1, axis=0, keepdims=True)
        dcw_ref[2:3] += jnp.sum(dy * u, axis=0, keepdims=True)

        nxt = carry[...]
        r = lax.broadcasted_iota(jnp.int32, (tm, 1), 0)
        dy1 = jnp.where(r == tm - 1, nxt[0:1], pltpu.roll(dy, tm - 1, 0))
        dy2 = jnp.where(r == tm - 2, nxt[0:1],
                        jnp.where(r == tm - 1, nxt[1:2], pltpu.roll(dy, tm - 2, 0)))
        carry[...] = dy[0:8]
        du = jnp.where(valid, w2 * dy + w1 * dy1 + w0 * dy2, 0.0)
        dbgb = dbg.astype(BF16)
        dcgb = (du * val).astype(BF16)
        dvalb = (du * cg).astype(BF16)
        dp_ref[:, 0:d] = dbgb
        dp_ref[:, d:2 * d] = dcgb
        dp_ref[:, 2 * d:3 * d] = dvalb
        dhin_ref[...] = (alpha * dz + _dot_nt(dbgb, win_ref[:, 0:d])
                         + _dot_nt(dcgb, win_ref[:, d:2 * d]) + _dot_nt(dvalb, win_ref[:, 2 * d:3 * d]))

    row = pl.BlockSpec((tm, d), lambda i: (nt - 1 - i, 0))
    vec = pl.BlockSpec((1, d), lambda i: (0, 0))
    prow = pl.BlockSpec((tm, 3 * d), lambda i: (nt - 1 - i, 0))
    return pl.pallas_call(
        body, name=name, grid=(nt,),
        in_specs=[row, row, pl.BlockSpec((tm, 1), lambda i: (nt - 1 - i, 0)), vec, prow,
                  pl.BlockSpec((8, 3 * d), lambda i: (jnp.maximum((nt - 1 - i) * tb - 1, 0), 0)),
                  pl.BlockSpec((3, d), lambda i: (0, 0)),
                  pl.BlockSpec((d, 3 * d), lambda i: (0, 0)), pl.BlockSpec((d, d), lambda i: (0, 0))],
        out_specs=[row, row, prow, pl.BlockSpec((3, d), lambda i: (0, 0)), vec, vec],
        out_shape=[jax.ShapeDtypeStruct((t, d), F32), jax.ShapeDtypeStruct((t, d), BF16),
                   jax.ShapeDtypeStruct((t, 3 * d), BF16), jax.ShapeDtypeStruct((3, d), F32),
                   jax.ShapeDtypeStruct((1, d), F32), jax.ShapeDtypeStruct((1, d), F32)],
        scratch_shapes=[pltpu.VMEM((8, d), F32)],
        compiler_params=_params(("arbitrary",), VMEM_BIG),
    )(dh, xo, rs, go, p, p, cw, w_in, w_out)


def _kv_fwd(xh, gi, bi, wk, wv, wf, fb, name):
    t, d = xh.shape
    tm = _row_tile(t)
    nt = t // tm

    def body(xh_ref, gi_ref, bi_ref, wk_ref, wv_ref, wf_ref, fb_ref,
             k_ref, v_ref, lg_ref, c_ref, ct_ref, run):
        i = pl.program_id(0)

        @pl.when(i == 0)
        def _():
            run[...] = jnp.zeros_like(run)

        x = (xh_ref[...] * gi_ref[...] + bi_ref[...]).astype(BF16)
        k_ref[...] = _dot(x, wk_ref[...]).astype(BF16)
        v_ref[...] = _dot(x, wv_ref[...]).astype(BF16)
        logit = _dot(x, wf_ref[...]) + fb_ref[...]
        lg_ref[...] = logit
        logf = jnp.minimum(logit, 0.0) - jnp.log(1.0 + jnp.exp(-jnp.abs(logit)))
        rows = i * tm + lax.broadcasted_iota(jnp.int32, (tm, 1), 0)
        logf = jnp.where(rows >= PAD, logf, 0.0)
        tri = (lax.broadcasted_iota(jnp.int32, (tm, tm), 0)
               >= lax.broadcasted_iota(jnp.int32, (tm, tm), 1)).astype(F32)
        cs = jnp.dot(tri, logf, precision=lax.Precision.HIGHEST, preferred_element_type=F32) + run[...]
        run[...] = cs[tm - 1:tm]
        c_ref[...] = cs
        ct_ref[...] = cs.T

    row = pl.BlockSpec((tm, d), lambda i: (i, 0))
    vec = pl.BlockSpec((1, d), lambda i: (0, 0))
    gate = pl.BlockSpec((tm, LANES), lambda i: (i, 0))
    sq = pl.BlockSpec((d, d), lambda i: (0, 0))
    return pl.pallas_call(
        body, name=name, grid=(nt,),
        in_specs=[row, vec, vec, sq, sq, pl.BlockSpec((d, LANES), lambda i: (0, 0)),
                  pl.BlockSpec((1, LANES), lambda i: (0, 0))],
        out_specs=[row, row, gate, gate, pl.BlockSpec((LANES, tm), lambda i: (0, i))],
        out_shape=[jax.ShapeDtypeStruct((t, d), BF16), jax.ShapeDtypeStruct((t, d), BF16),
                   jax.ShapeDtypeStruct((t, LANES), F32), jax.ShapeDtypeStruct((t, LANES), F32),
                   jax.ShapeDtypeStruct((LANES, t), F32)],
        scratch_shapes=[pltpu.VMEM((1, LANES), F32)],
        compiler_params=_params(("arbitrary",), VMEM_MID),
    )(xh, gi, bi, wk, wv, wf, fb)


def _kv_bwd(dk, dv, dcs, dcq, logit, dh_other, wk, wv, wf, name):
    t, d = dk.shape
    tm = _row_tile(t)
    nt = t // tm

    def body(dk_ref, dv_ref, dcs_ref, dcq_ref, lg_ref, oth_ref, wk_ref, wv_ref, wf_ref,
             dh_ref, dl_ref, dfb_ref, run):
        i = pl.program_id(0)
        tile = nt - 1 - i

        @pl.when(i == 0)
        def _():
            run[...] = jnp.zeros_like(run)
            dfb_ref[...] = jnp.zeros_like(dfb_ref)

        lane = lax.broadcasted_iota(jnp.int32, (tm, LANES), 1)
        dc = dcq_ref[...]
        for hh in range(N_HEADS):
            dc = dc + jnp.where(lane == hh, jnp.sum(dcs_ref[hh], axis=1, keepdims=True), 0.0)
        tri = (lax.broadcasted_iota(jnp.int32, (tm, tm), 0)
               <= lax.broadcasted_iota(jnp.int32, (tm, tm), 1)).astype(F32)
        dlf = jnp.dot(tri, dc, precision=lax.Precision.HIGHEST, preferred_element_type=F32) + run[...]
        run[...] = dlf[0:1]
        rows = tile * tm + lax.broadcasted_iota(jnp.int32, (tm, 1), 0)
        dlogit = jnp.where(rows >= PAD, dlf * jax.nn.sigmoid(-lg_ref[...]), 0.0)
        dfb_ref[...] += jnp.sum(dlogit, axis=0, keepdims=True)
        dlb = dlogit.astype(BF16)
        dl_ref[...] = dlb
        dh_ref[...] = (oth_ref[...] + _dot_nt(dk_ref[...], wk_ref[...])
                       + _dot_nt(dv_ref[...], wv_ref[...]) + _dot_nt(dlb, wf_ref[...]))

    row = pl.BlockSpec((tm, d), lambda i: (nt - 1 - i, 0))
    gate = pl.BlockSpec((tm, LANES), lambda i: (nt - 1 - i, 0))
    sq = pl.BlockSpec((d, d), lambda i: (0, 0))
    return pl.pallas_call(
        body, name=name, grid=(nt,),
        in_specs=[row, row, pl.BlockSpec((N_HEADS, tm, LANES), lambda i: (0, nt - 1 - i, 0)), gate, gate, row,
                  sq, sq, pl.BlockSpec((d, LANES), lambda i: (0, 0))],
        out_specs=[row, gate, pl.BlockSpec((1, LANES), lambda i: (0, 0))],
        out_shape=[jax.ShapeDtypeStruct((t, d), F32), jax.ShapeDtypeStruct((t, LANES), BF16),
                   jax.ShapeDtypeStruct((1, LANES), F32)],
        scratch_shapes=[pltpu.VMEM((1, LANES), F32)],
        compiler_params=_params(("arbitrary",), VMEM_MID),
    )(dk, dv, dcs, dcq, logit, dh_other, wk, wv, wf)


def _proj(xh, gi, bi, w, name):
    t, k = xh.shape
    n = w.shape[1]
    tm = _row_tile(t)

    def body(x_ref, g_ref, b_ref, w_ref, o_ref):
        x = (x_ref[...] * g_ref[...] + b_ref[...]).astype(BF16)
        o_ref[...] = _dot(x, w_ref[...]).astype(BF16)

    vec = pl.BlockSpec((1, k), lambda i: (0, 0))
    return pl.pallas_call(
        body, name=name, grid=(t // tm,),
        in_specs=[pl.BlockSpec((tm, k), lambda i: (i, 0)), vec, vec, pl.BlockSpec((k, n), lambda i: (0, 0))],
        out_specs=pl.BlockSpec((tm, n), lambda i: (i, 0)),
        out_shape=jax.ShapeDtypeStruct((t, n), BF16),
        compiler_params=_params(("arbitrary",), VMEM_MID),
    )(xh, gi, bi, w)


def _add_proj_nt(base, y, w, name):
    t, n = y.shape
    k = w.shape[0]
    tm = _row_tile(t)

    def body(b_ref, y_ref, w_ref, o_ref):
        o_ref[...] = b_ref[...] + _dot_nt(y_ref[...].astype(BF16), w_ref[...])

    return pl.pallas_call(
        body, name=name, grid=(t // tm,),
        in_specs=[pl.BlockSpec((tm, k), lambda i: (i, 0)), pl.BlockSpec((tm, n), lambda i: (i, 0)),
                  pl.BlockSpec((k, n), lambda i: (0, 0))],
        out_specs=pl.BlockSpec((tm, k), lambda i: (i, 0)),
        out_shape=jax.ShapeDtypeStruct((t, k), F32),
        compiler_params=_params(("arbitrary",), VMEM_MID),
    )(base, y, w)


def _attn_out_fwd(ot, xh, gi, bi, w_o, go, bo, alpha, name):
    t, d = xh.shape
    tm = _row_tile(t)

    def body(ot_ref, xh_ref, gi_ref, bi_ref, wo_ref, go_ref, bo_ref, xo_ref, rs_ref, hb_ref):
        h = xh_ref[...] * gi_ref[...] + bi_ref[...]
        xhat, rstd = _ln_fwd(alpha * h + _dot_tn(ot_ref[...], wo_ref[...]))
        xo_ref[...] = xhat
        rs_ref[...] = rstd
        hb_ref[...] = (xhat * go_ref[...] + bo_ref[...]).astype(BF16).T

    row = pl.BlockSpec((tm, d), lambda i: (i, 0))
    col = pl.BlockSpec((d, tm), lambda i: (0, i))
    vec = pl.BlockSpec((1, d), lambda i: (0, 0))
    return pl.pallas_call(
        body, name=name, grid=(t // tm,),
        in_specs=[col, row, vec, vec, pl.BlockSpec((d, d), lambda i: (0, 0)), vec, vec],
        out_specs=[row, pl.BlockSpec((tm, 1), lambda i: (i, 0)), col],
        out_shape=[jax.ShapeDtypeStruct((t, d), F32), jax.ShapeDtypeStruct((t, 1), F32),
                   jax.ShapeDtypeStruct((d, t), BF16)],
        compiler_params=_params(("arbitrary",), VMEM_MID),
    )(ot, xh, gi, bi, w_o, go, bo)


def _attn_out_bwd(dh, xo, rs, go, ot, w_o, alpha, name):
    t, d = dh.shape
    tm = _row_tile(t)
    hd = d // N_HEADS

    def body(dh_ref, xo_ref, rs_ref, go_ref, ot_ref, wo_ref,
             dres_ref, dmix_ref, dot_ref, delta_ref, dgain_ref, dbias_ref):
        @pl.when(pl.program_id(0) == 0)
        def _():
            dgain_ref[...] = jnp.zeros_like(dgain_ref)
            dbias_ref[...] = jnp.zeros_like(dbias_ref)

        dz, dgp, dbp = _ln_bwd(dh_ref[...], xo_ref[...], rs_ref[...], go_ref[...])
        dgain_ref[...] += dgp
        dbias_ref[...] += dbp
        dres_ref[...] = alpha * dz
        dmixb = dz.astype(BF16)
        dmix_ref[...] = dmixb
        dot_t = _dot_nt(wo_ref[...], dmixb)
        dot_ref[...] = dot_t.astype(BF16)
        prod = dot_t * ot_ref[...].astype(F32)
        delta_ref[...] = jnp.sum(prod.reshape(N_HEADS, hd, tm), axis=1)

    row = pl.BlockSpec((tm, d), lambda i: (i, 0))
    vec = pl.BlockSpec((1, d), lambda i: (0, 0))
    col = pl.BlockSpec((d, tm), lambda i: (0, i))
    return pl.pallas_call(
        body, name=name, grid=(t // tm,),
        in_specs=[row, row, pl.BlockSpec((tm, 1), lambda i: (i, 0)), vec, col,
                  pl.BlockSpec((d, d), lambda i: (0, 0))],
        out_specs=[row, row, col, pl.BlockSpec((N_HEADS, tm), lambda i: (0, i)), vec, vec],
        out_shape=[jax.ShapeDtypeStruct((t, d), F32), jax.ShapeDtypeStruct((t, d), BF16),
                   jax.ShapeDtypeStruct((d, t), BF16), jax.ShapeDtypeStruct((N_HEADS, t), F32),
                   jax.ShapeDtypeStruct((1, d), F32), jax.ShapeDtypeStruct((1, d), F32)],
        compiler_params=_params(("arbitrary",), VMEM_MID),
    )(dh, xo, rs, go, ot, w_o)


def _scores_t(k, q, ct_ref, c_ref, h, i, j, tq, tk, scale, masked):
    st = _dot_nt(k, q) * scale
    sub = lax.broadcasted_iota(jnp.int32, (8, tq), 0)
    cq = jnp.sum(jnp.where(sub == h, ct_ref[...], 0.0), axis=0, keepdims=True)
    lane = lax.broadcasted_iota(jnp.int32, (tk, LANES), 1)
    ck = jnp.sum(jnp.where(lane == h, c_ref[...], 0.0), axis=1, keepdims=True)
    st = st + cq - ck
    if not masked:
        return st
    kpos = j * tk + lax.broadcasted_iota(jnp.int32, (tk, 1), 0)
    qpos = i * tq + lax.broadcasted_iota(jnp.int32, (1, tq), 1)
    return jnp.where((kpos <= qpos) & (kpos >= PAD), st, NEG_INF)


def _tri_pairs(n, by_row):
    if by_row:
        pairs = [(i, j) for i in range(n) for j in range(i + 1)]
    else:
        pairs = [(i, j) for j in range(n) for i in range(j, n)]
    return (jnp.asarray([p[0] for p in pairs], jnp.int32), jnp.asarray([p[1] for p in pairs], jnp.int32))


def _attn_fwd(q, k, v, c, ct, name):
    t, d = q.shape
    hd = d // N_HEADS
    tq = tk = _row_tile(t)
    nq = t // tq
    scale = 1.0 / math.sqrt(hd)

    def body(it_ref, jt_ref, q_ref, k_ref, v_ref, c_ref, ct_ref, ot_ref, lse_ref, m_s, l_s, acc):
        h, p_ = pl.program_id(0), pl.program_id(1)
        i, j = it_ref[p_], jt_ref[p_]

        @pl.when(j == 0)
        def _():
            m_s[...] = jnp.full_like(m_s, NEG_INF)
            l_s[...] = jnp.zeros_like(l_s)
            acc[...] = jnp.zeros_like(acc)

        def update(masked):
            st = _scores_t(k_ref[...], q_ref[...], ct_ref, c_ref, h, i, j, tq, tk, scale, masked)
            m_new = jnp.maximum(m_s[...], jnp.max(st, axis=0, keepdims=True))
            a = jnp.exp(m_s[...] - m_new)
            p = jnp.exp(st - m_new)
            l_s[...] = a * l_s[...] + jnp.sum(p, axis=0, keepdims=True)
            acc[...] = a * acc[...] + _dot_tn(v_ref[...], p.astype(BF16))
            m_s[...] = m_new

        edge = (j == i) | (j == 0)
        pl.when(edge)(lambda: update(True))
        pl.when(jnp.logical_not(edge))(lambda: update(False))

        @pl.when(j == i)
        def _():
            ot_ref[...] = (acc[...] / l_s[...]).astype(BF16)
            lse_ref[0] = m_s[...] + jnp.log(l_s[...])

    it, jt = _tri_pairs(nq, by_row=True)
    kv = pl.BlockSpec((tk, hd), lambda h, p, it, jt: (jt[p], h))
    return pl.pallas_call(
        body, name=name,
        grid_spec=pltpu.PrefetchScalarGridSpec(
            num_scalar_prefetch=2, grid=(N_HEADS, it.shape[0]),
            in_specs=[pl.BlockSpec((tq, hd), lambda h, p, it, jt: (it[p], h)), kv, kv,
                      pl.BlockSpec((tk, LANES), lambda h, p, it, jt: (jt[p], 0)),
                      pl.BlockSpec((8, tq), lambda h, p, it, jt: (0, it[p]))],
            out_specs=[pl.BlockSpec((hd, tq), lambda h, p, it, jt: (h, it[p])),
                       pl.BlockSpec((1, 1, tq), lambda h, p, it, jt: (h, 0, it[p]))],
            scratch_shapes=[pltpu.VMEM((1, tq), F32), pltpu.VMEM((1, tq), F32), pltpu.VMEM((hd, tq), F32)]),
        out_shape=[jax.ShapeDtypeStruct((d, t), BF16), jax.ShapeDtypeStruct((N_HEADS, 1, t), F32)],
        compiler_params=_params(("arbitrary", "arbitrary"), VMEM_MID),
    )(it, jt, q, k, v, c, ct)


def _attn_bwd(q, k, v, c, ct, lse, delta, dot_t, name):
    t, d = q.shape
    hd = d // N_HEADS
    tq = tk = _row_tile(t)
    nq = t // tq
    scale = 1.0 / math.sqrt(hd)

    def body(it_ref, jt_ref, q_ref, k_ref, v_ref, c_ref, ct_ref, lse_ref, delta_ref, dot_ref,
             dq_ref, dk_ref, dv_ref, dcs_ref, drow_ref, dk_acc, dv_acc, dc_acc):
        h, p_ = pl.program_id(0), pl.program_id(1)
        i, j = it_ref[p_], jt_ref[p_]

        @pl.when(p_ == 0)
        def _():
            dq_ref[...] = jnp.zeros_like(dq_ref)
            drow_ref[...] = jnp.zeros_like(drow_ref)

        @pl.when(i == j)
        def _():
            dk_acc[...] = jnp.zeros_like(dk_acc)
            dv_acc[...] = jnp.zeros_like(dv_acc)
            dc_acc[...] = jnp.zeros_like(dc_acc)

        def update(masked):
            qv, kv_, vv = q_ref[...], k_ref[...], v_ref[...]
            st = _scores_t(kv_, qv, ct_ref, c_ref, h, i, j, tq, tk, scale, masked)
            p = jnp.exp(st - lse_ref[0])
            do_t = dot_ref[...]
            dp = _dot(vv, do_t)
            sub = lax.broadcasted_iota(jnp.int32, (8, tq), 0)
            dl = jnp.sum(jnp.where(sub == h, delta_ref[...], 0.0), axis=0, keepdims=True)
            ds = p * (dp - dl)
            dsb = ds.astype(BF16)
            dv_acc[...] += _dot_nt(p.astype(BF16), do_t)
            dk_acc[...] += _dot(dsb, qv) * scale
            rows = pl.ds(pl.multiple_of(i * tq, tq), tq)
            dq_ref[rows, :] += _dot_tn(dsb, kv_) * scale
            part = ds[:, 0:LANES]
            for g in range(1, tq // LANES):
                part = part + ds[:, g * LANES:(g + 1) * LANES]
            dc_acc[...] += part
            drow_ref[0, i] += jnp.broadcast_to(jnp.sum(ds, axis=0, keepdims=True), (8, tq))

        edge = (j == i) | (j == 0)
        pl.when(edge)(lambda: update(True))
        pl.when(jnp.logical_not(edge))(lambda: update(False))

        @pl.when(i == nq - 1)
        def _():
            dk_ref[...] = dk_acc[...].astype(BF16)
            dv_ref[...] = dv_acc[...].astype(BF16)
            dcs_ref[0] = -dc_acc[...]

    it, jt = _tri_pairs(nq, by_row=False)
    kv = pl.BlockSpec((tk, hd), lambda h, p, it, jt: (jt[p], h))
    return pl.pallas_call(
        body, name=name,
        grid_spec=pltpu.PrefetchScalarGridSpec(
            num_scalar_prefetch=2, grid=(N_HEADS, it.shape[0]),
            in_specs=[pl.BlockSpec((tq, hd), lambda h, p, it, jt: (it[p], h)), kv, kv,
                      pl.BlockSpec((tk, LANES), lambda h, p, it, jt: (jt[p], 0)),
                      pl.BlockSpec((8, tq), lambda h, p, it, jt: (0, it[p])),
                      pl.BlockSpec((1, 1, tq), lambda h, p, it, jt: (h, 0, it[p])),
                      pl.BlockSpec((N_HEADS, tq), lambda h, p, it, jt: (0, it[p])),
                      pl.BlockSpec((hd, tq), lambda h, p, it, jt: (h, it[p]))],
            out_specs=[pl.BlockSpec((t, hd), lambda h, p, it, jt: (0, h)), kv, kv,
                       pl.BlockSpec((1, tk, LANES), lambda h, p, it, jt: (h, jt[p], 0)),
                       pl.BlockSpec((1, nq, 8, tq), lambda h, p, it, jt: (h, 0, 0, 0))],
            scratch_shapes=[pltpu.VMEM((tk, hd), F32), pltpu.VMEM((tk, hd), F32),
                            pltpu.VMEM((tk, LANES), F32)]),
        out_shape=[jax.ShapeDtypeStruct((t, d), F32), jax.ShapeDtypeStruct((t, d), BF16),
                   jax.ShapeDtypeStruct((t, d), BF16), jax.ShapeDtypeStruct((N_HEADS, t, LANES), F32),
                   jax.ShapeDtypeStruct((N_HEADS, nq, 8, tq), F32)],
        compiler_params=_params(("arbitrary", "arbitrary"), VMEM_MID),
    )(it, jt, q, k, v, c, ct, lse, delta, dot_t)


def _loss_head(xh, g, b, target, name):
    t, d = xh.shape
    tm = LOSS_TILE
    nt = t // tm
    lead = ROW0 // tm

    def body(xh_ref, g_ref, b_ref, tg_ref, dh_ref, loss_ref, part):
        i = pl.program_id(0)

        @pl.when(i == 0)
        def _():
            part[...] = jnp.zeros_like(part)

        @pl.when(i < lead)
        def _():
            dh_ref[...] = jnp.zeros_like(dh_ref)

        @pl.when(i >= lead)
        def _():
            e = xh_ref[...] * g_ref[...] + b_ref[...] - tg_ref[...]
            dh_ref[...] = e * (1.0 / d)
            part[...] += jnp.sum(e * e, axis=0, keepdims=True)

        @pl.when(i == nt - 1)
        def _():
            loss_ref[...] = jnp.full((1, LANES), 0.5 / d, F32) * jnp.sum(part[...])

    return pl.pallas_call(
        body, name=name, grid=(nt,),
        in_specs=[pl.BlockSpec((tm, d), lambda i: (i, 0)), pl.BlockSpec((1, d), lambda i: (0, 0)),
                  pl.BlockSpec((1, d), lambda i: (0, 0)),
                  pl.BlockSpec((tm, d), lambda i: (jnp.maximum(i - lead, 0), 0))],
        out_specs=[pl.BlockSpec((tm, d), lambda i: (i, 0)), pl.BlockSpec((1, LANES), lambda i: (0, 0))],
        out_shape=[jax.ShapeDtypeStruct((t, d), F32), jax.ShapeDtypeStruct((1, LANES), F32)],
        scratch_shapes=[pltpu.VMEM((1, d), F32)],
        compiler_params=_params(("arbitrary",), VMEM_MID),
    )(xh, g, b, target)


def _adamw(w, g, m, v, name):
    r, c = w.shape
    tr = r
    for cand in (256, 128, 64, 32, 16, 8):
        if r % cand == 0 and r > cand:
            tr = cand
            break
    bc1 = 1.0 - ADAM_B1 ** ADAM_STEP
    bc2 = 1.0 - ADAM_B2 ** ADAM_STEP

    def body(w_ref, g_ref, m_ref, v_ref, d_ref, nm_ref, nv_ref):
        gg = g_ref[...]
        nm = ADAM_B1 * m_ref[...] + (1.0 - ADAM_B1) * gg
        nv = ADAM_B2 * v_ref[...] + (1.0 - ADAM_B2) * (gg * gg)
        d_ref[...] = -ADAM_LR * ((nm / bc1) / (jnp.sqrt(nv / bc2) + ADAM_EPS) + ADAM_WD * w_ref[...])
        nm_ref[...] = nm
        nv_ref[...] = nv

    blk = pl.BlockSpec((tr, c), lambda i: (i, 0))
    shp = jax.ShapeDtypeStruct((r, c), F32)
    return pl.pallas_call(
        body, name=name, grid=(r // tr,), in_specs=[blk] * 4, out_specs=[blk] * 3,
        out_shape=[shp] * 3, compiler_params=_params(("arbitrary",), VMEM_MID),
    )(w, g, m, v)


def _sum_sources(r, name):
    n, rows, c = r.shape
    tr = rows
    for cand in (752, 512, 256, 128, 64, 32, 16):
        if rows % cand == 0 and rows > cand:
            tr = cand
            break

    def body(r_ref, o_ref):
        acc = r_ref[0].astype(F32)
        for s in range(1, n):
            acc = acc + r_ref[s].astype(F32)
        o_ref[...] = acc

    return pl.pallas_call(
        body, name=name, grid=(rows // tr,),
        in_specs=[pl.BlockSpec((n, tr, c), lambda i: (0, i, 0))],
        out_specs=pl.BlockSpec((tr, c), lambda i: (i, 0)),
        out_shape=jax.ShapeDtypeStruct((rows, c), F32),
        compiler_params=_params(("arbitrary",), VMEM_MID),
    )(r)


def _all_gather(x, name):
    rows, cols = x.shape

    def body(x_ref, out_ref, send_sems, recv_sems, local_sem):
        mx, my, mc = lax.axis_index("x"), lax.axis_index("y"), lax.axis_index("c")
        me, sibling = (mx, my, mc), (mx, my, 1 - mc)
        chips = [(1 - mx, my), (mx, 1 - my), (1 - mx, 1 - my)]

        def slot(px, py, pc):
            return out_ref.at[4 * px + 2 * py + pc]

        def copy(k, block, to, src=None):
            return pltpu.make_async_remote_copy(
                src_ref=slot(*block) if src is None else src, dst_ref=slot(*block),
                send_sem=send_sems.at[k], recv_sem=recv_sems.at[k],
                device_id=to, device_id_type=MESH)

        mine = pltpu.make_async_copy(x_ref, slot(*me), local_sem)
        mine.start()
        first = [copy(0, me, sibling, src=x_ref)]
        first += [copy(1 + n, me, (*chip, mc), src=x_ref) for n, chip in enumerate(chips)]
        for cp in first:
            cp.start()
        passed = [copy(4 + n, (*chip, mc), sibling) for n, chip in enumerate(chips)]
        for n, chip in enumerate(chips):
            copy(1 + n, (*chip, mc), me).wait_recv()
            passed[n].start()
        copy(0, sibling, me).wait_recv()
        for n, chip in enumerate(chips):
            copy(4 + n, (*chip, 1 - mc), me).wait_recv()
        for cp in first + passed:
            cp.wait_send()
        mine.wait()

    return pl.pallas_call(
        body, name=name, in_specs=[ANY], out_specs=ANY,
        out_shape=jax.ShapeDtypeStruct((N_DEV, rows, cols), x.dtype),
        scratch_shapes=[pltpu.SemaphoreType.DMA((7,)), pltpu.SemaphoreType.DMA((7,)),
                        pltpu.SemaphoreType.DMA],
    )(x)


def _exchange(s, name):
    _, rows, cols = s.shape

    def body(s_ref, r_ref, send_sems, recv_sems, local_sem):
        mx, my, mc = lax.axis_index("x"), lax.axis_index("y"), lax.axis_index("c")
        me_id = 4 * mx + 2 * my + mc
        mine = pltpu.make_async_copy(s_ref.at[me_id], r_ref.at[me_id], local_sem)
        mine.start()
        sends, recvs = [], []
        for kk in range(1, N_DEV):
            px = 1 - mx if (kk >> 2) & 1 else mx
            py = 1 - my if (kk >> 1) & 1 else my
            pc = 1 - mc if kk & 1 else mc
            pid = 4 * px + 2 * py + pc
            sends.append(pltpu.make_async_remote_copy(
                src_ref=s_ref.at[pid], dst_ref=r_ref.at[me_id],
                send_sem=send_sems.at[kk - 1], recv_sem=recv_sems.at[kk - 1],
                device_id=(px, py, pc), device_id_type=MESH))
            recvs.append(pltpu.make_async_remote_copy(
                src_ref=s_ref.at[pid], dst_ref=r_ref.at[pid],
                send_sem=send_sems.at[kk - 1], recv_sem=recv_sems.at[kk - 1],
                device_id=(px, py, pc), device_id_type=MESH))
        for cp in sends:
            cp.start()
        for cp in recvs:
            cp.wait_recv()
        for cp in sends:
            cp.wait_send()
        mine.wait()

    return pl.pallas_call(
        body, name=name, in_specs=[ANY], out_specs=ANY,
        out_shape=jax.ShapeDtypeStruct(s.shape, s.dtype),
        scratch_shapes=[pltpu.SemaphoreType.DMA((7,)), pltpu.SemaphoreType.DMA((7,)),
                        pltpu.SemaphoreType.DMA],
    )(s)


def _pack_rows(parts, width, mult):
    out = []
    for a in parts:
        flat = a.reshape(-1)
        per = width * mult
        padn = (-flat.shape[0]) % per
        if padn:
            flat = jnp.concatenate([flat, jnp.zeros((padn,), flat.dtype)])
        out.append(flat.reshape(-1, width))
    return jnp.concatenate(out, axis=0)


def _rows_of(shape, width, mult):
    n = math.prod(shape)
    per = width * mult
    return ((n + per - 1) // per) * mult


def _unpack_rows(buf, shapes, width, mult):
    lead = buf.shape[:-2]
    out, off = [], 0
    for shp in shapes:
        r = _rows_of(shp, width, mult)
        flat = buf[..., off:off + r, :].reshape(lead + (r * width,))
        out.append(flat[..., :math.prod(shp)].reshape(lead + tuple(shp)))
        off += r
    return out


def _cols_from_devices(g):
    nd = g.ndim
    perm = tuple(range(1, nd - 1)) + (0, nd - 1)
    t = jnp.transpose(g, perm)
    return t.reshape(t.shape[:-2] + (t.shape[-2] * t.shape[-1],))


def _cols_to_devices(a):
    c = a.shape[-1] // N_DEV
    t = a.reshape(a.shape[:-1] + (N_DEV, c))
    nd = t.ndim
    perm = (nd - 2,) + tuple(range(0, nd - 2)) + (nd - 1,)
    return jnp.transpose(t, perm)


def _rows_from_devices(g):
    t = jnp.transpose(g, (1, 0, 2, 3))
    return t.reshape(t.shape[0], t.shape[1] * t.shape[2], t.shape[3])


def _rows_to_devices(a):
    l, r, c = a.shape
    return jnp.transpose(a.reshape(l, N_DEV, r // N_DEV, c), (1, 0, 2, 3))


WIDTH = 1024
BF16_ROWS = 16
F32_ROWS = 8


def kernel(x, meta, ffn1_wg, ffn1_wu, ffn1_wd, ffn2_wg, ffn2_wu, ffn2_wd, ln_gain, ln_bias, conv_w_in, conv_w, conv_w_out, kv_w, f_bias, attn_w_q, attn_w_o, loss_target, m_meta, m_ffn1_wg, m_ffn1_wu, m_ffn1_wd, m_ffn2_wg, m_ffn2_wu, m_ffn2_wd, m_ln_gain, m_ln_bias, m_conv_w_in, m_conv_w, m_conv_w_out, m_kv_w, m_f_bias, m_attn_w_q, m_attn_w_o, v_meta, v_ffn1_wg, v_ffn1_wu, v_ffn1_wd, v_ffn2_wg, v_ffn2_wu, v_ffn2_wd, v_ln_gain, v_ln_bias, v_conv_w_in, v_conv_w, v_conv_w_out, v_kv_w, v_f_bias, v_attn_w_q, v_attn_w_o):
    depth = ln_gain.shape[0]
    alpha = float((2 * depth) ** 0.25)
    d = x.shape[-1]
    seq = x.shape[1]
    t = ROW0 + seq
    fsh = ffn1_wg.shape[-1]
    f = fsh * N_DEV
    nc = f // FFN_CHUNK
    me = 4 * lax.axis_index("x") + 2 * lax.axis_index("y") + lax.axis_index("c")

    big = [ffn1_wg, ffn1_wu, ffn1_wd, ffn2_wg, ffn2_wu, ffn2_wd, conv_w_in, conv_w_out,
           attn_w_q, attn_w_o, kv_w]
    big_shapes = [a.shape for a in big]
    small = [meta, ln_gain, ln_bias, conv_w]
    small_shapes = [a.shape for a in small]
    gbig = _all_gather(_pack_rows([a.astype(BF16) for a in big], WIDTH, BF16_ROWS), "ag_weights")
    gsmall = _all_gather(_pack_rows(small, WIDTH, F32_ROWS), "ag_small")
    (g1g, g1u, g1d, g2g, g2u, g2d, gcin, gcout, gwq, gwo, gkv) = _unpack_rows(gbig, big_shapes, WIDTH, BF16_ROWS)
    gmeta, ggain, gbias, gcw = _unpack_rows(gsmall, small_shapes, WIDTH, F32_ROWS)

    def up_chunks(g, l):
        full = _cols_from_devices(g[:, l])
        return jnp.transpose(full.reshape(d, nc, FFN_CHUNK), (1, 0, 2))

    def down_chunks(g, l):
        return _rows_from_devices(g)[l].reshape(nc, FFN_CHUNK, d)

    w_in = _cols_from_devices(gcin)[0]
    w_out = _rows_from_devices(gcout)[0]
    w_q = _rows_from_devices(gwq)[0]
    w_o = _rows_from_devices(gwo)[0]
    kvw = _cols_from_devices(gkv)
    wk, wv = kvw[:, :d], kvw[:, d:2 * d]
    wf = jnp.pad(kvw[:, 2 * d:], ((0, 0), (0, LANES - N_HEADS)))
    fb = jnp.pad(f_bias, (0, LANES - N_HEADS)).reshape(1, LANES)
    meta_f = _cols_from_devices(gmeta)
    gain_f = _cols_from_devices(ggain)
    bias_f = _cols_from_devices(gbias)
    cw_f = _cols_from_devices(gcw)[0]

    def gb(l, n):
        return gain_f[l, n].reshape(1, d), bias_f[l, n].reshape(1, d)

    ones = jnp.ones((1, d), F32)
    zeros = jnp.zeros((1, d), F32)

    h0 = jnp.concatenate([jnp.zeros((PAD, d), F32), meta_f, x[0]], axis=0)
    hb0 = _cast_t(h0, "h0_bf16_t")

    w1 = (up_chunks(g1g, 0), up_chunks(g1u, 0), down_chunks(g1d, 0))
    g00, b00 = gb(0, 0)
    xh1, rs1, hb1, gg1, uu1 = _ffn_fwd(h0, ones, zeros, *w1, g00, b00, alpha, "ffn_fwd_0a")
    g01, b01 = gb(0, 1)
    xh2, rs2, hb2, pp, mb = _conv_fwd(xh1, g00, b00, w_in, cw_f, w_out, g01, b01, alpha, "conv_fwd")
    w2 = (up_chunks(g2g, 0), up_chunks(g2u, 0), down_chunks(g2d, 0))
    g02, b02 = gb(0, 2)
    xh3, rs3, hb3, gg3, uu3 = _ffn_fwd(xh2, g01, b01, *w2, g02, b02, alpha, "ffn_fwd_0b")
    kk, vv, logit, cc, cct = _kv_fwd(xh3, g02, b02, wk, wv, wf, fb, "kv_fwd")

    w3 = (up_chunks(g1g, 1), up_chunks(g1u, 1), down_chunks(g1d, 1))
    g10, b10 = gb(1, 0)
    xh4, rs4, hb4, gg4, uu4 = _ffn_fwd(xh3, g02, b02, *w3, g10, b10, alpha, "ffn_fwd_1a")
    qq = _proj(xh4, g10, b10, w_q, "q_proj")
    ot, lse = _attn_fwd(qq, kk, vv, cc, cct, "attn_fwd")
    g11, b11 = gb(1, 1)
    xh5, rs5, hb5 = _attn_out_fwd(ot, xh4, g10, b10, w_o, g11, b11, alpha, "attn_out_fwd")
    w4 = (up_chunks(g2g, 1), up_chunks(g2u, 1), down_chunks(g2d, 1))
    g12, b12 = gb(1, 2)
    xh6, rs6, _, gg6, uu6 = _ffn_fwd(xh5, g11, b11, *w4, g12, b12, alpha, "ffn_fwd_1b")

    dh6, loss_l = _loss_head(xh6, g12, b12, loss_target[0], "loss_head")
    loss = lax.psum(loss_l[0, 0], ("x", "y", "c"))

    dgain = [[None] * 3 for _ in range(depth)]
    dbias = [[None] * 3 for _ in range(depth)]

    dh5, do6, dg6, du6, a6, dgain[1][2], dbias[1][2] = _ffn_bwd(dh6, xh6, rs6, g12, gg6, uu6, *w4, alpha, "ffn_bwd_1b")
    dw2g_1, dw2u_1 = _wgrad(hb5, [dg6, du6], "wgrad_up_1b")
    (dw2dT_1,) = _wgrad(do6, [a6], "wgrad_down_1b")

    dres4, dmix5, dot_t, delta, dgain[1][1], dbias[1][1] = _attn_out_bwd(dh5, xh5, rs5, g11, ot, w_o, alpha, "attn_out_bwd")
    (dwo,) = _wgrad(ot, [dmix5], "wgrad_wo")
    dq, dkk, dvv, dcs, drow = _attn_bwd(qq, kk, vv, cc, cct, lse, delta, dot_t, "attn_bwd")
    dh4 = _add_proj_nt(dres4, dq, w_q, "q_bwd")
    (dwq,) = _wgrad(hb4, [dq], "wgrad_wq")

    dh3a, do4, dg4, du4, a4, dgain[1][0], dbias[1][0] = _ffn_bwd(dh4, xh4, rs4, g10, gg4, uu4, *w3, alpha, "ffn_bwd_1a")
    dw1g_1, dw1u_1 = _wgrad(hb3, [dg4, du4], "wgrad_up_1a")
    (dw1dT_1,) = _wgrad(do4, [a4], "wgrad_down_1a")

    dcq = jnp.pad(drow[:, :, 0, :].reshape(N_HEADS, t).T, ((0, 0), (0, LANES - N_HEADS)))
    dh3, dlogit, dfb = _kv_bwd(dkk, dvv, dcs, dcq, logit, dh3a, wk, wv, wf, "kv_bwd")
    dwk, dwv = _wgrad(hb3, [dkk, dvv], "wgrad_kv")
    (dwf,) = _wgrad(hb3, [dlogit], "wgrad_f")

    dh2, do3, dg3, du3, a3, dgain[0][2], dbias[0][2] = _ffn_bwd(dh3, xh3, rs3, g02, gg3, uu3, *w2, alpha, "ffn_bwd_0b")
    dw2g_0, dw2u_0 = _wgrad(hb2, [dg3, du3], "wgrad_up_0b")
    (dw2dT_0,) = _wgrad(do3, [a3], "wgrad_down_0b")

    dh1, dmix2, dpp, dcw, dgain[0][1], dbias[0][1] = _conv_bwd(dh2, xh2, rs2, g01, pp, cw_f, w_in, w_out, alpha, "conv_bwd")
    (dwin,) = _wgrad(hb1, [dpp], "wgrad_conv_in")
    (dwout,) = _wgrad(mb, [dmix2], "wgrad_conv_out")

    dh0, do1, dg1, du1, a1, dgain[0][0], dbias[0][0] = _ffn_bwd(dh1, xh1, rs1, g00, gg1, uu1, *w1, alpha, "ffn_bwd_0a")
    dw1g_0, dw1u_0 = _wgrad(hb0, [dg1, du1], "wgrad_up_0a")
    (dw1dT_0,) = _wgrad(do1, [a1], "wgrad_down_0a")

    grad_x = dh0[ROW0:].reshape(1, seq, d)

    def up_send(g0, g1):
        return _cols_to_devices(jnp.stack([g0, g1]))

    def down_send(g0t, g1t):
        return _rows_to_devices(jnp.stack([g0t.T, g1t.T]))

    dkv = jnp.concatenate([dwk, dwv, dwf[:, :N_HEADS]], axis=1)
    send_big = [up_send(dw1g_0, dw1g_1), up_send(dw1u_0, dw1u_1), down_send(dw1dT_0, dw1dT_1),
                up_send(dw2g_0, dw2g_1), up_send(dw2u_0, dw2u_1), down_send(dw2dT_0, dw2dT_1),
                _cols_to_devices(dwin[None]), _rows_to_devices(dwout[None]),
                _rows_to_devices(dwq[None]), _rows_to_devices(dwo[None]), _cols_to_devices(dkv)]
    sbuf = jnp.stack([_pack_rows([p[dev].astype(BF16) for p in send_big], WIDTH, BF16_ROWS)
                      for dev in range(N_DEV)])
    rbig = _sum_sources(_exchange(sbuf, "rs_weights"), "rs_sum")
    grads_big = _unpack_rows(rbig, big_shapes, WIDTH, BF16_ROWS)

    dmeta = dh0[PAD:ROW0]
    dgain_f = jnp.stack([jnp.concatenate(r, axis=0) for r in dgain])
    dbias_f = jnp.stack([jnp.concatenate(r, axis=0) for r in dbias])
    small_full = [dmeta, dgain_f, dbias_f, dcw[None], dfb]
    small_full_shapes = [a.shape for a in small_full]
    rsmall = _sum_sources(_all_gather(_pack_rows(small_full, WIDTH, F32_ROWS), "ag_small_grads"), "small_sum")
    smeta, sgain, sbias, scw, sfb = _unpack_rows(rsmall, small_full_shapes, WIDTH, F32_ROWS)
    csh = d // N_DEV

    def my_cols(a):
        return lax.dynamic_slice_in_dim(a, me * csh, csh, axis=a.ndim - 1)

    grads = {
        "meta": my_cols(smeta), "ffn1_wg": grads_big[0], "ffn1_wu": grads_big[1], "ffn1_wd": grads_big[2],
        "ffn2_wg": grads_big[3], "ffn2_wu": grads_big[4], "ffn2_wd": grads_big[5],
        "ln_gain": my_cols(sgain), "ln_bias": my_cols(sbias), "conv_w_in": grads_big[6],
        "conv_w": my_cols(scw), "conv_w_out": grads_big[7], "kv_w": grads_big[10],
        "f_bias": sfb[0, :N_HEADS], "attn_w_q": grads_big[8], "attn_w_o": grads_big[9],
    }
    weights = dict(meta=meta, ffn1_wg=ffn1_wg, ffn1_wu=ffn1_wu, ffn1_wd=ffn1_wd, ffn2_wg=ffn2_wg,
                   ffn2_wu=ffn2_wu, ffn2_wd=ffn2_wd, ln_gain=ln_gain, ln_bias=ln_bias,
                   conv_w_in=conv_w_in, conv_w=conv_w, conv_w_out=conv_w_out, kv_w=kv_w,
                   f_bias=f_bias, attn_w_q=attn_w_q, attn_w_o=attn_w_o)
    moms = dict(meta=(m_meta, v_meta), ffn1_wg=(m_ffn1_wg, v_ffn1_wg), ffn1_wu=(m_ffn1_wu, v_ffn1_wu),
                ffn1_wd=(m_ffn1_wd, v_ffn1_wd), ffn2_wg=(m_ffn2_wg, v_ffn2_wg), ffn2_wu=(m_ffn2_wu, v_ffn2_wu),
                ffn2_wd=(m_ffn2_wd, v_ffn2_wd), ln_gain=(m_ln_gain, v_ln_gain), ln_bias=(m_ln_bias, v_ln_bias),
                conv_w_in=(m_conv_w_in, v_conv_w_in), conv_w=(m_conv_w, v_conv_w),
                conv_w_out=(m_conv_w_out, v_conv_w_out), kv_w=(m_kv_w, v_kv_w), f_bias=(m_f_bias, v_f_bias),
                attn_w_q=(m_attn_w_q, v_attn_w_q), attn_w_o=(m_attn_w_o, v_attn_w_o))

    names = list(weights)
    g_out, d_out, m_out, v_out = [], [], [], []
    for n in names:
        w = weights[n]
        shp = w.shape
        two = (1, shp[0]) if w.ndim == 1 else (math.prod(shp[:-1]), shp[-1])
        g = grads[n].reshape(shp)
        mm, vv_ = moms[n]
        dl, nm, nv = _adamw(w.reshape(two), g.reshape(two), mm.reshape(two), vv_.reshape(two), "adamw_" + n)
        g_out.append(g)
        d_out.append(dl.reshape(shp))
        m_out.append(nm.reshape(shp))
        v_out.append(nv.reshape(shp))
    return (loss, grad_x, *g_out, *d_out, *m_out, *v_out)
```

```python
import functools
import math

import jax
import jax.numpy as jnp
from jax import lax
from jax.experimental import pallas as pl
from jax.experimental.pallas import tpu as pltpu

F32 = jnp.float32
BF16 = jnp.bfloat16

N_DEV = 8
N_HEADS = 8
N_META = 16
PAD = 112
ROW0 = PAD + N_META
LN_EPS = 1e-5
NEG_INF = -1e30
LANES = 128
FFN_CHUNK = 256

ADAM_LR = 0.001
ADAM_B1 = 0.9
ADAM_B2 = 0.999
ADAM_EPS = 1e-08
ADAM_WD = 0.01
ADAM_STEP = 10

ROW_TILES = (640, 128)
LOSS_TILE = 128
BF16_ROWS = 16
F32_ROWS = 8
SUM_ROWS_MAX = 768
VMEM_BIG = 56 << 20
VMEM_MID = 40 << 20

ANY = pl.BlockSpec(memory_space=pl.ANY)
MESH = pl.DeviceIdType.MESH


def _row_tile(t):
    for c in ROW_TILES:
        if t % c == 0:
            return c
    raise ValueError(f"no row tile for {t}")


def _dot(a, b):
    return jnp.dot(a, b, preferred_element_type=F32)


def _dot_nt(a, b):
    return lax.dot_general(a, b, (((1,), (1,)), ((), ())), preferred_element_type=F32)


def _dot_tn(a, b):
    return lax.dot_general(a, b, (((0,), (0,)), ((), ())), preferred_element_type=F32)


def _params(sem, vmem):
    return pltpu.CompilerParams(dimension_semantics=sem, vmem_limit_bytes=vmem)


def _ln_fwd(z):
    mu = jnp.mean(z, axis=-1, keepdims=True)
    zc = z - mu
    var = jnp.mean(zc * zc, axis=-1, keepdims=True)
    rstd = lax.rsqrt(var + LN_EPS)
    return zc * rstd, rstd


def _ln_bwd(dh, xhat, rstd, gain):
    dxh = dh * gain
    m1 = jnp.mean(dxh, axis=-1, keepdims=True)
    m2 = jnp.mean(dxh * xhat, axis=-1, keepdims=True)
    dz = rstd * (dxh - m1 - xhat * m2)
    return dz, jnp.sum(dh * xhat, axis=0, keepdims=True), jnp.sum(dh, axis=0, keepdims=True)


def _load_resident(pairs, sems):
    cps = [pltpu.make_async_copy(src, dst, sems.at[k]) for k, (src, dst) in enumerate(pairs)]
    for cp in cps:
        cp.start()
    for cp in cps:
        cp.wait()


def _peer_ids():
    mx, my, mc = lax.axis_index("x"), lax.axis_index("y"), lax.axis_index("c")
    peers = []
    for kk in range(1, N_DEV):
        px = 1 - mx if (kk >> 2) & 1 else mx
        py = 1 - my if (kk >> 1) & 1 else my
        pc = 1 - mc if kk & 1 else mc
        peers.append(((px, py, pc), 4 * px + 2 * py + pc))
    return 4 * mx + 2 * my + mc, peers


def _exchange_copies(jobs, send_sems, recv_sems, local_sems, starting):
    me_id, peers = _peer_ids()
    for n, (gather, src, dst) in enumerate(jobs):
        own = pltpu.make_async_copy(src if gather else src.at[me_id], dst.at[me_id], local_sems.at[n])
        own.start() if starting else own.wait()
        for k, (dev, pid) in enumerate(peers):
            sem = (N_DEV - 1) * n + k
            out = src if gather else src.at[pid]
            send = pltpu.make_async_remote_copy(
                src_ref=out, dst_ref=dst.at[me_id], send_sem=send_sems.at[sem], recv_sem=recv_sems.at[sem],
                device_id=dev, device_id_type=MESH)
            if starting:
                send.start()
            else:
                pltpu.make_async_remote_copy(
                    src_ref=out, dst_ref=dst.at[pid], send_sem=send_sems.at[sem], recv_sem=recv_sems.at[sem],
                    device_id=dev, device_id_type=MESH).wait_recv()
                send.wait_send()


def _carried(body, n_in, n_out, carry, first, last):
    nj = len(carry)
    if nj == 0:
        return body

    def wrapped(*refs):
        ins, srcs = refs[:n_in], refs[n_in:n_in + nj]
        outs = refs[n_in + nj:n_in + nj + n_out]
        dsts = refs[n_in + nj + n_out:n_in + 2 * nj + n_out]
        scratch, sems = refs[n_in + 2 * nj + n_out:-3], refs[-3:]
        jobs = [(g, s, r) for (g, _), s, r in zip(carry, srcs, dsts)]

        @pl.when(first())
        def _():
            _exchange_copies(jobs, *sems, starting=True)

        body(*ins, *outs, *scratch)

        @pl.when(last())
        def _():
            _exchange_copies(jobs, *sems, starting=False)

    return wrapped


def _carry_shapes(carry):
    return [jax.ShapeDtypeStruct((N_DEV,) + a.shape if g else a.shape, a.dtype) for g, a in carry]


def _carry_scratch(carry):
    if not carry:
        return []
    n = len(carry)
    return [pltpu.SemaphoreType.DMA(((N_DEV - 1) * n,)), pltpu.SemaphoreType.DMA(((N_DEV - 1) * n,)),
            pltpu.SemaphoreType.DMA((n,))]


def _exchange(s, name):
    carry = [(False, s)]

    def body(s_ref, r_ref, send_sems, recv_sems, local_sems):
        jobs = [(False, s_ref, r_ref)]
        _exchange_copies(jobs, send_sems, recv_sems, local_sems, starting=True)
        _exchange_copies(jobs, send_sems, recv_sems, local_sems, starting=False)

    return pl.pallas_call(
        body, name=name, in_specs=[ANY], out_specs=ANY, out_shape=_carry_shapes(carry)[0],
        scratch_shapes=_carry_scratch(carry),
    )(s)


def _ffn_fwd(xh, gi, bi, wg, wu, wd, go, bo, alpha, name, carry=()):
    t, d = xh.shape
    nc, _, fc = wg.shape
    f = nc * fc
    tm = _row_tile(t)
    nt = t // tm

    def body(xh_ref, gi_ref, bi_ref, wg_hbm, wu_hbm, wd_hbm, go_ref, bo_ref,
             xo_ref, rs_ref, hb_ref, g_ref, u_ref,
             wg_v, wu_v, wd_v, acc, hbs, sems):
        i = pl.program_id(0)
        c = pl.program_id(1)

        @pl.when((i == 0) & (c == 0))
        def _():
            _load_resident([(wg_hbm, wg_v), (wu_hbm, wu_v), (wd_hbm, wd_v)], sems)

        @pl.when(c == 0)
        def _():
            h = xh_ref[...] * gi_ref[...] + bi_ref[...]
            hbs[...] = h.astype(BF16)
            acc[...] = jnp.zeros_like(acc)

        hb = hbs[...]
        g = _dot(hb, wg_v[c])
        u = _dot(hb, wu_v[c])
        a = (g * jax.nn.sigmoid(g)) * u
        g_ref[...] = g.astype(BF16)
        u_ref[...] = u.astype(BF16)
        acc[...] += _dot(a.astype(BF16), wd_v[c])

        @pl.when(c == nc - 1)
        def _():
            h = xh_ref[...] * gi_ref[...] + bi_ref[...]
            xhat, rstd = _ln_fwd(alpha * h + 0.5 * acc[...])
            xo_ref[...] = xhat
            rs_ref[...] = rstd
            hb_ref[...] = (xhat * go_ref[...] + bo_ref[...]).astype(BF16).T

    row = pl.BlockSpec((tm, d), lambda i, c: (i, 0))
    vec = pl.BlockSpec((1, d), lambda i, c: (0, 0))
    chunk = pl.BlockSpec((tm, fc), lambda i, c: (i, c))
    first = lambda: (pl.program_id(0) == 0) & (pl.program_id(1) == 0)
    last = lambda: (pl.program_id(0) == nt - 1) & (pl.program_id(1) == nc - 1)
    return pl.pallas_call(
        _carried(body, 8, 5, carry, first, last), name=name, grid=(nt, nc),
        in_specs=[row, vec, vec, ANY, ANY, ANY, vec, vec] + [ANY] * len(carry),
        out_specs=[row, pl.BlockSpec((tm, 1), lambda i, c: (i, 0)),
                   pl.BlockSpec((d, tm), lambda i, c: (0, i)), chunk, chunk] + [ANY] * len(carry),
        out_shape=[jax.ShapeDtypeStruct((t, d), F32), jax.ShapeDtypeStruct((t, 1), F32),
                   jax.ShapeDtypeStruct((d, t), BF16), jax.ShapeDtypeStruct((t, f), BF16),
                   jax.ShapeDtypeStruct((t, f), BF16)] + _carry_shapes(carry),
        scratch_shapes=[pltpu.VMEM((nc, d, fc), BF16), pltpu.VMEM((nc, d, fc), BF16),
                        pltpu.VMEM((nc, fc, d), BF16), pltpu.VMEM((tm, d), F32),
                        pltpu.VMEM((tm, d), BF16), pltpu.SemaphoreType.DMA((3,))] + _carry_scratch(carry),
        compiler_params=_params(("arbitrary", "arbitrary"), VMEM_BIG),
    )(xh, gi, bi, wg, wu, wd, go, bo, *[a for _, a in carry])


def _ffn_bwd(dh, xo, rs, go, gs, us, wg, wu, wd, alpha, name, carry=()):
    t, d = dh.shape
    nc, _, fc = wg.shape
    f = nc * fc
    tm = _row_tile(t)
    nt = t // tm

    def body(dh_ref, xo_ref, rs_ref, go_ref, g_ref, u_ref, wg_hbm, wu_hbm, wd_hbm,
             dhin_ref, dot_ref, dg_ref, du_ref, a_ref, dgain_ref, dbias_ref,
             wg_v, wu_v, wd_v, acc, do_ref, sems):
        i = pl.program_id(0)
        c = pl.program_id(1)

        @pl.when((i == 0) & (c == 0))
        def _():
            _load_resident([(wg_hbm, wg_v), (wu_hbm, wu_v), (wd_hbm, wd_v)], sems)
            dgain_ref[...] = jnp.zeros_like(dgain_ref)
            dbias_ref[...] = jnp.zeros_like(dbias_ref)

        @pl.when(c == 0)
        def _():
            dz, dgp, dbp = _ln_bwd(dh_ref[...], xo_ref[...], rs_ref[...], go_ref[...])
            dgain_ref[...] += dgp
            dbias_ref[...] += dbp
            dob = (0.5 * dz).astype(BF16)
            do_ref[...] = dob
            dot_ref[...] = dob.T
            acc[...] = alpha * dz

        do = do_ref[...]
        g = g_ref[...].astype(F32)
        u = u_ref[...].astype(F32)
        sg = jax.nn.sigmoid(g)
        sl = g * sg
        da = _dot_nt(do, wd_v[c])
        dgb = (da * u * (sg * (1.0 + g * (1.0 - sg)))).astype(BF16)
        dub = (da * sl).astype(BF16)
        a_ref[...] = (sl * u).astype(BF16)
        dg_ref[...] = dgb
        du_ref[...] = dub
        acc[...] += _dot_nt(dgb, wg_v[c]) + _dot_nt(dub, wu_v[c])

        @pl.when(c == nc - 1)
        def _():
            dhin_ref[...] = acc[...]

    row = pl.BlockSpec((tm, d), lambda i, c: (i, 0))
    vec = pl.BlockSpec((1, d), lambda i, c: (0, 0))
    chunk = pl.BlockSpec((tm, fc), lambda i, c: (i, c))
    first = lambda: (pl.program_id(0) == 0) & (pl.program_id(1) == 0)
    last = lambda: (pl.program_id(0) == nt - 1) & (pl.program_id(1) == nc - 1)
    return pl.pallas_call(
        _carried(body, 9, 7, carry, first, last), name=name, grid=(nt, nc),
        in_specs=[row, row, pl.BlockSpec((tm, 1), lambda i, c: (i, 0)), vec, chunk, chunk,
                  ANY, ANY, ANY] + [ANY] * len(carry),
        out_specs=[row, pl.BlockSpec((d, tm), lambda i, c: (0, i)), chunk, chunk, chunk, vec, vec]
                  + [ANY] * len(carry),
        out_shape=[jax.ShapeDtypeStruct((t, d), F32), jax.ShapeDtypeStruct((d, t), BF16),
                   jax.ShapeDtypeStruct((t, f), BF16), jax.ShapeDtypeStruct((t, f), BF16),
                   jax.ShapeDtypeStruct((t, f), BF16), jax.ShapeDtypeStruct((1, d), F32),
                   jax.ShapeDtypeStruct((1, d), F32)] + _carry_shapes(carry),
        scratch_shapes=[pltpu.VMEM((nc, d, fc), BF16), pltpu.VMEM((nc, d, fc), BF16),
                        pltpu.VMEM((nc, fc, d), BF16), pltpu.VMEM((tm, d), F32),
                        pltpu.VMEM((tm, d), BF16), pltpu.SemaphoreType.DMA((3,))] + _carry_scratch(carry),
        compiler_params=_params(("arbitrary", "arbitrary"), VMEM_BIG),
    )(dh, xo, rs, go, gs, us, wg, wu, wd, *[a for _, a in carry])


def _wgrad(xt, ys, name):
    m, t = xt.shape
    n = ys[0].shape[1]
    tn = min(n, FFN_CHUNK)
    ny = len(ys)

    def body(*refs):
        x_hbm = refs[0]
        y_refs = refs[1:1 + ny]
        o_refs = refs[1 + ny:1 + 2 * ny]
        xv, sems = refs[1 + 2 * ny:]

        @pl.when(pl.program_id(0) == 0)
        def _():
            _load_resident([(x_hbm, xv)], sems)

        for y_ref, o_ref in zip(y_refs, o_refs):
            o_ref[...] = _dot(xv[...], y_ref[...].astype(BF16))

    return pl.pallas_call(
        body, name=name, grid=(n // tn,),
        in_specs=[ANY] + [pl.BlockSpec((t, tn), lambda c: (0, c)) for _ in ys],
        out_specs=[pl.BlockSpec((m, tn), lambda c: (0, c)) for _ in ys],
        out_shape=[jax.ShapeDtypeStruct((m, n), F32) for _ in ys],
        scratch_shapes=[pltpu.VMEM((m, t), BF16), pltpu.SemaphoreType.DMA((1,))],
        compiler_params=_params(("arbitrary",), VMEM_BIG),
    )(xt, *ys)


def _cast_t(h, name):
    t, d = h.shape
    tm = _row_tile(t)

    def body(h_ref, o_ref):
        o_ref[...] = h_ref[...].astype(BF16).T

    return pl.pallas_call(
        body, name=name, grid=(t // tm,),
        in_specs=[pl.BlockSpec((tm, d), lambda i: (i, 0))],
        out_specs=pl.BlockSpec((d, tm), lambda i: (0, i)),
        out_shape=jax.ShapeDtypeStruct((d, t), BF16),
        compiler_params=_params(("arbitrary",), VMEM_MID),
    )(h)


def _shift_rows(u, halo, tm):
    r = lax.broadcasted_iota(jnp.int32, (tm, 1), 0)
    u1 = jnp.where(r == 0, halo[7:8], pltpu.roll(u, 1, 0))
    u2 = jnp.where(r == 0, halo[6:7], jnp.where(r == 1, halo[7:8], pltpu.roll(u, 2, 0)))
    return u1, u2


def _conv_fwd(xh, gi, bi, w_in, cw, w_out, go, bo, alpha, name):
    t, d = xh.shape
    tm = _row_tile(t)
    nt = t // tm

    def body(xh_ref, gi_ref, bi_ref, win_ref, cw_ref, wout_ref, go_ref, bo_ref,
             xo_ref, rs_ref, hb_ref, p_ref, m_ref, halo):
        i = pl.program_id(0)

        @pl.when(i == 0)
        def _():
            halo[...] = jnp.zeros_like(halo)

        h = xh_ref[...] * gi_ref[...] + bi_ref[...]
        hb = h.astype(BF16)
        bg = _dot(hb, win_ref[:, 0:d])
        cg = _dot(hb, win_ref[:, d:2 * d])
        val = _dot(hb, win_ref[:, 2 * d:3 * d])
        p_ref[:, 0:d] = bg.astype(BF16)
        p_ref[:, d:2 * d] = cg.astype(BF16)
        p_ref[:, 2 * d:3 * d] = val.astype(BF16)
        rows = i * tm + lax.broadcasted_iota(jnp.int32, (tm, 1), 0)
        u = jnp.where(rows >= PAD, cg * val, 0.0)
        u1, u2 = _shift_rows(u, halo[...], tm)
        halo[...] = u[tm - 8:tm]
        y = cw_ref[0:1] * u2 + cw_ref[1:2] * u1 + cw_ref[2:3] * u
        mb = (bg * y).astype(BF16)
        m_ref[...] = mb.T
        xhat, rstd = _ln_fwd(alpha * h + _dot(mb, wout_ref[...]))
        xo_ref[...] = xhat
        rs_ref[...] = rstd
        hb_ref[...] = (xhat * go_ref[...] + bo_ref[...]).astype(BF16).T

    row = pl.BlockSpec((tm, d), lambda i: (i, 0))
    col = pl.BlockSpec((d, tm), lambda i: (0, i))
    vec = pl.BlockSpec((1, d), lambda i: (0, 0))
    return pl.pallas_call(
        body, name=name, grid=(nt,),
        in_specs=[row, vec, vec, pl.BlockSpec((d, 3 * d), lambda i: (0, 0)),
                  pl.BlockSpec((3, d), lambda i: (0, 0)), pl.BlockSpec((d, d), lambda i: (0, 0)),
                  vec, vec],
        out_specs=[row, pl.BlockSpec((tm, 1), lambda i: (i, 0)), col,
                   pl.BlockSpec((tm, 3 * d), lambda i: (i, 0)), col],
        out_shape=[jax.ShapeDtypeStruct((t, d), F32), jax.ShapeDtypeStruct((t, 1), F32),
                   jax.ShapeDtypeStruct((d, t), BF16), jax.ShapeDtypeStruct((t, 3 * d), BF16),
                   jax.ShapeDtypeStruct((d, t), BF16)],
        scratch_shapes=[pltpu.VMEM((8, d), F32)],
        compiler_params=_params(("arbitrary",), VMEM_BIG),
    )(xh, gi, bi, w_in, cw, w_out, go, bo)


def _conv_bwd(dh, xo, rs, go, p, cw, w_in, w_out, alpha, name):
    t, d = dh.shape
    tm = _row_tile(t)
    nt = t // tm
    tb = tm // 8

    def body(dh_ref, xo_ref, rs_ref, go_ref, p_ref, ph_ref, cw_ref, win_ref, wout_ref,
             dhin_ref, dmix_ref, dp_ref, dcw_ref, dgain_ref, dbias_ref, carry):
        i = pl.program_id(0)
        tile = nt - 1 - i

        @pl.when(i == 0)
        def _():
            carry[...] = jnp.zeros_like(carry)
            dcw_ref[...] = jnp.zeros_like(dcw_ref)
            dgain_ref[...] = jnp.zeros_like(dgain_ref)
            dbias_ref[...] = jnp.zeros_like(dbias_ref)

        dz, dgp, dbp = _ln_bwd(dh_ref[...], xo_ref[...], rs_ref[...], go_ref[...])
        dgain_ref[...] += dgp
        dbias_ref[...] += dbp
        dmixb = dz.astype(BF16)
        dmix_ref[...] = dmixb
        dm = _dot_nt(dmixb, wout_ref[...])

        bg = p_ref[:, 0:d].astype(F32)
        cg = p_ref[:, d:2 * d].astype(F32)
        val = p_ref[:, 2 * d:3 * d].astype(F32)
        rows = tile * tm + lax.broadcasted_iota(jnp.int32, (tm, 1), 0)
        valid = rows >= PAD
        u = jnp.where(valid, cg * val, 0.0)
        hrows = tile * tm - 8 + lax.broadcasted_iota(jnp.int32, (8, 1), 0)
        hu = jnp.where((hrows >= PAD) & (tile > 0),
                       ph_ref[:, d:2 * d].astype(F32) * ph_ref[:, 2 * d:3 * d].astype(F32), 0.0)
        u1, u2 = _shift_rows(u, hu, tm)
        w0, w1, w2 = cw_ref[0:1], cw_ref[1:2], cw_ref[2:3]
        y = w0 * u2 + w1 * u1 + w2 * u
        dbg = dm * y
        dy = dm * bg
        dcw_ref[0:1] += jnp.sum(dy * u2, axis=0, keepdims=True)
        dcw_ref[1:2] += jnp.sum(dy * u1, axis=0, keepdims=True)
        dcw_ref[2:3] += jnp.sum(dy * u, axis=0, keepdims=True)

        nxt = carry[...]
        r = lax.broadcasted_iota(jnp.int32, (tm, 1), 0)
        dy1 = jnp.where(r == tm - 1, nxt[0:1], pltpu.roll(dy, tm - 1, 0))
        dy2 = jnp.where(r == tm - 2, nxt[0:1],
                        jnp.where(r == tm - 1, nxt[1:2], pltpu.roll(dy, tm - 2, 0)))
        carry[...] = dy[0:8]
        du = jnp.where(valid, w2 * dy + w1 * dy1 + w0 * dy2, 0.0)
        dbgb = dbg.astype(BF16)
        dcgb = (du * val).astype(BF16)
        dvalb = (du * cg).astype(BF16)
        dp_ref[:, 0:d] = dbgb
        dp_ref[:, d:2 * d] = dcgb
        dp_ref[:, 2 * d:3 * d] = dvalb
        dhin_ref[...] = (alpha * dz + _dot_nt(dbgb, win_ref[:, 0:d])
                         + _dot_nt(dcgb, win_ref[:, d:2 * d]) + _dot_nt(dvalb, win_ref[:, 2 * d:3 * d]))

    row = pl.BlockSpec((tm, d), lambda i: (nt - 1 - i, 0))
    vec = pl.BlockSpec((1, d), lambda i: (0, 0))
    prow = pl.BlockSpec((tm, 3 * d), lambda i: (nt - 1 - i, 0))
    return pl.pallas_call(
        body, name=name, grid=(nt,),
        in_specs=[row, row, pl.BlockSpec((tm, 1), lambda i: (nt - 1 - i, 0)), vec, prow,
                  pl.BlockSpec((8, 3 * d), lambda i: (jnp.maximum((nt - 1 - i) * tb - 1, 0), 0)),
                  pl.BlockSpec((3, d), lambda i: (0, 0)),
                  pl.BlockSpec((d, 3 * d), lambda i: (0, 0)), pl.BlockSpec((d, d), lambda i: (0, 0))],
        out_specs=[row, row, prow, pl.BlockSpec((3, d), lambda i: (0, 0)), vec, vec],
        out_shape=[jax.ShapeDtypeStruct((t, d), F32), jax.ShapeDtypeStruct((t, d), BF16),
                   jax.ShapeDtypeStruct((t, 3 * d), BF16), jax.ShapeDtypeStruct((3, d), F32),
                   jax.ShapeDtypeStruct((1, d), F32), jax.ShapeDtypeStruct((1, d), F32)],
        scratch_shapes=[pltpu.VMEM((8, d), F32)],
        compiler_params=_params(("arbitrary",), VMEM_BIG),
    )(dh, xo, rs, go, p, p, cw, w_in, w_out)


def _kv_fwd(xh, gi, bi, wk, wv, wf, fb, name):
    t, d = xh.shape
    tm = _row_tile(t)
    nt = t // tm

    def body(xh_ref, gi_ref, bi_ref, wk_ref, wv_ref, wf_ref, fb_ref,
             k_ref, v_ref, lg_ref, c_ref, ct_ref, run):
        i = pl.program_id(0)

        @pl.when(i == 0)
        def _():
            run[...] = jnp.zeros_like(run)

        x = (xh_ref[...] * gi_ref[...] + bi_ref[...]).astype(BF16)
        k_ref[...] = _dot(x, wk_ref[...]).astype(BF16)
        v_ref[...] = _dot(x, wv_ref[...]).astype(BF16)
        logit = _dot(x, wf_ref[...]) + fb_ref[...]
        lg_ref[...] = logit
        logf = jnp.minimum(logit, 0.0) - jnp.log(1.0 + jnp.exp(-jnp.abs(logit)))
        rows = i * tm + lax.broadcasted_iota(jnp.int32, (tm, 1), 0)
        logf = jnp.where(rows >= PAD, logf, 0.0)
        tri = (lax.broadcasted_iota(jnp.int32, (tm, tm), 0)
               >= lax.broadcasted_iota(jnp.int32, (tm, tm), 1)).astype(F32)
        cs = jnp.dot(tri, logf, precision=lax.Precision.HIGHEST, preferred_element_type=F32) + run[...]
        run[...] = cs[tm - 1:tm]
        c_ref[...] = cs
        ct_ref[...] = cs.T

    row = pl.BlockSpec((tm, d), lambda i: (i, 0))
    vec = pl.BlockSpec((1, d), lambda i: (0, 0))
    gate = pl.BlockSpec((tm, LANES), lambda i: (i, 0))
    sq = pl.BlockSpec((d, d), lambda i: (0, 0))
    return pl.pallas_call(
        body, name=name, grid=(nt,),
        in_specs=[row, vec, vec, sq, sq, pl.BlockSpec((d, LANES), lambda i: (0, 0)),
                  pl.BlockSpec((1, LANES), lambda i: (0, 0))],
        out_specs=[row, row, gate, gate, pl.BlockSpec((LANES, tm), lambda i: (0, i))],
        out_shape=[jax.ShapeDtypeStruct((t, d), BF16), jax.ShapeDtypeStruct((t, d), BF16),
                   jax.ShapeDtypeStruct((t, LANES), F32), jax.ShapeDtypeStruct((t, LANES), F32),
                   jax.ShapeDtypeStruct((LANES, t), F32)],
        scratch_shapes=[pltpu.VMEM((1, LANES), F32)],
        compiler_params=_params(("arbitrary",), VMEM_MID),
    )(xh, gi, bi, wk, wv, wf, fb)


def _kv_bwd(dk, dv, dcs, dcq, logit, dh_other, wk, wv, wf, name):
    t, d = dk.shape
    tm = _row_tile(t)
    nt = t // tm

    def body(dk_ref, dv_ref, dcs_ref, dcq_ref, lg_ref, oth_ref, wk_ref, wv_ref, wf_ref,
             dh_ref, dl_ref, dfb_ref, run):
        i = pl.program_id(0)
        tile = nt - 1 - i

        @pl.when(i == 0)
        def _():
            run[...] = jnp.zeros_like(run)
            dfb_ref[...] = jnp.zeros_like(dfb_ref)

        lane = lax.broadcasted_iota(jnp.int32, (tm, LANES), 1)
        dc = dcq_ref[...]
        for hh in range(N_HEADS):
            dc = dc + jnp.where(lane == hh, jnp.sum(dcs_ref[hh], axis=1, keepdims=True), 0.0)
        tri = (lax.broadcasted_iota(jnp.int32, (tm, tm), 0)
               <= lax.broadcasted_iota(jnp.int32, (tm, tm), 1)).astype(F32)
        dlf = jnp.dot(tri, dc, precision=lax.Precision.HIGHEST, preferred_element_type=F32) + run[...]
        run[...] = dlf[0:1]
        rows = tile * tm + lax.broadcasted_iota(jnp.int32, (tm, 1), 0)
        dlogit = jnp.where(rows >= PAD, dlf * jax.nn.sigmoid(-lg_ref[...]), 0.0)
        dfb_ref[...] += jnp.sum(dlogit, axis=0, keepdims=True)
        dlb = dlogit.astype(BF16)
        dl_ref[...] = dlb
        dh_ref[...] = (oth_ref[...] + _dot_nt(dk_ref[...], wk_ref[...])
                       + _dot_nt(dv_ref[...], wv_ref[...]) + _dot_nt(dlb, wf_ref[...]))

    row = pl.BlockSpec((tm, d), lambda i: (nt - 1 - i, 0))
    gate = pl.BlockSpec((tm, LANES), lambda i: (nt - 1 - i, 0))
    sq = pl.BlockSpec((d, d), lambda i: (0, 0))
    return pl.pallas_call(
        body, name=name, grid=(nt,),
        in_specs=[row, row, pl.BlockSpec((N_HEADS, tm, LANES), lambda i: (0, nt - 1 - i, 0)), gate, gate, row,
                  sq, sq, pl.BlockSpec((d, LANES), lambda i: (0, 0))],
        out_specs=[row, gate, pl.BlockSpec((1, LANES), lambda i: (0, 0))],
        out_shape=[jax.ShapeDtypeStruct((t, d), F32), jax.ShapeDtypeStruct((t, LANES), BF16),
                   jax.ShapeDtypeStruct((1, LANES), F32)],
        scratch_shapes=[pltpu.VMEM((1, LANES), F32)],
        compiler_params=_params(("arbitrary",), VMEM_MID),
    )(dk, dv, dcs, dcq, logit, dh_other, wk, wv, wf)


def _proj(xh, gi, bi, w, name):
    t, k = xh.shape
    n = w.shape[1]
    tm = _row_tile(t)

    def body(x_ref, g_ref, b_ref, w_ref, o_ref):
        x = (x_ref[...] * g_ref[...] + b_ref[...]).astype(BF16)
        o_ref[...] = _dot(x, w_ref[...]).astype(BF16)

    vec = pl.BlockSpec((1, k), lambda i: (0, 0))
    return pl.pallas_call(
        body, name=name, grid=(t // tm,),
        in_specs=[pl.BlockSpec((tm, k), lambda i: (i, 0)), vec, vec, pl.BlockSpec((k, n), lambda i: (0, 0))],
        out_specs=pl.BlockSpec((tm, n), lambda i: (i, 0)),
        out_shape=jax.ShapeDtypeStruct((t, n), BF16),
        compiler_params=_params(("arbitrary",), VMEM_MID),
    )(xh, gi, bi, w)


def _add_proj_nt(base, y, w, name):
    t, n = y.shape
    k = w.shape[0]
    tm = _row_tile(t)

    def body(b_ref, y_ref, w_ref, o_ref):
        o_ref[...] = b_ref[...] + _dot_nt(y_ref[...].astype(BF16), w_ref[...])

    return pl.pallas_call(
        body, name=name, grid=(t // tm,),
        in_specs=[pl.BlockSpec((tm, k), lambda i: (i, 0)), pl.BlockSpec((tm, n), lambda i: (i, 0)),
                  pl.BlockSpec((k, n), lambda i: (0, 0))],
        out_specs=pl.BlockSpec((tm, k), lambda i: (i, 0)),
        out_shape=jax.ShapeDtypeStruct((t, k), F32),
        compiler_params=_params(("arbitrary",), VMEM_MID),
    )(base, y, w)


def _attn_out_fwd(ot, xh, gi, bi, w_o, go, bo, alpha, name):
    t, d = xh.shape
    tm = _row_tile(t)

    def body(ot_ref, xh_ref, gi_ref, bi_ref, wo_ref, go_ref, bo_ref, xo_ref, rs_ref, hb_ref):
        h = xh_ref[...] * gi_ref[...] + bi_ref[...]
        xhat, rstd = _ln_fwd(alpha * h + _dot_tn(ot_ref[...], wo_ref[...]))
        xo_ref[...] = xhat
        rs_ref[...] = rstd
        hb_ref[...] = (xhat * go_ref[...] + bo_ref[...]).astype(BF16).T

    row = pl.BlockSpec((tm, d), lambda i: (i, 0))
    col = pl.BlockSpec((d, tm), lambda i: (0, i))
    vec = pl.BlockSpec((1, d), lambda i: (0, 0))
    return pl.pallas_call(
        body, name=name, grid=(t // tm,),
        in_specs=[col, row, vec, vec, pl.BlockSpec((d, d), lambda i: (0, 0)), vec, vec],
        out_specs=[row, pl.BlockSpec((tm, 1), lambda i: (i, 0)), col],
        out_shape=[jax.ShapeDtypeStruct((t, d), F32), jax.ShapeDtypeStruct((t, 1), F32),
                   jax.ShapeDtypeStruct((d, t), BF16)],
        compiler_params=_params(("arbitrary",), VMEM_MID),
    )(ot, xh, gi, bi, w_o, go, bo)


def _attn_out_bwd(dh, xo, rs, go, ot, w_o, alpha, name):
    t, d = dh.shape
    tm = _row_tile(t)
    hd = d // N_HEADS

    def body(dh_ref, xo_ref, rs_ref, go_ref, ot_ref, wo_ref,
             dres_ref, dmix_ref, dot_ref, delta_ref, dgain_ref, dbias_ref):
        @pl.when(pl.program_id(0) == 0)
        def _():
            dgain_ref[...] = jnp.zeros_like(dgain_ref)
            dbias_ref[...] = jnp.zeros_like(dbias_ref)

        dz, dgp, dbp = _ln_bwd(dh_ref[...], xo_ref[...], rs_ref[...], go_ref[...])
        dgain_ref[...] += dgp
        dbias_ref[...] += dbp
        dres_ref[...] = alpha * dz
        dmixb = dz.astype(BF16)
        dmix_ref[...] = dmixb
        dot_t = _dot_nt(wo_ref[...], dmixb)
        dot_ref[...] = dot_t.astype(BF16)
        prod = dot_t * ot_ref[...].astype(F32)
        delta_ref[...] = jnp.sum(prod.reshape(N_HEADS, hd, tm), axis=1)

    row = pl.BlockSpec((tm, d), lambda i: (i, 0))
    vec = pl.BlockSpec((1, d), lambda i: (0, 0))
    col = pl.BlockSpec((d, tm), lambda i: (0, i))
    return pl.pallas_call(
        body, name=name, grid=(t // tm,),
        in_specs=[row, row, pl.BlockSpec((tm, 1), lambda i: (i, 0)), vec, col,
                  pl.BlockSpec((d, d), lambda i: (0, 0))],
        out_specs=[row, row, col, pl.BlockSpec((N_HEADS, tm), lambda i: (0, i)), vec, vec],
        out_shape=[jax.ShapeDtypeStruct((t, d), F32), jax.ShapeDtypeStruct((t, d), BF16),
                   jax.ShapeDtypeStruct((d, t), BF16), jax.ShapeDtypeStruct((N_HEADS, t), F32),
                   jax.ShapeDtypeStruct((1, d), F32), jax.ShapeDtypeStruct((1, d), F32)],
        compiler_params=_params(("arbitrary",), VMEM_MID),
    )(dh, xo, rs, go, ot, w_o)


def _scores_t(k, q, ct_ref, c_ref, h, i, j, tq, tk, scale, masked):
    st = _dot_nt(k, q) * scale
    sub = lax.broadcasted_iota(jnp.int32, (8, tq), 0)
    cq = jnp.sum(jnp.where(sub == h, ct_ref[...], 0.0), axis=0, keepdims=True)
    lane = lax.broadcasted_iota(jnp.int32, (tk, LANES), 1)
    ck = jnp.sum(jnp.where(lane == h, c_ref[...], 0.0), axis=1, keepdims=True)
    st = st + cq - ck
    if not masked:
        return st
    kpos = j * tk + lax.broadcasted_iota(jnp.int32, (tk, 1), 0)
    qpos = i * tq + lax.broadcasted_iota(jnp.int32, (1, tq), 1)
    return jnp.where((kpos <= qpos) & (kpos >= PAD), st, NEG_INF)


def _tri_pairs(n, by_row):
    if by_row:
        pairs = [(i, j) for i in range(n) for j in range(i + 1)]
    else:
        pairs = [(i, j) for j in range(n) for i in range(j, n)]
    return (jnp.asarray([p[0] for p in pairs], jnp.int32), jnp.asarray([p[1] for p in pairs], jnp.int32))


def _attn_fwd(q, k, v, c, ct, name):
    t, d = q.shape
    hd = d // N_HEADS
    tq = tk = _row_tile(t)
    nq = t // tq
    scale = 1.0 / math.sqrt(hd)

    def body(it_ref, jt_ref, q_ref, k_ref, v_ref, c_ref, ct_ref, ot_ref, lse_ref, m_s, l_s, acc):
        h, p_ = pl.program_id(0), pl.program_id(1)
        i, j = it_ref[p_], jt_ref[p_]

        @pl.when(j == 0)
        def _():
            m_s[...] = jnp.full_like(m_s, NEG_INF)
            l_s[...] = jnp.zeros_like(l_s)
            acc[...] = jnp.zeros_like(acc)

        def update(masked):
            st = _scores_t(k_ref[...], q_ref[...], ct_ref, c_ref, h, i, j, tq, tk, scale, masked)
            m_new = jnp.maximum(m_s[...], jnp.max(st, axis=0, keepdims=True))
            a = jnp.exp(m_s[...] - m_new)
            p = jnp.exp(st - m_new)
            l_s[...] = a * l_s[...] + jnp.sum(p, axis=0, keepdims=True)
            acc[...] = a * acc[...] + _dot_tn(v_ref[...], p.astype(BF16))
            m_s[...] = m_new

        edge = (j == i) | (j == 0)
        pl.when(edge)(lambda: update(True))
        pl.when(jnp.logical_not(edge))(lambda: update(False))

        @pl.when(j == i)
        def _():
            ot_ref[...] = (acc[...] / l_s[...]).astype(BF16)
            lse_ref[0] = m_s[...] + jnp.log(l_s[...])

    it, jt = _tri_pairs(nq, by_row=True)
    kv = pl.BlockSpec((tk, hd), lambda h, p, it, jt: (jt[p], h))
    return pl.pallas_call(
        body, name=name,
        grid_spec=pltpu.PrefetchScalarGridSpec(
            num_scalar_prefetch=2, grid=(N_HEADS, it.shape[0]),
            in_specs=[pl.BlockSpec((tq, hd), lambda h, p, it, jt: (it[p], h)), kv, kv,
                      pl.BlockSpec((tk, LANES), lambda h, p, it, jt: (jt[p], 0)),
                      pl.BlockSpec((8, tq), lambda h, p, it, jt: (0, it[p]))],
            out_specs=[pl.BlockSpec((hd, tq), lambda h, p, it, jt: (h, it[p])),
                       pl.BlockSpec((1, 1, tq), lambda h, p, it, jt: (h, 0, it[p]))],
            scratch_shapes=[pltpu.VMEM((1, tq), F32), pltpu.VMEM((1, tq), F32), pltpu.VMEM((hd, tq), F32)]),
        out_shape=[jax.ShapeDtypeStruct((d, t), BF16), jax.ShapeDtypeStruct((N_HEADS, 1, t), F32)],
        compiler_params=_params(("arbitrary", "arbitrary"), VMEM_MID),
    )(it, jt, q, k, v, c, ct)


def _attn_bwd(q, k, v, c, ct, lse, delta, dot_t, name, carry=()):
    t, d = q.shape
    hd = d // N_HEADS
    tq = tk = _row_tile(t)
    nq = t // tq
    scale = 1.0 / math.sqrt(hd)

    def body(it_ref, jt_ref, q_ref, k_ref, v_ref, c_ref, ct_ref, lse_ref, delta_ref, dot_ref,
             dq_ref, dk_ref, dv_ref, dcs_ref, drow_ref, dk_acc, dv_acc, dc_acc):
        h, p_ = pl.program_id(0), pl.program_id(1)
        i, j = it_ref[p_], jt_ref[p_]

        @pl.when(p_ == 0)
        def _():
            dq_ref[...] = jnp.zeros_like(dq_ref)
            drow_ref[...] = jnp.zeros_like(drow_ref)

        @pl.when(i == j)
        def _():
            dk_acc[...] = jnp.zeros_like(dk_acc)
            dv_acc[...] = jnp.zeros_like(dv_acc)
            dc_acc[...] = jnp.zeros_like(dc_acc)

        def update(masked):
            qv, kv_, vv = q_ref[...], k_ref[...], v_ref[...]
            st = _scores_t(kv_, qv, ct_ref, c_ref, h, i, j, tq, tk, scale, masked)
            p = jnp.exp(st - lse_ref[0])
            do_t = dot_ref[...]
            dp = _dot(vv, do_t)
            sub = lax.broadcasted_iota(jnp.int32, (8, tq), 0)
            dl = jnp.sum(jnp.where(sub == h, delta_ref[...], 0.0), axis=0, keepdims=True)
            ds = p * (dp - dl)
            dsb = ds.astype(BF16)
            dv_acc[...] += _dot_nt(p.astype(BF16), do_t)
            dk_acc[...] += _dot(dsb, qv) * scale
            rows = pl.ds(pl.multiple_of(i * tq, tq), tq)
            dq_ref[rows, :] += _dot_tn(dsb, kv_) * scale
            part = ds[:, 0:LANES]
            for g in range(1, tq // LANES):
                part = part + ds[:, g * LANES:(g + 1) * LANES]
            dc_acc[...] += part
            drow_ref[0, i] += jnp.broadcast_to(jnp.sum(ds, axis=0, keepdims=True), (8, tq))

        edge = (j == i) | (j == 0)
        pl.when(edge)(lambda: update(True))
        pl.when(jnp.logical_not(edge))(lambda: update(False))

        @pl.when(i == nq - 1)
        def _():
            dk_ref[...] = dk_acc[...].astype(BF16)
            dv_ref[...] = dv_acc[...].astype(BF16)
            dcs_ref[0] = -dc_acc[...]

    it, jt = _tri_pairs(nq, by_row=False)
    npairs = it.shape[0]
    kv = pl.BlockSpec((tk, hd), lambda h, p, it, jt: (jt[p], h))
    first = lambda: (pl.program_id(0) == 0) & (pl.program_id(1) == 0)
    last = lambda: (pl.program_id(0) == N_HEADS - 1) & (pl.program_id(1) == npairs - 1)
    return pl.pallas_call(
        _carried(body, 10, 5, carry, first, last), name=name,
        grid_spec=pltpu.PrefetchScalarGridSpec(
            num_scalar_prefetch=2, grid=(N_HEADS, npairs),
            in_specs=[pl.BlockSpec((tq, hd), lambda h, p, it, jt: (it[p], h)), kv, kv,
                      pl.BlockSpec((tk, LANES), lambda h, p, it, jt: (jt[p], 0)),
                      pl.BlockSpec((8, tq), lambda h, p, it, jt: (0, it[p])),
                      pl.BlockSpec((1, 1, tq), lambda h, p, it, jt: (h, 0, it[p])),
                      pl.BlockSpec((N_HEADS, tq), lambda h, p, it, jt: (0, it[p])),
                      pl.BlockSpec((hd, tq), lambda h, p, it, jt: (h, it[p]))] + [ANY] * len(carry),
            out_specs=[pl.BlockSpec((t, hd), lambda h, p, it, jt: (0, h)), kv, kv,
                       pl.BlockSpec((1, tk, LANES), lambda h, p, it, jt: (h, jt[p], 0)),
                       pl.BlockSpec((1, nq, 8, tq), lambda h, p, it, jt: (h, 0, 0, 0))] + [ANY] * len(carry),
            scratch_shapes=[pltpu.VMEM((tk, hd), F32), pltpu.VMEM((tk, hd), F32),
                            pltpu.VMEM((tk, LANES), F32)] + _carry_scratch(carry)),
        out_shape=[jax.ShapeDtypeStruct((t, d), F32), jax.ShapeDtypeStruct((t, d), BF16),
                   jax.ShapeDtypeStruct((t, d), BF16), jax.ShapeDtypeStruct((N_HEADS, t, LANES), F32),
                   jax.ShapeDtypeStruct((N_HEADS, nq, 8, tq), F32)] + _carry_shapes(carry),
        compiler_params=_params(("arbitrary", "arbitrary"), VMEM_MID),
    )(it, jt, q, k, v, c, ct, lse, delta, dot_t, *[a for _, a in carry])


def _loss_head(xh, g, b, target, name):
    t, d = xh.shape
    tm = LOSS_TILE
    nt = t // tm
    lead = ROW0 // tm

    def body(xh_ref, g_ref, b_ref, tg_ref, dh_ref, loss_ref, part):
        i = pl.program_id(0)

        @pl.when(i == 0)
        def _():
            part[...] = jnp.zeros_like(part)

        @pl.when(i < lead)
        def _():
            dh_ref[...] = jnp.zeros_like(dh_ref)

        @pl.when(i >= lead)
        def _():
            e = xh_ref[...] * g_ref[...] + b_ref[...] - tg_ref[...]
            dh_ref[...] = e * (1.0 / d)
            part[...] += jnp.sum(e * e, axis=0, keepdims=True)

        @pl.when(i == nt - 1)
        def _():
            loss_ref[...] = jnp.full((1, LANES), 0.5 / d, F32) * jnp.sum(part[...])

    return pl.pallas_call(
        body, name=name, grid=(nt,),
        in_specs=[pl.BlockSpec((tm, d), lambda i: (i, 0)), pl.BlockSpec((1, d), lambda i: (0, 0)),
                  pl.BlockSpec((1, d), lambda i: (0, 0)),
                  pl.BlockSpec((tm, d), lambda i: (jnp.maximum(i - lead, 0), 0))],
        out_specs=[pl.BlockSpec((tm, d), lambda i: (i, 0)), pl.BlockSpec((1, LANES), lambda i: (0, 0))],
        out_shape=[jax.ShapeDtypeStruct((t, d), F32), jax.ShapeDtypeStruct((1, LANES), F32)],
        scratch_shapes=[pltpu.VMEM((1, d), F32)],
        compiler_params=_params(("arbitrary",), VMEM_MID),
    )(xh, g, b, target)


def _adamw(w, g, m, v, name):
    r, c = w.shape
    tr = r
    for cand in (256, 128, 64, 32, 16, 8):
        if r % cand == 0 and r > cand:
            tr = cand
            break
    bc1 = 1.0 - ADAM_B1 ** ADAM_STEP
    bc2 = 1.0 - ADAM_B2 ** ADAM_STEP

    def body(w_ref, g_ref, m_ref, v_ref, d_ref, nm_ref, nv_ref):
        gg = g_ref[...]
        nm = ADAM_B1 * m_ref[...] + (1.0 - ADAM_B1) * gg
        nv = ADAM_B2 * v_ref[...] + (1.0 - ADAM_B2) * (gg * gg)
        d_ref[...] = -ADAM_LR * ((nm / bc1) / (jnp.sqrt(nv / bc2) + ADAM_EPS) + ADAM_WD * w_ref[...])
        nm_ref[...] = nm
        nv_ref[...] = nv

    blk = pl.BlockSpec((tr, c), lambda i: (i, 0))
    shp = jax.ShapeDtypeStruct((r, c), F32)
    return pl.pallas_call(
        body, name=name, grid=(r // tr,), in_specs=[blk] * 4, out_specs=[blk] * 3,
        out_shape=[shp] * 3, compiler_params=_params(("arbitrary",), VMEM_MID),
    )(w, g, m, v)


def _sum_sources(r, name):
    n, rows, c = r.shape
    tr = next(cand for cand in range(min(rows, SUM_ROWS_MAX), 0, -BF16_ROWS) if rows % cand == 0)

    def body(r_ref, o_ref):
        acc = r_ref[0].astype(F32)
        for s in range(1, n):
            acc = acc + r_ref[s].astype(F32)
        o_ref[...] = acc

    return pl.pallas_call(
        body, name=name, grid=(rows // tr,),
        in_specs=[pl.BlockSpec((n, tr, c), lambda i: (0, i, 0))],
        out_specs=pl.BlockSpec((tr, c), lambda i: (i, 0)),
        out_shape=jax.ShapeDtypeStruct((rows, c), F32),
        compiler_params=_params(("arbitrary",), VMEM_MID),
    )(r)


def _all_gather(x, name):
    rows, cols = x.shape

    def body(x_ref, out_ref, send_sems, recv_sems, local_sem):
        mx, my, mc = lax.axis_index("x"), lax.axis_index("y"), lax.axis_index("c")
        me, sibling = (mx, my, mc), (mx, my, 1 - mc)
        chips = [(1 - mx, my), (mx, 1 - my), (1 - mx, 1 - my)]

        def slot(px, py, pc):
            return out_ref.at[4 * px + 2 * py + pc]

        def copy(k, block, to, src=None):
            return pltpu.make_async_remote_copy(
                src_ref=slot(*block) if src is None else src, dst_ref=slot(*block),
                send_sem=send_sems.at[k], recv_sem=recv_sems.at[k],
                device_id=to, device_id_type=MESH)

        mine = pltpu.make_async_copy(x_ref, slot(*me), local_sem)
        mine.start()
        first = [copy(0, me, sibling, src=x_ref)]
        first += [copy(1 + n, me, (*chip, mc), src=x_ref) for n, chip in enumerate(chips)]
        for cp in first:
            cp.start()
        passed = [copy(4 + n, (*chip, mc), sibling) for n, chip in enumerate(chips)]
        for n, chip in enumerate(chips):
            copy(1 + n, (*chip, mc), me).wait_recv()
            passed[n].start()
        copy(0, sibling, me).wait_recv()
        for n, chip in enumerate(chips):
            copy(4 + n, (*chip, 1 - mc), me).wait_recv()
        for cp in first + passed:
            cp.wait_send()
        mine.wait()

    return pl.pallas_call(
        body, name=name, in_specs=[ANY], out_specs=ANY,
        out_shape=jax.ShapeDtypeStruct((N_DEV, rows, cols), x.dtype),
        scratch_shapes=[pltpu.SemaphoreType.DMA((7,)), pltpu.SemaphoreType.DMA((7,)),
                        pltpu.SemaphoreType.DMA],
    )(x)


def _pack_rows(parts, width, mult, lead=0):
    out = []
    for a in parts:
        head = a.shape[:lead]
        flat = a.reshape(head + (-1,))
        padn = (-flat.shape[-1]) % (width * mult)
        if padn:
            flat = jnp.pad(flat, [(0, 0)] * lead + [(0, padn)])
        out.append(flat.reshape(head + (-1, width)))
    return jnp.concatenate(out, axis=lead)


def _rows_of(shape, width, mult):
    n = math.prod(shape)
    per = width * mult
    return ((n + per - 1) // per) * mult


def _unpack_rows(buf, shapes, width, mult):
    lead = buf.shape[:-2]
    out, off = [], 0
    for shp in shapes:
        r = _rows_of(shp, width, mult)
        flat = buf[..., off:off + r, :].reshape(lead + (r * width,))
        out.append(flat[..., :math.prod(shp)].reshape(lead + tuple(shp)))
        off += r
    return out


def _cols_from_devices(g):
    nd = g.ndim
    perm = tuple(range(1, nd - 1)) + (0, nd - 1)
    t = jnp.transpose(g, perm)
    return t.reshape(t.shape[:-2] + (t.shape[-2] * t.shape[-1],))


def _cols_to_devices(a):
    c = a.shape[-1] // N_DEV
    t = a.reshape(a.shape[:-1] + (N_DEV, c))
    nd = t.ndim
    perm = (nd - 2,) + tuple(range(0, nd - 2)) + (nd - 1,)
    return jnp.transpose(t, perm)


WIDTH = 1024


def kernel(x, meta, ffn1_wg, ffn1_wu, ffn1_wd, ffn2_wg, ffn2_wu, ffn2_wd, ln_gain, ln_bias, conv_w_in, conv_w, conv_w_out, kv_w, f_bias, attn_w_q, attn_w_o, loss_target, m_meta, m_ffn1_wg, m_ffn1_wu, m_ffn1_wd, m_ffn2_wg, m_ffn2_wu, m_ffn2_wd, m_ln_gain, m_ln_bias, m_conv_w_in, m_conv_w, m_conv_w_out, m_kv_w, m_f_bias, m_attn_w_q, m_attn_w_o, v_meta, v_ffn1_wg, v_ffn1_wu, v_ffn1_wd, v_ffn2_wg, v_ffn2_wu, v_ffn2_wd, v_ln_gain, v_ln_bias, v_conv_w_in, v_conv_w, v_conv_w_out, v_kv_w, v_f_bias, v_attn_w_q, v_attn_w_o):
    depth = ln_gain.shape[0]
    alpha = float((2 * depth) ** 0.25)
    d = x.shape[-1]
    seq = x.shape[1]
    t = ROW0 + seq
    fsh = ffn1_wg.shape[-1]
    f = fsh * N_DEV
    nc = f // FFN_CHUNK
    me = 4 * lax.axis_index("x") + 2 * lax.axis_index("y") + lax.axis_index("c")

    def pack16(parts, lead=0):
        return _pack_rows([a.astype(BF16) for a in parts], WIDTH, BF16_ROWS, lead)

    def unpack16(buf, parts):
        return _unpack_rows(buf, [a.shape for a in parts], WIDTH, BF16_ROWS)

    grp0 = [ffn1_wg[0], ffn1_wu[0], ffn1_wd[0], conv_w_in[0], conv_w_out[0]]
    grp1 = [ffn2_wg[0], ffn2_wu[0], ffn2_wd[0]]
    grp2 = [kv_w, ffn1_wg[1], ffn1_wu[1], ffn1_wd[1]]
    grp3 = [attn_w_q[0], attn_w_o[0], ffn2_wg[1], ffn2_wu[1], ffn2_wd[1]]
    small = [meta, ln_gain, ln_bias, conv_w]
    small_shapes = [a.shape for a in small]
    gat0 = _all_gather(pack16(grp0), "ag_first")
    gsmall = _all_gather(_pack_rows(small, WIDTH, F32_ROWS), "ag_small")
    gmeta, ggain, gbias, gcw = _unpack_rows(gsmall, small_shapes, WIDTH, F32_ROWS)

    def ffn_chunks(gg, gu, gd):
        up = lambda g: jnp.transpose(_cols_from_devices(g).reshape(d, nc, FFN_CHUNK), (1, 0, 2))
        return up(gg), up(gu), gd.reshape(nc, FFN_CHUNK, d)

    g1g, g1u, g1d, gcin, gcout = unpack16(gat0, grp0)
    w_in = _cols_from_devices(gcin)
    w_out = gcout.reshape(d, d)
    fb = jnp.pad(f_bias, (0, LANES - N_HEADS)).reshape(1, LANES)
    meta_f = _cols_from_devices(gmeta)
    gain_f = _cols_from_devices(ggain)
    bias_f = _cols_from_devices(gbias)
    cw_f = _cols_from_devices(gcw)[0]

    def gb(l, n):
        return gain_f[l, n].reshape(1, d), bias_f[l, n].reshape(1, d)

    ones = jnp.ones((1, d), F32)
    zeros = jnp.zeros((1, d), F32)

    h0 = jnp.concatenate([jnp.zeros((PAD, d), F32), meta_f, x[0]], axis=0)
    hb0 = _cast_t(h0, "h0_bf16_t")

    w1 = ffn_chunks(g1g, g1u, g1d)
    g00, b00 = gb(0, 0)
    xh1, rs1, hb1, gg1, uu1, gat1 = _ffn_fwd(h0, ones, zeros, *w1, g00, b00, alpha, "ffn_fwd_0a",
                                              carry=[(True, pack16(grp1))])
    g01, b01 = gb(0, 1)
    xh2, rs2, hb2, pp, mb = _conv_fwd(xh1, g00, b00, w_in, cw_f, w_out, g01, b01, alpha, "conv_fwd")
    w2 = ffn_chunks(*unpack16(gat1, grp1))
    g02, b02 = gb(0, 2)
    xh3, rs3, hb3, gg3, uu3, gat2 = _ffn_fwd(xh2, g01, b01, *w2, g02, b02, alpha, "ffn_fwd_0b",
                                              carry=[(True, pack16(grp2))])
    gkv, g3g, g3u, g3d = unpack16(gat2, grp2)
    kvw = _cols_from_devices(gkv)
    wk, wv = kvw[:, :d], kvw[:, d:2 * d]
    wf = jnp.pad(kvw[:, 2 * d:], ((0, 0), (0, LANES - N_HEADS)))
    kk, vv, logit, cc, cct = _kv_fwd(xh3, g02, b02, wk, wv, wf, fb, "kv_fwd")

    w3 = ffn_chunks(g3g, g3u, g3d)
    g10, b10 = gb(1, 0)
    xh4, rs4, hb4, gg4, uu4, gat3 = _ffn_fwd(xh3, g02, b02, *w3, g10, b10, alpha, "ffn_fwd_1a",
                                              carry=[(True, pack16(grp3))])
    gwq, gwo, g4g, g4u, g4d = unpack16(gat3, grp3)
    w_q, w_o = gwq.reshape(d, d), gwo.reshape(d, d)
    qq = _proj(xh4, g10, b10, w_q, "q_proj")
    ot, lse = _attn_fwd(qq, kk, vv, cc, cct, "attn_fwd")
    g11, b11 = gb(1, 1)
    xh5, rs5, hb5 = _attn_out_fwd(ot, xh4, g10, b10, w_o, g11, b11, alpha, "attn_out_fwd")
    w4 = ffn_chunks(g4g, g4u, g4d)
    g12, b12 = gb(1, 2)
    xh6, rs6, _, gg6, uu6 = _ffn_fwd(xh5, g11, b11, *w4, g12, b12, alpha, "ffn_fwd_1b")

    dh6, loss_l = _loss_head(xh6, g12, b12, loss_target[0], "loss_head")
    loss = lax.psum(loss_l[0, 0], ("x", "y", "c"))

    dgain = [[None] * 3 for _ in range(depth)]
    dbias = [[None] * 3 for _ in range(depth)]

    def ffn_send(dg_, du_, ddt):
        return [_cols_to_devices(dg_), _cols_to_devices(du_), ddt.T.reshape(N_DEV, fsh, d)]

    def received(landed, parts, name):
        return _unpack_rows(_sum_sources(landed, name), [a.shape for a in parts], WIDTH, BF16_ROWS)

    dh5, do6, dg6, du6, a6, dgain[1][2], dbias[1][2] = _ffn_bwd(dh6, xh6, rs6, g12, gg6, uu6, *w4, alpha, "ffn_bwd_1b")
    dw4g, dw4u = _wgrad(hb5, [dg6, du6], "wgrad_up_1b")
    (dw4dt,) = _wgrad(do6, [a6], "wgrad_down_1b")

    dres4, dmix5, dot_t, delta, dgain[1][1], dbias[1][1] = _attn_out_bwd(dh5, xh5, rs5, g11, ot, w_o, alpha, "attn_out_bwd")
    (dwo,) = _wgrad(ot, [dmix5], "wgrad_wo")
    out3 = [ffn2_wg[1], ffn2_wu[1], ffn2_wd[1], attn_w_o[0]]
    send3 = pack16(ffn_send(dw4g, dw4u, dw4dt) + [dwo.reshape(N_DEV, d // N_DEV, d)], lead=1)
    dq, dkk, dvv, dcs, drow, land3 = _attn_bwd(qq, kk, vv, cc, cct, lse, delta, dot_t, "attn_bwd",
                                               carry=[(False, send3)])
    dh4 = _add_proj_nt(dres4, dq, w_q, "q_bwd")
    (dwq,) = _wgrad(hb4, [dq], "wgrad_wq")

    dh3a, do4, dg4, du4, a4, dgain[1][0], dbias[1][0] = _ffn_bwd(dh4, xh4, rs4, g10, gg4, uu4, *w3, alpha, "ffn_bwd_1a")
    dw3g, dw3u = _wgrad(hb3, [dg4, du4], "wgrad_up_1a")
    (dw3dt,) = _wgrad(do4, [a4], "wgrad_down_1a")

    dcq = jnp.pad(drow[:, :, 0, :].reshape(N_HEADS, t).T, ((0, 0), (0, LANES - N_HEADS)))
    dh3, dlogit, dfb = _kv_bwd(dkk, dvv, dcs, dcq, logit, dh3a, wk, wv, wf, "kv_bwd")
    dwk, dwv = _wgrad(hb3, [dkk, dvv], "wgrad_kv")
    (dwf,) = _wgrad(hb3, [dlogit], "wgrad_f")
    dkv = jnp.concatenate([dwk, dwv, dwf[:, :N_HEADS]], axis=1)

    out2 = [attn_w_q[0], ffn1_wg[1], ffn1_wu[1], ffn1_wd[1], kv_w]
    send2 = pack16([dwq.reshape(N_DEV, d // N_DEV, d)] + ffn_send(dw3g, dw3u, dw3dt) + [_cols_to_devices(dkv)], lead=1)
    dh2, do3, dg3, du3, a3, dgain[0][2], dbias[0][2], land2 = _ffn_bwd(
        dh3, xh3, rs3, g02, gg3, uu3, *w2, alpha, "ffn_bwd_0b", carry=[(False, send2)])
    dw2g, dw2u = _wgrad(hb2, [dg3, du3], "wgrad_up_0b")
    (dw2dt,) = _wgrad(do3, [a3], "wgrad_down_0b")

    dh1, dmix2, dpp, dcw, dgain[0][1], dbias[0][1] = _conv_bwd(dh2, xh2, rs2, g01, pp, cw_f, w_in, w_out, alpha, "conv_bwd")
    (dwin,) = _wgrad(hb1, [dpp], "wgrad_conv_in")
    (dwout,) = _wgrad(mb, [dmix2], "wgrad_conv_out")

    out1 = [ffn2_wg[0], ffn2_wu[0], ffn2_wd[0], conv_w_in[0], conv_w_out[0]]
    send1 = pack16(ffn_send(dw2g, dw2u, dw2dt) + [_cols_to_devices(dwin), dwout.reshape(N_DEV, d // N_DEV, d)], lead=1)
    dh0, do1, dg1, du1, a1, dgain[0][0], dbias[0][0], land1 = _ffn_bwd(
        dh1, xh1, rs1, g00, gg1, uu1, *w1, alpha, "ffn_bwd_0a", carry=[(False, send1)])
    dw1g, dw1u = _wgrad(hb0, [dg1, du1], "wgrad_up_0a")
    (dw1dt,) = _wgrad(do1, [a1], "wgrad_down_0a")
    out0 = [ffn1_wg[0], ffn1_wu[0], ffn1_wd[0]]
    land0 = _exchange(pack16(ffn_send(dw1g, dw1u, dw1dt), lead=1), "rs_last")

    grad_x = dh0[ROW0:].reshape(1, seq, d)

    r2g, r2u, r2d, rwo = received(land3, out3, "rs_sum_3")
    rwq, r3g, r3u, r3d, rkv = received(land2, out2, "rs_sum_2")
    r1g_, r1u_, r1d_, rcin, rcout = received(land1, out1, "rs_sum_1")
    r0g, r0u, r0d = received(land0, out0, "rs_sum_0")

    dmeta = dh0[PAD:ROW0]
    dgain_f = jnp.stack([jnp.concatenate(r, axis=0) for r in dgain])
    dbias_f = jnp.stack([jnp.concatenate(r, axis=0) for r in dbias])
    small_full = [dmeta, dgain_f, dbias_f, dcw[None], dfb]
    small_full_shapes = [a.shape for a in small_full]
    rsmall = _sum_sources(_all_gather(_pack_rows(small_full, WIDTH, F32_ROWS), "ag_small_grads"), "small_sum")
    smeta, sgain, sbias, scw, sfb = _unpack_rows(rsmall, small_full_shapes, WIDTH, F32_ROWS)
    csh = d // N_DEV

    def my_cols(a):
        return lax.dynamic_slice_in_dim(a, me * csh, csh, axis=a.ndim - 1)

    grads = {
        "meta": my_cols(smeta), "ffn1_wg": jnp.stack([r0g, r3g]), "ffn1_wu": jnp.stack([r0u, r3u]),
        "ffn1_wd": jnp.stack([r0d, r3d]), "ffn2_wg": jnp.stack([r1g_, r2g]), "ffn2_wu": jnp.stack([r1u_, r2u]),
        "ffn2_wd": jnp.stack([r1d_, r2d]), "ln_gain": my_cols(sgain), "ln_bias": my_cols(sbias),
        "conv_w_in": rcin[None], "conv_w": my_cols(scw), "conv_w_out": rcout[None], "kv_w": rkv,
        "f_bias": sfb[0, :N_HEADS], "attn_w_q": rwq[None], "attn_w_o": rwo[None],
    }
    weights = dict(meta=meta, ffn1_wg=ffn1_wg, ffn1_wu=ffn1_wu, ffn1_wd=ffn1_wd, ffn2_wg=ffn2_wg,
                   ffn2_wu=ffn2_wu, ffn2_wd=ffn2_wd, ln_gain=ln_gain, ln_bias=ln_bias,
                   conv_w_in=conv_w_in, conv_w=conv_w, conv_w_out=conv_w_out, kv_w=kv_w,
                   f_bias=f_bias, attn_w_q=attn_w_q, attn_w_o=attn_w_o)
    moms = dict(meta=(m_meta, v_meta), ffn1_wg=(m_ffn1_wg, v_ffn1_wg), ffn1_wu=(m_ffn1_wu, v_ffn1_wu),
                ffn1_wd=(m_ffn1_wd, v_ffn1_wd), ffn2_wg=(m_ffn2_wg, v_ffn2_wg), ffn2_wu=(m_ffn2_wu, v_ffn2_wu),
                ffn2_wd=(m_ffn2_wd, v_ffn2_wd), ln_gain=(m_ln_gain, v_ln_gain), ln_bias=(m_ln_bias, v_ln_bias),
                conv_w_in=(m_conv_w_in, v_conv_w_in), conv_w=(m_conv_w, v_conv_w),
                conv_w_out=(m_conv_w_out, v_conv_w_out), kv_w=(m_kv_w, v_kv_w), f_bias=(m_f_bias, v_f_bias),
                attn_w_q=(m_attn_w_q, v_attn_w_q), attn_w_o=(m_attn_w_o, v_attn_w_o))

    names = list(weights)
    g_out, d_out, m_out, v_out = [], [], [], []
    for n in names:
        w = weights[n]
        shp = w.shape
        two = (1, shp[0]) if w.ndim == 1 else (math.prod(shp[:-1]), shp[-1])
        g = grads[n].reshape(shp)
        mm, vv_ = moms[n]
        dl, nm, nv = _adamw(w.reshape(two), g.reshape(two), mm.reshape(two), vv_.reshape(two), "adamw_" + n)
        g_out.append(g)
        d_out.append(dl.reshape(shp))
        m_out.append(nm.reshape(shp))
        v_out.append(nv.reshape(shp))
    return (loss, grad_x, *g_out, *d_out, *m_out, *v_out)
```

```python
import functools
import math

import jax
import jax.numpy as jnp
from jax import lax
from jax.experimental import pallas as pl
from jax.experimental.pallas import tpu as pltpu

F32 = jnp.float32
BF16 = jnp.bfloat16

N_DEV = 8
N_HEADS = 8
N_META = 16
PAD = 112
ROW0 = PAD + N_META
LN_EPS = 1e-5
NEG_INF = -1e30
LANES = 128
MXU_COLS = 256
FFN_FWD_CHUNKS = 6
FFN_BWD_CHUNKS = 4

ADAM_LR = 0.001
ADAM_B1 = 0.9
ADAM_B2 = 0.999
ADAM_EPS = 1e-08
ADAM_WD = 0.01
ADAM_STEP = 10

ROW_TILES = (640, 128)
LOSS_TILE = 128
BF16_ROWS = 16
F32_ROWS = 8
SUM_ROWS_MAX = 768
VMEM_BIG = 56 << 20
VMEM_MID = 40 << 20

ANY = pl.BlockSpec(memory_space=pl.ANY)
MESH = pl.DeviceIdType.MESH


def _row_tile(t):
    for c in ROW_TILES:
        if t % c == 0:
            return c
    raise ValueError(f"no row tile for {t}")


def _dot(a, b):
    return jnp.dot(a, b, preferred_element_type=F32)


def _dot_nt(a, b):
    return lax.dot_general(a, b, (((1,), (1,)), ((), ())), preferred_element_type=F32)


def _dot_tn(a, b):
    return lax.dot_general(a, b, (((0,), (0,)), ((), ())), preferred_element_type=F32)


def _params(sem, vmem):
    return pltpu.CompilerParams(dimension_semantics=sem, vmem_limit_bytes=vmem)


def _ln_fwd(z):
    mu = jnp.mean(z, axis=-1, keepdims=True)
    zc = z - mu
    var = jnp.mean(zc * zc, axis=-1, keepdims=True)
    rstd = lax.rsqrt(var + LN_EPS)
    return zc * rstd, rstd


def _ln_bwd(dh, xhat, rstd, gain):
    dxh = dh * gain
    m1 = jnp.mean(dxh, axis=-1, keepdims=True)
    m2 = jnp.mean(dxh * xhat, axis=-1, keepdims=True)
    dz = rstd * (dxh - m1 - xhat * m2)
    return dz, jnp.sum(dh * xhat, axis=0, keepdims=True), jnp.sum(dh, axis=0, keepdims=True)


def _load_resident(pairs, sems):
    cps = [pltpu.make_async_copy(src, dst, sems.at[k]) for k, (src, dst) in enumerate(pairs)]
    for cp in cps:
        cp.start()
    for cp in cps:
        cp.wait()


def _peer_ids():
    mx, my, mc = lax.axis_index("x"), lax.axis_index("y"), lax.axis_index("c")
    peers = []
    for kk in range(1, N_DEV):
        px = 1 - mx if (kk >> 2) & 1 else mx
        py = 1 - my if (kk >> 1) & 1 else my
        pc = 1 - mc if kk & 1 else mc
        peers.append(((px, py, pc), 4 * px + 2 * py + pc))
    return 4 * mx + 2 * my + mc, peers


def _exchange_copies(jobs, send_sems, recv_sems, local_sems, starting):
    me_id, peers = _peer_ids()
    for n, (gather, src, dst) in enumerate(jobs):
        own = pltpu.make_async_copy(src if gather else src.at[me_id], dst.at[me_id], local_sems.at[n])
        own.start() if starting else own.wait()
        for k, (dev, pid) in enumerate(peers):
            sem = (N_DEV - 1) * n + k
            out = src if gather else src.at[pid]
            send = pltpu.make_async_remote_copy(
                src_ref=out, dst_ref=dst.at[me_id], send_sem=send_sems.at[sem], recv_sem=recv_sems.at[sem],
                device_id=dev, device_id_type=MESH)
            if starting:
                send.start()
            else:
                pltpu.make_async_remote_copy(
                    src_ref=out, dst_ref=dst.at[pid], send_sem=send_sems.at[sem], recv_sem=recv_sems.at[sem],
                    device_id=dev, device_id_type=MESH).wait_recv()
                send.wait_send()


def _carried(body, n_in, n_out, carry, first, last):
    nj = len(carry)
    if nj == 0:
        return body

    def wrapped(*refs):
        ins, srcs = refs[:n_in], refs[n_in:n_in + nj]
        outs = refs[n_in + nj:n_in + nj + n_out]
        dsts = refs[n_in + nj + n_out:n_in + 2 * nj + n_out]
        scratch, sems = refs[n_in + 2 * nj + n_out:-3], refs[-3:]
        jobs = [(g, s, r) for (g, _), s, r in zip(carry, srcs, dsts)]

        @pl.when(first())
        def _():
            _exchange_copies(jobs, *sems, starting=True)

        body(*ins, *outs, *scratch)

        @pl.when(last())
        def _():
            _exchange_copies(jobs, *sems, starting=False)

    return wrapped


def _carry_shapes(carry):
    return [jax.ShapeDtypeStruct((N_DEV,) + a.shape if g else a.shape, a.dtype) for g, a in carry]


def _carry_scratch(carry):
    if not carry:
        return []
    n = len(carry)
    return [pltpu.SemaphoreType.DMA(((N_DEV - 1) * n,)), pltpu.SemaphoreType.DMA(((N_DEV - 1) * n,)),
            pltpu.SemaphoreType.DMA((n,))]


def _exchange(s, name):
    carry = [(False, s)]

    def body(s_ref, r_ref, send_sems, recv_sems, local_sems):
        jobs = [(False, s_ref, r_ref)]
        _exchange_copies(jobs, send_sems, recv_sems, local_sems, starting=True)
        _exchange_copies(jobs, send_sems, recv_sems, local_sems, starting=False)

    return pl.pallas_call(
        body, name=name, in_specs=[ANY], out_specs=ANY, out_shape=_carry_shapes(carry)[0],
        scratch_shapes=_carry_scratch(carry),
    )(s)


def _ffn_fwd(xh, gi, bi, wg, wu, wd, go, bo, alpha, name, carry=()):
    t, d = xh.shape
    nch, _, fc = wg.shape
    f = nch * fc
    per = min(FFN_FWD_CHUNKS, nch)
    nc = -(-nch // per)
    tm = _row_tile(t)
    nt = t // tm

    def body(xh_ref, gi_ref, bi_ref, wg_hbm, wu_hbm, wd_hbm, go_ref, bo_ref,
             xo_ref, rs_ref, hb_ref, g_ref, u_ref,
             wg_v, wu_v, wd_v, acc, hbs, sems):
        i = pl.program_id(0)
        c = pl.program_id(1)

        @pl.when((i == 0) & (c == 0))
        def _():
            _load_resident([(wg_hbm, wg_v), (wu_hbm, wu_v), (wd_hbm, wd_v)], sems)

        @pl.when(c == 0)
        def _():
            h = xh_ref[...] * gi_ref[...] + bi_ref[...]
            hbs[...] = h.astype(BF16)
            acc[...] = jnp.zeros_like(acc)

        def chunk(k):
            ck = c * per + k
            cols = slice(k * fc, (k + 1) * fc)
            hb = hbs[...]
            g = _dot(hb, wg_v[ck])
            u = _dot(hb, wu_v[ck])
            a = (g * jax.nn.sigmoid(g)) * u
            g_ref[:, cols] = g.astype(BF16)
            u_ref[:, cols] = u.astype(BF16)
            acc[...] += _dot(a.astype(BF16), wd_v[ck])

        for k in range(per):
            if (nc - 1) * per + k < nch:
                chunk(k)
            else:
                pl.when(c * per + k < nch)(functools.partial(chunk, k))

        @pl.when(c == nc - 1)
        def _():
            h = xh_ref[...] * gi_ref[...] + bi_ref[...]
            xhat, rstd = _ln_fwd(alpha * h + 0.5 * acc[...])
            xo_ref[...] = xhat
            rs_ref[...] = rstd
            hb_ref[...] = (xhat * go_ref[...] + bo_ref[...]).astype(BF16).T

    row = pl.BlockSpec((tm, d), lambda i, c: (i, 0))
    vec = pl.BlockSpec((1, d), lambda i, c: (0, 0))
    chunk = pl.BlockSpec((tm, per * fc), lambda i, c: (i, c))
    first = lambda: (pl.program_id(0) == 0) & (pl.program_id(1) == 0)
    last = lambda: (pl.program_id(0) == nt - 1) & (pl.program_id(1) == nc - 1)
    return pl.pallas_call(
        _carried(body, 8, 5, carry, first, last), name=name, grid=(nt, nc),
        in_specs=[row, vec, vec, ANY, ANY, ANY, vec, vec] + [ANY] * len(carry),
        out_specs=[row, pl.BlockSpec((tm, 1), lambda i, c: (i, 0)),
                   pl.BlockSpec((d, tm), lambda i, c: (0, i)), chunk, chunk] + [ANY] * len(carry),
        out_shape=[jax.ShapeDtypeStruct((t, d), F32), jax.ShapeDtypeStruct((t, 1), F32),
                   jax.ShapeDtypeStruct((d, t), BF16), jax.ShapeDtypeStruct((t, f), BF16),
                   jax.ShapeDtypeStruct((t, f), BF16)] + _carry_shapes(carry),
        scratch_shapes=[pltpu.VMEM((nch, d, fc), BF16), pltpu.VMEM((nch, d, fc), BF16),
                        pltpu.VMEM((nch, fc, d), BF16), pltpu.VMEM((tm, d), F32),
                        pltpu.VMEM((tm, d), BF16), pltpu.SemaphoreType.DMA((3,))] + _carry_scratch(carry),
        compiler_params=_params(("arbitrary", "arbitrary"), VMEM_BIG),
    )(xh, gi, bi, wg, wu, wd, go, bo, *[a for _, a in carry])


def _ffn_bwd(dh, xo, rs, go, gs, us, wg, wu, wd, alpha, name, carry=()):
    t, d = dh.shape
    nch, _, fc = wg.shape
    f = nch * fc
    per = min(FFN_BWD_CHUNKS, nch)
    nc = -(-nch // per)
    tm = _row_tile(t)
    nt = t // tm

    def body(dh_ref, xo_ref, rs_ref, go_ref, g_ref, u_ref, wg_hbm, wu_hbm, wd_hbm,
             dhin_ref, dot_ref, dg_ref, du_ref, a_ref, dgain_ref, dbias_ref,
             wg_v, wu_v, wd_v, do_ref, sems):
        i = pl.program_id(0)
        c = pl.program_id(1)

        @pl.when((i == 0) & (c == 0))
        def _():
            _load_resident([(wg_hbm, wg_v), (wu_hbm, wu_v), (wd_hbm, wd_v)], sems)
            dgain_ref[...] = jnp.zeros_like(dgain_ref)
            dbias_ref[...] = jnp.zeros_like(dbias_ref)

        @pl.when(c == 0)
        def _():
            dz, dgp, dbp = _ln_bwd(dh_ref[...], xo_ref[...], rs_ref[...], go_ref[...])
            dgain_ref[...] += dgp
            dbias_ref[...] += dbp
            dob = (0.5 * dz).astype(BF16)
            do_ref[...] = dob
            dot_ref[...] = dob.T
            dhin_ref[...] = alpha * dz

        def chunk(k):
            ck = c * per + k
            cols = slice(k * fc, (k + 1) * fc)
            g = g_ref[:, cols].astype(F32)
            u = u_ref[:, cols].astype(F32)
            sg = jax.nn.sigmoid(g)
            sl = g * sg
            da = _dot_nt(do_ref[...], wd_v[ck])
            dgb = (da * u * (sg * (1.0 + g * (1.0 - sg)))).astype(BF16)
            dub = (da * sl).astype(BF16)
            a_ref[:, cols] = (sl * u).astype(BF16)
            dg_ref[:, cols] = dgb
            du_ref[:, cols] = dub
            dhin_ref[...] += _dot_nt(dgb, wg_v[ck]) + _dot_nt(dub, wu_v[ck])

        for k in range(per):
            if (nc - 1) * per + k < nch:
                chunk(k)
            else:
                pl.when(c * per + k < nch)(functools.partial(chunk, k))

    row = pl.BlockSpec((tm, d), lambda i, c: (i, 0))
    vec = pl.BlockSpec((1, d), lambda i, c: (0, 0))
    chunk = pl.BlockSpec((tm, per * fc), lambda i, c: (i, c))
    first = lambda: (pl.program_id(0) == 0) & (pl.program_id(1) == 0)
    last = lambda: (pl.program_id(0) == nt - 1) & (pl.program_id(1) == nc - 1)
    return pl.pallas_call(
        _carried(body, 9, 7, carry, first, last), name=name, grid=(nt, nc),
        in_specs=[row, row, pl.BlockSpec((tm, 1), lambda i, c: (i, 0)), vec, chunk, chunk,
                  ANY, ANY, ANY] + [ANY] * len(carry),
        out_specs=[row, pl.BlockSpec((d, tm), lambda i, c: (0, i)), chunk, chunk, chunk, vec, vec]
                  + [ANY] * len(carry),
        out_shape=[jax.ShapeDtypeStruct((t, d), F32), jax.ShapeDtypeStruct((d, t), BF16),
                   jax.ShapeDtypeStruct((t, f), BF16), jax.ShapeDtypeStruct((t, f), BF16),
                   jax.ShapeDtypeStruct((t, f), BF16), jax.ShapeDtypeStruct((1, d), F32),
                   jax.ShapeDtypeStruct((1, d), F32)] + _carry_shapes(carry),
        scratch_shapes=[pltpu.VMEM((nch, d, fc), BF16), pltpu.VMEM((nch, d, fc), BF16),
                        pltpu.VMEM((nch, fc, d), BF16), pltpu.VMEM((tm, d), BF16),
                        pltpu.SemaphoreType.DMA((3,))] + _carry_scratch(carry),
        compiler_params=_params(("arbitrary", "arbitrary"), VMEM_BIG),
    )(dh, xo, rs, go, gs, us, wg, wu, wd, *[a for _, a in carry])


def _wgrad(xt, ys, name):
    m, t = xt.shape
    n = ys[0].shape[1]
    tn = min(n, MXU_COLS)
    ny = len(ys)

    def body(*refs):
        x_hbm = refs[0]
        y_refs = refs[1:1 + ny]
        o_refs = refs[1 + ny:1 + 2 * ny]
        xv, sems = refs[1 + 2 * ny:]

        @pl.when(pl.program_id(0) == 0)
        def _():
            _load_resident([(x_hbm, xv)], sems)

        for y_ref, o_ref in zip(y_refs, o_refs):
            o_ref[...] = _dot(xv[...], y_ref[...].astype(BF16))

    return pl.pallas_call(
        body, name=name, grid=(n // tn,),
        in_specs=[ANY] + [pl.BlockSpec((t, tn), lambda c: (0, c)) for _ in ys],
        out_specs=[pl.BlockSpec((m, tn), lambda c: (0, c)) for _ in ys],
        out_shape=[jax.ShapeDtypeStruct((m, n), F32) for _ in ys],
        scratch_shapes=[pltpu.VMEM((m, t), BF16), pltpu.SemaphoreType.DMA((1,))],
        compiler_params=_params(("arbitrary",), VMEM_BIG),
    )(xt, *ys)


def _cast_t(h, name):
    t, d = h.shape
    tm = _row_tile(t)

    def body(h_ref, o_ref):
        o_ref[...] = h_ref[...].astype(BF16).T

    return pl.pallas_call(
        body, name=name, grid=(t // tm,),
        in_specs=[pl.BlockSpec((tm, d), lambda i: (i, 0))],
        out_specs=pl.BlockSpec((d, tm), lambda i: (0, i)),
        out_shape=jax.ShapeDtypeStruct((d, t), BF16),
        compiler_params=_params(("arbitrary",), VMEM_MID),
    )(h)


def _shift_rows(u, halo, tm):
    r = lax.broadcasted_iota(jnp.int32, (tm, 1), 0)
    u1 = jnp.where(r == 0, halo[7:8], pltpu.roll(u, 1, 0))
    u2 = jnp.where(r == 0, halo[6:7], jnp.where(r == 1, halo[7:8], pltpu.roll(u, 2, 0)))
    return u1, u2


def _conv_fwd(xh, gi, bi, w_in, cw, w_out, go, bo, alpha, name):
    t, d = xh.shape
    tm = _row_tile(t)
    nt = t // tm

    def body(xh_ref, gi_ref, bi_ref, win_ref, cw_ref, wout_ref, go_ref, bo_ref,
             xo_ref, rs_ref, hb_ref, p_ref, m_ref, halo):
        i = pl.program_id(0)

        @pl.when(i == 0)
        def _():
            halo[...] = jnp.zeros_like(halo)

        h = xh_ref[...] * gi_ref[...] + bi_ref[...]
        hb = h.astype(BF16)
        bg = _dot(hb, win_ref[:, 0:d])
        cg = _dot(hb, win_ref[:, d:2 * d])
        val = _dot(hb, win_ref[:, 2 * d:3 * d])
        p_ref[:, 0:d] = bg.astype(BF16)
        p_ref[:, d:2 * d] = cg.astype(BF16)
        p_ref[:, 2 * d:3 * d] = val.astype(BF16)
        rows = i * tm + lax.broadcasted_iota(jnp.int32, (tm, 1), 0)
        u = jnp.where(rows >= PAD, cg * val, 0.0)
        u1, u2 = _shift_rows(u, halo[...], tm)
        halo[...] = u[tm - 8:tm]
        y = cw_ref[0:1] * u2 + cw_ref[1:2] * u1 + cw_ref[2:3] * u
        mb = (bg * y).astype(BF16)
        m_ref[...] = mb.T
        xhat, rstd = _ln_fwd(alpha * h + _dot(mb, wout_ref[...]))
        xo_ref[...] = xhat
        rs_ref[...] = rstd
        hb_ref[...] = (xhat * go_ref[...] + bo_ref[...]).astype(BF16).T

    row = pl.BlockSpec((tm, d), lambda i: (i, 0))
    col = pl.BlockSpec((d, tm), lambda i: (0, i))
    vec = pl.BlockSpec((1, d), lambda i: (0, 0))
    return pl.pallas_call(
        body, name=name, grid=(nt,),
        in_specs=[row, vec, vec, pl.BlockSpec((d, 3 * d), lambda i: (0, 0)),
                  pl.BlockSpec((3, d), lambda i: (0, 0)), pl.BlockSpec((d, d), lambda i: (0, 0)),
                  vec, vec],
        out_specs=[row, pl.BlockSpec((tm, 1), lambda i: (i, 0)), col,
                   pl.BlockSpec((tm, 3 * d), lambda i: (i, 0)), col],
        out_shape=[jax.ShapeDtypeStruct((t, d), F32), jax.ShapeDtypeStruct((t, 1), F32),
                   jax.ShapeDtypeStruct((d, t), BF16), jax.ShapeDtypeStruct((t, 3 * d), BF16),
                   jax.ShapeDtypeStruct((d, t), BF16)],
        scratch_shapes=[pltpu.VMEM((8, d), F32)],
        compiler_params=_params(("arbitrary",), VMEM_BIG),
    )(xh, gi, bi, w_in, cw, w_out, go, bo)


def _conv_bwd(dh, xo, rs, go, p, cw, w_in, w_out, alpha, name):
    t, d = dh.shape
    tm = _row_tile(t)
    nt = t // tm
    tb = tm // 8

    def body(dh_ref, xo_ref, rs_ref, go_ref, p_ref, ph_ref, cw_ref, win_ref, wout_ref,
             dhin_ref, dmix_ref, dp_ref, dcw_ref, dgain_ref, dbias_ref, carry):
        i = pl.program_id(0)
        tile = nt - 1 - i

        @pl.when(i == 0)
        def _():
            carry[...] = jnp.zeros_like(carry)
            dcw_ref[...] = jnp.zeros_like(dcw_ref)
            dgain_ref[...] = jnp.zeros_like(dgain_ref)
            dbias_ref[...] = jnp.zeros_like(dbias_ref)

        dz, dgp, dbp = _ln_bwd(dh_ref[...], xo_ref[...], rs_ref[...], go_ref[...])
        dgain_ref[...] += dgp
        dbias_ref[...] += dbp
        dmixb = dz.astype(BF16)
        dmix_ref[...] = dmixb
        dm = _dot_nt(dmixb, wout_ref[...])

        bg = p_ref[:, 0:d].astype(F32)
        cg = p_ref[:, d:2 * d].astype(F32)
        val = p_ref[:, 2 * d:3 * d].astype(F32)
        rows = tile * tm + lax.broadcasted_iota(jnp.int32, (tm, 1), 0)
        valid = rows >= PAD
        u = jnp.where(valid, cg * val, 0.0)
        hrows = tile * tm - 8 + lax.broadcasted_iota(jnp.int32, (8, 1), 0)
        hu = jnp.where((hrows >= PAD) & (tile > 0),
                       ph_ref[:, d:2 * d].astype(F32) * ph_ref[:, 2 * d:3 * d].astype(F32), 0.0)
        u1, u2 = _shift_rows(u, hu, tm)
        w0, w1, w2 = cw_ref[0:1], cw_ref[1:2], cw_ref[2:3]
        y = w0 * u2 + w1 * u1 + w2 * u
        dbg = dm * y
        dy = dm * bg
        dcw_ref[0:1] += jnp.sum(dy * u2, axis=0, keepdims=True)
        dcw_ref[1:2] += jnp.sum(dy * u1, axis=0, keepdims=True)
        dcw_ref[2:3] += jnp.sum(dy * u, axis=0, keepdims=True)

        nxt = carry[...]
        r = lax.broadcasted_iota(jnp.int32, (tm, 1), 0)
        dy1 = jnp.where(r == tm - 1, nxt[0:1], pltpu.roll(dy, tm - 1, 0))
        dy2 = jnp.where(r == tm - 2, nxt[0:1],
                        jnp.where(r == tm - 1, nxt[1:2], pltpu.roll(dy, tm - 2, 0)))
        carry[...] = dy[0:8]
        du = jnp.where(valid, w2 * dy + w1 * dy1 + w0 * dy2, 0.0)
        dbgb = dbg.astype(BF16)
        dcgb = (du * val).astype(BF16)
        dvalb = (du * cg).astype(BF16)
        dp_ref[:, 0:d] = dbgb
        dp_ref[:, d:2 * d] = dcgb
        dp_ref[:, 2 * d:3 * d] = dvalb
        dhin_ref[...] = (alpha * dz + _dot_nt(dbgb, win_ref[:, 0:d])
                         + _dot_nt(dcgb, win_ref[:, d:2 * d]) + _dot_nt(dvalb, win_ref[:, 2 * d:3 * d]))

    row = pl.BlockSpec((tm, d), lambda i: (nt - 1 - i, 0))
    vec = pl.BlockSpec((1, d), lambda i: (0, 0))
    prow = pl.BlockSpec((tm, 3 * d), lambda i: (nt - 1 - i, 0))
    return pl.pallas_call(
        body, name=name, grid=(nt,),
        in_specs=[row, row, pl.BlockSpec((tm, 1), lambda i: (nt - 1 - i, 0)), vec, prow,
                  pl.BlockSpec((8, 3 * d), lambda i: (jnp.maximum((nt - 1 - i) * tb - 1, 0), 0)),
                  pl.BlockSpec((3, d), lambda i: (0, 0)),
                  pl.BlockSpec((d, 3 * d), lambda i: (0, 0)), pl.BlockSpec((d, d), lambda i: (0, 0))],
        out_specs=[row, row, prow, pl.BlockSpec((3, d), lambda i: (0, 0)), vec, vec],
        out_shape=[jax.ShapeDtypeStruct((t, d), F32), jax.ShapeDtypeStruct((t, d), BF16),
                   jax.ShapeDtypeStruct((t, 3 * d), BF16), jax.ShapeDtypeStruct((3, d), F32),
                   jax.ShapeDtypeStruct((1, d), F32), jax.ShapeDtypeStruct((1, d), F32)],
        scratch_shapes=[pltpu.VMEM((8, d), F32)],
        compiler_params=_params(("arbitrary",), VMEM_BIG),
    )(dh, xo, rs, go, p, p, cw, w_in, w_out)


def _kv_fwd(xh, gi, bi, wk, wv, wf, fb, name):
    t, d = xh.shape
    tm = _row_tile(t)
    nt = t // tm

    def body(xh_ref, gi_ref, bi_ref, wk_ref, wv_ref, wf_ref, fb_ref,
             k_ref, v_ref, lg_ref, c_ref, ct_ref, run):
        i = pl.program_id(0)

        @pl.when(i == 0)
        def _():
            run[...] = jnp.zeros_like(run)

        x = (xh_ref[...] * gi_ref[...] + bi_ref[...]).astype(BF16)
        k_ref[...] = _dot(x, wk_ref[...]).astype(BF16)
        v_ref[...] = _dot(x, wv_ref[...]).astype(BF16)
        logit = _dot(x, wf_ref[...]) + fb_ref[...]
        lg_ref[...] = logit
        logf = jnp.minimum(logit, 0.0) - jnp.log(1.0 + jnp.exp(-jnp.abs(logit)))
        rows = i * tm + lax.broadcasted_iota(jnp.int32, (tm, 1), 0)
        logf = jnp.where(rows >= PAD, logf, 0.0)
        tri = (lax.broadcasted_iota(jnp.int32, (tm, tm), 0)
               >= lax.broadcasted_iota(jnp.int32, (tm, tm), 1)).astype(F32)
        cs = jnp.dot(tri, logf, precision=lax.Precision.HIGHEST, preferred_element_type=F32) + run[...]
        run[...] = cs[tm - 1:tm]
        c_ref[...] = cs
        ct_ref[...] = cs.T

    row = pl.BlockSpec((tm, d), lambda i: (i, 0))
    vec = pl.BlockSpec((1, d), lambda i: (0, 0))
    gate = pl.BlockSpec((tm, LANES), lambda i: (i, 0))
    sq = pl.BlockSpec((d, d), lambda i: (0, 0))
    return pl.pallas_call(
        body, name=name, grid=(nt,),
        in_specs=[row, vec, vec, sq, sq, pl.BlockSpec((d, LANES), lambda i: (0, 0)),
                  pl.BlockSpec((1, LANES), lambda i: (0, 0))],
        out_specs=[row, row, gate, gate, pl.BlockSpec((LANES, tm), lambda i: (0, i))],
        out_shape=[jax.ShapeDtypeStruct((t, d), BF16), jax.ShapeDtypeStruct((t, d), BF16),
                   jax.ShapeDtypeStruct((t, LANES), F32), jax.ShapeDtypeStruct((t, LANES), F32),
                   jax.ShapeDtypeStruct((LANES, t), F32)],
        scratch_shapes=[pltpu.VMEM((1, LANES), F32)],
        compiler_params=_params(("arbitrary",), VMEM_MID),
    )(xh, gi, bi, wk, wv, wf, fb)


def _kv_bwd(dk, dv, dcs, dcq, logit, dh_other, wk, wv, wf, name):
    t, d = dk.shape
    tm = _row_tile(t)
    nt = t // tm

    def body(dk_ref, dv_ref, dcs_ref, dcq_ref, lg_ref, oth_ref, wk_ref, wv_ref, wf_ref,
             dh_ref, dl_ref, dfb_ref, run):
        i = pl.program_id(0)
        tile = nt - 1 - i

        @pl.when(i == 0)
        def _():
            run[...] = jnp.zeros_like(run)
            dfb_ref[...] = jnp.zeros_like(dfb_ref)

        lane = lax.broadcasted_iota(jnp.int32, (tm, LANES), 1)
        dc = dcq_ref[...]
        for hh in range(N_HEADS):
            dc = dc + jnp.where(lane == hh, jnp.sum(dcs_ref[hh], axis=1, keepdims=True), 0.0)
        tri = (lax.broadcasted_iota(jnp.int32, (tm, tm), 0)
               <= lax.broadcasted_iota(jnp.int32, (tm, tm), 1)).astype(F32)
        dlf = jnp.dot(tri, dc, precision=lax.Precision.HIGHEST, preferred_element_type=F32) + run[...]
        run[...] = dlf[0:1]
        rows = tile * tm + lax.broadcasted_iota(jnp.int32, (tm, 1), 0)
        dlogit = jnp.where(rows >= PAD, dlf * jax.nn.sigmoid(-lg_ref[...]), 0.0)
        dfb_ref[...] += jnp.sum(dlogit, axis=0, keepdims=True)
        dlb = dlogit.astype(BF16)
        dl_ref[...] = dlb
        dh_ref[...] = (oth_ref[...] + _dot_nt(dk_ref[...], wk_ref[...])
                       + _dot_nt(dv_ref[...], wv_ref[...]) + _dot_nt(dlb, wf_ref[...]))

    row = pl.BlockSpec((tm, d), lambda i: (nt - 1 - i, 0))
    gate = pl.BlockSpec((tm, LANES), lambda i: (nt - 1 - i, 0))
    sq = pl.BlockSpec((d, d), lambda i: (0, 0))
    return pl.pallas_call(
        body, name=name, grid=(nt,),
        in_specs=[row, row, pl.BlockSpec((N_HEADS, tm, LANES), lambda i: (0, nt - 1 - i, 0)), gate, gate, row,
                  sq, sq, pl.BlockSpec((d, LANES), lambda i: (0, 0))],
        out_specs=[row, gate, pl.BlockSpec((1, LANES), lambda i: (0, 0))],
        out_shape=[jax.ShapeDtypeStruct((t, d), F32), jax.ShapeDtypeStruct((t, LANES), BF16),
                   jax.ShapeDtypeStruct((1, LANES), F32)],
        scratch_shapes=[pltpu.VMEM((1, LANES), F32)],
        compiler_params=_params(("arbitrary",), VMEM_MID),
    )(dk, dv, dcs, dcq, logit, dh_other, wk, wv, wf)


def _proj(xh, gi, bi, w, name):
    t, k = xh.shape
    n = w.shape[1]
    tm = _row_tile(t)

    def body(x_ref, g_ref, b_ref, w_ref, o_ref):
        x = (x_ref[...] * g_ref[...] + b_ref[...]).astype(BF16)
        o_ref[...] = _dot(x, w_ref[...]).astype(BF16)

    vec = pl.BlockSpec((1, k), lambda i: (0, 0))
    return pl.pallas_call(
        body, name=name, grid=(t // tm,),
        in_specs=[pl.BlockSpec((tm, k), lambda i: (i, 0)), vec, vec, pl.BlockSpec((k, n), lambda i: (0, 0))],
        out_specs=pl.BlockSpec((tm, n), lambda i: (i, 0)),
        out_shape=jax.ShapeDtypeStruct((t, n), BF16),
        compiler_params=_params(("arbitrary",), VMEM_MID),
    )(xh, gi, bi, w)


def _add_proj_nt(base, y, w, name):
    t, n = y.shape
    k = w.shape[0]
    tm = _row_tile(t)

    def body(b_ref, y_ref, w_ref, o_ref):
        o_ref[...] = b_ref[...] + _dot_nt(y_ref[...].astype(BF16), w_ref[...])

    return pl.pallas_call(
        body, name=name, grid=(t // tm,),
        in_specs=[pl.BlockSpec((tm, k), lambda i: (i, 0)), pl.BlockSpec((tm, n), lambda i: (i, 0)),
                  pl.BlockSpec((k, n), lambda i: (0, 0))],
        out_specs=pl.BlockSpec((tm, k), lambda i: (i, 0)),
        out_shape=jax.ShapeDtypeStruct((t, k), F32),
        compiler_params=_params(("arbitrary",), VMEM_MID),
    )(base, y, w)


def _attn_out_fwd(ot, xh, gi, bi, w_o, go, bo, alpha, name):
    t, d = xh.shape
    tm = _row_tile(t)

    def body(ot_ref, xh_ref, gi_ref, bi_ref, wo_ref, go_ref, bo_ref, xo_ref, rs_ref, hb_ref):
        h = xh_ref[...] * gi_ref[...] + bi_ref[...]
        xhat, rstd = _ln_fwd(alpha * h + _dot_tn(ot_ref[...], wo_ref[...]))
        xo_ref[...] = xhat
        rs_ref[...] = rstd
        hb_ref[...] = (xhat * go_ref[...] + bo_ref[...]).astype(BF16).T

    row = pl.BlockSpec((tm, d), lambda i: (i, 0))
    col = pl.BlockSpec((d, tm), lambda i: (0, i))
    vec = pl.BlockSpec((1, d), lambda i: (0, 0))
    return pl.pallas_call(
        body, name=name, grid=(t // tm,),
        in_specs=[col, row, vec, vec, pl.BlockSpec((d, d), lambda i: (0, 0)), vec, vec],
        out_specs=[row, pl.BlockSpec((tm, 1), lambda i: (i, 0)), col],
        out_shape=[jax.ShapeDtypeStruct((t, d), F32), jax.ShapeDtypeStruct((t, 1), F32),
                   jax.ShapeDtypeStruct((d, t), BF16)],
        compiler_params=_params(("arbitrary",), VMEM_MID),
    )(ot, xh, gi, bi, w_o, go, bo)


def _attn_out_bwd(dh, xo, rs, go, ot, w_o, alpha, name):
    t, d = dh.shape
    tm = _row_tile(t)
    hd = d // N_HEADS

    def body(dh_ref, xo_ref, rs_ref, go_ref, ot_ref, wo_ref,
             dres_ref, dmix_ref, dot_ref, delta_ref, dgain_ref, dbias_ref):
        @pl.when(pl.program_id(0) == 0)
        def _():
            dgain_ref[...] = jnp.zeros_like(dgain_ref)
            dbias_ref[...] = jnp.zeros_like(dbias_ref)

        dz, dgp, dbp = _ln_bwd(dh_ref[...], xo_ref[...], rs_ref[...], go_ref[...])
        dgain_ref[...] += dgp
        dbias_ref[...] += dbp
        dres_ref[...] = alpha * dz
        dmixb = dz.astype(BF16)
        dmix_ref[...] = dmixb
        dot_t = _dot_nt(wo_ref[...], dmixb)
        dot_ref[...] = dot_t.astype(BF16)
        prod = dot_t * ot_ref[...].astype(F32)
        delta_ref[...] = jnp.sum(prod.reshape(N_HEADS, hd, tm), axis=1)

    row = pl.BlockSpec((tm, d), lambda i: (i, 0))
    vec = pl.BlockSpec((1, d), lambda i: (0, 0))
    col = pl.BlockSpec((d, tm), lambda i: (0, i))
    return pl.pallas_call(
        body, name=name, grid=(t // tm,),
        in_specs=[row, row, pl.BlockSpec((tm, 1), lambda i: (i, 0)), vec, col,
                  pl.BlockSpec((d, d), lambda i: (0, 0))],
        out_specs=[row, row, col, pl.BlockSpec((N_HEADS, tm), lambda i: (0, i)), vec, vec],
        out_shape=[jax.ShapeDtypeStruct((t, d), F32), jax.ShapeDtypeStruct((t, d), BF16),
                   jax.ShapeDtypeStruct((d, t), BF16), jax.ShapeDtypeStruct((N_HEADS, t), F32),
                   jax.ShapeDtypeStruct((1, d), F32), jax.ShapeDtypeStruct((1, d), F32)],
        compiler_params=_params(("arbitrary",), VMEM_MID),
    )(dh, xo, rs, go, ot, w_o)


def _scores_t(k, q, ct_ref, c_ref, h, i, j, tq, tk, scale, masked):
    st = _dot_nt(k, q) * scale
    sub = lax.broadcasted_iota(jnp.int32, (8, tq), 0)
    cq = jnp.sum(jnp.where(sub == h, ct_ref[...], 0.0), axis=0, keepdims=True)
    lane = lax.broadcasted_iota(jnp.int32, (tk, LANES), 1)
    ck = jnp.sum(jnp.where(lane == h, c_ref[...], 0.0), axis=1, keepdims=True)
    st = st + cq - ck
    if not masked:
        return st
    kpos = j * tk + lax.broadcasted_iota(jnp.int32, (tk, 1), 0)
    qpos = i * tq + lax.broadcasted_iota(jnp.int32, (1, tq), 1)
    return jnp.where((kpos <= qpos) & (kpos >= PAD), st, NEG_INF)


def _tri_pairs(n, by_row):
    if by_row:
        pairs = [(i, j) for i in range(n) for j in range(i + 1)]
    else:
        pairs = [(i, j) for j in range(n) for i in range(j, n)]
    return (jnp.asarray([p[0] for p in pairs], jnp.int32), jnp.asarray([p[1] for p in pairs], jnp.int32))


def _attn_fwd(q, k, v, c, ct, name):
    t, d = q.shape
    hd = d // N_HEADS
    tq = tk = _row_tile(t)
    nq = t // tq
    scale = 1.0 / math.sqrt(hd)

    def body(it_ref, jt_ref, q_ref, k_ref, v_ref, c_ref, ct_ref, ot_ref, lse_ref, m_s, l_s, acc):
        h, p_ = pl.program_id(0), pl.program_id(1)
        i, j = it_ref[p_], jt_ref[p_]

        @pl.when(j == 0)
        def _():
            m_s[...] = jnp.full_like(m_s, NEG_INF)
            l_s[...] = jnp.zeros_like(l_s)
            acc[...] = jnp.zeros_like(acc)

        def update(masked):
            st = _scores_t(k_ref[...], q_ref[...], ct_ref, c_ref, h, i, j, tq, tk, scale, masked)
            m_new = jnp.maximum(m_s[...], jnp.max(st, axis=0, keepdims=True))
            a = jnp.exp(m_s[...] - m_new)
            p = jnp.exp(st - m_new)
            l_s[...] = a * l_s[...] + jnp.sum(p, axis=0, keepdims=True)
            acc[...] = a * acc[...] + _dot_tn(v_ref[...], p.astype(BF16))
            m_s[...] = m_new

        edge = (j == i) | (j == 0)
        pl.when(edge)(lambda: update(True))
        pl.when(jnp.logical_not(edge))(lambda: update(False))

        @pl.when(j == i)
        def _():
            ot_ref[...] = (acc[...] / l_s[...]).astype(BF16)
            lse_ref[0] = m_s[...] + jnp.log(l_s[...])

    it, jt = _tri_pairs(nq, by_row=True)
    kv = pl.BlockSpec((tk, hd), lambda h, p, it, jt: (jt[p], h))
    return pl.pallas_call(
        body, name=name,
        grid_spec=pltpu.PrefetchScalarGridSpec(
            num_scalar_prefetch=2, grid=(N_HEADS, it.shape[0]),
            in_specs=[pl.BlockSpec((tq, hd), lambda h, p, it, jt: (it[p], h)), kv, kv,
                      pl.BlockSpec((tk, LANES), lambda h, p, it, jt: (jt[p], 0)),
                      pl.BlockSpec((8, tq), lambda h, p, it, jt: (0, it[p]))],
            out_specs=[pl.BlockSpec((hd, tq), lambda h, p, it, jt: (h, it[p])),
                       pl.BlockSpec((1, 1, tq), lambda h, p, it, jt: (h, 0, it[p]))],
            scratch_shapes=[pltpu.VMEM((1, tq), F32), pltpu.VMEM((1, tq), F32), pltpu.VMEM((hd, tq), F32)]),
        out_shape=[jax.ShapeDtypeStruct((d, t), BF16), jax.ShapeDtypeStruct((N_HEADS, 1, t), F32)],
        compiler_params=_params(("arbitrary", "arbitrary"), VMEM_MID),
    )(it, jt, q, k, v, c, ct)


def _attn_bwd(q, k, v, c, ct, lse, delta, dot_t, name, carry=()):
    t, d = q.shape
    hd = d // N_HEADS
    tq = tk = _row_tile(t)
    nq = t // tq
    scale = 1.0 / math.sqrt(hd)

    def body(it_ref, jt_ref, q_ref, k_ref, v_ref, c_ref, ct_ref, lse_ref, delta_ref, dot_ref,
             dq_ref, dk_ref, dv_ref, dcs_ref, drow_ref, dk_acc, dv_acc, dc_acc):
        h, p_ = pl.program_id(0), pl.program_id(1)
        i, j = it_ref[p_], jt_ref[p_]

        @pl.when(p_ == 0)
        def _():
            dq_ref[...] = jnp.zeros_like(dq_ref)
            drow_ref[...] = jnp.zeros_like(drow_ref)

        @pl.when(i == j)
        def _():
            dk_acc[...] = jnp.zeros_like(dk_acc)
            dv_acc[...] = jnp.zeros_like(dv_acc)
            dc_acc[...] = jnp.zeros_like(dc_acc)

        def update(masked):
            qv, kv_, vv = q_ref[...], k_ref[...], v_ref[...]
            st = _scores_t(kv_, qv, ct_ref, c_ref, h, i, j, tq, tk, scale, masked)
            p = jnp.exp(st - lse_ref[0])
            do_t = dot_ref[...]
            dp = _dot(vv, do_t)
            sub = lax.broadcasted_iota(jnp.int32, (8, tq), 0)
            dl = jnp.sum(jnp.where(sub == h, delta_ref[...], 0.0), axis=0, keepdims=True)
            ds = p * (dp - dl)
            dsb = ds.astype(BF16)
            dv_acc[...] += _dot_nt(p.astype(BF16), do_t)
            dk_acc[...] += _dot(dsb, qv) * scale
            rows = pl.ds(pl.multiple_of(i * tq, tq), tq)
            dq_ref[rows, :] += _dot_tn(dsb, kv_) * scale
            part = ds[:, 0:LANES]
            for g in range(1, tq // LANES):
                part = part + ds[:, g * LANES:(g + 1) * LANES]
            dc_acc[...] += part
            drow_ref[0, i] += jnp.broadcast_to(jnp.sum(ds, axis=0, keepdims=True), (8, tq))

        edge = (j == i) | (j == 0)
        pl.when(edge)(lambda: update(True))
        pl.when(jnp.logical_not(edge))(lambda: update(False))

        @pl.when(i == nq - 1)
        def _():
            dk_ref[...] = dk_acc[...].astype(BF16)
            dv_ref[...] = dv_acc[...].astype(BF16)
            dcs_ref[0] = -dc_acc[...]

    it, jt = _tri_pairs(nq, by_row=False)
    npairs = it.shape[0]
    kv = pl.BlockSpec((tk, hd), lambda h, p, it, jt: (jt[p], h))
    first = lambda: (pl.program_id(0) == 0) & (pl.program_id(1) == 0)
    last = lambda: (pl.program_id(0) == N_HEADS - 1) & (pl.program_id(1) == npairs - 1)
    return pl.pallas_call(
        _carried(body, 10, 5, carry, first, last), name=name,
        grid_spec=pltpu.PrefetchScalarGridSpec(
            num_scalar_prefetch=2, grid=(N_HEADS, npairs),
            in_specs=[pl.BlockSpec((tq, hd), lambda h, p, it, jt: (it[p], h)), kv, kv,
                      pl.BlockSpec((tk, LANES), lambda h, p, it, jt: (jt[p], 0)),
                      pl.BlockSpec((8, tq), lambda h, p, it, jt: (0, it[p])),
                      pl.BlockSpec((1, 1, tq), lambda h, p, it, jt: (h, 0, it[p])),
                      pl.BlockSpec((N_HEADS, tq), lambda h, p, it, jt: (0, it[p])),
                      pl.BlockSpec((hd, tq), lambda h, p, it, jt: (h, it[p]))] + [ANY] * len(carry),
            out_specs=[pl.BlockSpec((t, hd), lambda h, p, it, jt: (0, h)), kv, kv,
                       pl.BlockSpec((1, tk, LANES), lambda h, p, it, jt: (h, jt[p], 0)),
                       pl.BlockSpec((1, nq, 8, tq), lambda h, p, it, jt: (h, 0, 0, 0))] + [ANY] * len(carry),
            scratch_shapes=[pltpu.VMEM((tk, hd), F32), pltpu.VMEM((tk, hd), F32),
                            pltpu.VMEM((tk, LANES), F32)] + _carry_scratch(carry)),
        out_shape=[jax.ShapeDtypeStruct((t, d), F32), jax.ShapeDtypeStruct((t, d), BF16),
                   jax.ShapeDtypeStruct((t, d), BF16), jax.ShapeDtypeStruct((N_HEADS, t, LANES), F32),
                   jax.ShapeDtypeStruct((N_HEADS, nq, 8, tq), F32)] + _carry_shapes(carry),
        compiler_params=_params(("arbitrary", "arbitrary"), VMEM_MID),
    )(it, jt, q, k, v, c, ct, lse, delta, dot_t, *[a for _, a in carry])


def _loss_head(xh, g, b, target, name):
    t, d = xh.shape
    tm = LOSS_TILE
    nt = t // tm
    lead = ROW0 // tm

    def body(xh_ref, g_ref, b_ref, tg_ref, dh_ref, loss_ref, part):
        i = pl.program_id(0)

        @pl.when(i == 0)
        def _():
            part[...] = jnp.zeros_like(part)

        @pl.when(i < lead)
        def _():
            dh_ref[...] = jnp.zeros_like(dh_ref)

        @pl.when(i >= lead)
        def _():
            e = xh_ref[...] * g_ref[...] + b_ref[...] - tg_ref[...]
            dh_ref[...] = e * (1.0 / d)
            part[...] += jnp.sum(e * e, axis=0, keepdims=True)

        @pl.when(i == nt - 1)
        def _():
            loss_ref[...] = jnp.full((1, LANES), 0.5 / d, F32) * jnp.sum(part[...])

    return pl.pallas_call(
        body, name=name, grid=(nt,),
        in_specs=[pl.BlockSpec((tm, d), lambda i: (i, 0)), pl.BlockSpec((1, d), lambda i: (0, 0)),
                  pl.BlockSpec((1, d), lambda i: (0, 0)),
                  pl.BlockSpec((tm, d), lambda i: (jnp.maximum(i - lead, 0), 0))],
        out_specs=[pl.BlockSpec((tm, d), lambda i: (i, 0)), pl.BlockSpec((1, LANES), lambda i: (0, 0))],
        out_shape=[jax.ShapeDtypeStruct((t, d), F32), jax.ShapeDtypeStruct((1, LANES), F32)],
        scratch_shapes=[pltpu.VMEM((1, d), F32)],
        compiler_params=_params(("arbitrary",), VMEM_MID),
    )(xh, g, b, target)


def _adamw(w, g, m, v, name):
    r, c = w.shape
    tr = r
    for cand in (256, 128, 64, 32, 16, 8):
        if r % cand == 0 and r > cand:
            tr = cand
            break
    bc1 = 1.0 - ADAM_B1 ** ADAM_STEP
    bc2 = 1.0 - ADAM_B2 ** ADAM_STEP

    def body(w_ref, g_ref, m_ref, v_ref, d_ref, nm_ref, nv_ref):
        gg = g_ref[...]
        nm = ADAM_B1 * m_ref[...] + (1.0 - ADAM_B1) * gg
        nv = ADAM_B2 * v_ref[...] + (1.0 - ADAM_B2) * (gg * gg)
        d_ref[...] = -ADAM_LR * ((nm / bc1) / (jnp.sqrt(nv / bc2) + ADAM_EPS) + ADAM_WD * w_ref[...])
        nm_ref[...] = nm
        nv_ref[...] = nv

    blk = pl.BlockSpec((tr, c), lambda i: (i, 0))
    shp = jax.ShapeDtypeStruct((r, c), F32)
    return pl.pallas_call(
        body, name=name, grid=(r // tr,), in_specs=[blk] * 4, out_specs=[blk] * 3,
        out_shape=[shp] * 3, compiler_params=_params(("arbitrary",), VMEM_MID),
    )(w, g, m, v)


def _sum_sources(r, name):
    n, rows, c = r.shape
    tr = next(cand for cand in range(min(rows, SUM_ROWS_MAX), 0, -BF16_ROWS) if rows % cand == 0)

    def body(r_ref, o_ref):
        acc = r_ref[0].astype(F32)
        for s in range(1, n):
            acc = acc + r_ref[s].astype(F32)
        o_ref[...] = acc

    return pl.pallas_call(
        body, name=name, grid=(rows // tr,),
        in_specs=[pl.BlockSpec((n, tr, c), lambda i: (0, i, 0))],
        out_specs=pl.BlockSpec((tr, c), lambda i: (i, 0)),
        out_shape=jax.ShapeDtypeStruct((rows, c), F32),
        compiler_params=_params(("arbitrary",), VMEM_MID),
    )(r)


def _all_gather(x, name):
    rows, cols = x.shape

    def body(x_ref, out_ref, send_sems, recv_sems, local_sem):
        mx, my, mc = lax.axis_index("x"), lax.axis_index("y"), lax.axis_index("c")
        me, sibling = (mx, my, mc), (mx, my, 1 - mc)
        chips = [(1 - mx, my), (mx, 1 - my), (1 - mx, 1 - my)]

        def slot(px, py, pc):
            return out_ref.at[4 * px + 2 * py + pc]

        def copy(k, block, to, src=None):
            return pltpu.make_async_remote_copy(
                src_ref=slot(*block) if src is None else src, dst_ref=slot(*block),
                send_sem=send_sems.at[k], recv_sem=recv_sems.at[k],
                device_id=to, device_id_type=MESH)

        mine = pltpu.make_async_copy(x_ref, slot(*me), local_sem)
        mine.start()
        first = [copy(0, me, sibling, src=x_ref)]
        first += [copy(1 + n, me, (*chip, mc), src=x_ref) for n, chip in enumerate(chips)]
        for cp in first:
            cp.start()
        passed = [copy(4 + n, (*chip, mc), sibling) for n, chip in enumerate(chips)]
        for n, chip in enumerate(chips):
            copy(1 + n, (*chip, mc), me).wait_recv()
            passed[n].start()
        copy(0, sibling, me).wait_recv()
        for n, chip in enumerate(chips):
            copy(4 + n, (*chip, 1 - mc), me).wait_recv()
        for cp in first + passed:
            cp.wait_send()
        mine.wait()

    return pl.pallas_call(
        body, name=name, in_specs=[ANY], out_specs=ANY,
        out_shape=jax.ShapeDtypeStruct((N_DEV, rows, cols), x.dtype),
        scratch_shapes=[pltpu.SemaphoreType.DMA((7,)), pltpu.SemaphoreType.DMA((7,)),
                        pltpu.SemaphoreType.DMA],
    )(x)


def _pack_rows(parts, width, mult, lead=0):
    out = []
    for a in parts:
        head = a.shape[:lead]
        flat = a.reshape(head + (-1,))
        padn = (-flat.shape[-1]) % (width * mult)
        if padn:
            flat = jnp.pad(flat, [(0, 0)] * lead + [(0, padn)])
        out.append(flat.reshape(head + (-1, width)))
    return jnp.concatenate(out, axis=lead)


def _rows_of(shape, width, mult):
    n = math.prod(shape)
    per = width * mult
    return ((n + per - 1) // per) * mult


def _unpack_rows(buf, shapes, width, mult):
    lead = buf.shape[:-2]
    out, off = [], 0
    for shp in shapes:
        r = _rows_of(shp, width, mult)
        flat = buf[..., off:off + r, :].reshape(lead + (r * width,))
        out.append(flat[..., :math.prod(shp)].reshape(lead + tuple(shp)))
        off += r
    return out


def _cols_from_devices(g):
    nd = g.ndim
    perm = tuple(range(1, nd - 1)) + (0, nd - 1)
    t = jnp.transpose(g, perm)
    return t.reshape(t.shape[:-2] + (t.shape[-2] * t.shape[-1],))


def _cols_to_devices(a):
    c = a.shape[-1] // N_DEV
    t = a.reshape(a.shape[:-1] + (N_DEV, c))
    nd = t.ndim
    perm = (nd - 2,) + tuple(range(0, nd - 2)) + (nd - 1,)
    return jnp.transpose(t, perm)


WIDTH = 1024


def kernel(x, meta, ffn1_wg, ffn1_wu, ffn1_wd, ffn2_wg, ffn2_wu, ffn2_wd, ln_gain, ln_bias, conv_w_in, conv_w, conv_w_out, kv_w, f_bias, attn_w_q, attn_w_o, loss_target, m_meta, m_ffn1_wg, m_ffn1_wu, m_ffn1_wd, m_ffn2_wg, m_ffn2_wu, m_ffn2_wd, m_ln_gain, m_ln_bias, m_conv_w_in, m_conv_w, m_conv_w_out, m_kv_w, m_f_bias, m_attn_w_q, m_attn_w_o, v_meta, v_ffn1_wg, v_ffn1_wu, v_ffn1_wd, v_ffn2_wg, v_ffn2_wu, v_ffn2_wd, v_ln_gain, v_ln_bias, v_conv_w_in, v_conv_w, v_conv_w_out, v_kv_w, v_f_bias, v_attn_w_q, v_attn_w_o):
    depth = ln_gain.shape[0]
    alpha = float((2 * depth) ** 0.25)
    d = x.shape[-1]
    seq = x.shape[1]
    t = ROW0 + seq
    fsh = ffn1_wg.shape[-1]
    f = fsh * N_DEV
    fck = MXU_COLS
    nc = f // fck
    me = 4 * lax.axis_index("x") + 2 * lax.axis_index("y") + lax.axis_index("c")

    def pack16(parts, lead=0):
        return _pack_rows([a.astype(BF16) for a in parts], WIDTH, BF16_ROWS, lead)

    def unpack16(buf, parts):
        return _unpack_rows(buf, [a.shape for a in parts], WIDTH, BF16_ROWS)

    grp0 = [ffn1_wg[0], ffn1_wu[0], ffn1_wd[0], conv_w_in[0], conv_w_out[0]]
    grp1 = [ffn2_wg[0], ffn2_wu[0], ffn2_wd[0]]
    grp2 = [kv_w, ffn1_wg[1], ffn1_wu[1], ffn1_wd[1]]
    grp3 = [attn_w_q[0], attn_w_o[0], ffn2_wg[1], ffn2_wu[1], ffn2_wd[1]]
    small = [meta, ln_gain, ln_bias, conv_w]
    small_shapes = [a.shape for a in small]
    gat0 = _all_gather(pack16(grp0), "ag_first")
    gsmall = _all_gather(_pack_rows(small, WIDTH, F32_ROWS), "ag_small")
    gmeta, ggain, gbias, gcw = _unpack_rows(gsmall, small_shapes, WIDTH, F32_ROWS)

    def ffn_chunks(gg, gu, gd):
        up = lambda g: jnp.transpose(_cols_from_devices(g).reshape(d, nc, fck), (1, 0, 2))
        return up(gg), up(gu), gd.reshape(nc, fck, d)

    g1g, g1u, g1d, gcin, gcout = unpack16(gat0, grp0)
    w_in = _cols_from_devices(gcin)
    w_out = gcout.reshape(d, d)
    fb = jnp.pad(f_bias, (0, LANES - N_HEADS)).reshape(1, LANES)
    meta_f = _cols_from_devices(gmeta)
    gain_f = _cols_from_devices(ggain)
    bias_f = _cols_from_devices(gbias)
    cw_f = _cols_from_devices(gcw)[0]

    def gb(l, n):
        return gain_f[l, n].reshape(1, d), bias_f[l, n].reshape(1, d)

    ones = jnp.ones((1, d), F32)
    zeros = jnp.zeros((1, d), F32)

    h0 = jnp.concatenate([jnp.zeros((PAD, d), F32), meta_f, x[0]], axis=0)
    hb0 = _cast_t(h0, "h0_bf16_t")

    w1 = ffn_chunks(g1g, g1u, g1d)
    g00, b00 = gb(0, 0)
    xh1, rs1, hb1, gg1, uu1, gat1 = _ffn_fwd(h0, ones, zeros, *w1, g00, b00, alpha, "ffn_fwd_0a",
                                              carry=[(True, pack16(grp1))])
    g01, b01 = gb(0, 1)
    xh2, rs2, hb2, pp, mb = _conv_fwd(xh1, g00, b00, w_in, cw_f, w_out, g01, b01, alpha, "conv_fwd")
    w2 = ffn_chunks(*unpack16(gat1, grp1))
    g02, b02 = gb(0, 2)
    xh3, rs3, hb3, gg3, uu3, gat2 = _ffn_fwd(xh2, g01, b01, *w2, g02, b02, alpha, "ffn_fwd_0b",
                                              carry=[(True, pack16(grp2))])
    gkv, g3g, g3u, g3d = unpack16(gat2, grp2)
    kvw = _cols_from_devices(gkv)
    wk, wv = kvw[:, :d], kvw[:, d:2 * d]
    wf = jnp.pad(kvw[:, 2 * d:], ((0, 0), (0, LANES - N_HEADS)))
    kk, vv, logit, cc, cct = _kv_fwd(xh3, g02, b02, wk, wv, wf, fb, "kv_fwd")

    w3 = ffn_chunks(g3g, g3u, g3d)
    g10, b10 = gb(1, 0)
    xh4, rs4, hb4, gg4, uu4, gat3 = _ffn_fwd(xh3, g02, b02, *w3, g10, b10, alpha, "ffn_fwd_1a",
                                              carry=[(True, pack16(grp3))])
    gwq, gwo, g4g, g4u, g4d = unpack16(gat3, grp3)
    w_q, w_o = gwq.reshape(d, d), gwo.reshape(d, d)
    qq = _proj(xh4, g10, b10, w_q, "q_proj")
    ot, lse = _attn_fwd(qq, kk, vv, cc, cct, "attn_fwd")
    g11, b11 = gb(1, 1)
    xh5, rs5, hb5 = _attn_out_fwd(ot, xh4, g10, b10, w_o, g11, b11, alpha, "attn_out_fwd")
    w4 = ffn_chunks(g4g, g4u, g4d)
    g12, b12 = gb(1, 2)
    xh6, rs6, _, gg6, uu6 = _ffn_fwd(xh5, g11, b11, *w4, g12, b12, alpha, "ffn_fwd_1b")

    dh6, loss_l = _loss_head(xh6, g12, b12, loss_target[0], "loss_head")
    loss = lax.psum(loss_l[0, 0], ("x", "y", "c"))

    dgain = [[None] * 3 for _ in range(depth)]
    dbias = [[None] * 3 for _ in range(depth)]

    def ffn_send(dg_, du_, ddt):
        return [_cols_to_devices(dg_), _cols_to_devices(du_), ddt.T.reshape(N_DEV, fsh, d)]

    def received(landed, parts, name):
        return _unpack_rows(_sum_sources(landed, name), [a.shape for a in parts], WIDTH, BF16_ROWS)

    dh5, do6, dg6, du6, a6, dgain[1][2], dbias[1][2] = _ffn_bwd(dh6, xh6, rs6, g12, gg6, uu6, *w4, alpha, "ffn_bwd_1b")
    dw4g, dw4u = _wgrad(hb5, [dg6, du6], "wgrad_up_1b")
    (dw4dt,) = _wgrad(do6, [a6], "wgrad_down_1b")

    dres4, dmix5, dot_t, delta, dgain[1][1], dbias[1][1] = _attn_out_bwd(dh5, xh5, rs5, g11, ot, w_o, alpha, "attn_out_bwd")
    (dwo,) = _wgrad(ot, [dmix5], "wgrad_wo")
    out3 = [ffn2_wg[1], ffn2_wu[1], ffn2_wd[1], attn_w_o[0]]
    send3 = pack16(ffn_send(dw4g, dw4u, dw4dt) + [dwo.reshape(N_DEV, d // N_DEV, d)], lead=1)
    dq, dkk, dvv, dcs, drow, land3 = _attn_bwd(qq, kk, vv, cc, cct, lse, delta, dot_t, "attn_bwd",
                                               carry=[(False, send3)])
    dh4 = _add_proj_nt(dres4, dq, w_q, "q_bwd")
    (dwq,) = _wgrad(hb4, [dq], "wgrad_wq")

    dh3a, do4, dg4, du4, a4, dgain[1][0], dbias[1][0] = _ffn_bwd(dh4, xh4, rs4, g10, gg4, uu4, *w3, alpha, "ffn_bwd_1a")
    dw3g, dw3u = _wgrad(hb3, [dg4, du4], "wgrad_up_1a")
    (dw3dt,) = _wgrad(do4, [a4], "wgrad_down_1a")

    dcq = jnp.pad(drow[:, :, 0, :].reshape(N_HEADS, t).T, ((0, 0), (0, LANES - N_HEADS)))
    dh3, dlogit, dfb = _kv_bwd(dkk, dvv, dcs, dcq, logit, dh3a, wk, wv, wf, "kv_bwd")
    dwk, dwv = _wgrad(hb3, [dkk, dvv], "wgrad_kv")
    (dwf,) = _wgrad(hb3, [dlogit], "wgrad_f")
    dkv = jnp.concatenate([dwk, dwv, dwf[:, :N_HEADS]], axis=1)

    out2 = [attn_w_q[0], ffn1_wg[1], ffn1_wu[1], ffn1_wd[1], kv_w]
    send2 = pack16([dwq.reshape(N_DEV, d // N_DEV, d)] + ffn_send(dw3g, dw3u, dw3dt) + [_cols_to_devices(dkv)], lead=1)
    dh2, do3, dg3, du3, a3, dgain[0][2], dbias[0][2], land2 = _ffn_bwd(
        dh3, xh3, rs3, g02, gg3, uu3, *w2, alpha, "ffn_bwd_0b", carry=[(False, send2)])
    dw2g, dw2u = _wgrad(hb2, [dg3, du3], "wgrad_up_0b")
    (dw2dt,) = _wgrad(do3, [a3], "wgrad_down_0b")

    dh1, dmix2, dpp, dcw, dgain[0][1], dbias[0][1] = _conv_bwd(dh2, xh2, rs2, g01, pp, cw_f, w_in, w_out, alpha, "conv_bwd")
    (dwin,) = _wgrad(hb1, [dpp], "wgrad_conv_in")
    (dwout,) = _wgrad(mb, [dmix2], "wgrad_conv_out")

    out1 = [ffn2_wg[0], ffn2_wu[0], ffn2_wd[0], conv_w_in[0], conv_w_out[0]]
    send1 = pack16(ffn_send(dw2g, dw2u, dw2dt) + [_cols_to_devices(dwin), dwout.reshape(N_DEV, d // N_DEV, d)], lead=1)
    dh0, do1, dg1, du1, a1, dgain[0][0], dbias[0][0], land1 = _ffn_bwd(
        dh1, xh1, rs1, g00, gg1, uu1, *w1, alpha, "ffn_bwd_0a", carry=[(False, send1)])
    dw1g, dw1u = _wgrad(hb0, [dg1, du1], "wgrad_up_0a")
    (dw1dt,) = _wgrad(do1, [a1], "wgrad_down_0a")
    out0 = [ffn1_wg[0], ffn1_wu[0], ffn1_wd[0]]
    land0 = _exchange(pack16(ffn_send(dw1g, dw1u, dw1dt), lead=1), "rs_last")

    grad_x = dh0[ROW0:].reshape(1, seq, d)

    r2g, r2u, r2d, rwo = received(land3, out3, "rs_sum_3")
    rwq, r3g, r3u, r3d, rkv = received(land2, out2, "rs_sum_2")
    r1g_, r1u_, r1d_, rcin, rcout = received(land1, out1, "rs_sum_1")
    r0g, r0u, r0d = received(land0, out0, "rs_sum_0")

    dmeta = dh0[PAD:ROW0]
    dgain_f = jnp.stack([jnp.concatenate(r, axis=0) for r in dgain])
    dbias_f = jnp.stack([jnp.concatenate(r, axis=0) for r in dbias])
    small_full = [dmeta, dgain_f, dbias_f, dcw[None], dfb]
    small_full_shapes = [a.shape for a in small_full]
    rsmall = _sum_sources(_all_gather(_pack_rows(small_full, WIDTH, F32_ROWS), "ag_small_grads"), "small_sum")
    smeta, sgain, sbias, scw, sfb = _unpack_rows(rsmall, small_full_shapes, WIDTH, F32_ROWS)
    csh = d // N_DEV

    def my_cols(a):
        return lax.dynamic_slice_in_dim(a, me * csh, csh, axis=a.ndim - 1)

    grads = {
        "meta": my_cols(smeta), "ffn1_wg": jnp.stack([r0g, r3g]), "ffn1_wu": jnp.stack([r0u, r3u]),
        "ffn1_wd": jnp.stack([r0d, r3d]), "ffn2_wg": jnp.stack([r1g_, r2g]), "ffn2_wu": jnp.stack([r1u_, r2u]),
        "ffn2_wd": jnp.stack([r1d_, r2d]), "ln_gain": my_cols(sgain), "ln_bias": my_cols(sbias),
        "conv_w_in": rcin[None], "conv_w": my_cols(scw), "conv_w_out": rcout[None], "kv_w": rkv,
        "f_bias": sfb[0, :N_HEADS], "attn_w_q": rwq[None], "attn_w_o": rwo[None],
    }
    weights = dict(meta=meta, ffn1_wg=ffn1_wg, ffn1_wu=ffn1_wu, ffn1_wd=ffn1_wd, ffn2_wg=ffn2_wg,
                   ffn2_wu=ffn2_wu, ffn2_wd=ffn2_wd, ln_gain=ln_gain, ln_bias=ln_bias,
                   conv_w_in=conv_w_in, conv_w=conv_w, conv_w_out=conv_w_out, kv_w=kv_w,
                   f_bias=f_bias, attn_w_q=attn_w_q, attn_w_o=attn_w_o)
    moms = dict(meta=(m_meta, v_meta), ffn1_wg=(m_ffn1_wg, v_ffn1_wg), ffn1_wu=(m_ffn1_wu, v_ffn1_wu),
                ffn1_wd=(m_ffn1_wd, v_ffn1_wd), ffn2_wg=(m_ffn2_wg, v_ffn2_wg), ffn2_wu=(m_ffn2_wu, v_ffn2_wu),
                ffn2_wd=(m_ffn2_wd, v_ffn2_wd), ln_gain=(m_ln_gain, v_ln_gain), ln_bias=(m_ln_bias, v_ln_bias),
                conv_w_in=(m_conv_w_in, v_conv_w_in), conv_w=(m_conv_w, v_conv_w),
                conv_w_out=(m_conv_w_out, v_conv_w_out), kv_w=(m_kv_w, v_kv_w), f_bias=(m_f_bias, v_f_bias),
                attn_w_q=(m_attn_w_q, v_attn_w_q), attn_w_o=(m_attn_w_o, v_attn_w_o))

    names = list(weights)
    g_out, d_out, m_out, v_out = [], [], [], []
    for n in names:
        w = weights[n]
        shp = w.shape
        two = (1, shp[0]) if w.ndim == 1 else (math.prod(shp[:-1]), shp[-1])
        g = grads[n].reshape(shp)
        mm, vv_ = moms[n]
        dl, nm, nv = _adamw(w.reshape(two), g.reshape(two), mm.reshape(two), vv_.reshape(two), "adamw_" + n)
        g_out.append(g)
        d_out.append(dl.reshape(shp))
        m_out.append(nm.reshape(shp))
        v_out.append(nv.reshape(shp))
    return (loss, grad_x, *g_out, *d_out, *m_out, *v_out)
```

```python
import functools
import math

import jax
import jax.numpy as jnp
from jax import lax
from jax.experimental import pallas as pl
from jax.experimental.pallas import tpu as pltpu

F32 = jnp.float32
BF16 = jnp.bfloat16

N_DEV = 8
N_HEADS = 8
N_META = 16
PAD = 112
ROW0 = PAD + N_META
LN_EPS = 1e-5
NEG_INF = -1e30
LANES = 128
MXU_COLS = 256
FFN_FWD_CHUNKS = 6
FFN_BWD_CHUNKS = 4

ADAM_LR = 0.001
ADAM_B1 = 0.9
ADAM_B2 = 0.999
ADAM_EPS = 1e-08
ADAM_WD = 0.01
ADAM_STEP = 10

ROW_TILES = (640, 128)
LOSS_TILE = 128
BF16_ROWS = 16
F32_ROWS = 8
SUM_ROWS_MAX = 768
ADAM_ROWS_MAX = 256
VMEM_BIG = 56 << 20
VMEM_MID = 40 << 20

ANY = pl.BlockSpec(memory_space=pl.ANY)
MESH = pl.DeviceIdType.MESH


def _row_tile(t):
    for c in ROW_TILES:
        if t % c == 0:
            return c
    raise ValueError(f"no row tile for {t}")


def _dot(a, b):
    return jnp.dot(a, b, preferred_element_type=F32)


def _dot_nt(a, b):
    return lax.dot_general(a, b, (((1,), (1,)), ((), ())), preferred_element_type=F32)


def _dot_tn(a, b):
    return lax.dot_general(a, b, (((0,), (0,)), ((), ())), preferred_element_type=F32)


def _params(sem, vmem):
    return pltpu.CompilerParams(dimension_semantics=sem, vmem_limit_bytes=vmem)


def _ln_fwd(z):
    mu = jnp.mean(z, axis=-1, keepdims=True)
    zc = z - mu
    var = jnp.mean(zc * zc, axis=-1, keepdims=True)
    rstd = lax.rsqrt(var + LN_EPS)
    return zc * rstd, rstd


def _ln_bwd(dh, xhat, rstd, gain):
    dxh = dh * gain
    m1 = jnp.mean(dxh, axis=-1, keepdims=True)
    m2 = jnp.mean(dxh * xhat, axis=-1, keepdims=True)
    dz = rstd * (dxh - m1 - xhat * m2)
    return dz, jnp.sum(dh * xhat, axis=0, keepdims=True), jnp.sum(dh, axis=0, keepdims=True)


def _load_resident(pairs, sems):
    cps = [pltpu.make_async_copy(src, dst, sems.at[k]) for k, (src, dst) in enumerate(pairs)]
    for cp in cps:
        cp.start()
    for cp in cps:
        cp.wait()


def _peer_ids():
    mx, my, mc = lax.axis_index("x"), lax.axis_index("y"), lax.axis_index("c")
    peers = []
    for kk in range(1, N_DEV):
        px = 1 - mx if (kk >> 2) & 1 else mx
        py = 1 - my if (kk >> 1) & 1 else my
        pc = 1 - mc if kk & 1 else mc
        peers.append(((px, py, pc), 4 * px + 2 * py + pc))
    return 4 * mx + 2 * my + mc, peers


def _exchange_copies(jobs, send_sems, recv_sems, local_sems, starting):
    me_id, peers = _peer_ids()
    for n, (gather, src, dst) in enumerate(jobs):
        own = pltpu.make_async_copy(src if gather else src.at[me_id], dst.at[me_id], local_sems.at[n])
        own.start() if starting else own.wait()
        for k, (dev, pid) in enumerate(peers):
            sem = (N_DEV - 1) * n + k
            out = src if gather else src.at[pid]
            send = pltpu.make_async_remote_copy(
                src_ref=out, dst_ref=dst.at[me_id], send_sem=send_sems.at[sem], recv_sem=recv_sems.at[sem],
                device_id=dev, device_id_type=MESH)
            if starting:
                send.start()
            else:
                pltpu.make_async_remote_copy(
                    src_ref=out, dst_ref=dst.at[pid], send_sem=send_sems.at[sem], recv_sem=recv_sems.at[sem],
                    device_id=dev, device_id_type=MESH).wait_recv()
                send.wait_send()


def _carried(body, n_in, n_out, carry, first, last):
    nj = len(carry)
    if nj == 0:
        return body

    def wrapped(*refs):
        ins, srcs = refs[:n_in], refs[n_in:n_in + nj]
        outs = refs[n_in + nj:n_in + nj + n_out]
        dsts = refs[n_in + nj + n_out:n_in + 2 * nj + n_out]
        scratch, sems = refs[n_in + 2 * nj + n_out:-3], refs[-3:]
        jobs = [(g, s, r) for (g, _), s, r in zip(carry, srcs, dsts)]

        @pl.when(first())
        def _():
            _exchange_copies(jobs, *sems, starting=True)

        body(*ins, *outs, *scratch)

        @pl.when(last())
        def _():
            _exchange_copies(jobs, *sems, starting=False)

    return wrapped


def _carry_shapes(carry):
    return [jax.ShapeDtypeStruct((N_DEV,) + a.shape if g else a.shape, a.dtype) for g, a in carry]


def _carry_scratch(carry):
    if not carry:
        return []
    n = len(carry)
    return [pltpu.SemaphoreType.DMA(((N_DEV - 1) * n,)), pltpu.SemaphoreType.DMA(((N_DEV - 1) * n,)),
            pltpu.SemaphoreType.DMA((n,))]


def _exchange(parts, name):
    carry = [(False, a) for a in parts]
    n = len(parts)

    def body(*refs):
        jobs = [(False, s, r) for s, r in zip(refs[:n], refs[n:2 * n])]
        _exchange_copies(jobs, *refs[2 * n:], starting=True)
        _exchange_copies(jobs, *refs[2 * n:], starting=False)

    return pl.pallas_call(
        body, name=name, in_specs=[ANY] * n, out_specs=[ANY] * n, out_shape=_carry_shapes(carry),
        scratch_shapes=_carry_scratch(carry),
    )(*parts)


def _ffn_fwd(xh, gi, bi, wg, wu, wd, go, bo, alpha, name, carry=()):
    t, d = xh.shape
    nch, _, fc = wg.shape
    f = nch * fc
    per = min(FFN_FWD_CHUNKS, nch)
    nc = -(-nch // per)
    tm = _row_tile(t)
    nt = t // tm

    def body(xh_ref, gi_ref, bi_ref, wg_hbm, wu_hbm, wd_hbm, go_ref, bo_ref,
             xo_ref, rs_ref, hb_ref, g_ref, u_ref,
             wg_v, wu_v, wd_v, acc, hbs, sems):
        i = pl.program_id(0)
        c = pl.program_id(1)

        @pl.when((i == 0) & (c == 0))
        def _():
            _load_resident([(wg_hbm, wg_v), (wu_hbm, wu_v), (wd_hbm, wd_v)], sems)

        @pl.when(c == 0)
        def _():
            h = xh_ref[...] * gi_ref[...] + bi_ref[...]
            hbs[...] = h.astype(BF16)
            acc[...] = jnp.zeros_like(acc)

        def chunk(k):
            ck = c * per + k
            cols = slice(k * fc, (k + 1) * fc)
            hb = hbs[...]
            g = _dot(hb, wg_v[ck])
            u = _dot(hb, wu_v[ck])
            a = (g * jax.nn.sigmoid(g)) * u
            g_ref[:, cols] = g.astype(BF16)
            u_ref[:, cols] = u.astype(BF16)
            acc[...] += _dot(a.astype(BF16), wd_v[ck])

        for k in range(per):
            if (nc - 1) * per + k < nch:
                chunk(k)
            else:
                pl.when(c * per + k < nch)(functools.partial(chunk, k))

        @pl.when(c == nc - 1)
        def _():
            h = xh_ref[...] * gi_ref[...] + bi_ref[...]
            xhat, rstd = _ln_fwd(alpha * h + 0.5 * acc[...])
            xo_ref[...] = xhat
            rs_ref[...] = rstd
            hb_ref[...] = (xhat * go_ref[...] + bo_ref[...]).astype(BF16).T

    row = pl.BlockSpec((tm, d), lambda i, c: (i, 0))
    vec = pl.BlockSpec((1, d), lambda i, c: (0, 0))
    chunk = pl.BlockSpec((tm, per * fc), lambda i, c: (i, c))
    first = lambda: (pl.program_id(0) == 0) & (pl.program_id(1) == 0)
    last = lambda: (pl.program_id(0) == nt - 1) & (pl.program_id(1) == nc - 1)
    return pl.pallas_call(
        _carried(body, 8, 5, carry, first, last), name=name, grid=(nt, nc),
        in_specs=[row, vec, vec, ANY, ANY, ANY, vec, vec] + [ANY] * len(carry),
        out_specs=[row, pl.BlockSpec((tm, 1), lambda i, c: (i, 0)),
                   pl.BlockSpec((d, tm), lambda i, c: (0, i)), chunk, chunk] + [ANY] * len(carry),
        out_shape=[jax.ShapeDtypeStruct((t, d), F32), jax.ShapeDtypeStruct((t, 1), F32),
                   jax.ShapeDtypeStruct((d, t), BF16), jax.ShapeDtypeStruct((t, f), BF16),
                   jax.ShapeDtypeStruct((t, f), BF16)] + _carry_shapes(carry),
        scratch_shapes=[pltpu.VMEM((nch, d, fc), BF16), pltpu.VMEM((nch, d, fc), BF16),
                        pltpu.VMEM((nch, fc, d), BF16), pltpu.VMEM((tm, d), F32),
                        pltpu.VMEM((tm, d), BF16), pltpu.SemaphoreType.DMA((3,))] + _carry_scratch(carry),
        compiler_params=_params(("arbitrary", "arbitrary"), VMEM_BIG),
    )(xh, gi, bi, wg, wu, wd, go, bo, *[a for _, a in carry])


def _ffn_bwd(dh, xo, rs, go, gs, us, wg, wu, wd, alpha, name, carry=()):
    t, d = dh.shape
    nch, _, fc = wg.shape
    f = nch * fc
    per = min(FFN_BWD_CHUNKS, nch)
    nc = -(-nch // per)
    tm = _row_tile(t)
    nt = t // tm

    def body(dh_ref, xo_ref, rs_ref, go_ref, g_ref, u_ref, wg_hbm, wu_hbm, wd_hbm,
             dhin_ref, dot_ref, dg_ref, du_ref, a_ref, dgain_ref, dbias_ref,
             wg_v, wu_v, wd_v, do_ref, sems):
        i = pl.program_id(0)
        c = pl.program_id(1)

        @pl.when((i == 0) & (c == 0))
        def _():
            _load_resident([(wg_hbm, wg_v), (wu_hbm, wu_v), (wd_hbm, wd_v)], sems)
            dgain_ref[...] = jnp.zeros_like(dgain_ref)
            dbias_ref[...] = jnp.zeros_like(dbias_ref)

        @pl.when(c == 0)
        def _():
            dz, dgp, dbp = _ln_bwd(dh_ref[...], xo_ref[...], rs_ref[...], go_ref[...])
            dgain_ref[...] += dgp
            dbias_ref[...] += dbp
            dob = (0.5 * dz).astype(BF16)
            do_ref[...] = dob
            dot_ref[...] = dob.T
            dhin_ref[...] = alpha * dz

        def chunk(k):
            ck = c * per + k
            cols = slice(k * fc, (k + 1) * fc)
            g = g_ref[:, cols].astype(F32)
            u = u_ref[:, cols].astype(F32)
            sg = jax.nn.sigmoid(g)
            sl = g * sg
            da = _dot_nt(do_ref[...], wd_v[ck])
            dgb = (da * u * (sg * (1.0 + g * (1.0 - sg)))).astype(BF16)
            dub = (da * sl).astype(BF16)
            a_ref[:, cols] = (sl * u).astype(BF16)
            dg_ref[:, cols] = dgb
            du_ref[:, cols] = dub
            dhin_ref[...] += _dot_nt(dgb, wg_v[ck]) + _dot_nt(dub, wu_v[ck])

        for k in range(per):
            if (nc - 1) * per + k < nch:
                chunk(k)
            else:
                pl.when(c * per + k < nch)(functools.partial(chunk, k))

    row = pl.BlockSpec((tm, d), lambda i, c: (i, 0))
    vec = pl.BlockSpec((1, d), lambda i, c: (0, 0))
    chunk = pl.BlockSpec((tm, per * fc), lambda i, c: (i, c))
    first = lambda: (pl.program_id(0) == 0) & (pl.program_id(1) == 0)
    last = lambda: (pl.program_id(0) == nt - 1) & (pl.program_id(1) == nc - 1)
    return pl.pallas_call(
        _carried(body, 9, 7, carry, first, last), name=name, grid=(nt, nc),
        in_specs=[row, row, pl.BlockSpec((tm, 1), lambda i, c: (i, 0)), vec, chunk, chunk,
                  ANY, ANY, ANY] + [ANY] * len(carry),
        out_specs=[row, pl.BlockSpec((d, tm), lambda i, c: (0, i)), chunk, chunk, chunk, vec, vec]
                  + [ANY] * len(carry),
        out_shape=[jax.ShapeDtypeStruct((t, d), F32), jax.ShapeDtypeStruct((d, t), BF16),
                   jax.ShapeDtypeStruct((t, f), BF16), jax.ShapeDtypeStruct((t, f), BF16),
                   jax.ShapeDtypeStruct((t, f), BF16), jax.ShapeDtypeStruct((1, d), F32),
                   jax.ShapeDtypeStruct((1, d), F32)] + _carry_shapes(carry),
        scratch_shapes=[pltpu.VMEM((nch, d, fc), BF16), pltpu.VMEM((nch, d, fc), BF16),
                        pltpu.VMEM((nch, fc, d), BF16), pltpu.VMEM((tm, d), BF16),
                        pltpu.SemaphoreType.DMA((3,))] + _carry_scratch(carry),
        compiler_params=_params(("arbitrary", "arbitrary"), VMEM_BIG),
    )(dh, xo, rs, go, gs, us, wg, wu, wd, *[a for _, a in carry])


def _wgrad(xt, ys, name, carry=()):
    m, t = xt.shape
    n = ys[0].shape[1]
    tn = min(n, MXU_COLS)
    ny = len(ys)

    def body(*refs):
        x_hbm = refs[0]
        y_refs = refs[1:1 + ny]
        o_refs = refs[1 + ny:1 + 2 * ny]
        xv, sems = refs[1 + 2 * ny:]

        @pl.when(pl.program_id(0) == 0)
        def _():
            _load_resident([(x_hbm, xv)], sems)

        for y_ref, o_ref in zip(y_refs, o_refs):
            o_ref[...] = _dot(xv[...], y_ref[...].astype(BF16))

    steps = n // tn
    first = lambda: pl.program_id(0) == 0
    last = lambda: pl.program_id(0) == steps - 1
    return pl.pallas_call(
        _carried(body, 1 + ny, ny, carry, first, last), name=name, grid=(steps,),
        in_specs=[ANY] + [pl.BlockSpec((t, tn), lambda c: (0, c)) for _ in ys] + [ANY] * len(carry),
        out_specs=[pl.BlockSpec((m, tn), lambda c: (0, c)) for _ in ys] + [ANY] * len(carry),
        out_shape=[jax.ShapeDtypeStruct((m, n), F32) for _ in ys] + _carry_shapes(carry),
        scratch_shapes=[pltpu.VMEM((m, t), BF16), pltpu.SemaphoreType.DMA((1,))] + _carry_scratch(carry),
        compiler_params=_params(("arbitrary",), VMEM_BIG),
    )(xt, *ys, *[a for _, a in carry])


def _cast_t(h, name):
    t, d = h.shape
    tm = _row_tile(t)

    def body(h_ref, o_ref):
        o_ref[...] = h_ref[...].astype(BF16).T

    return pl.pallas_call(
        body, name=name, grid=(t // tm,),
        in_specs=[pl.BlockSpec((tm, d), lambda i: (i, 0))],
        out_specs=pl.BlockSpec((d, tm), lambda i: (0, i)),
        out_shape=jax.ShapeDtypeStruct((d, t), BF16),
        compiler_params=_params(("arbitrary",), VMEM_MID),
    )(h)


def _shift_rows(u, halo, tm):
    r = lax.broadcasted_iota(jnp.int32, (tm, 1), 0)
    u1 = jnp.where(r == 0, halo[7:8], pltpu.roll(u, 1, 0))
    u2 = jnp.where(r == 0, halo[6:7], jnp.where(r == 1, halo[7:8], pltpu.roll(u, 2, 0)))
    return u1, u2


def _conv_fwd(xh, gi, bi, w_in, cw, w_out, go, bo, alpha, name, carry=()):
    t, d = xh.shape
    tm = _row_tile(t)
    nt = t // tm

    def body(xh_ref, gi_ref, bi_ref, win_ref, cw_ref, wout_ref, go_ref, bo_ref,
             xo_ref, rs_ref, hb_ref, p_ref, m_ref, halo):
        i = pl.program_id(0)

        @pl.when(i == 0)
        def _():
            halo[...] = jnp.zeros_like(halo)

        h = xh_ref[...] * gi_ref[...] + bi_ref[...]
        hb = h.astype(BF16)
        bg = _dot(hb, win_ref[:, 0:d])
        cg = _dot(hb, win_ref[:, d:2 * d])
        val = _dot(hb, win_ref[:, 2 * d:3 * d])
        p_ref[:, 0:d] = bg.astype(BF16)
        p_ref[:, d:2 * d] = cg.astype(BF16)
        p_ref[:, 2 * d:3 * d] = val.astype(BF16)
        rows = i * tm + lax.broadcasted_iota(jnp.int32, (tm, 1), 0)
        u = jnp.where(rows >= PAD, cg * val, 0.0)
        u1, u2 = _shift_rows(u, halo[...], tm)
        halo[...] = u[tm - 8:tm]
        y = cw_ref[0:1] * u2 + cw_ref[1:2] * u1 + cw_ref[2:3] * u
        mb = (bg * y).astype(BF16)
        m_ref[...] = mb.T
        xhat, rstd = _ln_fwd(alpha * h + _dot(mb, wout_ref[...]))
        xo_ref[...] = xhat
        rs_ref[...] = rstd
        hb_ref[...] = (xhat * go_ref[...] + bo_ref[...]).astype(BF16).T

    row = pl.BlockSpec((tm, d), lambda i: (i, 0))
    col = pl.BlockSpec((d, tm), lambda i: (0, i))
    vec = pl.BlockSpec((1, d), lambda i: (0, 0))
    first = lambda: pl.program_id(0) == 0
    last = lambda: pl.program_id(0) == nt - 1
    return pl.pallas_call(
        _carried(body, 8, 5, carry, first, last), name=name, grid=(nt,),
        in_specs=[row, vec, vec, pl.BlockSpec((d, 3 * d), lambda i: (0, 0)),
                  pl.BlockSpec((3, d), lambda i: (0, 0)), pl.BlockSpec((d, d), lambda i: (0, 0)),
                  vec, vec] + [ANY] * len(carry),
        out_specs=[row, pl.BlockSpec((tm, 1), lambda i: (i, 0)), col,
                   pl.BlockSpec((tm, 3 * d), lambda i: (i, 0)), col] + [ANY] * len(carry),
        out_shape=[jax.ShapeDtypeStruct((t, d), F32), jax.ShapeDtypeStruct((t, 1), F32),
                   jax.ShapeDtypeStruct((d, t), BF16), jax.ShapeDtypeStruct((t, 3 * d), BF16),
                   jax.ShapeDtypeStruct((d, t), BF16)] + _carry_shapes(carry),
        scratch_shapes=[pltpu.VMEM((8, d), F32)] + _carry_scratch(carry),
        compiler_params=_params(("arbitrary",), VMEM_BIG),
    )(xh, gi, bi, w_in, cw, w_out, go, bo, *[a for _, a in carry])


def _conv_bwd(dh, xo, rs, go, p, cw, w_in, w_out, alpha, name):
    t, d = dh.shape
    tm = _row_tile(t)
    nt = t // tm
    tb = tm // 8

    def body(dh_ref, xo_ref, rs_ref, go_ref, p_ref, ph_ref, cw_ref, win_ref, wout_ref,
             dhin_ref, dmix_ref, dp_ref, dcw_ref, dgain_ref, dbias_ref, carry):
        i = pl.program_id(0)
        tile = nt - 1 - i

        @pl.when(i == 0)
        def _():
            carry[...] = jnp.zeros_like(carry)
            dcw_ref[...] = jnp.zeros_like(dcw_ref)
            dgain_ref[...] = jnp.zeros_like(dgain_ref)
            dbias_ref[...] = jnp.zeros_like(dbias_ref)

        dz, dgp, dbp = _ln_bwd(dh_ref[...], xo_ref[...], rs_ref[...], go_ref[...])
        dgain_ref[...] += dgp
        dbias_ref[...] += dbp
        dmixb = dz.astype(BF16)
        dmix_ref[...] = dmixb
        dm = _dot_nt(dmixb, wout_ref[...])

        bg = p_ref[:, 0:d].astype(F32)
        cg = p_ref[:, d:2 * d].astype(F32)
        val = p_ref[:, 2 * d:3 * d].astype(F32)
        rows = tile * tm + lax.broadcasted_iota(jnp.int32, (tm, 1), 0)
        valid = rows >= PAD
        u = jnp.where(valid, cg * val, 0.0)
        hrows = tile * tm - 8 + lax.broadcasted_iota(jnp.int32, (8, 1), 0)
        hu = jnp.where((hrows >= PAD) & (tile > 0),
                       ph_ref[:, d:2 * d].astype(F32) * ph_ref[:, 2 * d:3 * d].astype(F32), 0.0)
        u1, u2 = _shift_rows(u, hu, tm)
        w0, w1, w2 = cw_ref[0:1], cw_ref[1:2], cw_ref[2:3]
        y = w0 * u2 + w1 * u1 + w2 * u
        dbg = dm * y
        dy = dm * bg
        dcw_ref[0:1] += jnp.sum(dy * u2, axis=0, keepdims=True)
        dcw_ref[1:2] += jnp.sum(dy * u1, axis=0, keepdims=True)
        dcw_ref[2:3] += jnp.sum(dy * u, axis=0, keepdims=True)

        nxt = carry[...]
        r = lax.broadcasted_iota(jnp.int32, (tm, 1), 0)
        dy1 = jnp.where(r == tm - 1, nxt[0:1], pltpu.roll(dy, tm - 1, 0))
        dy2 = jnp.where(r == tm - 2, nxt[0:1],
                        jnp.where(r == tm - 1, nxt[1:2], pltpu.roll(dy, tm - 2, 0)))
        carry[...] = dy[0:8]
        du = jnp.where(valid, w2 * dy + w1 * dy1 + w0 * dy2, 0.0)
        dbgb = dbg.astype(BF16)
        dcgb = (du * val).astype(BF16)
        dvalb = (du * cg).astype(BF16)
        dp_ref[:, 0:d] = dbgb
        dp_ref[:, d:2 * d] = dcgb
        dp_ref[:, 2 * d:3 * d] = dvalb
        dhin_ref[...] = (alpha * dz + _dot_nt(dbgb, win_ref[:, 0:d])
                         + _dot_nt(dcgb, win_ref[:, d:2 * d]) + _dot_nt(dvalb, win_ref[:, 2 * d:3 * d]))

    row = pl.BlockSpec((tm, d), lambda i: (nt - 1 - i, 0))
    vec = pl.BlockSpec((1, d), lambda i: (0, 0))
    prow = pl.BlockSpec((tm, 3 * d), lambda i: (nt - 1 - i, 0))
    return pl.pallas_call(
        body, name=name, grid=(nt,),
        in_specs=[row, row, pl.BlockSpec((tm, 1), lambda i: (nt - 1 - i, 0)), vec, prow,
                  pl.BlockSpec((8, 3 * d), lambda i: (jnp.maximum((nt - 1 - i) * tb - 1, 0), 0)),
                  pl.BlockSpec((3, d), lambda i: (0, 0)),
                  pl.BlockSpec((d, 3 * d), lambda i: (0, 0)), pl.BlockSpec((d, d), lambda i: (0, 0))],
        out_specs=[row, row, prow, pl.BlockSpec((3, d), lambda i: (0, 0)), vec, vec],
        out_shape=[jax.ShapeDtypeStruct((t, d), F32), jax.ShapeDtypeStruct((t, d), BF16),
                   jax.ShapeDtypeStruct((t, 3 * d), BF16), jax.ShapeDtypeStruct((3, d), F32),
                   jax.ShapeDtypeStruct((1, d), F32), jax.ShapeDtypeStruct((1, d), F32)],
        scratch_shapes=[pltpu.VMEM((8, d), F32)],
        compiler_params=_params(("arbitrary",), VMEM_BIG),
    )(dh, xo, rs, go, p, p, cw, w_in, w_out)


def _kv_fwd(xh, gi, bi, wk, wv, wf, fb, name):
    t, d = xh.shape
    tm = _row_tile(t)
    nt = t // tm

    def body(xh_ref, gi_ref, bi_ref, wk_ref, wv_ref, wf_ref, fb_ref,
             k_ref, v_ref, lg_ref, c_ref, ct_ref, run):
        i = pl.program_id(0)

        @pl.when(i == 0)
        def _():
            run[...] = jnp.zeros_like(run)

        x = (xh_ref[...] * gi_ref[...] + bi_ref[...]).astype(BF16)
        k_ref[...] = _dot(x, wk_ref[...]).astype(BF16)
        v_ref[...] = _dot(x, wv_ref[...]).astype(BF16)
        logit = _dot(x, wf_ref[...]) + fb_ref[...]
        lg_ref[...] = logit
        logf = jnp.minimum(logit, 0.0) - jnp.log(1.0 + jnp.exp(-jnp.abs(logit)))
        rows = i * tm + lax.broadcasted_iota(jnp.int32, (tm, 1), 0)
        logf = jnp.where(rows >= PAD, logf, 0.0)
        tri = (lax.broadcasted_iota(jnp.int32, (tm, tm), 0)
               >= lax.broadcasted_iota(jnp.int32, (tm, tm), 1)).astype(F32)
        cs = jnp.dot(tri, logf, precision=lax.Precision.HIGHEST, preferred_element_type=F32) + run[...]
        run[...] = cs[tm - 1:tm]
        c_ref[...] = cs
        ct_ref[...] = cs.T

    row = pl.BlockSpec((tm, d), lambda i: (i, 0))
    vec = pl.BlockSpec((1, d), lambda i: (0, 0))
    gate = pl.BlockSpec((tm, LANES), lambda i: (i, 0))
    sq = pl.BlockSpec((d, d), lambda i: (0, 0))
    return pl.pallas_call(
        body, name=name, grid=(nt,),
        in_specs=[row, vec, vec, sq, sq, pl.BlockSpec((d, LANES), lambda i: (0, 0)),
                  pl.BlockSpec((1, LANES), lambda i: (0, 0))],
        out_specs=[row, row, gate, gate, pl.BlockSpec((LANES, tm), lambda i: (0, i))],
        out_shape=[jax.ShapeDtypeStruct((t, d), BF16), jax.ShapeDtypeStruct((t, d), BF16),
                   jax.ShapeDtypeStruct((t, LANES), F32), jax.ShapeDtypeStruct((t, LANES), F32),
                   jax.ShapeDtypeStruct((LANES, t), F32)],
        scratch_shapes=[pltpu.VMEM((1, LANES), F32)],
        compiler_params=_params(("arbitrary",), VMEM_MID),
    )(xh, gi, bi, wk, wv, wf, fb)


def _kv_bwd(dk, dv, dcs, dcq, logit, dh_other, wk, wv, wf, name):
    t, d = dk.shape
    tm = _row_tile(t)
    nt = t // tm

    def body(dk_ref, dv_ref, dcs_ref, dcq_ref, lg_ref, oth_ref, wk_ref, wv_ref, wf_ref,
             dh_ref, dl_ref, dfb_ref, run):
        i = pl.program_id(0)
        tile = nt - 1 - i

        @pl.when(i == 0)
        def _():
            run[...] = jnp.zeros_like(run)
            dfb_ref[...] = jnp.zeros_like(dfb_ref)

        lane = lax.broadcasted_iota(jnp.int32, (tm, LANES), 1)
        dc = dcq_ref[...]
        for hh in range(N_HEADS):
            dc = dc + jnp.where(lane == hh, jnp.sum(dcs_ref[hh], axis=1, keepdims=True), 0.0)
        tri = (lax.broadcasted_iota(jnp.int32, (tm, tm), 0)
               <= lax.broadcasted_iota(jnp.int32, (tm, tm), 1)).astype(F32)
        dlf = jnp.dot(tri, dc, precision=lax.Precision.HIGHEST, preferred_element_type=F32) + run[...]
        run[...] = dlf[0:1]
        rows = tile * tm + lax.broadcasted_iota(jnp.int32, (tm, 1), 0)
        dlogit = jnp.where(rows >= PAD, dlf * jax.nn.sigmoid(-lg_ref[...]), 0.0)
        dfb_ref[...] += jnp.sum(dlogit, axis=0, keepdims=True)
        dlb = dlogit.astype(BF16)
        dl_ref[...] = dlb
        dh_ref[...] = (oth_ref[...] + _dot_nt(dk_ref[...], wk_ref[...])
                       + _dot_nt(dv_ref[...], wv_ref[...]) + _dot_nt(dlb, wf_ref[...]))

    row = pl.BlockSpec((tm, d), lambda i: (nt - 1 - i, 0))
    gate = pl.BlockSpec((tm, LANES), lambda i: (nt - 1 - i, 0))
    sq = pl.BlockSpec((d, d), lambda i: (0, 0))
    return pl.pallas_call(
        body, name=name, grid=(nt,),
        in_specs=[row, row, pl.BlockSpec((N_HEADS, tm, LANES), lambda i: (0, nt - 1 - i, 0)), gate, gate, row,
                  sq, sq, pl.BlockSpec((d, LANES), lambda i: (0, 0))],
        out_specs=[row, gate, pl.BlockSpec((1, LANES), lambda i: (0, 0))],
        out_shape=[jax.ShapeDtypeStruct((t, d), F32), jax.ShapeDtypeStruct((t, LANES), BF16),
                   jax.ShapeDtypeStruct((1, LANES), F32)],
        scratch_shapes=[pltpu.VMEM((1, LANES), F32)],
        compiler_params=_params(("arbitrary",), VMEM_MID),
    )(dk, dv, dcs, dcq, logit, dh_other, wk, wv, wf)


def _proj(xh, gi, bi, w, name):
    t, k = xh.shape
    n = w.shape[1]
    tm = _row_tile(t)

    def body(x_ref, g_ref, b_ref, w_ref, o_ref):
        x = (x_ref[...] * g_ref[...] + b_ref[...]).astype(BF16)
        o_ref[...] = _dot(x, w_ref[...]).astype(BF16)

    vec = pl.BlockSpec((1, k), lambda i: (0, 0))
    return pl.pallas_call(
        body, name=name, grid=(t // tm,),
        in_specs=[pl.BlockSpec((tm, k), lambda i: (i, 0)), vec, vec, pl.BlockSpec((k, n), lambda i: (0, 0))],
        out_specs=pl.BlockSpec((tm, n), lambda i: (i, 0)),
        out_shape=jax.ShapeDtypeStruct((t, n), BF16),
        compiler_params=_params(("arbitrary",), VMEM_MID),
    )(xh, gi, bi, w)


def _add_proj_nt(base, y, w, name):
    t, n = y.shape
    k = w.shape[0]
    tm = _row_tile(t)

    def body(b_ref, y_ref, w_ref, o_ref):
        o_ref[...] = b_ref[...] + _dot_nt(y_ref[...].astype(BF16), w_ref[...])

    return pl.pallas_call(
        body, name=name, grid=(t // tm,),
        in_specs=[pl.BlockSpec((tm, k), lambda i: (i, 0)), pl.BlockSpec((tm, n), lambda i: (i, 0)),
                  pl.BlockSpec((k, n), lambda i: (0, 0))],
        out_specs=pl.BlockSpec((tm, k), lambda i: (i, 0)),
        out_shape=jax.ShapeDtypeStruct((t, k), F32),
        compiler_params=_params(("arbitrary",), VMEM_MID),
    )(base, y, w)


def _attn_out_fwd(ot, xh, gi, bi, w_o, go, bo, alpha, name):
    t, d = xh.shape
    tm = _row_tile(t)

    def body(ot_ref, xh_ref, gi_ref, bi_ref, wo_ref, go_ref, bo_ref, xo_ref, rs_ref, hb_ref):
        h = xh_ref[...] * gi_ref[...] + bi_ref[...]
        xhat, rstd = _ln_fwd(alpha * h + _dot_tn(ot_ref[...], wo_ref[...]))
        xo_ref[...] = xhat
        rs_ref[...] = rstd
        hb_ref[...] = (xhat * go_ref[...] + bo_ref[...]).astype(BF16).T

    row = pl.BlockSpec((tm, d), lambda i: (i, 0))
    col = pl.BlockSpec((d, tm), lambda i: (0, i))
    vec = pl.BlockSpec((1, d), lambda i: (0, 0))
    return pl.pallas_call(
        body, name=name, grid=(t // tm,),
        in_specs=[col, row, vec, vec, pl.BlockSpec((d, d), lambda i: (0, 0)), vec, vec],
        out_specs=[row, pl.BlockSpec((tm, 1), lambda i: (i, 0)), col],
        out_shape=[jax.ShapeDtypeStruct((t, d), F32), jax.ShapeDtypeStruct((t, 1), F32),
                   jax.ShapeDtypeStruct((d, t), BF16)],
        compiler_params=_params(("arbitrary",), VMEM_MID),
    )(ot, xh, gi, bi, w_o, go, bo)


def _attn_out_bwd(dh, xo, rs, go, ot, w_o, alpha, name):
    t, d = dh.shape
    tm = _row_tile(t)
    hd = d // N_HEADS

    def body(dh_ref, xo_ref, rs_ref, go_ref, ot_ref, wo_ref,
             dres_ref, dmix_ref, dot_ref, delta_ref, dgain_ref, dbias_ref):
        @pl.when(pl.program_id(0) == 0)
        def _():
            dgain_ref[...] = jnp.zeros_like(dgain_ref)
            dbias_ref[...] = jnp.zeros_like(dbias_ref)

        dz, dgp, dbp = _ln_bwd(dh_ref[...], xo_ref[...], rs_ref[...], go_ref[...])
        dgain_ref[...] += dgp
        dbias_ref[...] += dbp
        dres_ref[...] = alpha * dz
        dmixb = dz.astype(BF16)
        dmix_ref[...] = dmixb
        dot_t = _dot_nt(wo_ref[...], dmixb)
        dot_ref[...] = dot_t.astype(BF16)
        prod = dot_t * ot_ref[...].astype(F32)
        delta_ref[...] = jnp.sum(prod.reshape(N_HEADS, hd, tm), axis=1)

    row = pl.BlockSpec((tm, d), lambda i: (i, 0))
    vec = pl.BlockSpec((1, d), lambda i: (0, 0))
    col = pl.BlockSpec((d, tm), lambda i: (0, i))
    return pl.pallas_call(
        body, name=name, grid=(t // tm,),
        in_specs=[row, row, pl.BlockSpec((tm, 1), lambda i: (i, 0)), vec, col,
                  pl.BlockSpec((d, d), lambda i: (0, 0))],
        out_specs=[row, row, col, pl.BlockSpec((N_HEADS, tm), lambda i: (0, i)), vec, vec],
        out_shape=[jax.ShapeDtypeStruct((t, d), F32), jax.ShapeDtypeStruct((t, d), BF16),
                   jax.ShapeDtypeStruct((d, t), BF16), jax.ShapeDtypeStruct((N_HEADS, t), F32),
                   jax.ShapeDtypeStruct((1, d), F32), jax.ShapeDtypeStruct((1, d), F32)],
        compiler_params=_params(("arbitrary",), VMEM_MID),
    )(dh, xo, rs, go, ot, w_o)


def _scores_t(k, q, ct_ref, c_ref, h, i, j, tq, tk, scale, masked):
    st = _dot_nt(k, q) * scale
    sub = lax.broadcasted_iota(jnp.int32, (8, tq), 0)
    cq = jnp.sum(jnp.where(sub == h, ct_ref[...], 0.0), axis=0, keepdims=True)
    lane = lax.broadcasted_iota(jnp.int32, (tk, LANES), 1)
    ck = jnp.sum(jnp.where(lane == h, c_ref[...], 0.0), axis=1, keepdims=True)
    st = st + cq - ck
    if not masked:
        return st
    kpos = j * tk + lax.broadcasted_iota(jnp.int32, (tk, 1), 0)
    qpos = i * tq + lax.broadcasted_iota(jnp.int32, (1, tq), 1)
    return jnp.where((kpos <= qpos) & (kpos >= PAD), st, NEG_INF)


def _tri_pairs(n, by_row):
    if by_row:
        pairs = [(i, j) for i in range(n) for j in range(i + 1)]
    else:
        pairs = [(i, j) for j in range(n) for i in range(j, n)]
    return (jnp.asarray([p[0] for p in pairs], jnp.int32), jnp.asarray([p[1] for p in pairs], jnp.int32))


def _attn_fwd(q, k, v, c, ct, name, carry=()):
    t, d = q.shape
    hd = d // N_HEADS
    tq = tk = _row_tile(t)
    nq = t // tq
    scale = 1.0 / math.sqrt(hd)

    def body(it_ref, jt_ref, q_ref, k_ref, v_ref, c_ref, ct_ref, ot_ref, lse_ref, m_s, l_s, acc):
        h, p_ = pl.program_id(0), pl.program_id(1)
        i, j = it_ref[p_], jt_ref[p_]

        @pl.when(j == 0)
        def _():
            m_s[...] = jnp.full_like(m_s, NEG_INF)
            l_s[...] = jnp.zeros_like(l_s)
            acc[...] = jnp.zeros_like(acc)

        def update(masked):
            st = _scores_t(k_ref[...], q_ref[...], ct_ref, c_ref, h, i, j, tq, tk, scale, masked)
            m_new = jnp.maximum(m_s[...], jnp.max(st, axis=0, keepdims=True))
            a = jnp.exp(m_s[...] - m_new)
            p = jnp.exp(st - m_new)
            l_s[...] = a * l_s[...] + jnp.sum(p, axis=0, keepdims=True)
            acc[...] = a * acc[...] + _dot_tn(v_ref[...], p.astype(BF16))
            m_s[...] = m_new

        edge = (j == i) | (j == 0)
        pl.when(edge)(lambda: update(True))
        pl.when(jnp.logical_not(edge))(lambda: update(False))

        @pl.when(j == i)
        def _():
            ot_ref[...] = (acc[...] / l_s[...]).astype(BF16)
            lse_ref[0] = m_s[...] + jnp.log(l_s[...])

    it, jt = _tri_pairs(nq, by_row=True)
    npairs = it.shape[0]
    kv = pl.BlockSpec((tk, hd), lambda h, p, it, jt: (jt[p], h))
    first = lambda: (pl.program_id(0) == 0) & (pl.program_id(1) == 0)
    last = lambda: (pl.program_id(0) == N_HEADS - 1) & (pl.program_id(1) == npairs - 1)
    return pl.pallas_call(
        _carried(body, 7, 2, carry, first, last), name=name,
        grid_spec=pltpu.PrefetchScalarGridSpec(
            num_scalar_prefetch=2, grid=(N_HEADS, npairs),
            in_specs=[pl.BlockSpec((tq, hd), lambda h, p, it, jt: (it[p], h)), kv, kv,
                      pl.BlockSpec((tk, LANES), lambda h, p, it, jt: (jt[p], 0)),
                      pl.BlockSpec((8, tq), lambda h, p, it, jt: (0, it[p]))] + [ANY] * len(carry),
            out_specs=[pl.BlockSpec((hd, tq), lambda h, p, it, jt: (h, it[p])),
                       pl.BlockSpec((1, 1, tq), lambda h, p, it, jt: (h, 0, it[p]))] + [ANY] * len(carry),
            scratch_shapes=[pltpu.VMEM((1, tq), F32), pltpu.VMEM((1, tq), F32),
                            pltpu.VMEM((hd, tq), F32)] + _carry_scratch(carry)),
        out_shape=[jax.ShapeDtypeStruct((d, t), BF16), jax.ShapeDtypeStruct((N_HEADS, 1, t), F32)]
                  + _carry_shapes(carry),
        compiler_params=_params(("arbitrary", "arbitrary"), VMEM_MID),
    )(it, jt, q, k, v, c, ct, *[a for _, a in carry])


def _attn_bwd(q, k, v, c, ct, lse, delta, dot_t, name, carry=()):
    t, d = q.shape
    hd = d // N_HEADS
    tq = tk = _row_tile(t)
    nq = t // tq
    scale = 1.0 / math.sqrt(hd)

    def body(it_ref, jt_ref, q_ref, k_ref, v_ref, c_ref, ct_ref, lse_ref, delta_ref, dot_ref,
             dq_ref, dk_ref, dv_ref, dcs_ref, drow_ref, dk_acc, dv_acc, dc_acc):
        h, p_ = pl.program_id(0), pl.program_id(1)
        i, j = it_ref[p_], jt_ref[p_]

        @pl.when(p_ == 0)
        def _():
            dq_ref[...] = jnp.zeros_like(dq_ref)
            drow_ref[...] = jnp.zeros_like(drow_ref)

        @pl.when(i == j)
        def _():
            dk_acc[...] = jnp.zeros_like(dk_acc)
            dv_acc[...] = jnp.zeros_like(dv_acc)
            dc_acc[...] = jnp.zeros_like(dc_acc)

        def update(masked):
            qv, kv_, vv = q_ref[...], k_ref[...], v_ref[...]
            st = _scores_t(kv_, qv, ct_ref, c_ref, h, i, j, tq, tk, scale, masked)
            p = jnp.exp(st - lse_ref[0])
            do_t = dot_ref[...]
            dp = _dot(vv, do_t)
            sub = lax.broadcasted_iota(jnp.int32, (8, tq), 0)
            dl = jnp.sum(jnp.where(sub == h, delta_ref[...], 0.0), axis=0, keepdims=True)
            ds = p * (dp - dl)
            dsb = ds.astype(BF16)
            dv_acc[...] += _dot_nt(p.astype(BF16), do_t)
            dk_acc[...] += _dot(dsb, qv) * scale
            rows = pl.ds(pl.multiple_of(i * tq, tq), tq)
            dq_ref[rows, :] += _dot_tn(dsb, kv_) * scale
            part = ds[:, 0:LANES]
            for g in range(1, tq // LANES):
                part = part + ds[:, g * LANES:(g + 1) * LANES]
            dc_acc[...] += part
            drow_ref[0, i] += jnp.broadcast_to(jnp.sum(ds, axis=0, keepdims=True), (8, tq))

        edge = (j == i) | (j == 0)
        pl.when(edge)(lambda: update(True))
        pl.when(jnp.logical_not(edge))(lambda: update(False))

        @pl.when(i == nq - 1)
        def _():
            dk_ref[...] = dk_acc[...].astype(BF16)
            dv_ref[...] = dv_acc[...].astype(BF16)
            dcs_ref[0] = -dc_acc[...]

    it, jt = _tri_pairs(nq, by_row=False)
    npairs = it.shape[0]
    kv = pl.BlockSpec((tk, hd), lambda h, p, it, jt: (jt[p], h))
    first = lambda: (pl.program_id(0) == 0) & (pl.program_id(1) == 0)
    last = lambda: (pl.program_id(0) == N_HEADS - 1) & (pl.program_id(1) == npairs - 1)
    return pl.pallas_call(
        _carried(body, 10, 5, carry, first, last), name=name,
        grid_spec=pltpu.PrefetchScalarGridSpec(
            num_scalar_prefetch=2, grid=(N_HEADS, npairs),
            in_specs=[pl.BlockSpec((tq, hd), lambda h, p, it, jt: (it[p], h)), kv, kv,
                      pl.BlockSpec((tk, LANES), lambda h, p, it, jt: (jt[p], 0)),
                      pl.BlockSpec((8, tq), lambda h, p, it, jt: (0, it[p])),
                      pl.BlockSpec((1, 1, tq), lambda h, p, it, jt: (h, 0, it[p])),
                      pl.BlockSpec((N_HEADS, tq), lambda h, p, it, jt: (0, it[p])),
                      pl.BlockSpec((hd, tq), lambda h, p, it, jt: (h, it[p]))] + [ANY] * len(carry),
            out_specs=[pl.BlockSpec((t, hd), lambda h, p, it, jt: (0, h)), kv, kv,
                       pl.BlockSpec((1, tk, LANES), lambda h, p, it, jt: (h, jt[p], 0)),
                       pl.BlockSpec((1, nq, 8, tq), lambda h, p, it, jt: (h, 0, 0, 0))] + [ANY] * len(carry),
            scratch_shapes=[pltpu.VMEM((tk, hd), F32), pltpu.VMEM((tk, hd), F32),
                            pltpu.VMEM((tk, LANES), F32)] + _carry_scratch(carry)),
        out_shape=[jax.ShapeDtypeStruct((t, d), F32), jax.ShapeDtypeStruct((t, d), BF16),
                   jax.ShapeDtypeStruct((t, d), BF16), jax.ShapeDtypeStruct((N_HEADS, t, LANES), F32),
                   jax.ShapeDtypeStruct((N_HEADS, nq, 8, tq), F32)] + _carry_shapes(carry),
        compiler_params=_params(("arbitrary", "arbitrary"), VMEM_MID),
    )(it, jt, q, k, v, c, ct, lse, delta, dot_t, *[a for _, a in carry])


def _loss_head(xh, g, b, target, name):
    t, d = xh.shape
    tm = LOSS_TILE
    nt = t // tm
    lead = ROW0 // tm

    def body(xh_ref, g_ref, b_ref, tg_ref, dh_ref, loss_ref, part):
        i = pl.program_id(0)

        @pl.when(i == 0)
        def _():
            part[...] = jnp.zeros_like(part)

        @pl.when(i < lead)
        def _():
            dh_ref[...] = jnp.zeros_like(dh_ref)

        @pl.when(i >= lead)
        def _():
            e = xh_ref[...] * g_ref[...] + b_ref[...] - tg_ref[...]
            dh_ref[...] = e * (1.0 / d)
            part[...] += jnp.sum(e * e, axis=0, keepdims=True)

        @pl.when(i == nt - 1)
        def _():
            loss_ref[...] = jnp.full((1, LANES), 0.5 / d, F32) * jnp.sum(part[...])

    return pl.pallas_call(
        body, name=name, grid=(nt,),
        in_specs=[pl.BlockSpec((tm, d), lambda i: (i, 0)), pl.BlockSpec((1, d), lambda i: (0, 0)),
                  pl.BlockSpec((1, d), lambda i: (0, 0)),
                  pl.BlockSpec((tm, d), lambda i: (jnp.maximum(i - lead, 0), 0))],
        out_specs=[pl.BlockSpec((tm, d), lambda i: (i, 0)), pl.BlockSpec((1, LANES), lambda i: (0, 0))],
        out_shape=[jax.ShapeDtypeStruct((t, d), F32), jax.ShapeDtypeStruct((1, LANES), F32)],
        scratch_shapes=[pltpu.VMEM((1, d), F32)],
        compiler_params=_params(("arbitrary",), VMEM_MID),
    )(xh, g, b, target)


def _adamw(w, g, m, v, name):
    r, c = w.shape
    tr = r
    for cand in (256, 128, 64, 32, 16, 8):
        if r % cand == 0 and r > cand:
            tr = cand
            break
    bc1 = 1.0 - ADAM_B1 ** ADAM_STEP
    bc2 = 1.0 - ADAM_B2 ** ADAM_STEP

    def body(w_ref, g_ref, m_ref, v_ref, d_ref, nm_ref, nv_ref):
        gg = g_ref[...]
        nm = ADAM_B1 * m_ref[...] + (1.0 - ADAM_B1) * gg
        nv = ADAM_B2 * v_ref[...] + (1.0 - ADAM_B2) * (gg * gg)
        d_ref[...] = -ADAM_LR * ((nm / bc1) / (jnp.sqrt(nv / bc2) + ADAM_EPS) + ADAM_WD * w_ref[...])
        nm_ref[...] = nm
        nv_ref[...] = nv

    blk = pl.BlockSpec((tr, c), lambda i: (i, 0))
    shp = jax.ShapeDtypeStruct((r, c), F32)
    return pl.pallas_call(
        body, name=name, grid=(r // tr,), in_specs=[blk] * 4, out_specs=[blk] * 3,
        out_shape=[shp] * 3, compiler_params=_params(("arbitrary",), VMEM_MID),
    )(w, g, m, v)


def _reduce_adamw(w, m, v, landed, name):
    nl, r, c = w.shape
    tr = next(cand for cand in range(min(r, ADAM_ROWS_MAX), 0, -BF16_ROWS) if r % cand == 0)
    nr = r // tr
    bc1 = 1.0 - ADAM_B1 ** ADAM_STEP
    bc2 = 1.0 - ADAM_B2 ** ADAM_STEP

    def body(*refs):
        w_ref, m_ref, v_ref = refs[:3]
        src_refs = refs[3:3 + nl]
        g_ref, d_ref, nm_ref, nv_ref = refs[3 + nl:]

        def update(src):
            gg = src[0].astype(F32)
            for s in range(1, N_DEV):
                gg = gg + src[s].astype(F32)
            nm = ADAM_B1 * m_ref[0] + (1.0 - ADAM_B1) * gg
            nv = ADAM_B2 * v_ref[0] + (1.0 - ADAM_B2) * (gg * gg)
            g_ref[0] = gg
            d_ref[0] = -ADAM_LR * ((nm / bc1) / (jnp.sqrt(nv / bc2) + ADAM_EPS) + ADAM_WD * w_ref[0])
            nm_ref[0] = nm
            nv_ref[0] = nv

        for idx in range(nl):
            pl.when(pl.program_id(0) == idx)(functools.partial(update, src_refs[idx]))

    def src_spec(idx):
        return pl.BlockSpec((N_DEV, tr, c),
                            lambda l, i: (0, jnp.where(l == idx, i, jnp.where(l < idx, 0, nr - 1)), 0))

    blk = pl.BlockSpec((1, tr, c), lambda l, i: (l, i, 0))
    shp = jax.ShapeDtypeStruct((nl, r, c), F32)
    return pl.pallas_call(
        body, name=name, grid=(nl, nr), in_specs=[blk] * 3 + [src_spec(idx) for idx in range(nl)],
        out_specs=[blk] * 4, out_shape=[shp] * 4,
        compiler_params=_params(("arbitrary", "arbitrary"), VMEM_MID),
    )(w, m, v, *landed)


def _sum_sources(r, name):
    n, rows, c = r.shape
    tr = next(cand for cand in range(min(rows, SUM_ROWS_MAX), 0, -BF16_ROWS) if rows % cand == 0)

    def body(r_ref, o_ref):
        acc = r_ref[0].astype(F32)
        for s in range(1, n):
            acc = acc + r_ref[s].astype(F32)
        o_ref[...] = acc

    return pl.pallas_call(
        body, name=name, grid=(rows // tr,),
        in_specs=[pl.BlockSpec((n, tr, c), lambda i: (0, i, 0))],
        out_specs=pl.BlockSpec((tr, c), lambda i: (i, 0)),
        out_shape=jax.ShapeDtypeStruct((rows, c), F32),
        compiler_params=_params(("arbitrary",), VMEM_MID),
    )(r)


def _all_gather(parts, name):
    n = len(parts)

    def body(*refs):
        x_refs, out_refs = refs[:n], refs[n:2 * n]
        send_sems, recv_sems, local_sems = refs[2 * n:]
        mx, my, mc = lax.axis_index("x"), lax.axis_index("y"), lax.axis_index("c")
        me, sibling = (mx, my, mc), (mx, my, 1 - mc)
        chips = [(1 - mx, my), (mx, 1 - my), (1 - mx, 1 - my)]

        def copy(p, k, block, to, from_input=False):
            px, py, pc = block
            rows = out_refs[p].at[4 * px + 2 * py + pc]
            return pltpu.make_async_remote_copy(
                src_ref=x_refs[p] if from_input else rows, dst_ref=rows,
                send_sem=send_sems.at[7 * p + k], recv_sem=recv_sems.at[7 * p + k],
                device_id=to, device_id_type=MESH)

        mine, sent = [], []
        for p in range(n):
            own = pltpu.make_async_copy(x_refs[p], out_refs[p].at[4 * mx + 2 * my + mc], local_sems.at[p])
            own.start()
            mine.append(own)
            first = [copy(p, 0, me, sibling, True)]
            first += [copy(p, 1 + j, me, (*chip, mc), True) for j, chip in enumerate(chips)]
            for cp in first:
                cp.start()
            sent += first
        for p in range(n):
            for j, chip in enumerate(chips):
                copy(p, 1 + j, (*chip, mc), me).wait_recv()
                fwd = copy(p, 4 + j, (*chip, mc), sibling)
                fwd.start()
                sent.append(fwd)
        for p in range(n):
            copy(p, 0, sibling, me).wait_recv()
            for j, chip in enumerate(chips):
                copy(p, 4 + j, (*chip, 1 - mc), me).wait_recv()
        for cp in sent:
            cp.wait_send()
        for own in mine:
            own.wait()

    return pl.pallas_call(
        body, name=name, in_specs=[ANY] * n, out_specs=[ANY] * n,
        out_shape=[jax.ShapeDtypeStruct((N_DEV,) + a.shape, a.dtype) for a in parts],
        scratch_shapes=[pltpu.SemaphoreType.DMA((7 * n,)), pltpu.SemaphoreType.DMA((7 * n,)),
                        pltpu.SemaphoreType.DMA((n,))],
    )(*parts)


def _pack_rows(parts, width, mult, lead=0):
    out = []
    for a in parts:
        head = a.shape[:lead]
        flat = a.reshape(head + (-1,))
        padn = (-flat.shape[-1]) % (width * mult)
        if padn:
            flat = jnp.pad(flat, [(0, 0)] * lead + [(0, padn)])
        out.append(flat.reshape(head + (-1, width)))
    return jnp.concatenate(out, axis=lead)


def _rows_of(shape, width, mult):
    n = math.prod(shape)
    per = width * mult
    return ((n + per - 1) // per) * mult


def _unpack_rows(buf, shapes, width, mult):
    lead = buf.shape[:-2]
    out, off = [], 0
    for shp in shapes:
        r = _rows_of(shp, width, mult)
        flat = buf[..., off:off + r, :].reshape(lead + (r * width,))
        out.append(flat[..., :math.prod(shp)].reshape(lead + tuple(shp)))
        off += r
    return out


def _cols_from_devices(g):
    nd = g.ndim
    perm = tuple(range(1, nd - 1)) + (0, nd - 1)
    t = jnp.transpose(g, perm)
    return t.reshape(t.shape[:-2] + (t.shape[-2] * t.shape[-1],))


def _cols_to_devices(a):
    c = a.shape[-1] // N_DEV
    t = a.reshape(a.shape[:-1] + (N_DEV, c))
    nd = t.ndim
    perm = (nd - 2,) + tuple(range(0, nd - 2)) + (nd - 1,)
    return jnp.transpose(t, perm)


WIDTH = 1024


def kernel(x, meta, ffn1_wg, ffn1_wu, ffn1_wd, ffn2_wg, ffn2_wu, ffn2_wd, ln_gain, ln_bias, conv_w_in, conv_w, conv_w_out, kv_w, f_bias, attn_w_q, attn_w_o, loss_target, m_meta, m_ffn1_wg, m_ffn1_wu, m_ffn1_wd, m_ffn2_wg, m_ffn2_wu, m_ffn2_wd, m_ln_gain, m_ln_bias, m_conv_w_in, m_conv_w, m_conv_w_out, m_kv_w, m_f_bias, m_attn_w_q, m_attn_w_o, v_meta, v_ffn1_wg, v_ffn1_wu, v_ffn1_wd, v_ffn2_wg, v_ffn2_wu, v_ffn2_wd, v_ln_gain, v_ln_bias, v_conv_w_in, v_conv_w, v_conv_w_out, v_kv_w, v_f_bias, v_attn_w_q, v_attn_w_o):
    depth = ln_gain.shape[0]
    alpha = float((2 * depth) ** 0.25)
    d = x.shape[-1]
    seq = x.shape[1]
    t = ROW0 + seq
    fsh = ffn1_wg.shape[-1]
    f = fsh * N_DEV
    fck = MXU_COLS
    nc = f // fck
    me = 4 * lax.axis_index("x") + 2 * lax.axis_index("y") + lax.axis_index("c")

    def gather_of(parts):
        return [(True, a.astype(BF16)) for a in parts]

    small = [meta, ln_gain, ln_bias, conv_w]
    small_shapes = [a.shape for a in small]
    g1g, g1u, g1d, gcin, gcout = _all_gather(
        [a.astype(BF16) for a in (ffn1_wg[0], ffn1_wu[0], ffn1_wd[0], conv_w_in[0], conv_w_out[0])], "ag_first")
    (gsmall,) = _all_gather([_pack_rows(small, WIDTH, F32_ROWS)], "ag_small")
    gmeta, ggain, gbias, gcw = _unpack_rows(gsmall, small_shapes, WIDTH, F32_ROWS)

    def ffn_chunks(gg, gu, gd):
        up = lambda g: jnp.transpose(_cols_from_devices(g).reshape(d, nc, fck), (1, 0, 2))
        return up(gg), up(gu), gd.reshape(nc, fck, d)

    w_in = _cols_from_devices(gcin)
    w_out = gcout.reshape(d, d)
    fb = jnp.pad(f_bias, (0, LANES - N_HEADS)).reshape(1, LANES)
    meta_f = _cols_from_devices(gmeta)
    gain_f = _cols_from_devices(ggain)
    bias_f = _cols_from_devices(gbias)
    cw_f = _cols_from_devices(gcw)[0]

    def gb(l, n):
        return gain_f[l, n].reshape(1, d), bias_f[l, n].reshape(1, d)

    ones = jnp.ones((1, d), F32)
    zeros = jnp.zeros((1, d), F32)

    h0 = jnp.concatenate([jnp.zeros((PAD, d), F32), meta_f, x[0]], axis=0)
    hb0 = _cast_t(h0, "h0_bf16_t")

    w1 = ffn_chunks(g1g, g1u, g1d)
    g00, b00 = gb(0, 0)
    xh1, rs1, hb1, gg1, uu1, g2g, g2u, g2d = _ffn_fwd(
        h0, ones, zeros, *w1, g00, b00, alpha, "ffn_fwd_0a",
        carry=gather_of([ffn2_wg[0], ffn2_wu[0], ffn2_wd[0]]))
    g01, b01 = gb(0, 1)
    xh2, rs2, hb2, pp, mb, gkv, gwq = _conv_fwd(
        xh1, g00, b00, w_in, cw_f, w_out, g01, b01, alpha, "conv_fwd", carry=gather_of([kv_w, attn_w_q[0]]))
    w2 = ffn_chunks(g2g, g2u, g2d)
    g02, b02 = gb(0, 2)
    xh3, rs3, hb3, gg3, uu3, g3g, g3u, g3d = _ffn_fwd(
        xh2, g01, b01, *w2, g02, b02, alpha, "ffn_fwd_0b",
        carry=gather_of([ffn1_wg[1], ffn1_wu[1], ffn1_wd[1]]))
    kvw = _cols_from_devices(gkv)
    wk, wv = kvw[:, :d], kvw[:, d:2 * d]
    wf = jnp.pad(kvw[:, 2 * d:], ((0, 0), (0, LANES - N_HEADS)))
    kk, vv, logit, cc, cct = _kv_fwd(xh3, g02, b02, wk, wv, wf, fb, "kv_fwd")

    w3 = ffn_chunks(g3g, g3u, g3d)
    g10, b10 = gb(1, 0)
    xh4, rs4, hb4, gg4, uu4 = _ffn_fwd(xh3, g02, b02, *w3, g10, b10, alpha, "ffn_fwd_1a")
    w_q = gwq.reshape(d, d)
    qq = _proj(xh4, g10, b10, w_q, "q_proj")
    ot, lse, gwo, g4g, g4u, g4d = _attn_fwd(
        qq, kk, vv, cc, cct, "attn_fwd", carry=gather_of([attn_w_o[0], ffn2_wg[1], ffn2_wu[1], ffn2_wd[1]]))
    w_o = gwo.reshape(d, d)
    g11, b11 = gb(1, 1)
    xh5, rs5, hb5 = _attn_out_fwd(ot, xh4, g10, b10, w_o, g11, b11, alpha, "attn_out_fwd")
    w4 = ffn_chunks(g4g, g4u, g4d)
    g12, b12 = gb(1, 2)
    xh6, rs6, _, gg6, uu6 = _ffn_fwd(xh5, g11, b11, *w4, g12, b12, alpha, "ffn_fwd_1b")

    dh6, loss_l = _loss_head(xh6, g12, b12, loss_target[0], "loss_head")
    loss = lax.psum(loss_l[0, 0], ("x", "y", "c"))

    dgain = [[None] * 3 for _ in range(depth)]
    dbias = [[None] * 3 for _ in range(depth)]

    def to_col_owners(g):
        return (False, _cols_to_devices(g).astype(BF16))

    def to_row_owners(g):
        return (False, g.reshape(N_DEV, g.shape[0] // N_DEV, g.shape[1]).astype(BF16))

    dh5, do6, dg6, du6, a6, dgain[1][2], dbias[1][2] = _ffn_bwd(dh6, xh6, rs6, g12, gg6, uu6, *w4, alpha, "ffn_bwd_1b")
    dw4g, dw4u = _wgrad(hb5, [dg6, du6], "wgrad_up_1b")
    (dw4dt,) = _wgrad(do6, [a6], "wgrad_down_1b")

    dres4, dmix5, dot_t, delta, dgain[1][1], dbias[1][1] = _attn_out_bwd(dh5, xh5, rs5, g11, ot, w_o, alpha, "attn_out_bwd")
    (dwo,) = _wgrad(ot, [dmix5], "wgrad_wo")
    dq, dkk, dvv, dcs, drow, l4g, l4u, l4d, lwo = _attn_bwd(
        qq, kk, vv, cc, cct, lse, delta, dot_t, "attn_bwd",
        carry=[to_col_owners(dw4g), to_col_owners(dw4u), to_row_owners(dw4dt.T), to_row_owners(dwo)])
    dh4 = _add_proj_nt(dres4, dq, w_q, "q_bwd")
    (dwq,) = _wgrad(hb4, [dq], "wgrad_wq")

    dh3a, do4, dg4, du4, a4, dgain[1][0], dbias[1][0] = _ffn_bwd(dh4, xh4, rs4, g10, gg4, uu4, *w3, alpha, "ffn_bwd_1a")
    dw3g, dw3u = _wgrad(hb3, [dg4, du4], "wgrad_up_1a")
    (dw3dt,) = _wgrad(do4, [a4], "wgrad_down_1a")

    dcq = jnp.pad(drow[:, :, 0, :].reshape(N_HEADS, t).T, ((0, 0), (0, LANES - N_HEADS)))
    dh3, dlogit, dfb = _kv_bwd(dkk, dvv, dcs, dcq, logit, dh3a, wk, wv, wf, "kv_bwd")
    dwk, dwv = _wgrad(hb3, [dkk, dvv], "wgrad_kv")
    (dwf,) = _wgrad(hb3, [dlogit], "wgrad_f")
    dkv = jnp.concatenate([dwk, dwv, dwf[:, :N_HEADS]], axis=1)

    dh2, do3, dg3, du3, a3, dgain[0][2], dbias[0][2], lwq, l3g, l3u, l3d, lkv = _ffn_bwd(
        dh3, xh3, rs3, g02, gg3, uu3, *w2, alpha, "ffn_bwd_0b",
        carry=[to_row_owners(dwq), to_col_owners(dw3g), to_col_owners(dw3u), to_row_owners(dw3dt.T),
               to_col_owners(dkv)])
    dw2g, dw2u = _wgrad(hb2, [dg3, du3], "wgrad_up_0b")
    (dw2dt,) = _wgrad(do3, [a3], "wgrad_down_0b")

    dh1, dmix2, dpp, dcw, dgain[0][1], dbias[0][1] = _conv_bwd(dh2, xh2, rs2, g01, pp, cw_f, w_in, w_out, alpha, "conv_bwd")
    (dwin,) = _wgrad(hb1, [dpp], "wgrad_conv_in")
    (dwout,) = _wgrad(mb, [dmix2], "wgrad_conv_out")

    dh0, do1, dg1, du1, a1, dgain[0][0], dbias[0][0], l2g, l2u, l2d, lcin, lcout = _ffn_bwd(
        dh1, xh1, rs1, g00, gg1, uu1, *w1, alpha, "ffn_bwd_0a",
        carry=[to_col_owners(dw2g), to_col_owners(dw2u), to_row_owners(dw2dt.T), to_col_owners(dwin),
               to_row_owners(dwout)])
    (dw1dt,) = _wgrad(do1, [a1], "wgrad_down_0a")
    dw1g, l1d = _wgrad(hb0, [dg1], "wgrad_upg_0a", carry=[to_row_owners(dw1dt.T)])
    dw1u, l1g = _wgrad(hb0, [du1], "wgrad_upu_0a", carry=[to_col_owners(dw1g)])
    (l1u,) = _exchange([to_col_owners(dw1u)[1]], "rs_last")

    grad_x = dh0[ROW0:].reshape(1, seq, d)

    dmeta = dh0[PAD:ROW0]
    dgain_f = jnp.stack([jnp.concatenate(r, axis=0) for r in dgain])
    dbias_f = jnp.stack([jnp.concatenate(r, axis=0) for r in dbias])
    small_full = [dmeta, dgain_f, dbias_f, dcw[None], dfb]
    small_full_shapes = [a.shape for a in small_full]
    (gsmall_grads,) = _all_gather([_pack_rows(small_full, WIDTH, F32_ROWS)], "ag_small_grads")
    rsmall = _sum_sources(gsmall_grads, "small_sum")
    smeta, sgain, sbias, scw, sfb = _unpack_rows(rsmall, small_full_shapes, WIDTH, F32_ROWS)
    csh = d // N_DEV

    def my_cols(a):
        return lax.dynamic_slice_in_dim(a, me * csh, csh, axis=a.ndim - 1)

    grads = {"meta": my_cols(smeta), "ln_gain": my_cols(sgain), "ln_bias": my_cols(sbias),
             "conv_w": my_cols(scw), "f_bias": sfb[0, :N_HEADS]}
    landed = {"ffn1_wg": [l1g, l3g], "ffn1_wu": [l1u, l3u], "ffn1_wd": [l1d, l3d],
              "ffn2_wg": [l2g, l4g], "ffn2_wu": [l2u, l4u], "ffn2_wd": [l2d, l4d],
              "conv_w_in": [lcin], "conv_w_out": [lcout], "kv_w": [lkv], "attn_w_q": [lwq], "attn_w_o": [lwo]}
    weights = dict(meta=meta, ffn1_wg=ffn1_wg, ffn1_wu=ffn1_wu, ffn1_wd=ffn1_wd, ffn2_wg=ffn2_wg,
                   ffn2_wu=ffn2_wu, ffn2_wd=ffn2_wd, ln_gain=ln_gain, ln_bias=ln_bias,
                   conv_w_in=conv_w_in, conv_w=conv_w, conv_w_out=conv_w_out, kv_w=kv_w,
                   f_bias=f_bias, attn_w_q=attn_w_q, attn_w_o=attn_w_o)
    moms = dict(meta=(m_meta, v_meta), ffn1_wg=(m_ffn1_wg, v_ffn1_wg), ffn1_wu=(m_ffn1_wu, v_ffn1_wu),
                ffn1_wd=(m_ffn1_wd, v_ffn1_wd), ffn2_wg=(m_ffn2_wg, v_ffn2_wg), ffn2_wu=(m_ffn2_wu, v_ffn2_wu),
                ffn2_wd=(m_ffn2_wd, v_ffn2_wd), ln_gain=(m_ln_gain, v_ln_gain), ln_bias=(m_ln_bias, v_ln_bias),
                conv_w_in=(m_conv_w_in, v_conv_w_in), conv_w=(m_conv_w, v_conv_w),
                conv_w_out=(m_conv_w_out, v_conv_w_out), kv_w=(m_kv_w, v_kv_w), f_bias=(m_f_bias, v_f_bias),
                attn_w_q=(m_attn_w_q, v_attn_w_q), attn_w_o=(m_attn_w_o, v_attn_w_o))

    names = list(weights)
    g_out, d_out, m_out, v_out = [], [], [], []
    for n in names:
        w = weights[n]
        shp = w.shape
        mm, vv_ = moms[n]
        if n in landed:
            three = (len(landed[n]),) + shp[-2:]
            g, dl, nm, nv = _reduce_adamw(w.reshape(three), mm.reshape(three), vv_.reshape(three),
                                          landed[n], "adamw_" + n)
            g = g.reshape(shp)
        else:
            two = (1, shp[0]) if w.ndim == 1 else (math.prod(shp[:-1]), shp[-1])
            g = grads[n].reshape(shp)
            dl, nm, nv = _adamw(w.reshape(two), g.reshape(two), mm.reshape(two), vv_.reshape(two), "adamw_" + n)
        g_out.append(g)
        d_out.append(dl.reshape(shp))
        m_out.append(nm.reshape(shp))
        v_out.append(nv.reshape(shp))
    return (loss, grad_x, *g_out, *d_out, *m_out, *v_out)
```

```python
import functools
import math

import jax
import jax.numpy as jnp
from jax import lax
from jax.experimental import pallas as pl
from jax.experimental.pallas import tpu as pltpu

F32 = jnp.float32
BF16 = jnp.bfloat16

N_DEV = 8
N_HEADS = 8
N_META = 16
PAD = 112
ROW0 = PAD + N_META
LN_EPS = 1e-5
NEG_INF = -1e30
LOG2E = 1.4426950408889634
ATTN_HEADS_PER_STEP = 4
ATTN_BWD_HEADS_PER_STEP = 2
LANES = 128
MXU_COLS = 256
FFN_FWD_CHUNKS = 6
FFN_BWD_CHUNKS = 4

ADAM_LR = 0.001
ADAM_B1 = 0.9
ADAM_B2 = 0.999
ADAM_EPS = 1e-08
ADAM_WD = 0.01
ADAM_STEP = 10

ROW_TILES = (640, 128)
LOSS_TILE = 128
BF16_ROWS = 16
F32_ROWS = 8
SUM_ROWS_MAX = 768
ADAM_ROWS_MAX = 256
VMEM_BIG = 56 << 20
VMEM_MID = 40 << 20

ANY = pl.BlockSpec(memory_space=pl.ANY)
MESH = pl.DeviceIdType.MESH


def _row_tile(t):
    for c in ROW_TILES:
        if t % c == 0:
            return c
    raise ValueError(f"no row tile for {t}")


def _dot(a, b):
    return jnp.dot(a, b, preferred_element_type=F32)


def _dot_nt(a, b):
    return lax.dot_general(a, b, (((1,), (1,)), ((), ())), preferred_element_type=F32)


def _dot_tn(a, b):
    return lax.dot_general(a, b, (((0,), (0,)), ((), ())), preferred_element_type=F32)


def _params(sem, vmem):
    return pltpu.CompilerParams(dimension_semantics=sem, vmem_limit_bytes=vmem)


def _ln_fwd(z):
    mu = jnp.mean(z, axis=-1, keepdims=True)
    zc = z - mu
    var = jnp.mean(zc * zc, axis=-1, keepdims=True)
    rstd = lax.rsqrt(var + LN_EPS)
    return zc * rstd, rstd


def _ln_bwd(dh, xhat, rstd, gain):
    dxh = dh * gain
    m1 = jnp.mean(dxh, axis=-1, keepdims=True)
    m2 = jnp.mean(dxh * xhat, axis=-1, keepdims=True)
    dz = rstd * (dxh - m1 - xhat * m2)
    return dz, jnp.sum(dh * xhat, axis=0, keepdims=True), jnp.sum(dh, axis=0, keepdims=True)


def _load_resident(pairs, sems):
    cps = [pltpu.make_async_copy(src, dst, sems.at[k]) for k, (src, dst) in enumerate(pairs)]
    for cp in cps:
        cp.start()
    for cp in cps:
        cp.wait()


def _peer_ids():
    mx, my, mc = lax.axis_index("x"), lax.axis_index("y"), lax.axis_index("c")
    peers = []
    for kk in range(1, N_DEV):
        px = 1 - mx if (kk >> 2) & 1 else mx
        py = 1 - my if (kk >> 1) & 1 else my
        pc = 1 - mc if kk & 1 else mc
        peers.append(((px, py, pc), 4 * px + 2 * py + pc))
    return 4 * mx + 2 * my + mc, peers


def _exchange_copies(jobs, send_sems, recv_sems, local_sems, starting):
    me_id, peers = _peer_ids()
    for n, (gather, src, dst) in enumerate(jobs):
        own = pltpu.make_async_copy(src if gather else src.at[me_id], dst.at[me_id], local_sems.at[n])
        own.start() if starting else own.wait()
        for k, (dev, pid) in enumerate(peers):
            sem = (N_DEV - 1) * n + k
            out = src if gather else src.at[pid]
            send = pltpu.make_async_remote_copy(
                src_ref=out, dst_ref=dst.at[me_id], send_sem=send_sems.at[sem], recv_sem=recv_sems.at[sem],
                device_id=dev, device_id_type=MESH)
            if starting:
                send.start()
            else:
                pltpu.make_async_remote_copy(
                    src_ref=out, dst_ref=dst.at[pid], send_sem=send_sems.at[sem], recv_sem=recv_sems.at[sem],
                    device_id=dev, device_id_type=MESH).wait_recv()
                send.wait_send()


def _carried(body, n_in, n_out, carry, first, last):
    nj = len(carry)
    if nj == 0:
        return body

    def wrapped(*refs):
        ins, srcs = refs[:n_in], refs[n_in:n_in + nj]
        outs = refs[n_in + nj:n_in + nj + n_out]
        dsts = refs[n_in + nj + n_out:n_in + 2 * nj + n_out]
        scratch, sems = refs[n_in + 2 * nj + n_out:-3], refs[-3:]
        jobs = [(g, s, r) for (g, _), s, r in zip(carry, srcs, dsts)]

        @pl.when(first())
        def _():
            _exchange_copies(jobs, *sems, starting=True)

        body(*ins, *outs, *scratch)

        @pl.when(last())
        def _():
            _exchange_copies(jobs, *sems, starting=False)

    return wrapped


def _carry_shapes(carry):
    return [jax.ShapeDtypeStruct((N_DEV,) + a.shape if g else a.shape, a.dtype) for g, a in carry]


def _carry_scratch(carry):
    if not carry:
        return []
    n = len(carry)
    return [pltpu.SemaphoreType.DMA(((N_DEV - 1) * n,)), pltpu.SemaphoreType.DMA(((N_DEV - 1) * n,)),
            pltpu.SemaphoreType.DMA((n,))]


def _exchange(parts, name):
    carry = [(False, a) for a in parts]
    n = len(parts)

    def body(*refs):
        jobs = [(False, s, r) for s, r in zip(refs[:n], refs[n:2 * n])]
        _exchange_copies(jobs, *refs[2 * n:], starting=True)
        _exchange_copies(jobs, *refs[2 * n:], starting=False)

    return pl.pallas_call(
        body, name=name, in_specs=[ANY] * n, out_specs=[ANY] * n, out_shape=_carry_shapes(carry),
        scratch_shapes=_carry_scratch(carry),
    )(*parts)


def _ffn_fwd(xh, gi, bi, wg, wu, wd, go, bo, alpha, name, carry=()):
    t, d = xh.shape
    nch, _, fc = wg.shape
    f = nch * fc
    per = min(FFN_FWD_CHUNKS, nch)
    nc = -(-nch // per)
    tm = _row_tile(t)
    nt = t // tm

    def body(xh_ref, gi_ref, bi_ref, wg_hbm, wu_hbm, wd_hbm, go_ref, bo_ref,
             xo_ref, rs_ref, hb_ref, g_ref, u_ref,
             wg_v, wu_v, wd_v, acc, hbs, sems):
        i = pl.program_id(0)
        c = pl.program_id(1)

        @pl.when((i == 0) & (c == 0))
        def _():
            _load_resident([(wg_hbm, wg_v), (wu_hbm, wu_v), (wd_hbm, wd_v)], sems)

        @pl.when(c == 0)
        def _():
            h = xh_ref[...] * gi_ref[...] + bi_ref[...]
            hbs[...] = h.astype(BF16)
            acc[...] = jnp.zeros_like(acc)

        def chunk(k):
            ck = c * per + k
            cols = slice(k * fc, (k + 1) * fc)
            hb = hbs[...]
            g = _dot(hb, wg_v[ck])
            u = _dot(hb, wu_v[ck])
            a = (g * jax.nn.sigmoid(g)) * u
            g_ref[:, cols] = g.astype(BF16)
            u_ref[:, cols] = u.astype(BF16)
            acc[...] += _dot(a.astype(BF16), wd_v[ck])

        for k in range(per):
            if (nc - 1) * per + k < nch:
                chunk(k)
            else:
                pl.when(c * per + k < nch)(functools.partial(chunk, k))

        @pl.when(c == nc - 1)
        def _():
            h = xh_ref[...] * gi_ref[...] + bi_ref[...]
            xhat, rstd = _ln_fwd(alpha * h + 0.5 * acc[...])
            xo_ref[...] = xhat
            rs_ref[...] = rstd
            hb_ref[...] = (xhat * go_ref[...] + bo_ref[...]).astype(BF16).T

    row = pl.BlockSpec((tm, d), lambda i, c: (i, 0))
    vec = pl.BlockSpec((1, d), lambda i, c: (0, 0))
    chunk = pl.BlockSpec((tm, per * fc), lambda i, c: (i, c))
    first = lambda: (pl.program_id(0) == 0) & (pl.program_id(1) == 0)
    last = lambda: (pl.program_id(0) == nt - 1) & (pl.program_id(1) == nc - 1)
    return pl.pallas_call(
        _carried(body, 8, 5, carry, first, last), name=name, grid=(nt, nc),
        in_specs=[row, vec, vec, ANY, ANY, ANY, vec, vec] + [ANY] * len(carry),
        out_specs=[row, pl.BlockSpec((tm, 1), lambda i, c: (i, 0)),
                   pl.BlockSpec((d, tm), lambda i, c: (0, i)), chunk, chunk] + [ANY] * len(carry),
        out_shape=[jax.ShapeDtypeStruct((t, d), F32), jax.ShapeDtypeStruct((t, 1), F32),
                   jax.ShapeDtypeStruct((d, t), BF16), jax.ShapeDtypeStruct((t, f), BF16),
                   jax.ShapeDtypeStruct((t, f), BF16)] + _carry_shapes(carry),
        scratch_shapes=[pltpu.VMEM((nch, d, fc), BF16), pltpu.VMEM((nch, d, fc), BF16),
                        pltpu.VMEM((nch, fc, d), BF16), pltpu.VMEM((tm, d), F32),
                        pltpu.VMEM((tm, d), BF16), pltpu.SemaphoreType.DMA((3,))] + _carry_scratch(carry),
        compiler_params=_params(("arbitrary", "arbitrary"), VMEM_BIG),
    )(xh, gi, bi, wg, wu, wd, go, bo, *[a for _, a in carry])


def _ffn_bwd(dh, xo, rs, go, gs, us, wg, wu, wd, alpha, name, carry=()):
    t, d = dh.shape
    nch, _, fc = wg.shape
    f = nch * fc
    per = min(FFN_BWD_CHUNKS, nch)
    nc = -(-nch // per)
    tm = _row_tile(t)
    nt = t // tm

    def body(dh_ref, xo_ref, rs_ref, go_ref, g_ref, u_ref, wg_hbm, wu_hbm, wd_hbm,
             dhin_ref, dot_ref, dg_ref, du_ref, a_ref, dgain_ref, dbias_ref,
             wg_v, wu_v, wd_v, do_ref, sems):
        i = pl.program_id(0)
        c = pl.program_id(1)

        @pl.when((i == 0) & (c == 0))
        def _():
            _load_resident([(wg_hbm, wg_v), (wu_hbm, wu_v), (wd_hbm, wd_v)], sems)
            dgain_ref[...] = jnp.zeros_like(dgain_ref)
            dbias_ref[...] = jnp.zeros_like(dbias_ref)

        @pl.when(c == 0)
        def _():
            dz, dgp, dbp = _ln_bwd(dh_ref[...], xo_ref[...], rs_ref[...], go_ref[...])
            dgain_ref[...] += dgp
            dbias_ref[...] += dbp
            dob = (0.5 * dz).astype(BF16)
            do_ref[...] = dob
            dot_ref[...] = dob.T
            dhin_ref[...] = alpha * dz

        def chunk(k):
            ck = c * per + k
            cols = slice(k * fc, (k + 1) * fc)
            g = g_ref[:, cols].astype(F32)
            u = u_ref[:, cols].astype(F32)
            sg = jax.nn.sigmoid(g)
            sl = g * sg
            da = _dot_nt(do_ref[...], wd_v[ck])
            dgb = (da * u * (sg * (1.0 + g * (1.0 - sg)))).astype(BF16)
            dub = (da * sl).astype(BF16)
            a_ref[:, cols] = (sl * u).astype(BF16)
            dg_ref[:, cols] = dgb
            du_ref[:, cols] = dub
            dhin_ref[...] += _dot_nt(dgb, wg_v[ck]) + _dot_nt(dub, wu_v[ck])

        for k in range(per):
            if (nc - 1) * per + k < nch:
                chunk(k)
            else:
                pl.when(c * per + k < nch)(functools.partial(chunk, k))

    row = pl.BlockSpec((tm, d), lambda i, c: (i, 0))
    vec = pl.BlockSpec((1, d), lambda i, c: (0, 0))
    chunk = pl.BlockSpec((tm, per * fc), lambda i, c: (i, c))
    first = lambda: (pl.program_id(0) == 0) & (pl.program_id(1) == 0)
    last = lambda: (pl.program_id(0) == nt - 1) & (pl.program_id(1) == nc - 1)
    return pl.pallas_call(
        _carried(body, 9, 7, carry, first, last), name=name, grid=(nt, nc),
        in_specs=[row, row, pl.BlockSpec((tm, 1), lambda i, c: (i, 0)), vec, chunk, chunk,
                  ANY, ANY, ANY] + [ANY] * len(carry),
        out_specs=[row, pl.BlockSpec((d, tm), lambda i, c: (0, i)), chunk, chunk, chunk, vec, vec]
                  + [ANY] * len(carry),
        out_shape=[jax.ShapeDtypeStruct((t, d), F32), jax.ShapeDtypeStruct((d, t), BF16),
                   jax.ShapeDtypeStruct((t, f), BF16), jax.ShapeDtypeStruct((t, f), BF16),
                   jax.ShapeDtypeStruct((t, f), BF16), jax.ShapeDtypeStruct((1, d), F32),
                   jax.ShapeDtypeStruct((1, d), F32)] + _carry_shapes(carry),
        scratch_shapes=[pltpu.VMEM((nch, d, fc), BF16), pltpu.VMEM((nch, d, fc), BF16),
                        pltpu.VMEM((nch, fc, d), BF16), pltpu.VMEM((tm, d), BF16),
                        pltpu.SemaphoreType.DMA((3,))] + _carry_scratch(carry),
        compiler_params=_params(("arbitrary", "arbitrary"), VMEM_BIG),
    )(dh, xo, rs, go, gs, us, wg, wu, wd, *[a for _, a in carry])


def _wgrad(xt, ys, name, carry=()):
    m, t = xt.shape
    n = ys[0].shape[1]
    tn = min(n, MXU_COLS)
    ny = len(ys)

    def body(*refs):
        x_hbm = refs[0]
        y_refs = refs[1:1 + ny]
        o_refs = refs[1 + ny:1 + 2 * ny]
        xv, sems = refs[1 + 2 * ny:]

        @pl.when(pl.program_id(0) == 0)
        def _():
            _load_resident([(x_hbm, xv)], sems)

        for y_ref, o_ref in zip(y_refs, o_refs):
            o_ref[...] = _dot(xv[...], y_ref[...].astype(BF16))

    steps = n // tn
    first = lambda: pl.program_id(0) == 0
    last = lambda: pl.program_id(0) == steps - 1
    return pl.pallas_call(
        _carried(body, 1 + ny, ny, carry, first, last), name=name, grid=(steps,),
        in_specs=[ANY] + [pl.BlockSpec((t, tn), lambda c: (0, c)) for _ in ys] + [ANY] * len(carry),
        out_specs=[pl.BlockSpec((m, tn), lambda c: (0, c)) for _ in ys] + [ANY] * len(carry),
        out_shape=[jax.ShapeDtypeStruct((m, n), F32) for _ in ys] + _carry_shapes(carry),
        scratch_shapes=[pltpu.VMEM((m, t), BF16), pltpu.SemaphoreType.DMA((1,))] + _carry_scratch(carry),
        compiler_params=_params(("arbitrary",), VMEM_BIG),
    )(xt, *ys, *[a for _, a in carry])


def _cast_t(h, name):
    t, d = h.shape
    tm = _row_tile(t)

    def body(h_ref, o_ref):
        o_ref[...] = h_ref[...].astype(BF16).T

    return pl.pallas_call(
        body, name=name, grid=(t // tm,),
        in_specs=[pl.BlockSpec((tm, d), lambda i: (i, 0))],
        out_specs=pl.BlockSpec((d, tm), lambda i: (0, i)),
        out_shape=jax.ShapeDtypeStruct((d, t), BF16),
        compiler_params=_params(("arbitrary",), VMEM_MID),
    )(h)


def _shift_rows(u, halo, tm):
    r = lax.broadcasted_iota(jnp.int32, (tm, 1), 0)
    u1 = jnp.where(r == 0, halo[7:8], pltpu.roll(u, 1, 0))
    u2 = jnp.where(r == 0, halo[6:7], jnp.where(r == 1, halo[7:8], pltpu.roll(u, 2, 0)))
    return u1, u2


def _conv_fwd(xh, gi, bi, w_in, cw, w_out, go, bo, alpha, name, carry=()):
    t, d = xh.shape
    tm = _row_tile(t)
    nt = t // tm

    def body(xh_ref, gi_ref, bi_ref, win_ref, cw_ref, wout_ref, go_ref, bo_ref,
             xo_ref, rs_ref, hb_ref, p_ref, m_ref, halo):
        i = pl.program_id(0)

        @pl.when(i == 0)
        def _():
            halo[...] = jnp.zeros_like(halo)

        h = xh_ref[...] * gi_ref[...] + bi_ref[...]
        hb = h.astype(BF16)
        bg = _dot(hb, win_ref[:, 0:d])
        cg = _dot(hb, win_ref[:, d:2 * d])
        val = _dot(hb, win_ref[:, 2 * d:3 * d])
        p_ref[:, 0:d] = bg.astype(BF16)
        p_ref[:, d:2 * d] = cg.astype(BF16)
        p_ref[:, 2 * d:3 * d] = val.astype(BF16)
        rows = i * tm + lax.broadcasted_iota(jnp.int32, (tm, 1), 0)
        u = jnp.where(rows >= PAD, cg * val, 0.0)
        u1, u2 = _shift_rows(u, halo[...], tm)
        halo[...] = u[tm - 8:tm]
        y = cw_ref[0:1] * u2 + cw_ref[1:2] * u1 + cw_ref[2:3] * u
        mb = (bg * y).astype(BF16)
        m_ref[...] = mb.T
        xhat, rstd = _ln_fwd(alpha * h + _dot(mb, wout_ref[...]))
        xo_ref[...] = xhat
        rs_ref[...] = rstd
        hb_ref[...] = (xhat * go_ref[...] + bo_ref[...]).astype(BF16).T

    row = pl.BlockSpec((tm, d), lambda i: (i, 0))
    col = pl.BlockSpec((d, tm), lambda i: (0, i))
    vec = pl.BlockSpec((1, d), lambda i: (0, 0))
    first = lambda: pl.program_id(0) == 0
    last = lambda: pl.program_id(0) == nt - 1
    return pl.pallas_call(
        _carried(body, 8, 5, carry, first, last), name=name, grid=(nt,),
        in_specs=[row, vec, vec, pl.BlockSpec((d, 3 * d), lambda i: (0, 0)),
                  pl.BlockSpec((3, d), lambda i: (0, 0)), pl.BlockSpec((d, d), lambda i: (0, 0)),
                  vec, vec] + [ANY] * len(carry),
        out_specs=[row, pl.BlockSpec((tm, 1), lambda i: (i, 0)), col,
                   pl.BlockSpec((tm, 3 * d), lambda i: (i, 0)), col] + [ANY] * len(carry),
        out_shape=[jax.ShapeDtypeStruct((t, d), F32), jax.ShapeDtypeStruct((t, 1), F32),
                   jax.ShapeDtypeStruct((d, t), BF16), jax.ShapeDtypeStruct((t, 3 * d), BF16),
                   jax.ShapeDtypeStruct((d, t), BF16)] + _carry_shapes(carry),
        scratch_shapes=[pltpu.VMEM((8, d), F32)] + _carry_scratch(carry),
        compiler_params=_params(("arbitrary",), VMEM_BIG),
    )(xh, gi, bi, w_in, cw, w_out, go, bo, *[a for _, a in carry])


def _conv_bwd(dh, xo, rs, go, p, cw, w_in, w_out, alpha, name):
    t, d = dh.shape
    tm = _row_tile(t)
    nt = t // tm
    tb = tm // 8

    def body(dh_ref, xo_ref, rs_ref, go_ref, p_ref, ph_ref, cw_ref, win_ref, wout_ref,
             dhin_ref, dmix_ref, dp_ref, dcw_ref, dgain_ref, dbias_ref, carry):
        i = pl.program_id(0)
        tile = nt - 1 - i

        @pl.when(i == 0)
        def _():
            carry[...] = jnp.zeros_like(carry)
            dcw_ref[...] = jnp.zeros_like(dcw_ref)
            dgain_ref[...] = jnp.zeros_like(dgain_ref)
            dbias_ref[...] = jnp.zeros_like(dbias_ref)

        dz, dgp, dbp = _ln_bwd(dh_ref[...], xo_ref[...], rs_ref[...], go_ref[...])
        dgain_ref[...] += dgp
        dbias_ref[...] += dbp
        dmixb = dz.astype(BF16)
        dmix_ref[...] = dmixb
        dm = _dot_nt(dmixb, wout_ref[...])

        bg = p_ref[:, 0:d].astype(F32)
        cg = p_ref[:, d:2 * d].astype(F32)
        val = p_ref[:, 2 * d:3 * d].astype(F32)
        rows = tile * tm + lax.broadcasted_iota(jnp.int32, (tm, 1), 0)
        valid = rows >= PAD
        u = jnp.where(valid, cg * val, 0.0)
        hrows = tile * tm - 8 + lax.broadcasted_iota(jnp.int32, (8, 1), 0)
        hu = jnp.where((hrows >= PAD) & (tile > 0),
                       ph_ref[:, d:2 * d].astype(F32) * ph_ref[:, 2 * d:3 * d].astype(F32), 0.0)
        u1, u2 = _shift_rows(u, hu, tm)
        w0, w1, w2 = cw_ref[0:1], cw_ref[1:2], cw_ref[2:3]
        y = w0 * u2 + w1 * u1 + w2 * u
        dbg = dm * y
        dy = dm * bg
        dcw_ref[0:1] += jnp.sum(dy * u2, axis=0, keepdims=True)
        dcw_ref[1:2] += jnp.sum(dy * u1, axis=0, keepdims=True)
        dcw_ref[2:3] += jnp.sum(dy * u, axis=0, keepdims=True)

        nxt = carry[...]
        r = lax.broadcasted_iota(jnp.int32, (tm, 1), 0)
        dy1 = jnp.where(r == tm - 1, nxt[0:1], pltpu.roll(dy, tm - 1, 0))
        dy2 = jnp.where(r == tm - 2, nxt[0:1],
                        jnp.where(r == tm - 1, nxt[1:2], pltpu.roll(dy, tm - 2, 0)))
        carry[...] = dy[0:8]
        du = jnp.where(valid, w2 * dy + w1 * dy1 + w0 * dy2, 0.0)
        dbgb = dbg.astype(BF16)
        dcgb = (du * val).astype(BF16)
        dvalb = (du * cg).astype(BF16)
        dp_ref[:, 0:d] = dbgb
        dp_ref[:, d:2 * d] = dcgb
        dp_ref[:, 2 * d:3 * d] = dvalb
        dhin_ref[...] = (alpha * dz + _dot_nt(dbgb, win_ref[:, 0:d])
                         + _dot_nt(dcgb, win_ref[:, d:2 * d]) + _dot_nt(dvalb, win_ref[:, 2 * d:3 * d]))

    row = pl.BlockSpec((tm, d), lambda i: (nt - 1 - i, 0))
    vec = pl.BlockSpec((1, d), lambda i: (0, 0))
    prow = pl.BlockSpec((tm, 3 * d), lambda i: (nt - 1 - i, 0))
    return pl.pallas_call(
        body, name=name, grid=(nt,),
        in_specs=[row, row, pl.BlockSpec((tm, 1), lambda i: (nt - 1 - i, 0)), vec, prow,
                  pl.BlockSpec((8, 3 * d), lambda i: (jnp.maximum((nt - 1 - i) * tb - 1, 0), 0)),
                  pl.BlockSpec((3, d), lambda i: (0, 0)),
                  pl.BlockSpec((d, 3 * d), lambda i: (0, 0)), pl.BlockSpec((d, d), lambda i: (0, 0))],
        out_specs=[row, row, prow, pl.BlockSpec((3, d), lambda i: (0, 0)), vec, vec],
        out_shape=[jax.ShapeDtypeStruct((t, d), F32), jax.ShapeDtypeStruct((t, d), BF16),
                   jax.ShapeDtypeStruct((t, 3 * d), BF16), jax.ShapeDtypeStruct((3, d), F32),
                   jax.ShapeDtypeStruct((1, d), F32), jax.ShapeDtypeStruct((1, d), F32)],
        scratch_shapes=[pltpu.VMEM((8, d), F32)],
        compiler_params=_params(("arbitrary",), VMEM_BIG),
    )(dh, xo, rs, go, p, p, cw, w_in, w_out)


def _kv_fwd(xh, gi, bi, wk, wv, wf, fb, name):
    t, d = xh.shape
    tm = _row_tile(t)
    nt = t // tm

    def body(xh_ref, gi_ref, bi_ref, wk_ref, wv_ref, wf_ref, fb_ref,
             k_ref, v_ref, lg_ref, c_ref, ct_ref, run):
        i = pl.program_id(0)

        @pl.when(i == 0)
        def _():
            run[...] = jnp.zeros_like(run)

        x = (xh_ref[...] * gi_ref[...] + bi_ref[...]).astype(BF16)
        k_ref[...] = _dot(x, wk_ref[...]).astype(BF16)
        v_ref[...] = _dot(x, wv_ref[...]).astype(BF16)
        logit = _dot(x, wf_ref[...]) + fb_ref[...]
        lg_ref[...] = logit
        logf = jnp.minimum(logit, 0.0) - jnp.log(1.0 + jnp.exp(-jnp.abs(logit)))
        rows = i * tm + lax.broadcasted_iota(jnp.int32, (tm, 1), 0)
        logf = jnp.where(rows >= PAD, logf, 0.0)
        tri = (lax.broadcasted_iota(jnp.int32, (tm, tm), 0)
               >= lax.broadcasted_iota(jnp.int32, (tm, tm), 1)).astype(F32)
        cs = jnp.dot(tri, logf, precision=lax.Precision.HIGHEST, preferred_element_type=F32) + run[...]
        run[...] = cs[tm - 1:tm]
        c_ref[...] = cs
        ct_ref[...] = cs.T

    row = pl.BlockSpec((tm, d), lambda i: (i, 0))
    vec = pl.BlockSpec((1, d), lambda i: (0, 0))
    gate = pl.BlockSpec((tm, LANES), lambda i: (i, 0))
    sq = pl.BlockSpec((d, d), lambda i: (0, 0))
    return pl.pallas_call(
        body, name=name, grid=(nt,),
        in_specs=[row, vec, vec, sq, sq, pl.BlockSpec((d, LANES), lambda i: (0, 0)),
                  pl.BlockSpec((1, LANES), lambda i: (0, 0))],
        out_specs=[row, row, gate, gate, pl.BlockSpec((LANES, tm), lambda i: (0, i))],
        out_shape=[jax.ShapeDtypeStruct((t, d), BF16), jax.ShapeDtypeStruct((t, d), BF16),
                   jax.ShapeDtypeStruct((t, LANES), F32), jax.ShapeDtypeStruct((t, LANES), F32),
                   jax.ShapeDtypeStruct((LANES, t), F32)],
        scratch_shapes=[pltpu.VMEM((1, LANES), F32)],
        compiler_params=_params(("arbitrary",), VMEM_MID),
    )(xh, gi, bi, wk, wv, wf, fb)


def _kv_bwd(dk, dv, dcs, dcq, logit, dh_other, wk, wv, wf, name):
    t, d = dk.shape
    tm = _row_tile(t)
    nt = t // tm

    def body(dk_ref, dv_ref, dcs_ref, dcq_ref, lg_ref, oth_ref, wk_ref, wv_ref, wf_ref,
             dh_ref, dl_ref, dfb_ref, run):
        i = pl.program_id(0)
        tile = nt - 1 - i

        @pl.when(i == 0)
        def _():
            run[...] = jnp.zeros_like(run)
            dfb_ref[...] = jnp.zeros_like(dfb_ref)

        lane = lax.broadcasted_iota(jnp.int32, (tm, LANES), 1)
        dc = dcq_ref[...]
        for hh in range(N_HEADS):
            dc = dc + jnp.where(lane == hh, jnp.sum(dcs_ref[hh], axis=1, keepdims=True), 0.0)
        tri = (lax.broadcasted_iota(jnp.int32, (tm, tm), 0)
               <= lax.broadcasted_iota(jnp.int32, (tm, tm), 1)).astype(F32)
        dlf = jnp.dot(tri, dc, precision=lax.Precision.HIGHEST, preferred_element_type=F32) + run[...]
        run[...] = dlf[0:1]
        rows = tile * tm + lax.broadcasted_iota(jnp.int32, (tm, 1), 0)
        dlogit = jnp.where(rows >= PAD, dlf * jax.nn.sigmoid(-lg_ref[...]), 0.0)
        dfb_ref[...] += jnp.sum(dlogit, axis=0, keepdims=True)
        dlb = dlogit.astype(BF16)
        dl_ref[...] = dlb
        dh_ref[...] = (oth_ref[...] + _dot_nt(dk_ref[...], wk_ref[...])
                       + _dot_nt(dv_ref[...], wv_ref[...]) + _dot_nt(dlb, wf_ref[...]))

    row = pl.BlockSpec((tm, d), lambda i: (nt - 1 - i, 0))
    gate = pl.BlockSpec((tm, LANES), lambda i: (nt - 1 - i, 0))
    sq = pl.BlockSpec((d, d), lambda i: (0, 0))
    return pl.pallas_call(
        body, name=name, grid=(nt,),
        in_specs=[row, row, pl.BlockSpec((N_HEADS, tm, LANES), lambda i: (0, nt - 1 - i, 0)), gate, gate, row,
                  sq, sq, pl.BlockSpec((d, LANES), lambda i: (0, 0))],
        out_specs=[row, gate, pl.BlockSpec((1, LANES), lambda i: (0, 0))],
        out_shape=[jax.ShapeDtypeStruct((t, d), F32), jax.ShapeDtypeStruct((t, LANES), BF16),
                   jax.ShapeDtypeStruct((1, LANES), F32)],
        scratch_shapes=[pltpu.VMEM((1, LANES), F32)],
        compiler_params=_params(("arbitrary",), VMEM_MID),
    )(dk, dv, dcs, dcq, logit, dh_other, wk, wv, wf)


def _proj(xh, gi, bi, w, name):
    t, k = xh.shape
    n = w.shape[1]
    tm = _row_tile(t)

    def body(x_ref, g_ref, b_ref, w_ref, o_ref):
        x = (x_ref[...] * g_ref[...] + b_ref[...]).astype(BF16)
        o_ref[...] = _dot(x, w_ref[...]).astype(BF16)

    vec = pl.BlockSpec((1, k), lambda i: (0, 0))
    return pl.pallas_call(
        body, name=name, grid=(t // tm,),
        in_specs=[pl.BlockSpec((tm, k), lambda i: (i, 0)), vec, vec, pl.BlockSpec((k, n), lambda i: (0, 0))],
        out_specs=pl.BlockSpec((tm, n), lambda i: (i, 0)),
        out_shape=jax.ShapeDtypeStruct((t, n), BF16),
        compiler_params=_params(("arbitrary",), VMEM_MID),
    )(xh, gi, bi, w)


def _add_proj_nt(base, y, w, name):
    t, n = y.shape
    k = w.shape[0]
    tm = _row_tile(t)

    def body(b_ref, y_ref, w_ref, o_ref):
        o_ref[...] = b_ref[...] + _dot_nt(y_ref[...].astype(BF16), w_ref[...])

    return pl.pallas_call(
        body, name=name, grid=(t // tm,),
        in_specs=[pl.BlockSpec((tm, k), lambda i: (i, 0)), pl.BlockSpec((tm, n), lambda i: (i, 0)),
                  pl.BlockSpec((k, n), lambda i: (0, 0))],
        out_specs=pl.BlockSpec((tm, k), lambda i: (i, 0)),
        out_shape=jax.ShapeDtypeStruct((t, k), F32),
        compiler_params=_params(("arbitrary",), VMEM_MID),
    )(base, y, w)


def _attn_out_fwd(ot, xh, gi, bi, w_o, go, bo, alpha, name):
    t, d = xh.shape
    tm = _row_tile(t)

    def body(ot_ref, xh_ref, gi_ref, bi_ref, wo_ref, go_ref, bo_ref, xo_ref, rs_ref, hb_ref):
        h = xh_ref[...] * gi_ref[...] + bi_ref[...]
        xhat, rstd = _ln_fwd(alpha * h + _dot_tn(ot_ref[...], wo_ref[...]))
        xo_ref[...] = xhat
        rs_ref[...] = rstd
        hb_ref[...] = (xhat * go_ref[...] + bo_ref[...]).astype(BF16).T

    row = pl.BlockSpec((tm, d), lambda i: (i, 0))
    col = pl.BlockSpec((d, tm), lambda i: (0, i))
    vec = pl.BlockSpec((1, d), lambda i: (0, 0))
    return pl.pallas_call(
        body, name=name, grid=(t // tm,),
        in_specs=[col, row, vec, vec, pl.BlockSpec((d, d), lambda i: (0, 0)), vec, vec],
        out_specs=[row, pl.BlockSpec((tm, 1), lambda i: (i, 0)), col],
        out_shape=[jax.ShapeDtypeStruct((t, d), F32), jax.ShapeDtypeStruct((t, 1), F32),
                   jax.ShapeDtypeStruct((d, t), BF16)],
        compiler_params=_params(("arbitrary",), VMEM_MID),
    )(ot, xh, gi, bi, w_o, go, bo)


def _attn_out_bwd(dh, xo, rs, go, ot, w_o, alpha, name):
    t, d = dh.shape
    tm = _row_tile(t)
    hd = d // N_HEADS

    def body(dh_ref, xo_ref, rs_ref, go_ref, ot_ref, wo_ref,
             dres_ref, dmix_ref, dot_ref, delta_ref, dgain_ref, dbias_ref):
        @pl.when(pl.program_id(0) == 0)
        def _():
            dgain_ref[...] = jnp.zeros_like(dgain_ref)
            dbias_ref[...] = jnp.zeros_like(dbias_ref)

        dz, dgp, dbp = _ln_bwd(dh_ref[...], xo_ref[...], rs_ref[...], go_ref[...])
        dgain_ref[...] += dgp
        dbias_ref[...] += dbp
        dres_ref[...] = alpha * dz
        dmixb = dz.astype(BF16)
        dmix_ref[...] = dmixb
        dot_t = _dot_nt(wo_ref[...], dmixb)
        dot_ref[...] = dot_t.astype(BF16)
        prod = dot_t * ot_ref[...].astype(F32)
        delta_ref[...] = jnp.sum(prod.reshape(N_HEADS, hd, tm), axis=1)

    row = pl.BlockSpec((tm, d), lambda i: (i, 0))
    vec = pl.BlockSpec((1, d), lambda i: (0, 0))
    col = pl.BlockSpec((d, tm), lambda i: (0, i))
    return pl.pallas_call(
        body, name=name, grid=(t // tm,),
        in_specs=[row, row, pl.BlockSpec((tm, 1), lambda i: (i, 0)), vec, col,
                  pl.BlockSpec((d, d), lambda i: (0, 0))],
        out_specs=[row, row, col, pl.BlockSpec((N_HEADS, tm), lambda i: (0, i)), vec, vec],
        out_shape=[jax.ShapeDtypeStruct((t, d), F32), jax.ShapeDtypeStruct((t, d), BF16),
                   jax.ShapeDtypeStruct((d, t), BF16), jax.ShapeDtypeStruct((N_HEADS, t), F32),
                   jax.ShapeDtypeStruct((1, d), F32), jax.ShapeDtypeStruct((1, d), F32)],
        compiler_params=_params(("arbitrary",), VMEM_MID),
    )(dh, xo, rs, go, ot, w_o)


def _scores_t(k, q, ct_ref, c_ref, h, i, j, tq, tk, scale, masked):
    sub = lax.broadcasted_iota(jnp.int32, (8, tq), 0)
    cq = jnp.sum(jnp.where(sub == h, ct_ref[...], 0.0), axis=0, keepdims=True) * LOG2E
    lane = lax.broadcasted_iota(jnp.int32, (tk, LANES), 1)
    ck = jnp.sum(jnp.where(lane == h, c_ref[...], 0.0), axis=1, keepdims=True) * LOG2E
    st = _dot_nt(k, q) * (scale * LOG2E) - ck
    if masked:
        kpos = j * tk + lax.broadcasted_iota(jnp.int32, (tk, 1), 0)
        qpos = i * tq + lax.broadcasted_iota(jnp.int32, (1, tq), 1)
        st = jnp.where((kpos <= qpos) & (kpos >= PAD), st, NEG_INF)
    return st, cq


def _tri_pairs(n, by_row):
    if by_row:
        pairs = [(i, j) for i in range(n) for j in range(i + 1)]
    else:
        pairs = [(i, j) for j in range(n) for i in range(j, n)]
    return (jnp.asarray([p[0] for p in pairs], jnp.int32), jnp.asarray([p[1] for p in pairs], jnp.int32))


def _attn_fwd(q, k, v, c, ct, name, carry=()):
    t, d = q.shape
    hd = d // N_HEADS
    tq = tk = _row_tile(t)
    nq = t // tq
    scale = 1.0 / math.sqrt(hd)

    hps = ATTN_HEADS_PER_STEP

    def body(it_ref, jt_ref, q_ref, k_ref, v_ref, c_ref, ct_ref, ot_ref, lse_ref, m_s, l_s, acc):
        hp, p_ = pl.program_id(0), pl.program_id(1)
        i, j = it_ref[p_], jt_ref[p_]

        @pl.when(j == 0)
        def _():
            m_s[...] = jnp.full_like(m_s, NEG_INF)
            l_s[...] = jnp.zeros_like(l_s)
            acc[...] = jnp.zeros_like(acc)

        def update(masked):
            scores = []
            for e in range(hps):
                cols = slice(e * hd, (e + 1) * hd)
                scores.append(_scores_t(k_ref[:, cols], q_ref[:, cols], ct_ref, c_ref, hp * hps + e,
                                        i, j, tq, tk, scale, masked))
            probs = []
            for e, (st, cq) in enumerate(scores):
                m_new = jnp.maximum(m_s[e], jnp.max(st, axis=0, keepdims=True) + cq)
                a = jnp.exp2(m_s[e] - m_new)
                p = jnp.exp2(st - (m_new - cq))
                l_s[e] = a * l_s[e] + jnp.sum(p, axis=0, keepdims=True)
                m_s[e] = m_new
                probs.append((a, p.astype(BF16)))
            for e, (a, pb) in enumerate(probs):
                acc[e] = a * acc[e] + _dot_tn(v_ref[:, e * hd:(e + 1) * hd], pb)

        edge = (j == i) | (j == 0)
        pl.when(edge)(lambda: update(True))
        pl.when(jnp.logical_not(edge))(lambda: update(False))

        @pl.when(j == i)
        def _():
            for e in range(hps):
                ot_ref[e * hd:(e + 1) * hd, :] = (acc[e] / l_s[e]).astype(BF16)
                lse_ref[e] = m_s[e] + jnp.log2(l_s[e])

    it, jt = _tri_pairs(nq, by_row=True)
    npairs = it.shape[0]
    nhp = N_HEADS // hps
    kv = pl.BlockSpec((tk, hps * hd), lambda h, p, it, jt: (jt[p], h))
    first = lambda: (pl.program_id(0) == 0) & (pl.program_id(1) == 0)
    last = lambda: (pl.program_id(0) == nhp - 1) & (pl.program_id(1) == npairs - 1)
    return pl.pallas_call(
        _carried(body, 7, 2, carry, first, last), name=name,
        grid_spec=pltpu.PrefetchScalarGridSpec(
            num_scalar_prefetch=2, grid=(nhp, npairs),
            in_specs=[pl.BlockSpec((tq, hps * hd), lambda h, p, it, jt: (it[p], h)), kv, kv,
                      pl.BlockSpec((tk, LANES), lambda h, p, it, jt: (jt[p], 0)),
                      pl.BlockSpec((8, tq), lambda h, p, it, jt: (0, it[p]))] + [ANY] * len(carry),
            out_specs=[pl.BlockSpec((hps * hd, tq), lambda h, p, it, jt: (h, it[p])),
                       pl.BlockSpec((hps, 1, tq), lambda h, p, it, jt: (h, 0, it[p]))] + [ANY] * len(carry),
            scratch_shapes=[pltpu.VMEM((hps, 1, tq), F32), pltpu.VMEM((hps, 1, tq), F32),
                            pltpu.VMEM((hps, hd, tq), F32)] + _carry_scratch(carry)),
        out_shape=[jax.ShapeDtypeStruct((d, t), BF16), jax.ShapeDtypeStruct((N_HEADS, 1, t), F32)]
                  + _carry_shapes(carry),
        compiler_params=_params(("arbitrary", "arbitrary"), VMEM_MID),
    )(it, jt, q, k, v, c, ct, *[a for _, a in carry])


def _attn_bwd(q, k, v, c, ct, lse, delta, dot_t, name, carry=()):
    t, d = q.shape
    hd = d // N_HEADS
    tq = tk = _row_tile(t)
    nq = t // tq
    scale = 1.0 / math.sqrt(hd)
    hps = ATTN_BWD_HEADS_PER_STEP

    def body(it_ref, jt_ref, q_ref, k_ref, v_ref, c_ref, ct_ref, lse_ref, delta_ref, dot_ref,
             dq_ref, dk_ref, dv_ref, dcs_ref, drow_ref, dk_acc, dv_acc, dc_acc):
        hp, p_ = pl.program_id(0), pl.program_id(1)
        i, j = it_ref[p_], jt_ref[p_]

        @pl.when(p_ == 0)
        def _():
            dq_ref[...] = jnp.zeros_like(dq_ref)
            drow_ref[...] = jnp.zeros_like(drow_ref)

        @pl.when(i == j)
        def _():
            dk_acc[...] = jnp.zeros_like(dk_acc)
            dv_acc[...] = jnp.zeros_like(dv_acc)
            dc_acc[...] = jnp.zeros_like(dc_acc)

        def update(masked):
            sub = lax.broadcasted_iota(jnp.int32, (8, tq), 0)
            rows = pl.ds(pl.multiple_of(i * tq, tq), tq)
            stage = []
            for e in range(hps):
                cols = slice(e * hd, (e + 1) * hd)
                st, cq = _scores_t(k_ref[:, cols], q_ref[:, cols], ct_ref, c_ref, hp * hps + e,
                                   i, j, tq, tk, scale, masked)
                dp = _dot(v_ref[:, cols], dot_ref[cols, :])
                stage.append((st, cq, dp))
            grads = []
            for e, (st, cq, dp) in enumerate(stage):
                p = jnp.exp2(st - (lse_ref[e] - cq))
                dl = jnp.sum(jnp.where(sub == hp * hps + e, delta_ref[...], 0.0), axis=0, keepdims=True)
                ds = p * (dp - dl)
                part = ds[:, 0:LANES]
                for g in range(1, tq // LANES):
                    part = part + ds[:, g * LANES:(g + 1) * LANES]
                dc_acc[e] += part
                drow_ref[e, i] += jnp.broadcast_to(jnp.sum(ds, axis=0, keepdims=True), (8, tq))
                grads.append((p.astype(BF16), ds.astype(BF16)))
            for e, (pb, dsb) in enumerate(grads):
                cols = slice(e * hd, (e + 1) * hd)
                dv_acc[e] += _dot_nt(pb, dot_ref[cols, :])
                dk_acc[e] += _dot(dsb, q_ref[:, cols]) * scale
                dq_ref[rows, cols] += _dot_tn(dsb, k_ref[:, cols]) * scale

        edge = (j == i) | (j == 0)
        pl.when(edge)(lambda: update(True))
        pl.when(jnp.logical_not(edge))(lambda: update(False))

        @pl.when(i == nq - 1)
        def _():
            for e in range(hps):
                cols = slice(e * hd, (e + 1) * hd)
                dk_ref[:, cols] = dk_acc[e].astype(BF16)
                dv_ref[:, cols] = dv_acc[e].astype(BF16)
                dcs_ref[e] = -dc_acc[e]

    it, jt = _tri_pairs(nq, by_row=False)
    npairs = it.shape[0]
    nhp = N_HEADS // hps
    kv = pl.BlockSpec((tk, hps * hd), lambda h, p, it, jt: (jt[p], h))
    first = lambda: (pl.program_id(0) == 0) & (pl.program_id(1) == 0)
    last = lambda: (pl.program_id(0) == nhp - 1) & (pl.program_id(1) == npairs - 1)
    return pl.pallas_call(
        _carried(body, 10, 5, carry, first, last), name=name,
        grid_spec=pltpu.PrefetchScalarGridSpec(
            num_scalar_prefetch=2, grid=(nhp, npairs),
            in_specs=[pl.BlockSpec((tq, hps * hd), lambda h, p, it, jt: (it[p], h)), kv, kv,
                      pl.BlockSpec((tk, LANES), lambda h, p, it, jt: (jt[p], 0)),
                      pl.BlockSpec((8, tq), lambda h, p, it, jt: (0, it[p])),
                      pl.BlockSpec((hps, 1, tq), lambda h, p, it, jt: (h, 0, it[p])),
                      pl.BlockSpec((N_HEADS, tq), lambda h, p, it, jt: (0, it[p])),
                      pl.BlockSpec((hps * hd, tq), lambda h, p, it, jt: (h, it[p]))] + [ANY] * len(carry),
            out_specs=[pl.BlockSpec((t, hps * hd), lambda h, p, it, jt: (0, h)), kv, kv,
                       pl.BlockSpec((hps, tk, LANES), lambda h, p, it, jt: (h, jt[p], 0)),
                       pl.BlockSpec((hps, nq, 8, tq), lambda h, p, it, jt: (h, 0, 0, 0))] + [ANY] * len(carry),
            scratch_shapes=[pltpu.VMEM((hps, tk, hd), F32), pltpu.VMEM((hps, tk, hd), F32),
                            pltpu.VMEM((hps, tk, LANES), F32)] + _carry_scratch(carry)),
        out_shape=[jax.ShapeDtypeStruct((t, d), F32), jax.ShapeDtypeStruct((t, d), BF16),
                   jax.ShapeDtypeStruct((t, d), BF16), jax.ShapeDtypeStruct((N_HEADS, t, LANES), F32),
                   jax.ShapeDtypeStruct((N_HEADS, nq, 8, tq), F32)] + _carry_shapes(carry),
        compiler_params=_params(("arbitrary", "arbitrary"), VMEM_BIG),
    )(it, jt, q, k, v, c, ct, lse, delta, dot_t, *[a for _, a in carry])


def _loss_head(xh, g, b, target, name):
    t, d = xh.shape
    tm = LOSS_TILE
    nt = t // tm
    lead = ROW0 // tm

    def body(xh_ref, g_ref, b_ref, tg_ref, dh_ref, loss_ref, part):
        i = pl.program_id(0)

        @pl.when(i == 0)
        def _():
            part[...] = jnp.zeros_like(part)

        @pl.when(i < lead)
        def _():
            dh_ref[...] = jnp.zeros_like(dh_ref)

        @pl.when(i >= lead)
        def _():
            e = xh_ref[...] * g_ref[...] + b_ref[...] - tg_ref[...]
            dh_ref[...] = e * (1.0 / d)
            part[...] += jnp.sum(e * e, axis=0, keepdims=True)

        @pl.when(i == nt - 1)
        def _():
            loss_ref[...] = jnp.full((1, LANES), 0.5 / d, F32) * jnp.sum(part[...])

    return pl.pallas_call(
        body, name=name, grid=(nt,),
        in_specs=[pl.BlockSpec((tm, d), lambda i: (i, 0)), pl.BlockSpec((1, d), lambda i: (0, 0)),
                  pl.BlockSpec((1, d), lambda i: (0, 0)),
                  pl.BlockSpec((tm, d), lambda i: (jnp.maximum(i - lead, 0), 0))],
        out_specs=[pl.BlockSpec((tm, d), lambda i: (i, 0)), pl.BlockSpec((1, LANES), lambda i: (0, 0))],
        out_shape=[jax.ShapeDtypeStruct((t, d), F32), jax.ShapeDtypeStruct((1, LANES), F32)],
        scratch_shapes=[pltpu.VMEM((1, d), F32)],
        compiler_params=_params(("arbitrary",), VMEM_MID),
    )(xh, g, b, target)


def _adamw(w, g, m, v, name):
    r, c = w.shape
    tr = r
    for cand in (256, 128, 64, 32, 16, 8):
        if r % cand == 0 and r > cand:
            tr = cand
            break
    bc1 = 1.0 - ADAM_B1 ** ADAM_STEP
    bc2 = 1.0 - ADAM_B2 ** ADAM_STEP

    def body(w_ref, g_ref, m_ref, v_ref, d_ref, nm_ref, nv_ref):
        gg = g_ref[...]
        nm = ADAM_B1 * m_ref[...] + (1.0 - ADAM_B1) * gg
        nv = ADAM_B2 * v_ref[...] + (1.0 - ADAM_B2) * (gg * gg)
        d_ref[...] = -ADAM_LR * ((nm / bc1) / (jnp.sqrt(nv / bc2) + ADAM_EPS) + ADAM_WD * w_ref[...])
        nm_ref[...] = nm
        nv_ref[...] = nv

    blk = pl.BlockSpec((tr, c), lambda i: (i, 0))
    shp = jax.ShapeDtypeStruct((r, c), F32)
    return pl.pallas_call(
        body, name=name, grid=(r // tr,), in_specs=[blk] * 4, out_specs=[blk] * 3,
        out_shape=[shp] * 3, compiler_params=_params(("arbitrary",), VMEM_MID),
    )(w, g, m, v)


def _reduce_adamw(w, m, v, landed, name):
    nl, r, c = w.shape
    tr = next(cand for cand in range(min(r, ADAM_ROWS_MAX), 0, -BF16_ROWS) if r % cand == 0)
    nr = r // tr
    bc1 = 1.0 - ADAM_B1 ** ADAM_STEP
    bc2 = 1.0 - ADAM_B2 ** ADAM_STEP

    def body(*refs):
        w_ref, m_ref, v_ref = refs[:3]
        src_refs = refs[3:3 + nl]
        g_ref, d_ref, nm_ref, nv_ref = refs[3 + nl:]

        def update(src):
            gg = src[0].astype(F32)
            for s in range(1, N_DEV):
                gg = gg + src[s].astype(F32)
            nm = ADAM_B1 * m_ref[0] + (1.0 - ADAM_B1) * gg
            nv = ADAM_B2 * v_ref[0] + (1.0 - ADAM_B2) * (gg * gg)
            g_ref[0] = gg
            d_ref[0] = -ADAM_LR * ((nm / bc1) / (jnp.sqrt(nv / bc2) + ADAM_EPS) + ADAM_WD * w_ref[0])
            nm_ref[0] = nm
            nv_ref[0] = nv

        for idx in range(nl):
            pl.when(pl.program_id(0) == idx)(functools.partial(update, src_refs[idx]))

    def src_spec(idx):
        return pl.BlockSpec((N_DEV, tr, c),
                            lambda l, i: (0, jnp.where(l == idx, i, jnp.where(l < idx, 0, nr - 1)), 0))

    blk = pl.BlockSpec((1, tr, c), lambda l, i: (l, i, 0))
    shp = jax.ShapeDtypeStruct((nl, r, c), F32)
    return pl.pallas_call(
        body, name=name, grid=(nl, nr), in_specs=[blk] * 3 + [src_spec(idx) for idx in range(nl)],
        out_specs=[blk] * 4, out_shape=[shp] * 4,
        compiler_params=_params(("arbitrary", "arbitrary"), VMEM_MID),
    )(w, m, v, *landed)


def _sum_sources(r, name):
    n, rows, c = r.shape
    tr = next(cand for cand in range(min(rows, SUM_ROWS_MAX), 0, -BF16_ROWS) if rows % cand == 0)

    def body(r_ref, o_ref):
        acc = r_ref[0].astype(F32)
        for s in range(1, n):
            acc = acc + r_ref[s].astype(F32)
        o_ref[...] = acc

    return pl.pallas_call(
        body, name=name, grid=(rows // tr,),
        in_specs=[pl.BlockSpec((n, tr, c), lambda i: (0, i, 0))],
        out_specs=pl.BlockSpec((tr, c), lambda i: (i, 0)),
        out_shape=jax.ShapeDtypeStruct((rows, c), F32),
        compiler_params=_params(("arbitrary",), VMEM_MID),
    )(r)


def _all_gather(parts, name):
    n = len(parts)

    def body(*refs):
        x_refs, out_refs = refs[:n], refs[n:2 * n]
        send_sems, recv_sems, local_sems = refs[2 * n:]
        mx, my, mc = lax.axis_index("x"), lax.axis_index("y"), lax.axis_index("c")
        me, sibling = (mx, my, mc), (mx, my, 1 - mc)
        chips = [(1 - mx, my), (mx, 1 - my), (1 - mx, 1 - my)]

        def copy(p, k, block, to, from_input=False):
            px, py, pc = block
            rows = out_refs[p].at[4 * px + 2 * py + pc]
            return pltpu.make_async_remote_copy(
                src_ref=x_refs[p] if from_input else rows, dst_ref=rows,
                send_sem=send_sems.at[7 * p + k], recv_sem=recv_sems.at[7 * p + k],
                device_id=to, device_id_type=MESH)

        mine, sent = [], []
        for p in range(n):
            own = pltpu.make_async_copy(x_refs[p], out_refs[p].at[4 * mx + 2 * my + mc], local_sems.at[p])
            own.start()
            mine.append(own)
            first = [copy(p, 0, me, sibling, True)]
            first += [copy(p, 1 + j, me, (*chip, mc), True) for j, chip in enumerate(chips)]
            for cp in first:
                cp.start()
            sent += first
        for p in range(n):
            for j, chip in enumerate(chips):
                copy(p, 1 + j, (*chip, mc), me).wait_recv()
                fwd = copy(p, 4 + j, (*chip, mc), sibling)
                fwd.start()
                sent.append(fwd)
        for p in range(n):
            copy(p, 0, sibling, me).wait_recv()
            for j, chip in enumerate(chips):
                copy(p, 4 + j, (*chip, 1 - mc), me).wait_recv()
        for cp in sent:
            cp.wait_send()
        for own in mine:
            own.wait()

    return pl.pallas_call(
        body, name=name, in_specs=[ANY] * n, out_specs=[ANY] * n,
        out_shape=[jax.ShapeDtypeStruct((N_DEV,) + a.shape, a.dtype) for a in parts],
        scratch_shapes=[pltpu.SemaphoreType.DMA((7 * n,)), pltpu.SemaphoreType.DMA((7 * n,)),
                        pltpu.SemaphoreType.DMA((n,))],
    )(*parts)


def _pack_rows(parts, width, mult, lead=0):
    out = []
    for a in parts:
        head = a.shape[:lead]
        flat = a.reshape(head + (-1,))
        padn = (-flat.shape[-1]) % (width * mult)
        if padn:
            flat = jnp.pad(flat, [(0, 0)] * lead + [(0, padn)])
        out.append(flat.reshape(head + (-1, width)))
    return jnp.concatenate(out, axis=lead)


def _rows_of(shape, width, mult):
    n = math.prod(shape)
    per = width * mult
    return ((n + per - 1) // per) * mult


def _unpack_rows(buf, shapes, width, mult):
    lead = buf.shape[:-2]
    out, off = [], 0
    for shp in shapes:
        r = _rows_of(shp, width, mult)
        flat = buf[..., off:off + r, :].reshape(lead + (r * width,))
        out.append(flat[..., :math.prod(shp)].reshape(lead + tuple(shp)))
        off += r
    return out


def _cols_from_devices(g):
    nd = g.ndim
    perm = tuple(range(1, nd - 1)) + (0, nd - 1)
    t = jnp.transpose(g, perm)
    return t.reshape(t.shape[:-2] + (t.shape[-2] * t.shape[-1],))


def _cols_to_devices(a):
    c = a.shape[-1] // N_DEV
    t = a.reshape(a.shape[:-1] + (N_DEV, c))
    nd = t.ndim
    perm = (nd - 2,) + tuple(range(0, nd - 2)) + (nd - 1,)
    return jnp.transpose(t, perm)


WIDTH = 1024


def kernel(x, meta, ffn1_wg, ffn1_wu, ffn1_wd, ffn2_wg, ffn2_wu, ffn2_wd, ln_gain, ln_bias, conv_w_in, conv_w, conv_w_out, kv_w, f_bias, attn_w_q, attn_w_o, loss_target, m_meta, m_ffn1_wg, m_ffn1_wu, m_ffn1_wd, m_ffn2_wg, m_ffn2_wu, m_ffn2_wd, m_ln_gain, m_ln_bias, m_conv_w_in, m_conv_w, m_conv_w_out, m_kv_w, m_f_bias, m_attn_w_q, m_attn_w_o, v_meta, v_ffn1_wg, v_ffn1_wu, v_ffn1_wd, v_ffn2_wg, v_ffn2_wu, v_ffn2_wd, v_ln_gain, v_ln_bias, v_conv_w_in, v_conv_w, v_conv_w_out, v_kv_w, v_f_bias, v_attn_w_q, v_attn_w_o):
    depth = ln_gain.shape[0]
    alpha = float((2 * depth) ** 0.25)
    d = x.shape[-1]
    seq = x.shape[1]
    t = ROW0 + seq
    fsh = ffn1_wg.shape[-1]
    f = fsh * N_DEV
    fck = MXU_COLS
    nc = f // fck
    me = 4 * lax.axis_index("x") + 2 * lax.axis_index("y") + lax.axis_index("c")

    def gather_of(parts):
        return [(True, a.astype(BF16)) for a in parts]

    small = [meta, ln_gain, ln_bias, conv_w]
    small_shapes = [a.shape for a in small]
    g1g, g1u, g1d, gcin, gcout = _all_gather(
        [a.astype(BF16) for a in (ffn1_wg[0], ffn1_wu[0], ffn1_wd[0], conv_w_in[0], conv_w_out[0])], "ag_first")
    (gsmall,) = _all_gather([_pack_rows(small, WIDTH, F32_ROWS)], "ag_small")
    gmeta, ggain, gbias, gcw = _unpack_rows(gsmall, small_shapes, WIDTH, F32_ROWS)

    def ffn_chunks(gg, gu, gd):
        up = lambda g: jnp.transpose(_cols_from_devices(g).reshape(d, nc, fck), (1, 0, 2))
        return up(gg), up(gu), gd.reshape(nc, fck, d)

    w_in = _cols_from_devices(gcin)
    w_out = gcout.reshape(d, d)
    fb = jnp.pad(f_bias, (0, LANES - N_HEADS)).reshape(1, LANES)
    meta_f = _cols_from_devices(gmeta)
    gain_f = _cols_from_devices(ggain)
    bias_f = _cols_from_devices(gbias)
    cw_f = _cols_from_devices(gcw)[0]

    def gb(l, n):
        return gain_f[l, n].reshape(1, d), bias_f[l, n].reshape(1, d)

    ones = jnp.ones((1, d), F32)
    zeros = jnp.zeros((1, d), F32)

    h0 = jnp.concatenate([jnp.zeros((PAD, d), F32), meta_f, x[0]], axis=0)
    hb0 = _cast_t(h0, "h0_bf16_t")

    w1 = ffn_chunks(g1g, g1u, g1d)
    g00, b00 = gb(0, 0)
    xh1, rs1, hb1, gg1, uu1, g2g, g2u, g2d = _ffn_fwd(
        h0, ones, zeros, *w1, g00, b00, alpha, "ffn_fwd_0a",
        carry=gather_of([ffn2_wg[0], ffn2_wu[0], ffn2_wd[0]]))
    g01, b01 = gb(0, 1)
    xh2, rs2, hb2, pp, mb, gkv, gwq = _conv_fwd(
        xh1, g00, b00, w_in, cw_f, w_out, g01, b01, alpha, "conv_fwd", carry=gather_of([kv_w, attn_w_q[0]]))
    w2 = ffn_chunks(g2g, g2u, g2d)
    g02, b02 = gb(0, 2)
    xh3, rs3, hb3, gg3, uu3, g3g, g3u, g3d = _ffn_fwd(
        xh2, g01, b01, *w2, g02, b02, alpha, "ffn_fwd_0b",
        carry=gather_of([ffn1_wg[1], ffn1_wu[1], ffn1_wd[1]]))
    kvw = _cols_from_devices(gkv)
    wk, wv = kvw[:, :d], kvw[:, d:2 * d]
    wf = jnp.pad(kvw[:, 2 * d:], ((0, 0), (0, LANES - N_HEADS)))
    kk, vv, logit, cc, cct = _kv_fwd(xh3, g02, b02, wk, wv, wf, fb, "kv_fwd")

    w3 = ffn_chunks(g3g, g3u, g3d)
    g10, b10 = gb(1, 0)
    xh4, rs4, hb4, gg4, uu4 = _ffn_fwd(xh3, g02, b02, *w3, g10, b10, alpha, "ffn_fwd_1a")
    w_q = gwq.reshape(d, d)
    qq = _proj(xh4, g10, b10, w_q, "q_proj")
    ot, lse, gwo, g4g, g4u, g4d = _attn_fwd(
        qq, kk, vv, cc, cct, "attn_fwd", carry=gather_of([attn_w_o[0], ffn2_wg[1], ffn2_wu[1], ffn2_wd[1]]))
    w_o = gwo.reshape(d, d)
    g11, b11 = gb(1, 1)
    xh5, rs5, hb5 = _attn_out_fwd(ot, xh4, g10, b10, w_o, g11, b11, alpha, "attn_out_fwd")
    w4 = ffn_chunks(g4g, g4u, g4d)
    g12, b12 = gb(1, 2)
    xh6, rs6, _, gg6, uu6 = _ffn_fwd(xh5, g11, b11, *w4, g12, b12, alpha, "ffn_fwd_1b")

    dh6, loss_l = _loss_head(xh6, g12, b12, loss_target[0], "loss_head")
    loss = lax.psum(loss_l[0, 0], ("x", "y", "c"))

    dgain = [[None] * 3 for _ in range(depth)]
    dbias = [[None] * 3 for _ in range(depth)]

    def to_col_owners(g):
        return (False, _cols_to_devices(g).astype(BF16))

    def to_row_owners(g):
        return (False, g.reshape(N_DEV, g.shape[0] // N_DEV, g.shape[1]).astype(BF16))

    dh5, do6, dg6, du6, a6, dgain[1][2], dbias[1][2] = _ffn_bwd(dh6, xh6, rs6, g12, gg6, uu6, *w4, alpha, "ffn_bwd_1b")
    dw4g, dw4u = _wgrad(hb5, [dg6, du6], "wgrad_up_1b")
    (dw4dt,) = _wgrad(do6, [a6], "wgrad_down_1b")

    dres4, dmix5, dot_t, delta, dgain[1][1], dbias[1][1] = _attn_out_bwd(dh5, xh5, rs5, g11, ot, w_o, alpha, "attn_out_bwd")
    (dwo,) = _wgrad(ot, [dmix5], "wgrad_wo")
    dq, dkk, dvv, dcs, drow, l4g, l4u, l4d, lwo = _attn_bwd(
        qq, kk, vv, cc, cct, lse, delta, dot_t, "attn_bwd",
        carry=[to_col_owners(dw4g), to_col_owners(dw4u), to_row_owners(dw4dt.T), to_row_owners(dwo)])
    dh4 = _add_proj_nt(dres4, dq, w_q, "q_bwd")
    (dwq,) = _wgrad(hb4, [dq], "wgrad_wq")

    dh3a, do4, dg4, du4, a4, dgain[1][0], dbias[1][0] = _ffn_bwd(dh4, xh4, rs4, g10, gg4, uu4, *w3, alpha, "ffn_bwd_1a")
    dw3g, dw3u = _wgrad(hb3, [dg4, du4], "wgrad_up_1a")
    (dw3dt,) = _wgrad(do4, [a4], "wgrad_down_1a")

    dcq = jnp.pad(drow[:, :, 0, :].reshape(N_HEADS, t).T, ((0, 0), (0, LANES - N_HEADS)))
    dh3, dlogit, dfb = _kv_bwd(dkk, dvv, dcs, dcq, logit, dh3a, wk, wv, wf, "kv_bwd")
    dwk, dwv = _wgrad(hb3, [dkk, dvv], "wgrad_kv")
    (dwf,) = _wgrad(hb3, [dlogit], "wgrad_f")
    dkv = jnp.concatenate([dwk, dwv, dwf[:, :N_HEADS]], axis=1)

    dh2, do3, dg3, du3, a3, dgain[0][2], dbias[0][2], lwq, l3g, l3u, l3d, lkv = _ffn_bwd(
        dh3, xh3, rs3, g02, gg3, uu3, *w2, alpha, "ffn_bwd_0b",
        carry=[to_row_owners(dwq), to_col_owners(dw3g), to_col_owners(dw3u), to_row_owners(dw3dt.T),
               to_col_owners(dkv)])
    dw2g, dw2u = _wgrad(hb2, [dg3, du3], "wgrad_up_0b")
    (dw2dt,) = _wgrad(do3, [a3], "wgrad_down_0b")

    dh1, dmix2, dpp, dcw, dgain[0][1], dbias[0][1] = _conv_bwd(dh2, xh2, rs2, g01, pp, cw_f, w_in, w_out, alpha, "conv_bwd")
    (dwin,) = _wgrad(hb1, [dpp], "wgrad_conv_in")
    (dwout,) = _wgrad(mb, [dmix2], "wgrad_conv_out")

    dh0, do1, dg1, du1, a1, dgain[0][0], dbias[0][0], l2g, l2u, l2d, lcin, lcout = _ffn_bwd(
        dh1, xh1, rs1, g00, gg1, uu1, *w1, alpha, "ffn_bwd_0a",
        carry=[to_col_owners(dw2g), to_col_owners(dw2u), to_row_owners(dw2dt.T), to_col_owners(dwin),
               to_row_owners(dwout)])
    (dw1dt,) = _wgrad(do1, [a1], "wgrad_down_0a")
    dw1g, l1d = _wgrad(hb0, [dg1], "wgrad_upg_0a", carry=[to_row_owners(dw1dt.T)])
    dw1u, l1g = _wgrad(hb0, [du1], "wgrad_upu_0a", carry=[to_col_owners(dw1g)])
    (l1u,) = _exchange([to_col_owners(dw1u)[1]], "rs_last")

    grad_x = dh0[ROW0:].reshape(1, seq, d)

    dmeta = dh0[PAD:ROW0]
    dgain_f = jnp.stack([jnp.concatenate(r, axis=0) for r in dgain])
    dbias_f = jnp.stack([jnp.concatenate(r, axis=0) for r in dbias])
    small_full = [dmeta, dgain_f, dbias_f, dcw[None], dfb]
    small_full_shapes = [a.shape for a in small_full]
    (gsmall_grads,) = _all_gather([_pack_rows(small_full, WIDTH, F32_ROWS)], "ag_small_grads")
    rsmall = _sum_sources(gsmall_grads, "small_sum")
    smeta, sgain, sbias, scw, sfb = _unpack_rows(rsmall, small_full_shapes, WIDTH, F32_ROWS)
    csh = d // N_DEV

    def my_cols(a):
        return lax.dynamic_slice_in_dim(a, me * csh, csh, axis=a.ndim - 1)

    grads = {"meta": my_cols(smeta), "ln_gain": my_cols(sgain), "ln_bias": my_cols(sbias),
             "conv_w": my_cols(scw), "f_bias": sfb[0, :N_HEADS]}
    landed = {"ffn1_wg": [l1g, l3g], "ffn1_wu": [l1u, l3u], "ffn1_wd": [l1d, l3d],
              "ffn2_wg": [l2g, l4g], "ffn2_wu": [l2u, l4u], "ffn2_wd": [l2d, l4d],
              "conv_w_in": [lcin], "conv_w_out": [lcout], "kv_w": [lkv], "attn_w_q": [lwq], "attn_w_o": [lwo]}
    weights = dict(meta=meta, ffn1_wg=ffn1_wg, ffn1_wu=ffn1_wu, ffn1_wd=ffn1_wd, ffn2_wg=ffn2_wg,
                   ffn2_wu=ffn2_wu, ffn2_wd=ffn2_wd, ln_gain=ln_gain, ln_bias=ln_bias,
                   conv_w_in=conv_w_in, conv_w=conv_w, conv_w_out=conv_w_out, kv_w=kv_w,
                   f_bias=f_bias, attn_w_q=attn_w_q, attn_w_o=attn_w_o)
    moms = dict(meta=(m_meta, v_meta), ffn1_wg=(m_ffn1_wg, v_ffn1_wg), ffn1_wu=(m_ffn1_wu, v_ffn1_wu),
                ffn1_wd=(m_ffn1_wd, v_ffn1_wd), ffn2_wg=(m_ffn2_wg, v_ffn2_wg), ffn2_wu=(m_ffn2_wu, v_ffn2_wu),
                ffn2_wd=(m_ffn2_wd, v_ffn2_wd), ln_gain=(m_ln_gain, v_ln_gain), ln_bias=(m_ln_bias, v_ln_bias),
                conv_w_in=(m_conv_w_in, v_conv_w_in), conv_w=(m_conv_w, v_conv_w),
                conv_w_out=(m_conv_w_out, v_conv_w_out), kv_w=(m_kv_w, v_kv_w), f_bias=(m_f_bias, v_f_bias),
                attn_w_q=(m_attn_w_q, v_attn_w_q), attn_w_o=(m_attn_w_o, v_attn_w_o))

    names = list(weights)
    g_out, d_out, m_out, v_out = [], [], [], []
    for n in names:
        w = weights[n]
        shp = w.shape
        mm, vv_ = moms[n]
        if n in landed:
            three = (len(landed[n]),) + shp[-2:]
            g, dl, nm, nv = _reduce_adamw(w.reshape(three), mm.reshape(three), vv_.reshape(three),
                                          landed[n], "adamw_" + n)
            g = g.reshape(shp)
        else:
            two = (1, shp[0]) if w.ndim == 1 else (math.prod(shp[:-1]), shp[-1])
            g = grads[n].reshape(shp)
            dl, nm, nv = _adamw(w.reshape(two), g.reshape(two), mm.reshape(two), vv_.reshape(two), "adamw_" + n)
        g_out.append(g)
        d_out.append(dl.reshape(shp))
        m_out.append(nm.reshape(shp))
        v_out.append(nv.reshape(shp))
    return (loss, grad_x, *g_out, *d_out, *m_out, *v_out)
```

```python
import functools
import math

import jax
import jax.numpy as jnp
from jax import lax
from jax.experimental import pallas as pl
from jax.experimental.pallas import tpu as pltpu

F32 = jnp.float32
BF16 = jnp.bfloat16

N_DEV = 8
N_HEADS = 8
N_META = 16
PAD = 112
ROW0 = PAD + N_META
LN_EPS = 1e-5
NEG_INF = -1e30
LOG2E = 1.4426950408889634
ATTN_HEADS_PER_STEP = 4
ATTN_BWD_HEADS_PER_STEP = 2
LANES = 128
MXU_COLS = 256
FFN_FWD_CHUNKS = 11
FFN_BWD_CHUNKS = 4

ADAM_LR = 0.001
ADAM_B1 = 0.9
ADAM_B2 = 0.999
ADAM_EPS = 1e-08
ADAM_WD = 0.01
ADAM_STEP = 10

ROW_TILES = (640, 128)
LOSS_TILE = 128
BF16_ROWS = 16
F32_ROWS = 8
SUM_ROWS_MAX = 768
ADAM_ROWS_MAX = 256
VMEM_BIG = 56 << 20
VMEM_MID = 40 << 20

ANY = pl.BlockSpec(memory_space=pl.ANY)
MESH = pl.DeviceIdType.MESH


def _row_tile(t):
    for c in ROW_TILES:
        if t % c == 0:
            return c
    raise ValueError(f"no row tile for {t}")


def _dot(a, b):
    return jnp.dot(a, b, preferred_element_type=F32)


def _dot_nt(a, b):
    return lax.dot_general(a, b, (((1,), (1,)), ((), ())), preferred_element_type=F32)


def _dot_tn(a, b):
    return lax.dot_general(a, b, (((0,), (0,)), ((), ())), preferred_element_type=F32)


def _params(sem, vmem):
    return pltpu.CompilerParams(dimension_semantics=sem, vmem_limit_bytes=vmem)


def _ln_fwd(z):
    mu = jnp.mean(z, axis=-1, keepdims=True)
    zc = z - mu
    var = jnp.mean(zc * zc, axis=-1, keepdims=True)
    rstd = lax.rsqrt(var + LN_EPS)
    return zc * rstd, rstd


def _ln_bwd(dh, xhat, rstd, gain):
    dxh = dh * gain
    m1 = jnp.mean(dxh, axis=-1, keepdims=True)
    m2 = jnp.mean(dxh * xhat, axis=-1, keepdims=True)
    dz = rstd * (dxh - m1 - xhat * m2)
    return dz, jnp.sum(dh * xhat, axis=0, keepdims=True), jnp.sum(dh, axis=0, keepdims=True)


def _load_resident(pairs, sems):
    cps = [pltpu.make_async_copy(src, dst, sems.at[k]) for k, (src, dst) in enumerate(pairs)]
    for cp in cps:
        cp.start()
    for cp in cps:
        cp.wait()


def _peer_ids():
    mx, my, mc = lax.axis_index("x"), lax.axis_index("y"), lax.axis_index("c")
    peers = []
    for kk in range(1, N_DEV):
        px = 1 - mx if (kk >> 2) & 1 else mx
        py = 1 - my if (kk >> 1) & 1 else my
        pc = 1 - mc if kk & 1 else mc
        peers.append(((px, py, pc), 4 * px + 2 * py + pc))
    return 4 * mx + 2 * my + mc, peers


def _exchange_copies(jobs, send_sems, recv_sems, local_sems, starting):
    me_id, peers = _peer_ids()
    for n, (gather, src, dst) in enumerate(jobs):
        own = pltpu.make_async_copy(src if gather else src.at[me_id], dst.at[me_id], local_sems.at[n])
        own.start() if starting else own.wait()
        for k, (dev, pid) in enumerate(peers):
            sem = (N_DEV - 1) * n + k
            out = src if gather else src.at[pid]
            send = pltpu.make_async_remote_copy(
                src_ref=out, dst_ref=dst.at[me_id], send_sem=send_sems.at[sem], recv_sem=recv_sems.at[sem],
                device_id=dev, device_id_type=MESH)
            if starting:
                send.start()
            else:
                pltpu.make_async_remote_copy(
                    src_ref=out, dst_ref=dst.at[pid], send_sem=send_sems.at[sem], recv_sem=recv_sems.at[sem],
                    device_id=dev, device_id_type=MESH).wait_recv()
                send.wait_send()


def _carried(body, n_in, n_out, carry, first, last):
    nj = len(carry)
    if nj == 0:
        return body

    def wrapped(*refs):
        ins, srcs = refs[:n_in], refs[n_in:n_in + nj]
        outs = refs[n_in + nj:n_in + nj + n_out]
        dsts = refs[n_in + nj + n_out:n_in + 2 * nj + n_out]
        scratch, sems = refs[n_in + 2 * nj + n_out:-3], refs[-3:]
        jobs = [(g, s, r) for (g, _), s, r in zip(carry, srcs, dsts)]

        @pl.when(first())
        def _():
            _exchange_copies(jobs, *sems, starting=True)

        body(*ins, *outs, *scratch)

        @pl.when(last())
        def _():
            _exchange_copies(jobs, *sems, starting=False)

    return wrapped


def _carry_shapes(carry):
    return [jax.ShapeDtypeStruct((N_DEV,) + a.shape if g else a.shape, a.dtype) for g, a in carry]


def _carry_scratch(carry):
    if not carry:
        return []
    n = len(carry)
    return [pltpu.SemaphoreType.DMA(((N_DEV - 1) * n,)), pltpu.SemaphoreType.DMA(((N_DEV - 1) * n,)),
            pltpu.SemaphoreType.DMA((n,))]


def _exchange(carry, name):
    n = len(carry)

    def body(*refs):
        jobs = [(g, s, r) for (g, _), s, r in zip(carry, refs[:n], refs[n:2 * n])]
        _exchange_copies(jobs, *refs[2 * n:], starting=True)
        _exchange_copies(jobs, *refs[2 * n:], starting=False)

    return pl.pallas_call(
        body, name=name, in_specs=[ANY] * n, out_specs=[ANY] * n, out_shape=_carry_shapes(carry),
        scratch_shapes=_carry_scratch(carry),
    )(*[a for _, a in carry])


def _ffn_fwd(xh, gi, bi, wg, wu, wd, go, bo, alpha, name, carry=()):
    t, d = xh.shape
    nch, _, fc = wg.shape
    f = nch * fc
    per = min(FFN_FWD_CHUNKS, nch)
    nc = -(-nch // per)
    tm = _row_tile(t)
    nt = t // tm

    def body(xh_ref, gi_ref, bi_ref, wg_hbm, wu_hbm, wd_hbm, go_ref, bo_ref,
             xo_ref, rs_ref, hb_ref, g_ref, u_ref,
             wg_v, wu_v, wd_v, acc, hbs, sems):
        i = pl.program_id(0)
        c = pl.program_id(1)

        @pl.when((i == 0) & (c == 0))
        def _():
            _load_resident([(wg_hbm, wg_v), (wu_hbm, wu_v), (wd_hbm, wd_v)], sems)

        @pl.when(c == 0)
        def _():
            h = xh_ref[...] * gi_ref[...] + bi_ref[...]
            hbs[...] = h.astype(BF16)
            acc[...] = jnp.zeros_like(acc)

        def chunk(k):
            ck = c * per + k
            cols = slice(k * fc, (k + 1) * fc)
            hb = hbs[...]
            g = _dot(hb, wg_v[ck])
            u = _dot(hb, wu_v[ck])
            a = (g * jax.nn.sigmoid(g)) * u
            g_ref[:, cols] = g.astype(BF16)
            u_ref[:, cols] = u.astype(BF16)
            acc[...] += _dot(a.astype(BF16), wd_v[ck])

        for k in range(per):
            if (nc - 1) * per + k < nch:
                chunk(k)
            else:
                pl.when(c * per + k < nch)(functools.partial(chunk, k))

        @pl.when(c == nc - 1)
        def _():
            h = xh_ref[...] * gi_ref[...] + bi_ref[...]
            xhat, rstd = _ln_fwd(alpha * h + 0.5 * acc[...])
            xo_ref[...] = xhat
            rs_ref[...] = rstd
            hb_ref[...] = (xhat * go_ref[...] + bo_ref[...]).astype(BF16).T

    row = pl.BlockSpec((tm, d), lambda i, c: (i, 0))
    vec = pl.BlockSpec((1, d), lambda i, c: (0, 0))
    chunk = pl.BlockSpec((tm, per * fc), lambda i, c: (i, c))
    first = lambda: (pl.program_id(0) == 0) & (pl.program_id(1) == 0)
    last = lambda: (pl.program_id(0) == nt - 1) & (pl.program_id(1) == nc - 1)
    return pl.pallas_call(
        _carried(body, 8, 5, carry, first, last), name=name, grid=(nt, nc),
        in_specs=[row, vec, vec, ANY, ANY, ANY, vec, vec] + [ANY] * len(carry),
        out_specs=[row, pl.BlockSpec((tm, 1), lambda i, c: (i, 0)),
                   pl.BlockSpec((d, tm), lambda i, c: (0, i)), chunk, chunk] + [ANY] * len(carry),
        out_shape=[jax.ShapeDtypeStruct((t, d), F32), jax.ShapeDtypeStruct((t, 1), F32),
                   jax.ShapeDtypeStruct((d, t), BF16), jax.ShapeDtypeStruct((t, f), BF16),
                   jax.ShapeDtypeStruct((t, f), BF16)] + _carry_shapes(carry),
        scratch_shapes=[pltpu.VMEM((nch, d, fc), BF16), pltpu.VMEM((nch, d, fc), BF16),
                        pltpu.VMEM((nch, fc, d), BF16), pltpu.VMEM((tm, d), F32),
                        pltpu.VMEM((tm, d), BF16), pltpu.SemaphoreType.DMA((3,))] + _carry_scratch(carry),
        compiler_params=_params(("arbitrary", "arbitrary"), VMEM_BIG),
    )(xh, gi, bi, wg, wu, wd, go, bo, *[a for _, a in carry])


def _ffn_bwd(dh, xo, rs, go, gs, us, wg, wu, wd, alpha, name, carry=()):
    t, d = dh.shape
    nch, _, fc = wg.shape
    f = nch * fc
    per = min(FFN_BWD_CHUNKS, nch)
    nc = -(-nch // per)
    tm = _row_tile(t)
    nt = t // tm

    def body(dh_ref, xo_ref, rs_ref, go_ref, g_ref, u_ref, wg_hbm, wu_hbm, wd_hbm,
             dhin_ref, dot_ref, dg_ref, du_ref, a_ref, dgain_ref, dbias_ref,
             wg_v, wu_v, wd_v, do_ref, sems):
        i = pl.program_id(0)
        c = pl.program_id(1)

        @pl.when((i == 0) & (c == 0))
        def _():
            _load_resident([(wg_hbm, wg_v), (wu_hbm, wu_v), (wd_hbm, wd_v)], sems)
            dgain_ref[...] = jnp.zeros_like(dgain_ref)
            dbias_ref[...] = jnp.zeros_like(dbias_ref)

        @pl.when(c == 0)
        def _():
            dz, dgp, dbp = _ln_bwd(dh_ref[...], xo_ref[...], rs_ref[...], go_ref[...])
            dgain_ref[...] += dgp
            dbias_ref[...] += dbp
            dob = (0.5 * dz).astype(BF16)
            do_ref[...] = dob
            dot_ref[...] = dob.T
            dhin_ref[...] = alpha * dz

        def chunk(k):
            ck = c * per + k
            cols = slice(k * fc, (k + 1) * fc)
            g = g_ref[:, cols].astype(F32)
            u = u_ref[:, cols].astype(F32)
            sg = jax.nn.sigmoid(g)
            sl = g * sg
            da = _dot_nt(do_ref[...], wd_v[ck])
            dgb = (da * u * (sg * (1.0 + g * (1.0 - sg)))).astype(BF16)
            dub = (da * sl).astype(BF16)
            a_ref[:, cols] = (sl * u).astype(BF16)
            dg_ref[:, cols] = dgb
            du_ref[:, cols] = dub
            dhin_ref[...] += _dot_nt(dgb, wg_v[ck]) + _dot_nt(dub, wu_v[ck])

        for k in range(per):
            if (nc - 1) * per + k < nch:
                chunk(k)
            else:
                pl.when(c * per + k < nch)(functools.partial(chunk, k))

    row = pl.BlockSpec((tm, d), lambda i, c: (i, 0))
    vec = pl.BlockSpec((1, d), lambda i, c: (0, 0))
    chunk = pl.BlockSpec((tm, per * fc), lambda i, c: (i, c))
    first = lambda: (pl.program_id(0) == 0) & (pl.program_id(1) == 0)
    last = lambda: (pl.program_id(0) == nt - 1) & (pl.program_id(1) == nc - 1)
    return pl.pallas_call(
        _carried(body, 9, 7, carry, first, last), name=name, grid=(nt, nc),
        in_specs=[row, row, pl.BlockSpec((tm, 1), lambda i, c: (i, 0)), vec, chunk, chunk,
                  ANY, ANY, ANY] + [ANY] * len(carry),
        out_specs=[row, pl.BlockSpec((d, tm), lambda i, c: (0, i)), chunk, chunk, chunk, vec, vec]
                  + [ANY] * len(carry),
        out_shape=[jax.ShapeDtypeStruct((t, d), F32), jax.ShapeDtypeStruct((d, t), BF16),
                   jax.ShapeDtypeStruct((t, f), BF16), jax.ShapeDtypeStruct((t, f), BF16),
                   jax.ShapeDtypeStruct((t, f), BF16), jax.ShapeDtypeStruct((1, d), F32),
                   jax.ShapeDtypeStruct((1, d), F32)] + _carry_shapes(carry),
        scratch_shapes=[pltpu.VMEM((nch, d, fc), BF16), pltpu.VMEM((nch, d, fc), BF16),
                        pltpu.VMEM((nch, fc, d), BF16), pltpu.VMEM((tm, d), BF16),
                        pltpu.SemaphoreType.DMA((3,))] + _carry_scratch(carry),
        compiler_params=_params(("arbitrary", "arbitrary"), VMEM_BIG),
    )(dh, xo, rs, go, gs, us, wg, wu, wd, *[a for _, a in carry])


def _wgrad(xt, ys, name, carry=()):
    m, t = xt.shape
    n = ys[0].shape[1]
    tn = min(n, MXU_COLS)
    ny = len(ys)

    def body(*refs):
        x_hbm = refs[0]
        y_refs = refs[1:1 + ny]
        o_refs = refs[1 + ny:1 + 2 * ny]
        xv, sems = refs[1 + 2 * ny:]

        @pl.when(pl.program_id(0) == 0)
        def _():
            _load_resident([(x_hbm, xv)], sems)

        for y_ref, o_ref in zip(y_refs, o_refs):
            o_ref[...] = _dot(xv[...], y_ref[...].astype(BF16))

    steps = n // tn
    first = lambda: pl.program_id(0) == 0
    last = lambda: pl.program_id(0) == steps - 1
    return pl.pallas_call(
        _carried(body, 1 + ny, ny, carry, first, last), name=name, grid=(steps,),
        in_specs=[ANY] + [pl.BlockSpec((t, tn), lambda c: (0, c)) for _ in ys] + [ANY] * len(carry),
        out_specs=[pl.BlockSpec((m, tn), lambda c: (0, c)) for _ in ys] + [ANY] * len(carry),
        out_shape=[jax.ShapeDtypeStruct((m, n), F32) for _ in ys] + _carry_shapes(carry),
        scratch_shapes=[pltpu.VMEM((m, t), BF16), pltpu.SemaphoreType.DMA((1,))] + _carry_scratch(carry),
        compiler_params=_params(("arbitrary",), VMEM_BIG),
    )(xt, *ys, *[a for _, a in carry])


def _cast_t(h, name):
    t, d = h.shape
    tm = _row_tile(t)

    def body(h_ref, o_ref):
        o_ref[...] = h_ref[...].astype(BF16).T

    return pl.pallas_call(
        body, name=name, grid=(t // tm,),
        in_specs=[pl.BlockSpec((tm, d), lambda i: (i, 0))],
        out_specs=pl.BlockSpec((d, tm), lambda i: (0, i)),
        out_shape=jax.ShapeDtypeStruct((d, t), BF16),
        compiler_params=_params(("arbitrary",), VMEM_MID),
    )(h)


def _shift_rows(u, halo, tm):
    r = lax.broadcasted_iota(jnp.int32, (tm, 1), 0)
    u1 = jnp.where(r == 0, halo[7:8], pltpu.roll(u, 1, 0))
    u2 = jnp.where(r == 0, halo[6:7], jnp.where(r == 1, halo[7:8], pltpu.roll(u, 2, 0)))
    return u1, u2


def _conv_fwd(xh, gi, bi, w_in, cw, w_out, go, bo, alpha, name, carry=()):
    t, d = xh.shape
    tm = _row_tile(t)
    nt = t // tm

    def body(xh_ref, gi_ref, bi_ref, win_ref, cw_ref, wout_ref, go_ref, bo_ref,
             xo_ref, rs_ref, hb_ref, p_ref, m_ref, halo):
        i = pl.program_id(0)

        @pl.when(i == 0)
        def _():
            halo[...] = jnp.zeros_like(halo)

        h = xh_ref[...] * gi_ref[...] + bi_ref[...]
        hb = h.astype(BF16)
        bg = _dot(hb, win_ref[:, 0:d])
        cg = _dot(hb, win_ref[:, d:2 * d])
        val = _dot(hb, win_ref[:, 2 * d:3 * d])
        p_ref[:, 0:d] = bg.astype(BF16)
        p_ref[:, d:2 * d] = cg.astype(BF16)
        p_ref[:, 2 * d:3 * d] = val.astype(BF16)
        rows = i * tm + lax.broadcasted_iota(jnp.int32, (tm, 1), 0)
        u = jnp.where(rows >= PAD, cg * val, 0.0)
        u1, u2 = _shift_rows(u, halo[...], tm)
        halo[...] = u[tm - 8:tm]
        y = cw_ref[0:1] * u2 + cw_ref[1:2] * u1 + cw_ref[2:3] * u
        mb = (bg * y).astype(BF16)
        m_ref[...] = mb.T
        xhat, rstd = _ln_fwd(alpha * h + _dot(mb, wout_ref[...]))
        xo_ref[...] = xhat
        rs_ref[...] = rstd
        hb_ref[...] = (xhat * go_ref[...] + bo_ref[...]).astype(BF16).T

    row = pl.BlockSpec((tm, d), lambda i: (i, 0))
    col = pl.BlockSpec((d, tm), lambda i: (0, i))
    vec = pl.BlockSpec((1, d), lambda i: (0, 0))
    first = lambda: pl.program_id(0) == 0
    last = lambda: pl.program_id(0) == nt - 1
    return pl.pallas_call(
        _carried(body, 8, 5, carry, first, last), name=name, grid=(nt,),
        in_specs=[row, vec, vec, pl.BlockSpec((d, 3 * d), lambda i: (0, 0)),
                  pl.BlockSpec((3, d), lambda i: (0, 0)), pl.BlockSpec((d, d), lambda i: (0, 0)),
                  vec, vec] + [ANY] * len(carry),
        out_specs=[row, pl.BlockSpec((tm, 1), lambda i: (i, 0)), col,
                   pl.BlockSpec((tm, 3 * d), lambda i: (i, 0)), col] + [ANY] * len(carry),
        out_shape=[jax.ShapeDtypeStruct((t, d), F32), jax.ShapeDtypeStruct((t, 1), F32),
                   jax.ShapeDtypeStruct((d, t), BF16), jax.ShapeDtypeStruct((t, 3 * d), BF16),
                   jax.ShapeDtypeStruct((d, t), BF16)] + _carry_shapes(carry),
        scratch_shapes=[pltpu.VMEM((8, d), F32)] + _carry_scratch(carry),
        compiler_params=_params(("arbitrary",), VMEM_BIG),
    )(xh, gi, bi, w_in, cw, w_out, go, bo, *[a for _, a in carry])


def _conv_bwd(dh, xo, rs, go, p, cw, w_in, w_out, alpha, name):
    t, d = dh.shape
    tm = _row_tile(t)
    nt = t // tm
    tb = tm // 8

    def body(dh_ref, xo_ref, rs_ref, go_ref, p_ref, ph_ref, cw_ref, win_ref, wout_ref,
             dhin_ref, dmix_ref, dp_ref, dcw_ref, dgain_ref, dbias_ref, carry):
        i = pl.program_id(0)
        tile = nt - 1 - i

        @pl.when(i == 0)
        def _():
            carry[...] = jnp.zeros_like(carry)
            dcw_ref[...] = jnp.zeros_like(dcw_ref)
            dgain_ref[...] = jnp.zeros_like(dgain_ref)
            dbias_ref[...] = jnp.zeros_like(dbias_ref)

        dz, dgp, dbp = _ln_bwd(dh_ref[...], xo_ref[...], rs_ref[...], go_ref[...])
        dgain_ref[...] += dgp
        dbias_ref[...] += dbp
        dmixb = dz.astype(BF16)
        dmix_ref[...] = dmixb
        dm = _dot_nt(dmixb, wout_ref[...])

        bg = p_ref[:, 0:d].astype(F32)
        cg = p_ref[:, d:2 * d].astype(F32)
        val = p_ref[:, 2 * d:3 * d].astype(F32)
        rows = tile * tm + lax.broadcasted_iota(jnp.int32, (tm, 1), 0)
        valid = rows >= PAD
        u = jnp.where(valid, cg * val, 0.0)
        hrows = tile * tm - 8 + lax.broadcasted_iota(jnp.int32, (8, 1), 0)
        hu = jnp.where((hrows >= PAD) & (tile > 0),
                       ph_ref[:, d:2 * d].astype(F32) * ph_ref[:, 2 * d:3 * d].astype(F32), 0.0)
        u1, u2 = _shift_rows(u, hu, tm)
        w0, w1, w2 = cw_ref[0:1], cw_ref[1:2], cw_ref[2:3]
        y = w0 * u2 + w1 * u1 + w2 * u
        dbg = dm * y
        dy = dm * bg
        dcw_ref[0:1] += jnp.sum(dy * u2, axis=0, keepdims=True)
        dcw_ref[1:2] += jnp.sum(dy * u1, axis=0, keepdims=True)
        dcw_ref[2:3] += jnp.sum(dy * u, axis=0, keepdims=True)

        nxt = carry[...]
        r = lax.broadcasted_iota(jnp.int32, (tm, 1), 0)
        dy1 = jnp.where(r == tm - 1, nxt[0:1], pltpu.roll(dy, tm - 1, 0))
        dy2 = jnp.where(r == tm - 2, nxt[0:1],
                        jnp.where(r == tm - 1, nxt[1:2], pltpu.roll(dy, tm - 2, 0)))
        carry[...] = dy[0:8]
        du = jnp.where(valid, w2 * dy + w1 * dy1 + w0 * dy2, 0.0)
        dbgb = dbg.astype(BF16)
        dcgb = (du * val).astype(BF16)
        dvalb = (du * cg).astype(BF16)
        dp_ref[:, 0:d] = dbgb
        dp_ref[:, d:2 * d] = dcgb
        dp_ref[:, 2 * d:3 * d] = dvalb
        dhin_ref[...] = (alpha * dz + _dot_nt(dbgb, win_ref[:, 0:d])
                         + _dot_nt(dcgb, win_ref[:, d:2 * d]) + _dot_nt(dvalb, win_ref[:, 2 * d:3 * d]))

    row = pl.BlockSpec((tm, d), lambda i: (nt - 1 - i, 0))
    vec = pl.BlockSpec((1, d), lambda i: (0, 0))
    prow = pl.BlockSpec((tm, 3 * d), lambda i: (nt - 1 - i, 0))
    return pl.pallas_call(
        body, name=name, grid=(nt,),
        in_specs=[row, row, pl.BlockSpec((tm, 1), lambda i: (nt - 1 - i, 0)), vec, prow,
                  pl.BlockSpec((8, 3 * d), lambda i: (jnp.maximum((nt - 1 - i) * tb - 1, 0), 0)),
                  pl.BlockSpec((3, d), lambda i: (0, 0)),
                  pl.BlockSpec((d, 3 * d), lambda i: (0, 0)), pl.BlockSpec((d, d), lambda i: (0, 0))],
        out_specs=[row, row, prow, pl.BlockSpec((3, d), lambda i: (0, 0)), vec, vec],
        out_shape=[jax.ShapeDtypeStruct((t, d), F32), jax.ShapeDtypeStruct((t, d), BF16),
                   jax.ShapeDtypeStruct((t, 3 * d), BF16), jax.ShapeDtypeStruct((3, d), F32),
                   jax.ShapeDtypeStruct((1, d), F32), jax.ShapeDtypeStruct((1, d), F32)],
        scratch_shapes=[pltpu.VMEM((8, d), F32)],
        compiler_params=_params(("arbitrary",), VMEM_BIG),
    )(dh, xo, rs, go, p, p, cw, w_in, w_out)


def _kv_fwd(xh, gi, bi, wk, wv, wf, fb, name, carry=()):
    t, d = xh.shape
    tm = _row_tile(t)
    nt = t // tm

    def body(xh_ref, gi_ref, bi_ref, wk_ref, wv_ref, wf_ref, fb_ref,
             k_ref, v_ref, lg_ref, c_ref, ct_ref, run):
        i = pl.program_id(0)

        @pl.when(i == 0)
        def _():
            run[...] = jnp.zeros_like(run)

        x = (xh_ref[...] * gi_ref[...] + bi_ref[...]).astype(BF16)
        k_ref[...] = _dot(x, wk_ref[...]).astype(BF16)
        v_ref[...] = _dot(x, wv_ref[...]).astype(BF16)
        logit = _dot(x, wf_ref[...]) + fb_ref[...]
        lg_ref[...] = logit
        logf = jnp.minimum(logit, 0.0) - jnp.log(1.0 + jnp.exp(-jnp.abs(logit)))
        rows = i * tm + lax.broadcasted_iota(jnp.int32, (tm, 1), 0)
        logf = jnp.where(rows >= PAD, logf, 0.0)
        tri = (lax.broadcasted_iota(jnp.int32, (tm, tm), 0)
               >= lax.broadcasted_iota(jnp.int32, (tm, tm), 1)).astype(F32)
        cs = jnp.dot(tri, logf, precision=lax.Precision.HIGHEST, preferred_element_type=F32) + run[...]
        run[...] = cs[tm - 1:tm]
        c_ref[...] = cs
        ct_ref[...] = cs.T

    row = pl.BlockSpec((tm, d), lambda i: (i, 0))
    vec = pl.BlockSpec((1, d), lambda i: (0, 0))
    gate = pl.BlockSpec((tm, LANES), lambda i: (i, 0))
    sq = pl.BlockSpec((d, d), lambda i: (0, 0))
    first = lambda: pl.program_id(0) == 0
    last = lambda: pl.program_id(0) == nt - 1
    return pl.pallas_call(
        _carried(body, 7, 5, carry, first, last), name=name, grid=(nt,),
        in_specs=[row, vec, vec, sq, sq, pl.BlockSpec((d, LANES), lambda i: (0, 0)),
                  pl.BlockSpec((1, LANES), lambda i: (0, 0))] + [ANY] * len(carry),
        out_specs=[row, row, gate, gate, pl.BlockSpec((LANES, tm), lambda i: (0, i))] + [ANY] * len(carry),
        out_shape=[jax.ShapeDtypeStruct((t, d), BF16), jax.ShapeDtypeStruct((t, d), BF16),
                   jax.ShapeDtypeStruct((t, LANES), F32), jax.ShapeDtypeStruct((t, LANES), F32),
                   jax.ShapeDtypeStruct((LANES, t), F32)] + _carry_shapes(carry),
        scratch_shapes=[pltpu.VMEM((1, LANES), F32)] + _carry_scratch(carry),
        compiler_params=_params(("arbitrary",), VMEM_MID),
    )(xh, gi, bi, wk, wv, wf, fb, *[a for _, a in carry])


def _kv_bwd(dk, dv, dcs, dcq, logit, dh_other, wk, wv, wf, name):
    t, d = dk.shape
    tm = _row_tile(t)
    nt = t // tm

    def body(dk_ref, dv_ref, dcs_ref, dcq_ref, lg_ref, oth_ref, wk_ref, wv_ref, wf_ref,
             dh_ref, dl_ref, dfb_ref, run):
        i = pl.program_id(0)
        tile = nt - 1 - i

        @pl.when(i == 0)
        def _():
            run[...] = jnp.zeros_like(run)
            dfb_ref[...] = jnp.zeros_like(dfb_ref)

        lane = lax.broadcasted_iota(jnp.int32, (tm, LANES), 1)
        dc = dcq_ref[...]
        for hh in range(N_HEADS):
            dc = dc + jnp.where(lane == hh, jnp.sum(dcs_ref[hh], axis=1, keepdims=True), 0.0)
        tri = (lax.broadcasted_iota(jnp.int32, (tm, tm), 0)
               <= lax.broadcasted_iota(jnp.int32, (tm, tm), 1)).astype(F32)
        dlf = jnp.dot(tri, dc, precision=lax.Precision.HIGHEST, preferred_element_type=F32) + run[...]
        run[...] = dlf[0:1]
        rows = tile * tm + lax.broadcasted_iota(jnp.int32, (tm, 1), 0)
        dlogit = jnp.where(rows >= PAD, dlf * jax.nn.sigmoid(-lg_ref[...]), 0.0)
        dfb_ref[...] += jnp.sum(dlogit, axis=0, keepdims=True)
        dlb = dlogit.astype(BF16)
        dl_ref[...] = dlb
        dh_ref[...] = (oth_ref[...] + _dot_nt(dk_ref[...], wk_ref[...])
                       + _dot_nt(dv_ref[...], wv_ref[...]) + _dot_nt(dlb, wf_ref[...]))

    row = pl.BlockSpec((tm, d), lambda i: (nt - 1 - i, 0))
    gate = pl.BlockSpec((tm, LANES), lambda i: (nt - 1 - i, 0))
    sq = pl.BlockSpec((d, d), lambda i: (0, 0))
    return pl.pallas_call(
        body, name=name, grid=(nt,),
        in_specs=[row, row, pl.BlockSpec((N_HEADS, tm, LANES), lambda i: (0, nt - 1 - i, 0)), gate, gate, row,
                  sq, sq, pl.BlockSpec((d, LANES), lambda i: (0, 0))],
        out_specs=[row, gate, pl.BlockSpec((1, LANES), lambda i: (0, 0))],
        out_shape=[jax.ShapeDtypeStruct((t, d), F32), jax.ShapeDtypeStruct((t, LANES), BF16),
                   jax.ShapeDtypeStruct((1, LANES), F32)],
        scratch_shapes=[pltpu.VMEM((1, LANES), F32)],
        compiler_params=_params(("arbitrary",), VMEM_MID),
    )(dk, dv, dcs, dcq, logit, dh_other, wk, wv, wf)


def _proj(xh, gi, bi, w, name):
    t, k = xh.shape
    n = w.shape[1]
    tm = _row_tile(t)

    def body(x_ref, g_ref, b_ref, w_ref, o_ref):
        x = (x_ref[...] * g_ref[...] + b_ref[...]).astype(BF16)
        o_ref[...] = _dot(x, w_ref[...]).astype(BF16)

    vec = pl.BlockSpec((1, k), lambda i: (0, 0))
    return pl.pallas_call(
        body, name=name, grid=(t // tm,),
        in_specs=[pl.BlockSpec((tm, k), lambda i: (i, 0)), vec, vec, pl.BlockSpec((k, n), lambda i: (0, 0))],
        out_specs=pl.BlockSpec((tm, n), lambda i: (i, 0)),
        out_shape=jax.ShapeDtypeStruct((t, n), BF16),
        compiler_params=_params(("arbitrary",), VMEM_MID),
    )(xh, gi, bi, w)


def _add_proj_nt(base, y, w, name):
    t, n = y.shape
    k = w.shape[0]
    tm = _row_tile(t)

    def body(b_ref, y_ref, w_ref, o_ref):
        o_ref[...] = b_ref[...] + _dot_nt(y_ref[...].astype(BF16), w_ref[...])

    return pl.pallas_call(
        body, name=name, grid=(t // tm,),
        in_specs=[pl.BlockSpec((tm, k), lambda i: (i, 0)), pl.BlockSpec((tm, n), lambda i: (i, 0)),
                  pl.BlockSpec((k, n), lambda i: (0, 0))],
        out_specs=pl.BlockSpec((tm, k), lambda i: (i, 0)),
        out_shape=jax.ShapeDtypeStruct((t, k), F32),
        compiler_params=_params(("arbitrary",), VMEM_MID),
    )(base, y, w)


def _attn_out_fwd(ot, xh, gi, bi, w_o, go, bo, alpha, name):
    t, d = xh.shape
    tm = _row_tile(t)

    def body(ot_ref, xh_ref, gi_ref, bi_ref, wo_ref, go_ref, bo_ref, xo_ref, rs_ref, hb_ref):
        h = xh_ref[...] * gi_ref[...] + bi_ref[...]
        xhat, rstd = _ln_fwd(alpha * h + _dot_tn(ot_ref[...], wo_ref[...]))
        xo_ref[...] = xhat
        rs_ref[...] = rstd
        hb_ref[...] = (xhat * go_ref[...] + bo_ref[...]).astype(BF16).T

    row = pl.BlockSpec((tm, d), lambda i: (i, 0))
    col = pl.BlockSpec((d, tm), lambda i: (0, i))
    vec = pl.BlockSpec((1, d), lambda i: (0, 0))
    return pl.pallas_call(
        body, name=name, grid=(t // tm,),
        in_specs=[col, row, vec, vec, pl.BlockSpec((d, d), lambda i: (0, 0)), vec, vec],
        out_specs=[row, pl.BlockSpec((tm, 1), lambda i: (i, 0)), col],
        out_shape=[jax.ShapeDtypeStruct((t, d), F32), jax.ShapeDtypeStruct((t, 1), F32),
                   jax.ShapeDtypeStruct((d, t), BF16)],
        compiler_params=_params(("arbitrary",), VMEM_MID),
    )(ot, xh, gi, bi, w_o, go, bo)


def _attn_out_bwd(dh, xo, rs, go, ot, w_o, alpha, name):
    t, d = dh.shape
    tm = _row_tile(t)
    hd = d // N_HEADS

    def body(dh_ref, xo_ref, rs_ref, go_ref, ot_ref, wo_ref,
             dres_ref, dmix_ref, dot_ref, delta_ref, dgain_ref, dbias_ref):
        @pl.when(pl.program_id(0) == 0)
        def _():
            dgain_ref[...] = jnp.zeros_like(dgain_ref)
            dbias_ref[...] = jnp.zeros_like(dbias_ref)

        dz, dgp, dbp = _ln_bwd(dh_ref[...], xo_ref[...], rs_ref[...], go_ref[...])
        dgain_ref[...] += dgp
        dbias_ref[...] += dbp
        dres_ref[...] = alpha * dz
        dmixb = dz.astype(BF16)
        dmix_ref[...] = dmixb
        dot_t = _dot_nt(wo_ref[...], dmixb)
        dot_ref[...] = dot_t.astype(BF16)
        prod = dot_t * ot_ref[...].astype(F32)
        delta_ref[...] = jnp.sum(prod.reshape(N_HEADS, hd, tm), axis=1)

    row = pl.BlockSpec((tm, d), lambda i: (i, 0))
    vec = pl.BlockSpec((1, d), lambda i: (0, 0))
    col = pl.BlockSpec((d, tm), lambda i: (0, i))
    return pl.pallas_call(
        body, name=name, grid=(t // tm,),
        in_specs=[row, row, pl.BlockSpec((tm, 1), lambda i: (i, 0)), vec, col,
                  pl.BlockSpec((d, d), lambda i: (0, 0))],
        out_specs=[row, row, col, pl.BlockSpec((N_HEADS, tm), lambda i: (0, i)), vec, vec],
        out_shape=[jax.ShapeDtypeStruct((t, d), F32), jax.ShapeDtypeStruct((t, d), BF16),
                   jax.ShapeDtypeStruct((d, t), BF16), jax.ShapeDtypeStruct((N_HEADS, t), F32),
                   jax.ShapeDtypeStruct((1, d), F32), jax.ShapeDtypeStruct((1, d), F32)],
        compiler_params=_params(("arbitrary",), VMEM_MID),
    )(dh, xo, rs, go, ot, w_o)


def _scores_t(k, q, ct_ref, c_ref, h, i, j, tq, tk, scale, masked):
    sub = lax.broadcasted_iota(jnp.int32, (8, tq), 0)
    cq = jnp.sum(jnp.where(sub == h, ct_ref[...], 0.0), axis=0, keepdims=True) * LOG2E
    lane = lax.broadcasted_iota(jnp.int32, (tk, LANES), 1)
    ck = jnp.sum(jnp.where(lane == h, c_ref[...], 0.0), axis=1, keepdims=True) * LOG2E
    st = _dot_nt(k, q) * (scale * LOG2E) - ck
    if masked:
        kpos = j * tk + lax.broadcasted_iota(jnp.int32, (tk, 1), 0)
        qpos = i * tq + lax.broadcasted_iota(jnp.int32, (1, tq), 1)
        st = jnp.where((kpos <= qpos) & (kpos >= PAD), st, NEG_INF)
    return st, cq


def _tri_pairs(n, by_row):
    if by_row:
        pairs = [(i, j) for i in range(n) for j in range(i + 1)]
    else:
        pairs = [(i, j) for j in range(n) for i in range(j, n)]
    return (jnp.asarray([p[0] for p in pairs], jnp.int32), jnp.asarray([p[1] for p in pairs], jnp.int32))


def _attn_fwd(q, k, v, c, ct, name, carry=()):
    t, d = q.shape
    hd = d // N_HEADS
    tq = tk = _row_tile(t)
    nq = t // tq
    scale = 1.0 / math.sqrt(hd)

    hps = ATTN_HEADS_PER_STEP

    def body(it_ref, jt_ref, q_ref, k_ref, v_ref, c_ref, ct_ref, ot_ref, lse_ref, m_s, l_s, acc):
        hp, p_ = pl.program_id(0), pl.program_id(1)
        i, j = it_ref[p_], jt_ref[p_]

        @pl.when(j == 0)
        def _():
            m_s[...] = jnp.full_like(m_s, NEG_INF)
            l_s[...] = jnp.zeros_like(l_s)
            acc[...] = jnp.zeros_like(acc)

        def update(masked):
            scores = []
            for e in range(hps):
                cols = slice(e * hd, (e + 1) * hd)
                scores.append(_scores_t(k_ref[:, cols], q_ref[:, cols], ct_ref, c_ref, hp * hps + e,
                                        i, j, tq, tk, scale, masked))
            probs = []
            for e, (st, cq) in enumerate(scores):
                m_new = jnp.maximum(m_s[e], jnp.max(st, axis=0, keepdims=True) + cq)
                a = jnp.exp2(m_s[e] - m_new)
                p = jnp.exp2(st - (m_new - cq))
                l_s[e] = a * l_s[e] + jnp.sum(p, axis=0, keepdims=True)
                m_s[e] = m_new
                probs.append((a, p.astype(BF16)))
            for e, (a, pb) in enumerate(probs):
                acc[e] = a * acc[e] + _dot_tn(v_ref[:, e * hd:(e + 1) * hd], pb)

        edge = (j == i) | (j == 0)
        pl.when(edge)(lambda: update(True))
        pl.when(jnp.logical_not(edge))(lambda: update(False))

        @pl.when(j == i)
        def _():
            for e in range(hps):
                ot_ref[e * hd:(e + 1) * hd, :] = (acc[e] / l_s[e]).astype(BF16)
                lse_ref[e] = m_s[e] + jnp.log2(l_s[e])

    it, jt = _tri_pairs(nq, by_row=True)
    npairs = it.shape[0]
    nhp = N_HEADS // hps
    kv = pl.BlockSpec((tk, hps * hd), lambda h, p, it, jt: (jt[p], h))
    first = lambda: (pl.program_id(0) == 0) & (pl.program_id(1) == 0)
    last = lambda: (pl.program_id(0) == nhp - 1) & (pl.program_id(1) == npairs - 1)
    return pl.pallas_call(
        _carried(body, 7, 2, carry, first, last), name=name,
        grid_spec=pltpu.PrefetchScalarGridSpec(
            num_scalar_prefetch=2, grid=(nhp, npairs),
            in_specs=[pl.BlockSpec((tq, hps * hd), lambda h, p, it, jt: (it[p], h)), kv, kv,
                      pl.BlockSpec((tk, LANES), lambda h, p, it, jt: (jt[p], 0)),
                      pl.BlockSpec((8, tq), lambda h, p, it, jt: (0, it[p]))] + [ANY] * len(carry),
            out_specs=[pl.BlockSpec((hps * hd, tq), lambda h, p, it, jt: (h, it[p])),
                       pl.BlockSpec((hps, 1, tq), lambda h, p, it, jt: (h, 0, it[p]))] + [ANY] * len(carry),
            scratch_shapes=[pltpu.VMEM((hps, 1, tq), F32), pltpu.VMEM((hps, 1, tq), F32),
                            pltpu.VMEM((hps, hd, tq), F32)] + _carry_scratch(carry)),
        out_shape=[jax.ShapeDtypeStruct((d, t), BF16), jax.ShapeDtypeStruct((N_HEADS, 1, t), F32)]
                  + _carry_shapes(carry),
        compiler_params=_params(("arbitrary", "arbitrary"), VMEM_MID),
    )(it, jt, q, k, v, c, ct, *[a for _, a in carry])


def _attn_bwd(q, k, v, c, ct, lse, delta, dot_t, name, carry=()):
    t, d = q.shape
    hd = d // N_HEADS
    tq = tk = _row_tile(t)
    nq = t // tq
    scale = 1.0 / math.sqrt(hd)
    hps = ATTN_BWD_HEADS_PER_STEP

    def body(it_ref, jt_ref, q_ref, k_ref, v_ref, c_ref, ct_ref, lse_ref, delta_ref, dot_ref,
             dq_ref, dk_ref, dv_ref, dcs_ref, drow_ref, dk_acc, dv_acc, dc_acc):
        hp, p_ = pl.program_id(0), pl.program_id(1)
        i, j = it_ref[p_], jt_ref[p_]

        @pl.when(p_ == 0)
        def _():
            dq_ref[...] = jnp.zeros_like(dq_ref)
            drow_ref[...] = jnp.zeros_like(drow_ref)

        @pl.when(i == j)
        def _():
            dk_acc[...] = jnp.zeros_like(dk_acc)
            dv_acc[...] = jnp.zeros_like(dv_acc)
            dc_acc[...] = jnp.zeros_like(dc_acc)

        def update(masked):
            sub = lax.broadcasted_iota(jnp.int32, (8, tq), 0)
            rows = pl.ds(pl.multiple_of(i * tq, tq), tq)
            stage = []
            for e in range(hps):
                cols = slice(e * hd, (e + 1) * hd)
                st, cq = _scores_t(k_ref[:, cols], q_ref[:, cols], ct_ref, c_ref, hp * hps + e,
                                   i, j, tq, tk, scale, masked)
                dp = _dot(v_ref[:, cols], dot_ref[cols, :])
                stage.append((st, cq, dp))
            grads = []
            for e, (st, cq, dp) in enumerate(stage):
                p = jnp.exp2(st - (lse_ref[e] - cq))
                dl = jnp.sum(jnp.where(sub == hp * hps + e, delta_ref[...], 0.0), axis=0, keepdims=True)
                ds = p * (dp - dl)
                part = ds[:, 0:LANES]
                for g in range(1, tq // LANES):
                    part = part + ds[:, g * LANES:(g + 1) * LANES]
                dc_acc[e] += part
                drow_ref[e, i] += jnp.broadcast_to(jnp.sum(ds, axis=0, keepdims=True), (8, tq))
                grads.append((p.astype(BF16), ds.astype(BF16)))
            for e, (pb, dsb) in enumerate(grads):
                cols = slice(e * hd, (e + 1) * hd)
                dv_acc[e] += _dot_nt(pb, dot_ref[cols, :])
                dk_acc[e] += _dot(dsb, q_ref[:, cols]) * scale
                dq_ref[rows, cols] += _dot_tn(dsb, k_ref[:, cols]) * scale

        edge = (j == i) | (j == 0)
        pl.when(edge)(lambda: update(True))
        pl.when(jnp.logical_not(edge))(lambda: update(False))

        @pl.when(i == nq - 1)
        def _():
            for e in range(hps):
                cols = slice(e * hd, (e + 1) * hd)
                dk_ref[:, cols] = dk_acc[e].astype(BF16)
                dv_ref[:, cols] = dv_acc[e].astype(BF16)
                dcs_ref[e] = -dc_acc[e]

    it, jt = _tri_pairs(nq, by_row=False)
    npairs = it.shape[0]
    nhp = N_HEADS // hps
    kv = pl.BlockSpec((tk, hps * hd), lambda h, p, it, jt: (jt[p], h))
    first = lambda: (pl.program_id(0) == 0) & (pl.program_id(1) == 0)
    last = lambda: (pl.program_id(0) == nhp - 1) & (pl.program_id(1) == npairs - 1)
    return pl.pallas_call(
        _carried(body, 10, 5, carry, first, last), name=name,
        grid_spec=pltpu.PrefetchScalarGridSpec(
            num_scalar_prefetch=2, grid=(nhp, npairs),
            in_specs=[pl.BlockSpec((tq, hps * hd), lambda h, p, it, jt: (it[p], h)), kv, kv,
                      pl.BlockSpec((tk, LANES), lambda h, p, it, jt: (jt[p], 0)),
                      pl.BlockSpec((8, tq), lambda h, p, it, jt: (0, it[p])),
                      pl.BlockSpec((hps, 1, tq), lambda h, p, it, jt: (h, 0, it[p])),
                      pl.BlockSpec((N_HEADS, tq), lambda h, p, it, jt: (0, it[p])),
                      pl.BlockSpec((hps * hd, tq), lambda h, p, it, jt: (h, it[p]))] + [ANY] * len(carry),
            out_specs=[pl.BlockSpec((t, hps * hd), lambda h, p, it, jt: (0, h)), kv, kv,
                       pl.BlockSpec((hps, tk, LANES), lambda h, p, it, jt: (h, jt[p], 0)),
                       pl.BlockSpec((hps, nq, 8, tq), lambda h, p, it, jt: (h, 0, 0, 0))] + [ANY] * len(carry),
            scratch_shapes=[pltpu.VMEM((hps, tk, hd), F32), pltpu.VMEM((hps, tk, hd), F32),
                            pltpu.VMEM((hps, tk, LANES), F32)] + _carry_scratch(carry)),
        out_shape=[jax.ShapeDtypeStruct((t, d), F32), jax.ShapeDtypeStruct((t, d), BF16),
                   jax.ShapeDtypeStruct((t, d), BF16), jax.ShapeDtypeStruct((N_HEADS, t, LANES), F32),
                   jax.ShapeDtypeStruct((N_HEADS, nq, 8, tq), F32)] + _carry_shapes(carry),
        compiler_params=_params(("arbitrary", "arbitrary"), VMEM_BIG),
    )(it, jt, q, k, v, c, ct, lse, delta, dot_t, *[a for _, a in carry])


def _loss_head(xh, g, b, target, name):
    t, d = xh.shape
    tm = LOSS_TILE
    nt = t // tm
    lead = ROW0 // tm

    def body(xh_ref, g_ref, b_ref, tg_ref, dh_ref, loss_ref, part):
        i = pl.program_id(0)

        @pl.when(i == 0)
        def _():
            part[...] = jnp.zeros_like(part)

        @pl.when(i < lead)
        def _():
            dh_ref[...] = jnp.zeros_like(dh_ref)

        @pl.when(i >= lead)
        def _():
            e = xh_ref[...] * g_ref[...] + b_ref[...] - tg_ref[...]
            dh_ref[...] = e * (1.0 / d)
            part[...] += jnp.sum(e * e, axis=0, keepdims=True)

        @pl.when(i == nt - 1)
        def _():
            loss_ref[...] = jnp.full((1, LANES), 0.5 / d, F32) * jnp.sum(part[...])

    return pl.pallas_call(
        body, name=name, grid=(nt,),
        in_specs=[pl.BlockSpec((tm, d), lambda i: (i, 0)), pl.BlockSpec((1, d), lambda i: (0, 0)),
                  pl.BlockSpec((1, d), lambda i: (0, 0)),
                  pl.BlockSpec((tm, d), lambda i: (jnp.maximum(i - lead, 0), 0))],
        out_specs=[pl.BlockSpec((tm, d), lambda i: (i, 0)), pl.BlockSpec((1, LANES), lambda i: (0, 0))],
        out_shape=[jax.ShapeDtypeStruct((t, d), F32), jax.ShapeDtypeStruct((1, LANES), F32)],
        scratch_shapes=[pltpu.VMEM((1, d), F32)],
        compiler_params=_params(("arbitrary",), VMEM_MID),
    )(xh, g, b, target)


def _adamw(w, g, m, v, name):
    r, c = w.shape
    tr = r
    for cand in (256, 128, 64, 32, 16, 8):
        if r % cand == 0 and r > cand:
            tr = cand
            break
    bc1 = 1.0 - ADAM_B1 ** ADAM_STEP
    bc2 = 1.0 - ADAM_B2 ** ADAM_STEP

    def body(w_ref, g_ref, m_ref, v_ref, d_ref, nm_ref, nv_ref):
        gg = g_ref[...]
        nm = ADAM_B1 * m_ref[...] + (1.0 - ADAM_B1) * gg
        nv = ADAM_B2 * v_ref[...] + (1.0 - ADAM_B2) * (gg * gg)
        d_ref[...] = -ADAM_LR * ((nm / bc1) / (jnp.sqrt(nv / bc2) + ADAM_EPS) + ADAM_WD * w_ref[...])
        nm_ref[...] = nm
        nv_ref[...] = nv

    blk = pl.BlockSpec((tr, c), lambda i: (i, 0))
    shp = jax.ShapeDtypeStruct((r, c), F32)
    return pl.pallas_call(
        body, name=name, grid=(r // tr,), in_specs=[blk] * 4, out_specs=[blk] * 3,
        out_shape=[shp] * 3, compiler_params=_params(("arbitrary",), VMEM_MID),
    )(w, g, m, v)


def _reduce_adamw(w, m, v, landed, name):
    nl, r, c = w.shape
    tr = next(cand for cand in range(min(r, ADAM_ROWS_MAX), 0, -BF16_ROWS) if r % cand == 0)
    nr = r // tr
    bc1 = 1.0 - ADAM_B1 ** ADAM_STEP
    bc2 = 1.0 - ADAM_B2 ** ADAM_STEP

    def body(*refs):
        w_ref, m_ref, v_ref = refs[:3]
        src_refs = refs[3:3 + nl]
        g_ref, d_ref, nm_ref, nv_ref = refs[3 + nl:]

        def update(src):
            gg = src[0].astype(F32)
            for s in range(1, N_DEV):
                gg = gg + src[s].astype(F32)
            nm = ADAM_B1 * m_ref[0] + (1.0 - ADAM_B1) * gg
            nv = ADAM_B2 * v_ref[0] + (1.0 - ADAM_B2) * (gg * gg)
            g_ref[0] = gg
            d_ref[0] = -ADAM_LR * ((nm / bc1) / (jnp.sqrt(nv / bc2) + ADAM_EPS) + ADAM_WD * w_ref[0])
            nm_ref[0] = nm
            nv_ref[0] = nv

        for idx in range(nl):
            pl.when(pl.program_id(0) == idx)(functools.partial(update, src_refs[idx]))

    def src_spec(idx):
        return pl.BlockSpec((N_DEV, tr, c),
                            lambda l, i: (0, jnp.where(l == idx, i, jnp.where(l < idx, 0, nr - 1)), 0))

    blk = pl.BlockSpec((1, tr, c), lambda l, i: (l, i, 0))
    shp = jax.ShapeDtypeStruct((nl, r, c), F32)
    return pl.pallas_call(
        body, name=name, grid=(nl, nr), in_specs=[blk] * 3 + [src_spec(idx) for idx in range(nl)],
        out_specs=[blk] * 4, out_shape=[shp] * 4,
        compiler_params=_params(("arbitrary", "arbitrary"), VMEM_MID),
    )(w, m, v, *landed)


def _sum_sources(r, name):
    n, rows, c = r.shape
    tr = next(cand for cand in range(min(rows, SUM_ROWS_MAX), 0, -BF16_ROWS) if rows % cand == 0)

    def body(r_ref, o_ref):
        acc = r_ref[0].astype(F32)
        for s in range(1, n):
            acc = acc + r_ref[s].astype(F32)
        o_ref[...] = acc

    return pl.pallas_call(
        body, name=name, grid=(rows // tr,),
        in_specs=[pl.BlockSpec((n, tr, c), lambda i: (0, i, 0))],
        out_specs=pl.BlockSpec((tr, c), lambda i: (i, 0)),
        out_shape=jax.ShapeDtypeStruct((rows, c), F32),
        compiler_params=_params(("arbitrary",), VMEM_MID),
    )(r)


def _all_gather(parts, name):
    n = len(parts)

    def body(*refs):
        x_refs, out_refs = refs[:n], refs[n:2 * n]
        send_sems, recv_sems, local_sems = refs[2 * n:]
        mx, my, mc = lax.axis_index("x"), lax.axis_index("y"), lax.axis_index("c")
        me, sibling = (mx, my, mc), (mx, my, 1 - mc)
        chips = [(1 - mx, my), (mx, 1 - my), (1 - mx, 1 - my)]

        def copy(p, k, block, to, from_input=False):
            px, py, pc = block
            rows = out_refs[p].at[4 * px + 2 * py + pc]
            return pltpu.make_async_remote_copy(
                src_ref=x_refs[p] if from_input else rows, dst_ref=rows,
                send_sem=send_sems.at[7 * p + k], recv_sem=recv_sems.at[7 * p + k],
                device_id=to, device_id_type=MESH)

        mine, sent = [], []
        for p in range(n):
            own = pltpu.make_async_copy(x_refs[p], out_refs[p].at[4 * mx + 2 * my + mc], local_sems.at[p])
            own.start()
            mine.append(own)
            first = [copy(p, 0, me, sibling, True)]
            first += [copy(p, 1 + j, me, (*chip, mc), True) for j, chip in enumerate(chips)]
            for cp in first:
                cp.start()
            sent += first
        for p in range(n):
            for j, chip in enumerate(chips):
                copy(p, 1 + j, (*chip, mc), me).wait_recv()
                fwd = copy(p, 4 + j, (*chip, mc), sibling)
                fwd.start()
                sent.append(fwd)
        for p in range(n):
            copy(p, 0, sibling, me).wait_recv()
            for j, chip in enumerate(chips):
                copy(p, 4 + j, (*chip, 1 - mc), me).wait_recv()
        for cp in sent:
            cp.wait_send()
        for own in mine:
            own.wait()

    return pl.pallas_call(
        body, name=name, in_specs=[ANY] * n, out_specs=[ANY] * n,
        out_shape=[jax.ShapeDtypeStruct((N_DEV,) + a.shape, a.dtype) for a in parts],
        scratch_shapes=[pltpu.SemaphoreType.DMA((7 * n,)), pltpu.SemaphoreType.DMA((7 * n,)),
                        pltpu.SemaphoreType.DMA((n,))],
    )(*parts)


def _pack_rows(parts, width, mult, lead=0):
    out = []
    for a in parts:
        head = a.shape[:lead]
        flat = a.reshape(head + (-1,))
        padn = (-flat.shape[-1]) % (width * mult)
        if padn:
            flat = jnp.pad(flat, [(0, 0)] * lead + [(0, padn)])
        out.append(flat.reshape(head + (-1, width)))
    return jnp.concatenate(out, axis=lead)


def _rows_of(shape, width, mult):
    n = math.prod(shape)
    per = width * mult
    return ((n + per - 1) // per) * mult


def _unpack_rows(buf, shapes, width, mult):
    lead = buf.shape[:-2]
    out, off = [], 0
    for shp in shapes:
        r = _rows_of(shp, width, mult)
        flat = buf[..., off:off + r, :].reshape(lead + (r * width,))
        out.append(flat[..., :math.prod(shp)].reshape(lead + tuple(shp)))
        off += r
    return out


def _cols_from_devices(g):
    nd = g.ndim
    perm = tuple(range(1, nd - 1)) + (0, nd - 1)
    t = jnp.transpose(g, perm)
    return t.reshape(t.shape[:-2] + (t.shape[-2] * t.shape[-1],))


def _cols_to_devices(a):
    c = a.shape[-1] // N_DEV
    t = a.reshape(a.shape[:-1] + (N_DEV, c))
    nd = t.ndim
    perm = (nd - 2,) + tuple(range(0, nd - 2)) + (nd - 1,)
    return jnp.transpose(t, perm)


WIDTH = 1024


def kernel(x, meta, ffn1_wg, ffn1_wu, ffn1_wd, ffn2_wg, ffn2_wu, ffn2_wd, ln_gain, ln_bias, conv_w_in, conv_w, conv_w_out, kv_w, f_bias, attn_w_q, attn_w_o, loss_target, m_meta, m_ffn1_wg, m_ffn1_wu, m_ffn1_wd, m_ffn2_wg, m_ffn2_wu, m_ffn2_wd, m_ln_gain, m_ln_bias, m_conv_w_in, m_conv_w, m_conv_w_out, m_kv_w, m_f_bias, m_attn_w_q, m_attn_w_o, v_meta, v_ffn1_wg, v_ffn1_wu, v_ffn1_wd, v_ffn2_wg, v_ffn2_wu, v_ffn2_wd, v_ln_gain, v_ln_bias, v_conv_w_in, v_conv_w, v_conv_w_out, v_kv_w, v_f_bias, v_attn_w_q, v_attn_w_o):
    depth = ln_gain.shape[0]
    alpha = float((2 * depth) ** 0.25)
    d = x.shape[-1]
    seq = x.shape[1]
    t = ROW0 + seq
    fsh = ffn1_wg.shape[-1]
    f = fsh * N_DEV
    fck = MXU_COLS
    nc = f // fck
    me = 4 * lax.axis_index("x") + 2 * lax.axis_index("y") + lax.axis_index("c")

    def gather_of(parts):
        return [(True, a.astype(BF16)) for a in parts]

    small = [meta, ln_gain, ln_bias, conv_w]
    small_shapes = [a.shape for a in small]
    g1g, g1u, g1d, gcin, gcout, gsmall = _all_gather(
        [a.astype(BF16) for a in (ffn1_wg[0], ffn1_wu[0], ffn1_wd[0], conv_w_in[0], conv_w_out[0])]
        + [_pack_rows(small, WIDTH, F32_ROWS)], "ag_first")
    gmeta, ggain, gbias, gcw = _unpack_rows(gsmall, small_shapes, WIDTH, F32_ROWS)

    def ffn_chunks(gg, gu, gd):
        up = lambda g: jnp.transpose(_cols_from_devices(g).reshape(d, nc, fck), (1, 0, 2))
        return up(gg), up(gu), gd.reshape(nc, fck, d)

    w_in = _cols_from_devices(gcin)
    w_out = gcout.reshape(d, d)
    fb = jnp.pad(f_bias, (0, LANES - N_HEADS)).reshape(1, LANES)
    meta_f = _cols_from_devices(gmeta)
    gain_f = _cols_from_devices(ggain)
    bias_f = _cols_from_devices(gbias)
    cw_f = _cols_from_devices(gcw)[0]

    def gb(l, n):
        return gain_f[l, n].reshape(1, d), bias_f[l, n].reshape(1, d)

    ones = jnp.ones((1, d), F32)
    zeros = jnp.zeros((1, d), F32)

    h0 = jnp.concatenate([jnp.zeros((PAD, d), F32), meta_f, x[0]], axis=0)
    hb0 = _cast_t(h0, "h0_bf16_t")

    w1 = ffn_chunks(g1g, g1u, g1d)
    g00, b00 = gb(0, 0)
    xh1, rs1, hb1, gg1, uu1, g2g, g2u = _ffn_fwd(
        h0, ones, zeros, *w1, g00, b00, alpha, "ffn_fwd_0a", carry=gather_of([ffn2_wg[0], ffn2_wu[0]]))
    g01, b01 = gb(0, 1)
    xh2, rs2, hb2, pp, mb, g2d = _conv_fwd(
        xh1, g00, b00, w_in, cw_f, w_out, g01, b01, alpha, "conv_fwd", carry=gather_of([ffn2_wd[0]]))
    w2 = ffn_chunks(g2g, g2u, g2d)
    g02, b02 = gb(0, 2)
    xh3, rs3, hb3, gg3, uu3, gkv, g3g, g3u = _ffn_fwd(
        xh2, g01, b01, *w2, g02, b02, alpha, "ffn_fwd_0b", carry=gather_of([kv_w, ffn1_wg[1], ffn1_wu[1]]))
    kvw = _cols_from_devices(gkv)
    wk, wv = kvw[:, :d], kvw[:, d:2 * d]
    wf = jnp.pad(kvw[:, 2 * d:], ((0, 0), (0, LANES - N_HEADS)))
    kk, vv, logit, cc, cct, g3d = _kv_fwd(xh3, g02, b02, wk, wv, wf, fb, "kv_fwd",
                                          carry=gather_of([ffn1_wd[1]]))

    w3 = ffn_chunks(g3g, g3u, g3d)
    g10, b10 = gb(1, 0)
    xh4, rs4, hb4, gg4, uu4, gwq = _ffn_fwd(xh3, g02, b02, *w3, g10, b10, alpha, "ffn_fwd_1a",
                                             carry=gather_of([attn_w_q[0]]))
    w_q = gwq.reshape(d, d)
    qq = _proj(xh4, g10, b10, w_q, "q_proj")
    ot, lse, gwo, g4g, g4u, g4d = _attn_fwd(
        qq, kk, vv, cc, cct, "attn_fwd", carry=gather_of([attn_w_o[0], ffn2_wg[1], ffn2_wu[1], ffn2_wd[1]]))
    w_o = gwo.reshape(d, d)
    g11, b11 = gb(1, 1)
    xh5, rs5, hb5 = _attn_out_fwd(ot, xh4, g10, b10, w_o, g11, b11, alpha, "attn_out_fwd")
    w4 = ffn_chunks(g4g, g4u, g4d)
    g12, b12 = gb(1, 2)
    xh6, rs6, _, gg6, uu6 = _ffn_fwd(xh5, g11, b11, *w4, g12, b12, alpha, "ffn_fwd_1b")

    dh6, loss_l = _loss_head(xh6, g12, b12, loss_target[0], "loss_head")
    loss = lax.psum(loss_l[0, 0], ("x", "y", "c"))

    dgain = [[None] * 3 for _ in range(depth)]
    dbias = [[None] * 3 for _ in range(depth)]

    def to_col_owners(g):
        return (False, _cols_to_devices(g).astype(BF16))

    def to_row_owners(g):
        return (False, g.reshape(N_DEV, g.shape[0] // N_DEV, g.shape[1]).astype(BF16))

    dh5, do6, dg6, du6, a6, dgain[1][2], dbias[1][2] = _ffn_bwd(dh6, xh6, rs6, g12, gg6, uu6, *w4, alpha, "ffn_bwd_1b")
    dw4g, dw4u = _wgrad(hb5, [dg6, du6], "wgrad_up_1b")
    (dw4dt,) = _wgrad(do6, [a6], "wgrad_down_1b")

    dres4, dmix5, dot_t, delta, dgain[1][1], dbias[1][1] = _attn_out_bwd(dh5, xh5, rs5, g11, ot, w_o, alpha, "attn_out_bwd")
    (dwo,) = _wgrad(ot, [dmix5], "wgrad_wo")
    dq, dkk, dvv, dcs, drow, l4g, l4u, l4d, lwo = _attn_bwd(
        qq, kk, vv, cc, cct, lse, delta, dot_t, "attn_bwd",
        carry=[to_col_owners(dw4g), to_col_owners(dw4u), to_row_owners(dw4dt.T), to_row_owners(dwo)])
    dh4 = _add_proj_nt(dres4, dq, w_q, "q_bwd")
    (dwq,) = _wgrad(hb4, [dq], "wgrad_wq")

    dh3a, do4, dg4, du4, a4, dgain[1][0], dbias[1][0] = _ffn_bwd(dh4, xh4, rs4, g10, gg4, uu4, *w3, alpha, "ffn_bwd_1a")
    dw3g, dw3u = _wgrad(hb3, [dg4, du4], "wgrad_up_1a")
    (dw3dt,) = _wgrad(do4, [a4], "wgrad_down_1a")

    dcq = jnp.pad(drow[:, :, 0, :].reshape(N_HEADS, t).T, ((0, 0), (0, LANES - N_HEADS)))
    dh3, dlogit, dfb = _kv_bwd(dkk, dvv, dcs, dcq, logit, dh3a, wk, wv, wf, "kv_bwd")
    dwk, dwv = _wgrad(hb3, [dkk, dvv], "wgrad_kv")
    (dwf,) = _wgrad(hb3, [dlogit], "wgrad_f")
    dkv = jnp.concatenate([dwk, dwv, dwf[:, :N_HEADS]], axis=1)

    dh2, do3, dg3, du3, a3, dgain[0][2], dbias[0][2], lwq, l3g, l3u, l3d, lkv = _ffn_bwd(
        dh3, xh3, rs3, g02, gg3, uu3, *w2, alpha, "ffn_bwd_0b",
        carry=[to_row_owners(dwq), to_col_owners(dw3g), to_col_owners(dw3u), to_row_owners(dw3dt.T),
               to_col_owners(dkv)])
    dw2g, dw2u = _wgrad(hb2, [dg3, du3], "wgrad_up_0b")
    (dw2dt,) = _wgrad(do3, [a3], "wgrad_down_0b")

    dh1, dmix2, dpp, dcw, dgain[0][1], dbias[0][1] = _conv_bwd(dh2, xh2, rs2, g01, pp, cw_f, w_in, w_out, alpha, "conv_bwd")
    (dwin,) = _wgrad(hb1, [dpp], "wgrad_conv_in")
    (dwout,) = _wgrad(mb, [dmix2], "wgrad_conv_out")

    dh0, do1, dg1, du1, a1, dgain[0][0], dbias[0][0], l2g, l2u, l2d, lcin, lcout = _ffn_bwd(
        dh1, xh1, rs1, g00, gg1, uu1, *w1, alpha, "ffn_bwd_0a",
        carry=[to_col_owners(dw2g), to_col_owners(dw2u), to_row_owners(dw2dt.T), to_col_owners(dwin),
               to_row_owners(dwout)])
    (dw1dt,) = _wgrad(do1, [a1], "wgrad_down_0a")
    dw1g, l1d = _wgrad(hb0, [dg1], "wgrad_upg_0a", carry=[to_row_owners(dw1dt.T)])
    dw1u, l1g = _wgrad(hb0, [du1], "wgrad_upu_0a", carry=[to_col_owners(dw1g)])
    dmeta = dh0[PAD:ROW0]
    dgain_f = jnp.stack([jnp.concatenate(r, axis=0) for r in dgain])
    dbias_f = jnp.stack([jnp.concatenate(r, axis=0) for r in dbias])
    small_full = [dmeta, dgain_f, dbias_f, dcw[None], dfb]
    small_full_shapes = [a.shape for a in small_full]
    l1u, gsmall_grads = _exchange([to_col_owners(dw1u), (True, _pack_rows(small_full, WIDTH, F32_ROWS))], "rs_last")

    grad_x = dh0[ROW0:].reshape(1, seq, d)
    rsmall = _sum_sources(gsmall_grads, "small_sum")
    smeta, sgain, sbias, scw, sfb = _unpack_rows(rsmall, small_full_shapes, WIDTH, F32_ROWS)
    csh = d // N_DEV

    def my_cols(a):
        return lax.dynamic_slice_in_dim(a, me * csh, csh, axis=a.ndim - 1)

    grads = {"meta": my_cols(smeta), "ln_gain": my_cols(sgain), "ln_bias": my_cols(sbias),
             "conv_w": my_cols(scw), "f_bias": sfb[0, :N_HEADS]}
    landed = {"ffn1_wg": [l1g, l3g], "ffn1_wu": [l1u, l3u], "ffn1_wd": [l1d, l3d],
              "ffn2_wg": [l2g, l4g], "ffn2_wu": [l2u, l4u], "ffn2_wd": [l2d, l4d],
              "conv_w_in": [lcin], "conv_w_out": [lcout], "kv_w": [lkv], "attn_w_q": [lwq], "attn_w_o": [lwo]}
    weights = dict(meta=meta, ffn1_wg=ffn1_wg, ffn1_wu=ffn1_wu, ffn1_wd=ffn1_wd, ffn2_wg=ffn2_wg,
                   ffn2_wu=ffn2_wu, ffn2_wd=ffn2_wd, ln_gain=ln_gain, ln_bias=ln_bias,
                   conv_w_in=conv_w_in, conv_w=conv_w, conv_w_out=conv_w_out, kv_w=kv_w,
                   f_bias=f_bias, attn_w_q=attn_w_q, attn_w_o=attn_w_o)
    moms = dict(meta=(m_meta, v_meta), ffn1_wg=(m_ffn1_wg, v_ffn1_wg), ffn1_wu=(m_ffn1_wu, v_ffn1_wu),
                ffn1_wd=(m_ffn1_wd, v_ffn1_wd), ffn2_wg=(m_ffn2_wg, v_ffn2_wg), ffn2_wu=(m_ffn2_wu, v_ffn2_wu),
                ffn2_wd=(m_ffn2_wd, v_ffn2_wd), ln_gain=(m_ln_gain, v_ln_gain), ln_bias=(m_ln_bias, v_ln_bias),
                conv_w_in=(m_conv_w_in, v_conv_w_in), conv_w=(m_conv_w, v_conv_w),
                conv_w_out=(m_conv_w_out, v_conv_w_out), kv_w=(m_kv_w, v_kv_w), f_bias=(m_f_bias, v_f_bias),
                attn_w_q=(m_attn_w_q, v_attn_w_q), attn_w_o=(m_attn_w_o, v_attn_w_o))

    names = list(weights)
    g_out, d_out, m_out, v_out = [], [], [], []
    for n in names:
        w = weights[n]
        shp = w.shape
        mm, vv_ = moms[n]
        if n in landed:
            three = (len(landed[n]),) + shp[-2:]
            g, dl, nm, nv = _reduce_adamw(w.reshape(three), mm.reshape(three), vv_.reshape(three),
                                          landed[n], "adamw_" + n)
            g = g.reshape(shp)
        else:
            two = (1, shp[0]) if w.ndim == 1 else (math.prod(shp[:-1]), shp[-1])
            g = grads[n].reshape(shp)
            dl, nm, nv = _adamw(w.reshape(two), g.reshape(two), mm.reshape(two), vv_.reshape(two), "adamw_" + n)
        g_out.append(g)
        d_out.append(dl.reshape(shp))
        m_out.append(nm.reshape(shp))
        v_out.append(nv.reshape(shp))
    return (loss, grad_x, *g_out, *d_out, *m_out, *v_out)
```

```python
import functools
import math

import jax
import jax.numpy as jnp
from jax import lax
from jax.experimental import pallas as pl
from jax.experimental.pallas import tpu as pltpu

F32 = jnp.float32
BF16 = jnp.bfloat16

N_DEV = 8
N_HEADS = 8
N_META = 16
PAD = 112
ROW0 = PAD + N_META
LN_EPS = 1e-5
NEG_INF = -1e30
LOG2E = 1.4426950408889634
ATTN_HEADS_PER_STEP = 8
ATTN_BWD_HEADS_PER_STEP = 2
LANES = 128
MXU_COLS = 256
FFN_FWD_CHUNKS = 11
FFN_BWD_CHUNKS = 4

ADAM_LR = 0.001
ADAM_B1 = 0.9
ADAM_B2 = 0.999
ADAM_EPS = 1e-08
ADAM_WD = 0.01
ADAM_STEP = 10

ROW_TILES = (640, 128)
LOSS_TILE = 128
BF16_ROWS = 16
F32_ROWS = 8
SUM_ROWS_MAX = 768
ADAM_ROWS_MAX = 256
VMEM_BIG = 56 << 20
VMEM_MID = 40 << 20

ANY = pl.BlockSpec(memory_space=pl.ANY)
MESH = pl.DeviceIdType.MESH


def _row_tile(t):
    for c in ROW_TILES:
        if t % c == 0:
            return c
    raise ValueError(f"no row tile for {t}")


def _dot(a, b):
    return jnp.dot(a, b, preferred_element_type=F32)


def _dot_nt(a, b):
    return lax.dot_general(a, b, (((1,), (1,)), ((), ())), preferred_element_type=F32)


def _dot_tn(a, b):
    return lax.dot_general(a, b, (((0,), (0,)), ((), ())), preferred_element_type=F32)


def _params(sem, vmem):
    return pltpu.CompilerParams(dimension_semantics=sem, vmem_limit_bytes=vmem)


def _ln_fwd(z):
    mu = jnp.mean(z, axis=-1, keepdims=True)
    zc = z - mu
    var = jnp.mean(zc * zc, axis=-1, keepdims=True)
    rstd = lax.rsqrt(var + LN_EPS)
    return zc * rstd, rstd


def _ln_bwd(dh, xhat, rstd, gain):
    dxh = dh * gain
    m1 = jnp.mean(dxh, axis=-1, keepdims=True)
    m2 = jnp.mean(dxh * xhat, axis=-1, keepdims=True)
    dz = rstd * (dxh - m1 - xhat * m2)
    return dz, jnp.sum(dh * xhat, axis=0, keepdims=True), jnp.sum(dh, axis=0, keepdims=True)


def _load_resident(pairs, sems):
    cps = [pltpu.make_async_copy(src, dst, sems.at[k]) for k, (src, dst) in enumerate(pairs)]
    for cp in cps:
        cp.start()
    for cp in cps:
        cp.wait()


def _peer_ids():
    mx, my, mc = lax.axis_index("x"), lax.axis_index("y"), lax.axis_index("c")
    peers = []
    for kk in range(1, N_DEV):
        px = 1 - mx if (kk >> 2) & 1 else mx
        py = 1 - my if (kk >> 1) & 1 else my
        pc = 1 - mc if kk & 1 else mc
        peers.append(((px, py, pc), 4 * px + 2 * py + pc))
    return 4 * mx + 2 * my + mc, peers


def _exchange_copies(jobs, send_sems, recv_sems, local_sems, starting):
    me_id, peers = _peer_ids()
    for n, (gather, src, dst) in enumerate(jobs):
        own = pltpu.make_async_copy(src if gather else src.at[me_id], dst.at[me_id], local_sems.at[n])
        own.start() if starting else own.wait()
        for k, (dev, pid) in enumerate(peers):
            sem = (N_DEV - 1) * n + k
            out = src if gather else src.at[pid]
            send = pltpu.make_async_remote_copy(
                src_ref=out, dst_ref=dst.at[me_id], send_sem=send_sems.at[sem], recv_sem=recv_sems.at[sem],
                device_id=dev, device_id_type=MESH)
            if starting:
                send.start()
            else:
                pltpu.make_async_remote_copy(
                    src_ref=out, dst_ref=dst.at[pid], send_sem=send_sems.at[sem], recv_sem=recv_sems.at[sem],
                    device_id=dev, device_id_type=MESH).wait_recv()
                send.wait_send()


def _carried(body, n_in, n_out, carry, first, last):
    nj = len(carry)
    if nj == 0:
        return body

    def wrapped(*refs):
        ins, srcs = refs[:n_in], refs[n_in:n_in + nj]
        outs = refs[n_in + nj:n_in + nj + n_out]
        dsts = refs[n_in + nj + n_out:n_in + 2 * nj + n_out]
        scratch, sems = refs[n_in + 2 * nj + n_out:-3], refs[-3:]
        jobs = [(g, s, r) for (g, _), s, r in zip(carry, srcs, dsts)]

        @pl.when(first())
        def _():
            _exchange_copies(jobs, *sems, starting=True)

        body(*ins, *outs, *scratch)

        @pl.when(last())
        def _():
            _exchange_copies(jobs, *sems, starting=False)

    return wrapped


def _carry_shapes(carry):
    return [jax.ShapeDtypeStruct((N_DEV,) + a.shape if g else a.shape, a.dtype) for g, a in carry]


def _carry_scratch(carry):
    if not carry:
        return []
    n = len(carry)
    return [pltpu.SemaphoreType.DMA(((N_DEV - 1) * n,)), pltpu.SemaphoreType.DMA(((N_DEV - 1) * n,)),
            pltpu.SemaphoreType.DMA((n,))]


def _exchange(carry, name):
    n = len(carry)

    def body(*refs):
        jobs = [(g, s, r) for (g, _), s, r in zip(carry, refs[:n], refs[n:2 * n])]
        _exchange_copies(jobs, *refs[2 * n:], starting=True)
        _exchange_copies(jobs, *refs[2 * n:], starting=False)

    return pl.pallas_call(
        body, name=name, in_specs=[ANY] * n, out_specs=[ANY] * n, out_shape=_carry_shapes(carry),
        scratch_shapes=_carry_scratch(carry),
    )(*[a for _, a in carry])


def _ffn_fwd(xh, gi, bi, wg, wu, wd, go, bo, alpha, name, carry=()):
    t, d = xh.shape
    nch, _, fc = wg.shape
    f = nch * fc
    per = min(FFN_FWD_CHUNKS, nch)
    nc = -(-nch // per)
    tm = _row_tile(t)
    nt = t // tm

    def body(xh_ref, gi_ref, bi_ref, wg_hbm, wu_hbm, wd_hbm, go_ref, bo_ref,
             xo_ref, rs_ref, hb_ref, g_ref, u_ref,
             wg_v, wu_v, wd_v, acc, hbs, sems):
        i = pl.program_id(0)
        c = pl.program_id(1)

        @pl.when((i == 0) & (c == 0))
        def _():
            _load_resident([(wg_hbm, wg_v), (wu_hbm, wu_v), (wd_hbm, wd_v)], sems)

        @pl.when(c == 0)
        def _():
            h = xh_ref[...] * gi_ref[...] + bi_ref[...]
            hbs[...] = h.astype(BF16)
            acc[...] = jnp.zeros_like(acc)

        def chunk(k):
            ck = c * per + k
            cols = slice(k * fc, (k + 1) * fc)
            hb = hbs[...]
            g = _dot(hb, wg_v[ck])
            u = _dot(hb, wu_v[ck])
            a = (g * jax.nn.sigmoid(g)) * u
            g_ref[:, cols] = g.astype(BF16)
            u_ref[:, cols] = u.astype(BF16)
            acc[...] += _dot(a.astype(BF16), wd_v[ck])

        for k in range(per):
            if (nc - 1) * per + k < nch:
                chunk(k)
            else:
                pl.when(c * per + k < nch)(functools.partial(chunk, k))

        @pl.when(c == nc - 1)
        def _():
            h = xh_ref[...] * gi_ref[...] + bi_ref[...]
            xhat, rstd = _ln_fwd(alpha * h + 0.5 * acc[...])
            xo_ref[...] = xhat
            rs_ref[...] = rstd
            hb_ref[...] = (xhat * go_ref[...] + bo_ref[...]).astype(BF16).T

    row = pl.BlockSpec((tm, d), lambda i, c: (i, 0))
    vec = pl.BlockSpec((1, d), lambda i, c: (0, 0))
    chunk = pl.BlockSpec((tm, per * fc), lambda i, c: (i, c))
    first = lambda: (pl.program_id(0) == 0) & (pl.program_id(1) == 0)
    last = lambda: (pl.program_id(0) == nt - 1) & (pl.program_id(1) == nc - 1)
    return pl.pallas_call(
        _carried(body, 8, 5, carry, first, last), name=name, grid=(nt, nc),
        in_specs=[row, vec, vec, ANY, ANY, ANY, vec, vec] + [ANY] * len(carry),
        out_specs=[row, pl.BlockSpec((tm, 1), lambda i, c: (i, 0)),
                   pl.BlockSpec((d, tm), lambda i, c: (0, i)), chunk, chunk] + [ANY] * len(carry),
        out_shape=[jax.ShapeDtypeStruct((t, d), F32), jax.ShapeDtypeStruct((t, 1), F32),
                   jax.ShapeDtypeStruct((d, t), BF16), jax.ShapeDtypeStruct((t, f), BF16),
                   jax.ShapeDtypeStruct((t, f), BF16)] + _carry_shapes(carry),
        scratch_shapes=[pltpu.VMEM((nch, d, fc), BF16), pltpu.VMEM((nch, d, fc), BF16),
                        pltpu.VMEM((nch, fc, d), BF16), pltpu.VMEM((tm, d), F32),
                        pltpu.VMEM((tm, d), BF16), pltpu.SemaphoreType.DMA((3,))] + _carry_scratch(carry),
        compiler_params=_params(("arbitrary", "arbitrary"), VMEM_BIG),
    )(xh, gi, bi, wg, wu, wd, go, bo, *[a for _, a in carry])


def _ffn_bwd(dh, xo, rs, go, gs, us, wg, wu, wd, alpha, name, carry=(), loss_target=None, loss_bias=None):
    t, d = xo.shape
    nch, _, fc = wg.shape
    f = nch * fc
    per = min(FFN_BWD_CHUNKS, nch)
    nc = -(-nch // per)
    tm = _row_tile(t)
    nt = t // tm

    with_loss = loss_target is not None
    nsub, lead = tm // LOSS_TILE, ROW0 // LOSS_TILE
    nlead = nsub + 1 if with_loss else 1

    def body(*refs):
        lead_refs = refs[:nlead]
        xo_ref, rs_ref, go_ref, g_ref, u_ref, wg_hbm, wu_hbm, wd_hbm = refs[nlead:nlead + 8]
        dhin_ref, dot_ref, dg_ref, du_ref, a_ref, dgain_ref, dbias_ref = refs[nlead + 8:nlead + 15]
        rest = refs[nlead + 15:]
        loss_ref, rest = (rest[0], rest[1:]) if with_loss else (None, rest)
        wg_v, wu_v, wd_v, do_ref, sems = rest[:5]
        i = pl.program_id(0)
        c = pl.program_id(1)

        @pl.when((i == 0) & (c == 0))
        def _():
            _load_resident([(wg_hbm, wg_v), (wu_hbm, wu_v), (wd_hbm, wd_v)], sems)
            dgain_ref[...] = jnp.zeros_like(dgain_ref)
            dbias_ref[...] = jnp.zeros_like(dbias_ref)
            if with_loss:
                rest[5][...] = jnp.zeros_like(rest[5])

        def tile_dh():
            if not with_loss:
                return lead_refs[0][...]
            part = rest[5]
            for k in range(nsub):
                sl = slice(k * LOSS_TILE, (k + 1) * LOSS_TILE)
                rows = i * tm + k * LOSS_TILE + lax.broadcasted_iota(jnp.int32, (LOSS_TILE, 1), 0)
                y = xo_ref[sl, :] * go_ref[...] + lead_refs[nsub][...]
                e = jnp.where(rows >= ROW0, y - lead_refs[k][...], 0.0)
                part[...] += jnp.sum(e * e, axis=0, keepdims=True)
                dhin_ref[sl, :] = e * (1.0 / d)

            @pl.when(i == nt - 1)
            def _():
                loss_ref[...] = jnp.full((1, LANES), 0.5 / d, F32) * jnp.sum(part[...])

            return dhin_ref[...]

        @pl.when(c == 0)
        def _():
            dz, dgp, dbp = _ln_bwd(tile_dh(), xo_ref[...], rs_ref[...], go_ref[...])
            dgain_ref[...] += dgp
            dbias_ref[...] += dbp
            dob = (0.5 * dz).astype(BF16)
            do_ref[...] = dob
            dot_ref[...] = dob.T
            dhin_ref[...] = alpha * dz

        def chunk(k):
            ck = c * per + k
            cols = slice(k * fc, (k + 1) * fc)
            g = g_ref[:, cols].astype(F32)
            u = u_ref[:, cols].astype(F32)
            sg = jax.nn.sigmoid(g)
            sl = g * sg
            da = _dot_nt(do_ref[...], wd_v[ck])
            dgb = (da * u * (sg * (1.0 + g * (1.0 - sg)))).astype(BF16)
            dub = (da * sl).astype(BF16)
            a_ref[:, cols] = (sl * u).astype(BF16)
            dg_ref[:, cols] = dgb
            du_ref[:, cols] = dub
            dhin_ref[...] += _dot_nt(dgb, wg_v[ck]) + _dot_nt(dub, wu_v[ck])

        for k in range(per):
            if (nc - 1) * per + k < nch:
                chunk(k)
            else:
                pl.when(c * per + k < nch)(functools.partial(chunk, k))

    row = pl.BlockSpec((tm, d), lambda i, c: (i, 0))
    vec = pl.BlockSpec((1, d), lambda i, c: (0, 0))
    chunk = pl.BlockSpec((tm, per * fc), lambda i, c: (i, c))
    first = lambda: (pl.program_id(0) == 0) & (pl.program_id(1) == 0)
    last = lambda: (pl.program_id(0) == nt - 1) & (pl.program_id(1) == nc - 1)
    if with_loss:
        lead_specs = [pl.BlockSpec((LOSS_TILE, d), lambda i, c, k=k: (jnp.maximum(i * nsub + k - lead, 0), 0))
                      for k in range(nsub)] + [vec]
        lead_args = [loss_target] * nsub + [loss_bias]
        loss_spec, loss_shape = [pl.BlockSpec((1, LANES), lambda i, c: (0, 0))], [jax.ShapeDtypeStruct((1, LANES), F32)]
        loss_scratch = [pltpu.VMEM((1, d), F32)]
    else:
        lead_specs, lead_args, loss_spec, loss_shape, loss_scratch = [row], [dh], [], [], []
    return pl.pallas_call(
        _carried(body, nlead + 8, 7 + len(loss_spec), carry, first, last), name=name, grid=(nt, nc),
        in_specs=lead_specs + [row, pl.BlockSpec((tm, 1), lambda i, c: (i, 0)), vec, chunk, chunk,
                               ANY, ANY, ANY] + [ANY] * len(carry),
        out_specs=[row, pl.BlockSpec((d, tm), lambda i, c: (0, i)), chunk, chunk, chunk, vec, vec]
                  + loss_spec + [ANY] * len(carry),
        out_shape=[jax.ShapeDtypeStruct((t, d), F32), jax.ShapeDtypeStruct((d, t), BF16),
                   jax.ShapeDtypeStruct((t, f), BF16), jax.ShapeDtypeStruct((t, f), BF16),
                   jax.ShapeDtypeStruct((t, f), BF16), jax.ShapeDtypeStruct((1, d), F32),
                   jax.ShapeDtypeStruct((1, d), F32)] + loss_shape + _carry_shapes(carry),
        scratch_shapes=[pltpu.VMEM((nch, d, fc), BF16), pltpu.VMEM((nch, d, fc), BF16),
                        pltpu.VMEM((nch, fc, d), BF16), pltpu.VMEM((tm, d), BF16),
                        pltpu.SemaphoreType.DMA((3,))] + loss_scratch + _carry_scratch(carry),
        compiler_params=_params(("arbitrary", "arbitrary"), VMEM_BIG),
    )(*lead_args, xo, rs, go, gs, us, wg, wu, wd, *[a for _, a in carry])


def _wgrad(xt, ys, name, carry=()):
    m, t = xt.shape
    n = ys[0].shape[1]
    tn = min(n, MXU_COLS)
    ny = len(ys)

    def body(*refs):
        x_hbm = refs[0]
        y_refs = refs[1:1 + ny]
        o_refs = refs[1 + ny:1 + 2 * ny]
        xv, sems = refs[1 + 2 * ny:]

        @pl.when(pl.program_id(0) == 0)
        def _():
            _load_resident([(x_hbm, xv)], sems)

        for y_ref, o_ref in zip(y_refs, o_refs):
            o_ref[...] = _dot(xv[...], y_ref[...].astype(BF16))

    steps = n // tn
    first = lambda: pl.program_id(0) == 0
    last = lambda: pl.program_id(0) == steps - 1
    return pl.pallas_call(
        _carried(body, 1 + ny, ny, carry, first, last), name=name, grid=(steps,),
        in_specs=[ANY] + [pl.BlockSpec((t, tn), lambda c: (0, c)) for _ in ys] + [ANY] * len(carry),
        out_specs=[pl.BlockSpec((m, tn), lambda c: (0, c)) for _ in ys] + [ANY] * len(carry),
        out_shape=[jax.ShapeDtypeStruct((m, n), F32) for _ in ys] + _carry_shapes(carry),
        scratch_shapes=[pltpu.VMEM((m, t), BF16), pltpu.SemaphoreType.DMA((1,))] + _carry_scratch(carry),
        compiler_params=_params(("arbitrary",), VMEM_BIG),
    )(xt, *ys, *[a for _, a in carry])


def _cast_t(h, name):
    t, d = h.shape
    tm = _row_tile(t)

    def body(h_ref, o_ref):
        o_ref[...] = h_ref[...].astype(BF16).T

    return pl.pallas_call(
        body, name=name, grid=(t // tm,),
        in_specs=[pl.BlockSpec((tm, d), lambda i: (i, 0))],
        out_specs=pl.BlockSpec((d, tm), lambda i: (0, i)),
        out_shape=jax.ShapeDtypeStruct((d, t), BF16),
        compiler_params=_params(("arbitrary",), VMEM_MID),
    )(h)


def _shift_rows(u, halo, tm):
    r = lax.broadcasted_iota(jnp.int32, (tm, 1), 0)
    u1 = jnp.where(r == 0, halo[7:8], pltpu.roll(u, 1, 0))
    u2 = jnp.where(r == 0, halo[6:7], jnp.where(r == 1, halo[7:8], pltpu.roll(u, 2, 0)))
    return u1, u2


def _conv_fwd(xh, gi, bi, w_in, cw, w_out, go, bo, alpha, name, carry=()):
    t, d = xh.shape
    tm = _row_tile(t)
    nt = t // tm

    def body(xh_ref, gi_ref, bi_ref, win_ref, cw_ref, wout_ref, go_ref, bo_ref,
             xo_ref, rs_ref, hb_ref, p_ref, m_ref, halo):
        i = pl.program_id(0)

        @pl.when(i == 0)
        def _():
            halo[...] = jnp.zeros_like(halo)

        h = xh_ref[...] * gi_ref[...] + bi_ref[...]
        hb = h.astype(BF16)
        bg = _dot(hb, win_ref[:, 0:d])
        cg = _dot(hb, win_ref[:, d:2 * d])
        val = _dot(hb, win_ref[:, 2 * d:3 * d])
        p_ref[:, 0:d] = bg.astype(BF16)
        p_ref[:, d:2 * d] = cg.astype(BF16)
        p_ref[:, 2 * d:3 * d] = val.astype(BF16)
        rows = i * tm + lax.broadcasted_iota(jnp.int32, (tm, 1), 0)
        u = jnp.where(rows >= PAD, cg * val, 0.0)
        u1, u2 = _shift_rows(u, halo[...], tm)
        halo[...] = u[tm - 8:tm]
        y = cw_ref[0:1] * u2 + cw_ref[1:2] * u1 + cw_ref[2:3] * u
        mb = (bg * y).astype(BF16)
        m_ref[...] = mb.T
        xhat, rstd = _ln_fwd(alpha * h + _dot(mb, wout_ref[...]))
        xo_ref[...] = xhat
        rs_ref[...] = rstd
        hb_ref[...] = (xhat * go_ref[...] + bo_ref[...]).astype(BF16).T

    row = pl.BlockSpec((tm, d), lambda i: (i, 0))
    col = pl.BlockSpec((d, tm), lambda i: (0, i))
    vec = pl.BlockSpec((1, d), lambda i: (0, 0))
    first = lambda: pl.program_id(0) == 0
    last = lambda: pl.program_id(0) == nt - 1
    return pl.pallas_call(
        _carried(body, 8, 5, carry, first, last), name=name, grid=(nt,),
        in_specs=[row, vec, vec, pl.BlockSpec((d, 3 * d), lambda i: (0, 0)),
                  pl.BlockSpec((3, d), lambda i: (0, 0)), pl.BlockSpec((d, d), lambda i: (0, 0)),
                  vec, vec] + [ANY] * len(carry),
        out_specs=[row, pl.BlockSpec((tm, 1), lambda i: (i, 0)), col,
                   pl.BlockSpec((tm, 3 * d), lambda i: (i, 0)), col] + [ANY] * len(carry),
        out_shape=[jax.ShapeDtypeStruct((t, d), F32), jax.ShapeDtypeStruct((t, 1), F32),
                   jax.ShapeDtypeStruct((d, t), BF16), jax.ShapeDtypeStruct((t, 3 * d), BF16),
                   jax.ShapeDtypeStruct((d, t), BF16)] + _carry_shapes(carry),
        scratch_shapes=[pltpu.VMEM((8, d), F32)] + _carry_scratch(carry),
        compiler_params=_params(("arbitrary",), VMEM_BIG),
    )(xh, gi, bi, w_in, cw, w_out, go, bo, *[a for _, a in carry])


def _conv_bwd(dh, xo, rs, go, p, cw, w_in, w_out, alpha, name):
    t, d = dh.shape
    tm = _row_tile(t)
    nt = t // tm
    tb = tm // 8

    def body(dh_ref, xo_ref, rs_ref, go_ref, p_ref, ph_ref, cw_ref, win_ref, wout_ref,
             dhin_ref, dmix_ref, dp_ref, dcw_ref, dgain_ref, dbias_ref, carry):
        i = pl.program_id(0)
        tile = nt - 1 - i

        @pl.when(i == 0)
        def _():
            carry[...] = jnp.zeros_like(carry)
            dcw_ref[...] = jnp.zeros_like(dcw_ref)
            dgain_ref[...] = jnp.zeros_like(dgain_ref)
            dbias_ref[...] = jnp.zeros_like(dbias_ref)

        dz, dgp, dbp = _ln_bwd(dh_ref[...], xo_ref[...], rs_ref[...], go_ref[...])
        dgain_ref[...] += dgp
        dbias_ref[...] += dbp
        dmixb = dz.astype(BF16)
        dmix_ref[...] = dmixb
        dm = _dot_nt(dmixb, wout_ref[...])

        bg = p_ref[:, 0:d].astype(F32)
        cg = p_ref[:, d:2 * d].astype(F32)
        val = p_ref[:, 2 * d:3 * d].astype(F32)
        rows = tile * tm + lax.broadcasted_iota(jnp.int32, (tm, 1), 0)
        valid = rows >= PAD
        u = jnp.where(valid, cg * val, 0.0)
        hrows = tile * tm - 8 + lax.broadcasted_iota(jnp.int32, (8, 1), 0)
        hu = jnp.where((hrows >= PAD) & (tile > 0),
                       ph_ref[:, d:2 * d].astype(F32) * ph_ref[:, 2 * d:3 * d].astype(F32), 0.0)
        u1, u2 = _shift_rows(u, hu, tm)
        w0, w1, w2 = cw_ref[0:1], cw_ref[1:2], cw_ref[2:3]
        y = w0 * u2 + w1 * u1 + w2 * u
        dbg = dm * y
        dy = dm * bg
        dcw_ref[0:1] += jnp.sum(dy * u2, axis=0, keepdims=True)
        dcw_ref[1:2] += jnp.sum(dy * u1, axis=0, keepdims=True)
        dcw_ref[2:3] += jnp.sum(dy * u, axis=0, keepdims=True)

        nxt = carry[...]
        r = lax.broadcasted_iota(jnp.int32, (tm, 1), 0)
        dy1 = jnp.where(r == tm - 1, nxt[0:1], pltpu.roll(dy, tm - 1, 0))
        dy2 = jnp.where(r == tm - 2, nxt[0:1],
                        jnp.where(r == tm - 1, nxt[1:2], pltpu.roll(dy, tm - 2, 0)))
        carry[...] = dy[0:8]
        du = jnp.where(valid, w2 * dy + w1 * dy1 + w0 * dy2, 0.0)
        dbgb = dbg.astype(BF16)
        dcgb = (du * val).astype(BF16)
        dvalb = (du * cg).astype(BF16)
        dp_ref[:, 0:d] = dbgb
        dp_ref[:, d:2 * d] = dcgb
        dp_ref[:, 2 * d:3 * d] = dvalb
        dhin_ref[...] = (alpha * dz + _dot_nt(dbgb, win_ref[:, 0:d])
                         + _dot_nt(dcgb, win_ref[:, d:2 * d]) + _dot_nt(dvalb, win_ref[:, 2 * d:3 * d]))

    row = pl.BlockSpec((tm, d), lambda i: (nt - 1 - i, 0))
    vec = pl.BlockSpec((1, d), lambda i: (0, 0))
    prow = pl.BlockSpec((tm, 3 * d), lambda i: (nt - 1 - i, 0))
    return pl.pallas_call(
        body, name=name, grid=(nt,),
        in_specs=[row, row, pl.BlockSpec((tm, 1), lambda i: (nt - 1 - i, 0)), vec, prow,
                  pl.BlockSpec((8, 3 * d), lambda i: (jnp.maximum((nt - 1 - i) * tb - 1, 0), 0)),
                  pl.BlockSpec((3, d), lambda i: (0, 0)),
                  pl.BlockSpec((d, 3 * d), lambda i: (0, 0)), pl.BlockSpec((d, d), lambda i: (0, 0))],
        out_specs=[row, row, prow, pl.BlockSpec((3, d), lambda i: (0, 0)), vec, vec],
        out_shape=[jax.ShapeDtypeStruct((t, d), F32), jax.ShapeDtypeStruct((t, d), BF16),
                   jax.ShapeDtypeStruct((t, 3 * d), BF16), jax.ShapeDtypeStruct((3, d), F32),
                   jax.ShapeDtypeStruct((1, d), F32), jax.ShapeDtypeStruct((1, d), F32)],
        scratch_shapes=[pltpu.VMEM((8, d), F32)],
        compiler_params=_params(("arbitrary",), VMEM_BIG),
    )(dh, xo, rs, go, p, p, cw, w_in, w_out)


def _kv_fwd(xh, gi, bi, wk, wv, wf, fb, name, carry=()):
    t, d = xh.shape
    tm = _row_tile(t)
    nt = t // tm

    def body(xh_ref, gi_ref, bi_ref, wk_ref, wv_ref, wf_ref, fb_ref,
             k_ref, v_ref, lg_ref, c_ref, ct_ref, run):
        i = pl.program_id(0)

        @pl.when(i == 0)
        def _():
            run[...] = jnp.zeros_like(run)

        x = (xh_ref[...] * gi_ref[...] + bi_ref[...]).astype(BF16)
        k_ref[...] = _dot(x, wk_ref[...]).astype(BF16)
        v_ref[...] = _dot(x, wv_ref[...]).astype(BF16)
        logit = _dot(x, wf_ref[...]) + fb_ref[...]
        lg_ref[...] = logit
        logf = jnp.minimum(logit, 0.0) - jnp.log(1.0 + jnp.exp(-jnp.abs(logit)))
        rows = i * tm + lax.broadcasted_iota(jnp.int32, (tm, 1), 0)
        logf = jnp.where(rows >= PAD, logf, 0.0)
        tri = (lax.broadcasted_iota(jnp.int32, (tm, tm), 0)
               >= lax.broadcasted_iota(jnp.int32, (tm, tm), 1)).astype(F32)
        cs = jnp.dot(tri, logf, precision=lax.Precision.HIGHEST, preferred_element_type=F32) + run[...]
        run[...] = cs[tm - 1:tm]
        c_ref[...] = cs
        ct_ref[...] = cs.T

    row = pl.BlockSpec((tm, d), lambda i: (i, 0))
    vec = pl.BlockSpec((1, d), lambda i: (0, 0))
    gate = pl.BlockSpec((tm, LANES), lambda i: (i, 0))
    sq = pl.BlockSpec((d, d), lambda i: (0, 0))
    first = lambda: pl.program_id(0) == 0
    last = lambda: pl.program_id(0) == nt - 1
    return pl.pallas_call(
        _carried(body, 7, 5, carry, first, last), name=name, grid=(nt,),
        in_specs=[row, vec, vec, sq, sq, pl.BlockSpec((d, LANES), lambda i: (0, 0)),
                  pl.BlockSpec((1, LANES), lambda i: (0, 0))] + [ANY] * len(carry),
        out_specs=[row, row, gate, gate, pl.BlockSpec((LANES, tm), lambda i: (0, i))] + [ANY] * len(carry),
        out_shape=[jax.ShapeDtypeStruct((t, d), BF16), jax.ShapeDtypeStruct((t, d), BF16),
                   jax.ShapeDtypeStruct((t, LANES), F32), jax.ShapeDtypeStruct((t, LANES), F32),
                   jax.ShapeDtypeStruct((LANES, t), F32)] + _carry_shapes(carry),
        scratch_shapes=[pltpu.VMEM((1, LANES), F32)] + _carry_scratch(carry),
        compiler_params=_params(("arbitrary",), VMEM_MID),
    )(xh, gi, bi, wk, wv, wf, fb, *[a for _, a in carry])


def _kv_bwd(dk, dv, dcs, dcq, logit, dh_other, wk, wv, wf, name):
    t, d = dk.shape
    tm = _row_tile(t)
    nt = t // tm

    def body(dk_ref, dv_ref, dcs_ref, dcq_ref, lg_ref, oth_ref, wk_ref, wv_ref, wf_ref,
             dh_ref, dl_ref, dfb_ref, run):
        i = pl.program_id(0)
        tile = nt - 1 - i

        @pl.when(i == 0)
        def _():
            run[...] = jnp.zeros_like(run)
            dfb_ref[...] = jnp.zeros_like(dfb_ref)

        lane = lax.broadcasted_iota(jnp.int32, (tm, LANES), 1)
        dc = dcq_ref[...]
        for hh in range(N_HEADS):
            dc = dc + jnp.where(lane == hh, jnp.sum(dcs_ref[hh], axis=1, keepdims=True), 0.0)
        tri = (lax.broadcasted_iota(jnp.int32, (tm, tm), 0)
               <= lax.broadcasted_iota(jnp.int32, (tm, tm), 1)).astype(F32)
        dlf = jnp.dot(tri, dc, precision=lax.Precision.HIGHEST, preferred_element_type=F32) + run[...]
        run[...] = dlf[0:1]
        rows = tile * tm + lax.broadcasted_iota(jnp.int32, (tm, 1), 0)
        dlogit = jnp.where(rows >= PAD, dlf * jax.nn.sigmoid(-lg_ref[...]), 0.0)
        dfb_ref[...] += jnp.sum(dlogit, axis=0, keepdims=True)
        dlb = dlogit.astype(BF16)
        dl_ref[...] = dlb
        dh_ref[...] = (oth_ref[...] + _dot_nt(dk_ref[...], wk_ref[...])
                       + _dot_nt(dv_ref[...], wv_ref[...]) + _dot_nt(dlb, wf_ref[...]))

    row = pl.BlockSpec((tm, d), lambda i: (nt - 1 - i, 0))
    gate = pl.BlockSpec((tm, LANES), lambda i: (nt - 1 - i, 0))
    sq = pl.BlockSpec((d, d), lambda i: (0, 0))
    return pl.pallas_call(
        body, name=name, grid=(nt,),
        in_specs=[row, row, pl.BlockSpec((N_HEADS, tm, LANES), lambda i: (0, nt - 1 - i, 0)), gate, gate, row,
                  sq, sq, pl.BlockSpec((d, LANES), lambda i: (0, 0))],
        out_specs=[row, gate, pl.BlockSpec((1, LANES), lambda i: (0, 0))],
        out_shape=[jax.ShapeDtypeStruct((t, d), F32), jax.ShapeDtypeStruct((t, LANES), BF16),
                   jax.ShapeDtypeStruct((1, LANES), F32)],
        scratch_shapes=[pltpu.VMEM((1, LANES), F32)],
        compiler_params=_params(("arbitrary",), VMEM_MID),
    )(dk, dv, dcs, dcq, logit, dh_other, wk, wv, wf)


def _proj(xh, gi, bi, w, name):
    t, k = xh.shape
    n = w.shape[1]
    tm = _row_tile(t)

    def body(x_ref, g_ref, b_ref, w_ref, o_ref):
        x = (x_ref[...] * g_ref[...] + b_ref[...]).astype(BF16)
        o_ref[...] = _dot(x, w_ref[...]).astype(BF16)

    vec = pl.BlockSpec((1, k), lambda i: (0, 0))
    return pl.pallas_call(
        body, name=name, grid=(t // tm,),
        in_specs=[pl.BlockSpec((tm, k), lambda i: (i, 0)), vec, vec, pl.BlockSpec((k, n), lambda i: (0, 0))],
        out_specs=pl.BlockSpec((tm, n), lambda i: (i, 0)),
        out_shape=jax.ShapeDtypeStruct((t, n), BF16),
        compiler_params=_params(("arbitrary",), VMEM_MID),
    )(xh, gi, bi, w)


def _add_proj_nt(base, y, w, name):
    t, n = y.shape
    k = w.shape[0]
    tm = _row_tile(t)

    def body(b_ref, y_ref, w_ref, o_ref):
        o_ref[...] = b_ref[...] + _dot_nt(y_ref[...].astype(BF16), w_ref[...])

    return pl.pallas_call(
        body, name=name, grid=(t // tm,),
        in_specs=[pl.BlockSpec((tm, k), lambda i: (i, 0)), pl.BlockSpec((tm, n), lambda i: (i, 0)),
                  pl.BlockSpec((k, n), lambda i: (0, 0))],
        out_specs=pl.BlockSpec((tm, k), lambda i: (i, 0)),
        out_shape=jax.ShapeDtypeStruct((t, k), F32),
        compiler_params=_params(("arbitrary",), VMEM_MID),
    )(base, y, w)


def _attn_out_fwd(ot, xh, gi, bi, w_o, go, bo, alpha, name):
    t, d = xh.shape
    tm = _row_tile(t)

    def body(ot_ref, xh_ref, gi_ref, bi_ref, wo_ref, go_ref, bo_ref, xo_ref, rs_ref, hb_ref):
        h = xh_ref[...] * gi_ref[...] + bi_ref[...]
        xhat, rstd = _ln_fwd(alpha * h + _dot_tn(ot_ref[...], wo_ref[...]))
        xo_ref[...] = xhat
        rs_ref[...] = rstd
        hb_ref[...] = (xhat * go_ref[...] + bo_ref[...]).astype(BF16).T

    row = pl.BlockSpec((tm, d), lambda i: (i, 0))
    col = pl.BlockSpec((d, tm), lambda i: (0, i))
    vec = pl.BlockSpec((1, d), lambda i: (0, 0))
    return pl.pallas_call(
        body, name=name, grid=(t // tm,),
        in_specs=[col, row, vec, vec, pl.BlockSpec((d, d), lambda i: (0, 0)), vec, vec],
        out_specs=[row, pl.BlockSpec((tm, 1), lambda i: (i, 0)), col],
        out_shape=[jax.ShapeDtypeStruct((t, d), F32), jax.ShapeDtypeStruct((t, 1), F32),
                   jax.ShapeDtypeStruct((d, t), BF16)],
        compiler_params=_params(("arbitrary",), VMEM_MID),
    )(ot, xh, gi, bi, w_o, go, bo)


def _attn_out_bwd(dh, xo, rs, go, ot, w_o, alpha, name):
    t, d = dh.shape
    tm = _row_tile(t)
    hd = d // N_HEADS

    def body(dh_ref, xo_ref, rs_ref, go_ref, ot_ref, wo_ref,
             dres_ref, dmix_ref, dot_ref, delta_ref, dgain_ref, dbias_ref):
        @pl.when(pl.program_id(0) == 0)
        def _():
            dgain_ref[...] = jnp.zeros_like(dgain_ref)
            dbias_ref[...] = jnp.zeros_like(dbias_ref)

        dz, dgp, dbp = _ln_bwd(dh_ref[...], xo_ref[...], rs_ref[...], go_ref[...])
        dgain_ref[...] += dgp
        dbias_ref[...] += dbp
        dres_ref[...] = alpha * dz
        dmixb = dz.astype(BF16)
        dmix_ref[...] = dmixb
        dot_t = _dot_nt(wo_ref[...], dmixb)
        dot_ref[...] = dot_t.astype(BF16)
        prod = dot_t * ot_ref[...].astype(F32)
        delta_ref[...] = jnp.sum(prod.reshape(N_HEADS, hd, tm), axis=1)

    row = pl.BlockSpec((tm, d), lambda i: (i, 0))
    vec = pl.BlockSpec((1, d), lambda i: (0, 0))
    col = pl.BlockSpec((d, tm), lambda i: (0, i))
    return pl.pallas_call(
        body, name=name, grid=(t // tm,),
        in_specs=[row, row, pl.BlockSpec((tm, 1), lambda i: (i, 0)), vec, col,
                  pl.BlockSpec((d, d), lambda i: (0, 0))],
        out_specs=[row, row, col, pl.BlockSpec((N_HEADS, tm), lambda i: (0, i)), vec, vec],
        out_shape=[jax.ShapeDtypeStruct((t, d), F32), jax.ShapeDtypeStruct((t, d), BF16),
                   jax.ShapeDtypeStruct((d, t), BF16), jax.ShapeDtypeStruct((N_HEADS, t), F32),
                   jax.ShapeDtypeStruct((1, d), F32), jax.ShapeDtypeStruct((1, d), F32)],
        compiler_params=_params(("arbitrary",), VMEM_MID),
    )(dh, xo, rs, go, ot, w_o)


def _scores_t(k, q, ct_ref, c_ref, h, i, j, tq, tk, scale, masked):
    sub = lax.broadcasted_iota(jnp.int32, (8, tq), 0)
    cq = jnp.sum(jnp.where(sub == h, ct_ref[...], 0.0), axis=0, keepdims=True) * LOG2E
    lane = lax.broadcasted_iota(jnp.int32, (tk, LANES), 1)
    ck = jnp.sum(jnp.where(lane == h, c_ref[...], 0.0), axis=1, keepdims=True) * LOG2E
    st = _dot_nt(k, q) * (scale * LOG2E) - ck
    if masked:
        kpos = j * tk + lax.broadcasted_iota(jnp.int32, (tk, 1), 0)
        qpos = i * tq + lax.broadcasted_iota(jnp.int32, (1, tq), 1)
        st = jnp.where((kpos <= qpos) & (kpos >= PAD), st, NEG_INF)
    return st, cq


def _tri_pairs(n, by_row):
    if by_row:
        pairs = [(i, j) for i in range(n) for j in range(i + 1)]
    else:
        pairs = [(i, j) for j in range(n) for i in range(j, n)]
    return (jnp.asarray([p[0] for p in pairs], jnp.int32), jnp.asarray([p[1] for p in pairs], jnp.int32))


def _attn_fwd(q, k, v, c, ct, name, carry=()):
    t, d = q.shape
    hd = d // N_HEADS
    tq = tk = _row_tile(t)
    nq = t // tq
    scale = 1.0 / math.sqrt(hd)

    hps = ATTN_HEADS_PER_STEP

    def body(it_ref, jt_ref, q_ref, k_ref, v_ref, c_ref, ct_ref, ot_ref, lse_ref, m_s, l_s, acc):
        hp, p_ = pl.program_id(0), pl.program_id(1)
        i, j = it_ref[p_], jt_ref[p_]

        @pl.when(j == 0)
        def _():
            m_s[...] = jnp.full_like(m_s, NEG_INF)
            l_s[...] = jnp.zeros_like(l_s)
            acc[...] = jnp.zeros_like(acc)

        def update(masked):
            scores = []
            for e in range(hps):
                cols = slice(e * hd, (e + 1) * hd)
                scores.append(_scores_t(k_ref[:, cols], q_ref[:, cols], ct_ref, c_ref, hp * hps + e,
                                        i, j, tq, tk, scale, masked))
            probs = []
            for e, (st, cq) in enumerate(scores):
                m_new = jnp.maximum(m_s[e], jnp.max(st, axis=0, keepdims=True) + cq)
                a = jnp.exp2(m_s[e] - m_new)
                p = jnp.exp2(st - (m_new - cq))
                l_s[e] = a * l_s[e] + jnp.sum(p, axis=0, keepdims=True)
                m_s[e] = m_new
                probs.append((a, p.astype(BF16)))
            for e, (a, pb) in enumerate(probs):
                acc[e] = a * acc[e] + _dot_tn(v_ref[:, e * hd:(e + 1) * hd], pb)

        edge = (j == i) | (j == 0)
        pl.when(edge)(lambda: update(True))
        pl.when(jnp.logical_not(edge))(lambda: update(False))

        @pl.when(j == i)
        def _():
            for e in range(hps):
                ot_ref[e * hd:(e + 1) * hd, :] = (acc[e] / l_s[e]).astype(BF16)
                lse_ref[e] = m_s[e] + jnp.log2(l_s[e])

    it, jt = _tri_pairs(nq, by_row=True)
    npairs = it.shape[0]
    nhp = N_HEADS // hps
    kv = pl.BlockSpec((tk, hps * hd), lambda h, p, it, jt: (jt[p], h))
    first = lambda: (pl.program_id(0) == 0) & (pl.program_id(1) == 0)
    last = lambda: (pl.program_id(0) == nhp - 1) & (pl.program_id(1) == npairs - 1)
    return pl.pallas_call(
        _carried(body, 7, 2, carry, first, last), name=name,
        grid_spec=pltpu.PrefetchScalarGridSpec(
            num_scalar_prefetch=2, grid=(nhp, npairs),
            in_specs=[pl.BlockSpec((tq, hps * hd), lambda h, p, it, jt: (it[p], h)), kv, kv,
                      pl.BlockSpec((tk, LANES), lambda h, p, it, jt: (jt[p], 0)),
                      pl.BlockSpec((8, tq), lambda h, p, it, jt: (0, it[p]))] + [ANY] * len(carry),
            out_specs=[pl.BlockSpec((hps * hd, tq), lambda h, p, it, jt: (h, it[p])),
                       pl.BlockSpec((hps, 1, tq), lambda h, p, it, jt: (h, 0, it[p]))] + [ANY] * len(carry),
            scratch_shapes=[pltpu.VMEM((hps, 1, tq), F32), pltpu.VMEM((hps, 1, tq), F32),
                            pltpu.VMEM((hps, hd, tq), F32)] + _carry_scratch(carry)),
        out_shape=[jax.ShapeDtypeStruct((d, t), BF16), jax.ShapeDtypeStruct((N_HEADS, 1, t), F32)]
                  + _carry_shapes(carry),
        compiler_params=_params(("arbitrary", "arbitrary"), VMEM_MID),
    )(it, jt, q, k, v, c, ct, *[a for _, a in carry])


def _attn_bwd(q, k, v, c, ct, lse, delta, dot_t, name, carry=()):
    t, d = q.shape
    hd = d // N_HEADS
    tq = tk = _row_tile(t)
    nq = t // tq
    scale = 1.0 / math.sqrt(hd)
    hps = ATTN_BWD_HEADS_PER_STEP

    def body(it_ref, jt_ref, q_ref, k_ref, v_ref, c_ref, ct_ref, lse_ref, delta_ref, dot_ref,
             dq_ref, dk_ref, dv_ref, dcs_ref, drow_ref, dk_acc, dv_acc, dc_acc):
        hp, p_ = pl.program_id(0), pl.program_id(1)
        i, j = it_ref[p_], jt_ref[p_]

        @pl.when(p_ == 0)
        def _():
            dq_ref[...] = jnp.zeros_like(dq_ref)
            drow_ref[...] = jnp.zeros_like(drow_ref)

        @pl.when(i == j)
        def _():
            dk_acc[...] = jnp.zeros_like(dk_acc)
            dv_acc[...] = jnp.zeros_like(dv_acc)
            dc_acc[...] = jnp.zeros_like(dc_acc)

        def update(masked):
            sub = lax.broadcasted_iota(jnp.int32, (8, tq), 0)
            rows = pl.ds(pl.multiple_of(i * tq, tq), tq)
            stage = []
            for e in range(hps):
                cols = slice(e * hd, (e + 1) * hd)
                st, cq = _scores_t(k_ref[:, cols], q_ref[:, cols], ct_ref, c_ref, hp * hps + e,
                                   i, j, tq, tk, scale, masked)
                dp = _dot(v_ref[:, cols], dot_ref[cols, :])
                stage.append((st, cq, dp))
            grads = []
            for e, (st, cq, dp) in enumerate(stage):
                p = jnp.exp2(st - (lse_ref[e] - cq))
                dl = jnp.sum(jnp.where(sub == hp * hps + e, delta_ref[...], 0.0), axis=0, keepdims=True)
                ds = p * (dp - dl)
                part = ds[:, 0:LANES]
                for g in range(1, tq // LANES):
                    part = part + ds[:, g * LANES:(g + 1) * LANES]
                dc_acc[e] += part
                drow_ref[e, i] += jnp.broadcast_to(jnp.sum(ds, axis=0, keepdims=True), (8, tq))
                grads.append((p.astype(BF16), ds.astype(BF16)))
            for e, (pb, dsb) in enumerate(grads):
                cols = slice(e * hd, (e + 1) * hd)
                dv_acc[e] += _dot_nt(pb, dot_ref[cols, :])
                dk_acc[e] += _dot(dsb, q_ref[:, cols]) * scale
                dq_ref[rows, cols] += _dot_tn(dsb, k_ref[:, cols]) * scale

        edge = (j == i) | (j == 0)
        pl.when(edge)(lambda: update(True))
        pl.when(jnp.logical_not(edge))(lambda: update(False))

        @pl.when(i == nq - 1)
        def _():
            for e in range(hps):
                cols = slice(e * hd, (e + 1) * hd)
                dk_ref[:, cols] = dk_acc[e].astype(BF16)
                dv_ref[:, cols] = dv_acc[e].astype(BF16)
                dcs_ref[e] = -dc_acc[e]

    it, jt = _tri_pairs(nq, by_row=False)
    npairs = it.shape[0]
    nhp = N_HEADS // hps
    kv = pl.BlockSpec((tk, hps * hd), lambda h, p, it, jt: (jt[p], h))
    first = lambda: (pl.program_id(0) == 0) & (pl.program_id(1) == 0)
    last = lambda: (pl.program_id(0) == nhp - 1) & (pl.program_id(1) == npairs - 1)
    return pl.pallas_call(
        _carried(body, 10, 5, carry, first, last), name=name,
        grid_spec=pltpu.PrefetchScalarGridSpec(
            num_scalar_prefetch=2, grid=(nhp, npairs),
            in_specs=[pl.BlockSpec((tq, hps * hd), lambda h, p, it, jt: (it[p], h)), kv, kv,
                      pl.BlockSpec((tk, LANES), lambda h, p, it, jt: (jt[p], 0)),
                      pl.BlockSpec((8, tq), lambda h, p, it, jt: (0, it[p])),
                      pl.BlockSpec((hps, 1, tq), lambda h, p, it, jt: (h, 0, it[p])),
                      pl.BlockSpec((N_HEADS, tq), lambda h, p, it, jt: (0, it[p])),
                      pl.BlockSpec((hps * hd, tq), lambda h, p, it, jt: (h, it[p]))] + [ANY] * len(carry),
            out_specs=[pl.BlockSpec((t, hps * hd), lambda h, p, it, jt: (0, h)), kv, kv,
                       pl.BlockSpec((hps, tk, LANES), lambda h, p, it, jt: (h, jt[p], 0)),
                       pl.BlockSpec((hps, nq, 8, tq), lambda h, p, it, jt: (h, 0, 0, 0))] + [ANY] * len(carry),
            scratch_shapes=[pltpu.VMEM((hps, tk, hd), F32), pltpu.VMEM((hps, tk, hd), F32),
                            pltpu.VMEM((hps, tk, LANES), F32)] + _carry_scratch(carry)),
        out_shape=[jax.ShapeDtypeStruct((t, d), F32), jax.ShapeDtypeStruct((t, d), BF16),
                   jax.ShapeDtypeStruct((t, d), BF16), jax.ShapeDtypeStruct((N_HEADS, t, LANES), F32),
                   jax.ShapeDtypeStruct((N_HEADS, nq, 8, tq), F32)] + _carry_shapes(carry),
        compiler_params=_params(("arbitrary", "arbitrary"), VMEM_BIG),
    )(it, jt, q, k, v, c, ct, lse, delta, dot_t, *[a for _, a in carry])


def _adamw(w, g, m, v, name):
    r, c = w.shape
    tr = r
    for cand in (256, 128, 64, 32, 16, 8):
        if r % cand == 0 and r > cand:
            tr = cand
            break
    bc1 = 1.0 - ADAM_B1 ** ADAM_STEP
    bc2 = 1.0 - ADAM_B2 ** ADAM_STEP

    def body(w_ref, g_ref, m_ref, v_ref, d_ref, nm_ref, nv_ref):
        gg = g_ref[...]
        nm = ADAM_B1 * m_ref[...] + (1.0 - ADAM_B1) * gg
        nv = ADAM_B2 * v_ref[...] + (1.0 - ADAM_B2) * (gg * gg)
        d_ref[...] = -ADAM_LR * ((nm / bc1) / (jnp.sqrt(nv / bc2) + ADAM_EPS) + ADAM_WD * w_ref[...])
        nm_ref[...] = nm
        nv_ref[...] = nv

    blk = pl.BlockSpec((tr, c), lambda i: (i, 0))
    shp = jax.ShapeDtypeStruct((r, c), F32)
    return pl.pallas_call(
        body, name=name, grid=(r // tr,), in_specs=[blk] * 4, out_specs=[blk] * 3,
        out_shape=[shp] * 3, compiler_params=_params(("arbitrary",), VMEM_MID),
    )(w, g, m, v)


def _reduce_adamw(w, m, v, landed, name):
    nl, r, c = w.shape
    tr = next(cand for cand in range(min(r, ADAM_ROWS_MAX), 0, -BF16_ROWS) if r % cand == 0)
    nr = r // tr
    bc1 = 1.0 - ADAM_B1 ** ADAM_STEP
    bc2 = 1.0 - ADAM_B2 ** ADAM_STEP

    def body(*refs):
        w_ref, m_ref, v_ref = refs[:3]
        src_refs = refs[3:3 + nl]
        g_ref, d_ref, nm_ref, nv_ref = refs[3 + nl:]

        def update(src):
            gg = src[0].astype(F32)
            for s in range(1, N_DEV):
                gg = gg + src[s].astype(F32)
            nm = ADAM_B1 * m_ref[0] + (1.0 - ADAM_B1) * gg
            nv = ADAM_B2 * v_ref[0] + (1.0 - ADAM_B2) * (gg * gg)
            g_ref[0] = gg
            d_ref[0] = -ADAM_LR * ((nm / bc1) / (jnp.sqrt(nv / bc2) + ADAM_EPS) + ADAM_WD * w_ref[0])
            nm_ref[0] = nm
            nv_ref[0] = nv

        for idx in range(nl):
            pl.when(pl.program_id(0) == idx)(functools.partial(update, src_refs[idx]))

    def src_spec(idx):
        return pl.BlockSpec((N_DEV, tr, c),
                            lambda l, i: (0, jnp.where(l == idx, i, jnp.where(l < idx, 0, nr - 1)), 0))

    blk = pl.BlockSpec((1, tr, c), lambda l, i: (l, i, 0))
    shp = jax.ShapeDtypeStruct((nl, r, c), F32)
    return pl.pallas_call(
        body, name=name, grid=(nl, nr), in_specs=[blk] * 3 + [src_spec(idx) for idx in range(nl)],
        out_specs=[blk] * 4, out_shape=[shp] * 4,
        compiler_params=_params(("arbitrary", "arbitrary"), VMEM_MID),
    )(w, m, v, *landed)


def _sum_sources(r, name):
    n, rows, c = r.shape
    tr = next(cand for cand in range(min(rows, SUM_ROWS_MAX), 0, -BF16_ROWS) if rows % cand == 0)

    def body(r_ref, o_ref):
        acc = r_ref[0].astype(F32)
        for s in range(1, n):
            acc = acc + r_ref[s].astype(F32)
        o_ref[...] = acc

    return pl.pallas_call(
        body, name=name, grid=(rows // tr,),
        in_specs=[pl.BlockSpec((n, tr, c), lambda i: (0, i, 0))],
        out_specs=pl.BlockSpec((tr, c), lambda i: (i, 0)),
        out_shape=jax.ShapeDtypeStruct((rows, c), F32),
        compiler_params=_params(("arbitrary",), VMEM_MID),
    )(r)


def _all_gather(parts, name):
    n = len(parts)

    def body(*refs):
        x_refs, out_refs = refs[:n], refs[n:2 * n]
        send_sems, recv_sems, local_sems = refs[2 * n:]
        mx, my, mc = lax.axis_index("x"), lax.axis_index("y"), lax.axis_index("c")
        me, sibling = (mx, my, mc), (mx, my, 1 - mc)
        chips = [(1 - mx, my), (mx, 1 - my), (1 - mx, 1 - my)]

        def copy(p, k, block, to, from_input=False):
            px, py, pc = block
            rows = out_refs[p].at[4 * px + 2 * py + pc]
            return pltpu.make_async_remote_copy(
                src_ref=x_refs[p] if from_input else rows, dst_ref=rows,
                send_sem=send_sems.at[7 * p + k], recv_sem=recv_sems.at[7 * p + k],
                device_id=to, device_id_type=MESH)

        mine, sent = [], []
        for p in range(n):
            own = pltpu.make_async_copy(x_refs[p], out_refs[p].at[4 * mx + 2 * my + mc], local_sems.at[p])
            own.start()
            mine.append(own)
            first = [copy(p, 0, me, sibling, True)]
            first += [copy(p, 1 + j, me, (*chip, mc), True) for j, chip in enumerate(chips)]
            for cp in first:
                cp.start()
            sent += first
        for p in range(n):
            for j, chip in enumerate(chips):
                copy(p, 1 + j, (*chip, mc), me).wait_recv()
                fwd = copy(p, 4 + j, (*chip, mc), sibling)
                fwd.start()
                sent.append(fwd)
        for p in range(n):
            copy(p, 0, sibling, me).wait_recv()
            for j, chip in enumerate(chips):
                copy(p, 4 + j, (*chip, 1 - mc), me).wait_recv()
        for cp in sent:
            cp.wait_send()
        for own in mine:
            own.wait()

    return pl.pallas_call(
        body, name=name, in_specs=[ANY] * n, out_specs=[ANY] * n,
        out_shape=[jax.ShapeDtypeStruct((N_DEV,) + a.shape, a.dtype) for a in parts],
        scratch_shapes=[pltpu.SemaphoreType.DMA((7 * n,)), pltpu.SemaphoreType.DMA((7 * n,)),
                        pltpu.SemaphoreType.DMA((n,))],
    )(*parts)


def _pack_rows(parts, width, mult, lead=0):
    out = []
    for a in parts:
        head = a.shape[:lead]
        flat = a.reshape(head + (-1,))
        padn = (-flat.shape[-1]) % (width * mult)
        if padn:
            flat = jnp.pad(flat, [(0, 0)] * lead + [(0, padn)])
        out.append(flat.reshape(head + (-1, width)))
    return jnp.concatenate(out, axis=lead)


def _rows_of(shape, width, mult):
    n = math.prod(shape)
    per = width * mult
    return ((n + per - 1) // per) * mult


def _unpack_rows(buf, shapes, width, mult):
    lead = buf.shape[:-2]
    out, off = [], 0
    for shp in shapes:
        r = _rows_of(shp, width, mult)
        flat = buf[..., off:off + r, :].reshape(lead + (r * width,))
        out.append(flat[..., :math.prod(shp)].reshape(lead + tuple(shp)))
        off += r
    return out


def _cols_from_devices(g):
    nd = g.ndim
    perm = tuple(range(1, nd - 1)) + (0, nd - 1)
    t = jnp.transpose(g, perm)
    return t.reshape(t.shape[:-2] + (t.shape[-2] * t.shape[-1],))


def _cols_to_devices(a):
    c = a.shape[-1] // N_DEV
    t = a.reshape(a.shape[:-1] + (N_DEV, c))
    nd = t.ndim
    perm = (nd - 2,) + tuple(range(0, nd - 2)) + (nd - 1,)
    return jnp.transpose(t, perm)


WIDTH = 1024


def kernel(x, meta, ffn1_wg, ffn1_wu, ffn1_wd, ffn2_wg, ffn2_wu, ffn2_wd, ln_gain, ln_bias, conv_w_in, conv_w, conv_w_out, kv_w, f_bias, attn_w_q, attn_w_o, loss_target, m_meta, m_ffn1_wg, m_ffn1_wu, m_ffn1_wd, m_ffn2_wg, m_ffn2_wu, m_ffn2_wd, m_ln_gain, m_ln_bias, m_conv_w_in, m_conv_w, m_conv_w_out, m_kv_w, m_f_bias, m_attn_w_q, m_attn_w_o, v_meta, v_ffn1_wg, v_ffn1_wu, v_ffn1_wd, v_ffn2_wg, v_ffn2_wu, v_ffn2_wd, v_ln_gain, v_ln_bias, v_conv_w_in, v_conv_w, v_conv_w_out, v_kv_w, v_f_bias, v_attn_w_q, v_attn_w_o):
    depth = ln_gain.shape[0]
    alpha = float((2 * depth) ** 0.25)
    d = x.shape[-1]
    seq = x.shape[1]
    t = ROW0 + seq
    fsh = ffn1_wg.shape[-1]
    f = fsh * N_DEV
    fck = MXU_COLS
    nc = f // fck
    me = 4 * lax.axis_index("x") + 2 * lax.axis_index("y") + lax.axis_index("c")

    def gather_of(parts):
        return [(True, a.astype(BF16)) for a in parts]

    small = [meta, ln_gain, ln_bias, conv_w]
    small_shapes = [a.shape for a in small]
    g1g, g1u, g1d, gcin, gcout, gsmall = _all_gather(
        [a.astype(BF16) for a in (ffn1_wg[0], ffn1_wu[0], ffn1_wd[0], conv_w_in[0], conv_w_out[0])]
        + [_pack_rows(small, WIDTH, F32_ROWS)], "ag_first")
    gmeta, ggain, gbias, gcw = _unpack_rows(gsmall, small_shapes, WIDTH, F32_ROWS)

    def ffn_chunks(gg, gu, gd):
        up = lambda g: jnp.transpose(_cols_from_devices(g).reshape(d, nc, fck), (1, 0, 2))
        return up(gg), up(gu), gd.reshape(nc, fck, d)

    w_in = _cols_from_devices(gcin)
    w_out = gcout.reshape(d, d)
    fb = jnp.pad(f_bias, (0, LANES - N_HEADS)).reshape(1, LANES)
    meta_f = _cols_from_devices(gmeta)
    gain_f = _cols_from_devices(ggain)
    bias_f = _cols_from_devices(gbias)
    cw_f = _cols_from_devices(gcw)[0]

    def gb(l, n):
        return gain_f[l, n].reshape(1, d), bias_f[l, n].reshape(1, d)

    ones = jnp.ones((1, d), F32)
    zeros = jnp.zeros((1, d), F32)

    h0 = jnp.concatenate([jnp.zeros((PAD, d), F32), meta_f, x[0]], axis=0)
    hb0 = _cast_t(h0, "h0_bf16_t")

    w1 = ffn_chunks(g1g, g1u, g1d)
    g00, b00 = gb(0, 0)
    xh1, rs1, hb1, gg1, uu1, g2g, g2u = _ffn_fwd(
        h0, ones, zeros, *w1, g00, b00, alpha, "ffn_fwd_0a", carry=gather_of([ffn2_wg[0], ffn2_wu[0]]))
    g01, b01 = gb(0, 1)
    xh2, rs2, hb2, pp, mb, g2d = _conv_fwd(
        xh1, g00, b00, w_in, cw_f, w_out, g01, b01, alpha, "conv_fwd", carry=gather_of([ffn2_wd[0]]))
    w2 = ffn_chunks(g2g, g2u, g2d)
    g02, b02 = gb(0, 2)
    xh3, rs3, hb3, gg3, uu3, gkv, g3g, g3u = _ffn_fwd(
        xh2, g01, b01, *w2, g02, b02, alpha, "ffn_fwd_0b", carry=gather_of([kv_w, ffn1_wg[1], ffn1_wu[1]]))
    kvw = _cols_from_devices(gkv)
    wk, wv = kvw[:, :d], kvw[:, d:2 * d]
    wf = jnp.pad(kvw[:, 2 * d:], ((0, 0), (0, LANES - N_HEADS)))
    kk, vv, logit, cc, cct, g3d = _kv_fwd(xh3, g02, b02, wk, wv, wf, fb, "kv_fwd",
                                          carry=gather_of([ffn1_wd[1]]))

    w3 = ffn_chunks(g3g, g3u, g3d)
    g10, b10 = gb(1, 0)
    xh4, rs4, hb4, gg4, uu4, gwq = _ffn_fwd(xh3, g02, b02, *w3, g10, b10, alpha, "ffn_fwd_1a",
                                             carry=gather_of([attn_w_q[0]]))
    w_q = gwq.reshape(d, d)
    qq = _proj(xh4, g10, b10, w_q, "q_proj")
    ot, lse, gwo, g4g, g4u, g4d = _attn_fwd(
        qq, kk, vv, cc, cct, "attn_fwd", carry=gather_of([attn_w_o[0], ffn2_wg[1], ffn2_wu[1], ffn2_wd[1]]))
    w_o = gwo.reshape(d, d)
    g11, b11 = gb(1, 1)
    xh5, rs5, hb5 = _attn_out_fwd(ot, xh4, g10, b10, w_o, g11, b11, alpha, "attn_out_fwd")
    w4 = ffn_chunks(g4g, g4u, g4d)
    g12, b12 = gb(1, 2)
    xh6, rs6, _, gg6, uu6 = _ffn_fwd(xh5, g11, b11, *w4, g12, b12, alpha, "ffn_fwd_1b")


    dgain = [[None] * 3 for _ in range(depth)]
    dbias = [[None] * 3 for _ in range(depth)]

    def to_col_owners(g):
        return (False, _cols_to_devices(g).astype(BF16))

    def to_row_owners(g):
        return (False, g.reshape(N_DEV, g.shape[0] // N_DEV, g.shape[1]).astype(BF16))

    dh5, do6, dg6, du6, a6, dgain[1][2], dbias[1][2], loss_l = _ffn_bwd(
        None, xh6, rs6, g12, gg6, uu6, *w4, alpha, "ffn_bwd_1b", loss_target=loss_target[0], loss_bias=b12)
    loss = lax.psum(loss_l[0, 0], ("x", "y", "c"))
    dw4g, dw4u = _wgrad(hb5, [dg6, du6], "wgrad_up_1b")
    (dw4dt,) = _wgrad(do6, [a6], "wgrad_down_1b")

    dres4, dmix5, dot_t, delta, dgain[1][1], dbias[1][1] = _attn_out_bwd(dh5, xh5, rs5, g11, ot, w_o, alpha, "attn_out_bwd")
    (dwo,) = _wgrad(ot, [dmix5], "wgrad_wo")
    dq, dkk, dvv, dcs, drow, l4g, l4u, l4d, lwo = _attn_bwd(
        qq, kk, vv, cc, cct, lse, delta, dot_t, "attn_bwd",
        carry=[to_col_owners(dw4g), to_col_owners(dw4u), to_row_owners(dw4dt.T), to_row_owners(dwo)])
    dh4 = _add_proj_nt(dres4, dq, w_q, "q_bwd")
    (dwq,) = _wgrad(hb4, [dq], "wgrad_wq")

    dh3a, do4, dg4, du4, a4, dgain[1][0], dbias[1][0] = _ffn_bwd(dh4, xh4, rs4, g10, gg4, uu4, *w3, alpha, "ffn_bwd_1a")
    dw3g, dw3u = _wgrad(hb3, [dg4, du4], "wgrad_up_1a")
    (dw3dt,) = _wgrad(do4, [a4], "wgrad_down_1a")

    dcq = jnp.pad(drow[:, :, 0, :].reshape(N_HEADS, t).T, ((0, 0), (0, LANES - N_HEADS)))
    dh3, dlogit, dfb = _kv_bwd(dkk, dvv, dcs, dcq, logit, dh3a, wk, wv, wf, "kv_bwd")
    dwk, dwv = _wgrad(hb3, [dkk, dvv], "wgrad_kv")
    (dwf,) = _wgrad(hb3, [dlogit], "wgrad_f")
    dkv = jnp.concatenate([dwk, dwv, dwf[:, :N_HEADS]], axis=1)

    dh2, do3, dg3, du3, a3, dgain[0][2], dbias[0][2], lwq, l3g, l3u, l3d, lkv = _ffn_bwd(
        dh3, xh3, rs3, g02, gg3, uu3, *w2, alpha, "ffn_bwd_0b",
        carry=[to_row_owners(dwq), to_col_owners(dw3g), to_col_owners(dw3u), to_row_owners(dw3dt.T),
               to_col_owners(dkv)])
    dw2g, dw2u = _wgrad(hb2, [dg3, du3], "wgrad_up_0b")
    (dw2dt,) = _wgrad(do3, [a3], "wgrad_down_0b")

    dh1, dmix2, dpp, dcw, dgain[0][1], dbias[0][1] = _conv_bwd(dh2, xh2, rs2, g01, pp, cw_f, w_in, w_out, alpha, "conv_bwd")
    (dwin,) = _wgrad(hb1, [dpp], "wgrad_conv_in")
    (dwout,) = _wgrad(mb, [dmix2], "wgrad_conv_out")

    dh0, do1, dg1, du1, a1, dgain[0][0], dbias[0][0], l2g, l2u, l2d, lcin, lcout = _ffn_bwd(
        dh1, xh1, rs1, g00, gg1, uu1, *w1, alpha, "ffn_bwd_0a",
        carry=[to_col_owners(dw2g), to_col_owners(dw2u), to_row_owners(dw2dt.T), to_col_owners(dwin),
               to_row_owners(dwout)])
    (dw1dt,) = _wgrad(do1, [a1], "wgrad_down_0a")
    dw1g, l1d = _wgrad(hb0, [dg1], "wgrad_upg_0a", carry=[to_row_owners(dw1dt.T)])
    dw1u, l1g = _wgrad(hb0, [du1], "wgrad_upu_0a", carry=[to_col_owners(dw1g)])
    dmeta = dh0[PAD:ROW0]
    dgain_f = jnp.stack([jnp.concatenate(r, axis=0) for r in dgain])
    dbias_f = jnp.stack([jnp.concatenate(r, axis=0) for r in dbias])
    small_full = [dmeta, dgain_f, dbias_f, dcw[None], dfb]
    small_full_shapes = [a.shape for a in small_full]
    l1u, gsmall_grads = _exchange([to_col_owners(dw1u), (True, _pack_rows(small_full, WIDTH, F32_ROWS))], "rs_last")

    grad_x = dh0[ROW0:].reshape(1, seq, d)
    rsmall = _sum_sources(gsmall_grads, "small_sum")
    smeta, sgain, sbias, scw, sfb = _unpack_rows(rsmall, small_full_shapes, WIDTH, F32_ROWS)
    csh = d // N_DEV

    def my_cols(a):
        return lax.dynamic_slice_in_dim(a, me * csh, csh, axis=a.ndim - 1)

    grads = {"meta": my_cols(smeta), "ln_gain": my_cols(sgain), "ln_bias": my_cols(sbias),
             "conv_w": my_cols(scw), "f_bias": sfb[0, :N_HEADS]}
    landed = {"ffn1_wg": [l1g, l3g], "ffn1_wu": [l1u, l3u], "ffn1_wd": [l1d, l3d],
              "ffn2_wg": [l2g, l4g], "ffn2_wu": [l2u, l4u], "ffn2_wd": [l2d, l4d],
              "conv_w_in": [lcin], "conv_w_out": [lcout], "kv_w": [lkv], "attn_w_q": [lwq], "attn_w_o": [lwo]}
    weights = dict(meta=meta, ffn1_wg=ffn1_wg, ffn1_wu=ffn1_wu, ffn1_wd=ffn1_wd, ffn2_wg=ffn2_wg,
                   ffn2_wu=ffn2_wu, ffn2_wd=ffn2_wd, ln_gain=ln_gain, ln_bias=ln_bias,
                   conv_w_in=conv_w_in, conv_w=conv_w, conv_w_out=conv_w_out, kv_w=kv_w,
                   f_bias=f_bias, attn_w_q=attn_w_q, attn_w_o=attn_w_o)
    moms = dict(meta=(m_meta, v_meta), ffn1_wg=(m_ffn1_wg, v_ffn1_wg), ffn1_wu=(m_ffn1_wu, v_ffn1_wu),
                ffn1_wd=(m_ffn1_wd, v_ffn1_wd), ffn2_wg=(m_ffn2_wg, v_ffn2_wg), ffn2_wu=(m_ffn2_wu, v_ffn2_wu),
                ffn2_wd=(m_ffn2_wd, v_ffn2_wd), ln_gain=(m_ln_gain, v_ln_gain), ln_bias=(m_ln_bias, v_ln_bias),
                conv_w_in=(m_conv_w_in, v_conv_w_in), conv_w=(m_conv_w, v_conv_w),
                conv_w_out=(m_conv_w_out, v_conv_w_out), kv_w=(m_kv_w, v_kv_w), f_bias=(m_f_bias, v_f_bias),
                attn_w_q=(m_attn_w_q, v_attn_w_q), attn_w_o=(m_attn_w_o, v_attn_w_o))

    names = list(weights)
    g_out, d_out, m_out, v_out = [], [], [], []
    for n in names:
        w = weights[n]
        shp = w.shape
        mm, vv_ = moms[n]
        if n in landed:
            three = (len(landed[n]),) + shp[-2:]
            g, dl, nm, nv = _reduce_adamw(w.reshape(three), mm.reshape(three), vv_.reshape(three),
                                          landed[n], "adamw_" + n)
            g = g.reshape(shp)
        else:
            two = (1, shp[0]) if w.ndim == 1 else (math.prod(shp[:-1]), shp[-1])
            g = grads[n].reshape(shp)
            dl, nm, nv = _adamw(w.reshape(two), g.reshape(two), mm.reshape(two), vv_.reshape(two), "adamw_" + n)
        g_out.append(g)
        d_out.append(dl.reshape(shp))
        m_out.append(nm.reshape(shp))
        v_out.append(nv.reshape(shp))
    return (loss, grad_x, *g_out, *d_out, *m_out, *v_out)
```

```python
import functools
import math

import jax
import jax.numpy as jnp
from jax import lax
from jax.experimental import pallas as pl
from jax.experimental.pallas import tpu as pltpu

F32 = jnp.float32
BF16 = jnp.bfloat16

N_DEV = 8
N_HEADS = 8
N_META = 16
PAD = 112
ROW0 = PAD + N_META
LN_EPS = 1e-5
NEG_INF = -1e30
LOG2E = 1.4426950408889634
ATTN_HEADS_PER_STEP = 8
ATTN_BWD_HEADS_PER_STEP = 2
LANES = 128
MXU_COLS = 256
FFN_FWD_CHUNKS = 11
FFN_BWD_CHUNKS = 4

ADAM_LR = 0.001
ADAM_B1 = 0.9
ADAM_B2 = 0.999
ADAM_EPS = 1e-08
ADAM_WD = 0.01
ADAM_STEP = 10

ROW_TILES = (640, 128)
LOSS_TILE = 128
BF16_ROWS = 16
F32_ROWS = 8
SUM_ROWS_MAX = 768
ADAM_ROWS_MAX = 256
VMEM_BIG = 56 << 20
VMEM_MID = 40 << 20

ANY = pl.BlockSpec(memory_space=pl.ANY)
MESH = pl.DeviceIdType.MESH


def _row_tile(t):
    for c in ROW_TILES:
        if t % c == 0:
            return c
    raise ValueError(f"no row tile for {t}")


def _dot(a, b):
    return jnp.dot(a, b, preferred_element_type=F32)


def _dot_nt(a, b):
    return lax.dot_general(a, b, (((1,), (1,)), ((), ())), preferred_element_type=F32)


def _dot_tn(a, b):
    return lax.dot_general(a, b, (((0,), (0,)), ((), ())), preferred_element_type=F32)


def _params(sem, vmem):
    return pltpu.CompilerParams(dimension_semantics=sem, vmem_limit_bytes=vmem)


def _ln_fwd(z):
    mu = jnp.mean(z, axis=-1, keepdims=True)
    zc = z - mu
    var = jnp.mean(zc * zc, axis=-1, keepdims=True)
    rstd = lax.rsqrt(var + LN_EPS)
    return zc * rstd, rstd


def _ln_bwd(dh, xhat, rstd, gain):
    dxh = dh * gain
    m1 = jnp.mean(dxh, axis=-1, keepdims=True)
    m2 = jnp.mean(dxh * xhat, axis=-1, keepdims=True)
    dz = rstd * (dxh - m1 - xhat * m2)
    return dz, jnp.sum(dh * xhat, axis=0, keepdims=True), jnp.sum(dh, axis=0, keepdims=True)


def _load_resident(pairs, sems):
    cps = [pltpu.make_async_copy(src, dst, sems.at[k]) for k, (src, dst) in enumerate(pairs)]
    for cp in cps:
        cp.start()
    for cp in cps:
        cp.wait()


def _peer_ids():
    mx, my, mc = lax.axis_index("x"), lax.axis_index("y"), lax.axis_index("c")
    peers = []
    for kk in range(1, N_DEV):
        px = 1 - mx if (kk >> 2) & 1 else mx
        py = 1 - my if (kk >> 1) & 1 else my
        pc = 1 - mc if kk & 1 else mc
        peers.append(((px, py, pc), 4 * px + 2 * py + pc))
    return 4 * mx + 2 * my + mc, peers


def _exchange_copies(jobs, send_sems, recv_sems, local_sems, starting):
    me_id, peers = _peer_ids()
    for n, (gather, src, dst) in enumerate(jobs):
        own = pltpu.make_async_copy(src if gather else src.at[me_id], dst.at[me_id], local_sems.at[n])
        own.start() if starting else own.wait()
        for k, (dev, pid) in enumerate(peers):
            sem = (N_DEV - 1) * n + k
            out = src if gather else src.at[pid]
            send = pltpu.make_async_remote_copy(
                src_ref=out, dst_ref=dst.at[me_id], send_sem=send_sems.at[sem], recv_sem=recv_sems.at[sem],
                device_id=dev, device_id_type=MESH)
            if starting:
                send.start()
            else:
                pltpu.make_async_remote_copy(
                    src_ref=out, dst_ref=dst.at[pid], send_sem=send_sems.at[sem], recv_sem=recv_sems.at[sem],
                    device_id=dev, device_id_type=MESH).wait_recv()
                send.wait_send()


def _carried(body, n_in, n_out, carry, first, last):
    nj = len(carry)
    if nj == 0:
        return body

    def wrapped(*refs):
        ins, srcs = refs[:n_in], refs[n_in:n_in + nj]
        outs = refs[n_in + nj:n_in + nj + n_out]
        dsts = refs[n_in + nj + n_out:n_in + 2 * nj + n_out]
        scratch, sems = refs[n_in + 2 * nj + n_out:-3], refs[-3:]
        jobs = [(g, s, r) for (g, _), s, r in zip(carry, srcs, dsts)]

        @pl.when(first())
        def _():
            _exchange_copies(jobs, *sems, starting=True)

        body(*ins, *outs, *scratch)

        @pl.when(last())
        def _():
            _exchange_copies(jobs, *sems, starting=False)

    return wrapped


def _carry_shapes(carry):
    return [jax.ShapeDtypeStruct((N_DEV,) + a.shape if g else a.shape, a.dtype) for g, a in carry]


def _carry_scratch(carry):
    if not carry:
        return []
    n = len(carry)
    return [pltpu.SemaphoreType.DMA(((N_DEV - 1) * n,)), pltpu.SemaphoreType.DMA(((N_DEV - 1) * n,)),
            pltpu.SemaphoreType.DMA((n,))]


def _exchange(carry, name):
    n = len(carry)

    def body(*refs):
        jobs = [(g, s, r) for (g, _), s, r in zip(carry, refs[:n], refs[n:2 * n])]
        _exchange_copies(jobs, *refs[2 * n:], starting=True)
        _exchange_copies(jobs, *refs[2 * n:], starting=False)

    return pl.pallas_call(
        body, name=name, in_specs=[ANY] * n, out_specs=[ANY] * n, out_shape=_carry_shapes(carry),
        scratch_shapes=_carry_scratch(carry),
    )(*[a for _, a in carry])


def _ffn_fwd(xh, gi, bi, wg, wu, wd, go, bo, alpha, name, carry=()):
    t, d = xh.shape
    nch, _, fc = wg.shape
    f = nch * fc
    per = min(FFN_FWD_CHUNKS, nch)
    nc = -(-nch // per)
    tm = _row_tile(t)
    nt = t // tm

    def body(xh_ref, gi_ref, bi_ref, wg_hbm, wu_hbm, wd_hbm, go_ref, bo_ref,
             xo_ref, rs_ref, hb_ref, g_ref, u_ref,
             wg_v, wu_v, wd_v, acc, hbs, sems):
        i = pl.program_id(0)
        c = pl.program_id(1)

        @pl.when((i == 0) & (c == 0))
        def _():
            _load_resident([(wg_hbm, wg_v), (wu_hbm, wu_v), (wd_hbm, wd_v)], sems)

        @pl.when(c == 0)
        def _():
            h = xh_ref[...] * gi_ref[...] + bi_ref[...]
            hbs[...] = h.astype(BF16)
            acc[...] = jnp.zeros_like(acc)

        def chunk(k):
            ck = c * per + k
            cols = slice(k * fc, (k + 1) * fc)
            hb = hbs[...]
            g = _dot(hb, wg_v[ck])
            u = _dot(hb, wu_v[ck])
            a = (g * jax.nn.sigmoid(g)) * u
            g_ref[:, cols] = g.astype(BF16)
            u_ref[:, cols] = u.astype(BF16)
            acc[...] += _dot(a.astype(BF16), wd_v[ck])

        for k in range(per):
            if (nc - 1) * per + k < nch:
                chunk(k)
            else:
                pl.when(c * per + k < nch)(functools.partial(chunk, k))

        @pl.when(c == nc - 1)
        def _():
            h = xh_ref[...] * gi_ref[...] + bi_ref[...]
            xhat, rstd = _ln_fwd(alpha * h + 0.5 * acc[...])
            xo_ref[...] = xhat
            rs_ref[...] = rstd
            hb_ref[...] = (xhat * go_ref[...] + bo_ref[...]).astype(BF16).T

    row = pl.BlockSpec((tm, d), lambda i, c: (i, 0))
    vec = pl.BlockSpec((1, d), lambda i, c: (0, 0))
    chunk = pl.BlockSpec((tm, per * fc), lambda i, c: (i, c))
    first = lambda: (pl.program_id(0) == 0) & (pl.program_id(1) == 0)
    last = lambda: (pl.program_id(0) == nt - 1) & (pl.program_id(1) == nc - 1)
    return pl.pallas_call(
        _carried(body, 8, 5, carry, first, last), name=name, grid=(nt, nc),
        in_specs=[row, vec, vec, ANY, ANY, ANY, vec, vec] + [ANY] * len(carry),
        out_specs=[row, pl.BlockSpec((tm, 1), lambda i, c: (i, 0)),
                   pl.BlockSpec((d, tm), lambda i, c: (0, i)), chunk, chunk] + [ANY] * len(carry),
        out_shape=[jax.ShapeDtypeStruct((t, d), F32), jax.ShapeDtypeStruct((t, 1), F32),
                   jax.ShapeDtypeStruct((d, t), BF16), jax.ShapeDtypeStruct((t, f), BF16),
                   jax.ShapeDtypeStruct((t, f), BF16)] + _carry_shapes(carry),
        scratch_shapes=[pltpu.VMEM((nch, d, fc), BF16), pltpu.VMEM((nch, d, fc), BF16),
                        pltpu.VMEM((nch, fc, d), BF16), pltpu.VMEM((tm, d), F32),
                        pltpu.VMEM((tm, d), BF16), pltpu.SemaphoreType.DMA((3,))] + _carry_scratch(carry),
        compiler_params=_params(("arbitrary", "arbitrary"), VMEM_BIG),
    )(xh, gi, bi, wg, wu, wd, go, bo, *[a for _, a in carry])


def _ffn_bwd(dh, xo, rs, go, gs, us, wg, wu, wd, alpha, name, carry=(), loss_target=None, loss_bias=None):
    t, d = xo.shape
    nch, _, fc = wg.shape
    f = nch * fc
    per = min(FFN_BWD_CHUNKS, nch)
    nc = -(-nch // per)
    tm = _row_tile(t)
    nt = t // tm

    with_loss = loss_target is not None
    nsub, lead = tm // LOSS_TILE, ROW0 // LOSS_TILE
    nlead = nsub + 1 if with_loss else 1

    def body(*refs):
        lead_refs = refs[:nlead]
        xo_ref, rs_ref, go_ref, g_ref, u_ref, wg_hbm, wu_hbm, wd_hbm = refs[nlead:nlead + 8]
        dhin_ref, dot_ref, dg_ref, du_ref, a_ref, dgain_ref, dbias_ref = refs[nlead + 8:nlead + 15]
        rest = refs[nlead + 15:]
        loss_ref, rest = (rest[0], rest[1:]) if with_loss else (None, rest)
        wg_v, wu_v, wd_v, do_ref, sems = rest[:5]
        i = pl.program_id(0)
        c = pl.program_id(1)

        @pl.when((i == 0) & (c == 0))
        def _():
            _load_resident([(wg_hbm, wg_v), (wu_hbm, wu_v), (wd_hbm, wd_v)], sems)
            dgain_ref[...] = jnp.zeros_like(dgain_ref)
            dbias_ref[...] = jnp.zeros_like(dbias_ref)
            if with_loss:
                rest[5][...] = jnp.zeros_like(rest[5])

        def tile_dh():
            if not with_loss:
                return lead_refs[0][...]
            part = rest[5]
            for k in range(nsub):
                sl = slice(k * LOSS_TILE, (k + 1) * LOSS_TILE)
                rows = i * tm + k * LOSS_TILE + lax.broadcasted_iota(jnp.int32, (LOSS_TILE, 1), 0)
                y = xo_ref[sl, :] * go_ref[...] + lead_refs[nsub][...]
                e = jnp.where(rows >= ROW0, y - lead_refs[k][...], 0.0)
                part[...] += jnp.sum(e * e, axis=0, keepdims=True)
                dhin_ref[sl, :] = e * (1.0 / d)

            @pl.when(i == nt - 1)
            def _():
                loss_ref[...] = jnp.full((1, LANES), 0.5 / d, F32) * jnp.sum(part[...])

            return dhin_ref[...]

        @pl.when(c == 0)
        def _():
            dz, dgp, dbp = _ln_bwd(tile_dh(), xo_ref[...], rs_ref[...], go_ref[...])
            dgain_ref[...] += dgp
            dbias_ref[...] += dbp
            dob = (0.5 * dz).astype(BF16)
            do_ref[...] = dob
            dot_ref[...] = dob.T
            dhin_ref[...] = alpha * dz

        def chunk(k):
            ck = c * per + k
            cols = slice(k * fc, (k + 1) * fc)
            g = g_ref[:, cols].astype(F32)
            u = u_ref[:, cols].astype(F32)
            sg = jax.nn.sigmoid(g)
            sl = g * sg
            da = _dot_nt(do_ref[...], wd_v[ck])
            dgb = (da * u * (sg * (1.0 + g * (1.0 - sg)))).astype(BF16)
            dub = (da * sl).astype(BF16)
            a_ref[:, cols] = (sl * u).astype(BF16)
            dg_ref[:, cols] = dgb
            du_ref[:, cols] = dub
            dhin_ref[...] += _dot_nt(dgb, wg_v[ck]) + _dot_nt(dub, wu_v[ck])

        for k in range(per):
            if (nc - 1) * per + k < nch:
                chunk(k)
            else:
                pl.when(c * per + k < nch)(functools.partial(chunk, k))

    row = pl.BlockSpec((tm, d), lambda i, c: (i, 0))
    vec = pl.BlockSpec((1, d), lambda i, c: (0, 0))
    chunk = pl.BlockSpec((tm, per * fc), lambda i, c: (i, c))
    first = lambda: (pl.program_id(0) == 0) & (pl.program_id(1) == 0)
    last = lambda: (pl.program_id(0) == nt - 1) & (pl.program_id(1) == nc - 1)
    if with_loss:
        lead_specs = [pl.BlockSpec((LOSS_TILE, d), lambda i, c, k=k: (jnp.maximum(i * nsub + k - lead, 0), 0))
                      for k in range(nsub)] + [vec]
        lead_args = [loss_target] * nsub + [loss_bias]
        loss_spec, loss_shape = [pl.BlockSpec((1, LANES), lambda i, c: (0, 0))], [jax.ShapeDtypeStruct((1, LANES), F32)]
        loss_scratch = [pltpu.VMEM((1, d), F32)]
    else:
        lead_specs, lead_args, loss_spec, loss_shape, loss_scratch = [row], [dh], [], [], []
    return pl.pallas_call(
        _carried(body, nlead + 8, 7 + len(loss_spec), carry, first, last), name=name, grid=(nt, nc),
        in_specs=lead_specs + [row, pl.BlockSpec((tm, 1), lambda i, c: (i, 0)), vec, chunk, chunk,
                               ANY, ANY, ANY] + [ANY] * len(carry),
        out_specs=[row, pl.BlockSpec((d, tm), lambda i, c: (0, i)), chunk, chunk, chunk, vec, vec]
                  + loss_spec + [ANY] * len(carry),
        out_shape=[jax.ShapeDtypeStruct((t, d), F32), jax.ShapeDtypeStruct((d, t), BF16),
                   jax.ShapeDtypeStruct((t, f), BF16), jax.ShapeDtypeStruct((t, f), BF16),
                   jax.ShapeDtypeStruct((t, f), BF16), jax.ShapeDtypeStruct((1, d), F32),
                   jax.ShapeDtypeStruct((1, d), F32)] + loss_shape + _carry_shapes(carry),
        scratch_shapes=[pltpu.VMEM((nch, d, fc), BF16), pltpu.VMEM((nch, d, fc), BF16),
                        pltpu.VMEM((nch, fc, d), BF16), pltpu.VMEM((tm, d), BF16),
                        pltpu.SemaphoreType.DMA((3,))] + loss_scratch + _carry_scratch(carry),
        compiler_params=_params(("arbitrary", "arbitrary"), VMEM_BIG),
    )(*lead_args, xo, rs, go, gs, us, wg, wu, wd, *[a for _, a in carry])


def _wgrad(xt, ys, name, carry=()):
    m, t = xt.shape
    n = ys[0].shape[1]
    tn = min(n, MXU_COLS)
    ny = len(ys)

    def body(*refs):
        x_hbm = refs[0]
        y_refs = refs[1:1 + ny]
        o_refs = refs[1 + ny:1 + 2 * ny]
        xv, sems = refs[1 + 2 * ny:]

        @pl.when(pl.program_id(0) == 0)
        def _():
            _load_resident([(x_hbm, xv)], sems)

        for y_ref, o_ref in zip(y_refs, o_refs):
            o_ref[...] = _dot(xv[...], y_ref[...].astype(BF16)).astype(BF16)

    steps = n // tn
    first = lambda: pl.program_id(0) == 0
    last = lambda: pl.program_id(0) == steps - 1
    return pl.pallas_call(
        _carried(body, 1 + ny, ny, carry, first, last), name=name, grid=(steps,),
        in_specs=[ANY] + [pl.BlockSpec((t, tn), lambda c: (0, c)) for _ in ys] + [ANY] * len(carry),
        out_specs=[pl.BlockSpec((m, tn), lambda c: (0, c)) for _ in ys] + [ANY] * len(carry),
        out_shape=[jax.ShapeDtypeStruct((m, n), BF16) for _ in ys] + _carry_shapes(carry),
        scratch_shapes=[pltpu.VMEM((m, t), BF16), pltpu.SemaphoreType.DMA((1,))] + _carry_scratch(carry),
        compiler_params=_params(("arbitrary",), VMEM_BIG),
    )(xt, *ys, *[a for _, a in carry])


def _cast_t(h, name):
    t, d = h.shape
    tm = _row_tile(t)

    def body(h_ref, o_ref):
        o_ref[...] = h_ref[...].astype(BF16).T

    return pl.pallas_call(
        body, name=name, grid=(t // tm,),
        in_specs=[pl.BlockSpec((tm, d), lambda i: (i, 0))],
        out_specs=pl.BlockSpec((d, tm), lambda i: (0, i)),
        out_shape=jax.ShapeDtypeStruct((d, t), BF16),
        compiler_params=_params(("arbitrary",), VMEM_MID),
    )(h)


def _shift_rows(u, halo, tm):
    r = lax.broadcasted_iota(jnp.int32, (tm, 1), 0)
    u1 = jnp.where(r == 0, halo[7:8], pltpu.roll(u, 1, 0))
    u2 = jnp.where(r == 0, halo[6:7], jnp.where(r == 1, halo[7:8], pltpu.roll(u, 2, 0)))
    return u1, u2


def _conv_fwd(xh, gi, bi, w_in, cw, w_out, go, bo, alpha, name, carry=()):
    t, d = xh.shape
    tm = _row_tile(t)
    nt = t // tm

    def body(xh_ref, gi_ref, bi_ref, win_ref, cw_ref, wout_ref, go_ref, bo_ref,
             xo_ref, rs_ref, hb_ref, p_ref, m_ref, halo):
        i = pl.program_id(0)

        @pl.when(i == 0)
        def _():
            halo[...] = jnp.zeros_like(halo)

        h = xh_ref[...] * gi_ref[...] + bi_ref[...]
        hb = h.astype(BF16)
        bg = _dot(hb, win_ref[:, 0:d])
        cg = _dot(hb, win_ref[:, d:2 * d])
        val = _dot(hb, win_ref[:, 2 * d:3 * d])
        p_ref[:, 0:d] = bg.astype(BF16)
        p_ref[:, d:2 * d] = cg.astype(BF16)
        p_ref[:, 2 * d:3 * d] = val.astype(BF16)
        rows = i * tm + lax.broadcasted_iota(jnp.int32, (tm, 1), 0)
        u = jnp.where(rows >= PAD, cg * val, 0.0)
        u1, u2 = _shift_rows(u, halo[...], tm)
        halo[...] = u[tm - 8:tm]
        y = cw_ref[0:1] * u2 + cw_ref[1:2] * u1 + cw_ref[2:3] * u
        mb = (bg * y).astype(BF16)
        m_ref[...] = mb.T
        xhat, rstd = _ln_fwd(alpha * h + _dot(mb, wout_ref[...]))
        xo_ref[...] = xhat
        rs_ref[...] = rstd
        hb_ref[...] = (xhat * go_ref[...] + bo_ref[...]).astype(BF16).T

    row = pl.BlockSpec((tm, d), lambda i: (i, 0))
    col = pl.BlockSpec((d, tm), lambda i: (0, i))
    vec = pl.BlockSpec((1, d), lambda i: (0, 0))
    first = lambda: pl.program_id(0) == 0
    last = lambda: pl.program_id(0) == nt - 1
    return pl.pallas_call(
        _carried(body, 8, 5, carry, first, last), name=name, grid=(nt,),
        in_specs=[row, vec, vec, pl.BlockSpec((d, 3 * d), lambda i: (0, 0)),
                  pl.BlockSpec((3, d), lambda i: (0, 0)), pl.BlockSpec((d, d), lambda i: (0, 0)),
                  vec, vec] + [ANY] * len(carry),
        out_specs=[row, pl.BlockSpec((tm, 1), lambda i: (i, 0)), col,
                   pl.BlockSpec((tm, 3 * d), lambda i: (i, 0)), col] + [ANY] * len(carry),
        out_shape=[jax.ShapeDtypeStruct((t, d), F32), jax.ShapeDtypeStruct((t, 1), F32),
                   jax.ShapeDtypeStruct((d, t), BF16), jax.ShapeDtypeStruct((t, 3 * d), BF16),
                   jax.ShapeDtypeStruct((d, t), BF16)] + _carry_shapes(carry),
        scratch_shapes=[pltpu.VMEM((8, d), F32)] + _carry_scratch(carry),
        compiler_params=_params(("arbitrary",), VMEM_BIG),
    )(xh, gi, bi, w_in, cw, w_out, go, bo, *[a for _, a in carry])


def _conv_bwd(dh, xo, rs, go, p, cw, w_in, w_out, alpha, name):
    t, d = dh.shape
    tm = _row_tile(t)
    nt = t // tm
    tb = tm // 8

    def body(dh_ref, xo_ref, rs_ref, go_ref, p_ref, ph_ref, cw_ref, win_ref, wout_ref,
             dhin_ref, dmix_ref, dp_ref, dcw_ref, dgain_ref, dbias_ref, carry):
        i = pl.program_id(0)
        tile = nt - 1 - i

        @pl.when(i == 0)
        def _():
            carry[...] = jnp.zeros_like(carry)
            dcw_ref[...] = jnp.zeros_like(dcw_ref)
            dgain_ref[...] = jnp.zeros_like(dgain_ref)
            dbias_ref[...] = jnp.zeros_like(dbias_ref)

        dz, dgp, dbp = _ln_bwd(dh_ref[...], xo_ref[...], rs_ref[...], go_ref[...])
        dgain_ref[...] += dgp
        dbias_ref[...] += dbp
        dmixb = dz.astype(BF16)
        dmix_ref[...] = dmixb
        dm = _dot_nt(dmixb, wout_ref[...])

        bg = p_ref[:, 0:d].astype(F32)
        cg = p_ref[:, d:2 * d].astype(F32)
        val = p_ref[:, 2 * d:3 * d].astype(F32)
        rows = tile * tm + lax.broadcasted_iota(jnp.int32, (tm, 1), 0)
        valid = rows >= PAD
        u = jnp.where(valid, cg * val, 0.0)
        hrows = tile * tm - 8 + lax.broadcasted_iota(jnp.int32, (8, 1), 0)
        hu = jnp.where((hrows >= PAD) & (tile > 0),
                       ph_ref[:, d:2 * d].astype(F32) * ph_ref[:, 2 * d:3 * d].astype(F32), 0.0)
        u1, u2 = _shift_rows(u, hu, tm)
        w0, w1, w2 = cw_ref[0:1], cw_ref[1:2], cw_ref[2:3]
        y = w0 * u2 + w1 * u1 + w2 * u
        dbg = dm * y
        dy = dm * bg
        dcw_ref[0:1] += jnp.sum(dy * u2, axis=0, keepdims=True)
        dcw_ref[1:2] += jnp.sum(dy * u1, axis=0, keepdims=True)
        dcw_ref[2:3] += jnp.sum(dy * u, axis=0, keepdims=True)

        nxt = carry[...]
        r = lax.broadcasted_iota(jnp.int32, (tm, 1), 0)
        dy1 = jnp.where(r == tm - 1, nxt[0:1], pltpu.roll(dy, tm - 1, 0))
        dy2 = jnp.where(r == tm - 2, nxt[0:1],
                        jnp.where(r == tm - 1, nxt[1:2], pltpu.roll(dy, tm - 2, 0)))
        carry[...] = dy[0:8]
        du = jnp.where(valid, w2 * dy + w1 * dy1 + w0 * dy2, 0.0)
        dbgb = dbg.astype(BF16)
        dcgb = (du * val).astype(BF16)
        dvalb = (du * cg).astype(BF16)
        dp_ref[:, 0:d] = dbgb
        dp_ref[:, d:2 * d] = dcgb
        dp_ref[:, 2 * d:3 * d] = dvalb
        dhin_ref[...] = (alpha * dz + _dot_nt(dbgb, win_ref[:, 0:d])
                         + _dot_nt(dcgb, win_ref[:, d:2 * d]) + _dot_nt(dvalb, win_ref[:, 2 * d:3 * d]))

    row = pl.BlockSpec((tm, d), lambda i: (nt - 1 - i, 0))
    vec = pl.BlockSpec((1, d), lambda i: (0, 0))
    prow = pl.BlockSpec((tm, 3 * d), lambda i: (nt - 1 - i, 0))
    return pl.pallas_call(
        body, name=name, grid=(nt,),
        in_specs=[row, row, pl.BlockSpec((tm, 1), lambda i: (nt - 1 - i, 0)), vec, prow,
                  pl.BlockSpec((8, 3 * d), lambda i: (jnp.maximum((nt - 1 - i) * tb - 1, 0), 0)),
                  pl.BlockSpec((3, d), lambda i: (0, 0)),
                  pl.BlockSpec((d, 3 * d), lambda i: (0, 0)), pl.BlockSpec((d, d), lambda i: (0, 0))],
        out_specs=[row, row, prow, pl.BlockSpec((3, d), lambda i: (0, 0)), vec, vec],
        out_shape=[jax.ShapeDtypeStruct((t, d), F32), jax.ShapeDtypeStruct((t, d), BF16),
                   jax.ShapeDtypeStruct((t, 3 * d), BF16), jax.ShapeDtypeStruct((3, d), F32),
                   jax.ShapeDtypeStruct((1, d), F32), jax.ShapeDtypeStruct((1, d), F32)],
        scratch_shapes=[pltpu.VMEM((8, d), F32)],
        compiler_params=_params(("arbitrary",), VMEM_BIG),
    )(dh, xo, rs, go, p, p, cw, w_in, w_out)


def _kv_fwd(xh, gi, bi, wk, wv, wf, fb, name, carry=()):
    t, d = xh.shape
    tm = _row_tile(t)
    nt = t // tm

    def body(xh_ref, gi_ref, bi_ref, wk_ref, wv_ref, wf_ref, fb_ref,
             k_ref, v_ref, lg_ref, c_ref, ct_ref, run):
        i = pl.program_id(0)

        @pl.when(i == 0)
        def _():
            run[...] = jnp.zeros_like(run)

        x = (xh_ref[...] * gi_ref[...] + bi_ref[...]).astype(BF16)
        k_ref[...] = _dot(x, wk_ref[...]).astype(BF16)
        v_ref[...] = _dot(x, wv_ref[...]).astype(BF16)
        logit = _dot(x, wf_ref[...]) + fb_ref[...]
        lg_ref[...] = logit
        logf = jnp.minimum(logit, 0.0) - jnp.log(1.0 + jnp.exp(-jnp.abs(logit)))
        rows = i * tm + lax.broadcasted_iota(jnp.int32, (tm, 1), 0)
        logf = jnp.where(rows >= PAD, logf, 0.0)
        tri = (lax.broadcasted_iota(jnp.int32, (tm, tm), 0)
               >= lax.broadcasted_iota(jnp.int32, (tm, tm), 1)).astype(F32)
        cs = jnp.dot(tri, logf, precision=lax.Precision.HIGHEST, preferred_element_type=F32) + run[...]
        run[...] = cs[tm - 1:tm]
        c_ref[...] = cs
        ct_ref[...] = cs.T

    row = pl.BlockSpec((tm, d), lambda i: (i, 0))
    vec = pl.BlockSpec((1, d), lambda i: (0, 0))
    gate = pl.BlockSpec((tm, LANES), lambda i: (i, 0))
    sq = pl.BlockSpec((d, d), lambda i: (0, 0))
    first = lambda: pl.program_id(0) == 0
    last = lambda: pl.program_id(0) == nt - 1
    return pl.pallas_call(
        _carried(body, 7, 5, carry, first, last), name=name, grid=(nt,),
        in_specs=[row, vec, vec, sq, sq, pl.BlockSpec((d, LANES), lambda i: (0, 0)),
                  pl.BlockSpec((1, LANES), lambda i: (0, 0))] + [ANY] * len(carry),
        out_specs=[row, row, gate, gate, pl.BlockSpec((LANES, tm), lambda i: (0, i))] + [ANY] * len(carry),
        out_shape=[jax.ShapeDtypeStruct((t, d), BF16), jax.ShapeDtypeStruct((t, d), BF16),
                   jax.ShapeDtypeStruct((t, LANES), F32), jax.ShapeDtypeStruct((t, LANES), F32),
                   jax.ShapeDtypeStruct((LANES, t), F32)] + _carry_shapes(carry),
        scratch_shapes=[pltpu.VMEM((1, LANES), F32)] + _carry_scratch(carry),
        compiler_params=_params(("arbitrary",), VMEM_MID),
    )(xh, gi, bi, wk, wv, wf, fb, *[a for _, a in carry])


def _kv_bwd(dk, dv, dcs, dcq, logit, dh_other, wk, wv, wf, name):
    t, d = dk.shape
    tm = _row_tile(t)
    nt = t // tm

    def body(dk_ref, dv_ref, dcs_ref, dcq_ref, lg_ref, oth_ref, wk_ref, wv_ref, wf_ref,
             dh_ref, dl_ref, dfb_ref, run):
        i = pl.program_id(0)
        tile = nt - 1 - i

        @pl.when(i == 0)
        def _():
            run[...] = jnp.zeros_like(run)
            dfb_ref[...] = jnp.zeros_like(dfb_ref)

        lane = lax.broadcasted_iota(jnp.int32, (tm, LANES), 1)
        dc = dcq_ref[...]
        for hh in range(N_HEADS):
            dc = dc + jnp.where(lane == hh, jnp.sum(dcs_ref[hh], axis=1, keepdims=True), 0.0)
        tri = (lax.broadcasted_iota(jnp.int32, (tm, tm), 0)
               <= lax.broadcasted_iota(jnp.int32, (tm, tm), 1)).astype(F32)
        dlf = jnp.dot(tri, dc, precision=lax.Precision.HIGHEST, preferred_element_type=F32) + run[...]
        run[...] = dlf[0:1]
        rows = tile * tm + lax.broadcasted_iota(jnp.int32, (tm, 1), 0)
        dlogit = jnp.where(rows >= PAD, dlf * jax.nn.sigmoid(-lg_ref[...]), 0.0)
        dfb_ref[...] += jnp.sum(dlogit, axis=0, keepdims=True)
        dlb = dlogit.astype(BF16)
        dl_ref[...] = dlb
        dh_ref[...] = (oth_ref[...] + _dot_nt(dk_ref[...], wk_ref[...])
                       + _dot_nt(dv_ref[...], wv_ref[...]) + _dot_nt(dlb, wf_ref[...]))

    row = pl.BlockSpec((tm, d), lambda i: (nt - 1 - i, 0))
    gate = pl.BlockSpec((tm, LANES), lambda i: (nt - 1 - i, 0))
    sq = pl.BlockSpec((d, d), lambda i: (0, 0))
    return pl.pallas_call(
        body, name=name, grid=(nt,),
        in_specs=[row, row, pl.BlockSpec((N_HEADS, tm, LANES), lambda i: (0, nt - 1 - i, 0)), gate, gate, row,
                  sq, sq, pl.BlockSpec((d, LANES), lambda i: (0, 0))],
        out_specs=[row, gate, pl.BlockSpec((1, LANES), lambda i: (0, 0))],
        out_shape=[jax.ShapeDtypeStruct((t, d), F32), jax.ShapeDtypeStruct((t, LANES), BF16),
                   jax.ShapeDtypeStruct((1, LANES), F32)],
        scratch_shapes=[pltpu.VMEM((1, LANES), F32)],
        compiler_params=_params(("arbitrary",), VMEM_MID),
    )(dk, dv, dcs, dcq, logit, dh_other, wk, wv, wf)


def _proj(xh, gi, bi, w, name):
    t, k = xh.shape
    n = w.shape[1]
    tm = _row_tile(t)

    def body(x_ref, g_ref, b_ref, w_ref, o_ref):
        x = (x_ref[...] * g_ref[...] + b_ref[...]).astype(BF16)
        o_ref[...] = _dot(x, w_ref[...]).astype(BF16)

    vec = pl.BlockSpec((1, k), lambda i: (0, 0))
    return pl.pallas_call(
        body, name=name, grid=(t // tm,),
        in_specs=[pl.BlockSpec((tm, k), lambda i: (i, 0)), vec, vec, pl.BlockSpec((k, n), lambda i: (0, 0))],
        out_specs=pl.BlockSpec((tm, n), lambda i: (i, 0)),
        out_shape=jax.ShapeDtypeStruct((t, n), BF16),
        compiler_params=_params(("arbitrary",), VMEM_MID),
    )(xh, gi, bi, w)


def _add_proj_nt(base, y, w, name):
    t, n = y.shape
    k = w.shape[0]
    tm = _row_tile(t)

    def body(b_ref, y_ref, w_ref, o_ref):
        o_ref[...] = b_ref[...] + _dot_nt(y_ref[...].astype(BF16), w_ref[...])

    return pl.pallas_call(
        body, name=name, grid=(t // tm,),
        in_specs=[pl.BlockSpec((tm, k), lambda i: (i, 0)), pl.BlockSpec((tm, n), lambda i: (i, 0)),
                  pl.BlockSpec((k, n), lambda i: (0, 0))],
        out_specs=pl.BlockSpec((tm, k), lambda i: (i, 0)),
        out_shape=jax.ShapeDtypeStruct((t, k), F32),
        compiler_params=_params(("arbitrary",), VMEM_MID),
    )(base, y, w)


def _attn_out_fwd(ot, xh, gi, bi, w_o, go, bo, alpha, name):
    t, d = xh.shape
    tm = _row_tile(t)

    def body(ot_ref, xh_ref, gi_ref, bi_ref, wo_ref, go_ref, bo_ref, xo_ref, rs_ref, hb_ref):
        h = xh_ref[...] * gi_ref[...] + bi_ref[...]
        xhat, rstd = _ln_fwd(alpha * h + _dot_tn(ot_ref[...], wo_ref[...]))
        xo_ref[...] = xhat
        rs_ref[...] = rstd
        hb_ref[...] = (xhat * go_ref[...] + bo_ref[...]).astype(BF16).T

    row = pl.BlockSpec((tm, d), lambda i: (i, 0))
    col = pl.BlockSpec((d, tm), lambda i: (0, i))
    vec = pl.BlockSpec((1, d), lambda i: (0, 0))
    return pl.pallas_call(
        body, name=name, grid=(t // tm,),
        in_specs=[col, row, vec, vec, pl.BlockSpec((d, d), lambda i: (0, 0)), vec, vec],
        out_specs=[row, pl.BlockSpec((tm, 1), lambda i: (i, 0)), col],
        out_shape=[jax.ShapeDtypeStruct((t, d), F32), jax.ShapeDtypeStruct((t, 1), F32),
                   jax.ShapeDtypeStruct((d, t), BF16)],
        compiler_params=_params(("arbitrary",), VMEM_MID),
    )(ot, xh, gi, bi, w_o, go, bo)


def _attn_out_bwd(dh, xo, rs, go, ot, w_o, alpha, name):
    t, d = dh.shape
    tm = _row_tile(t)
    hd = d // N_HEADS

    def body(dh_ref, xo_ref, rs_ref, go_ref, ot_ref, wo_ref,
             dres_ref, dmix_ref, dot_ref, delta_ref, dgain_ref, dbias_ref):
        @pl.when(pl.program_id(0) == 0)
        def _():
            dgain_ref[...] = jnp.zeros_like(dgain_ref)
            dbias_ref[...] = jnp.zeros_like(dbias_ref)

        dz, dgp, dbp = _ln_bwd(dh_ref[...], xo_ref[...], rs_ref[...], go_ref[...])
        dgain_ref[...] += dgp
        dbias_ref[...] += dbp
        dres_ref[...] = alpha * dz
        dmixb = dz.astype(BF16)
        dmix_ref[...] = dmixb
        dot_t = _dot_nt(wo_ref[...], dmixb)
        dot_ref[...] = dot_t.astype(BF16)
        prod = dot_t * ot_ref[...].astype(F32)
        delta_ref[...] = jnp.sum(prod.reshape(N_HEADS, hd, tm), axis=1)

    row = pl.BlockSpec((tm, d), lambda i: (i, 0))
    vec = pl.BlockSpec((1, d), lambda i: (0, 0))
    col = pl.BlockSpec((d, tm), lambda i: (0, i))
    return pl.pallas_call(
        body, name=name, grid=(t // tm,),
        in_specs=[row, row, pl.BlockSpec((tm, 1), lambda i: (i, 0)), vec, col,
                  pl.BlockSpec((d, d), lambda i: (0, 0))],
        out_specs=[row, row, col, pl.BlockSpec((N_HEADS, tm), lambda i: (0, i)), vec, vec],
        out_shape=[jax.ShapeDtypeStruct((t, d), F32), jax.ShapeDtypeStruct((t, d), BF16),
                   jax.ShapeDtypeStruct((d, t), BF16), jax.ShapeDtypeStruct((N_HEADS, t), F32),
                   jax.ShapeDtypeStruct((1, d), F32), jax.ShapeDtypeStruct((1, d), F32)],
        compiler_params=_params(("arbitrary",), VMEM_MID),
    )(dh, xo, rs, go, ot, w_o)


def _scores_t(k, q, ct_ref, c_ref, h, i, j, tq, tk, scale, masked):
    sub = lax.broadcasted_iota(jnp.int32, (8, tq), 0)
    cq = jnp.sum(jnp.where(sub == h, ct_ref[...], 0.0), axis=0, keepdims=True) * LOG2E
    lane = lax.broadcasted_iota(jnp.int32, (tk, LANES), 1)
    ck = jnp.sum(jnp.where(lane == h, c_ref[...], 0.0), axis=1, keepdims=True) * LOG2E
    st = _dot_nt(k, q) * (scale * LOG2E) - ck
    if masked:
        kpos = j * tk + lax.broadcasted_iota(jnp.int32, (tk, 1), 0)
        qpos = i * tq + lax.broadcasted_iota(jnp.int32, (1, tq), 1)
        st = jnp.where((kpos <= qpos) & (kpos >= PAD), st, NEG_INF)
    return st, cq


def _tri_pairs(n, by_row):
    if by_row:
        pairs = [(i, j) for i in range(n) for j in range(i + 1)]
    else:
        pairs = [(i, j) for j in range(n) for i in range(j, n)]
    return (jnp.asarray([p[0] for p in pairs], jnp.int32), jnp.asarray([p[1] for p in pairs], jnp.int32))


def _attn_fwd(q, k, v, c, ct, name, carry=()):
    t, d = q.shape
    hd = d // N_HEADS
    tq = tk = _row_tile(t)
    nq = t // tq
    scale = 1.0 / math.sqrt(hd)

    hps = ATTN_HEADS_PER_STEP

    def body(it_ref, jt_ref, q_ref, k_ref, v_ref, c_ref, ct_ref, ot_ref, lse_ref, m_s, l_s, acc):
        hp, p_ = pl.program_id(0), pl.program_id(1)
        i, j = it_ref[p_], jt_ref[p_]

        @pl.when(j == 0)
        def _():
            m_s[...] = jnp.full_like(m_s, NEG_INF)
            l_s[...] = jnp.zeros_like(l_s)
            acc[...] = jnp.zeros_like(acc)

        def update(masked):
            scores = []
            for e in range(hps):
                cols = slice(e * hd, (e + 1) * hd)
                scores.append(_scores_t(k_ref[:, cols], q_ref[:, cols], ct_ref, c_ref, hp * hps + e,
                                        i, j, tq, tk, scale, masked))
            probs = []
            for e, (st, cq) in enumerate(scores):
                m_new = jnp.maximum(m_s[e], jnp.max(st, axis=0, keepdims=True) + cq)
                a = jnp.exp2(m_s[e] - m_new)
                p = jnp.exp2(st - (m_new - cq))
                l_s[e] = a * l_s[e] + jnp.sum(p, axis=0, keepdims=True)
                m_s[e] = m_new
                probs.append((a, p.astype(BF16)))
            for e, (a, pb) in enumerate(probs):
                acc[e] = a * acc[e] + _dot_tn(v_ref[:, e * hd:(e + 1) * hd], pb)

        edge = (j == i) | (j == 0)
        pl.when(edge)(lambda: update(True))
        pl.when(jnp.logical_not(edge))(lambda: update(False))

        @pl.when(j == i)
        def _():
            for e in range(hps):
                ot_ref[e * hd:(e + 1) * hd, :] = (acc[e] / l_s[e]).astype(BF16)
                lse_ref[e] = m_s[e] + jnp.log2(l_s[e])

    it, jt = _tri_pairs(nq, by_row=True)
    npairs = it.shape[0]
    nhp = N_HEADS // hps
    kv = pl.BlockSpec((tk, hps * hd), lambda h, p, it, jt: (jt[p], h))
    first = lambda: (pl.program_id(0) == 0) & (pl.program_id(1) == 0)
    last = lambda: (pl.program_id(0) == nhp - 1) & (pl.program_id(1) == npairs - 1)
    return pl.pallas_call(
        _carried(body, 7, 2, carry, first, last), name=name,
        grid_spec=pltpu.PrefetchScalarGridSpec(
            num_scalar_prefetch=2, grid=(nhp, npairs),
            in_specs=[pl.BlockSpec((tq, hps * hd), lambda h, p, it, jt: (it[p], h)), kv, kv,
                      pl.BlockSpec((tk, LANES), lambda h, p, it, jt: (jt[p], 0)),
                      pl.BlockSpec((8, tq), lambda h, p, it, jt: (0, it[p]))] + [ANY] * len(carry),
            out_specs=[pl.BlockSpec((hps * hd, tq), lambda h, p, it, jt: (h, it[p])),
                       pl.BlockSpec((hps, 1, tq), lambda h, p, it, jt: (h, 0, it[p]))] + [ANY] * len(carry),
            scratch_shapes=[pltpu.VMEM((hps, 1, tq), F32), pltpu.VMEM((hps, 1, tq), F32),
                            pltpu.VMEM((hps, hd, tq), F32)] + _carry_scratch(carry)),
        out_shape=[jax.ShapeDtypeStruct((d, t), BF16), jax.ShapeDtypeStruct((N_HEADS, 1, t), F32)]
                  + _carry_shapes(carry),
        compiler_params=_params(("arbitrary", "arbitrary"), VMEM_MID),
    )(it, jt, q, k, v, c, ct, *[a for _, a in carry])


def _attn_bwd(q, k, v, c, ct, lse, delta, dot_t, name, carry=()):
    t, d = q.shape
    hd = d // N_HEADS
    tq = tk = _row_tile(t)
    nq = t // tq
    scale = 1.0 / math.sqrt(hd)
    hps = ATTN_BWD_HEADS_PER_STEP

    def body(it_ref, jt_ref, q_ref, k_ref, v_ref, c_ref, ct_ref, lse_ref, delta_ref, dot_ref,
             dq_ref, dk_ref, dv_ref, dcs_ref, drow_ref, dk_acc, dv_acc, dc_acc):
        hp, p_ = pl.program_id(0), pl.program_id(1)
        i, j = it_ref[p_], jt_ref[p_]

        @pl.when(p_ == 0)
        def _():
            dq_ref[...] = jnp.zeros_like(dq_ref)
            drow_ref[...] = jnp.zeros_like(drow_ref)

        @pl.when(i == j)
        def _():
            dk_acc[...] = jnp.zeros_like(dk_acc)
            dv_acc[...] = jnp.zeros_like(dv_acc)
            dc_acc[...] = jnp.zeros_like(dc_acc)

        def update(masked):
            sub = lax.broadcasted_iota(jnp.int32, (8, tq), 0)
            rows = pl.ds(pl.multiple_of(i * tq, tq), tq)
            stage = []
            for e in range(hps):
                cols = slice(e * hd, (e + 1) * hd)
                st, cq = _scores_t(k_ref[:, cols], q_ref[:, cols], ct_ref, c_ref, hp * hps + e,
                                   i, j, tq, tk, scale, masked)
                dp = _dot(v_ref[:, cols], dot_ref[cols, :])
                stage.append((st, cq, dp))
            grads = []
            for e, (st, cq, dp) in enumerate(stage):
                p = jnp.exp2(st - (lse_ref[e] - cq))
                dl = jnp.sum(jnp.where(sub == hp * hps + e, delta_ref[...], 0.0), axis=0, keepdims=True)
                ds = p * (dp - dl)
                part = ds[:, 0:LANES]
                for g in range(1, tq // LANES):
                    part = part + ds[:, g * LANES:(g + 1) * LANES]
                dc_acc[e] += part
                drow_ref[e, i] += jnp.broadcast_to(jnp.sum(ds, axis=0, keepdims=True), (8, tq))
                grads.append((p.astype(BF16), ds.astype(BF16)))
            for e, (pb, dsb) in enumerate(grads):
                cols = slice(e * hd, (e + 1) * hd)
                dv_acc[e] += _dot_nt(pb, dot_ref[cols, :])
                dk_acc[e] += _dot(dsb, q_ref[:, cols]) * scale
                dq_ref[rows, cols] += _dot_tn(dsb, k_ref[:, cols]) * scale

        edge = (j == i) | (j == 0)
        pl.when(edge)(lambda: update(True))
        pl.when(jnp.logical_not(edge))(lambda: update(False))

        @pl.when(i == nq - 1)
        def _():
            for e in range(hps):
                cols = slice(e * hd, (e + 1) * hd)
                dk_ref[:, cols] = dk_acc[e].astype(BF16)
                dv_ref[:, cols] = dv_acc[e].astype(BF16)
                dcs_ref[e] = -dc_acc[e]

    it, jt = _tri_pairs(nq, by_row=False)
    npairs = it.shape[0]
    nhp = N_HEADS // hps
    kv = pl.BlockSpec((tk, hps * hd), lambda h, p, it, jt: (jt[p], h))
    first = lambda: (pl.program_id(0) == 0) & (pl.program_id(1) == 0)
    last = lambda: (pl.program_id(0) == nhp - 1) & (pl.program_id(1) == npairs - 1)
    return pl.pallas_call(
        _carried(body, 10, 5, carry, first, last), name=name,
        grid_spec=pltpu.PrefetchScalarGridSpec(
            num_scalar_prefetch=2, grid=(nhp, npairs),
            in_specs=[pl.BlockSpec((tq, hps * hd), lambda h, p, it, jt: (it[p], h)), kv, kv,
                      pl.BlockSpec((tk, LANES), lambda h, p, it, jt: (jt[p], 0)),
                      pl.BlockSpec((8, tq), lambda h, p, it, jt: (0, it[p])),
                      pl.BlockSpec((hps, 1, tq), lambda h, p, it, jt: (h, 0, it[p])),
                      pl.BlockSpec((N_HEADS, tq), lambda h, p, it, jt: (0, it[p])),
                      pl.BlockSpec((hps * hd, tq), lambda h, p, it, jt: (h, it[p]))] + [ANY] * len(carry),
            out_specs=[pl.BlockSpec((t, hps * hd), lambda h, p, it, jt: (0, h)), kv, kv,
                       pl.BlockSpec((hps, tk, LANES), lambda h, p, it, jt: (h, jt[p], 0)),
                       pl.BlockSpec((hps, nq, 8, tq), lambda h, p, it, jt: (h, 0, 0, 0))] + [ANY] * len(carry),
            scratch_shapes=[pltpu.VMEM((hps, tk, hd), F32), pltpu.VMEM((hps, tk, hd), F32),
                            pltpu.VMEM((hps, tk, LANES), F32)] + _carry_scratch(carry)),
        out_shape=[jax.ShapeDtypeStruct((t, d), F32), jax.ShapeDtypeStruct((t, d), BF16),
                   jax.ShapeDtypeStruct((t, d), BF16), jax.ShapeDtypeStruct((N_HEADS, t, LANES), F32),
                   jax.ShapeDtypeStruct((N_HEADS, nq, 8, tq), F32)] + _carry_shapes(carry),
        compiler_params=_params(("arbitrary", "arbitrary"), VMEM_BIG),
    )(it, jt, q, k, v, c, ct, lse, delta, dot_t, *[a for _, a in carry])


def _adamw(w, g, m, v, name):
    r, c = w.shape
    tr = r
    for cand in (256, 128, 64, 32, 16, 8):
        if r % cand == 0 and r > cand:
            tr = cand
            break
    bc1 = 1.0 - ADAM_B1 ** ADAM_STEP
    bc2 = 1.0 - ADAM_B2 ** ADAM_STEP

    def body(w_ref, g_ref, m_ref, v_ref, d_ref, nm_ref, nv_ref):
        gg = g_ref[...]
        nm = ADAM_B1 * m_ref[...] + (1.0 - ADAM_B1) * gg
        nv = ADAM_B2 * v_ref[...] + (1.0 - ADAM_B2) * (gg * gg)
        d_ref[...] = -ADAM_LR * ((nm / bc1) / (jnp.sqrt(nv / bc2) + ADAM_EPS) + ADAM_WD * w_ref[...])
        nm_ref[...] = nm
        nv_ref[...] = nv

    blk = pl.BlockSpec((tr, c), lambda i: (i, 0))
    shp = jax.ShapeDtypeStruct((r, c), F32)
    return pl.pallas_call(
        body, name=name, grid=(r // tr,), in_specs=[blk] * 4, out_specs=[blk] * 3,
        out_shape=[shp] * 3, compiler_params=_params(("arbitrary",), VMEM_MID),
    )(w, g, m, v)


def _reduce_adamw(w, m, v, landed, name):
    nl, r, c = w.shape
    tr = next(cand for cand in range(min(r, ADAM_ROWS_MAX), 0, -BF16_ROWS) if r % cand == 0)
    nr = r // tr
    bc1 = 1.0 - ADAM_B1 ** ADAM_STEP
    bc2 = 1.0 - ADAM_B2 ** ADAM_STEP

    def body(*refs):
        w_ref, m_ref, v_ref = refs[:3]
        src_refs = refs[3:3 + nl]
        g_ref, d_ref, nm_ref, nv_ref = refs[3 + nl:]

        def update(src):
            gg = src[0].astype(F32)
            for s in range(1, N_DEV):
                gg = gg + src[s].astype(F32)
            nm = ADAM_B1 * m_ref[0] + (1.0 - ADAM_B1) * gg
            nv = ADAM_B2 * v_ref[0] + (1.0 - ADAM_B2) * (gg * gg)
            g_ref[0] = gg
            d_ref[0] = -ADAM_LR * ((nm / bc1) / (jnp.sqrt(nv / bc2) + ADAM_EPS) + ADAM_WD * w_ref[0])
            nm_ref[0] = nm
            nv_ref[0] = nv

        for idx in range(nl):
            pl.when(pl.program_id(0) == idx)(functools.partial(update, src_refs[idx]))

    def src_spec(idx):
        return pl.BlockSpec((N_DEV, tr, c),
                            lambda l, i: (0, jnp.where(l == idx, i, jnp.where(l < idx, 0, nr - 1)), 0))

    blk = pl.BlockSpec((1, tr, c), lambda l, i: (l, i, 0))
    shp = jax.ShapeDtypeStruct((nl, r, c), F32)
    return pl.pallas_call(
        body, name=name, grid=(nl, nr), in_specs=[blk] * 3 + [src_spec(idx) for idx in range(nl)],
        out_specs=[blk] * 4, out_shape=[shp] * 4,
        compiler_params=_params(("arbitrary", "arbitrary"), VMEM_MID),
    )(w, m, v, *landed)


def _sum_sources(r, name):
    n, rows, c = r.shape
    tr = next(cand for cand in range(min(rows, SUM_ROWS_MAX), 0, -BF16_ROWS) if rows % cand == 0)

    def body(r_ref, o_ref):
        acc = r_ref[0].astype(F32)
        for s in range(1, n):
            acc = acc + r_ref[s].astype(F32)
        o_ref[...] = acc

    return pl.pallas_call(
        body, name=name, grid=(rows // tr,),
        in_specs=[pl.BlockSpec((n, tr, c), lambda i: (0, i, 0))],
        out_specs=pl.BlockSpec((tr, c), lambda i: (i, 0)),
        out_shape=jax.ShapeDtypeStruct((rows, c), F32),
        compiler_params=_params(("arbitrary",), VMEM_MID),
    )(r)


def _all_gather(parts, name):
    n = len(parts)

    def body(*refs):
        x_refs, out_refs = refs[:n], refs[n:2 * n]
        send_sems, recv_sems, local_sems = refs[2 * n:]
        mx, my, mc = lax.axis_index("x"), lax.axis_index("y"), lax.axis_index("c")
        me, sibling = (mx, my, mc), (mx, my, 1 - mc)
        chips = [(1 - mx, my), (mx, 1 - my), (1 - mx, 1 - my)]

        def copy(p, k, block, to, from_input=False):
            px, py, pc = block
            rows = out_refs[p].at[4 * px + 2 * py + pc]
            return pltpu.make_async_remote_copy(
                src_ref=x_refs[p] if from_input else rows, dst_ref=rows,
                send_sem=send_sems.at[7 * p + k], recv_sem=recv_sems.at[7 * p + k],
                device_id=to, device_id_type=MESH)

        mine, sent = [], []
        for p in range(n):
            own = pltpu.make_async_copy(x_refs[p], out_refs[p].at[4 * mx + 2 * my + mc], local_sems.at[p])
            own.start()
            mine.append(own)
            first = [copy(p, 0, me, sibling, True)]
            first += [copy(p, 1 + j, me, (*chip, mc), True) for j, chip in enumerate(chips)]
            for cp in first:
                cp.start()
            sent += first
        for p in range(n):
            for j, chip in enumerate(chips):
                copy(p, 1 + j, (*chip, mc), me).wait_recv()
                fwd = copy(p, 4 + j, (*chip, mc), sibling)
                fwd.start()
                sent.append(fwd)
        for p in range(n):
            copy(p, 0, sibling, me).wait_recv()
            for j, chip in enumerate(chips):
                copy(p, 4 + j, (*chip, 1 - mc), me).wait_recv()
        for cp in sent:
            cp.wait_send()
        for own in mine:
            own.wait()

    return pl.pallas_call(
        body, name=name, in_specs=[ANY] * n, out_specs=[ANY] * n,
        out_shape=[jax.ShapeDtypeStruct((N_DEV,) + a.shape, a.dtype) for a in parts],
        scratch_shapes=[pltpu.SemaphoreType.DMA((7 * n,)), pltpu.SemaphoreType.DMA((7 * n,)),
                        pltpu.SemaphoreType.DMA((n,))],
    )(*parts)


def _pack_rows(parts, width, mult, lead=0):
    out = []
    for a in parts:
        head = a.shape[:lead]
        flat = a.reshape(head + (-1,))
        padn = (-flat.shape[-1]) % (width * mult)
        if padn:
            flat = jnp.pad(flat, [(0, 0)] * lead + [(0, padn)])
        out.append(flat.reshape(head + (-1, width)))
    return jnp.concatenate(out, axis=lead)


def _rows_of(shape, width, mult):
    n = math.prod(shape)
    per = width * mult
    return ((n + per - 1) // per) * mult


def _unpack_rows(buf, shapes, width, mult):
    lead = buf.shape[:-2]
    out, off = [], 0
    for shp in shapes:
        r = _rows_of(shp, width, mult)
        flat = buf[..., off:off + r, :].reshape(lead + (r * width,))
        out.append(flat[..., :math.prod(shp)].reshape(lead + tuple(shp)))
        off += r
    return out


def _cols_from_devices(g):
    nd = g.ndim
    perm = tuple(range(1, nd - 1)) + (0, nd - 1)
    t = jnp.transpose(g, perm)
    return t.reshape(t.shape[:-2] + (t.shape[-2] * t.shape[-1],))


def _cols_to_devices(a):
    c = a.shape[-1] // N_DEV
    t = a.reshape(a.shape[:-1] + (N_DEV, c))
    nd = t.ndim
    perm = (nd - 2,) + tuple(range(0, nd - 2)) + (nd - 1,)
    return jnp.transpose(t, perm)


WIDTH = 1024


def kernel(x, meta, ffn1_wg, ffn1_wu, ffn1_wd, ffn2_wg, ffn2_wu, ffn2_wd, ln_gain, ln_bias, conv_w_in, conv_w, conv_w_out, kv_w, f_bias, attn_w_q, attn_w_o, loss_target, m_meta, m_ffn1_wg, m_ffn1_wu, m_ffn1_wd, m_ffn2_wg, m_ffn2_wu, m_ffn2_wd, m_ln_gain, m_ln_bias, m_conv_w_in, m_conv_w, m_conv_w_out, m_kv_w, m_f_bias, m_attn_w_q, m_attn_w_o, v_meta, v_ffn1_wg, v_ffn1_wu, v_ffn1_wd, v_ffn2_wg, v_ffn2_wu, v_ffn2_wd, v_ln_gain, v_ln_bias, v_conv_w_in, v_conv_w, v_conv_w_out, v_kv_w, v_f_bias, v_attn_w_q, v_attn_w_o):
    depth = ln_gain.shape[0]
    alpha = float((2 * depth) ** 0.25)
    d = x.shape[-1]
    seq = x.shape[1]
    t = ROW0 + seq
    fsh = ffn1_wg.shape[-1]
    f = fsh * N_DEV
    fck = MXU_COLS
    nc = f // fck
    me = 4 * lax.axis_index("x") + 2 * lax.axis_index("y") + lax.axis_index("c")

    def gather_of(parts):
        return [(True, a.astype(BF16)) for a in parts]

    small = [meta, ln_gain, ln_bias, conv_w]
    small_shapes = [a.shape for a in small]
    g1g, g1u, g1d, gcin, gcout, gsmall = _all_gather(
        [a.astype(BF16) for a in (ffn1_wg[0], ffn1_wu[0], ffn1_wd[0], conv_w_in[0], conv_w_out[0])]
        + [_pack_rows(small, WIDTH, F32_ROWS)], "ag_first")
    gmeta, ggain, gbias, gcw = _unpack_rows(gsmall, small_shapes, WIDTH, F32_ROWS)

    def ffn_chunks(gg, gu, gd):
        up = lambda g: jnp.transpose(_cols_from_devices(g).reshape(d, nc, fck), (1, 0, 2))
        return up(gg), up(gu), gd.reshape(nc, fck, d)

    w_in = _cols_from_devices(gcin)
    w_out = gcout.reshape(d, d)
    fb = jnp.pad(f_bias, (0, LANES - N_HEADS)).reshape(1, LANES)
    meta_f = _cols_from_devices(gmeta)
    gain_f = _cols_from_devices(ggain)
    bias_f = _cols_from_devices(gbias)
    cw_f = _cols_from_devices(gcw)[0]

    def gb(l, n):
        return gain_f[l, n].reshape(1, d), bias_f[l, n].reshape(1, d)

    ones = jnp.ones((1, d), F32)
    zeros = jnp.zeros((1, d), F32)

    h0 = jnp.concatenate([jnp.zeros((PAD, d), F32), meta_f, x[0]], axis=0)
    hb0 = _cast_t(h0, "h0_bf16_t")

    w1 = ffn_chunks(g1g, g1u, g1d)
    g00, b00 = gb(0, 0)
    xh1, rs1, hb1, gg1, uu1, g2g, g2u = _ffn_fwd(
        h0, ones, zeros, *w1, g00, b00, alpha, "ffn_fwd_0a", carry=gather_of([ffn2_wg[0], ffn2_wu[0]]))
    g01, b01 = gb(0, 1)
    xh2, rs2, hb2, pp, mb, g2d, gkv = _conv_fwd(
        xh1, g00, b00, w_in, cw_f, w_out, g01, b01, alpha, "conv_fwd", carry=gather_of([ffn2_wd[0], kv_w]))
    w2 = ffn_chunks(g2g, g2u, g2d)
    g02, b02 = gb(0, 2)
    xh3, rs3, hb3, gg3, uu3, g3g, g3u = _ffn_fwd(
        xh2, g01, b01, *w2, g02, b02, alpha, "ffn_fwd_0b", carry=gather_of([ffn1_wg[1], ffn1_wu[1]]))
    kvw = _cols_from_devices(gkv)
    wk, wv = kvw[:, :d], kvw[:, d:2 * d]
    wf = jnp.pad(kvw[:, 2 * d:], ((0, 0), (0, LANES - N_HEADS)))
    kk, vv, logit, cc, cct, g3d = _kv_fwd(xh3, g02, b02, wk, wv, wf, fb, "kv_fwd",
                                          carry=gather_of([ffn1_wd[1]]))

    w3 = ffn_chunks(g3g, g3u, g3d)
    g10, b10 = gb(1, 0)
    xh4, rs4, hb4, gg4, uu4, gwq = _ffn_fwd(xh3, g02, b02, *w3, g10, b10, alpha, "ffn_fwd_1a",
                                             carry=gather_of([attn_w_q[0]]))
    w_q = gwq.reshape(d, d)
    qq = _proj(xh4, g10, b10, w_q, "q_proj")
    ot, lse, gwo, g4g, g4u, g4d = _attn_fwd(
        qq, kk, vv, cc, cct, "attn_fwd", carry=gather_of([attn_w_o[0], ffn2_wg[1], ffn2_wu[1], ffn2_wd[1]]))
    w_o = gwo.reshape(d, d)
    g11, b11 = gb(1, 1)
    xh5, rs5, hb5 = _attn_out_fwd(ot, xh4, g10, b10, w_o, g11, b11, alpha, "attn_out_fwd")
    w4 = ffn_chunks(g4g, g4u, g4d)
    g12, b12 = gb(1, 2)
    xh6, rs6, _, gg6, uu6 = _ffn_fwd(xh5, g11, b11, *w4, g12, b12, alpha, "ffn_fwd_1b")


    dgain = [[None] * 3 for _ in range(depth)]
    dbias = [[None] * 3 for _ in range(depth)]

    def to_col_owners(g):
        return (False, _cols_to_devices(g).astype(BF16))

    def to_row_owners(g):
        return (False, g.reshape(N_DEV, g.shape[0] // N_DEV, g.shape[1]).astype(BF16))

    dh5, do6, dg6, du6, a6, dgain[1][2], dbias[1][2], loss_l = _ffn_bwd(
        None, xh6, rs6, g12, gg6, uu6, *w4, alpha, "ffn_bwd_1b", loss_target=loss_target[0], loss_bias=b12)
    loss = lax.psum(loss_l[0, 0], ("x", "y", "c"))
    dw4g, dw4u = _wgrad(hb5, [dg6, du6], "wgrad_up_1b")
    (dw4dt,) = _wgrad(do6, [a6], "wgrad_down_1b")

    dres4, dmix5, dot_t, delta, dgain[1][1], dbias[1][1] = _attn_out_bwd(dh5, xh5, rs5, g11, ot, w_o, alpha, "attn_out_bwd")
    (dwo,) = _wgrad(ot, [dmix5], "wgrad_wo")
    dq, dkk, dvv, dcs, drow, l4g, l4u, l4d, lwo = _attn_bwd(
        qq, kk, vv, cc, cct, lse, delta, dot_t, "attn_bwd",
        carry=[to_col_owners(dw4g), to_col_owners(dw4u), to_row_owners(dw4dt.T), to_row_owners(dwo)])
    dh4 = _add_proj_nt(dres4, dq, w_q, "q_bwd")
    (dwq,) = _wgrad(hb4, [dq], "wgrad_wq")

    dh3a, do4, dg4, du4, a4, dgain[1][0], dbias[1][0] = _ffn_bwd(dh4, xh4, rs4, g10, gg4, uu4, *w3, alpha, "ffn_bwd_1a")
    dw3g, dw3u = _wgrad(hb3, [dg4, du4], "wgrad_up_1a")
    (dw3dt,) = _wgrad(do4, [a4], "wgrad_down_1a")

    dcq = jnp.pad(drow[:, :, 0, :].reshape(N_HEADS, t).T, ((0, 0), (0, LANES - N_HEADS)))
    dh3, dlogit, dfb = _kv_bwd(dkk, dvv, dcs, dcq, logit, dh3a, wk, wv, wf, "kv_bwd")
    dwk, dwv = _wgrad(hb3, [dkk, dvv], "wgrad_kv")
    (dwf,) = _wgrad(hb3, [dlogit], "wgrad_f")
    dkv = jnp.concatenate([dwk, dwv, dwf[:, :N_HEADS]], axis=1)

    dh2, do3, dg3, du3, a3, dgain[0][2], dbias[0][2], lwq, l3g, l3u, l3d, lkv = _ffn_bwd(
        dh3, xh3, rs3, g02, gg3, uu3, *w2, alpha, "ffn_bwd_0b",
        carry=[to_row_owners(dwq), to_col_owners(dw3g), to_col_owners(dw3u), to_row_owners(dw3dt.T),
               to_col_owners(dkv)])
    dw2g, dw2u = _wgrad(hb2, [dg3, du3], "wgrad_up_0b")
    (dw2dt,) = _wgrad(do3, [a3], "wgrad_down_0b")

    dh1, dmix2, dpp, dcw, dgain[0][1], dbias[0][1] = _conv_bwd(dh2, xh2, rs2, g01, pp, cw_f, w_in, w_out, alpha, "conv_bwd")
    (dwin,) = _wgrad(hb1, [dpp], "wgrad_conv_in")
    (dwout,) = _wgrad(mb, [dmix2], "wgrad_conv_out")

    dh0, do1, dg1, du1, a1, dgain[0][0], dbias[0][0], l2g, l2u, l2d, lcin, lcout = _ffn_bwd(
        dh1, xh1, rs1, g00, gg1, uu1, *w1, alpha, "ffn_bwd_0a",
        carry=[to_col_owners(dw2g), to_col_owners(dw2u), to_row_owners(dw2dt.T), to_col_owners(dwin),
               to_row_owners(dwout)])
    (dw1dt,) = _wgrad(do1, [a1], "wgrad_down_0a")
    dw1g, l1d = _wgrad(hb0, [dg1], "wgrad_upg_0a", carry=[to_row_owners(dw1dt.T)])
    dw1u, l1g = _wgrad(hb0, [du1], "wgrad_upu_0a", carry=[to_col_owners(dw1g)])
    dmeta = dh0[PAD:ROW0]
    dgain_f = jnp.stack([jnp.concatenate(r, axis=0) for r in dgain])
    dbias_f = jnp.stack([jnp.concatenate(r, axis=0) for r in dbias])
    small_full = [dmeta, dgain_f, dbias_f, dcw[None], dfb]
    small_full_shapes = [a.shape for a in small_full]
    l1u, gsmall_grads = _exchange([to_col_owners(dw1u), (True, _pack_rows(small_full, WIDTH, F32_ROWS))], "rs_last")

    grad_x = dh0[ROW0:].reshape(1, seq, d)
    rsmall = _sum_sources(gsmall_grads, "small_sum")
    smeta, sgain, sbias, scw, sfb = _unpack_rows(rsmall, small_full_shapes, WIDTH, F32_ROWS)
    csh = d // N_DEV

    def my_cols(a):
        return lax.dynamic_slice_in_dim(a, me * csh, csh, axis=a.ndim - 1)

    grads = {"meta": my_cols(smeta), "ln_gain": my_cols(sgain), "ln_bias": my_cols(sbias),
             "conv_w": my_cols(scw), "f_bias": sfb[0, :N_HEADS]}
    landed = {"ffn1_wg": [l1g, l3g], "ffn1_wu": [l1u, l3u], "ffn1_wd": [l1d, l3d],
              "ffn2_wg": [l2g, l4g], "ffn2_wu": [l2u, l4u], "ffn2_wd": [l2d, l4d],
              "conv_w_in": [lcin], "conv_w_out": [lcout], "kv_w": [lkv], "attn_w_q": [lwq], "attn_w_o": [lwo]}
    weights = dict(meta=meta, ffn1_wg=ffn1_wg, ffn1_wu=ffn1_wu, ffn1_wd=ffn1_wd, ffn2_wg=ffn2_wg,
                   ffn2_wu=ffn2_wu, ffn2_wd=ffn2_wd, ln_gain=ln_gain, ln_bias=ln_bias,
                   conv_w_in=conv_w_in, conv_w=conv_w, conv_w_out=conv_w_out, kv_w=kv_w,
                   f_bias=f_bias, attn_w_q=attn_w_q, attn_w_o=attn_w_o)
    moms = dict(meta=(m_meta, v_meta), ffn1_wg=(m_ffn1_wg, v_ffn1_wg), ffn1_wu=(m_ffn1_wu, v_ffn1_wu),
                ffn1_wd=(m_ffn1_wd, v_ffn1_wd), ffn2_wg=(m_ffn2_wg, v_ffn2_wg), ffn2_wu=(m_ffn2_wu, v_ffn2_wu),
                ffn2_wd=(m_ffn2_wd, v_ffn2_wd), ln_gain=(m_ln_gain, v_ln_gain), ln_bias=(m_ln_bias, v_ln_bias),
                conv_w_in=(m_conv_w_in, v_conv_w_in), conv_w=(m_conv_w, v_conv_w),
                conv_w_out=(m_conv_w_out, v_conv_w_out), kv_w=(m_kv_w, v_kv_w), f_bias=(m_f_bias, v_f_bias),
                attn_w_q=(m_attn_w_q, v_attn_w_q), attn_w_o=(m_attn_w_o, v_attn_w_o))

    names = list(weights)
    g_out, d_out, m_out, v_out = [], [], [], []
    for n in names:
        w = weights[n]
        shp = w.shape
        mm, vv_ = moms[n]
        if n in landed:
            three = (len(landed[n]),) + shp[-2:]
            g, dl, nm, nv = _reduce_adamw(w.reshape(three), mm.reshape(three), vv_.reshape(three),
                                          landed[n], "adamw_" + n)
            g = g.reshape(shp)
        else:
            two = (1, shp[0]) if w.ndim == 1 else (math.prod(shp[:-1]), shp[-1])
            g = grads[n].reshape(shp)
            dl, nm, nv = _adamw(w.reshape(two), g.reshape(two), mm.reshape(two), vv_.reshape(two), "adamw_" + n)
        g_out.append(g)
        d_out.append(dl.reshape(shp))
        m_out.append(nm.reshape(shp))
        v_out.append(nv.reshape(shp))
    return (loss, grad_x, *g_out, *d_out, *m_out, *v_out)
```

```python
import functools
import math

import jax
import jax.numpy as jnp
from jax import lax
from jax.experimental import pallas as pl
from jax.experimental.pallas import tpu as pltpu

F32 = jnp.float32
BF16 = jnp.bfloat16

N_DEV = 8
N_HEADS = 8
N_META = 16
PAD = 112
ROW0 = PAD + N_META
LN_EPS = 1e-5
NEG_INF = -1e30
LOG2E = 1.4426950408889634
ATTN_HEADS_PER_STEP = 8
ATTN_BWD_HEADS_PER_STEP = 2
LANES = 128
MXU_COLS = 256
FFN_FWD_CHUNKS = 11
FFN_BWD_CHUNKS = 4

ADAM_LR = 0.001
ADAM_B1 = 0.9
ADAM_B2 = 0.999
ADAM_EPS = 1e-08
ADAM_WD = 0.01
ADAM_STEP = 10

ROW_TILES = (640, 128)
LOSS_TILE = 128
BF16_ROWS = 16
F32_ROWS = 8
SUM_ROWS_MAX = 768
ADAM_ROWS_MAX = 256
VMEM_BIG = 56 << 20
VMEM_MID = 40 << 20

ANY = pl.BlockSpec(memory_space=pl.ANY)
MESH = pl.DeviceIdType.MESH


def _row_tile(t):
    for c in ROW_TILES:
        if t % c == 0:
            return c
    raise ValueError(f"no row tile for {t}")


def _dot(a, b):
    return jnp.dot(a, b, preferred_element_type=F32)


def _dot_nt(a, b):
    return lax.dot_general(a, b, (((1,), (1,)), ((), ())), preferred_element_type=F32)


def _dot_tn(a, b):
    return lax.dot_general(a, b, (((0,), (0,)), ((), ())), preferred_element_type=F32)


def _params(sem, vmem):
    return pltpu.CompilerParams(dimension_semantics=sem, vmem_limit_bytes=vmem)


def _ln_fwd(z):
    mu = jnp.mean(z, axis=-1, keepdims=True)
    zc = z - mu
    var = jnp.mean(zc * zc, axis=-1, keepdims=True)
    rstd = lax.rsqrt(var + LN_EPS)
    return zc * rstd, rstd


def _ln_bwd(dh, xhat, rstd, gain):
    dxh = dh * gain
    m1 = jnp.mean(dxh, axis=-1, keepdims=True)
    m2 = jnp.mean(dxh * xhat, axis=-1, keepdims=True)
    dz = rstd * (dxh - m1 - xhat * m2)
    return dz, jnp.sum(dh * xhat, axis=0, keepdims=True), jnp.sum(dh, axis=0, keepdims=True)


def _load_resident(pairs, sems):
    cps = [pltpu.make_async_copy(src, dst, sems.at[k]) for k, (src, dst) in enumerate(pairs)]
    for cp in cps:
        cp.start()
    for cp in cps:
        cp.wait()


def _peer_ids():
    mx, my, mc = lax.axis_index("x"), lax.axis_index("y"), lax.axis_index("c")
    peers = []
    for kk in range(1, N_DEV):
        px = 1 - mx if (kk >> 2) & 1 else mx
        py = 1 - my if (kk >> 1) & 1 else my
        pc = 1 - mc if kk & 1 else mc
        peers.append(((px, py, pc), 4 * px + 2 * py + pc))
    return 4 * mx + 2 * my + mc, peers


def _exchange_copies(jobs, send_sems, recv_sems, local_sems, starting):
    me_id, peers = _peer_ids()
    for n, (gather, src, dst) in enumerate(jobs):
        own = pltpu.make_async_copy(src if gather else src.at[me_id], dst.at[me_id], local_sems.at[n])
        own.start() if starting else own.wait()
        for k, (dev, pid) in enumerate(peers):
            sem = (N_DEV - 1) * n + k
            out = src if gather else src.at[pid]
            send = pltpu.make_async_remote_copy(
                src_ref=out, dst_ref=dst.at[me_id], send_sem=send_sems.at[sem], recv_sem=recv_sems.at[sem],
                device_id=dev, device_id_type=MESH)
            if starting:
                send.start()
            else:
                pltpu.make_async_remote_copy(
                    src_ref=out, dst_ref=dst.at[pid], send_sem=send_sems.at[sem], recv_sem=recv_sems.at[sem],
                    device_id=dev, device_id_type=MESH).wait_recv()
                send.wait_send()


def _carried(body, n_in, n_out, carry, first, last):
    nj = len(carry)
    if nj == 0:
        return body

    def wrapped(*refs):
        ins, srcs = refs[:n_in], refs[n_in:n_in + nj]
        outs = refs[n_in + nj:n_in + nj + n_out]
        dsts = refs[n_in + nj + n_out:n_in + 2 * nj + n_out]
        scratch, sems = refs[n_in + 2 * nj + n_out:-3], refs[-3:]
        jobs = [(g, s, r) for (g, _), s, r in zip(carry, srcs, dsts)]

        @pl.when(first())
        def _():
            _exchange_copies(jobs, *sems, starting=True)

        body(*ins, *outs, *scratch)

        @pl.when(last())
        def _():
            _exchange_copies(jobs, *sems, starting=False)

    return wrapped


def _carry_shapes(carry):
    return [jax.ShapeDtypeStruct((N_DEV,) + a.shape if g else a.shape, a.dtype) for g, a in carry]


def _carry_scratch(carry):
    if not carry:
        return []
    n = len(carry)
    return [pltpu.SemaphoreType.DMA(((N_DEV - 1) * n,)), pltpu.SemaphoreType.DMA(((N_DEV - 1) * n,)),
            pltpu.SemaphoreType.DMA((n,))]


def _exchange(carry, name):
    n = len(carry)

    def body(*refs):
        jobs = [(g, s, r) for (g, _), s, r in zip(carry, refs[:n], refs[n:2 * n])]
        _exchange_copies(jobs, *refs[2 * n:], starting=True)
        _exchange_copies(jobs, *refs[2 * n:], starting=False)

    return pl.pallas_call(
        body, name=name, in_specs=[ANY] * n, out_specs=[ANY] * n, out_shape=_carry_shapes(carry),
        scratch_shapes=_carry_scratch(carry),
    )(*[a for _, a in carry])


def _ffn_fwd(xh, gi, bi, wg, wu, wd, go, bo, alpha, name, carry=()):
    t, d = xh.shape
    nch, _, fc = wg.shape
    f = nch * fc
    per = min(FFN_FWD_CHUNKS, nch)
    nc = -(-nch // per)
    tm = _row_tile(t)
    nt = t // tm

    def body(xh_ref, gi_ref, bi_ref, wg_hbm, wu_hbm, wd_hbm, go_ref, bo_ref,
             xo_ref, rs_ref, hb_ref, g_ref, u_ref,
             wg_v, wu_v, wd_v, acc, hbs, sems):
        i = pl.program_id(0)
        c = pl.program_id(1)

        @pl.when((i == 0) & (c == 0))
        def _():
            _load_resident([(wg_hbm, wg_v), (wu_hbm, wu_v), (wd_hbm, wd_v)], sems)

        @pl.when(c == 0)
        def _():
            h = xh_ref[...] * gi_ref[...] + bi_ref[...]
            hbs[...] = h.astype(BF16)
            acc[...] = jnp.zeros_like(acc)

        def chunk(k):
            ck = c * per + k
            cols = slice(k * fc, (k + 1) * fc)
            hb = hbs[...]
            g = _dot(hb, wg_v[ck])
            u = _dot(hb, wu_v[ck])
            a = (g * jax.nn.sigmoid(g)) * u
            g_ref[:, cols] = g.astype(BF16)
            u_ref[:, cols] = u.astype(BF16)
            acc[...] += _dot(a.astype(BF16), wd_v[ck])

        for k in range(per):
            if (nc - 1) * per + k < nch:
                chunk(k)
            else:
                pl.when(c * per + k < nch)(functools.partial(chunk, k))

        @pl.when(c == nc - 1)
        def _():
            h = xh_ref[...] * gi_ref[...] + bi_ref[...]
            xhat, rstd = _ln_fwd(alpha * h + 0.5 * acc[...])
            xo_ref[...] = xhat
            rs_ref[...] = rstd
            hb_ref[...] = (xhat * go_ref[...] + bo_ref[...]).astype(BF16).T

    row = pl.BlockSpec((tm, d), lambda i, c: (i, 0))
    vec = pl.BlockSpec((1, d), lambda i, c: (0, 0))
    chunk = pl.BlockSpec((tm, per * fc), lambda i, c: (i, c))
    first = lambda: (pl.program_id(0) == 0) & (pl.program_id(1) == 0)
    last = lambda: (pl.program_id(0) == nt - 1) & (pl.program_id(1) == nc - 1)
    return pl.pallas_call(
        _carried(body, 8, 5, carry, first, last), name=name, grid=(nt, nc),
        in_specs=[row, vec, vec, ANY, ANY, ANY, vec, vec] + [ANY] * len(carry),
        out_specs=[row, pl.BlockSpec((tm, 1), lambda i, c: (i, 0)),
                   pl.BlockSpec((d, tm), lambda i, c: (0, i)), chunk, chunk] + [ANY] * len(carry),
        out_shape=[jax.ShapeDtypeStruct((t, d), F32), jax.ShapeDtypeStruct((t, 1), F32),
                   jax.ShapeDtypeStruct((d, t), BF16), jax.ShapeDtypeStruct((t, f), BF16),
                   jax.ShapeDtypeStruct((t, f), BF16)] + _carry_shapes(carry),
        scratch_shapes=[pltpu.VMEM((nch, d, fc), BF16), pltpu.VMEM((nch, d, fc), BF16),
                        pltpu.VMEM((nch, fc, d), BF16), pltpu.VMEM((tm, d), F32),
                        pltpu.VMEM((tm, d), BF16), pltpu.SemaphoreType.DMA((3,))] + _carry_scratch(carry),
        compiler_params=_params(("arbitrary", "arbitrary"), VMEM_BIG),
    )(xh, gi, bi, wg, wu, wd, go, bo, *[a for _, a in carry])


def _ffn_bwd(dh, xo, rs, go, gs, us, wg, wu, wd, alpha, name, carry=(), loss_target=None, loss_bias=None):
    t, d = xo.shape
    nch, _, fc = wg.shape
    f = nch * fc
    per = min(FFN_BWD_CHUNKS, nch)
    nc = -(-nch // per)
    tm = _row_tile(t)
    nt = t // tm

    with_loss = loss_target is not None
    nsub, lead = tm // LOSS_TILE, ROW0 // LOSS_TILE
    nlead = nsub + 1 if with_loss else 1

    def body(*refs):
        lead_refs = refs[:nlead]
        xo_ref, rs_ref, go_ref, g_ref, u_ref, wg_hbm, wu_hbm, wd_hbm = refs[nlead:nlead + 8]
        dhin_ref, dot_ref, dg_ref, du_ref, a_ref, dgain_ref, dbias_ref = refs[nlead + 8:nlead + 15]
        rest = refs[nlead + 15:]
        loss_ref, rest = (rest[0], rest[1:]) if with_loss else (None, rest)
        wg_v, wu_v, wd_v, do_ref, sems = rest[:5]
        i = pl.program_id(0)
        c = pl.program_id(1)

        @pl.when((i == 0) & (c == 0))
        def _():
            _load_resident([(wg_hbm, wg_v), (wu_hbm, wu_v), (wd_hbm, wd_v)], sems)
            dgain_ref[...] = jnp.zeros_like(dgain_ref)
            dbias_ref[...] = jnp.zeros_like(dbias_ref)
            if with_loss:
                rest[5][...] = jnp.zeros_like(rest[5])

        def tile_dh():
            if not with_loss:
                return lead_refs[0][...]
            part = rest[5]
            for k in range(nsub):
                sl = slice(k * LOSS_TILE, (k + 1) * LOSS_TILE)
                rows = i * tm + k * LOSS_TILE + lax.broadcasted_iota(jnp.int32, (LOSS_TILE, 1), 0)
                y = xo_ref[sl, :] * go_ref[...] + lead_refs[nsub][...]
                e = jnp.where(rows >= ROW0, y - lead_refs[k][...], 0.0)
                part[...] += jnp.sum(e * e, axis=0, keepdims=True)
                dhin_ref[sl, :] = e * (1.0 / d)

            @pl.when(i == nt - 1)
            def _():
                loss_ref[...] = jnp.full((1, LANES), 0.5 / d, F32) * jnp.sum(part[...])

            return dhin_ref[...]

        @pl.when(c == 0)
        def _():
            dz, dgp, dbp = _ln_bwd(tile_dh(), xo_ref[...], rs_ref[...], go_ref[...])
            dgain_ref[...] += dgp
            dbias_ref[...] += dbp
            dob = (0.5 * dz).astype(BF16)
            do_ref[...] = dob
            dot_ref[...] = dob.T
            dhin_ref[...] = alpha * dz

        def chunk(k):
            ck = c * per + k
            cols = slice(k * fc, (k + 1) * fc)
            g = g_ref[:, cols].astype(F32)
            u = u_ref[:, cols].astype(F32)
            sg = jax.nn.sigmoid(g)
            sl = g * sg
            da = _dot_nt(do_ref[...], wd_v[ck])
            dgb = (da * u * (sg * (1.0 + g * (1.0 - sg)))).astype(BF16)
            dub = (da * sl).astype(BF16)
            a_ref[:, cols] = (sl * u).astype(BF16)
            dg_ref[:, cols] = dgb
            du_ref[:, cols] = dub
            dhin_ref[...] += _dot_nt(dgb, wg_v[ck]) + _dot_nt(dub, wu_v[ck])

        for k in range(per):
            if (nc - 1) * per + k < nch:
                chunk(k)
            else:
                pl.when(c * per + k < nch)(functools.partial(chunk, k))

    row = pl.BlockSpec((tm, d), lambda i, c: (i, 0))
    vec = pl.BlockSpec((1, d), lambda i, c: (0, 0))
    chunk = pl.BlockSpec((tm, per * fc), lambda i, c: (i, c))
    first = lambda: (pl.program_id(0) == 0) & (pl.program_id(1) == 0)
    last = lambda: (pl.program_id(0) == nt - 1) & (pl.program_id(1) == nc - 1)
    if with_loss:
        lead_specs = [pl.BlockSpec((LOSS_TILE, d), lambda i, c, k=k: (jnp.maximum(i * nsub + k - lead, 0), 0))
                      for k in range(nsub)] + [vec]
        lead_args = [loss_target] * nsub + [loss_bias]
        loss_spec, loss_shape = [pl.BlockSpec((1, LANES), lambda i, c: (0, 0))], [jax.ShapeDtypeStruct((1, LANES), F32)]
        loss_scratch = [pltpu.VMEM((1, d), F32)]
    else:
        lead_specs, lead_args, loss_spec, loss_shape, loss_scratch = [row], [dh], [], [], []
    return pl.pallas_call(
        _carried(body, nlead + 8, 7 + len(loss_spec), carry, first, last), name=name, grid=(nt, nc),
        in_specs=lead_specs + [row, pl.BlockSpec((tm, 1), lambda i, c: (i, 0)), vec, chunk, chunk,
                               ANY, ANY, ANY] + [ANY] * len(carry),
        out_specs=[row, pl.BlockSpec((d, tm), lambda i, c: (0, i)), chunk, chunk, chunk, vec, vec]
                  + loss_spec + [ANY] * len(carry),
        out_shape=[jax.ShapeDtypeStruct((t, d), F32), jax.ShapeDtypeStruct((d, t), BF16),
                   jax.ShapeDtypeStruct((t, f), BF16), jax.ShapeDtypeStruct((t, f), BF16),
                   jax.ShapeDtypeStruct((t, f), BF16), jax.ShapeDtypeStruct((1, d), F32),
                   jax.ShapeDtypeStruct((1, d), F32)] + loss_shape + _carry_shapes(carry),
        scratch_shapes=[pltpu.VMEM((nch, d, fc), BF16), pltpu.VMEM((nch, d, fc), BF16),
                        pltpu.VMEM((nch, fc, d), BF16), pltpu.VMEM((tm, d), BF16),
                        pltpu.SemaphoreType.DMA((3,))] + loss_scratch + _carry_scratch(carry),
        compiler_params=_params(("arbitrary", "arbitrary"), VMEM_BIG),
    )(*lead_args, xo, rs, go, gs, us, wg, wu, wd, *[a for _, a in carry])


def _wgrad(xt, ys, name, carry=()):
    m, t = xt.shape
    n = ys[0].shape[1]
    tn = min(n, MXU_COLS)
    ny = len(ys)

    def body(*refs):
        x_hbm = refs[0]
        y_refs = refs[1:1 + ny]
        o_refs = refs[1 + ny:1 + 2 * ny]
        xv, sems = refs[1 + 2 * ny:]

        @pl.when(pl.program_id(0) == 0)
        def _():
            _load_resident([(x_hbm, xv)], sems)

        for y_ref, o_ref in zip(y_refs, o_refs):
            o_ref[...] = _dot(xv[...], y_ref[...].astype(BF16)).astype(BF16)

    steps = n // tn
    first = lambda: pl.program_id(0) == 0
    last = lambda: pl.program_id(0) == steps - 1
    return pl.pallas_call(
        _carried(body, 1 + ny, ny, carry, first, last), name=name, grid=(steps,),
        in_specs=[ANY] + [pl.BlockSpec((t, tn), lambda c: (0, c)) for _ in ys] + [ANY] * len(carry),
        out_specs=[pl.BlockSpec((m, tn), lambda c: (0, c)) for _ in ys] + [ANY] * len(carry),
        out_shape=[jax.ShapeDtypeStruct((m, n), BF16) for _ in ys] + _carry_shapes(carry),
        scratch_shapes=[pltpu.VMEM((m, t), BF16), pltpu.SemaphoreType.DMA((1,))] + _carry_scratch(carry),
        compiler_params=_params(("arbitrary",), VMEM_BIG),
    )(xt, *ys, *[a for _, a in carry])


def _cast_t(h, name):
    t, d = h.shape
    tm = _row_tile(t)

    def body(h_ref, o_ref):
        o_ref[...] = h_ref[...].astype(BF16).T

    return pl.pallas_call(
        body, name=name, grid=(t // tm,),
        in_specs=[pl.BlockSpec((tm, d), lambda i: (i, 0))],
        out_specs=pl.BlockSpec((d, tm), lambda i: (0, i)),
        out_shape=jax.ShapeDtypeStruct((d, t), BF16),
        compiler_params=_params(("arbitrary",), VMEM_MID),
    )(h)


def _shift_rows(u, halo, tm):
    r = lax.broadcasted_iota(jnp.int32, (tm, 1), 0)
    u1 = jnp.where(r == 0, halo[7:8], pltpu.roll(u, 1, 0))
    u2 = jnp.where(r == 0, halo[6:7], jnp.where(r == 1, halo[7:8], pltpu.roll(u, 2, 0)))
    return u1, u2


def _conv_fwd(xh, gi, bi, w_in, cw, w_out, go, bo, alpha, name, carry=()):
    t, d = xh.shape
    tm = _row_tile(t)
    nt = t // tm

    def body(xh_ref, gi_ref, bi_ref, win_ref, cw_ref, wout_ref, go_ref, bo_ref,
             xo_ref, rs_ref, hb_ref, p_ref, m_ref, halo):
        i = pl.program_id(0)

        @pl.when(i == 0)
        def _():
            halo[...] = jnp.zeros_like(halo)

        h = xh_ref[...] * gi_ref[...] + bi_ref[...]
        hb = h.astype(BF16)
        bg = _dot(hb, win_ref[:, 0:d])
        cg = _dot(hb, win_ref[:, d:2 * d])
        val = _dot(hb, win_ref[:, 2 * d:3 * d])
        p_ref[:, 0:d] = bg.astype(BF16)
        p_ref[:, d:2 * d] = cg.astype(BF16)
        p_ref[:, 2 * d:3 * d] = val.astype(BF16)
        rows = i * tm + lax.broadcasted_iota(jnp.int32, (tm, 1), 0)
        u = jnp.where(rows >= PAD, cg * val, 0.0)
        u1, u2 = _shift_rows(u, halo[...], tm)
        halo[...] = u[tm - 8:tm]
        y = cw_ref[0:1] * u2 + cw_ref[1:2] * u1 + cw_ref[2:3] * u
        mb = (bg * y).astype(BF16)
        m_ref[...] = mb.T
        xhat, rstd = _ln_fwd(alpha * h + _dot(mb, wout_ref[...]))
        xo_ref[...] = xhat
        rs_ref[...] = rstd
        hb_ref[...] = (xhat * go_ref[...] + bo_ref[...]).astype(BF16).T

    row = pl.BlockSpec((tm, d), lambda i: (i, 0))
    col = pl.BlockSpec((d, tm), lambda i: (0, i))
    vec = pl.BlockSpec((1, d), lambda i: (0, 0))
    first = lambda: pl.program_id(0) == 0
    last = lambda: pl.program_id(0) == nt - 1
    return pl.pallas_call(
        _carried(body, 8, 5, carry, first, last), name=name, grid=(nt,),
        in_specs=[row, vec, vec, pl.BlockSpec((d, 3 * d), lambda i: (0, 0)),
                  pl.BlockSpec((3, d), lambda i: (0, 0)), pl.BlockSpec((d, d), lambda i: (0, 0)),
                  vec, vec] + [ANY] * len(carry),
        out_specs=[row, pl.BlockSpec((tm, 1), lambda i: (i, 0)), col,
                   pl.BlockSpec((tm, 3 * d), lambda i: (i, 0)), col] + [ANY] * len(carry),
        out_shape=[jax.ShapeDtypeStruct((t, d), F32), jax.ShapeDtypeStruct((t, 1), F32),
                   jax.ShapeDtypeStruct((d, t), BF16), jax.ShapeDtypeStruct((t, 3 * d), BF16),
                   jax.ShapeDtypeStruct((d, t), BF16)] + _carry_shapes(carry),
        scratch_shapes=[pltpu.VMEM((8, d), F32)] + _carry_scratch(carry),
        compiler_params=_params(("arbitrary",), VMEM_BIG),
    )(xh, gi, bi, w_in, cw, w_out, go, bo, *[a for _, a in carry])


def _conv_bwd(dh, xo, rs, go, p, cw, w_in, w_out, alpha, name):
    t, d = dh.shape
    tm = _row_tile(t)
    nt = t // tm
    tb = tm // 8

    def body(dh_ref, xo_ref, rs_ref, go_ref, p_ref, ph_ref, cw_ref, win_ref, wout_ref,
             dhin_ref, dmix_ref, dp_ref, dcw_ref, dgain_ref, dbias_ref, carry):
        i = pl.program_id(0)
        tile = nt - 1 - i

        @pl.when(i == 0)
        def _():
            carry[...] = jnp.zeros_like(carry)
            dcw_ref[...] = jnp.zeros_like(dcw_ref)
            dgain_ref[...] = jnp.zeros_like(dgain_ref)
            dbias_ref[...] = jnp.zeros_like(dbias_ref)

        dz, dgp, dbp = _ln_bwd(dh_ref[...], xo_ref[...], rs_ref[...], go_ref[...])
        dgain_ref[...] += dgp
        dbias_ref[...] += dbp
        dmixb = dz.astype(BF16)
        dmix_ref[...] = dmixb
        dm = _dot_nt(dmixb, wout_ref[...])

        bg = p_ref[:, 0:d].astype(F32)
        cg = p_ref[:, d:2 * d].astype(F32)
        val = p_ref[:, 2 * d:3 * d].astype(F32)
        rows = tile * tm + lax.broadcasted_iota(jnp.int32, (tm, 1), 0)
        valid = rows >= PAD
        u = jnp.where(valid, cg * val, 0.0)
        hrows = tile * tm - 8 + lax.broadcasted_iota(jnp.int32, (8, 1), 0)
        hu = jnp.where((hrows >= PAD) & (tile > 0),
                       ph_ref[:, d:2 * d].astype(F32) * ph_ref[:, 2 * d:3 * d].astype(F32), 0.0)
        u1, u2 = _shift_rows(u, hu, tm)
        w0, w1, w2 = cw_ref[0:1], cw_ref[1:2], cw_ref[2:3]
        y = w0 * u2 + w1 * u1 + w2 * u
        dbg = dm * y
        dy = dm * bg
        dcw_ref[0:1] += jnp.sum(dy * u2, axis=0, keepdims=True)
        dcw_ref[1:2] += jnp.sum(dy * u1, axis=0, keepdims=True)
        dcw_ref[2:3] += jnp.sum(dy * u, axis=0, keepdims=True)

        nxt = carry[...]
        r = lax.broadcasted_iota(jnp.int32, (tm, 1), 0)
        dy1 = jnp.where(r == tm - 1, nxt[0:1], pltpu.roll(dy, tm - 1, 0))
        dy2 = jnp.where(r == tm - 2, nxt[0:1],
                        jnp.where(r == tm - 1, nxt[1:2], pltpu.roll(dy, tm - 2, 0)))
        carry[...] = dy[0:8]
        du = jnp.where(valid, w2 * dy + w1 * dy1 + w0 * dy2, 0.0)
        dbgb = dbg.astype(BF16)
        dcgb = (du * val).astype(BF16)
        dvalb = (du * cg).astype(BF16)
        dp_ref[:, 0:d] = dbgb
        dp_ref[:, d:2 * d] = dcgb
        dp_ref[:, 2 * d:3 * d] = dvalb
        dhin_ref[...] = (alpha * dz + _dot_nt(dbgb, win_ref[:, 0:d])
                         + _dot_nt(dcgb, win_ref[:, d:2 * d]) + _dot_nt(dvalb, win_ref[:, 2 * d:3 * d]))

    row = pl.BlockSpec((tm, d), lambda i: (nt - 1 - i, 0))
    vec = pl.BlockSpec((1, d), lambda i: (0, 0))
    prow = pl.BlockSpec((tm, 3 * d), lambda i: (nt - 1 - i, 0))
    return pl.pallas_call(
        body, name=name, grid=(nt,),
        in_specs=[row, row, pl.BlockSpec((tm, 1), lambda i: (nt - 1 - i, 0)), vec, prow,
                  pl.BlockSpec((8, 3 * d), lambda i: (jnp.maximum((nt - 1 - i) * tb - 1, 0), 0)),
                  pl.BlockSpec((3, d), lambda i: (0, 0)),
                  pl.BlockSpec((d, 3 * d), lambda i: (0, 0)), pl.BlockSpec((d, d), lambda i: (0, 0))],
        out_specs=[row, row, prow, pl.BlockSpec((3, d), lambda i: (0, 0)), vec, vec],
        out_shape=[jax.ShapeDtypeStruct((t, d), F32), jax.ShapeDtypeStruct((t, d), BF16),
                   jax.ShapeDtypeStruct((t, 3 * d), BF16), jax.ShapeDtypeStruct((3, d), F32),
                   jax.ShapeDtypeStruct((1, d), F32), jax.ShapeDtypeStruct((1, d), F32)],
        scratch_shapes=[pltpu.VMEM((8, d), F32)],
        compiler_params=_params(("arbitrary",), VMEM_BIG),
    )(dh, xo, rs, go, p, p, cw, w_in, w_out)


def _kv_fwd(xh, gi, bi, wk, wv, wf, fb, name, carry=()):
    t, d = xh.shape
    tm = _row_tile(t)
    nt = t // tm

    def body(xh_ref, gi_ref, bi_ref, wk_ref, wv_ref, wf_ref, fb_ref,
             k_ref, v_ref, lg_ref, c_ref, ct_ref, run):
        i = pl.program_id(0)

        @pl.when(i == 0)
        def _():
            run[...] = jnp.zeros_like(run)

        x = (xh_ref[...] * gi_ref[...] + bi_ref[...]).astype(BF16)
        k_ref[...] = _dot(x, wk_ref[...]).astype(BF16)
        v_ref[...] = _dot(x, wv_ref[...]).astype(BF16)
        logit = _dot(x, wf_ref[...]) + fb_ref[...]
        lg_ref[...] = logit
        logf = jnp.minimum(logit, 0.0) - jnp.log(1.0 + jnp.exp(-jnp.abs(logit)))
        rows = i * tm + lax.broadcasted_iota(jnp.int32, (tm, 1), 0)
        logf = jnp.where(rows >= PAD, logf, 0.0)
        tri = (lax.broadcasted_iota(jnp.int32, (tm, tm), 0)
               >= lax.broadcasted_iota(jnp.int32, (tm, tm), 1)).astype(F32)
        cs = jnp.dot(tri, logf, precision=lax.Precision.HIGHEST, preferred_element_type=F32) + run[...]
        run[...] = cs[tm - 1:tm]
        c_ref[...] = cs
        ct_ref[...] = cs.T

    row = pl.BlockSpec((tm, d), lambda i: (i, 0))
    vec = pl.BlockSpec((1, d), lambda i: (0, 0))
    gate = pl.BlockSpec((tm, LANES), lambda i: (i, 0))
    sq = pl.BlockSpec((d, d), lambda i: (0, 0))
    first = lambda: pl.program_id(0) == 0
    last = lambda: pl.program_id(0) == nt - 1
    return pl.pallas_call(
        _carried(body, 7, 5, carry, first, last), name=name, grid=(nt,),
        in_specs=[row, vec, vec, sq, sq, pl.BlockSpec((d, LANES), lambda i: (0, 0)),
                  pl.BlockSpec((1, LANES), lambda i: (0, 0))] + [ANY] * len(carry),
        out_specs=[row, row, gate, gate, pl.BlockSpec((LANES, tm), lambda i: (0, i))] + [ANY] * len(carry),
        out_shape=[jax.ShapeDtypeStruct((t, d), BF16), jax.ShapeDtypeStruct((t, d), BF16),
                   jax.ShapeDtypeStruct((t, LANES), F32), jax.ShapeDtypeStruct((t, LANES), F32),
                   jax.ShapeDtypeStruct((LANES, t), F32)] + _carry_shapes(carry),
        scratch_shapes=[pltpu.VMEM((1, LANES), F32)] + _carry_scratch(carry),
        compiler_params=_params(("arbitrary",), VMEM_MID),
    )(xh, gi, bi, wk, wv, wf, fb, *[a for _, a in carry])


def _kv_bwd(dk, dv, dcs, dcq, logit, dh_other, wk, wv, wf, name):
    t, d = dk.shape
    tm = _row_tile(t)
    nt = t // tm

    def body(dk_ref, dv_ref, dcs_ref, dcq_ref, lg_ref, oth_ref, wk_ref, wv_ref, wf_ref,
             dh_ref, dl_ref, dfb_ref, run):
        i = pl.program_id(0)
        tile = nt - 1 - i

        @pl.when(i == 0)
        def _():
            run[...] = jnp.zeros_like(run)
            dfb_ref[...] = jnp.zeros_like(dfb_ref)

        lane = lax.broadcasted_iota(jnp.int32, (tm, LANES), 1)
        dc = dcq_ref[...]
        for hh in range(N_HEADS):
            dc = dc + jnp.where(lane == hh, jnp.sum(dcs_ref[hh], axis=1, keepdims=True), 0.0)
        tri = (lax.broadcasted_iota(jnp.int32, (tm, tm), 0)
               <= lax.broadcasted_iota(jnp.int32, (tm, tm), 1)).astype(F32)
        dlf = jnp.dot(tri, dc, precision=lax.Precision.HIGHEST, preferred_element_type=F32) + run[...]
        run[...] = dlf[0:1]
        rows = tile * tm + lax.broadcasted_iota(jnp.int32, (tm, 1), 0)
        dlogit = jnp.where(rows >= PAD, dlf * jax.nn.sigmoid(-lg_ref[...]), 0.0)
        dfb_ref[...] += jnp.sum(dlogit, axis=0, keepdims=True)
        dlb = dlogit.astype(BF16)
        dl_ref[...] = dlb
        dh_ref[...] = (oth_ref[...] + _dot_nt(dk_ref[...], wk_ref[...])
                       + _dot_nt(dv_ref[...], wv_ref[...]) + _dot_nt(dlb, wf_ref[...]))

    row = pl.BlockSpec((tm, d), lambda i: (nt - 1 - i, 0))
    gate = pl.BlockSpec((tm, LANES), lambda i: (nt - 1 - i, 0))
    sq = pl.BlockSpec((d, d), lambda i: (0, 0))
    return pl.pallas_call(
        body, name=name, grid=(nt,),
        in_specs=[row, row, pl.BlockSpec((N_HEADS, tm, LANES), lambda i: (0, nt - 1 - i, 0)), gate, gate, row,
                  sq, sq, pl.BlockSpec((d, LANES), lambda i: (0, 0))],
        out_specs=[row, gate, pl.BlockSpec((1, LANES), lambda i: (0, 0))],
        out_shape=[jax.ShapeDtypeStruct((t, d), F32), jax.ShapeDtypeStruct((t, LANES), BF16),
                   jax.ShapeDtypeStruct((1, LANES), F32)],
        scratch_shapes=[pltpu.VMEM((1, LANES), F32)],
        compiler_params=_params(("arbitrary",), VMEM_MID),
    )(dk, dv, dcs, dcq, logit, dh_other, wk, wv, wf)


def _proj(xh, gi, bi, w, name):
    t, k = xh.shape
    n = w.shape[1]
    tm = _row_tile(t)

    def body(x_ref, g_ref, b_ref, w_ref, o_ref):
        x = (x_ref[...] * g_ref[...] + b_ref[...]).astype(BF16)
        o_ref[...] = _dot(x, w_ref[...]).astype(BF16)

    vec = pl.BlockSpec((1, k), lambda i: (0, 0))
    return pl.pallas_call(
        body, name=name, grid=(t // tm,),
        in_specs=[pl.BlockSpec((tm, k), lambda i: (i, 0)), vec, vec, pl.BlockSpec((k, n), lambda i: (0, 0))],
        out_specs=pl.BlockSpec((tm, n), lambda i: (i, 0)),
        out_shape=jax.ShapeDtypeStruct((t, n), BF16),
        compiler_params=_params(("arbitrary",), VMEM_MID),
    )(xh, gi, bi, w)


def _add_proj_nt(base, y, w, name):
    t, n = y.shape
    k = w.shape[0]
    tm = _row_tile(t)

    def body(b_ref, y_ref, w_ref, o_ref):
        o_ref[...] = b_ref[...] + _dot_nt(y_ref[...].astype(BF16), w_ref[...])

    return pl.pallas_call(
        body, name=name, grid=(t // tm,),
        in_specs=[pl.BlockSpec((tm, k), lambda i: (i, 0)), pl.BlockSpec((tm, n), lambda i: (i, 0)),
                  pl.BlockSpec((k, n), lambda i: (0, 0))],
        out_specs=pl.BlockSpec((tm, k), lambda i: (i, 0)),
        out_shape=jax.ShapeDtypeStruct((t, k), F32),
        compiler_params=_params(("arbitrary",), VMEM_MID),
    )(base, y, w)


def _attn_out_fwd(ot, xh, gi, bi, w_o, go, bo, alpha, name):
    t, d = xh.shape
    tm = _row_tile(t)

    def body(ot_ref, xh_ref, gi_ref, bi_ref, wo_ref, go_ref, bo_ref, xo_ref, rs_ref, hb_ref):
        h = xh_ref[...] * gi_ref[...] + bi_ref[...]
        xhat, rstd = _ln_fwd(alpha * h + _dot_tn(ot_ref[...], wo_ref[...]))
        xo_ref[...] = xhat
        rs_ref[...] = rstd
        hb_ref[...] = (xhat * go_ref[...] + bo_ref[...]).astype(BF16).T

    row = pl.BlockSpec((tm, d), lambda i: (i, 0))
    col = pl.BlockSpec((d, tm), lambda i: (0, i))
    vec = pl.BlockSpec((1, d), lambda i: (0, 0))
    return pl.pallas_call(
        body, name=name, grid=(t // tm,),
        in_specs=[col, row, vec, vec, pl.BlockSpec((d, d), lambda i: (0, 0)), vec, vec],
        out_specs=[row, pl.BlockSpec((tm, 1), lambda i: (i, 0)), col],
        out_shape=[jax.ShapeDtypeStruct((t, d), F32), jax.ShapeDtypeStruct((t, 1), F32),
                   jax.ShapeDtypeStruct((d, t), BF16)],
        compiler_params=_params(("arbitrary",), VMEM_MID),
    )(ot, xh, gi, bi, w_o, go, bo)


def _attn_out_bwd(dh, xo, rs, go, ot, w_o, alpha, name):
    t, d = dh.shape
    tm = _row_tile(t)
    hd = d // N_HEADS

    def body(dh_ref, xo_ref, rs_ref, go_ref, ot_ref, wo_ref,
             dres_ref, dmix_ref, dot_ref, delta_ref, dgain_ref, dbias_ref):
        @pl.when(pl.program_id(0) == 0)
        def _():
            dgain_ref[...] = jnp.zeros_like(dgain_ref)
            dbias_ref[...] = jnp.zeros_like(dbias_ref)

        dz, dgp, dbp = _ln_bwd(dh_ref[...], xo_ref[...], rs_ref[...], go_ref[...])
        dgain_ref[...] += dgp
        dbias_ref[...] += dbp
        dres_ref[...] = alpha * dz
        dmixb = dz.astype(BF16)
        dmix_ref[...] = dmixb
        dot_t = _dot_nt(wo_ref[...], dmixb)
        dot_ref[...] = dot_t.astype(BF16)
        prod = dot_t * ot_ref[...].astype(F32)
        delta_ref[...] = jnp.sum(prod.reshape(N_HEADS, hd, tm), axis=1)

    row = pl.BlockSpec((tm, d), lambda i: (i, 0))
    vec = pl.BlockSpec((1, d), lambda i: (0, 0))
    col = pl.BlockSpec((d, tm), lambda i: (0, i))
    return pl.pallas_call(
        body, name=name, grid=(t // tm,),
        in_specs=[row, row, pl.BlockSpec((tm, 1), lambda i: (i, 0)), vec, col,
                  pl.BlockSpec((d, d), lambda i: (0, 0))],
        out_specs=[row, row, col, pl.BlockSpec((N_HEADS, tm), lambda i: (0, i)), vec, vec],
        out_shape=[jax.ShapeDtypeStruct((t, d), F32), jax.ShapeDtypeStruct((t, d), BF16),
                   jax.ShapeDtypeStruct((d, t), BF16), jax.ShapeDtypeStruct((N_HEADS, t), F32),
                   jax.ShapeDtypeStruct((1, d), F32), jax.ShapeDtypeStruct((1, d), F32)],
        compiler_params=_params(("arbitrary",), VMEM_MID),
    )(dh, xo, rs, go, ot, w_o)


def _scores_t(k, q, ct_ref, c_ref, h, i, j, tq, tk, scale, masked):
    sub = lax.broadcasted_iota(jnp.int32, (8, tq), 0)
    cq = jnp.sum(jnp.where(sub == h, ct_ref[...], 0.0), axis=0, keepdims=True) * LOG2E
    lane = lax.broadcasted_iota(jnp.int32, (tk, LANES), 1)
    ck = jnp.sum(jnp.where(lane == h, c_ref[...], 0.0), axis=1, keepdims=True) * LOG2E
    st = _dot_nt(k, q) * (scale * LOG2E) - ck
    if masked:
        kpos = j * tk + lax.broadcasted_iota(jnp.int32, (tk, 1), 0)
        qpos = i * tq + lax.broadcasted_iota(jnp.int32, (1, tq), 1)
        st = jnp.where((kpos <= qpos) & (kpos >= PAD), st, NEG_INF)
    return st, cq


def _tri_pairs(n, by_row):
    if by_row:
        pairs = [(i, j) for i in range(n) for j in range(i + 1)]
    else:
        pairs = [(i, j) for j in range(n) for i in range(j, n)]
    return (jnp.asarray([p[0] for p in pairs], jnp.int32), jnp.asarray([p[1] for p in pairs], jnp.int32))


def _attn_fwd(q, k, v, c, ct, name, carry=()):
    t, d = q.shape
    hd = d // N_HEADS
    tq = tk = _row_tile(t)
    nq = t // tq
    scale = 1.0 / math.sqrt(hd)

    hps = ATTN_HEADS_PER_STEP

    def body(it_ref, jt_ref, q_ref, k_ref, v_ref, c_ref, ct_ref, ot_ref, lse_ref, m_s, l_s, acc):
        hp, p_ = pl.program_id(0), pl.program_id(1)
        i, j = it_ref[p_], jt_ref[p_]

        @pl.when(j == 0)
        def _():
            m_s[...] = jnp.full_like(m_s, NEG_INF)
            l_s[...] = jnp.zeros_like(l_s)
            acc[...] = jnp.zeros_like(acc)

        def update(masked):
            scores = []
            for e in range(hps):
                cols = slice(e * hd, (e + 1) * hd)
                scores.append(_scores_t(k_ref[:, cols], q_ref[:, cols], ct_ref, c_ref, hp * hps + e,
                                        i, j, tq, tk, scale, masked))
            probs = []
            for e, (st, cq) in enumerate(scores):
                m_new = jnp.maximum(m_s[e], jnp.max(st, axis=0, keepdims=True) + cq)
                a = jnp.exp2(m_s[e] - m_new)
                p = jnp.exp2(st - (m_new - cq))
                l_s[e] = a * l_s[e] + jnp.sum(p, axis=0, keepdims=True)
                m_s[e] = m_new
                probs.append((a, p.astype(BF16)))
            for e, (a, pb) in enumerate(probs):
                acc[e] = a * acc[e] + _dot_tn(v_ref[:, e * hd:(e + 1) * hd], pb)

        edge = (j == i) | (j == 0)
        pl.when(edge)(lambda: update(True))
        pl.when(jnp.logical_not(edge))(lambda: update(False))

        @pl.when(j == i)
        def _():
            for e in range(hps):
                ot_ref[e * hd:(e + 1) * hd, :] = (acc[e] / l_s[e]).astype(BF16)
                lse_ref[e] = m_s[e] + jnp.log2(l_s[e])

    it, jt = _tri_pairs(nq, by_row=True)
    npairs = it.shape[0]
    nhp = N_HEADS // hps
    kv = pl.BlockSpec((tk, hps * hd), lambda h, p, it, jt: (jt[p], h))
    first = lambda: (pl.program_id(0) == 0) & (pl.program_id(1) == 0)
    last = lambda: (pl.program_id(0) == nhp - 1) & (pl.program_id(1) == npairs - 1)
    return pl.pallas_call(
        _carried(body, 7, 2, carry, first, last), name=name,
        grid_spec=pltpu.PrefetchScalarGridSpec(
            num_scalar_prefetch=2, grid=(nhp, npairs),
            in_specs=[pl.BlockSpec((tq, hps * hd), lambda h, p, it, jt: (it[p], h)), kv, kv,
                      pl.BlockSpec((tk, LANES), lambda h, p, it, jt: (jt[p], 0)),
                      pl.BlockSpec((8, tq), lambda h, p, it, jt: (0, it[p]))] + [ANY] * len(carry),
            out_specs=[pl.BlockSpec((hps * hd, tq), lambda h, p, it, jt: (h, it[p])),
                       pl.BlockSpec((hps, 1, tq), lambda h, p, it, jt: (h, 0, it[p]))] + [ANY] * len(carry),
            scratch_shapes=[pltpu.VMEM((hps, 1, tq), F32), pltpu.VMEM((hps, 1, tq), F32),
                            pltpu.VMEM((hps, hd, tq), F32)] + _carry_scratch(carry)),
        out_shape=[jax.ShapeDtypeStruct((d, t), BF16), jax.ShapeDtypeStruct((N_HEADS, 1, t), F32)]
                  + _carry_shapes(carry),
        compiler_params=_params(("arbitrary", "arbitrary"), VMEM_MID),
    )(it, jt, q, k, v, c, ct, *[a for _, a in carry])


def _attn_bwd(q, k, v, c, ct, lse, delta, dot_t, name, carry=()):
    t, d = q.shape
    hd = d // N_HEADS
    tq = tk = _row_tile(t)
    nq = t // tq
    scale = 1.0 / math.sqrt(hd)
    hps = ATTN_BWD_HEADS_PER_STEP

    def body(it_ref, jt_ref, q_ref, k_ref, v_ref, c_ref, ct_ref, lse_ref, delta_ref, dot_ref,
             dq_ref, dk_ref, dv_ref, dcs_ref, drow_ref, dk_acc, dv_acc, dc_acc):
        hp, p_ = pl.program_id(0), pl.program_id(1)
        i, j = it_ref[p_], jt_ref[p_]

        @pl.when(p_ == 0)
        def _():
            dq_ref[...] = jnp.zeros_like(dq_ref)
            drow_ref[...] = jnp.zeros_like(drow_ref)

        @pl.when(i == j)
        def _():
            dk_acc[...] = jnp.zeros_like(dk_acc)
            dv_acc[...] = jnp.zeros_like(dv_acc)
            dc_acc[...] = jnp.zeros_like(dc_acc)

        def update(masked):
            sub = lax.broadcasted_iota(jnp.int32, (8, tq), 0)
            rows = pl.ds(pl.multiple_of(i * tq, tq), tq)
            stage = []
            for e in range(hps):
                cols = slice(e * hd, (e + 1) * hd)
                st, cq = _scores_t(k_ref[:, cols], q_ref[:, cols], ct_ref, c_ref, hp * hps + e,
                                   i, j, tq, tk, scale, masked)
                dp = _dot(v_ref[:, cols], dot_ref[cols, :])
                stage.append((st, cq, dp))
            grads = []
            for e, (st, cq, dp) in enumerate(stage):
                p = jnp.exp2(st - (lse_ref[e] - cq))
                dl = jnp.sum(jnp.where(sub == hp * hps + e, delta_ref[...], 0.0), axis=0, keepdims=True)
                ds = p * (dp - dl)
                part = ds[:, 0:LANES]
                for g in range(1, tq // LANES):
                    part = part + ds[:, g * LANES:(g + 1) * LANES]
                dc_acc[e] += part
                drow_ref[e, i] += jnp.broadcast_to(jnp.sum(ds, axis=0, keepdims=True), (8, tq))
                grads.append((p.astype(BF16), ds.astype(BF16)))
            for e, (pb, dsb) in enumerate(grads):
                cols = slice(e * hd, (e + 1) * hd)
                dv_acc[e] += _dot_nt(pb, dot_ref[cols, :])
                dk_acc[e] += _dot(dsb, q_ref[:, cols]) * scale
                dq_ref[rows, cols] += _dot_tn(dsb, k_ref[:, cols]) * scale

        edge = (j == i) | (j == 0)
        pl.when(edge)(lambda: update(True))
        pl.when(jnp.logical_not(edge))(lambda: update(False))

        @pl.when(i == nq - 1)
        def _():
            for e in range(hps):
                cols = slice(e * hd, (e + 1) * hd)
                dk_ref[:, cols] = dk_acc[e].astype(BF16)
                dv_ref[:, cols] = dv_acc[e].astype(BF16)
                dcs_ref[e] = -dc_acc[e]

    it, jt = _tri_pairs(nq, by_row=False)
    npairs = it.shape[0]
    nhp = N_HEADS // hps
    kv = pl.BlockSpec((tk, hps * hd), lambda h, p, it, jt: (jt[p], h))
    first = lambda: (pl.program_id(0) == 0) & (pl.program_id(1) == 0)
    last = lambda: (pl.program_id(0) == nhp - 1) & (pl.program_id(1) == npairs - 1)
    return pl.pallas_call(
        _carried(body, 10, 5, carry, first, last), name=name,
        grid_spec=pltpu.PrefetchScalarGridSpec(
            num_scalar_prefetch=2, grid=(nhp, npairs),
            in_specs=[pl.BlockSpec((tq, hps * hd), lambda h, p, it, jt: (it[p], h)), kv, kv,
                      pl.BlockSpec((tk, LANES), lambda h, p, it, jt: (jt[p], 0)),
                      pl.BlockSpec((8, tq), lambda h, p, it, jt: (0, it[p])),
                      pl.BlockSpec((hps, 1, tq), lambda h, p, it, jt: (h, 0, it[p])),
                      pl.BlockSpec((N_HEADS, tq), lambda h, p, it, jt: (0, it[p])),
                      pl.BlockSpec((hps * hd, tq), lambda h, p, it, jt: (h, it[p]))] + [ANY] * len(carry),
            out_specs=[pl.BlockSpec((t, hps * hd), lambda h, p, it, jt: (0, h)), kv, kv,
                       pl.BlockSpec((hps, tk, LANES), lambda h, p, it, jt: (h, jt[p], 0)),
                       pl.BlockSpec((hps, nq, 8, tq), lambda h, p, it, jt: (h, 0, 0, 0))] + [ANY] * len(carry),
            scratch_shapes=[pltpu.VMEM((hps, tk, hd), F32), pltpu.VMEM((hps, tk, hd), F32),
                            pltpu.VMEM((hps, tk, LANES), F32)] + _carry_scratch(carry)),
        out_shape=[jax.ShapeDtypeStruct((t, d), F32), jax.ShapeDtypeStruct((t, d), BF16),
                   jax.ShapeDtypeStruct((t, d), BF16), jax.ShapeDtypeStruct((N_HEADS, t, LANES), F32),
                   jax.ShapeDtypeStruct((N_HEADS, nq, 8, tq), F32)] + _carry_shapes(carry),
        compiler_params=_params(("arbitrary", "arbitrary"), VMEM_BIG),
    )(it, jt, q, k, v, c, ct, lse, delta, dot_t, *[a for _, a in carry])


def _adamw(w, g, m, v, name):
    r, c = w.shape
    tr = r
    for cand in (256, 128, 64, 32, 16, 8):
        if r % cand == 0 and r > cand:
            tr = cand
            break
    bc1 = 1.0 - ADAM_B1 ** ADAM_STEP
    bc2 = 1.0 - ADAM_B2 ** ADAM_STEP

    def body(w_ref, g_ref, m_ref, v_ref, d_ref, nm_ref, nv_ref):
        gg = g_ref[...]
        nm = ADAM_B1 * m_ref[...] + (1.0 - ADAM_B1) * gg
        nv = ADAM_B2 * v_ref[...] + (1.0 - ADAM_B2) * (gg * gg)
        d_ref[...] = -ADAM_LR * ((nm / bc1) / (jnp.sqrt(nv / bc2) + ADAM_EPS) + ADAM_WD * w_ref[...])
        nm_ref[...] = nm
        nv_ref[...] = nv

    blk = pl.BlockSpec((tr, c), lambda i: (i, 0))
    shp = jax.ShapeDtypeStruct((r, c), F32)
    return pl.pallas_call(
        body, name=name, grid=(r // tr,), in_specs=[blk] * 4, out_specs=[blk] * 3,
        out_shape=[shp] * 3, compiler_params=_params(("arbitrary",), VMEM_MID),
    )(w, g, m, v)


def _reduce_adamw(w, m, v, landed, name):
    nl, r, c = w.shape
    tr = next(cand for cand in range(min(r, ADAM_ROWS_MAX), 0, -BF16_ROWS) if r % cand == 0)
    nr = r // tr
    bc1 = 1.0 - ADAM_B1 ** ADAM_STEP
    bc2 = 1.0 - ADAM_B2 ** ADAM_STEP

    def body(*refs):
        w_ref, m_ref, v_ref = refs[:3]
        src_refs = refs[3:3 + nl]
        g_ref, d_ref, nm_ref, nv_ref = refs[3 + nl:]

        def update(src):
            gg = src[0].astype(F32)
            for s in range(1, N_DEV):
                gg = gg + src[s].astype(F32)
            nm = ADAM_B1 * m_ref[0] + (1.0 - ADAM_B1) * gg
            nv = ADAM_B2 * v_ref[0] + (1.0 - ADAM_B2) * (gg * gg)
            g_ref[0] = gg
            d_ref[0] = -ADAM_LR * ((nm / bc1) / (jnp.sqrt(nv / bc2) + ADAM_EPS) + ADAM_WD * w_ref[0])
            nm_ref[0] = nm
            nv_ref[0] = nv

        for idx in range(nl):
            pl.when(pl.program_id(0) == idx)(functools.partial(update, src_refs[idx]))

    def src_spec(idx):
        return pl.BlockSpec((N_DEV, tr, c),
                            lambda l, i: (0, jnp.where(l == idx, i, jnp.where(l < idx, 0, nr - 1)), 0))

    blk = pl.BlockSpec((1, tr, c), lambda l, i: (l, i, 0))
    shp = jax.ShapeDtypeStruct((nl, r, c), F32)
    return pl.pallas_call(
        body, name=name, grid=(nl, nr), in_specs=[blk] * 3 + [src_spec(idx) for idx in range(nl)],
        out_specs=[blk] * 4, out_shape=[shp] * 4,
        compiler_params=_params(("arbitrary", "arbitrary"), VMEM_MID),
    )(w, m, v, *landed)


def _sum_sources(r, name):
    n, rows, c = r.shape
    tr = next(cand for cand in range(min(rows, SUM_ROWS_MAX), 0, -BF16_ROWS) if rows % cand == 0)

    def body(r_ref, o_ref):
        acc = r_ref[0].astype(F32)
        for s in range(1, n):
            acc = acc + r_ref[s].astype(F32)
        o_ref[...] = acc

    return pl.pallas_call(
        body, name=name, grid=(rows // tr,),
        in_specs=[pl.BlockSpec((n, tr, c), lambda i: (0, i, 0))],
        out_specs=pl.BlockSpec((tr, c), lambda i: (i, 0)),
        out_shape=jax.ShapeDtypeStruct((rows, c), F32),
        compiler_params=_params(("arbitrary",), VMEM_MID),
    )(r)


def _all_gather(parts, name):
    n = len(parts)

    def body(*refs):
        x_refs, out_refs = refs[:n], refs[n:2 * n]
        send_sems, recv_sems, local_sems = refs[2 * n:]
        mx, my, mc = lax.axis_index("x"), lax.axis_index("y"), lax.axis_index("c")
        me, sibling = (mx, my, mc), (mx, my, 1 - mc)
        chips = [(1 - mx, my), (mx, 1 - my), (1 - mx, 1 - my)]

        def copy(p, k, block, to, from_input=False):
            px, py, pc = block
            rows = out_refs[p].at[4 * px + 2 * py + pc]
            return pltpu.make_async_remote_copy(
                src_ref=x_refs[p] if from_input else rows, dst_ref=rows,
                send_sem=send_sems.at[7 * p + k], recv_sem=recv_sems.at[7 * p + k],
                device_id=to, device_id_type=MESH)

        mine, sent = [], []
        for p in range(n):
            own = pltpu.make_async_copy(x_refs[p], out_refs[p].at[4 * mx + 2 * my + mc], local_sems.at[p])
            own.start()
            mine.append(own)
            first = [copy(p, 0, me, sibling, True)]
            first += [copy(p, 1 + j, me, (*chip, mc), True) for j, chip in enumerate(chips)]
            for cp in first:
                cp.start()
            sent += first
        for p in range(n):
            for j, chip in enumerate(chips):
                copy(p, 1 + j, (*chip, mc), me).wait_recv()
                fwd = copy(p, 4 + j, (*chip, mc), sibling)
                fwd.start()
                sent.append(fwd)
        for p in range(n):
            copy(p, 0, sibling, me).wait_recv()
            for j, chip in enumerate(chips):
                copy(p, 4 + j, (*chip, 1 - mc), me).wait_recv()
        for cp in sent:
            cp.wait_send()
        for own in mine:
            own.wait()

    return pl.pallas_call(
        body, name=name, in_specs=[ANY] * n, out_specs=[ANY] * n,
        out_shape=[jax.ShapeDtypeStruct((N_DEV,) + a.shape, a.dtype) for a in parts],
        scratch_shapes=[pltpu.SemaphoreType.DMA((7 * n,)), pltpu.SemaphoreType.DMA((7 * n,)),
                        pltpu.SemaphoreType.DMA((n,))],
    )(*parts)


def _pack_rows(parts, width, mult, lead=0):
    out = []
    for a in parts:
        head = a.shape[:lead]
        flat = a.reshape(head + (-1,))
        padn = (-flat.shape[-1]) % (width * mult)
        if padn:
            flat = jnp.pad(flat, [(0, 0)] * lead + [(0, padn)])
        out.append(flat.reshape(head + (-1, width)))
    return jnp.concatenate(out, axis=lead)


def _rows_of(shape, width, mult):
    n = math.prod(shape)
    per = width * mult
    return ((n + per - 1) // per) * mult


def _unpack_rows(buf, shapes, width, mult):
    lead = buf.shape[:-2]
    out, off = [], 0
    for shp in shapes:
        r = _rows_of(shp, width, mult)
        flat = buf[..., off:off + r, :].reshape(lead + (r * width,))
        out.append(flat[..., :math.prod(shp)].reshape(lead + tuple(shp)))
        off += r
    return out


def _cols_from_devices(g):
    nd = g.ndim
    perm = tuple(range(1, nd - 1)) + (0, nd - 1)
    t = jnp.transpose(g, perm)
    return t.reshape(t.shape[:-2] + (t.shape[-2] * t.shape[-1],))


def _cols_to_devices(a):
    c = a.shape[-1] // N_DEV
    t = a.reshape(a.shape[:-1] + (N_DEV, c))
    nd = t.ndim
    perm = (nd - 2,) + tuple(range(0, nd - 2)) + (nd - 1,)
    return jnp.transpose(t, perm)


WIDTH = 1024


def kernel(x, meta, ffn1_wg, ffn1_wu, ffn1_wd, ffn2_wg, ffn2_wu, ffn2_wd, ln_gain, ln_bias, conv_w_in, conv_w, conv_w_out, kv_w, f_bias, attn_w_q, attn_w_o, loss_target, m_meta, m_ffn1_wg, m_ffn1_wu, m_ffn1_wd, m_ffn2_wg, m_ffn2_wu, m_ffn2_wd, m_ln_gain, m_ln_bias, m_conv_w_in, m_conv_w, m_conv_w_out, m_kv_w, m_f_bias, m_attn_w_q, m_attn_w_o, v_meta, v_ffn1_wg, v_ffn1_wu, v_ffn1_wd, v_ffn2_wg, v_ffn2_wu, v_ffn2_wd, v_ln_gain, v_ln_bias, v_conv_w_in, v_conv_w, v_conv_w_out, v_kv_w, v_f_bias, v_attn_w_q, v_attn_w_o):
    depth = ln_gain.shape[0]
    alpha = float((2 * depth) ** 0.25)
    d = x.shape[-1]
    seq = x.shape[1]
    t = ROW0 + seq
    fsh = ffn1_wg.shape[-1]
    f = fsh * N_DEV
    fck = MXU_COLS
    nc = f // fck
    me = 4 * lax.axis_index("x") + 2 * lax.axis_index("y") + lax.axis_index("c")

    def gather_of(parts):
        return [(True, a.astype(BF16)) for a in parts]

    small = [meta, ln_gain, ln_bias, conv_w]
    small_shapes = [a.shape for a in small]
    g1g, g1u, g1d, gcin, gcout, gsmall = _all_gather(
        [a.astype(BF16) for a in (ffn1_wg[0], ffn1_wu[0], ffn1_wd[0], conv_w_in[0], conv_w_out[0])]
        + [_pack_rows(small, WIDTH, F32_ROWS)], "ag_first")
    gmeta, ggain, gbias, gcw = _unpack_rows(gsmall, small_shapes, WIDTH, F32_ROWS)

    def ffn_chunks(gg, gu, gd):
        up = lambda g: jnp.transpose(_cols_from_devices(g).reshape(d, nc, fck), (1, 0, 2))
        return up(gg), up(gu), gd.reshape(nc, fck, d)

    w_in = _cols_from_devices(gcin)
    w_out = gcout.reshape(d, d)
    fb = jnp.pad(f_bias, (0, LANES - N_HEADS)).reshape(1, LANES)
    meta_f = _cols_from_devices(gmeta)
    gain_f = _cols_from_devices(ggain)
    bias_f = _cols_from_devices(gbias)
    cw_f = _cols_from_devices(gcw)[0]

    def gb(l, n):
        return gain_f[l, n].reshape(1, d), bias_f[l, n].reshape(1, d)

    ones = jnp.ones((1, d), F32)
    zeros = jnp.zeros((1, d), F32)

    h0 = jnp.concatenate([jnp.zeros((PAD, d), F32), meta_f, x[0]], axis=0)
    hb0 = _cast_t(h0, "h0_bf16_t")

    w1 = ffn_chunks(g1g, g1u, g1d)
    g00, b00 = gb(0, 0)
    xh1, rs1, hb1, gg1, uu1, g2g, g2u = _ffn_fwd(
        h0, ones, zeros, *w1, g00, b00, alpha, "ffn_fwd_0a", carry=gather_of([ffn2_wg[0], ffn2_wu[0]]))
    g01, b01 = gb(0, 1)
    xh2, rs2, hb2, pp, mb, g2d, gkv = _conv_fwd(
        xh1, g00, b00, w_in, cw_f, w_out, g01, b01, alpha, "conv_fwd", carry=gather_of([ffn2_wd[0], kv_w.T]))
    w2 = ffn_chunks(g2g, g2u, g2d)
    g02, b02 = gb(0, 2)
    xh3, rs3, hb3, gg3, uu3, g3g, g3u = _ffn_fwd(
        xh2, g01, b01, *w2, g02, b02, alpha, "ffn_fwd_0b", carry=gather_of([ffn1_wg[1], ffn1_wu[1]]))
    kvw = gkv.reshape(gkv.shape[0] * gkv.shape[1], d).T
    wk, wv = kvw[:, :d], kvw[:, d:2 * d]
    wf = jnp.pad(kvw[:, 2 * d:], ((0, 0), (0, LANES - N_HEADS)))
    kk, vv, logit, cc, cct, g3d = _kv_fwd(xh3, g02, b02, wk, wv, wf, fb, "kv_fwd",
                                          carry=gather_of([ffn1_wd[1]]))

    w3 = ffn_chunks(g3g, g3u, g3d)
    g10, b10 = gb(1, 0)
    xh4, rs4, hb4, gg4, uu4, gwq = _ffn_fwd(xh3, g02, b02, *w3, g10, b10, alpha, "ffn_fwd_1a",
                                             carry=gather_of([attn_w_q[0]]))
    w_q = gwq.reshape(d, d)
    qq = _proj(xh4, g10, b10, w_q, "q_proj")
    ot, lse, gwo, g4g, g4u, g4d = _attn_fwd(
        qq, kk, vv, cc, cct, "attn_fwd", carry=gather_of([attn_w_o[0], ffn2_wg[1], ffn2_wu[1], ffn2_wd[1]]))
    w_o = gwo.reshape(d, d)
    g11, b11 = gb(1, 1)
    xh5, rs5, hb5 = _attn_out_fwd(ot, xh4, g10, b10, w_o, g11, b11, alpha, "attn_out_fwd")
    w4 = ffn_chunks(g4g, g4u, g4d)
    g12, b12 = gb(1, 2)
    xh6, rs6, _, gg6, uu6 = _ffn_fwd(xh5, g11, b11, *w4, g12, b12, alpha, "ffn_fwd_1b")


    dgain = [[None] * 3 for _ in range(depth)]
    dbias = [[None] * 3 for _ in range(depth)]

    def to_col_owners(g):
        return (False, _cols_to_devices(g).astype(BF16))

    def to_row_owners(g):
        return (False, g.reshape(N_DEV, g.shape[0] // N_DEV, g.shape[1]).astype(BF16))

    dh5, do6, dg6, du6, a6, dgain[1][2], dbias[1][2], loss_l = _ffn_bwd(
        None, xh6, rs6, g12, gg6, uu6, *w4, alpha, "ffn_bwd_1b", loss_target=loss_target[0], loss_bias=b12)
    loss = lax.psum(loss_l[0, 0], ("x", "y", "c"))
    dw4g, dw4u = _wgrad(hb5, [dg6, du6], "wgrad_up_1b")
    (dw4dt,) = _wgrad(do6, [a6], "wgrad_down_1b")

    dres4, dmix5, dot_t, delta, dgain[1][1], dbias[1][1] = _attn_out_bwd(dh5, xh5, rs5, g11, ot, w_o, alpha, "attn_out_bwd")
    (dwo,) = _wgrad(ot, [dmix5], "wgrad_wo")
    dq, dkk, dvv, dcs, drow, l4g, l4u, l4d, lwo = _attn_bwd(
        qq, kk, vv, cc, cct, lse, delta, dot_t, "attn_bwd",
        carry=[to_col_owners(dw4g), to_col_owners(dw4u), to_row_owners(dw4dt.T), to_row_owners(dwo)])
    dh4 = _add_proj_nt(dres4, dq, w_q, "q_bwd")
    (dwq,) = _wgrad(hb4, [dq], "wgrad_wq")

    dh3a, do4, dg4, du4, a4, dgain[1][0], dbias[1][0] = _ffn_bwd(dh4, xh4, rs4, g10, gg4, uu4, *w3, alpha, "ffn_bwd_1a")
    dw3g, dw3u = _wgrad(hb3, [dg4, du4], "wgrad_up_1a")
    (dw3dt,) = _wgrad(do4, [a4], "wgrad_down_1a")

    dcq = jnp.pad(drow[:, :, 0, :].reshape(N_HEADS, t).T, ((0, 0), (0, LANES - N_HEADS)))
    dh3, dlogit, dfb = _kv_bwd(dkk, dvv, dcs, dcq, logit, dh3a, wk, wv, wf, "kv_bwd")
    dwk, dwv = _wgrad(hb3, [dkk, dvv], "wgrad_kv")
    (dwf,) = _wgrad(hb3, [dlogit], "wgrad_f")
    dkv = jnp.concatenate([dwk, dwv, dwf[:, :N_HEADS]], axis=1)

    dh2, do3, dg3, du3, a3, dgain[0][2], dbias[0][2], lwq, l3g, l3u, l3d, lkv = _ffn_bwd(
        dh3, xh3, rs3, g02, gg3, uu3, *w2, alpha, "ffn_bwd_0b",
        carry=[to_row_owners(dwq), to_col_owners(dw3g), to_col_owners(dw3u), to_row_owners(dw3dt.T),
               to_row_owners(dkv.T)])
    dw2g, dw2u = _wgrad(hb2, [dg3, du3], "wgrad_up_0b")
    (dw2dt,) = _wgrad(do3, [a3], "wgrad_down_0b")

    dh1, dmix2, dpp, dcw, dgain[0][1], dbias[0][1] = _conv_bwd(dh2, xh2, rs2, g01, pp, cw_f, w_in, w_out, alpha, "conv_bwd")
    (dwin,) = _wgrad(hb1, [dpp], "wgrad_conv_in")
    (dwout,) = _wgrad(mb, [dmix2], "wgrad_conv_out")

    dh0, do1, dg1, du1, a1, dgain[0][0], dbias[0][0], l2g, l2u, l2d, lcin, lcout = _ffn_bwd(
        dh1, xh1, rs1, g00, gg1, uu1, *w1, alpha, "ffn_bwd_0a",
        carry=[to_col_owners(dw2g), to_col_owners(dw2u), to_row_owners(dw2dt.T), to_col_owners(dwin),
               to_row_owners(dwout)])
    (dw1dt,) = _wgrad(do1, [a1], "wgrad_down_0a")
    dw1g, l1d = _wgrad(hb0, [dg1], "wgrad_upg_0a", carry=[to_row_owners(dw1dt.T)])
    dw1u, l1g = _wgrad(hb0, [du1], "wgrad_upu_0a", carry=[to_col_owners(dw1g)])
    dmeta = dh0[PAD:ROW0]
    dgain_f = jnp.stack([jnp.concatenate(r, axis=0) for r in dgain])
    dbias_f = jnp.stack([jnp.concatenate(r, axis=0) for r in dbias])
    small_full = [dmeta, dgain_f, dbias_f, dcw[None], dfb]
    small_full_shapes = [a.shape for a in small_full]
    l1u, gsmall_grads = _exchange([to_col_owners(dw1u), (True, _pack_rows(small_full, WIDTH, F32_ROWS))], "rs_last")

    grad_x = dh0[ROW0:].reshape(1, seq, d)
    rsmall = _sum_sources(gsmall_grads, "small_sum")
    smeta, sgain, sbias, scw, sfb = _unpack_rows(rsmall, small_full_shapes, WIDTH, F32_ROWS)
    csh = d // N_DEV

    def my_cols(a):
        return lax.dynamic_slice_in_dim(a, me * csh, csh, axis=a.ndim - 1)

    grads = {"meta": my_cols(smeta), "ln_gain": my_cols(sgain), "ln_bias": my_cols(sbias),
             "conv_w": my_cols(scw), "f_bias": sfb[0, :N_HEADS], "kv_w": _sum_sources(lkv, "kv_sum").T}
    landed = {"ffn1_wg": [l1g, l3g], "ffn1_wu": [l1u, l3u], "ffn1_wd": [l1d, l3d],
              "ffn2_wg": [l2g, l4g], "ffn2_wu": [l2u, l4u], "ffn2_wd": [l2d, l4d],
              "conv_w_in": [lcin], "conv_w_out": [lcout], "attn_w_q": [lwq], "attn_w_o": [lwo]}
    weights = dict(meta=meta, ffn1_wg=ffn1_wg, ffn1_wu=ffn1_wu, ffn1_wd=ffn1_wd, ffn2_wg=ffn2_wg,
                   ffn2_wu=ffn2_wu, ffn2_wd=ffn2_wd, ln_gain=ln_gain, ln_bias=ln_bias,
                   conv_w_in=conv_w_in, conv_w=conv_w, conv_w_out=conv_w_out, kv_w=kv_w,
                   f_bias=f_bias, attn_w_q=attn_w_q, attn_w_o=attn_w_o)
    moms = dict(meta=(m_meta, v_meta), ffn1_wg=(m_ffn1_wg, v_ffn1_wg), ffn1_wu=(m_ffn1_wu, v_ffn1_wu),
                ffn1_wd=(m_ffn1_wd, v_ffn1_wd), ffn2_wg=(m_ffn2_wg, v_ffn2_wg), ffn2_wu=(m_ffn2_wu, v_ffn2_wu),
                ffn2_wd=(m_ffn2_wd, v_ffn2_wd), ln_gain=(m_ln_gain, v_ln_gain), ln_bias=(m_ln_bias, v_ln_bias),
                conv_w_in=(m_conv_w_in, v_conv_w_in), conv_w=(m_conv_w, v_conv_w),
                conv_w_out=(m_conv_w_out, v_conv_w_out), kv_w=(m_kv_w, v_kv_w), f_bias=(m_f_bias, v_f_bias),
                attn_w_q=(m_attn_w_q, v_attn_w_q), attn_w_o=(m_attn_w_o, v_attn_w_o))

    names = list(weights)
    g_out, d_out, m_out, v_out = [], [], [], []
    for n in names:
        w = weights[n]
        shp = w.shape
        mm, vv_ = moms[n]
        if n in landed:
            three = (len(landed[n]),) + shp[-2:]
            g, dl, nm, nv = _reduce_adamw(w.reshape(three), mm.reshape(three), vv_.reshape(three),
                                          landed[n], "adamw_" + n)
            g = g.reshape(shp)
        else:
            two = (1, shp[0]) if w.ndim == 1 else (math.prod(shp[:-1]), shp[-1])
            g = grads[n].reshape(shp)
            dl, nm, nv = _adamw(w.reshape(two), g.reshape(two), mm.reshape(two), vv_.reshape(two), "adamw_" + n)
        g_out.append(g)
        d_out.append(dl.reshape(shp))
        m_out.append(nm.reshape(shp))
        v_out.append(nv.reshape(shp))
    return (loss, grad_x, *g_out, *d_out, *m_out, *v_out)
```

```python
import functools
import math

import jax
import jax.numpy as jnp
from jax import lax
from jax.experimental import pallas as pl
from jax.experimental.pallas import tpu as pltpu

F32 = jnp.float32
BF16 = jnp.bfloat16

N_DEV = 8
N_HEADS = 8
N_META = 16
PAD = 112
ROW0 = PAD + N_META
LN_EPS = 1e-5
NEG_INF = -1e30
LOG2E = 1.4426950408889634
ATTN_HEADS_PER_STEP = 8
ATTN_BWD_HEADS_PER_STEP = 2
LANES = 128
MXU_COLS = 256
FFN_FWD_CHUNKS = 11
FFN_BWD_CHUNKS = 4

ADAM_LR = 0.001
ADAM_B1 = 0.9
ADAM_B2 = 0.999
ADAM_EPS = 1e-08
ADAM_WD = 0.01
ADAM_STEP = 10

ROW_TILES = (640, 128)
LOSS_TILE = 128
BF16_ROWS = 16
F32_ROWS = 8
SUM_ROWS_MAX = 768
ADAM_ROWS_MAX = 256
VMEM_BIG = 56 << 20
VMEM_MID = 40 << 20

ANY = pl.BlockSpec(memory_space=pl.ANY)
MESH = pl.DeviceIdType.MESH


def _row_tile(t):
    for c in ROW_TILES:
        if t % c == 0:
            return c
    raise ValueError(f"no row tile for {t}")


def _dot(a, b):
    return jnp.dot(a, b, preferred_element_type=F32)


def _dot_nt(a, b):
    return lax.dot_general(a, b, (((1,), (1,)), ((), ())), preferred_element_type=F32)


def _dot_tn(a, b):
    return lax.dot_general(a, b, (((0,), (0,)), ((), ())), preferred_element_type=F32)


def _params(sem, vmem):
    return pltpu.CompilerParams(dimension_semantics=sem, vmem_limit_bytes=vmem)


def _ln_fwd(z):
    mu = jnp.mean(z, axis=-1, keepdims=True)
    zc = z - mu
    var = jnp.mean(zc * zc, axis=-1, keepdims=True)
    rstd = lax.rsqrt(var + LN_EPS)
    return zc * rstd, rstd


def _ln_bwd(dh, xhat, rstd, gain):
    dxh = dh * gain
    m1 = jnp.mean(dxh, axis=-1, keepdims=True)
    m2 = jnp.mean(dxh * xhat, axis=-1, keepdims=True)
    dz = rstd * (dxh - m1 - xhat * m2)
    return dz, jnp.sum(dh * xhat, axis=0, keepdims=True), jnp.sum(dh, axis=0, keepdims=True)


def _load_resident(pairs, sems):
    cps = [pltpu.make_async_copy(src, dst, sems.at[k]) for k, (src, dst) in enumerate(pairs)]
    for cp in cps:
        cp.start()
    for cp in cps:
        cp.wait()


def _peer_ids():
    mx, my, mc = lax.axis_index("x"), lax.axis_index("y"), lax.axis_index("c")
    peers = []
    for kk in range(1, N_DEV):
        px = 1 - mx if (kk >> 2) & 1 else mx
        py = 1 - my if (kk >> 1) & 1 else my
        pc = 1 - mc if kk & 1 else mc
        peers.append(((px, py, pc), 4 * px + 2 * py + pc))
    return 4 * mx + 2 * my + mc, peers


def _exchange_copies(jobs, send_sems, recv_sems, local_sems, starting):
    me_id, peers = _peer_ids()
    for n, (gather, src, dst) in enumerate(jobs):
        own = pltpu.make_async_copy(src if gather else src.at[me_id], dst.at[me_id], local_sems.at[n])
        own.start() if starting else own.wait()
        for k, (dev, pid) in enumerate(peers):
            sem = (N_DEV - 1) * n + k
            out = src if gather else src.at[pid]
            send = pltpu.make_async_remote_copy(
                src_ref=out, dst_ref=dst.at[me_id], send_sem=send_sems.at[sem], recv_sem=recv_sems.at[sem],
                device_id=dev, device_id_type=MESH)
            if starting:
                send.start()
            else:
                pltpu.make_async_remote_copy(
                    src_ref=out, dst_ref=dst.at[pid], send_sem=send_sems.at[sem], recv_sem=recv_sems.at[sem],
                    device_id=dev, device_id_type=MESH).wait_recv()
                send.wait_send()


def _carried(body, n_in, n_out, carry, first, last):
    nj = len(carry)
    if nj == 0:
        return body

    def wrapped(*refs):
        ins, srcs = refs[:n_in], refs[n_in:n_in + nj]
        outs = refs[n_in + nj:n_in + nj + n_out]
        dsts = refs[n_in + nj + n_out:n_in + 2 * nj + n_out]
        scratch, sems = refs[n_in + 2 * nj + n_out:-3], refs[-3:]
        jobs = [(g, s, r) for (g, _), s, r in zip(carry, srcs, dsts)]

        @pl.when(first())
        def _():
            _exchange_copies(jobs, *sems, starting=True)

        body(*ins, *outs, *scratch)

        @pl.when(last())
        def _():
            _exchange_copies(jobs, *sems, starting=False)

    return wrapped


def _carry_shapes(carry):
    return [jax.ShapeDtypeStruct((N_DEV,) + a.shape if g else a.shape, a.dtype) for g, a in carry]


def _carry_scratch(carry):
    if not carry:
        return []
    n = len(carry)
    return [pltpu.SemaphoreType.DMA(((N_DEV - 1) * n,)), pltpu.SemaphoreType.DMA(((N_DEV - 1) * n,)),
            pltpu.SemaphoreType.DMA((n,))]


def _exchange(carry, name):
    n = len(carry)

    def body(*refs):
        jobs = [(g, s, r) for (g, _), s, r in zip(carry, refs[:n], refs[n:2 * n])]
        _exchange_copies(jobs, *refs[2 * n:], starting=True)
        _exchange_copies(jobs, *refs[2 * n:], starting=False)

    return pl.pallas_call(
        body, name=name, in_specs=[ANY] * n, out_specs=[ANY] * n, out_shape=_carry_shapes(carry),
        scratch_shapes=_carry_scratch(carry),
    )(*[a for _, a in carry])


def _ffn_fwd(xh, gi, bi, wg, wu, wd, go, bo, alpha, name, carry=(), input_t=False):
    t, d = xh.shape
    nch, _, fc = wg.shape
    f = nch * fc
    per = min(FFN_FWD_CHUNKS, nch)
    nc = -(-nch // per)
    tm = _row_tile(t)
    nt = t // tm

    def body(xh_ref, gi_ref, bi_ref, wg_hbm, wu_hbm, wd_hbm, go_ref, bo_ref,
             xo_ref, rs_ref, hb_ref, g_ref, u_ref, *tail):
        hin_ref = tail[0] if input_t else None
        wg_v, wu_v, wd_v, acc, hbs, sems = tail[1:] if input_t else tail
        i = pl.program_id(0)
        c = pl.program_id(1)

        @pl.when((i == 0) & (c == 0))
        def _():
            _load_resident([(wg_hbm, wg_v), (wu_hbm, wu_v), (wd_hbm, wd_v)], sems)

        @pl.when(c == 0)
        def _():
            h = xh_ref[...] * gi_ref[...] + bi_ref[...]
            hbs[...] = h.astype(BF16)
            acc[...] = jnp.zeros_like(acc)
            if input_t:
                hin_ref[...] = hbs[...].T

        def chunk(k):
            ck = c * per + k
            cols = slice(k * fc, (k + 1) * fc)
            hb = hbs[...]
            g = _dot(hb, wg_v[ck])
            u = _dot(hb, wu_v[ck])
            a = (g * jax.nn.sigmoid(g)) * u
            g_ref[:, cols] = g.astype(BF16)
            u_ref[:, cols] = u.astype(BF16)
            acc[...] += _dot(a.astype(BF16), wd_v[ck])

        for k in range(per):
            if (nc - 1) * per + k < nch:
                chunk(k)
            else:
                pl.when(c * per + k < nch)(functools.partial(chunk, k))

        @pl.when(c == nc - 1)
        def _():
            h = xh_ref[...] * gi_ref[...] + bi_ref[...]
            xhat, rstd = _ln_fwd(alpha * h + 0.5 * acc[...])
            xo_ref[...] = xhat
            rs_ref[...] = rstd
            hb_ref[...] = (xhat * go_ref[...] + bo_ref[...]).astype(BF16).T

    row = pl.BlockSpec((tm, d), lambda i, c: (i, 0))
    vec = pl.BlockSpec((1, d), lambda i, c: (0, 0))
    chunk = pl.BlockSpec((tm, per * fc), lambda i, c: (i, c))
    first = lambda: (pl.program_id(0) == 0) & (pl.program_id(1) == 0)
    last = lambda: (pl.program_id(0) == nt - 1) & (pl.program_id(1) == nc - 1)
    col = pl.BlockSpec((d, tm), lambda i, c: (0, i))
    t_spec, t_shape = ([col], [jax.ShapeDtypeStruct((d, t), BF16)]) if input_t else ([], [])
    return pl.pallas_call(
        _carried(body, 8, 5 + len(t_spec), carry, first, last), name=name, grid=(nt, nc),
        in_specs=[row, vec, vec, ANY, ANY, ANY, vec, vec] + [ANY] * len(carry),
        out_specs=[row, pl.BlockSpec((tm, 1), lambda i, c: (i, 0)), col, chunk, chunk] + t_spec
                  + [ANY] * len(carry),
        out_shape=[jax.ShapeDtypeStruct((t, d), F32), jax.ShapeDtypeStruct((t, 1), F32),
                   jax.ShapeDtypeStruct((d, t), BF16), jax.ShapeDtypeStruct((t, f), BF16),
                   jax.ShapeDtypeStruct((t, f), BF16)] + t_shape + _carry_shapes(carry),
        scratch_shapes=[pltpu.VMEM((nch, d, fc), BF16), pltpu.VMEM((nch, d, fc), BF16),
                        pltpu.VMEM((nch, fc, d), BF16), pltpu.VMEM((tm, d), F32),
                        pltpu.VMEM((tm, d), BF16), pltpu.SemaphoreType.DMA((3,))] + _carry_scratch(carry),
        compiler_params=_params(("arbitrary", "arbitrary"), VMEM_BIG),
    )(xh, gi, bi, wg, wu, wd, go, bo, *[a for _, a in carry])


def _ffn_bwd(dh, xo, rs, go, gs, us, wg, wu, wd, alpha, name, carry=(), loss_target=None, loss_bias=None):
    t, d = xo.shape
    nch, _, fc = wg.shape
    f = nch * fc
    per = min(FFN_BWD_CHUNKS, nch)
    nc = -(-nch // per)
    tm = _row_tile(t)
    nt = t // tm

    with_loss = loss_target is not None
    nsub, lead = tm // LOSS_TILE, ROW0 // LOSS_TILE
    nlead = nsub + 1 if with_loss else 1

    def body(*refs):
        lead_refs = refs[:nlead]
        xo_ref, rs_ref, go_ref, g_ref, u_ref, wg_hbm, wu_hbm, wd_hbm = refs[nlead:nlead + 8]
        dhin_ref, dot_ref, dg_ref, du_ref, a_ref, dgain_ref, dbias_ref = refs[nlead + 8:nlead + 15]
        rest = refs[nlead + 15:]
        loss_ref, rest = (rest[0], rest[1:]) if with_loss else (None, rest)
        wg_v, wu_v, wd_v, do_ref, sems = rest[:5]
        i = pl.program_id(0)
        c = pl.program_id(1)

        @pl.when((i == 0) & (c == 0))
        def _():
            _load_resident([(wg_hbm, wg_v), (wu_hbm, wu_v), (wd_hbm, wd_v)], sems)
            dgain_ref[...] = jnp.zeros_like(dgain_ref)
            dbias_ref[...] = jnp.zeros_like(dbias_ref)
            if with_loss:
                rest[5][...] = jnp.zeros_like(rest[5])

        def tile_dh():
            if not with_loss:
                return lead_refs[0][...]
            part = rest[5]
            for k in range(nsub):
                sl = slice(k * LOSS_TILE, (k + 1) * LOSS_TILE)
                rows = i * tm + k * LOSS_TILE + lax.broadcasted_iota(jnp.int32, (LOSS_TILE, 1), 0)
                y = xo_ref[sl, :] * go_ref[...] + lead_refs[nsub][...]
                e = jnp.where(rows >= ROW0, y - lead_refs[k][...], 0.0)
                part[...] += jnp.sum(e * e, axis=0, keepdims=True)
                dhin_ref[sl, :] = e * (1.0 / d)

            @pl.when(i == nt - 1)
            def _():
                loss_ref[...] = jnp.full((1, LANES), 0.5 / d, F32) * jnp.sum(part[...])

            return dhin_ref[...]

        @pl.when(c == 0)
        def _():
            dz, dgp, dbp = _ln_bwd(tile_dh(), xo_ref[...], rs_ref[...], go_ref[...])
            dgain_ref[...] += dgp
            dbias_ref[...] += dbp
            dob = (0.5 * dz).astype(BF16)
            do_ref[...] = dob
            dot_ref[...] = dob.T
            dhin_ref[...] = alpha * dz

        def chunk(k):
            ck = c * per + k
            cols = slice(k * fc, (k + 1) * fc)
            g = g_ref[:, cols].astype(F32)
            u = u_ref[:, cols].astype(F32)
            sg = jax.nn.sigmoid(g)
            sl = g * sg
            da = _dot_nt(do_ref[...], wd_v[ck])
            dgb = (da * u * (sg * (1.0 + g * (1.0 - sg)))).astype(BF16)
            dub = (da * sl).astype(BF16)
            a_ref[:, cols] = (sl * u).astype(BF16)
            dg_ref[:, cols] = dgb
            du_ref[:, cols] = dub
            dhin_ref[...] += _dot_nt(dgb, wg_v[ck]) + _dot_nt(dub, wu_v[ck])

        for k in range(per):
            if (nc - 1) * per + k < nch:
                chunk(k)
            else:
                pl.when(c * per + k < nch)(functools.partial(chunk, k))

    row = pl.BlockSpec((tm, d), lambda i, c: (i, 0))
    vec = pl.BlockSpec((1, d), lambda i, c: (0, 0))
    chunk = pl.BlockSpec((tm, per * fc), lambda i, c: (i, c))
    first = lambda: (pl.program_id(0) == 0) & (pl.program_id(1) == 0)
    last = lambda: (pl.program_id(0) == nt - 1) & (pl.program_id(1) == nc - 1)
    if with_loss:
        lead_specs = [pl.BlockSpec((LOSS_TILE, d), lambda i, c, k=k: (jnp.maximum(i * nsub + k - lead, 0), 0))
                      for k in range(nsub)] + [vec]
        lead_args = [loss_target] * nsub + [loss_bias]
        loss_spec, loss_shape = [pl.BlockSpec((1, LANES), lambda i, c: (0, 0))], [jax.ShapeDtypeStruct((1, LANES), F32)]
        loss_scratch = [pltpu.VMEM((1, d), F32)]
    else:
        lead_specs, lead_args, loss_spec, loss_shape, loss_scratch = [row], [dh], [], [], []
    return pl.pallas_call(
        _carried(body, nlead + 8, 7 + len(loss_spec), carry, first, last), name=name, grid=(nt, nc),
        in_specs=lead_specs + [row, pl.BlockSpec((tm, 1), lambda i, c: (i, 0)), vec, chunk, chunk,
                               ANY, ANY, ANY] + [ANY] * len(carry),
        out_specs=[row, pl.BlockSpec((d, tm), lambda i, c: (0, i)), chunk, chunk, chunk, vec, vec]
                  + loss_spec + [ANY] * len(carry),
        out_shape=[jax.ShapeDtypeStruct((t, d), F32), jax.ShapeDtypeStruct((d, t), BF16),
                   jax.ShapeDtypeStruct((t, f), BF16), jax.ShapeDtypeStruct((t, f), BF16),
                   jax.ShapeDtypeStruct((t, f), BF16), jax.ShapeDtypeStruct((1, d), F32),
                   jax.ShapeDtypeStruct((1, d), F32)] + loss_shape + _carry_shapes(carry),
        scratch_shapes=[pltpu.VMEM((nch, d, fc), BF16), pltpu.VMEM((nch, d, fc), BF16),
                        pltpu.VMEM((nch, fc, d), BF16), pltpu.VMEM((tm, d), BF16),
                        pltpu.SemaphoreType.DMA((3,))] + loss_scratch + _carry_scratch(carry),
        compiler_params=_params(("arbitrary", "arbitrary"), VMEM_BIG),
    )(*lead_args, xo, rs, go, gs, us, wg, wu, wd, *[a for _, a in carry])


def _wgrad(xt, ys, name, carry=()):
    m, t = xt.shape
    n = ys[0].shape[1]
    tn = min(n, MXU_COLS)
    ny = len(ys)

    def body(*refs):
        x_hbm = refs[0]
        y_refs = refs[1:1 + ny]
        o_refs = refs[1 + ny:1 + 2 * ny]
        xv, sems = refs[1 + 2 * ny:]

        @pl.when(pl.program_id(0) == 0)
        def _():
            _load_resident([(x_hbm, xv)], sems)

        for y_ref, o_ref in zip(y_refs, o_refs):
            o_ref[...] = _dot(xv[...], y_ref[...].astype(BF16)).astype(BF16)

    steps = n // tn
    first = lambda: pl.program_id(0) == 0
    last = lambda: pl.program_id(0) == steps - 1
    return pl.pallas_call(
        _carried(body, 1 + ny, ny, carry, first, last), name=name, grid=(steps,),
        in_specs=[ANY] + [pl.BlockSpec((t, tn), lambda c: (0, c)) for _ in ys] + [ANY] * len(carry),
        out_specs=[pl.BlockSpec((m, tn), lambda c: (0, c)) for _ in ys] + [ANY] * len(carry),
        out_shape=[jax.ShapeDtypeStruct((m, n), BF16) for _ in ys] + _carry_shapes(carry),
        scratch_shapes=[pltpu.VMEM((m, t), BF16), pltpu.SemaphoreType.DMA((1,))] + _carry_scratch(carry),
        compiler_params=_params(("arbitrary",), VMEM_BIG),
    )(xt, *ys, *[a for _, a in carry])


def _shift_rows(u, halo, tm):
    r = lax.broadcasted_iota(jnp.int32, (tm, 1), 0)
    u1 = jnp.where(r == 0, halo[7:8], pltpu.roll(u, 1, 0))
    u2 = jnp.where(r == 0, halo[6:7], jnp.where(r == 1, halo[7:8], pltpu.roll(u, 2, 0)))
    return u1, u2


def _conv_fwd(xh, gi, bi, w_in, cw, w_out, go, bo, alpha, name, carry=()):
    t, d = xh.shape
    tm = _row_tile(t)
    nt = t // tm

    def body(xh_ref, gi_ref, bi_ref, win_ref, cw_ref, wout_ref, go_ref, bo_ref,
             xo_ref, rs_ref, hb_ref, p_ref, m_ref, halo):
        i = pl.program_id(0)

        @pl.when(i == 0)
        def _():
            halo[...] = jnp.zeros_like(halo)

        h = xh_ref[...] * gi_ref[...] + bi_ref[...]
        hb = h.astype(BF16)
        bg = _dot(hb, win_ref[:, 0:d])
        cg = _dot(hb, win_ref[:, d:2 * d])
        val = _dot(hb, win_ref[:, 2 * d:3 * d])
        p_ref[:, 0:d] = bg.astype(BF16)
        p_ref[:, d:2 * d] = cg.astype(BF16)
        p_ref[:, 2 * d:3 * d] = val.astype(BF16)
        rows = i * tm + lax.broadcasted_iota(jnp.int32, (tm, 1), 0)
        u = jnp.where(rows >= PAD, cg * val, 0.0)
        u1, u2 = _shift_rows(u, halo[...], tm)
        halo[...] = u[tm - 8:tm]
        y = cw_ref[0:1] * u2 + cw_ref[1:2] * u1 + cw_ref[2:3] * u
        mb = (bg * y).astype(BF16)
        m_ref[...] = mb.T
        xhat, rstd = _ln_fwd(alpha * h + _dot(mb, wout_ref[...]))
        xo_ref[...] = xhat
        rs_ref[...] = rstd
        hb_ref[...] = (xhat * go_ref[...] + bo_ref[...]).astype(BF16).T

    row = pl.BlockSpec((tm, d), lambda i: (i, 0))
    col = pl.BlockSpec((d, tm), lambda i: (0, i))
    vec = pl.BlockSpec((1, d), lambda i: (0, 0))
    first = lambda: pl.program_id(0) == 0
    last = lambda: pl.program_id(0) == nt - 1
    return pl.pallas_call(
        _carried(body, 8, 5, carry, first, last), name=name, grid=(nt,),
        in_specs=[row, vec, vec, pl.BlockSpec((d, 3 * d), lambda i: (0, 0)),
                  pl.BlockSpec((3, d), lambda i: (0, 0)), pl.BlockSpec((d, d), lambda i: (0, 0)),
                  vec, vec] + [ANY] * len(carry),
        out_specs=[row, pl.BlockSpec((tm, 1), lambda i: (i, 0)), col,
                   pl.BlockSpec((tm, 3 * d), lambda i: (i, 0)), col] + [ANY] * len(carry),
        out_shape=[jax.ShapeDtypeStruct((t, d), F32), jax.ShapeDtypeStruct((t, 1), F32),
                   jax.ShapeDtypeStruct((d, t), BF16), jax.ShapeDtypeStruct((t, 3 * d), BF16),
                   jax.ShapeDtypeStruct((d, t), BF16)] + _carry_shapes(carry),
        scratch_shapes=[pltpu.VMEM((8, d), F32)] + _carry_scratch(carry),
        compiler_params=_params(("arbitrary",), VMEM_BIG),
    )(xh, gi, bi, w_in, cw, w_out, go, bo, *[a for _, a in carry])


def _conv_bwd(dh, xo, rs, go, p, cw, w_in, w_out, alpha, name):
    t, d = dh.shape
    tm = _row_tile(t)
    nt = t // tm
    tb = tm // 8

    def body(dh_ref, xo_ref, rs_ref, go_ref, p_ref, ph_ref, cw_ref, win_ref, wout_ref,
             dhin_ref, dmix_ref, dp_ref, dcw_ref, dgain_ref, dbias_ref, carry):
        i = pl.program_id(0)
        tile = nt - 1 - i

        @pl.when(i == 0)
        def _():
            carry[...] = jnp.zeros_like(carry)
            dcw_ref[...] = jnp.zeros_like(dcw_ref)
            dgain_ref[...] = jnp.zeros_like(dgain_ref)
            dbias_ref[...] = jnp.zeros_like(dbias_ref)

        dz, dgp, dbp = _ln_bwd(dh_ref[...], xo_ref[...], rs_ref[...], go_ref[...])
        dgain_ref[...] += dgp
        dbias_ref[...] += dbp
        dmixb = dz.astype(BF16)
        dmix_ref[...] = dmixb
        dm = _dot_nt(dmixb, wout_ref[...])

        bg = p_ref[:, 0:d].astype(F32)
        cg = p_ref[:, d:2 * d].astype(F32)
        val = p_ref[:, 2 * d:3 * d].astype(F32)
        rows = tile * tm + lax.broadcasted_iota(jnp.int32, (tm, 1), 0)
        valid = rows >= PAD
        u = jnp.where(valid, cg * val, 0.0)
        hrows = tile * tm - 8 + lax.broadcasted_iota(jnp.int32, (8, 1), 0)
        hu = jnp.where((hrows >= PAD) & (tile > 0),
                       ph_ref[:, d:2 * d].astype(F32) * ph_ref[:, 2 * d:3 * d].astype(F32), 0.0)
        u1, u2 = _shift_rows(u, hu, tm)
        w0, w1, w2 = cw_ref[0:1], cw_ref[1:2], cw_ref[2:3]
        y = w0 * u2 + w1 * u1 + w2 * u
        dbg = dm * y
        dy = dm * bg
        dcw_ref[0:1] += jnp.sum(dy * u2, axis=0, keepdims=True)
        dcw_ref[1:2] += jnp.sum(dy * u1, axis=0, keepdims=True)
        dcw_ref[2:3] += jnp.sum(dy * u, axis=0, keepdims=True)

        nxt = carry[...]
        r = lax.broadcasted_iota(jnp.int32, (tm, 1), 0)
        dy1 = jnp.where(r == tm - 1, nxt[0:1], pltpu.roll(dy, tm - 1, 0))
        dy2 = jnp.where(r == tm - 2, nxt[0:1],
                        jnp.where(r == tm - 1, nxt[1:2], pltpu.roll(dy, tm - 2, 0)))
        carry[...] = dy[0:8]
        du = jnp.where(valid, w2 * dy + w1 * dy1 + w0 * dy2, 0.0)
        dbgb = dbg.astype(BF16)
        dcgb = (du * val).astype(BF16)
        dvalb = (du * cg).astype(BF16)
        dp_ref[:, 0:d] = dbgb
        dp_ref[:, d:2 * d] = dcgb
        dp_ref[:, 2 * d:3 * d] = dvalb
        dhin_ref[...] = (alpha * dz + _dot_nt(dbgb, win_ref[:, 0:d])
                         + _dot_nt(dcgb, win_ref[:, d:2 * d]) + _dot_nt(dvalb, win_ref[:, 2 * d:3 * d]))

    row = pl.BlockSpec((tm, d), lambda i: (nt - 1 - i, 0))
    vec = pl.BlockSpec((1, d), lambda i: (0, 0))
    prow = pl.BlockSpec((tm, 3 * d), lambda i: (nt - 1 - i, 0))
    return pl.pallas_call(
        body, name=name, grid=(nt,),
        in_specs=[row, row, pl.BlockSpec((tm, 1), lambda i: (nt - 1 - i, 0)), vec, prow,
                  pl.BlockSpec((8, 3 * d), lambda i: (jnp.maximum((nt - 1 - i) * tb - 1, 0), 0)),
                  pl.BlockSpec((3, d), lambda i: (0, 0)),
                  pl.BlockSpec((d, 3 * d), lambda i: (0, 0)), pl.BlockSpec((d, d), lambda i: (0, 0))],
        out_specs=[row, row, prow, pl.BlockSpec((3, d), lambda i: (0, 0)), vec, vec],
        out_shape=[jax.ShapeDtypeStruct((t, d), F32), jax.ShapeDtypeStruct((t, d), BF16),
                   jax.ShapeDtypeStruct((t, 3 * d), BF16), jax.ShapeDtypeStruct((3, d), F32),
                   jax.ShapeDtypeStruct((1, d), F32), jax.ShapeDtypeStruct((1, d), F32)],
        scratch_shapes=[pltpu.VMEM((8, d), F32)],
        compiler_params=_params(("arbitrary",), VMEM_BIG),
    )(dh, xo, rs, go, p, p, cw, w_in, w_out)


def _kv_fwd(xh, gi, bi, wk, wv, wf, fb, name, carry=()):
    t, d = xh.shape
    tm = _row_tile(t)
    nt = t // tm

    def body(xh_ref, gi_ref, bi_ref, wk_ref, wv_ref, wf_ref, fb_ref,
             k_ref, v_ref, lg_ref, c_ref, ct_ref, run):
        i = pl.program_id(0)

        @pl.when(i == 0)
        def _():
            run[...] = jnp.zeros_like(run)

        x = (xh_ref[...] * gi_ref[...] + bi_ref[...]).astype(BF16)
        k_ref[...] = _dot(x, wk_ref[...]).astype(BF16)
        v_ref[...] = _dot(x, wv_ref[...]).astype(BF16)
        logit = _dot(x, wf_ref[...]) + fb_ref[...]
        lg_ref[...] = logit
        logf = jnp.minimum(logit, 0.0) - jnp.log(1.0 + jnp.exp(-jnp.abs(logit)))
        rows = i * tm + lax.broadcasted_iota(jnp.int32, (tm, 1), 0)
        logf = jnp.where(rows >= PAD, logf, 0.0)
        tri = (lax.broadcasted_iota(jnp.int32, (tm, tm), 0)
               >= lax.broadcasted_iota(jnp.int32, (tm, tm), 1)).astype(F32)
        cs = jnp.dot(tri, logf, precision=lax.Precision.HIGHEST, preferred_element_type=F32) + run[...]
        run[...] = cs[tm - 1:tm]
        c_ref[...] = cs
        ct_ref[...] = cs.T

    row = pl.BlockSpec((tm, d), lambda i: (i, 0))
    vec = pl.BlockSpec((1, d), lambda i: (0, 0))
    gate = pl.BlockSpec((tm, LANES), lambda i: (i, 0))
    sq = pl.BlockSpec((d, d), lambda i: (0, 0))
    first = lambda: pl.program_id(0) == 0
    last = lambda: pl.program_id(0) == nt - 1
    return pl.pallas_call(
        _carried(body, 7, 5, carry, first, last), name=name, grid=(nt,),
        in_specs=[row, vec, vec, sq, sq, pl.BlockSpec((d, LANES), lambda i: (0, 0)),
                  pl.BlockSpec((1, LANES), lambda i: (0, 0))] + [ANY] * len(carry),
        out_specs=[row, row, gate, gate, pl.BlockSpec((LANES, tm), lambda i: (0, i))] + [ANY] * len(carry),
        out_shape=[jax.ShapeDtypeStruct((t, d), BF16), jax.ShapeDtypeStruct((t, d), BF16),
                   jax.ShapeDtypeStruct((t, LANES), F32), jax.ShapeDtypeStruct((t, LANES), F32),
                   jax.ShapeDtypeStruct((LANES, t), F32)] + _carry_shapes(carry),
        scratch_shapes=[pltpu.VMEM((1, LANES), F32)] + _carry_scratch(carry),
        compiler_params=_params(("arbitrary",), VMEM_MID),
    )(xh, gi, bi, wk, wv, wf, fb, *[a for _, a in carry])


def _kv_bwd(dk, dv, dcs, dcq, logit, dh_other, wk, wv, wf, name):
    t, d = dk.shape
    tm = _row_tile(t)
    nt = t // tm

    def body(dk_ref, dv_ref, dcs_ref, dcq_ref, lg_ref, oth_ref, wk_ref, wv_ref, wf_ref,
             dh_ref, dl_ref, dfb_ref, run):
        i = pl.program_id(0)
        tile = nt - 1 - i

        @pl.when(i == 0)
        def _():
            run[...] = jnp.zeros_like(run)
            dfb_ref[...] = jnp.zeros_like(dfb_ref)

        lane = lax.broadcasted_iota(jnp.int32, (tm, LANES), 1)
        dc = dcq_ref[...]
        for hh in range(N_HEADS):
            dc = dc + jnp.where(lane == hh, jnp.sum(dcs_ref[hh], axis=1, keepdims=True), 0.0)
        tri = (lax.broadcasted_iota(jnp.int32, (tm, tm), 0)
               <= lax.broadcasted_iota(jnp.int32, (tm, tm), 1)).astype(F32)
        dlf = jnp.dot(tri, dc, precision=lax.Precision.HIGHEST, preferred_element_type=F32) + run[...]
        run[...] = dlf[0:1]
        rows = tile * tm + lax.broadcasted_iota(jnp.int32, (tm, 1), 0)
        dlogit = jnp.where(rows >= PAD, dlf * jax.nn.sigmoid(-lg_ref[...]), 0.0)
        dfb_ref[...] += jnp.sum(dlogit, axis=0, keepdims=True)
        dlb = dlogit.astype(BF16)
        dl_ref[...] = dlb
        dh_ref[...] = (oth_ref[...] + _dot_nt(dk_ref[...], wk_ref[...])
                       + _dot_nt(dv_ref[...], wv_ref[...]) + _dot_nt(dlb, wf_ref[...]))

    row = pl.BlockSpec((tm, d), lambda i: (nt - 1 - i, 0))
    gate = pl.BlockSpec((tm, LANES), lambda i: (nt - 1 - i, 0))
    sq = pl.BlockSpec((d, d), lambda i: (0, 0))
    return pl.pallas_call(
        body, name=name, grid=(nt,),
        in_specs=[row, row, pl.BlockSpec((N_HEADS, tm, LANES), lambda i: (0, nt - 1 - i, 0)), gate, gate, row,
                  sq, sq, pl.BlockSpec((d, LANES), lambda i: (0, 0))],
        out_specs=[row, gate, pl.BlockSpec((1, LANES), lambda i: (0, 0))],
        out_shape=[jax.ShapeDtypeStruct((t, d), F32), jax.ShapeDtypeStruct((t, LANES), BF16),
                   jax.ShapeDtypeStruct((1, LANES), F32)],
        scratch_shapes=[pltpu.VMEM((1, LANES), F32)],
        compiler_params=_params(("arbitrary",), VMEM_MID),
    )(dk, dv, dcs, dcq, logit, dh_other, wk, wv, wf)


def _proj(xh, gi, bi, w, name):
    t, k = xh.shape
    n = w.shape[1]
    tm = _row_tile(t)

    def body(x_ref, g_ref, b_ref, w_ref, o_ref):
        x = (x_ref[...] * g_ref[...] + b_ref[...]).astype(BF16)
        o_ref[...] = _dot(x, w_ref[...]).astype(BF16)

    vec = pl.BlockSpec((1, k), lambda i: (0, 0))
    return pl.pallas_call(
        body, name=name, grid=(t // tm,),
        in_specs=[pl.BlockSpec((tm, k), lambda i: (i, 0)), vec, vec, pl.BlockSpec((k, n), lambda i: (0, 0))],
        out_specs=pl.BlockSpec((tm, n), lambda i: (i, 0)),
        out_shape=jax.ShapeDtypeStruct((t, n), BF16),
        compiler_params=_params(("arbitrary",), VMEM_MID),
    )(xh, gi, bi, w)


def _add_proj_nt(base, y, w, name):
    t, n = y.shape
    k = w.shape[0]
    tm = _row_tile(t)

    def body(b_ref, y_ref, w_ref, o_ref):
        o_ref[...] = b_ref[...] + _dot_nt(y_ref[...].astype(BF16), w_ref[...])

    return pl.pallas_call(
        body, name=name, grid=(t // tm,),
        in_specs=[pl.BlockSpec((tm, k), lambda i: (i, 0)), pl.BlockSpec((tm, n), lambda i: (i, 0)),
                  pl.BlockSpec((k, n), lambda i: (0, 0))],
        out_specs=pl.BlockSpec((tm, k), lambda i: (i, 0)),
        out_shape=jax.ShapeDtypeStruct((t, k), F32),
        compiler_params=_params(("arbitrary",), VMEM_MID),
    )(base, y, w)


def _attn_out_fwd(ot, xh, gi, bi, w_o, go, bo, alpha, name):
    t, d = xh.shape
    tm = _row_tile(t)

    def body(ot_ref, xh_ref, gi_ref, bi_ref, wo_ref, go_ref, bo_ref, xo_ref, rs_ref, hb_ref):
        h = xh_ref[...] * gi_ref[...] + bi_ref[...]
        xhat, rstd = _ln_fwd(alpha * h + _dot_tn(ot_ref[...], wo_ref[...]))
        xo_ref[...] = xhat
        rs_ref[...] = rstd
        hb_ref[...] = (xhat * go_ref[...] + bo_ref[...]).astype(BF16).T

    row = pl.BlockSpec((tm, d), lambda i: (i, 0))
    col = pl.BlockSpec((d, tm), lambda i: (0, i))
    vec = pl.BlockSpec((1, d), lambda i: (0, 0))
    return pl.pallas_call(
        body, name=name, grid=(t // tm,),
        in_specs=[col, row, vec, vec, pl.BlockSpec((d, d), lambda i: (0, 0)), vec, vec],
        out_specs=[row, pl.BlockSpec((tm, 1), lambda i: (i, 0)), col],
        out_shape=[jax.ShapeDtypeStruct((t, d), F32), jax.ShapeDtypeStruct((t, 1), F32),
                   jax.ShapeDtypeStruct((d, t), BF16)],
        compiler_params=_params(("arbitrary",), VMEM_MID),
    )(ot, xh, gi, bi, w_o, go, bo)


def _attn_out_bwd(dh, xo, rs, go, ot, w_o, alpha, name):
    t, d = dh.shape
    tm = _row_tile(t)
    hd = d // N_HEADS

    def body(dh_ref, xo_ref, rs_ref, go_ref, ot_ref, wo_ref,
             dres_ref, dmix_ref, dot_ref, delta_ref, dgain_ref, dbias_ref):
        @pl.when(pl.program_id(0) == 0)
        def _():
            dgain_ref[...] = jnp.zeros_like(dgain_ref)
            dbias_ref[...] = jnp.zeros_like(dbias_ref)

        dz, dgp, dbp = _ln_bwd(dh_ref[...], xo_ref[...], rs_ref[...], go_ref[...])
        dgain_ref[...] += dgp
        dbias_ref[...] += dbp
        dres_ref[...] = alpha * dz
        dmixb = dz.astype(BF16)
        dmix_ref[...] = dmixb
        dot_t = _dot_nt(wo_ref[...], dmixb)
        dot_ref[...] = dot_t.astype(BF16)
        prod = dot_t * ot_ref[...].astype(F32)
        delta_ref[...] = jnp.sum(prod.reshape(N_HEADS, hd, tm), axis=1)

    row = pl.BlockSpec((tm, d), lambda i: (i, 0))
    vec = pl.BlockSpec((1, d), lambda i: (0, 0))
    col = pl.BlockSpec((d, tm), lambda i: (0, i))
    return pl.pallas_call(
        body, name=name, grid=(t // tm,),
        in_specs=[row, row, pl.BlockSpec((tm, 1), lambda i: (i, 0)), vec, col,
                  pl.BlockSpec((d, d), lambda i: (0, 0))],
        out_specs=[row, row, col, pl.BlockSpec((N_HEADS, tm), lambda i: (0, i)), vec, vec],
        out_shape=[jax.ShapeDtypeStruct((t, d), F32), jax.ShapeDtypeStruct((t, d), BF16),
                   jax.ShapeDtypeStruct((d, t), BF16), jax.ShapeDtypeStruct((N_HEADS, t), F32),
                   jax.ShapeDtypeStruct((1, d), F32), jax.ShapeDtypeStruct((1, d), F32)],
        compiler_params=_params(("arbitrary",), VMEM_MID),
    )(dh, xo, rs, go, ot, w_o)


def _scores_t(k, q, ct_ref, c_ref, h, i, j, tq, tk, scale, masked):
    sub = lax.broadcasted_iota(jnp.int32, (8, tq), 0)
    cq = jnp.sum(jnp.where(sub == h, ct_ref[...], 0.0), axis=0, keepdims=True) * LOG2E
    lane = lax.broadcasted_iota(jnp.int32, (tk, LANES), 1)
    ck = jnp.sum(jnp.where(lane == h, c_ref[...], 0.0), axis=1, keepdims=True) * LOG2E
    st = _dot_nt(k, q) * (scale * LOG2E) - ck
    if masked:
        kpos = j * tk + lax.broadcasted_iota(jnp.int32, (tk, 1), 0)
        qpos = i * tq + lax.broadcasted_iota(jnp.int32, (1, tq), 1)
        st = jnp.where((kpos <= qpos) & (kpos >= PAD), st, NEG_INF)
    return st, cq


def _tri_pairs(n, by_row):
    if by_row:
        pairs = [(i, j) for i in range(n) for j in range(i + 1)]
    else:
        pairs = [(i, j) for j in range(n) for i in range(j, n)]
    return (jnp.asarray([p[0] for p in pairs], jnp.int32), jnp.asarray([p[1] for p in pairs], jnp.int32))


def _attn_fwd(q, k, v, c, ct, name, carry=()):
    t, d = q.shape
    hd = d // N_HEADS
    tq = tk = _row_tile(t)
    nq = t // tq
    scale = 1.0 / math.sqrt(hd)

    hps = ATTN_HEADS_PER_STEP

    def body(it_ref, jt_ref, q_ref, k_ref, v_ref, c_ref, ct_ref, ot_ref, lse_ref, m_s, l_s, acc):
        hp, p_ = pl.program_id(0), pl.program_id(1)
        i, j = it_ref[p_], jt_ref[p_]

        @pl.when(j == 0)
        def _():
            m_s[...] = jnp.full_like(m_s, NEG_INF)
            l_s[...] = jnp.zeros_like(l_s)
            acc[...] = jnp.zeros_like(acc)

        def update(masked):
            scores = []
            for e in range(hps):
                cols = slice(e * hd, (e + 1) * hd)
                scores.append(_scores_t(k_ref[:, cols], q_ref[:, cols], ct_ref, c_ref, hp * hps + e,
                                        i, j, tq, tk, scale, masked))
            probs = []
            for e, (st, cq) in enumerate(scores):
                m_new = jnp.maximum(m_s[e], jnp.max(st, axis=0, keepdims=True) + cq)
                a = jnp.exp2(m_s[e] - m_new)
                p = jnp.exp2(st - (m_new - cq))
                l_s[e] = a * l_s[e] + jnp.sum(p, axis=0, keepdims=True)
                m_s[e] = m_new
                probs.append((a, p.astype(BF16)))
            for e, (a, pb) in enumerate(probs):
                acc[e] = a * acc[e] + _dot_tn(v_ref[:, e * hd:(e + 1) * hd], pb)

        edge = (j == i) | (j == 0)
        pl.when(edge)(lambda: update(True))
        pl.when(jnp.logical_not(edge))(lambda: update(False))

        @pl.when(j == i)
        def _():
            for e in range(hps):
                ot_ref[e * hd:(e + 1) * hd, :] = (acc[e] / l_s[e]).astype(BF16)
                lse_ref[e] = m_s[e] + jnp.log2(l_s[e])

    it, jt = _tri_pairs(nq, by_row=True)
    npairs = it.shape[0]
    nhp = N_HEADS // hps
    kv = pl.BlockSpec((tk, hps * hd), lambda h, p, it, jt: (jt[p], h))
    first = lambda: (pl.program_id(0) == 0) & (pl.program_id(1) == 0)
    last = lambda: (pl.program_id(0) == nhp - 1) & (pl.program_id(1) == npairs - 1)
    return pl.pallas_call(
        _carried(body, 7, 2, carry, first, last), name=name,
        grid_spec=pltpu.PrefetchScalarGridSpec(
            num_scalar_prefetch=2, grid=(nhp, npairs),
            in_specs=[pl.BlockSpec((tq, hps * hd), lambda h, p, it, jt: (it[p], h)), kv, kv,
                      pl.BlockSpec((tk, LANES), lambda h, p, it, jt: (jt[p], 0)),
                      pl.BlockSpec((8, tq), lambda h, p, it, jt: (0, it[p]))] + [ANY] * len(carry),
            out_specs=[pl.BlockSpec((hps * hd, tq), lambda h, p, it, jt: (h, it[p])),
                       pl.BlockSpec((hps, 1, tq), lambda h, p, it, jt: (h, 0, it[p]))] + [ANY] * len(carry),
            scratch_shapes=[pltpu.VMEM((hps, 1, tq), F32), pltpu.VMEM((hps, 1, tq), F32),
                            pltpu.VMEM((hps, hd, tq), F32)] + _carry_scratch(carry)),
        out_shape=[jax.ShapeDtypeStruct((d, t), BF16), jax.ShapeDtypeStruct((N_HEADS, 1, t), F32)]
                  + _carry_shapes(carry),
        compiler_params=_params(("arbitrary", "arbitrary"), VMEM_MID),
    )(it, jt, q, k, v, c, ct, *[a for _, a in carry])


def _attn_bwd(q, k, v, c, ct, lse, delta, dot_t, name, carry=()):
    t, d = q.shape
    hd = d // N_HEADS
    tq = tk = _row_tile(t)
    nq = t // tq
    scale = 1.0 / math.sqrt(hd)
    hps = ATTN_BWD_HEADS_PER_STEP

    def body(it_ref, jt_ref, q_ref, k_ref, v_ref, c_ref, ct_ref, lse_ref, delta_ref, dot_ref,
             dq_ref, dk_ref, dv_ref, dcs_ref, drow_ref, dk_acc, dv_acc, dc_acc):
        hp, p_ = pl.program_id(0), pl.program_id(1)
        i, j = it_ref[p_], jt_ref[p_]

        @pl.when(p_ == 0)
        def _():
            dq_ref[...] = jnp.zeros_like(dq_ref)
            drow_ref[...] = jnp.zeros_like(drow_ref)

        @pl.when(i == j)
        def _():
            dk_acc[...] = jnp.zeros_like(dk_acc)
            dv_acc[...] = jnp.zeros_like(dv_acc)
            dc_acc[...] = jnp.zeros_like(dc_acc)

        def update(masked):
            sub = lax.broadcasted_iota(jnp.int32, (8, tq), 0)
            rows = pl.ds(pl.multiple_of(i * tq, tq), tq)
            stage = []
            for e in range(hps):
                cols = slice(e * hd, (e + 1) * hd)
                st, cq = _scores_t(k_ref[:, cols], q_ref[:, cols], ct_ref, c_ref, hp * hps + e,
                                   i, j, tq, tk, scale, masked)
                dp = _dot(v_ref[:, cols], dot_ref[cols, :])
                stage.append((st, cq, dp))
            grads = []
            for e, (st, cq, dp) in enumerate(stage):
                p = jnp.exp2(st - (lse_ref[e] - cq))
                dl = jnp.sum(jnp.where(sub == hp * hps + e, delta_ref[...], 0.0), axis=0, keepdims=True)
                ds = p * (dp - dl)
                part = ds[:, 0:LANES]
                for g in range(1, tq // LANES):
                    part = part + ds[:, g * LANES:(g + 1) * LANES]
                dc_acc[e] += part
                drow_ref[e, i] += jnp.broadcast_to(jnp.sum(ds, axis=0, keepdims=True), (8, tq))
                grads.append((p.astype(BF16), ds.astype(BF16)))
            for e, (pb, dsb) in enumerate(grads):
                cols = slice(e * hd, (e + 1) * hd)
                dv_acc[e] += _dot_nt(pb, dot_ref[cols, :])
                dk_acc[e] += _dot(dsb, q_ref[:, cols]) * scale
                dq_ref[rows, cols] += _dot_tn(dsb, k_ref[:, cols]) * scale

        edge = (j == i) | (j == 0)
        pl.when(edge)(lambda: update(True))
        pl.when(jnp.logical_not(edge))(lambda: update(False))

        @pl.when(i == nq - 1)
        def _():
            for e in range(hps):
                cols = slice(e * hd, (e + 1) * hd)
                dk_ref[:, cols] = dk_acc[e].astype(BF16)
                dv_ref[:, cols] = dv_acc[e].astype(BF16)
                dcs_ref[e] = -dc_acc[e]

    it, jt = _tri_pairs(nq, by_row=False)
    npairs = it.shape[0]
    nhp = N_HEADS // hps
    kv = pl.BlockSpec((tk, hps * hd), lambda h, p, it, jt: (jt[p], h))
    first = lambda: (pl.program_id(0) == 0) & (pl.program_id(1) == 0)
    last = lambda: (pl.program_id(0) == nhp - 1) & (pl.program_id(1) == npairs - 1)
    return pl.pallas_call(
        _carried(body, 10, 5, carry, first, last), name=name,
        grid_spec=pltpu.PrefetchScalarGridSpec(
            num_scalar_prefetch=2, grid=(nhp, npairs),
            in_specs=[pl.BlockSpec((tq, hps * hd), lambda h, p, it, jt: (it[p], h)), kv, kv,
                      pl.BlockSpec((tk, LANES), lambda h, p, it, jt: (jt[p], 0)),
                      pl.BlockSpec((8, tq), lambda h, p, it, jt: (0, it[p])),
                      pl.BlockSpec((hps, 1, tq), lambda h, p, it, jt: (h, 0, it[p])),
                      pl.BlockSpec((N_HEADS, tq), lambda h, p, it, jt: (0, it[p])),
                      pl.BlockSpec((hps * hd, tq), lambda h, p, it, jt: (h, it[p]))] + [ANY] * len(carry),
            out_specs=[pl.BlockSpec((t, hps * hd), lambda h, p, it, jt: (0, h)), kv, kv,
                       pl.BlockSpec((hps, tk, LANES), lambda h, p, it, jt: (h, jt[p], 0)),
                       pl.BlockSpec((hps, nq, 8, tq), lambda h, p, it, jt: (h, 0, 0, 0))] + [ANY] * len(carry),
            scratch_shapes=[pltpu.VMEM((hps, tk, hd), F32), pltpu.VMEM((hps, tk, hd), F32),
                            pltpu.VMEM((hps, tk, LANES), F32)] + _carry_scratch(carry)),
        out_shape=[jax.ShapeDtypeStruct((t, d), F32), jax.ShapeDtypeStruct((t, d), BF16),
                   jax.ShapeDtypeStruct((t, d), BF16), jax.ShapeDtypeStruct((N_HEADS, t, LANES), F32),
                   jax.ShapeDtypeStruct((N_HEADS, nq, 8, tq), F32)] + _carry_shapes(carry),
        compiler_params=_params(("arbitrary", "arbitrary"), VMEM_BIG),
    )(it, jt, q, k, v, c, ct, lse, delta, dot_t, *[a for _, a in carry])


def _adamw(w, g, m, v, name):
    r, c = w.shape
    tr = r
    for cand in (256, 128, 64, 32, 16, 8):
        if r % cand == 0 and r > cand:
            tr = cand
            break
    bc1 = 1.0 - ADAM_B1 ** ADAM_STEP
    bc2 = 1.0 - ADAM_B2 ** ADAM_STEP

    def body(w_ref, g_ref, m_ref, v_ref, d_ref, nm_ref, nv_ref):
        gg = g_ref[...]
        nm = ADAM_B1 * m_ref[...] + (1.0 - ADAM_B1) * gg
        nv = ADAM_B2 * v_ref[...] + (1.0 - ADAM_B2) * (gg * gg)
        d_ref[...] = -ADAM_LR * ((nm / bc1) / (jnp.sqrt(nv / bc2) + ADAM_EPS) + ADAM_WD * w_ref[...])
        nm_ref[...] = nm
        nv_ref[...] = nv

    blk = pl.BlockSpec((tr, c), lambda i: (i, 0))
    shp = jax.ShapeDtypeStruct((r, c), F32)
    return pl.pallas_call(
        body, name=name, grid=(r // tr,), in_specs=[blk] * 4, out_specs=[blk] * 3,
        out_shape=[shp] * 3, compiler_params=_params(("arbitrary",), VMEM_MID),
    )(w, g, m, v)


def _reduce_adamw(w, m, v, landed, name):
    nl, r, c = w.shape
    tr = next(cand for cand in range(min(r, ADAM_ROWS_MAX), 0, -BF16_ROWS) if r % cand == 0)
    nr = r // tr
    bc1 = 1.0 - ADAM_B1 ** ADAM_STEP
    bc2 = 1.0 - ADAM_B2 ** ADAM_STEP

    def body(*refs):
        w_ref, m_ref, v_ref = refs[:3]
        src_refs = refs[3:3 + nl]
        g_ref, d_ref, nm_ref, nv_ref = refs[3 + nl:]

        def update(src):
            gg = src[0].astype(F32)
            for s in range(1, N_DEV):
                gg = gg + src[s].astype(F32)
            nm = ADAM_B1 * m_ref[0] + (1.0 - ADAM_B1) * gg
            nv = ADAM_B2 * v_ref[0] + (1.0 - ADAM_B2) * (gg * gg)
            g_ref[0] = gg
            d_ref[0] = -ADAM_LR * ((nm / bc1) / (jnp.sqrt(nv / bc2) + ADAM_EPS) + ADAM_WD * w_ref[0])
            nm_ref[0] = nm
            nv_ref[0] = nv

        for idx in range(nl):
            pl.when(pl.program_id(0) == idx)(functools.partial(update, src_refs[idx]))

    def src_spec(idx):
        return pl.BlockSpec((N_DEV, tr, c),
                            lambda l, i: (0, jnp.where(l == idx, i, jnp.where(l < idx, 0, nr - 1)), 0))

    blk = pl.BlockSpec((1, tr, c), lambda l, i: (l, i, 0))
    shp = jax.ShapeDtypeStruct((nl, r, c), F32)
    return pl.pallas_call(
        body, name=name, grid=(nl, nr), in_specs=[blk] * 3 + [src_spec(idx) for idx in range(nl)],
        out_specs=[blk] * 4, out_shape=[shp] * 4,
        compiler_params=_params(("arbitrary", "arbitrary"), VMEM_MID),
    )(w, m, v, *landed)


def _sum_sources(r, name):
    n, rows, c = r.shape
    tr = next(cand for cand in range(min(rows, SUM_ROWS_MAX), 0, -BF16_ROWS) if rows % cand == 0)

    def body(r_ref, o_ref):
        acc = r_ref[0].astype(F32)
        for s in range(1, n):
            acc = acc + r_ref[s].astype(F32)
        o_ref[...] = acc

    return pl.pallas_call(
        body, name=name, grid=(rows // tr,),
        in_specs=[pl.BlockSpec((n, tr, c), lambda i: (0, i, 0))],
        out_specs=pl.BlockSpec((tr, c), lambda i: (i, 0)),
        out_shape=jax.ShapeDtypeStruct((rows, c), F32),
        compiler_params=_params(("arbitrary",), VMEM_MID),
    )(r)


def _all_gather(parts, name):
    n = len(parts)

    def body(*refs):
        x_refs, out_refs = refs[:n], refs[n:2 * n]
        send_sems, recv_sems, local_sems = refs[2 * n:]
        mx, my, mc = lax.axis_index("x"), lax.axis_index("y"), lax.axis_index("c")
        me, sibling = (mx, my, mc), (mx, my, 1 - mc)
        chips = [(1 - mx, my), (mx, 1 - my), (1 - mx, 1 - my)]

        def copy(p, k, block, to, from_input=False):
            px, py, pc = block
            rows = out_refs[p].at[4 * px + 2 * py + pc]
            return pltpu.make_async_remote_copy(
                src_ref=x_refs[p] if from_input else rows, dst_ref=rows,
                send_sem=send_sems.at[7 * p + k], recv_sem=recv_sems.at[7 * p + k],
                device_id=to, device_id_type=MESH)

        mine, sent = [], []
        for p in range(n):
            own = pltpu.make_async_copy(x_refs[p], out_refs[p].at[4 * mx + 2 * my + mc], local_sems.at[p])
            own.start()
            mine.append(own)
            first = [copy(p, 0, me, sibling, True)]
            first += [copy(p, 1 + j, me, (*chip, mc), True) for j, chip in enumerate(chips)]
            for cp in first:
                cp.start()
            sent += first
        for p in range(n):
            for j, chip in enumerate(chips):
                copy(p, 1 + j, (*chip, mc), me).wait_recv()
                fwd = copy(p, 4 + j, (*chip, mc), sibling)
                fwd.start()
                sent.append(fwd)
        for p in range(n):
            copy(p, 0, sibling, me).wait_recv()
            for j, chip in enumerate(chips):
                copy(p, 4 + j, (*chip, 1 - mc), me).wait_recv()
        for cp in sent:
            cp.wait_send()
        for own in mine:
            own.wait()

    return pl.pallas_call(
        body, name=name, in_specs=[ANY] * n, out_specs=[ANY] * n,
        out_shape=[jax.ShapeDtypeStruct((N_DEV,) + a.shape, a.dtype) for a in parts],
        scratch_shapes=[pltpu.SemaphoreType.DMA((7 * n,)), pltpu.SemaphoreType.DMA((7 * n,)),
                        pltpu.SemaphoreType.DMA((n,))],
    )(*parts)


def _pack_rows(parts, width, mult, lead=0):
    out = []
    for a in parts:
        head = a.shape[:lead]
        flat = a.reshape(head + (-1,))
        padn = (-flat.shape[-1]) % (width * mult)
        if padn:
            flat = jnp.pad(flat, [(0, 0)] * lead + [(0, padn)])
        out.append(flat.reshape(head + (-1, width)))
    return jnp.concatenate(out, axis=lead)


def _rows_of(shape, width, mult):
    n = math.prod(shape)
    per = width * mult
    return ((n + per - 1) // per) * mult


def _unpack_rows(buf, shapes, width, mult):
    lead = buf.shape[:-2]
    out, off = [], 0
    for shp in shapes:
        r = _rows_of(shp, width, mult)
        flat = buf[..., off:off + r, :].reshape(lead + (r * width,))
        out.append(flat[..., :math.prod(shp)].reshape(lead + tuple(shp)))
        off += r
    return out


def _cols_from_devices(g):
    nd = g.ndim
    perm = tuple(range(1, nd - 1)) + (0, nd - 1)
    t = jnp.transpose(g, perm)
    return t.reshape(t.shape[:-2] + (t.shape[-2] * t.shape[-1],))


def _cols_to_devices(a):
    c = a.shape[-1] // N_DEV
    t = a.reshape(a.shape[:-1] + (N_DEV, c))
    nd = t.ndim
    perm = (nd - 2,) + tuple(range(0, nd - 2)) + (nd - 1,)
    return jnp.transpose(t, perm)


WIDTH = 1024


def kernel(x, meta, ffn1_wg, ffn1_wu, ffn1_wd, ffn2_wg, ffn2_wu, ffn2_wd, ln_gain, ln_bias, conv_w_in, conv_w, conv_w_out, kv_w, f_bias, attn_w_q, attn_w_o, loss_target, m_meta, m_ffn1_wg, m_ffn1_wu, m_ffn1_wd, m_ffn2_wg, m_ffn2_wu, m_ffn2_wd, m_ln_gain, m_ln_bias, m_conv_w_in, m_conv_w, m_conv_w_out, m_kv_w, m_f_bias, m_attn_w_q, m_attn_w_o, v_meta, v_ffn1_wg, v_ffn1_wu, v_ffn1_wd, v_ffn2_wg, v_ffn2_wu, v_ffn2_wd, v_ln_gain, v_ln_bias, v_conv_w_in, v_conv_w, v_conv_w_out, v_kv_w, v_f_bias, v_attn_w_q, v_attn_w_o):
    depth = ln_gain.shape[0]
    alpha = float((2 * depth) ** 0.25)
    d = x.shape[-1]
    seq = x.shape[1]
    t = ROW0 + seq
    fsh = ffn1_wg.shape[-1]
    f = fsh * N_DEV
    fck = MXU_COLS
    nc = f // fck
    me = 4 * lax.axis_index("x") + 2 * lax.axis_index("y") + lax.axis_index("c")

    def gather_of(parts):
        return [(True, a.astype(BF16)) for a in parts]

    small = [meta, ln_gain, ln_bias, conv_w]
    small_shapes = [a.shape for a in small]
    g1g, g1u, g1d, gcin, gcout, gsmall = _all_gather(
        [a.astype(BF16) for a in (ffn1_wg[0], ffn1_wu[0], ffn1_wd[0], conv_w_in[0], conv_w_out[0])]
        + [_pack_rows(small, WIDTH, F32_ROWS)], "ag_first")
    gmeta, ggain, gbias, gcw = _unpack_rows(gsmall, small_shapes, WIDTH, F32_ROWS)

    def ffn_chunks(gg, gu, gd):
        up = lambda g: jnp.transpose(_cols_from_devices(g).reshape(d, nc, fck), (1, 0, 2))
        return up(gg), up(gu), gd.reshape(nc, fck, d)

    w_in = _cols_from_devices(gcin)
    w_out = gcout.reshape(d, d)
    fb = jnp.pad(f_bias, (0, LANES - N_HEADS)).reshape(1, LANES)
    meta_f = _cols_from_devices(gmeta)
    gain_f = _cols_from_devices(ggain)
    bias_f = _cols_from_devices(gbias)
    cw_f = _cols_from_devices(gcw)[0]

    def gb(l, n):
        return gain_f[l, n].reshape(1, d), bias_f[l, n].reshape(1, d)

    ones = jnp.ones((1, d), F32)
    zeros = jnp.zeros((1, d), F32)

    h0 = jnp.concatenate([jnp.zeros((PAD, d), F32), meta_f, x[0]], axis=0)

    w1 = ffn_chunks(g1g, g1u, g1d)
    g00, b00 = gb(0, 0)
    xh1, rs1, hb1, gg1, uu1, hb0, g2g, g2u = _ffn_fwd(
        h0, ones, zeros, *w1, g00, b00, alpha, "ffn_fwd_0a", carry=gather_of([ffn2_wg[0], ffn2_wu[0]]),
        input_t=True)
    g01, b01 = gb(0, 1)
    xh2, rs2, hb2, pp, mb, g2d, gkv = _conv_fwd(
        xh1, g00, b00, w_in, cw_f, w_out, g01, b01, alpha, "conv_fwd", carry=gather_of([ffn2_wd[0], kv_w.T]))
    w2 = ffn_chunks(g2g, g2u, g2d)
    g02, b02 = gb(0, 2)
    xh3, rs3, hb3, gg3, uu3, g3g, g3u = _ffn_fwd(
        xh2, g01, b01, *w2, g02, b02, alpha, "ffn_fwd_0b", carry=gather_of([ffn1_wg[1], ffn1_wu[1]]))
    kvw = gkv.reshape(gkv.shape[0] * gkv.shape[1], d).T
    wk, wv = kvw[:, :d], kvw[:, d:2 * d]
    wf = jnp.pad(kvw[:, 2 * d:], ((0, 0), (0, LANES - N_HEADS)))
    kk, vv, logit, cc, cct, g3d = _kv_fwd(xh3, g02, b02, wk, wv, wf, fb, "kv_fwd",
                                          carry=gather_of([ffn1_wd[1]]))

    w3 = ffn_chunks(g3g, g3u, g3d)
    g10, b10 = gb(1, 0)
    xh4, rs4, hb4, gg4, uu4, gwq = _ffn_fwd(xh3, g02, b02, *w3, g10, b10, alpha, "ffn_fwd_1a",
                                             carry=gather_of([attn_w_q[0]]))
    w_q = gwq.reshape(d, d)
    qq = _proj(xh4, g10, b10, w_q, "q_proj")
    ot, lse, gwo, g4g, g4u, g4d = _attn_fwd(
        qq, kk, vv, cc, cct, "attn_fwd", carry=gather_of([attn_w_o[0], ffn2_wg[1], ffn2_wu[1], ffn2_wd[1]]))
    w_o = gwo.reshape(d, d)
    g11, b11 = gb(1, 1)
    xh5, rs5, hb5 = _attn_out_fwd(ot, xh4, g10, b10, w_o, g11, b11, alpha, "attn_out_fwd")
    w4 = ffn_chunks(g4g, g4u, g4d)
    g12, b12 = gb(1, 2)
    xh6, rs6, _, gg6, uu6 = _ffn_fwd(xh5, g11, b11, *w4, g12, b12, alpha, "ffn_fwd_1b")


    dgain = [[None] * 3 for _ in range(depth)]
    dbias = [[None] * 3 for _ in range(depth)]

    def to_col_owners(g):
        return (False, _cols_to_devices(g).astype(BF16))

    def to_row_owners(g):
        return (False, g.reshape(N_DEV, g.shape[0] // N_DEV, g.shape[1]).astype(BF16))

    dh5, do6, dg6, du6, a6, dgain[1][2], dbias[1][2], loss_l = _ffn_bwd(
        None, xh6, rs6, g12, gg6, uu6, *w4, alpha, "ffn_bwd_1b", loss_target=loss_target[0], loss_bias=b12)
    loss = lax.psum(loss_l[0, 0], ("x", "y", "c"))
    dw4g, dw4u = _wgrad(hb5, [dg6, du6], "wgrad_up_1b")
    (dw4dt,) = _wgrad(do6, [a6], "wgrad_down_1b")

    dres4, dmix5, dot_t, delta, dgain[1][1], dbias[1][1] = _attn_out_bwd(dh5, xh5, rs5, g11, ot, w_o, alpha, "attn_out_bwd")
    (dwo,) = _wgrad(ot, [dmix5], "wgrad_wo")
    dq, dkk, dvv, dcs, drow, l4g, l4u, l4d, lwo = _attn_bwd(
        qq, kk, vv, cc, cct, lse, delta, dot_t, "attn_bwd",
        carry=[to_col_owners(dw4g), to_col_owners(dw4u), to_row_owners(dw4dt.T), to_row_owners(dwo)])
    dh4 = _add_proj_nt(dres4, dq, w_q, "q_bwd")
    (dwq,) = _wgrad(hb4, [dq], "wgrad_wq")

    dh3a, do4, dg4, du4, a4, dgain[1][0], dbias[1][0] = _ffn_bwd(dh4, xh4, rs4, g10, gg4, uu4, *w3, alpha, "ffn_bwd_1a")
    dw3g, dw3u = _wgrad(hb3, [dg4, du4], "wgrad_up_1a")
    (dw3dt,) = _wgrad(do4, [a4], "wgrad_down_1a")

    dcq = jnp.pad(drow[:, :, 0, :].reshape(N_HEADS, t).T, ((0, 0), (0, LANES - N_HEADS)))
    dh3, dlogit, dfb = _kv_bwd(dkk, dvv, dcs, dcq, logit, dh3a, wk, wv, wf, "kv_bwd")
    dwk, dwv = _wgrad(hb3, [dkk, dvv], "wgrad_kv")
    (dwf,) = _wgrad(hb3, [dlogit], "wgrad_f")
    dkv = jnp.concatenate([dwk, dwv, dwf[:, :N_HEADS]], axis=1)

    dh2, do3, dg3, du3, a3, dgain[0][2], dbias[0][2], lwq, l3g, l3u, l3d, lkv = _ffn_bwd(
        dh3, xh3, rs3, g02, gg3, uu3, *w2, alpha, "ffn_bwd_0b",
        carry=[to_row_owners(dwq), to_col_owners(dw3g), to_col_owners(dw3u), to_row_owners(dw3dt.T),
               to_row_owners(dkv.T)])
    dw2g, dw2u = _wgrad(hb2, [dg3, du3], "wgrad_up_0b")
    (dw2dt,) = _wgrad(do3, [a3], "wgrad_down_0b")

    dh1, dmix2, dpp, dcw, dgain[0][1], dbias[0][1] = _conv_bwd(dh2, xh2, rs2, g01, pp, cw_f, w_in, w_out, alpha, "conv_bwd")
    (dwin,) = _wgrad(hb1, [dpp], "wgrad_conv_in")
    (dwout,) = _wgrad(mb, [dmix2], "wgrad_conv_out")

    dh0, do1, dg1, du1, a1, dgain[0][0], dbias[0][0], l2g, l2u, l2d, lcin, lcout = _ffn_bwd(
        dh1, xh1, rs1, g00, gg1, uu1, *w1, alpha, "ffn_bwd_0a",
        carry=[to_col_owners(dw2g), to_col_owners(dw2u), to_row_owners(dw2dt.T), to_col_owners(dwin),
               to_row_owners(dwout)])
    (dw1dt,) = _wgrad(do1, [a1], "wgrad_down_0a")
    dw1g, l1d = _wgrad(hb0, [dg1], "wgrad_upg_0a", carry=[to_row_owners(dw1dt.T)])
    dw1u, l1g = _wgrad(hb0, [du1], "wgrad_upu_0a", carry=[to_col_owners(dw1g)])
    dmeta = dh0[PAD:ROW0]
    dgain_f = jnp.stack([jnp.concatenate(r, axis=0) for r in dgain])
    dbias_f = jnp.stack([jnp.concatenate(r, axis=0) for r in dbias])
    small_full = [dmeta, dgain_f, dbias_f, dcw[None], dfb]
    small_full_shapes = [a.shape for a in small_full]
    l1u, gsmall_grads = _exchange([to_col_owners(dw1u), (True, _pack_rows(small_full, WIDTH, F32_ROWS))], "rs_last")

    grad_x = dh0[ROW0:].reshape(1, seq, d)
    rsmall = _sum_sources(gsmall_grads, "small_sum")
    smeta, sgain, sbias, scw, sfb = _unpack_rows(rsmall, small_full_shapes, WIDTH, F32_ROWS)
    csh = d // N_DEV

    def my_cols(a):
        return lax.dynamic_slice_in_dim(a, me * csh, csh, axis=a.ndim - 1)

    grads = {"meta": my_cols(smeta), "ln_gain": my_cols(sgain), "ln_bias": my_cols(sbias),
             "conv_w": my_cols(scw), "f_bias": sfb[0, :N_HEADS], "kv_w": _sum_sources(lkv, "kv_sum").T}
    landed = {"ffn1_wg": [l1g, l3g], "ffn1_wu": [l1u, l3u], "ffn1_wd": [l1d, l3d],
              "ffn2_wg": [l2g, l4g], "ffn2_wu": [l2u, l4u], "ffn2_wd": [l2d, l4d],
              "conv_w_in": [lcin], "conv_w_out": [lcout], "attn_w_q": [lwq], "attn_w_o": [lwo]}
    weights = dict(meta=meta, ffn1_wg=ffn1_wg, ffn1_wu=ffn1_wu, ffn1_wd=ffn1_wd, ffn2_wg=ffn2_wg,
                   ffn2_wu=ffn2_wu, ffn2_wd=ffn2_wd, ln_gain=ln_gain, ln_bias=ln_bias,
                   conv_w_in=conv_w_in, conv_w=conv_w, conv_w_out=conv_w_out, kv_w=kv_w,
                   f_bias=f_bias, attn_w_q=attn_w_q, attn_w_o=attn_w_o)
    moms = dict(meta=(m_meta, v_meta), ffn1_wg=(m_ffn1_wg, v_ffn1_wg), ffn1_wu=(m_ffn1_wu, v_ffn1_wu),
                ffn1_wd=(m_ffn1_wd, v_ffn1_wd), ffn2_wg=(m_ffn2_wg, v_ffn2_wg), ffn2_wu=(m_ffn2_wu, v_ffn2_wu),
                ffn2_wd=(m_ffn2_wd, v_ffn2_wd), ln_gain=(m_ln_gain, v_ln_gain), ln_bias=(m_ln_bias, v_ln_bias),
                conv_w_in=(m_conv_w_in, v_conv_w_in), conv_w=(m_conv_w, v_conv_w),
                conv_w_out=(m_conv_w_out, v_conv_w_out), kv_w=(m_kv_w, v_kv_w), f_bias=(m_f_bias, v_f_bias),
                attn_w_q=(m_attn_w_q, v_attn_w_q), attn_w_o=(m_attn_w_o, v_attn_w_o))

    names = list(weights)
    g_out, d_out, m_out, v_out = [], [], [], []
    for n in names:
        w = weights[n]
        shp = w.shape
        mm, vv_ = moms[n]
        if n in landed:
            three = (len(landed[n]),) + shp[-2:]
            g, dl, nm, nv = _reduce_adamw(w.reshape(three), mm.reshape(three), vv_.reshape(three),
                                          landed[n], "adamw_" + n)
            g = g.reshape(shp)
        else:
            two = (1, shp[0]) if w.ndim == 1 else (math.prod(shp[:-1]), shp[-1])
            g = grads[n].reshape(shp)
            dl, nm, nv = _adamw(w.reshape(two), g.reshape(two), mm.reshape(two), vv_.reshape(two), "adamw_" + n)
        g_out.append(g)
        d_out.append(dl.reshape(shp))
        m_out.append(nm.reshape(shp))
        v_out.append(nv.reshape(shp))
    return (loss, grad_x, *g_out, *d_out, *m_out, *v_out)
```

```python
import functools
import math

import jax
import jax.numpy as jnp
from jax import lax
from jax.experimental import pallas as pl
from jax.experimental.pallas import tpu as pltpu

F32 = jnp.float32
BF16 = jnp.bfloat16

N_DEV = 8
N_HEADS = 8
N_META = 16
PAD = 112
ROW0 = PAD + N_META
LN_EPS = 1e-5
NEG_INF = -1e30
LOG2E = 1.4426950408889634
ATTN_HEADS_PER_STEP = 8
ATTN_BWD_HEADS_PER_STEP = 2
LANES = 128
MXU_COLS = 256
FFN_FWD_CHUNKS = 11
FFN_BWD_CHUNKS = 4

ADAM_LR = 0.001
ADAM_B1 = 0.9
ADAM_B2 = 0.999
ADAM_EPS = 1e-08
ADAM_WD = 0.01
ADAM_STEP = 10

ROW_TILES = (640, 128)
LOSS_TILE = 128
BF16_ROWS = 16
F32_ROWS = 8
SUM_ROWS_MAX = 768
ADAM_ROWS_MAX = 256
VMEM_BIG = 56 << 20
VMEM_MID = 40 << 20

ANY = pl.BlockSpec(memory_space=pl.ANY)
MESH = pl.DeviceIdType.MESH


def _row_tile(t):
    for c in ROW_TILES:
        if t % c == 0:
            return c
    raise ValueError(f"no row tile for {t}")


def _dot(a, b):
    return jnp.dot(a, b, preferred_element_type=F32)


def _dot_nt(a, b):
    return lax.dot_general(a, b, (((1,), (1,)), ((), ())), preferred_element_type=F32)


def _dot_tn(a, b):
    return lax.dot_general(a, b, (((0,), (0,)), ((), ())), preferred_element_type=F32)


def _params(sem, vmem):
    return pltpu.CompilerParams(dimension_semantics=sem, vmem_limit_bytes=vmem)


def _ln_fwd(z):
    mu = jnp.mean(z, axis=-1, keepdims=True)
    zc = z - mu
    var = jnp.mean(zc * zc, axis=-1, keepdims=True)
    rstd = lax.rsqrt(var + LN_EPS)
    return zc * rstd, rstd


def _ln_bwd(dh, xhat, rstd, gain):
    dxh = dh * gain
    m1 = jnp.mean(dxh, axis=-1, keepdims=True)
    m2 = jnp.mean(dxh * xhat, axis=-1, keepdims=True)
    dz = rstd * (dxh - m1 - xhat * m2)
    return dz, jnp.sum(dh * xhat, axis=0, keepdims=True), jnp.sum(dh, axis=0, keepdims=True)


def _load_resident(pairs, sems):
    cps = [pltpu.make_async_copy(src, dst, sems.at[k]) for k, (src, dst) in enumerate(pairs)]
    for cp in cps:
        cp.start()
    for cp in cps:
        cp.wait()


def _peer_ids():
    mx, my, mc = lax.axis_index("x"), lax.axis_index("y"), lax.axis_index("c")
    peers = []
    for kk in range(1, N_DEV):
        px = 1 - mx if (kk >> 2) & 1 else mx
        py = 1 - my if (kk >> 1) & 1 else my
        pc = 1 - mc if kk & 1 else mc
        peers.append(((px, py, pc), 4 * px + 2 * py + pc))
    return 4 * mx + 2 * my + mc, peers


def _exchange_copies(jobs, send_sems, recv_sems, local_sems, starting):
    me_id, peers = _peer_ids()
    for n, (gather, src, dst) in enumerate(jobs):
        own = pltpu.make_async_copy(src if gather else src.at[me_id], dst.at[me_id], local_sems.at[n])
        own.start() if starting else own.wait()
        for k, (dev, pid) in enumerate(peers):
            sem = (N_DEV - 1) * n + k
            out = src if gather else src.at[pid]
            send = pltpu.make_async_remote_copy(
                src_ref=out, dst_ref=dst.at[me_id], send_sem=send_sems.at[sem], recv_sem=recv_sems.at[sem],
                device_id=dev, device_id_type=MESH)
            if starting:
                send.start()
            else:
                pltpu.make_async_remote_copy(
                    src_ref=out, dst_ref=dst.at[pid], send_sem=send_sems.at[sem], recv_sem=recv_sems.at[sem],
                    device_id=dev, device_id_type=MESH).wait_recv()
                send.wait_send()


def _carried(body, n_in, n_out, carry, first, last):
    nj = len(carry)
    if nj == 0:
        return body

    def wrapped(*refs):
        ins, srcs = refs[:n_in], refs[n_in:n_in + nj]
        outs = refs[n_in + nj:n_in + nj + n_out]
        dsts = refs[n_in + nj + n_out:n_in + 2 * nj + n_out]
        scratch, sems = refs[n_in + 2 * nj + n_out:-3], refs[-3:]
        jobs = [(g, s, r) for (g, _), s, r in zip(carry, srcs, dsts)]

        @pl.when(first())
        def _():
            _exchange_copies(jobs, *sems, starting=True)

        body(*ins, *outs, *scratch)

        @pl.when(last())
        def _():
            _exchange_copies(jobs, *sems, starting=False)

    return wrapped


def _carry_shapes(carry):
    return [jax.ShapeDtypeStruct((N_DEV,) + a.shape if g else a.shape, a.dtype) for g, a in carry]


def _carry_scratch(carry):
    if not carry:
        return []
    n = len(carry)
    return [pltpu.SemaphoreType.DMA(((N_DEV - 1) * n,)), pltpu.SemaphoreType.DMA(((N_DEV - 1) * n,)),
            pltpu.SemaphoreType.DMA((n,))]


def _exchange(carry, name):
    n = len(carry)

    def body(*refs):
        jobs = [(g, s, r) for (g, _), s, r in zip(carry, refs[:n], refs[n:2 * n])]
        _exchange_copies(jobs, *refs[2 * n:], starting=True)
        _exchange_copies(jobs, *refs[2 * n:], starting=False)

    return pl.pallas_call(
        body, name=name, in_specs=[ANY] * n, out_specs=[ANY] * n, out_shape=_carry_shapes(carry),
        scratch_shapes=_carry_scratch(carry),
    )(*[a for _, a in carry])


def _ffn_fwd(xh, gi, bi, wg, wu, wd, go, bo, alpha, name, carry=(), input_t=False):
    t, d = xh.shape
    nch, _, fc = wg.shape
    f = nch * fc
    per = min(FFN_FWD_CHUNKS, nch)
    nc = -(-nch // per)
    tm = _row_tile(t)
    nt = t // tm

    def body(xh_ref, gi_ref, bi_ref, wg_hbm, wu_hbm, wd_hbm, go_ref, bo_ref,
             xo_ref, rs_ref, hb_ref, g_ref, u_ref, *tail):
        hin_ref = tail[0] if input_t else None
        wg_v, wu_v, wd_v, acc, hbs, sems = tail[1:] if input_t else tail
        i = pl.program_id(0)
        c = pl.program_id(1)

        @pl.when((i == 0) & (c == 0))
        def _():
            _load_resident([(wg_hbm, wg_v), (wu_hbm, wu_v), (wd_hbm, wd_v)], sems)

        @pl.when(c == 0)
        def _():
            h = xh_ref[...] * gi_ref[...] + bi_ref[...]
            hbs[...] = h.astype(BF16)
            acc[...] = jnp.zeros_like(acc)
            if input_t:
                hin_ref[...] = hbs[...].T

        def chunk(k):
            ck = c * per + k
            cols = slice(k * fc, (k + 1) * fc)
            hb = hbs[...]
            g = _dot(hb, wg_v[ck])
            u = _dot(hb, wu_v[ck])
            a = (g * jax.nn.sigmoid(g)) * u
            g_ref[:, cols] = g.astype(BF16)
            u_ref[:, cols] = u.astype(BF16)
            acc[...] += _dot(a.astype(BF16), wd_v[ck])

        for k in range(per):
            if (nc - 1) * per + k < nch:
                chunk(k)
            else:
                pl.when(c * per + k < nch)(functools.partial(chunk, k))

        @pl.when(c == nc - 1)
        def _():
            h = xh_ref[...] * gi_ref[...] + bi_ref[...]
            xhat, rstd = _ln_fwd(alpha * h + 0.5 * acc[...])
            xo_ref[...] = xhat
            rs_ref[...] = rstd
            hb_ref[...] = (xhat * go_ref[...] + bo_ref[...]).astype(BF16).T

    row = pl.BlockSpec((tm, d), lambda i, c: (i, 0))
    vec = pl.BlockSpec((1, d), lambda i, c: (0, 0))
    chunk = pl.BlockSpec((tm, per * fc), lambda i, c: (i, c))
    first = lambda: (pl.program_id(0) == 0) & (pl.program_id(1) == 0)
    last = lambda: (pl.program_id(0) == nt - 1) & (pl.program_id(1) == nc - 1)
    col = pl.BlockSpec((d, tm), lambda i, c: (0, i))
    t_spec, t_shape = ([col], [jax.ShapeDtypeStruct((d, t), BF16)]) if input_t else ([], [])
    return pl.pallas_call(
        _carried(body, 8, 5 + len(t_spec), carry, first, last), name=name, grid=(nt, nc),
        in_specs=[row, vec, vec, ANY, ANY, ANY, vec, vec] + [ANY] * len(carry),
        out_specs=[row, pl.BlockSpec((tm, 1), lambda i, c: (i, 0)), col, chunk, chunk] + t_spec
                  + [ANY] * len(carry),
        out_shape=[jax.ShapeDtypeStruct((t, d), F32), jax.ShapeDtypeStruct((t, 1), F32),
                   jax.ShapeDtypeStruct((d, t), BF16), jax.ShapeDtypeStruct((t, f), BF16),
                   jax.ShapeDtypeStruct((t, f), BF16)] + t_shape + _carry_shapes(carry),
        scratch_shapes=[pltpu.VMEM((nch, d, fc), BF16), pltpu.VMEM((nch, d, fc), BF16),
                        pltpu.VMEM((nch, fc, d), BF16), pltpu.VMEM((tm, d), F32),
                        pltpu.VMEM((tm, d), BF16), pltpu.SemaphoreType.DMA((3,))] + _carry_scratch(carry),
        compiler_params=_params(("arbitrary", "arbitrary"), VMEM_BIG),
    )(xh, gi, bi, wg, wu, wd, go, bo, *[a for _, a in carry])


def _ffn_bwd(dh, xo, rs, go, gs, us, wg, wu, wd, alpha, name, carry=(), loss_target=None, loss_bias=None):
    t, d = xo.shape
    nch, _, fc = wg.shape
    f = nch * fc
    per = min(FFN_BWD_CHUNKS, nch)
    nc = -(-nch // per)
    tm = _row_tile(t)
    nt = t // tm

    with_loss = loss_target is not None
    nsub, lead = tm // LOSS_TILE, ROW0 // LOSS_TILE
    nlead = nsub + 1 if with_loss else 1

    def body(*refs):
        lead_refs = refs[:nlead]
        xo_ref, rs_ref, go_ref, g_ref, u_ref, wg_hbm, wu_hbm, wd_hbm = refs[nlead:nlead + 8]
        dhin_ref, dot_ref, dg_ref, du_ref, a_ref, dgain_ref, dbias_ref = refs[nlead + 8:nlead + 15]
        rest = refs[nlead + 15:]
        loss_ref, rest = (rest[0], rest[1:]) if with_loss else (None, rest)
        wg_v, wu_v, wd_v, do_ref, sems = rest[:5]
        i = pl.program_id(0)
        c = pl.program_id(1)

        @pl.when((i == 0) & (c == 0))
        def _():
            _load_resident([(wg_hbm, wg_v), (wu_hbm, wu_v), (wd_hbm, wd_v)], sems)
            dgain_ref[...] = jnp.zeros_like(dgain_ref)
            dbias_ref[...] = jnp.zeros_like(dbias_ref)
            if with_loss:
                rest[5][...] = jnp.zeros_like(rest[5])

        def tile_dh():
            if not with_loss:
                return lead_refs[0][...]
            part = rest[5]
            for k in range(nsub):
                sl = slice(k * LOSS_TILE, (k + 1) * LOSS_TILE)
                rows = i * tm + k * LOSS_TILE + lax.broadcasted_iota(jnp.int32, (LOSS_TILE, 1), 0)
                y = xo_ref[sl, :] * go_ref[...] + lead_refs[nsub][...]
                e = jnp.where(rows >= ROW0, y - lead_refs[k][...], 0.0)
                part[...] += jnp.sum(e * e, axis=0, keepdims=True)
                dhin_ref[sl, :] = e * (1.0 / d)

            @pl.when(i == nt - 1)
            def _():
                loss_ref[...] = jnp.full((1, LANES), 0.5 / d, F32) * jnp.sum(part[...])

            return dhin_ref[...]

        @pl.when(c == 0)
        def _():
            dz, dgp, dbp = _ln_bwd(tile_dh(), xo_ref[...], rs_ref[...], go_ref[...])
            dgain_ref[...] += dgp
            dbias_ref[...] += dbp
            dob = (0.5 * dz).astype(BF16)
            do_ref[...] = dob
            dot_ref[...] = dob.T
            dhin_ref[...] = alpha * dz

        def chunk(k):
            ck = c * per + k
            cols = slice(k * fc, (k + 1) * fc)
            g = g_ref[:, cols].astype(F32)
            u = u_ref[:, cols].astype(F32)
            sg = jax.nn.sigmoid(g)
            sl = g * sg
            da = _dot_nt(do_ref[...], wd_v[ck])
            dgb = (da * u * (sg * (1.0 + g * (1.0 - sg)))).astype(BF16)
            dub = (da * sl).astype(BF16)
            a_ref[:, cols] = (sl * u).astype(BF16)
            dg_ref[:, cols] = dgb
            du_ref[:, cols] = dub
            dhin_ref[...] += _dot_nt(dgb, wg_v[ck]) + _dot_nt(dub, wu_v[ck])

        for k in range(per):
            if (nc - 1) * per + k < nch:
                chunk(k)
            else:
                pl.when(c * per + k < nch)(functools.partial(chunk, k))

    row = pl.BlockSpec((tm, d), lambda i, c: (i, 0))
    vec = pl.BlockSpec((1, d), lambda i, c: (0, 0))
    chunk = pl.BlockSpec((tm, per * fc), lambda i, c: (i, c))
    first = lambda: (pl.program_id(0) == 0) & (pl.program_id(1) == 0)
    last = lambda: (pl.program_id(0) == nt - 1) & (pl.program_id(1) == nc - 1)
    if with_loss:
        lead_specs = [pl.BlockSpec((LOSS_TILE, d), lambda i, c, k=k: (jnp.maximum(i * nsub + k - lead, 0), 0))
                      for k in range(nsub)] + [vec]
        lead_args = [loss_target] * nsub + [loss_bias]
        loss_spec, loss_shape = [pl.BlockSpec((1, LANES), lambda i, c: (0, 0))], [jax.ShapeDtypeStruct((1, LANES), F32)]
        loss_scratch = [pltpu.VMEM((1, d), F32)]
    else:
        lead_specs, lead_args, loss_spec, loss_shape, loss_scratch = [row], [dh], [], [], []
    return pl.pallas_call(
        _carried(body, nlead + 8, 7 + len(loss_spec), carry, first, last), name=name, grid=(nt, nc),
        in_specs=lead_specs + [row, pl.BlockSpec((tm, 1), lambda i, c: (i, 0)), vec, chunk, chunk,
                               ANY, ANY, ANY] + [ANY] * len(carry),
        out_specs=[row, pl.BlockSpec((d, tm), lambda i, c: (0, i)), chunk, chunk, chunk, vec, vec]
                  + loss_spec + [ANY] * len(carry),
        out_shape=[jax.ShapeDtypeStruct((t, d), F32), jax.ShapeDtypeStruct((d, t), BF16),
                   jax.ShapeDtypeStruct((t, f), BF16), jax.ShapeDtypeStruct((t, f), BF16),
                   jax.ShapeDtypeStruct((t, f), BF16), jax.ShapeDtypeStruct((1, d), F32),
                   jax.ShapeDtypeStruct((1, d), F32)] + loss_shape + _carry_shapes(carry),
        scratch_shapes=[pltpu.VMEM((nch, d, fc), BF16), pltpu.VMEM((nch, d, fc), BF16),
                        pltpu.VMEM((nch, fc, d), BF16), pltpu.VMEM((tm, d), BF16),
                        pltpu.SemaphoreType.DMA((3,))] + loss_scratch + _carry_scratch(carry),
        compiler_params=_params(("arbitrary", "arbitrary"), VMEM_BIG),
    )(*lead_args, xo, rs, go, gs, us, wg, wu, wd, *[a for _, a in carry])


def _wgrad(xt, ys, name, carry=()):
    m, t = xt.shape
    n = ys[0].shape[1]
    tn = min(n, MXU_COLS)
    ny = len(ys)

    def body(*refs):
        x_hbm = refs[0]
        y_refs = refs[1:1 + ny]
        o_refs = refs[1 + ny:1 + 2 * ny]
        xv, sems = refs[1 + 2 * ny:]

        @pl.when(pl.program_id(0) == 0)
        def _():
            _load_resident([(x_hbm, xv)], sems)

        for y_ref, o_ref in zip(y_refs, o_refs):
            o_ref[...] = _dot(xv[...], y_ref[...].astype(BF16)).astype(BF16)

    steps = n // tn
    first = lambda: pl.program_id(0) == 0
    last = lambda: pl.program_id(0) == steps - 1
    return pl.pallas_call(
        _carried(body, 1 + ny, ny, carry, first, last), name=name, grid=(steps,),
        in_specs=[ANY] + [pl.BlockSpec((t, tn), lambda c: (0, c)) for _ in ys] + [ANY] * len(carry),
        out_specs=[pl.BlockSpec((m, tn), lambda c: (0, c)) for _ in ys] + [ANY] * len(carry),
        out_shape=[jax.ShapeDtypeStruct((m, n), BF16) for _ in ys] + _carry_shapes(carry),
        scratch_shapes=[pltpu.VMEM((m, t), BF16), pltpu.SemaphoreType.DMA((1,))] + _carry_scratch(carry),
        compiler_params=_params(("arbitrary",), VMEM_BIG),
    )(xt, *ys, *[a for _, a in carry])


def _shift_rows(u, halo, tm):
    r = lax.broadcasted_iota(jnp.int32, (tm, 1), 0)
    u1 = jnp.where(r == 0, halo[7:8], pltpu.roll(u, 1, 0))
    u2 = jnp.where(r == 0, halo[6:7], jnp.where(r == 1, halo[7:8], pltpu.roll(u, 2, 0)))
    return u1, u2


def _conv_fwd(xh, gi, bi, w_in, cw, w_out, go, bo, alpha, name, carry=()):
    t, d = xh.shape
    tm = _row_tile(t)
    nt = t // tm

    def body(xh_ref, gi_ref, bi_ref, win_ref, cw_ref, wout_ref, go_ref, bo_ref,
             xo_ref, rs_ref, hb_ref, p_ref, m_ref, halo):
        i = pl.program_id(0)

        @pl.when(i == 0)
        def _():
            halo[...] = jnp.zeros_like(halo)

        h = xh_ref[...] * gi_ref[...] + bi_ref[...]
        hb = h.astype(BF16)
        bg = _dot(hb, win_ref[:, 0:d])
        cg = _dot(hb, win_ref[:, d:2 * d])
        val = _dot(hb, win_ref[:, 2 * d:3 * d])
        p_ref[:, 0:d] = bg.astype(BF16)
        p_ref[:, d:2 * d] = cg.astype(BF16)
        p_ref[:, 2 * d:3 * d] = val.astype(BF16)
        rows = i * tm + lax.broadcasted_iota(jnp.int32, (tm, 1), 0)
        u = jnp.where(rows >= PAD, cg * val, 0.0)
        u1, u2 = _shift_rows(u, halo[...], tm)
        halo[...] = u[tm - 8:tm]
        y = cw_ref[0:1] * u2 + cw_ref[1:2] * u1 + cw_ref[2:3] * u
        mb = (bg * y).astype(BF16)
        m_ref[...] = mb.T
        xhat, rstd = _ln_fwd(alpha * h + _dot(mb, wout_ref[...]))
        xo_ref[...] = xhat
        rs_ref[...] = rstd
        hb_ref[...] = (xhat * go_ref[...] + bo_ref[...]).astype(BF16).T

    row = pl.BlockSpec((tm, d), lambda i: (i, 0))
    col = pl.BlockSpec((d, tm), lambda i: (0, i))
    vec = pl.BlockSpec((1, d), lambda i: (0, 0))
    first = lambda: pl.program_id(0) == 0
    last = lambda: pl.program_id(0) == nt - 1
    return pl.pallas_call(
        _carried(body, 8, 5, carry, first, last), name=name, grid=(nt,),
        in_specs=[row, vec, vec, pl.BlockSpec((d, 3 * d), lambda i: (0, 0)),
                  pl.BlockSpec((3, d), lambda i: (0, 0)), pl.BlockSpec((d, d), lambda i: (0, 0)),
                  vec, vec] + [ANY] * len(carry),
        out_specs=[row, pl.BlockSpec((tm, 1), lambda i: (i, 0)), col,
                   pl.BlockSpec((tm, 3 * d), lambda i: (i, 0)), col] + [ANY] * len(carry),
        out_shape=[jax.ShapeDtypeStruct((t, d), F32), jax.ShapeDtypeStruct((t, 1), F32),
                   jax.ShapeDtypeStruct((d, t), BF16), jax.ShapeDtypeStruct((t, 3 * d), BF16),
                   jax.ShapeDtypeStruct((d, t), BF16)] + _carry_shapes(carry),
        scratch_shapes=[pltpu.VMEM((8, d), F32)] + _carry_scratch(carry),
        compiler_params=_params(("arbitrary",), VMEM_BIG),
    )(xh, gi, bi, w_in, cw, w_out, go, bo, *[a for _, a in carry])


def _conv_bwd(dh, xo, rs, go, p, cw, w_in, w_out, alpha, name):
    t, d = dh.shape
    tm = _row_tile(t)
    nt = t // tm
    tb = tm // 8

    def body(dh_ref, xo_ref, rs_ref, go_ref, p_ref, ph_ref, cw_ref, win_ref, wout_ref,
             dhin_ref, dmix_ref, dp_ref, dcw_ref, dgain_ref, dbias_ref, carry):
        i = pl.program_id(0)
        tile = nt - 1 - i

        @pl.when(i == 0)
        def _():
            carry[...] = jnp.zeros_like(carry)
            dcw_ref[...] = jnp.zeros_like(dcw_ref)
            dgain_ref[...] = jnp.zeros_like(dgain_ref)
            dbias_ref[...] = jnp.zeros_like(dbias_ref)

        dz, dgp, dbp = _ln_bwd(dh_ref[...], xo_ref[...], rs_ref[...], go_ref[...])
        dgain_ref[...] += dgp
        dbias_ref[...] += dbp
        dmixb = dz.astype(BF16)
        dmix_ref[...] = dmixb
        dm = _dot_nt(dmixb, wout_ref[...])

        bg = p_ref[:, 0:d].astype(F32)
        cg = p_ref[:, d:2 * d].astype(F32)
        val = p_ref[:, 2 * d:3 * d].astype(F32)
        rows = tile * tm + lax.broadcasted_iota(jnp.int32, (tm, 1), 0)
        valid = rows >= PAD
        u = jnp.where(valid, cg * val, 0.0)
        hrows = tile * tm - 8 + lax.broadcasted_iota(jnp.int32, (8, 1), 0)
        hu = jnp.where((hrows >= PAD) & (tile > 0),
                       ph_ref[:, d:2 * d].astype(F32) * ph_ref[:, 2 * d:3 * d].astype(F32), 0.0)
        u1, u2 = _shift_rows(u, hu, tm)
        w0, w1, w2 = cw_ref[0:1], cw_ref[1:2], cw_ref[2:3]
        y = w0 * u2 + w1 * u1 + w2 * u
        dbg = dm * y
        dy = dm * bg
        dcw_ref[0:1] += jnp.sum(dy * u2, axis=0, keepdims=True)
        dcw_ref[1:2] += jnp.sum(dy * u1, axis=0, keepdims=True)
        dcw_ref[2:3] += jnp.sum(dy * u, axis=0, keepdims=True)

        nxt = carry[...]
        r = lax.broadcasted_iota(jnp.int32, (tm, 1), 0)
        dy1 = jnp.where(r == tm - 1, nxt[0:1], pltpu.roll(dy, tm - 1, 0))
        dy2 = jnp.where(r == tm - 2, nxt[0:1],
                        jnp.where(r == tm - 1, nxt[1:2], pltpu.roll(dy, tm - 2, 0)))
        carry[...] = dy[0:8]
        du = jnp.where(valid, w2 * dy + w1 * dy1 + w0 * dy2, 0.0)
        dbgb = dbg.astype(BF16)
        dcgb = (du * val).astype(BF16)
        dvalb = (du * cg).astype(BF16)
        dp_ref[:, 0:d] = dbgb
        dp_ref[:, d:2 * d] = dcgb
        dp_ref[:, 2 * d:3 * d] = dvalb
        dhin_ref[...] = (alpha * dz + _dot_nt(dbgb, win_ref[:, 0:d])
                         + _dot_nt(dcgb, win_ref[:, d:2 * d]) + _dot_nt(dvalb, win_ref[:, 2 * d:3 * d]))

    row = pl.BlockSpec((tm, d), lambda i: (nt - 1 - i, 0))
    vec = pl.BlockSpec((1, d), lambda i: (0, 0))
    prow = pl.BlockSpec((tm, 3 * d), lambda i: (nt - 1 - i, 0))
    return pl.pallas_call(
        body, name=name, grid=(nt,),
        in_specs=[row, row, pl.BlockSpec((tm, 1), lambda i: (nt - 1 - i, 0)), vec, prow,
                  pl.BlockSpec((8, 3 * d), lambda i: (jnp.maximum((nt - 1 - i) * tb - 1, 0), 0)),
                  pl.BlockSpec((3, d), lambda i: (0, 0)),
                  pl.BlockSpec((d, 3 * d), lambda i: (0, 0)), pl.BlockSpec((d, d), lambda i: (0, 0))],
        out_specs=[row, row, prow, pl.BlockSpec((3, d), lambda i: (0, 0)), vec, vec],
        out_shape=[jax.ShapeDtypeStruct((t, d), F32), jax.ShapeDtypeStruct((t, d), BF16),
                   jax.ShapeDtypeStruct((t, 3 * d), BF16), jax.ShapeDtypeStruct((3, d), F32),
                   jax.ShapeDtypeStruct((1, d), F32), jax.ShapeDtypeStruct((1, d), F32)],
        scratch_shapes=[pltpu.VMEM((8, d), F32)],
        compiler_params=_params(("arbitrary",), VMEM_BIG),
    )(dh, xo, rs, go, p, p, cw, w_in, w_out)


def _kv_fwd(xh, gi, bi, wk, wv, wf, fb, name, carry=()):
    t, d = xh.shape
    tm = _row_tile(t)
    nt = t // tm

    def body(xh_ref, gi_ref, bi_ref, wk_ref, wv_ref, wf_ref, fb_ref,
             k_ref, v_ref, lg_ref, c_ref, ct_ref, run):
        i = pl.program_id(0)

        @pl.when(i == 0)
        def _():
            run[...] = jnp.zeros_like(run)

        x = (xh_ref[...] * gi_ref[...] + bi_ref[...]).astype(BF16)
        k_ref[...] = _dot(x, wk_ref[...]).astype(BF16)
        v_ref[...] = _dot(x, wv_ref[...]).astype(BF16)
        logit = _dot(x, wf_ref[...]) + fb_ref[...]
        lg_ref[...] = logit
        logf = jnp.minimum(logit, 0.0) - jnp.log(1.0 + jnp.exp(-jnp.abs(logit)))
        rows = i * tm + lax.broadcasted_iota(jnp.int32, (tm, 1), 0)
        logf = jnp.where(rows >= PAD, logf, 0.0)
        tri = (lax.broadcasted_iota(jnp.int32, (tm, tm), 0)
               >= lax.broadcasted_iota(jnp.int32, (tm, tm), 1)).astype(F32)
        cs = jnp.dot(tri, logf, precision=lax.Precision.HIGHEST, preferred_element_type=F32) + run[...]
        run[...] = cs[tm - 1:tm]
        c_ref[...] = cs
        ct_ref[...] = cs.T

    row = pl.BlockSpec((tm, d), lambda i: (i, 0))
    vec = pl.BlockSpec((1, d), lambda i: (0, 0))
    gate = pl.BlockSpec((tm, LANES), lambda i: (i, 0))
    sq = pl.BlockSpec((d, d), lambda i: (0, 0))
    first = lambda: pl.program_id(0) == 0
    last = lambda: pl.program_id(0) == nt - 1
    return pl.pallas_call(
        _carried(body, 7, 5, carry, first, last), name=name, grid=(nt,),
        in_specs=[row, vec, vec, sq, sq, pl.BlockSpec((d, LANES), lambda i: (0, 0)),
                  pl.BlockSpec((1, LANES), lambda i: (0, 0))] + [ANY] * len(carry),
        out_specs=[row, row, gate, gate, pl.BlockSpec((LANES, tm), lambda i: (0, i))] + [ANY] * len(carry),
        out_shape=[jax.ShapeDtypeStruct((t, d), BF16), jax.ShapeDtypeStruct((t, d), BF16),
                   jax.ShapeDtypeStruct((t, LANES), F32), jax.ShapeDtypeStruct((t, LANES), F32),
                   jax.ShapeDtypeStruct((LANES, t), F32)] + _carry_shapes(carry),
        scratch_shapes=[pltpu.VMEM((1, LANES), F32)] + _carry_scratch(carry),
        compiler_params=_params(("arbitrary",), VMEM_MID),
    )(xh, gi, bi, wk, wv, wf, fb, *[a for _, a in carry])


def _kv_bwd(dk, dv, dcs, dcq, logit, dh_other, wk, wv, wf, name):
    t, d = dk.shape
    tm = _row_tile(t)
    nt = t // tm

    def body(dk_ref, dv_ref, dcs_ref, dcq_ref, lg_ref, oth_ref, wk_ref, wv_ref, wf_ref,
             dh_ref, dl_ref, dfb_ref, run):
        i = pl.program_id(0)
        tile = nt - 1 - i

        @pl.when(i == 0)
        def _():
            run[...] = jnp.zeros_like(run)
            dfb_ref[...] = jnp.zeros_like(dfb_ref)

        lane = lax.broadcasted_iota(jnp.int32, (tm, LANES), 1)
        dc = dcq_ref[...]
        for hh in range(N_HEADS):
            dc = dc + jnp.where(lane == hh, jnp.sum(dcs_ref[hh], axis=1, keepdims=True), 0.0)
        tri = (lax.broadcasted_iota(jnp.int32, (tm, tm), 0)
               <= lax.broadcasted_iota(jnp.int32, (tm, tm), 1)).astype(F32)
        dlf = jnp.dot(tri, dc, precision=lax.Precision.HIGHEST, preferred_element_type=F32) + run[...]
        run[...] = dlf[0:1]
        rows = tile * tm + lax.broadcasted_iota(jnp.int32, (tm, 1), 0)
        dlogit = jnp.where(rows >= PAD, dlf * jax.nn.sigmoid(-lg_ref[...]), 0.0)
        dfb_ref[...] += jnp.sum(dlogit, axis=0, keepdims=True)
        dlb = dlogit.astype(BF16)
        dl_ref[...] = dlb
        dh_ref[...] = (oth_ref[...] + _dot_nt(dk_ref[...], wk_ref[...])
                       + _dot_nt(dv_ref[...], wv_ref[...]) + _dot_nt(dlb, wf_ref[...]))

    row = pl.BlockSpec((tm, d), lambda i: (nt - 1 - i, 0))
    gate = pl.BlockSpec((tm, LANES), lambda i: (nt - 1 - i, 0))
    sq = pl.BlockSpec((d, d), lambda i: (0, 0))
    return pl.pallas_call(
        body, name=name, grid=(nt,),
        in_specs=[row, row, pl.BlockSpec((N_HEADS, tm, LANES), lambda i: (0, nt - 1 - i, 0)), gate, gate, row,
                  sq, sq, pl.BlockSpec((d, LANES), lambda i: (0, 0))],
        out_specs=[row, gate, pl.BlockSpec((1, LANES), lambda i: (0, 0))],
        out_shape=[jax.ShapeDtypeStruct((t, d), F32), jax.ShapeDtypeStruct((t, LANES), BF16),
                   jax.ShapeDtypeStruct((1, LANES), F32)],
        scratch_shapes=[pltpu.VMEM((1, LANES), F32)],
        compiler_params=_params(("arbitrary",), VMEM_MID),
    )(dk, dv, dcs, dcq, logit, dh_other, wk, wv, wf)


def _proj(xh, gi, bi, w, name):
    t, k = xh.shape
    n = w.shape[1]
    tm = _row_tile(t)

    def body(x_ref, g_ref, b_ref, w_ref, o_ref):
        x = (x_ref[...] * g_ref[...] + b_ref[...]).astype(BF16)
        o_ref[...] = _dot(x, w_ref[...]).astype(BF16)

    vec = pl.BlockSpec((1, k), lambda i: (0, 0))
    return pl.pallas_call(
        body, name=name, grid=(t // tm,),
        in_specs=[pl.BlockSpec((tm, k), lambda i: (i, 0)), vec, vec, pl.BlockSpec((k, n), lambda i: (0, 0))],
        out_specs=pl.BlockSpec((tm, n), lambda i: (i, 0)),
        out_shape=jax.ShapeDtypeStruct((t, n), BF16),
        compiler_params=_params(("arbitrary",), VMEM_MID),
    )(xh, gi, bi, w)


def _add_proj_nt(base, y, w, name):
    t, n = y.shape
    k = w.shape[0]
    tm = _row_tile(t)

    def body(b_ref, y_ref, w_ref, o_ref):
        o_ref[...] = b_ref[...] + _dot_nt(y_ref[...].astype(BF16), w_ref[...])

    return pl.pallas_call(
        body, name=name, grid=(t // tm,),
        in_specs=[pl.BlockSpec((tm, k), lambda i: (i, 0)), pl.BlockSpec((tm, n), lambda i: (i, 0)),
                  pl.BlockSpec((k, n), lambda i: (0, 0))],
        out_specs=pl.BlockSpec((tm, k), lambda i: (i, 0)),
        out_shape=jax.ShapeDtypeStruct((t, k), F32),
        compiler_params=_params(("arbitrary",), VMEM_MID),
    )(base, y, w)


def _attn_out_fwd(ot, xh, gi, bi, w_o, go, bo, alpha, name):
    t, d = xh.shape
    tm = _row_tile(t)

    def body(ot_ref, xh_ref, gi_ref, bi_ref, wo_ref, go_ref, bo_ref, xo_ref, rs_ref, hb_ref):
        h = xh_ref[...] * gi_ref[...] + bi_ref[...]
        xhat, rstd = _ln_fwd(alpha * h + _dot_tn(ot_ref[...], wo_ref[...]))
        xo_ref[...] = xhat
        rs_ref[...] = rstd
        hb_ref[...] = (xhat * go_ref[...] + bo_ref[...]).astype(BF16).T

    row = pl.BlockSpec((tm, d), lambda i: (i, 0))
    col = pl.BlockSpec((d, tm), lambda i: (0, i))
    vec = pl.BlockSpec((1, d), lambda i: (0, 0))
    return pl.pallas_call(
        body, name=name, grid=(t // tm,),
        in_specs=[col, row, vec, vec, pl.BlockSpec((d, d), lambda i: (0, 0)), vec, vec],
        out_specs=[row, pl.BlockSpec((tm, 1), lambda i: (i, 0)), col],
        out_shape=[jax.ShapeDtypeStruct((t, d), F32), jax.ShapeDtypeStruct((t, 1), F32),
                   jax.ShapeDtypeStruct((d, t), BF16)],
        compiler_params=_params(("arbitrary",), VMEM_MID),
    )(ot, xh, gi, bi, w_o, go, bo)


def _attn_out_bwd(dh, xo, rs, go, ot, w_o, alpha, name):
    t, d = dh.shape
    tm = _row_tile(t)
    hd = d // N_HEADS

    def body(dh_ref, xo_ref, rs_ref, go_ref, ot_ref, wo_ref,
             dres_ref, dmix_ref, dot_ref, delta_ref, dgain_ref, dbias_ref):
        @pl.when(pl.program_id(0) == 0)
        def _():
            dgain_ref[...] = jnp.zeros_like(dgain_ref)
            dbias_ref[...] = jnp.zeros_like(dbias_ref)

        dz, dgp, dbp = _ln_bwd(dh_ref[...], xo_ref[...], rs_ref[...], go_ref[...])
        dgain_ref[...] += dgp
        dbias_ref[...] += dbp
        dres_ref[...] = alpha * dz
        dmixb = dz.astype(BF16)
        dmix_ref[...] = dmixb
        dot_t = _dot_nt(wo_ref[...], dmixb)
        dot_ref[...] = dot_t.astype(BF16)
        prod = dot_t * ot_ref[...].astype(F32)
        delta_ref[...] = jnp.sum(prod.reshape(N_HEADS, hd, tm), axis=1)

    row = pl.BlockSpec((tm, d), lambda i: (i, 0))
    vec = pl.BlockSpec((1, d), lambda i: (0, 0))
    col = pl.BlockSpec((d, tm), lambda i: (0, i))
    return pl.pallas_call(
        body, name=name, grid=(t // tm,),
        in_specs=[row, row, pl.BlockSpec((tm, 1), lambda i: (i, 0)), vec, col,
                  pl.BlockSpec((d, d), lambda i: (0, 0))],
        out_specs=[row, row, col, pl.BlockSpec((N_HEADS, tm), lambda i: (0, i)), vec, vec],
        out_shape=[jax.ShapeDtypeStruct((t, d), F32), jax.ShapeDtypeStruct((t, d), BF16),
                   jax.ShapeDtypeStruct((d, t), BF16), jax.ShapeDtypeStruct((N_HEADS, t), F32),
                   jax.ShapeDtypeStruct((1, d), F32), jax.ShapeDtypeStruct((1, d), F32)],
        compiler_params=_params(("arbitrary",), VMEM_MID),
    )(dh, xo, rs, go, ot, w_o)


def _scores_t(k, q, ct_ref, c_ref, h, i, j, tq, tk, scale, masked):
    sub = lax.broadcasted_iota(jnp.int32, (8, tq), 0)
    cq = jnp.sum(jnp.where(sub == h, ct_ref[...], 0.0), axis=0, keepdims=True) * LOG2E
    lane = lax.broadcasted_iota(jnp.int32, (tk, LANES), 1)
    ck = jnp.sum(jnp.where(lane == h, c_ref[...], 0.0), axis=1, keepdims=True) * LOG2E
    st = _dot_nt(k, q) * (scale * LOG2E) - ck
    if masked:
        kpos = j * tk + lax.broadcasted_iota(jnp.int32, (tk, 1), 0)
        qpos = i * tq + lax.broadcasted_iota(jnp.int32, (1, tq), 1)
        st = jnp.where((kpos <= qpos) & (kpos >= PAD), st, NEG_INF)
    return st, cq


def _tri_pairs(n, by_row):
    if by_row:
        pairs = [(i, j) for i in range(n) for j in range(i + 1)]
    else:
        pairs = [(i, j) for j in range(n) for i in range(j, n)]
    return (jnp.asarray([p[0] for p in pairs], jnp.int32), jnp.asarray([p[1] for p in pairs], jnp.int32))


def _attn_fwd(q, k, v, c, ct, name, carry=()):
    t, d = q.shape
    hd = d // N_HEADS
    tq = tk = _row_tile(t)
    nq = t // tq
    scale = 1.0 / math.sqrt(hd)

    hps = ATTN_HEADS_PER_STEP

    def body(it_ref, jt_ref, q_ref, k_ref, v_ref, c_ref, ct_ref, ot_ref, lse_ref, m_s, l_s, acc):
        hp, p_ = pl.program_id(0), pl.program_id(1)
        i, j = it_ref[p_], jt_ref[p_]

        @pl.when(j == 0)
        def _():
            m_s[...] = jnp.full_like(m_s, NEG_INF)
            l_s[...] = jnp.zeros_like(l_s)
            acc[...] = jnp.zeros_like(acc)

        def update(masked):
            scores = []
            for e in range(hps):
                cols = slice(e * hd, (e + 1) * hd)
                scores.append(_scores_t(k_ref[:, cols], q_ref[:, cols], ct_ref, c_ref, hp * hps + e,
                                        i, j, tq, tk, scale, masked))
            probs = []
            for e, (st, cq) in enumerate(scores):
                m_new = jnp.maximum(m_s[e], jnp.max(st, axis=0, keepdims=True) + cq)
                a = jnp.exp2(m_s[e] - m_new)
                p = jnp.exp2(st - (m_new - cq))
                l_s[e] = a * l_s[e] + jnp.sum(p, axis=0, keepdims=True)
                m_s[e] = m_new
                probs.append((a, p.astype(BF16)))
            for e, (a, pb) in enumerate(probs):
                acc[e] = a * acc[e] + _dot_tn(v_ref[:, e * hd:(e + 1) * hd], pb)

        edge = (j == i) | (j == 0)
        pl.when(edge)(lambda: update(True))
        pl.when(jnp.logical_not(edge))(lambda: update(False))

        @pl.when(j == i)
        def _():
            for e in range(hps):
                ot_ref[e * hd:(e + 1) * hd, :] = (acc[e] / l_s[e]).astype(BF16)
                lse_ref[e] = m_s[e] + jnp.log2(l_s[e])

    it, jt = _tri_pairs(nq, by_row=True)
    npairs = it.shape[0]
    nhp = N_HEADS // hps
    kv = pl.BlockSpec((tk, hps * hd), lambda h, p, it, jt: (jt[p], h))
    first = lambda: (pl.program_id(0) == 0) & (pl.program_id(1) == 0)
    last = lambda: (pl.program_id(0) == nhp - 1) & (pl.program_id(1) == npairs - 1)
    return pl.pallas_call(
        _carried(body, 7, 2, carry, first, last), name=name,
        grid_spec=pltpu.PrefetchScalarGridSpec(
            num_scalar_prefetch=2, grid=(nhp, npairs),
            in_specs=[pl.BlockSpec((tq, hps * hd), lambda h, p, it, jt: (it[p], h)), kv, kv,
                      pl.BlockSpec((tk, LANES), lambda h, p, it, jt: (jt[p], 0)),
                      pl.BlockSpec((8, tq), lambda h, p, it, jt: (0, it[p]))] + [ANY] * len(carry),
            out_specs=[pl.BlockSpec((hps * hd, tq), lambda h, p, it, jt: (h, it[p])),
                       pl.BlockSpec((hps, 1, tq), lambda h, p, it, jt: (h, 0, it[p]))] + [ANY] * len(carry),
            scratch_shapes=[pltpu.VMEM((hps, 1, tq), F32), pltpu.VMEM((hps, 1, tq), F32),
                            pltpu.VMEM((hps, hd, tq), F32)] + _carry_scratch(carry)),
        out_shape=[jax.ShapeDtypeStruct((d, t), BF16), jax.ShapeDtypeStruct((N_HEADS, 1, t), F32)]
                  + _carry_shapes(carry),
        compiler_params=_params(("arbitrary", "arbitrary"), VMEM_MID),
    )(it, jt, q, k, v, c, ct, *[a for _, a in carry])


def _attn_bwd(q, k, v, c, ct, lse, delta, dot_t, name, carry=()):
    t, d = q.shape
    hd = d // N_HEADS
    tq = tk = _row_tile(t)
    nq = t // tq
    scale = 1.0 / math.sqrt(hd)
    hps = ATTN_BWD_HEADS_PER_STEP

    steps = [(j, i, min(i + 1, nq - 1), int(i + 1 < nq)) for j in range(nq) for i in range(j, nq, 2)]
    jt, ia, ib, vb = (jnp.asarray([s[n] for s in steps], jnp.int32) for n in range(4))

    def body(jt_ref, ia_ref, ib_ref, vb_ref, qa_ref, k_ref, v_ref, c_ref, cta_ref, lsea_ref, deltaa_ref, dota_ref,
             qb_ref, ctb_ref, lseb_ref, deltab_ref, dotb_ref,
             dq_ref, dk_ref, dv_ref, dcs_ref, drow_ref, dk_acc, dv_acc, dc_acc):
        hp, p_ = pl.program_id(0), pl.program_id(1)
        j, i_a, i_b, has_b = jt_ref[p_], ia_ref[p_], ib_ref[p_], vb_ref[p_] == 1

        @pl.when(p_ == 0)
        def _():
            dq_ref[...] = jnp.zeros_like(dq_ref)
            drow_ref[...] = jnp.zeros_like(drow_ref)

        @pl.when(i_a == j)
        def _():
            dk_acc[...] = jnp.zeros_like(dk_acc)
            dv_acc[...] = jnp.zeros_like(dv_acc)
            dc_acc[...] = jnp.zeros_like(dc_acc)

        def update(q_ref, ct_ref, lse_ref, delta_ref, dot_ref, i, masked):
            sub = lax.broadcasted_iota(jnp.int32, (8, tq), 0)
            rows = pl.ds(pl.multiple_of(i * tq, tq), tq)
            stage = []
            for e in range(hps):
                cols = slice(e * hd, (e + 1) * hd)
                st, cq = _scores_t(k_ref[:, cols], q_ref[:, cols], ct_ref, c_ref, hp * hps + e,
                                   i, j, tq, tk, scale, masked)
                dp = _dot(v_ref[:, cols], dot_ref[cols, :])
                stage.append((st, cq, dp))
            grads = []
            for e, (st, cq, dp) in enumerate(stage):
                p = jnp.exp2(st - (lse_ref[e] - cq))
                dl = jnp.sum(jnp.where(sub == hp * hps + e, delta_ref[...], 0.0), axis=0, keepdims=True)
                ds = p * (dp - dl)
                part = ds[:, 0:LANES]
                for g in range(1, tq // LANES):
                    part = part + ds[:, g * LANES:(g + 1) * LANES]
                dc_acc[e] += part
                drow_ref[e, i] += jnp.broadcast_to(jnp.sum(ds, axis=0, keepdims=True), (8, tq))
                grads.append((p.astype(BF16), ds.astype(BF16)))
            for e, (pb, dsb) in enumerate(grads):
                cols = slice(e * hd, (e + 1) * hd)
                dv_acc[e] += _dot_nt(pb, dot_ref[cols, :])
                dk_acc[e] += _dot(dsb, q_ref[:, cols]) * scale
                dq_ref[rows, cols] += _dot_tn(dsb, k_ref[:, cols]) * scale

        slot_a = (qa_ref, cta_ref, lsea_ref, deltaa_ref, dota_ref, i_a)
        slot_b = (qb_ref, ctb_ref, lseb_ref, deltab_ref, dotb_ref, i_b)
        edge_a = (j == i_a) | (j == 0)
        pl.when(edge_a)(lambda: update(*slot_a, True))
        pl.when(jnp.logical_not(edge_a))(lambda: update(*slot_a, False))
        pl.when(has_b & (j == 0))(lambda: update(*slot_b, True))
        pl.when(has_b & (j != 0))(lambda: update(*slot_b, False))

        @pl.when((i_a == nq - 1) | (has_b & (i_b == nq - 1)))
        def _():
            for e in range(hps):
                cols = slice(e * hd, (e + 1) * hd)
                dk_ref[:, cols] = dk_acc[e].astype(BF16)
                dv_ref[:, cols] = dv_acc[e].astype(BF16)
                dcs_ref[e] = -dc_acc[e]

    nsteps = len(steps)
    nhp = N_HEADS // hps
    kv = pl.BlockSpec((tk, hps * hd), lambda h, p, jt, ia, ib, vb: (jt[p], h))

    def q_side(sel):
        return [pl.BlockSpec((tq, hps * hd), lambda h, p, jt, ia, ib, vb: (sel(ia, ib)[p], h)),
                pl.BlockSpec((8, tq), lambda h, p, jt, ia, ib, vb: (0, sel(ia, ib)[p])),
                pl.BlockSpec((hps, 1, tq), lambda h, p, jt, ia, ib, vb: (h, 0, sel(ia, ib)[p])),
                pl.BlockSpec((N_HEADS, tq), lambda h, p, jt, ia, ib, vb: (0, sel(ia, ib)[p])),
                pl.BlockSpec((hps * hd, tq), lambda h, p, jt, ia, ib, vb: (h, sel(ia, ib)[p]))]

    qa_specs, qb_specs = q_side(lambda ia, ib: ia), q_side(lambda ia, ib: ib)
    first = lambda: (pl.program_id(0) == 0) & (pl.program_id(1) == 0)
    last = lambda: (pl.program_id(0) == nhp - 1) & (pl.program_id(1) == nsteps - 1)
    q_args = (q, ct, lse, delta, dot_t)
    return pl.pallas_call(
        _carried(body, 17, 5, carry, first, last), name=name,
        grid_spec=pltpu.PrefetchScalarGridSpec(
            num_scalar_prefetch=4, grid=(nhp, nsteps),
            in_specs=[qa_specs[0], kv, kv, pl.BlockSpec((tk, LANES), lambda h, p, jt, ia, ib, vb: (jt[p], 0))]
                     + qa_specs[1:] + qb_specs + [ANY] * len(carry),
            out_specs=[pl.BlockSpec((t, hps * hd), lambda h, p, jt, ia, ib, vb: (0, h)), kv, kv,
                       pl.BlockSpec((hps, tk, LANES), lambda h, p, jt, ia, ib, vb: (h, jt[p], 0)),
                       pl.BlockSpec((hps, nq, 8, tq), lambda h, p, jt, ia, ib, vb: (h, 0, 0, 0))]
                      + [ANY] * len(carry),
            scratch_shapes=[pltpu.VMEM((hps, tk, hd), F32), pltpu.VMEM((hps, tk, hd), F32),
                            pltpu.VMEM((hps, tk, LANES), F32)] + _carry_scratch(carry)),
        out_shape=[jax.ShapeDtypeStruct((t, d), F32), jax.ShapeDtypeStruct((t, d), BF16),
                   jax.ShapeDtypeStruct((t, d), BF16), jax.ShapeDtypeStruct((N_HEADS, t, LANES), F32),
                   jax.ShapeDtypeStruct((N_HEADS, nq, 8, tq), F32)] + _carry_shapes(carry),
        compiler_params=_params(("arbitrary", "arbitrary"), VMEM_BIG),
    )(jt, ia, ib, vb, q, k, v, c, ct, lse, delta, dot_t, *q_args, *[a for _, a in carry])


def _adamw(w, g, m, v, name):
    r, c = w.shape
    tr = r
    for cand in (256, 128, 64, 32, 16, 8):
        if r % cand == 0 and r > cand:
            tr = cand
            break
    bc1 = 1.0 - ADAM_B1 ** ADAM_STEP
    bc2 = 1.0 - ADAM_B2 ** ADAM_STEP

    def body(w_ref, g_ref, m_ref, v_ref, d_ref, nm_ref, nv_ref):
        gg = g_ref[...]
        nm = ADAM_B1 * m_ref[...] + (1.0 - ADAM_B1) * gg
        nv = ADAM_B2 * v_ref[...] + (1.0 - ADAM_B2) * (gg * gg)
        d_ref[...] = -ADAM_LR * ((nm / bc1) / (jnp.sqrt(nv / bc2) + ADAM_EPS) + ADAM_WD * w_ref[...])
        nm_ref[...] = nm
        nv_ref[...] = nv

    blk = pl.BlockSpec((tr, c), lambda i: (i, 0))
    shp = jax.ShapeDtypeStruct((r, c), F32)
    return pl.pallas_call(
        body, name=name, grid=(r // tr,), in_specs=[blk] * 4, out_specs=[blk] * 3,
        out_shape=[shp] * 3, compiler_params=_params(("arbitrary",), VMEM_MID),
    )(w, g, m, v)


def _reduce_adamw(w, m, v, landed, name):
    nl, r, c = w.shape
    tr = next(cand for cand in range(min(r, ADAM_ROWS_MAX), 0, -BF16_ROWS) if r % cand == 0)
    nr = r // tr
    bc1 = 1.0 - ADAM_B1 ** ADAM_STEP
    bc2 = 1.0 - ADAM_B2 ** ADAM_STEP

    def body(*refs):
        w_ref, m_ref, v_ref = refs[:3]
        src_refs = refs[3:3 + nl]
        g_ref, d_ref, nm_ref, nv_ref = refs[3 + nl:]

        def update(src):
            gg = src[0].astype(F32)
            for s in range(1, N_DEV):
                gg = gg + src[s].astype(F32)
            nm = ADAM_B1 * m_ref[0] + (1.0 - ADAM_B1) * gg
            nv = ADAM_B2 * v_ref[0] + (1.0 - ADAM_B2) * (gg * gg)
            g_ref[0] = gg
            d_ref[0] = -ADAM_LR * ((nm / bc1) / (jnp.sqrt(nv / bc2) + ADAM_EPS) + ADAM_WD * w_ref[0])
            nm_ref[0] = nm
            nv_ref[0] = nv

        for idx in range(nl):
            pl.when(pl.program_id(0) == idx)(functools.partial(update, src_refs[idx]))

    def src_spec(idx):
        return pl.BlockSpec((N_DEV, tr, c),
                            lambda l, i: (0, jnp.where(l == idx, i, jnp.where(l < idx, 0, nr - 1)), 0))

    blk = pl.BlockSpec((1, tr, c), lambda l, i: (l, i, 0))
    shp = jax.ShapeDtypeStruct((nl, r, c), F32)
    return pl.pallas_call(
        body, name=name, grid=(nl, nr), in_specs=[blk] * 3 + [src_spec(idx) for idx in range(nl)],
        out_specs=[blk] * 4, out_shape=[shp] * 4,
        compiler_params=_params(("arbitrary", "arbitrary"), VMEM_MID),
    )(w, m, v, *landed)


def _sum_sources(r, name):
    n, rows, c = r.shape
    tr = next(cand for cand in range(min(rows, SUM_ROWS_MAX), 0, -BF16_ROWS) if rows % cand == 0)

    def body(r_ref, o_ref):
        acc = r_ref[0].astype(F32)
        for s in range(1, n):
            acc = acc + r_ref[s].astype(F32)
        o_ref[...] = acc

    return pl.pallas_call(
        body, name=name, grid=(rows // tr,),
        in_specs=[pl.BlockSpec((n, tr, c), lambda i: (0, i, 0))],
        out_specs=pl.BlockSpec((tr, c), lambda i: (i, 0)),
        out_shape=jax.ShapeDtypeStruct((rows, c), F32),
        compiler_params=_params(("arbitrary",), VMEM_MID),
    )(r)


def _all_gather(parts, name):
    n = len(parts)

    def body(*refs):
        x_refs, out_refs = refs[:n], refs[n:2 * n]
        send_sems, recv_sems, local_sems = refs[2 * n:]
        mx, my, mc = lax.axis_index("x"), lax.axis_index("y"), lax.axis_index("c")
        me, sibling = (mx, my, mc), (mx, my, 1 - mc)
        chips = [(1 - mx, my), (mx, 1 - my), (1 - mx, 1 - my)]

        def copy(p, k, block, to, from_input=False):
            px, py, pc = block
            rows = out_refs[p].at[4 * px + 2 * py + pc]
            return pltpu.make_async_remote_copy(
                src_ref=x_refs[p] if from_input else rows, dst_ref=rows,
                send_sem=send_sems.at[7 * p + k], recv_sem=recv_sems.at[7 * p + k],
                device_id=to, device_id_type=MESH)

        mine, sent = [], []
        for p in range(n):
            own = pltpu.make_async_copy(x_refs[p], out_refs[p].at[4 * mx + 2 * my + mc], local_sems.at[p])
            own.start()
            mine.append(own)
            first = [copy(p, 0, me, sibling, True)]
            first += [copy(p, 1 + j, me, (*chip, mc), True) for j, chip in enumerate(chips)]
            for cp in first:
                cp.start()
            sent += first
        for p in range(n):
            for j, chip in enumerate(chips):
                copy(p, 1 + j, (*chip, mc), me).wait_recv()
                fwd = copy(p, 4 + j, (*chip, mc), sibling)
                fwd.start()
                sent.append(fwd)
        for p in range(n):
            copy(p, 0, sibling, me).wait_recv()
            for j, chip in enumerate(chips):
                copy(p, 4 + j, (*chip, 1 - mc), me).wait_recv()
        for cp in sent:
            cp.wait_send()
        for own in mine:
            own.wait()

    return pl.pallas_call(
        body, name=name, in_specs=[ANY] * n, out_specs=[ANY] * n,
        out_shape=[jax.ShapeDtypeStruct((N_DEV,) + a.shape, a.dtype) for a in parts],
        scratch_shapes=[pltpu.SemaphoreType.DMA((7 * n,)), pltpu.SemaphoreType.DMA((7 * n,)),
                        pltpu.SemaphoreType.DMA((n,))],
    )(*parts)


def _pack_rows(parts, width, mult, lead=0):
    out = []
    for a in parts:
        head = a.shape[:lead]
        flat = a.reshape(head + (-1,))
        padn = (-flat.shape[-1]) % (width * mult)
        if padn:
            flat = jnp.pad(flat, [(0, 0)] * lead + [(0, padn)])
        out.append(flat.reshape(head + (-1, width)))
    return jnp.concatenate(out, axis=lead)


def _rows_of(shape, width, mult):
    n = math.prod(shape)
    per = width * mult
    return ((n + per - 1) // per) * mult


def _unpack_rows(buf, shapes, width, mult):
    lead = buf.shape[:-2]
    out, off = [], 0
    for shp in shapes:
        r = _rows_of(shp, width, mult)
        flat = buf[..., off:off + r, :].reshape(lead + (r * width,))
        out.append(flat[..., :math.prod(shp)].reshape(lead + tuple(shp)))
        off += r
    return out


def _cols_from_devices(g):
    nd = g.ndim
    perm = tuple(range(1, nd - 1)) + (0, nd - 1)
    t = jnp.transpose(g, perm)
    return t.reshape(t.shape[:-2] + (t.shape[-2] * t.shape[-1],))


def _cols_to_devices(a):
    c = a.shape[-1] // N_DEV
    t = a.reshape(a.shape[:-1] + (N_DEV, c))
    nd = t.ndim
    perm = (nd - 2,) + tuple(range(0, nd - 2)) + (nd - 1,)
    return jnp.transpose(t, perm)


WIDTH = 1024


def kernel(x, meta, ffn1_wg, ffn1_wu, ffn1_wd, ffn2_wg, ffn2_wu, ffn2_wd, ln_gain, ln_bias, conv_w_in, conv_w, conv_w_out, kv_w, f_bias, attn_w_q, attn_w_o, loss_target, m_meta, m_ffn1_wg, m_ffn1_wu, m_ffn1_wd, m_ffn2_wg, m_ffn2_wu, m_ffn2_wd, m_ln_gain, m_ln_bias, m_conv_w_in, m_conv_w, m_conv_w_out, m_kv_w, m_f_bias, m_attn_w_q, m_attn_w_o, v_meta, v_ffn1_wg, v_ffn1_wu, v_ffn1_wd, v_ffn2_wg, v_ffn2_wu, v_ffn2_wd, v_ln_gain, v_ln_bias, v_conv_w_in, v_conv_w, v_conv_w_out, v_kv_w, v_f_bias, v_attn_w_q, v_attn_w_o):
    depth = ln_gain.shape[0]
    alpha = float((2 * depth) ** 0.25)
    d = x.shape[-1]
    seq = x.shape[1]
    t = ROW0 + seq
    fsh = ffn1_wg.shape[-1]
    f = fsh * N_DEV
    fck = MXU_COLS
    nc = f // fck
    me = 4 * lax.axis_index("x") + 2 * lax.axis_index("y") + lax.axis_index("c")

    def gather_of(parts):
        return [(True, a.astype(BF16)) for a in parts]

    small = [meta, ln_gain, ln_bias, conv_w]
    small_shapes = [a.shape for a in small]
    g1g, g1u, g1d, gcin, gcout, gsmall = _all_gather(
        [a.astype(BF16) for a in (ffn1_wg[0], ffn1_wu[0], ffn1_wd[0], conv_w_in[0], conv_w_out[0])]
        + [_pack_rows(small, WIDTH, F32_ROWS)], "ag_first")
    gmeta, ggain, gbias, gcw = _unpack_rows(gsmall, small_shapes, WIDTH, F32_ROWS)

    def ffn_chunks(gg, gu, gd):
        up = lambda g: jnp.transpose(_cols_from_devices(g).reshape(d, nc, fck), (1, 0, 2))
        return up(gg), up(gu), gd.reshape(nc, fck, d)

    w_in = _cols_from_devices(gcin)
    w_out = gcout.reshape(d, d)
    fb = jnp.pad(f_bias, (0, LANES - N_HEADS)).reshape(1, LANES)
    meta_f = _cols_from_devices(gmeta)
    gain_f = _cols_from_devices(ggain)
    bias_f = _cols_from_devices(gbias)
    cw_f = _cols_from_devices(gcw)[0]

    def gb(l, n):
        return gain_f[l, n].reshape(1, d), bias_f[l, n].reshape(1, d)

    ones = jnp.ones((1, d), F32)
    zeros = jnp.zeros((1, d), F32)

    h0 = jnp.concatenate([jnp.zeros((PAD, d), F32), meta_f, x[0]], axis=0)

    w1 = ffn_chunks(g1g, g1u, g1d)
    g00, b00 = gb(0, 0)
    xh1, rs1, hb1, gg1, uu1, hb0, g2g, g2u = _ffn_fwd(
        h0, ones, zeros, *w1, g00, b00, alpha, "ffn_fwd_0a", carry=gather_of([ffn2_wg[0], ffn2_wu[0]]),
        input_t=True)
    g01, b01 = gb(0, 1)
    xh2, rs2, hb2, pp, mb, g2d, gkv = _conv_fwd(
        xh1, g00, b00, w_in, cw_f, w_out, g01, b01, alpha, "conv_fwd", carry=gather_of([ffn2_wd[0], kv_w.T]))
    w2 = ffn_chunks(g2g, g2u, g2d)
    g02, b02 = gb(0, 2)
    xh3, rs3, hb3, gg3, uu3, g3g, g3u = _ffn_fwd(
        xh2, g01, b01, *w2, g02, b02, alpha, "ffn_fwd_0b", carry=gather_of([ffn1_wg[1], ffn1_wu[1]]))
    kvw = gkv.reshape(gkv.shape[0] * gkv.shape[1], d).T
    wk, wv = kvw[:, :d], kvw[:, d:2 * d]
    wf = jnp.pad(kvw[:, 2 * d:], ((0, 0), (0, LANES - N_HEADS)))
    kk, vv, logit, cc, cct, g3d = _kv_fwd(xh3, g02, b02, wk, wv, wf, fb, "kv_fwd",
                                          carry=gather_of([ffn1_wd[1]]))

    w3 = ffn_chunks(g3g, g3u, g3d)
    g10, b10 = gb(1, 0)
    xh4, rs4, hb4, gg4, uu4, gwq = _ffn_fwd(xh3, g02, b02, *w3, g10, b10, alpha, "ffn_fwd_1a",
                                             carry=gather_of([attn_w_q[0]]))
    w_q = gwq.reshape(d, d)
    qq = _proj(xh4, g10, b10, w_q, "q_proj")
    ot, lse, gwo, g4g, g4u, g4d = _attn_fwd(
        qq, kk, vv, cc, cct, "attn_fwd", carry=gather_of([attn_w_o[0], ffn2_wg[1], ffn2_wu[1], ffn2_wd[1]]))
    w_o = gwo.reshape(d, d)
    g11, b11 = gb(1, 1)
    xh5, rs5, hb5 = _attn_out_fwd(ot, xh4, g10, b10, w_o, g11, b11, alpha, "attn_out_fwd")
    w4 = ffn_chunks(g4g, g4u, g4d)
    g12, b12 = gb(1, 2)
    xh6, rs6, _, gg6, uu6 = _ffn_fwd(xh5, g11, b11, *w4, g12, b12, alpha, "ffn_fwd_1b")


    dgain = [[None] * 3 for _ in range(depth)]
    dbias = [[None] * 3 for _ in range(depth)]

    def to_col_owners(g):
        return (False, _cols_to_devices(g).astype(BF16))

    def to_row_owners(g):
        return (False, g.reshape(N_DEV, g.shape[0] // N_DEV, g.shape[1]).astype(BF16))

    dh5, do6, dg6, du6, a6, dgain[1][2], dbias[1][2], loss_l = _ffn_bwd(
        None, xh6, rs6, g12, gg6, uu6, *w4, alpha, "ffn_bwd_1b", loss_target=loss_target[0], loss_bias=b12)
    loss = lax.psum(loss_l[0, 0], ("x", "y", "c"))
    dw4g, dw4u = _wgrad(hb5, [dg6, du6], "wgrad_up_1b")
    (dw4dt,) = _wgrad(do6, [a6], "wgrad_down_1b")

    dres4, dmix5, dot_t, delta, dgain[1][1], dbias[1][1] = _attn_out_bwd(dh5, xh5, rs5, g11, ot, w_o, alpha, "attn_out_bwd")
    (dwo,) = _wgrad(ot, [dmix5], "wgrad_wo")
    dq, dkk, dvv, dcs, drow, l4g, l4u, l4d, lwo = _attn_bwd(
        qq, kk, vv, cc, cct, lse, delta, dot_t, "attn_bwd",
        carry=[to_col_owners(dw4g), to_col_owners(dw4u), to_row_owners(dw4dt.T), to_row_owners(dwo)])
    dh4 = _add_proj_nt(dres4, dq, w_q, "q_bwd")
    (dwq,) = _wgrad(hb4, [dq], "wgrad_wq")

    dh3a, do4, dg4, du4, a4, dgain[1][0], dbias[1][0] = _ffn_bwd(dh4, xh4, rs4, g10, gg4, uu4, *w3, alpha, "ffn_bwd_1a")
    dw3g, dw3u = _wgrad(hb3, [dg4, du4], "wgrad_up_1a")
    (dw3dt,) = _wgrad(do4, [a4], "wgrad_down_1a")

    dcq = jnp.pad(drow[:, :, 0, :].reshape(N_HEADS, t).T, ((0, 0), (0, LANES - N_HEADS)))
    dh3, dlogit, dfb = _kv_bwd(dkk, dvv, dcs, dcq, logit, dh3a, wk, wv, wf, "kv_bwd")
    dwk, dwv = _wgrad(hb3, [dkk, dvv], "wgrad_kv")
    (dwf,) = _wgrad(hb3, [dlogit], "wgrad_f")
    dkv = jnp.concatenate([dwk, dwv, dwf[:, :N_HEADS]], axis=1)

    dh2, do3, dg3, du3, a3, dgain[0][2], dbias[0][2], lwq, l3g, l3u, l3d, lkv = _ffn_bwd(
        dh3, xh3, rs3, g02, gg3, uu3, *w2, alpha, "ffn_bwd_0b",
        carry=[to_row_owners(dwq), to_col_owners(dw3g), to_col_owners(dw3u), to_row_owners(dw3dt.T),
               to_row_owners(dkv.T)])
    dw2g, dw2u = _wgrad(hb2, [dg3, du3], "wgrad_up_0b")
    (dw2dt,) = _wgrad(do3, [a3], "wgrad_down_0b")

    dh1, dmix2, dpp, dcw, dgain[0][1], dbias[0][1] = _conv_bwd(dh2, xh2, rs2, g01, pp, cw_f, w_in, w_out, alpha, "conv_bwd")
    (dwin,) = _wgrad(hb1, [dpp], "wgrad_conv_in")
    (dwout,) = _wgrad(mb, [dmix2], "wgrad_conv_out")

    dh0, do1, dg1, du1, a1, dgain[0][0], dbias[0][0], l2g, l2u, l2d, lcin, lcout = _ffn_bwd(
        dh1, xh1, rs1, g00, gg1, uu1, *w1, alpha, "ffn_bwd_0a",
        carry=[to_col_owners(dw2g), to_col_owners(dw2u), to_row_owners(dw2dt.T), to_col_owners(dwin),
               to_row_owners(dwout)])
    (dw1dt,) = _wgrad(do1, [a1], "wgrad_down_0a")
    dw1g, l1d = _wgrad(hb0, [dg1], "wgrad_upg_0a", carry=[to_row_owners(dw1dt.T)])
    dw1u, l1g = _wgrad(hb0, [du1], "wgrad_upu_0a", carry=[to_col_owners(dw1g)])
    dmeta = dh0[PAD:ROW0]
    dgain_f = jnp.stack([jnp.concatenate(r, axis=0) for r in dgain])
    dbias_f = jnp.stack([jnp.concatenate(r, axis=0) for r in dbias])
    small_full = [dmeta, dgain_f, dbias_f, dcw[None], dfb]
    small_full_shapes = [a.shape for a in small_full]
    l1u, gsmall_grads = _exchange([to_col_owners(dw1u), (True, _pack_rows(small_full, WIDTH, F32_ROWS))], "rs_last")

    grad_x = dh0[ROW0:].reshape(1, seq, d)
    rsmall = _sum_sources(gsmall_grads, "small_sum")
    smeta, sgain, sbias, scw, sfb = _unpack_rows(rsmall, small_full_shapes, WIDTH, F32_ROWS)
    csh = d // N_DEV

    def my_cols(a):
        return lax.dynamic_slice_in_dim(a, me * csh, csh, axis=a.ndim - 1)

    grads = {"meta": my_cols(smeta), "ln_gain": my_cols(sgain), "ln_bias": my_cols(sbias),
             "conv_w": my_cols(scw), "f_bias": sfb[0, :N_HEADS], "kv_w": _sum_sources(lkv, "kv_sum").T}
    landed = {"ffn1_wg": [l1g, l3g], "ffn1_wu": [l1u, l3u], "ffn1_wd": [l1d, l3d],
              "ffn2_wg": [l2g, l4g], "ffn2_wu": [l2u, l4u], "ffn2_wd": [l2d, l4d],
              "conv_w_in": [lcin], "conv_w_out": [lcout], "attn_w_q": [lwq], "attn_w_o": [lwo]}
    weights = dict(meta=meta, ffn1_wg=ffn1_wg, ffn1_wu=ffn1_wu, ffn1_wd=ffn1_wd, ffn2_wg=ffn2_wg,
                   ffn2_wu=ffn2_wu, ffn2_wd=ffn2_wd, ln_gain=ln_gain, ln_bias=ln_bias,
                   conv_w_in=conv_w_in, conv_w=conv_w, conv_w_out=conv_w_out, kv_w=kv_w,
                   f_bias=f_bias, attn_w_q=attn_w_q, attn_w_o=attn_w_o)
    moms = dict(meta=(m_meta, v_meta), ffn1_wg=(m_ffn1_wg, v_ffn1_wg), ffn1_wu=(m_ffn1_wu, v_ffn1_wu),
                ffn1_wd=(m_ffn1_wd, v_ffn1_wd), ffn2_wg=(m_ffn2_wg, v_ffn2_wg), ffn2_wu=(m_ffn2_wu, v_ffn2_wu),
                ffn2_wd=(m_ffn2_wd, v_ffn2_wd), ln_gain=(m_ln_gain, v_ln_gain), ln_bias=(m_ln_bias, v_ln_bias),
                conv_w_in=(m_conv_w_in, v_conv_w_in), conv_w=(m_conv_w, v_conv_w),
                conv_w_out=(m_conv_w_out, v_conv_w_out), kv_w=(m_kv_w, v_kv_w), f_bias=(m_f_bias, v_f_bias),
                attn_w_q=(m_attn_w_q, v_attn_w_q), attn_w_o=(m_attn_w_o, v_attn_w_o))

    names = list(weights)
    g_out, d_out, m_out, v_out = [], [], [], []
    for n in names:
        w = weights[n]
        shp = w.shape
        mm, vv_ = moms[n]
        if n in landed:
            three = (len(landed[n]),) + shp[-2:]
            g, dl, nm, nv = _reduce_adamw(w.reshape(three), mm.reshape(three), vv_.reshape(three),
                                          landed[n], "adamw_" + n)
            g = g.reshape(shp)
        else:
            two = (1, shp[0]) if w.ndim == 1 else (math.prod(shp[:-1]), shp[-1])
            g = grads[n].reshape(shp)
            dl, nm, nv = _adamw(w.reshape(two), g.reshape(two), mm.reshape(two), vv_.reshape(two), "adamw_" + n)
        g_out.append(g)
        d_out.append(dl.reshape(shp))
        m_out.append(nm.reshape(shp))
        v_out.append(nv.reshape(shp))
    return (loss, grad_x, *g_out, *d_out, *m_out, *v_out)
```

```python
import functools
import math

import jax
import jax.numpy as jnp
from jax import lax
from jax.experimental import pallas as pl
from jax.experimental.pallas import tpu as pltpu

F32 = jnp.float32
BF16 = jnp.bfloat16

N_DEV = 8
N_HEADS = 8
N_META = 16
PAD = 112
ROW0 = PAD + N_META
LN_EPS = 1e-5
NEG_INF = -1e30
LOG2E = 1.4426950408889634
ATTN_HEADS_PER_STEP = 8
ATTN_BWD_HEADS_PER_STEP = 2
LANES = 128
MXU_COLS = 256
FFN_FWD_CHUNKS = 11
FFN_BWD_CHUNKS = 4

ADAM_LR = 0.001
ADAM_B1 = 0.9
ADAM_B2 = 0.999
ADAM_EPS = 1e-08
ADAM_WD = 0.01
ADAM_STEP = 10

ROW_TILES = (640, 128)
LOSS_TILE = 128
BF16_ROWS = 16
F32_ROWS = 8
SUM_ROWS_MAX = 768
ADAM_ROWS_MAX = 256
VMEM_BIG = 56 << 20
VMEM_MID = 40 << 20

ANY = pl.BlockSpec(memory_space=pl.ANY)
MESH = pl.DeviceIdType.MESH


def _row_tile(t):
    for c in ROW_TILES:
        if t % c == 0:
            return c
    raise ValueError(f"no row tile for {t}")


def _dot(a, b):
    return jnp.dot(a, b, preferred_element_type=F32)


def _dot_nt(a, b):
    return lax.dot_general(a, b, (((1,), (1,)), ((), ())), preferred_element_type=F32)


def _dot_tn(a, b):
    return lax.dot_general(a, b, (((0,), (0,)), ((), ())), preferred_element_type=F32)


def _params(sem, vmem):
    return pltpu.CompilerParams(dimension_semantics=sem, vmem_limit_bytes=vmem)


def _ln_fwd(z):
    mu = jnp.mean(z, axis=-1, keepdims=True)
    zc = z - mu
    var = jnp.mean(zc * zc, axis=-1, keepdims=True)
    rstd = lax.rsqrt(var + LN_EPS)
    return zc * rstd, rstd


def _ln_bwd(dh, xhat, rstd, gain):
    dxh = dh * gain
    m1 = jnp.mean(dxh, axis=-1, keepdims=True)
    m2 = jnp.mean(dxh * xhat, axis=-1, keepdims=True)
    dz = rstd * (dxh - m1 - xhat * m2)
    return dz, jnp.sum(dh * xhat, axis=0, keepdims=True), jnp.sum(dh, axis=0, keepdims=True)


def _load_resident(pairs, sems):
    cps = [pltpu.make_async_copy(src, dst, sems.at[k]) for k, (src, dst) in enumerate(pairs)]
    for cp in cps:
        cp.start()
    for cp in cps:
        cp.wait()


def _peer_ids():
    mx, my, mc = lax.axis_index("x"), lax.axis_index("y"), lax.axis_index("c")
    peers = []
    for kk in range(1, N_DEV):
        px = 1 - mx if (kk >> 2) & 1 else mx
        py = 1 - my if (kk >> 1) & 1 else my
        pc = 1 - mc if kk & 1 else mc
        peers.append(((px, py, pc), 4 * px + 2 * py + pc))
    return 4 * mx + 2 * my + mc, peers


def _exchange_copies(jobs, send_sems, recv_sems, local_sems, starting):
    me_id, peers = _peer_ids()
    for n, (gather, src, dst) in enumerate(jobs):
        own = pltpu.make_async_copy(src if gather else src.at[me_id], dst.at[me_id], local_sems.at[n])
        own.start() if starting else own.wait()
        for k, (dev, pid) in enumerate(peers):
            sem = (N_DEV - 1) * n + k
            out = src if gather else src.at[pid]
            send = pltpu.make_async_remote_copy(
                src_ref=out, dst_ref=dst.at[me_id], send_sem=send_sems.at[sem], recv_sem=recv_sems.at[sem],
                device_id=dev, device_id_type=MESH)
            if starting:
                send.start()
            else:
                pltpu.make_async_remote_copy(
                    src_ref=out, dst_ref=dst.at[pid], send_sem=send_sems.at[sem], recv_sem=recv_sems.at[sem],
                    device_id=dev, device_id_type=MESH).wait_recv()
                send.wait_send()


def _carried(body, n_in, n_out, carry, first, last):
    nj = len(carry)
    if nj == 0:
        return body

    def wrapped(*refs):
        ins, srcs = refs[:n_in], refs[n_in:n_in + nj]
        outs = refs[n_in + nj:n_in + nj + n_out]
        dsts = refs[n_in + nj + n_out:n_in + 2 * nj + n_out]
        scratch, sems = refs[n_in + 2 * nj + n_out:-3], refs[-3:]
        jobs = [(g, s, r) for (g, _), s, r in zip(carry, srcs, dsts)]

        @pl.when(first())
        def _():
            _exchange_copies(jobs, *sems, starting=True)

        body(*ins, *outs, *scratch)

        @pl.when(last())
        def _():
            _exchange_copies(jobs, *sems, starting=False)

    return wrapped


def _carry_shapes(carry):
    return [jax.ShapeDtypeStruct((N_DEV,) + a.shape if g else a.shape, a.dtype) for g, a in carry]


def _carry_scratch(carry):
    if not carry:
        return []
    n = len(carry)
    return [pltpu.SemaphoreType.DMA(((N_DEV - 1) * n,)), pltpu.SemaphoreType.DMA(((N_DEV - 1) * n,)),
            pltpu.SemaphoreType.DMA((n,))]


def _exchange(carry, name):
    n = len(carry)

    def body(*refs):
        jobs = [(g, s, r) for (g, _), s, r in zip(carry, refs[:n], refs[n:2 * n])]
        _exchange_copies(jobs, *refs[2 * n:], starting=True)
        _exchange_copies(jobs, *refs[2 * n:], starting=False)

    return pl.pallas_call(
        body, name=name, in_specs=[ANY] * n, out_specs=[ANY] * n, out_shape=_carry_shapes(carry),
        scratch_shapes=_carry_scratch(carry),
    )(*[a for _, a in carry])


def _ffn_fwd(xh, gi, bi, wg, wu, wd, go, bo, alpha, name, carry=(), input_t=False):
    t, d = xh.shape
    nch, _, fc = wg.shape
    f = nch * fc
    per = min(FFN_FWD_CHUNKS, nch)
    nc = -(-nch // per)
    tm = _row_tile(t)
    nt = t // tm

    def body(xh_ref, gi_ref, bi_ref, wg_hbm, wu_hbm, wd_hbm, go_ref, bo_ref,
             xo_ref, rs_ref, hb_ref, g_ref, u_ref, *tail):
        hin_ref = tail[0] if input_t else None
        wg_v, wu_v, wd_v, acc, hbs, sems = tail[1:] if input_t else tail
        i = pl.program_id(0)
        c = pl.program_id(1)

        @pl.when((i == 0) & (c == 0))
        def _():
            _load_resident([(wg_hbm, wg_v), (wu_hbm, wu_v), (wd_hbm, wd_v)], sems)

        @pl.when(c == 0)
        def _():
            h = xh_ref[...] * gi_ref[...] + bi_ref[...]
            hbs[...] = h.astype(BF16)
            acc[...] = jnp.zeros_like(acc)
            if input_t:
                hin_ref[...] = hbs[...].T

        def chunk(k):
            ck = c * per + k
            cols = slice(k * fc, (k + 1) * fc)
            hb = hbs[...]
            g = _dot(hb, wg_v[ck])
            u = _dot(hb, wu_v[ck])
            a = (g * jax.nn.sigmoid(g)) * u
            g_ref[:, cols] = g.astype(BF16)
            u_ref[:, cols] = u.astype(BF16)
            acc[...] += _dot(a.astype(BF16), wd_v[ck])

        for k in range(per):
            if (nc - 1) * per + k < nch:
                chunk(k)
            else:
                pl.when(c * per + k < nch)(functools.partial(chunk, k))

        @pl.when(c == nc - 1)
        def _():
            h = xh_ref[...] * gi_ref[...] + bi_ref[...]
            xhat, rstd = _ln_fwd(alpha * h + 0.5 * acc[...])
            xo_ref[...] = xhat
            rs_ref[...] = rstd
            hb_ref[...] = (xhat * go_ref[...] + bo_ref[...]).astype(BF16).T

    row = pl.BlockSpec((tm, d), lambda i, c: (i, 0))
    vec = pl.BlockSpec((1, d), lambda i, c: (0, 0))
    chunk = pl.BlockSpec((tm, per * fc), lambda i, c: (i, c))
    first = lambda: (pl.program_id(0) == 0) & (pl.program_id(1) == 0)
    last = lambda: (pl.program_id(0) == nt - 1) & (pl.program_id(1) == nc - 1)
    col = pl.BlockSpec((d, tm), lambda i, c: (0, i))
    t_spec, t_shape = ([col], [jax.ShapeDtypeStruct((d, t), BF16)]) if input_t else ([], [])
    return pl.pallas_call(
        _carried(body, 8, 5 + len(t_spec), carry, first, last), name=name, grid=(nt, nc),
        in_specs=[row, vec, vec, ANY, ANY, ANY, vec, vec] + [ANY] * len(carry),
        out_specs=[row, pl.BlockSpec((tm, 1), lambda i, c: (i, 0)), col, chunk, chunk] + t_spec
                  + [ANY] * len(carry),
        out_shape=[jax.ShapeDtypeStruct((t, d), F32), jax.ShapeDtypeStruct((t, 1), F32),
                   jax.ShapeDtypeStruct((d, t), BF16), jax.ShapeDtypeStruct((t, f), BF16),
                   jax.ShapeDtypeStruct((t, f), BF16)] + t_shape + _carry_shapes(carry),
        scratch_shapes=[pltpu.VMEM((nch, d, fc), BF16), pltpu.VMEM((nch, d, fc), BF16),
                        pltpu.VMEM((nch, fc, d), BF16), pltpu.VMEM((tm, d), F32),
                        pltpu.VMEM((tm, d), BF16), pltpu.SemaphoreType.DMA((3,))] + _carry_scratch(carry),
        compiler_params=_params(("arbitrary", "arbitrary"), VMEM_BIG),
    )(xh, gi, bi, wg, wu, wd, go, bo, *[a for _, a in carry])


def _ffn_bwd(dh, xo, rs, go, gs, us, wg, wu, wd, alpha, name, carry=(), loss_target=None, loss_bias=None):
    t, d = xo.shape
    nch, _, fc = wg.shape
    f = nch * fc
    per = min(FFN_BWD_CHUNKS, nch)
    nc = -(-nch // per)
    tm = _row_tile(t)
    nt = t // tm

    with_loss = loss_target is not None
    nsub, lead = tm // LOSS_TILE, ROW0 // LOSS_TILE
    nlead = nsub + 1 if with_loss else 1

    def body(*refs):
        lead_refs = refs[:nlead]
        xo_ref, rs_ref, go_ref, g_ref, u_ref, wg_hbm, wu_hbm, wd_hbm = refs[nlead:nlead + 8]
        dhin_ref, dot_ref, dg_ref, du_ref, a_ref, dgain_ref, dbias_ref = refs[nlead + 8:nlead + 15]
        rest = refs[nlead + 15:]
        loss_ref, rest = (rest[0], rest[1:]) if with_loss else (None, rest)
        wg_v, wu_v, wd_v, do_ref, sems = rest[:5]
        i = pl.program_id(0)
        c = pl.program_id(1)

        @pl.when((i == 0) & (c == 0))
        def _():
            _load_resident([(wg_hbm, wg_v), (wu_hbm, wu_v), (wd_hbm, wd_v)], sems)
            dgain_ref[...] = jnp.zeros_like(dgain_ref)
            dbias_ref[...] = jnp.zeros_like(dbias_ref)
            if with_loss:
                rest[5][...] = jnp.zeros_like(rest[5])

        def tile_dh():
            if not with_loss:
                return lead_refs[0][...]
            part = rest[5]
            for k in range(nsub):
                sl = slice(k * LOSS_TILE, (k + 1) * LOSS_TILE)
                rows = i * tm + k * LOSS_TILE + lax.broadcasted_iota(jnp.int32, (LOSS_TILE, 1), 0)
                y = xo_ref[sl, :] * go_ref[...] + lead_refs[nsub][...]
                e = jnp.where(rows >= ROW0, y - lead_refs[k][...], 0.0)
                part[...] += jnp.sum(e * e, axis=0, keepdims=True)
                dhin_ref[sl, :] = e * (1.0 / d)

            @pl.when(i == nt - 1)
            def _():
                loss_ref[...] = jnp.full((1, LANES), 0.5 / d, F32) * jnp.sum(part[...])

            return dhin_ref[...]

        @pl.when(c == 0)
        def _():
            dz, dgp, dbp = _ln_bwd(tile_dh(), xo_ref[...], rs_ref[...], go_ref[...])
            dgain_ref[...] += dgp
            dbias_ref[...] += dbp
            dob = (0.5 * dz).astype(BF16)
            do_ref[...] = dob
            dot_ref[...] = dob.T
            dhin_ref[...] = alpha * dz

        def chunk(k):
            ck = c * per + k
            cols = slice(k * fc, (k + 1) * fc)
            g = g_ref[:, cols].astype(F32)
            u = u_ref[:, cols].astype(F32)
            sg = jax.nn.sigmoid(g)
            sl = g * sg
            da = _dot_nt(do_ref[...], wd_v[ck])
            dgb = (da * u * (sg * (1.0 + g * (1.0 - sg)))).astype(BF16)
            dub = (da * sl).astype(BF16)
            a_ref[:, cols] = (sl * u).astype(BF16)
            dg_ref[:, cols] = dgb
            du_ref[:, cols] = dub
            dhin_ref[...] += _dot_nt(dgb, wg_v[ck]) + _dot_nt(dub, wu_v[ck])

        for k in range(per):
            if (nc - 1) * per + k < nch:
                chunk(k)
            else:
                pl.when(c * per + k < nch)(functools.partial(chunk, k))

    row = pl.BlockSpec((tm, d), lambda i, c: (i, 0))
    vec = pl.BlockSpec((1, d), lambda i, c: (0, 0))
    chunk = pl.BlockSpec((tm, per * fc), lambda i, c: (i, c))
    first = lambda: (pl.program_id(0) == 0) & (pl.program_id(1) == 0)
    last = lambda: (pl.program_id(0) == nt - 1) & (pl.program_id(1) == nc - 1)
    if with_loss:
        lead_specs = [pl.BlockSpec((LOSS_TILE, d), lambda i, c, k=k: (jnp.maximum(i * nsub + k - lead, 0), 0))
                      for k in range(nsub)] + [vec]
        lead_args = [loss_target] * nsub + [loss_bias]
        loss_spec, loss_shape = [pl.BlockSpec((1, LANES), lambda i, c: (0, 0))], [jax.ShapeDtypeStruct((1, LANES), F32)]
        loss_scratch = [pltpu.VMEM((1, d), F32)]
    else:
        lead_specs, lead_args, loss_spec, loss_shape, loss_scratch = [row], [dh], [], [], []
    return pl.pallas_call(
        _carried(body, nlead + 8, 7 + len(loss_spec), carry, first, last), name=name, grid=(nt, nc),
        in_specs=lead_specs + [row, pl.BlockSpec((tm, 1), lambda i, c: (i, 0)), vec, chunk, chunk,
                               ANY, ANY, ANY] + [ANY] * len(carry),
        out_specs=[row, pl.BlockSpec((d, tm), lambda i, c: (0, i)), chunk, chunk, chunk, vec, vec]
                  + loss_spec + [ANY] * len(carry),
        out_shape=[jax.ShapeDtypeStruct((t, d), F32), jax.ShapeDtypeStruct((d, t), BF16),
                   jax.ShapeDtypeStruct((t, f), BF16), jax.ShapeDtypeStruct((t, f), BF16),
                   jax.ShapeDtypeStruct((t, f), BF16), jax.ShapeDtypeStruct((1, d), F32),
                   jax.ShapeDtypeStruct((1, d), F32)] + loss_shape + _carry_shapes(carry),
        scratch_shapes=[pltpu.VMEM((nch, d, fc), BF16), pltpu.VMEM((nch, d, fc), BF16),
                        pltpu.VMEM((nch, fc, d), BF16), pltpu.VMEM((tm, d), BF16),
                        pltpu.SemaphoreType.DMA((3,))] + loss_scratch + _carry_scratch(carry),
        compiler_params=_params(("arbitrary", "arbitrary"), VMEM_BIG),
    )(*lead_args, xo, rs, go, gs, us, wg, wu, wd, *[a for _, a in carry])


def _wgrad(xt, ys, name, carry=()):
    m, t = xt.shape
    n = ys[0].shape[1]
    tn = min(n, MXU_COLS)
    ny = len(ys)

    def body(*refs):
        x_hbm = refs[0]
        y_refs = refs[1:1 + ny]
        o_refs = refs[1 + ny:1 + 2 * ny]
        xv, sems = refs[1 + 2 * ny:]

        @pl.when(pl.program_id(0) == 0)
        def _():
            _load_resident([(x_hbm, xv)], sems)

        for y_ref, o_ref in zip(y_refs, o_refs):
            o_ref[...] = _dot(xv[...], y_ref[...].astype(BF16)).astype(BF16)

    steps = n // tn
    first = lambda: pl.program_id(0) == 0
    last = lambda: pl.program_id(0) == steps - 1
    return pl.pallas_call(
        _carried(body, 1 + ny, ny, carry, first, last), name=name, grid=(steps,),
        in_specs=[ANY] + [pl.BlockSpec((t, tn), lambda c: (0, c)) for _ in ys] + [ANY] * len(carry),
        out_specs=[pl.BlockSpec((m, tn), lambda c: (0, c)) for _ in ys] + [ANY] * len(carry),
        out_shape=[jax.ShapeDtypeStruct((m, n), BF16) for _ in ys] + _carry_shapes(carry),
        scratch_shapes=[pltpu.VMEM((m, t), BF16), pltpu.SemaphoreType.DMA((1,))] + _carry_scratch(carry),
        compiler_params=_params(("arbitrary",), VMEM_BIG),
    )(xt, *ys, *[a for _, a in carry])


def _shift_rows(u, halo, tm):
    r = lax.broadcasted_iota(jnp.int32, (tm, 1), 0)
    u1 = jnp.where(r == 0, halo[7:8], pltpu.roll(u, 1, 0))
    u2 = jnp.where(r == 0, halo[6:7], jnp.where(r == 1, halo[7:8], pltpu.roll(u, 2, 0)))
    return u1, u2


def _conv_fwd(xh, gi, bi, w_in, cw, w_out, go, bo, alpha, name, carry=()):
    t, d = xh.shape
    tm = _row_tile(t)
    nt = t // tm

    def body(xh_ref, gi_ref, bi_ref, win_ref, cw_ref, wout_ref, go_ref, bo_ref,
             xo_ref, rs_ref, hb_ref, p_ref, m_ref, halo):
        i = pl.program_id(0)

        @pl.when(i == 0)
        def _():
            halo[...] = jnp.zeros_like(halo)

        h = xh_ref[...] * gi_ref[...] + bi_ref[...]
        hb = h.astype(BF16)
        bg = _dot(hb, win_ref[:, 0:d])
        cg = _dot(hb, win_ref[:, d:2 * d])
        val = _dot(hb, win_ref[:, 2 * d:3 * d])
        p_ref[:, 0:d] = bg.astype(BF16)
        p_ref[:, d:2 * d] = cg.astype(BF16)
        p_ref[:, 2 * d:3 * d] = val.astype(BF16)
        rows = i * tm + lax.broadcasted_iota(jnp.int32, (tm, 1), 0)
        u = jnp.where(rows >= PAD, cg * val, 0.0)
        u1, u2 = _shift_rows(u, halo[...], tm)
        halo[...] = u[tm - 8:tm]
        y = cw_ref[0:1] * u2 + cw_ref[1:2] * u1 + cw_ref[2:3] * u
        mb = (bg * y).astype(BF16)
        m_ref[...] = mb.T
        xhat, rstd = _ln_fwd(alpha * h + _dot(mb, wout_ref[...]))
        xo_ref[...] = xhat
        rs_ref[...] = rstd
        hb_ref[...] = (xhat * go_ref[...] + bo_ref[...]).astype(BF16).T

    row = pl.BlockSpec((tm, d), lambda i: (i, 0))
    col = pl.BlockSpec((d, tm), lambda i: (0, i))
    vec = pl.BlockSpec((1, d), lambda i: (0, 0))
    first = lambda: pl.program_id(0) == 0
    last = lambda: pl.program_id(0) == nt - 1
    return pl.pallas_call(
        _carried(body, 8, 5, carry, first, last), name=name, grid=(nt,),
        in_specs=[row, vec, vec, pl.BlockSpec((d, 3 * d), lambda i: (0, 0)),
                  pl.BlockSpec((3, d), lambda i: (0, 0)), pl.BlockSpec((d, d), lambda i: (0, 0)),
                  vec, vec] + [ANY] * len(carry),
        out_specs=[row, pl.BlockSpec((tm, 1), lambda i: (i, 0)), col,
                   pl.BlockSpec((tm, 3 * d), lambda i: (i, 0)), col] + [ANY] * len(carry),
        out_shape=[jax.ShapeDtypeStruct((t, d), F32), jax.ShapeDtypeStruct((t, 1), F32),
                   jax.ShapeDtypeStruct((d, t), BF16), jax.ShapeDtypeStruct((t, 3 * d), BF16),
                   jax.ShapeDtypeStruct((d, t), BF16)] + _carry_shapes(carry),
        scratch_shapes=[pltpu.VMEM((8, d), F32)] + _carry_scratch(carry),
        compiler_params=_params(("arbitrary",), VMEM_BIG),
    )(xh, gi, bi, w_in, cw, w_out, go, bo, *[a for _, a in carry])


def _conv_bwd(dh, xo, rs, go, p, cw, w_in, w_out, alpha, name):
    t, d = dh.shape
    tm = _row_tile(t)
    nt = t // tm
    tb = tm // 8

    def body(dh_ref, xo_ref, rs_ref, go_ref, p_ref, ph_ref, cw_ref, win_ref, wout_ref,
             dhin_ref, dmix_ref, dp_ref, dcw_ref, dgain_ref, dbias_ref, carry):
        i = pl.program_id(0)
        tile = nt - 1 - i

        @pl.when(i == 0)
        def _():
            carry[...] = jnp.zeros_like(carry)
            dcw_ref[...] = jnp.zeros_like(dcw_ref)
            dgain_ref[...] = jnp.zeros_like(dgain_ref)
            dbias_ref[...] = jnp.zeros_like(dbias_ref)

        dz, dgp, dbp = _ln_bwd(dh_ref[...], xo_ref[...], rs_ref[...], go_ref[...])
        dgain_ref[...] += dgp
        dbias_ref[...] += dbp
        dmixb = dz.astype(BF16)
        dmix_ref[...] = dmixb
        dm = _dot_nt(dmixb, wout_ref[...])

        bg = p_ref[:, 0:d].astype(F32)
        cg = p_ref[:, d:2 * d].astype(F32)
        val = p_ref[:, 2 * d:3 * d].astype(F32)
        rows = tile * tm + lax.broadcasted_iota(jnp.int32, (tm, 1), 0)
        valid = rows >= PAD
        u = jnp.where(valid, cg * val, 0.0)
        hrows = tile * tm - 8 + lax.broadcasted_iota(jnp.int32, (8, 1), 0)
        hu = jnp.where((hrows >= PAD) & (tile > 0),
                       ph_ref[:, d:2 * d].astype(F32) * ph_ref[:, 2 * d:3 * d].astype(F32), 0.0)
        u1, u2 = _shift_rows(u, hu, tm)
        w0, w1, w2 = cw_ref[0:1], cw_ref[1:2], cw_ref[2:3]
        y = w0 * u2 + w1 * u1 + w2 * u
        dbg = dm * y
        dy = dm * bg
        dcw_ref[0:1] += jnp.sum(dy * u2, axis=0, keepdims=True)
        dcw_ref[1:2] += jnp.sum(dy * u1, axis=0, keepdims=True)
        dcw_ref[2:3] += jnp.sum(dy * u, axis=0, keepdims=True)

        nxt = carry[...]
        r = lax.broadcasted_iota(jnp.int32, (tm, 1), 0)
        dy1 = jnp.where(r == tm - 1, nxt[0:1], pltpu.roll(dy, tm - 1, 0))
        dy2 = jnp.where(r == tm - 2, nxt[0:1],
                        jnp.where(r == tm - 1, nxt[1:2], pltpu.roll(dy, tm - 2, 0)))
        carry[...] = dy[0:8]
        du = jnp.where(valid, w2 * dy + w1 * dy1 + w0 * dy2, 0.0)
        dbgb = dbg.astype(BF16)
        dcgb = (du * val).astype(BF16)
        dvalb = (du * cg).astype(BF16)
        dp_ref[:, 0:d] = dbgb
        dp_ref[:, d:2 * d] = dcgb
        dp_ref[:, 2 * d:3 * d] = dvalb
        dhin_ref[...] = (alpha * dz + _dot_nt(dbgb, win_ref[:, 0:d])
                         + _dot_nt(dcgb, win_ref[:, d:2 * d]) + _dot_nt(dvalb, win_ref[:, 2 * d:3 * d]))

    row = pl.BlockSpec((tm, d), lambda i: (nt - 1 - i, 0))
    vec = pl.BlockSpec((1, d), lambda i: (0, 0))
    prow = pl.BlockSpec((tm, 3 * d), lambda i: (nt - 1 - i, 0))
    return pl.pallas_call(
        body, name=name, grid=(nt,),
        in_specs=[row, row, pl.BlockSpec((tm, 1), lambda i: (nt - 1 - i, 0)), vec, prow,
                  pl.BlockSpec((8, 3 * d), lambda i: (jnp.maximum((nt - 1 - i) * tb - 1, 0), 0)),
                  pl.BlockSpec((3, d), lambda i: (0, 0)),
                  pl.BlockSpec((d, 3 * d), lambda i: (0, 0)), pl.BlockSpec((d, d), lambda i: (0, 0))],
        out_specs=[row, row, prow, pl.BlockSpec((3, d), lambda i: (0, 0)), vec, vec],
        out_shape=[jax.ShapeDtypeStruct((t, d), F32), jax.ShapeDtypeStruct((t, d), BF16),
                   jax.ShapeDtypeStruct((t, 3 * d), BF16), jax.ShapeDtypeStruct((3, d), F32),
                   jax.ShapeDtypeStruct((1, d), F32), jax.ShapeDtypeStruct((1, d), F32)],
        scratch_shapes=[pltpu.VMEM((8, d), F32)],
        compiler_params=_params(("arbitrary",), VMEM_BIG),
    )(dh, xo, rs, go, p, p, cw, w_in, w_out)


def _kv_fwd(xh, gi, bi, wk, wv, wf, fb, name, carry=()):
    t, d = xh.shape
    tm = _row_tile(t)
    nt = t // tm

    def body(xh_ref, gi_ref, bi_ref, wk_ref, wv_ref, wf_ref, fb_ref,
             k_ref, v_ref, lg_ref, c_ref, ct_ref, run):
        i = pl.program_id(0)

        @pl.when(i == 0)
        def _():
            run[...] = jnp.zeros_like(run)

        x = (xh_ref[...] * gi_ref[...] + bi_ref[...]).astype(BF16)
        k_ref[...] = _dot(x, wk_ref[...]).astype(BF16)
        v_ref[...] = _dot(x, wv_ref[...]).astype(BF16)
        logit = _dot(x, wf_ref[...]) + fb_ref[...]
        lg_ref[...] = logit
        logf = jnp.minimum(logit, 0.0) - jnp.log(1.0 + jnp.exp(-jnp.abs(logit)))
        rows = i * tm + lax.broadcasted_iota(jnp.int32, (tm, 1), 0)
        logf = jnp.where(rows >= PAD, logf, 0.0)
        tri = (lax.broadcasted_iota(jnp.int32, (tm, tm), 0)
               >= lax.broadcasted_iota(jnp.int32, (tm, tm), 1)).astype(F32)
        cs = jnp.dot(tri, logf, precision=lax.Precision.HIGHEST, preferred_element_type=F32) + run[...]
        run[...] = cs[tm - 1:tm]
        c_ref[...] = cs
        ct_ref[...] = cs.T

    row = pl.BlockSpec((tm, d), lambda i: (i, 0))
    vec = pl.BlockSpec((1, d), lambda i: (0, 0))
    gate = pl.BlockSpec((tm, LANES), lambda i: (i, 0))
    sq = pl.BlockSpec((d, d), lambda i: (0, 0))
    first = lambda: pl.program_id(0) == 0
    last = lambda: pl.program_id(0) == nt - 1
    return pl.pallas_call(
        _carried(body, 7, 5, carry, first, last), name=name, grid=(nt,),
        in_specs=[row, vec, vec, sq, sq, pl.BlockSpec((d, LANES), lambda i: (0, 0)),
                  pl.BlockSpec((1, LANES), lambda i: (0, 0))] + [ANY] * len(carry),
        out_specs=[row, row, gate, gate, pl.BlockSpec((LANES, tm), lambda i: (0, i))] + [ANY] * len(carry),
        out_shape=[jax.ShapeDtypeStruct((t, d), BF16), jax.ShapeDtypeStruct((t, d), BF16),
                   jax.ShapeDtypeStruct((t, LANES), F32), jax.ShapeDtypeStruct((t, LANES), F32),
                   jax.ShapeDtypeStruct((LANES, t), F32)] + _carry_shapes(carry),
        scratch_shapes=[pltpu.VMEM((1, LANES), F32)] + _carry_scratch(carry),
        compiler_params=_params(("arbitrary",), VMEM_MID),
    )(xh, gi, bi, wk, wv, wf, fb, *[a for _, a in carry])


def _kv_bwd(dk, dv, dcs, dcq, logit, dh_other, wk, wv, wf, name):
    t, d = dk.shape
    tm = _row_tile(t)
    nt = t // tm

    def body(dk_ref, dv_ref, dcs_ref, dcq_ref, lg_ref, oth_ref, wk_ref, wv_ref, wf_ref,
             dh_ref, dl_ref, dfb_ref, run):
        i = pl.program_id(0)
        tile = nt - 1 - i

        @pl.when(i == 0)
        def _():
            run[...] = jnp.zeros_like(run)
            dfb_ref[...] = jnp.zeros_like(dfb_ref)

        lane = lax.broadcasted_iota(jnp.int32, (tm, LANES), 1)
        dc = dcq_ref[...]
        for hh in range(N_HEADS):
            dc = dc + jnp.where(lane == hh, jnp.sum(dcs_ref[hh], axis=1, keepdims=True), 0.0)
        tri = (lax.broadcasted_iota(jnp.int32, (tm, tm), 0)
               <= lax.broadcasted_iota(jnp.int32, (tm, tm), 1)).astype(F32)
        dlf = jnp.dot(tri, dc, precision=lax.Precision.HIGHEST, preferred_element_type=F32) + run[...]
        run[...] = dlf[0:1]
        rows = tile * tm + lax.broadcasted_iota(jnp.int32, (tm, 1), 0)
        dlogit = jnp.where(rows >= PAD, dlf * jax.nn.sigmoid(-lg_ref[...]), 0.0)
        dfb_ref[...] += jnp.sum(dlogit, axis=0, keepdims=True)
        dlb = dlogit.astype(BF16)
        dl_ref[...] = dlb
        dh_ref[...] = (oth_ref[...] + _dot_nt(dk_ref[...], wk_ref[...])
                       + _dot_nt(dv_ref[...], wv_ref[...]) + _dot_nt(dlb, wf_ref[...]))

    row = pl.BlockSpec((tm, d), lambda i: (nt - 1 - i, 0))
    gate = pl.BlockSpec((tm, LANES), lambda i: (nt - 1 - i, 0))
    sq = pl.BlockSpec((d, d), lambda i: (0, 0))
    return pl.pallas_call(
        body, name=name, grid=(nt,),
        in_specs=[row, row, pl.BlockSpec((N_HEADS, tm, LANES), lambda i: (0, nt - 1 - i, 0)), gate, gate, row,
                  sq, sq, pl.BlockSpec((d, LANES), lambda i: (0, 0))],
        out_specs=[row, gate, pl.BlockSpec((1, LANES), lambda i: (0, 0))],
        out_shape=[jax.ShapeDtypeStruct((t, d), F32), jax.ShapeDtypeStruct((t, LANES), BF16),
                   jax.ShapeDtypeStruct((1, LANES), F32)],
        scratch_shapes=[pltpu.VMEM((1, LANES), F32)],
        compiler_params=_params(("arbitrary",), VMEM_MID),
    )(dk, dv, dcs, dcq, logit, dh_other, wk, wv, wf)


def _proj(xh, gi, bi, w, name):
    t, k = xh.shape
    n = w.shape[1]
    tm = _row_tile(t)

    def body(x_ref, g_ref, b_ref, w_ref, o_ref):
        x = (x_ref[...] * g_ref[...] + b_ref[...]).astype(BF16)
        o_ref[...] = _dot(x, w_ref[...]).astype(BF16)

    vec = pl.BlockSpec((1, k), lambda i: (0, 0))
    return pl.pallas_call(
        body, name=name, grid=(t // tm,),
        in_specs=[pl.BlockSpec((tm, k), lambda i: (i, 0)), vec, vec, pl.BlockSpec((k, n), lambda i: (0, 0))],
        out_specs=pl.BlockSpec((tm, n), lambda i: (i, 0)),
        out_shape=jax.ShapeDtypeStruct((t, n), BF16),
        compiler_params=_params(("arbitrary",), VMEM_MID),
    )(xh, gi, bi, w)


def _add_proj_nt(base, y, w, name):
    t, n = y.shape
    k = w.shape[0]
    tm = _row_tile(t)

    def body(b_ref, y_ref, w_ref, o_ref):
        o_ref[...] = b_ref[...] + _dot_nt(y_ref[...].astype(BF16), w_ref[...])

    return pl.pallas_call(
        body, name=name, grid=(t // tm,),
        in_specs=[pl.BlockSpec((tm, k), lambda i: (i, 0)), pl.BlockSpec((tm, n), lambda i: (i, 0)),
                  pl.BlockSpec((k, n), lambda i: (0, 0))],
        out_specs=pl.BlockSpec((tm, k), lambda i: (i, 0)),
        out_shape=jax.ShapeDtypeStruct((t, k), F32),
        compiler_params=_params(("arbitrary",), VMEM_MID),
    )(base, y, w)


def _attn_out_fwd(ot, xh, gi, bi, w_o, go, bo, alpha, name):
    t, d = xh.shape
    tm = _row_tile(t)

    def body(ot_ref, xh_ref, gi_ref, bi_ref, wo_ref, go_ref, bo_ref, xo_ref, rs_ref, hb_ref):
        h = xh_ref[...] * gi_ref[...] + bi_ref[...]
        xhat, rstd = _ln_fwd(alpha * h + _dot_tn(ot_ref[...], wo_ref[...]))
        xo_ref[...] = xhat
        rs_ref[...] = rstd
        hb_ref[...] = (xhat * go_ref[...] + bo_ref[...]).astype(BF16).T

    row = pl.BlockSpec((tm, d), lambda i: (i, 0))
    col = pl.BlockSpec((d, tm), lambda i: (0, i))
    vec = pl.BlockSpec((1, d), lambda i: (0, 0))
    return pl.pallas_call(
        body, name=name, grid=(t // tm,),
        in_specs=[col, row, vec, vec, pl.BlockSpec((d, d), lambda i: (0, 0)), vec, vec],
        out_specs=[row, pl.BlockSpec((tm, 1), lambda i: (i, 0)), col],
        out_shape=[jax.ShapeDtypeStruct((t, d), F32), jax.ShapeDtypeStruct((t, 1), F32),
                   jax.ShapeDtypeStruct((d, t), BF16)],
        compiler_params=_params(("arbitrary",), VMEM_MID),
    )(ot, xh, gi, bi, w_o, go, bo)


def _attn_out_bwd(dh, xo, rs, go, ot, w_o, alpha, name):
    t, d = dh.shape
    tm = _row_tile(t)
    hd = d // N_HEADS

    def body(dh_ref, xo_ref, rs_ref, go_ref, ot_ref, wo_ref,
             dres_ref, dmix_ref, dot_ref, delta_ref, dgain_ref, dbias_ref):
        @pl.when(pl.program_id(0) == 0)
        def _():
            dgain_ref[...] = jnp.zeros_like(dgain_ref)
            dbias_ref[...] = jnp.zeros_like(dbias_ref)

        dz, dgp, dbp = _ln_bwd(dh_ref[...], xo_ref[...], rs_ref[...], go_ref[...])
        dgain_ref[...] += dgp
        dbias_ref[...] += dbp
        dres_ref[...] = alpha * dz
        dmixb = dz.astype(BF16)
        dmix_ref[...] = dmixb
        dot_t = _dot_nt(wo_ref[...], dmixb)
        dot_ref[...] = dot_t.astype(BF16)
        prod = dot_t * ot_ref[...].astype(F32)
        delta_ref[...] = jnp.sum(prod.reshape(N_HEADS, hd, tm), axis=1)

    row = pl.BlockSpec((tm, d), lambda i: (i, 0))
    vec = pl.BlockSpec((1, d), lambda i: (0, 0))
    col = pl.BlockSpec((d, tm), lambda i: (0, i))
    return pl.pallas_call(
        body, name=name, grid=(t // tm,),
        in_specs=[row, row, pl.BlockSpec((tm, 1), lambda i: (i, 0)), vec, col,
                  pl.BlockSpec((d, d), lambda i: (0, 0))],
        out_specs=[row, row, col, pl.BlockSpec((N_HEADS, tm), lambda i: (0, i)), vec, vec],
        out_shape=[jax.ShapeDtypeStruct((t, d), F32), jax.ShapeDtypeStruct((t, d), BF16),
                   jax.ShapeDtypeStruct((d, t), BF16), jax.ShapeDtypeStruct((N_HEADS, t), F32),
                   jax.ShapeDtypeStruct((1, d), F32), jax.ShapeDtypeStruct((1, d), F32)],
        compiler_params=_params(("arbitrary",), VMEM_MID),
    )(dh, xo, rs, go, ot, w_o)


def _scores_t(k, q, ct_ref, c_ref, h, i, j, tq, tk, scale, masked):
    sub = lax.broadcasted_iota(jnp.int32, (8, tq), 0)
    cq = jnp.sum(jnp.where(sub == h, ct_ref[...], 0.0), axis=0, keepdims=True) * LOG2E
    lane = lax.broadcasted_iota(jnp.int32, (tk, LANES), 1)
    ck = jnp.sum(jnp.where(lane == h, c_ref[...], 0.0), axis=1, keepdims=True) * LOG2E
    st = _dot_nt(k, q) * (scale * LOG2E) - ck
    if masked:
        kpos = j * tk + lax.broadcasted_iota(jnp.int32, (tk, 1), 0)
        qpos = i * tq + lax.broadcasted_iota(jnp.int32, (1, tq), 1)
        st = jnp.where((kpos <= qpos) & (kpos >= PAD), st, NEG_INF)
    return st, cq


def _tri_pairs(n, by_row):
    if by_row:
        pairs = [(i, j) for i in range(n) for j in range(i + 1)]
    else:
        pairs = [(i, j) for j in range(n) for i in range(j, n)]
    return (jnp.asarray([p[0] for p in pairs], jnp.int32), jnp.asarray([p[1] for p in pairs], jnp.int32))


def _attn_fwd(q, k, v, c, ct, name, carry=()):
    t, d = q.shape
    hd = d // N_HEADS
    tq = tk = _row_tile(t)
    nq = t // tq
    scale = 1.0 / math.sqrt(hd)

    hps = ATTN_HEADS_PER_STEP

    def body(it_ref, jt_ref, q_ref, k_ref, v_ref, c_ref, ct_ref, ot_ref, lse_ref, m_s, l_s, acc):
        hp, p_ = pl.program_id(0), pl.program_id(1)
        i, j = it_ref[p_], jt_ref[p_]

        @pl.when(j == 0)
        def _():
            m_s[...] = jnp.full_like(m_s, NEG_INF)
            l_s[...] = jnp.zeros_like(l_s)
            acc[...] = jnp.zeros_like(acc)

        def update(masked):
            scores = []
            for e in range(hps):
                cols = slice(e * hd, (e + 1) * hd)
                scores.append(_scores_t(k_ref[:, cols], q_ref[:, cols], ct_ref, c_ref, hp * hps + e,
                                        i, j, tq, tk, scale, masked))
            probs = []
            for e, (st, cq) in enumerate(scores):
                m_new = jnp.maximum(m_s[e], jnp.max(st, axis=0, keepdims=True) + cq)
                a = jnp.exp2(m_s[e] - m_new)
                p = jnp.exp2(st - (m_new - cq))
                l_s[e] = a * l_s[e] + jnp.sum(p, axis=0, keepdims=True)
                m_s[e] = m_new
                probs.append((a, p.astype(BF16)))
            for e, (a, pb) in enumerate(probs):
                acc[e] = a * acc[e] + _dot_tn(v_ref[:, e * hd:(e + 1) * hd], pb)

        edge = (j == i) | (j == 0)
        pl.when(edge)(lambda: update(True))
        pl.when(jnp.logical_not(edge))(lambda: update(False))

        @pl.when(j == i)
        def _():
            for e in range(hps):
                ot_ref[e * hd:(e + 1) * hd, :] = (acc[e] / l_s[e]).astype(BF16)
                lse_ref[e] = m_s[e] + jnp.log2(l_s[e])

    it, jt = _tri_pairs(nq, by_row=True)
    npairs = it.shape[0]
    nhp = N_HEADS // hps
    kv = pl.BlockSpec((tk, hps * hd), lambda h, p, it, jt: (jt[p], h))
    first = lambda: (pl.program_id(0) == 0) & (pl.program_id(1) == 0)
    last = lambda: (pl.program_id(0) == nhp - 1) & (pl.program_id(1) == npairs - 1)
    return pl.pallas_call(
        _carried(body, 7, 2, carry, first, last), name=name,
        grid_spec=pltpu.PrefetchScalarGridSpec(
            num_scalar_prefetch=2, grid=(nhp, npairs),
            in_specs=[pl.BlockSpec((tq, hps * hd), lambda h, p, it, jt: (it[p], h)), kv, kv,
                      pl.BlockSpec((tk, LANES), lambda h, p, it, jt: (jt[p], 0)),
                      pl.BlockSpec((8, tq), lambda h, p, it, jt: (0, it[p]))] + [ANY] * len(carry),
            out_specs=[pl.BlockSpec((hps * hd, tq), lambda h, p, it, jt: (h, it[p])),
                       pl.BlockSpec((hps, 1, tq), lambda h, p, it, jt: (h, 0, it[p]))] + [ANY] * len(carry),
            scratch_shapes=[pltpu.VMEM((hps, 1, tq), F32), pltpu.VMEM((hps, 1, tq), F32),
                            pltpu.VMEM((hps, hd, tq), F32)] + _carry_scratch(carry)),
        out_shape=[jax.ShapeDtypeStruct((d, t), BF16), jax.ShapeDtypeStruct((N_HEADS, 1, t), F32)]
                  + _carry_shapes(carry),
        compiler_params=_params(("arbitrary", "arbitrary"), VMEM_MID),
    )(it, jt, q, k, v, c, ct, *[a for _, a in carry])


def _attn_bwd(q, k, v, c, ct, lse, delta, dot_t, name, carry=()):
    t, d = q.shape
    hd = d // N_HEADS
    tq = tk = _row_tile(t)
    nq = t // tq
    scale = 1.0 / math.sqrt(hd)
    hps = ATTN_BWD_HEADS_PER_STEP

    steps = [(j, i, min(i + 1, nq - 1), int(i + 1 < nq)) for j in range(nq) for i in range(j, nq, 2)]
    jt, ia, ib, vb = (jnp.asarray([s[n] for s in steps], jnp.int32) for n in range(4))

    def body(jt_ref, ia_ref, ib_ref, vb_ref, qa_ref, k_ref, v_ref, c_ref, cta_ref, lsea_ref, deltaa_ref, dota_ref,
             qb_ref, ctb_ref, lseb_ref, deltab_ref, dotb_ref,
             dq_ref, dk_ref, dv_ref, dcs_ref, drow_ref, dk_acc, dv_acc, dc_acc):
        hp, p_ = pl.program_id(0), pl.program_id(1)
        j, i_a, i_b, has_b = jt_ref[p_], ia_ref[p_], ib_ref[p_], vb_ref[p_] == 1

        @pl.when(p_ == 0)
        def _():
            dq_ref[...] = jnp.zeros_like(dq_ref)
            drow_ref[...] = jnp.zeros_like(drow_ref)

        @pl.when(i_a == j)
        def _():
            dk_acc[...] = jnp.zeros_like(dk_acc)
            dv_acc[...] = jnp.zeros_like(dv_acc)
            dc_acc[...] = jnp.zeros_like(dc_acc)

        def update(q_ref, ct_ref, lse_ref, delta_ref, dot_ref, i, masked):
            sub = lax.broadcasted_iota(jnp.int32, (8, tq), 0)
            rows = pl.ds(pl.multiple_of(i * tq, tq), tq)
            stage = []
            for e in range(hps):
                cols = slice(e * hd, (e + 1) * hd)
                st, cq = _scores_t(k_ref[:, cols], q_ref[:, cols], ct_ref, c_ref, hp * hps + e,
                                   i, j, tq, tk, scale, masked)
                dp = _dot(v_ref[:, cols], dot_ref[cols, :])
                stage.append((st, cq, dp))
            grads = []
            for e, (st, cq, dp) in enumerate(stage):
                p = jnp.exp2(st - (lse_ref[e] - cq))
                dl = jnp.sum(jnp.where(sub == hp * hps + e, delta_ref[...], 0.0), axis=0, keepdims=True)
                ds = p * (dp - dl)
                part = ds[:, 0:LANES]
                for g in range(1, tq // LANES):
                    part = part + ds[:, g * LANES:(g + 1) * LANES]
                dc_acc[e] += part
                drow_ref[e, i] += jnp.broadcast_to(jnp.sum(ds, axis=0, keepdims=True), (8, tq))
                grads.append((p.astype(BF16), ds.astype(BF16)))
            for e, (pb, dsb) in enumerate(grads):
                cols = slice(e * hd, (e + 1) * hd)
                dv_acc[e] += _dot_nt(pb, dot_ref[cols, :])
                dk_acc[e] += _dot(dsb, q_ref[:, cols]) * scale
                dq_ref[rows, cols] += _dot_tn(dsb, k_ref[:, cols]) * scale

        slot_a = (qa_ref, cta_ref, lsea_ref, deltaa_ref, dota_ref, i_a)
        slot_b = (qb_ref, ctb_ref, lseb_ref, deltab_ref, dotb_ref, i_b)
        edge_a = (j == i_a) | (j == 0)
        pl.when(edge_a)(lambda: update(*slot_a, True))
        pl.when(jnp.logical_not(edge_a))(lambda: update(*slot_a, False))
        pl.when(has_b & (j == 0))(lambda: update(*slot_b, True))
        pl.when(has_b & (j != 0))(lambda: update(*slot_b, False))

        @pl.when((i_a == nq - 1) | (has_b & (i_b == nq - 1)))
        def _():
            for e in range(hps):
                cols = slice(e * hd, (e + 1) * hd)
                dk_ref[:, cols] = dk_acc[e].astype(BF16)
                dv_ref[:, cols] = dv_acc[e].astype(BF16)
                dcs_ref[e] = -dc_acc[e]

    nsteps = len(steps)
    nhp = N_HEADS // hps
    kv = pl.BlockSpec((tk, hps * hd), lambda h, p, jt, ia, ib, vb: (jt[p], h))

    def q_side(sel):
        return [pl.BlockSpec((tq, hps * hd), lambda h, p, jt, ia, ib, vb: (sel(ia, ib)[p], h)),
                pl.BlockSpec((8, tq), lambda h, p, jt, ia, ib, vb: (0, sel(ia, ib)[p])),
                pl.BlockSpec((hps, 1, tq), lambda h, p, jt, ia, ib, vb: (h, 0, sel(ia, ib)[p])),
                pl.BlockSpec((N_HEADS, tq), lambda h, p, jt, ia, ib, vb: (0, sel(ia, ib)[p])),
                pl.BlockSpec((hps * hd, tq), lambda h, p, jt, ia, ib, vb: (h, sel(ia, ib)[p]))]

    qa_specs, qb_specs = q_side(lambda ia, ib: ia), q_side(lambda ia, ib: ib)
    first = lambda: (pl.program_id(0) == 0) & (pl.program_id(1) == 0)
    last = lambda: (pl.program_id(0) == nhp - 1) & (pl.program_id(1) == nsteps - 1)
    q_args = (q, ct, lse, delta, dot_t)
    return pl.pallas_call(
        _carried(body, 17, 5, carry, first, last), name=name,
        grid_spec=pltpu.PrefetchScalarGridSpec(
            num_scalar_prefetch=4, grid=(nhp, nsteps),
            in_specs=[qa_specs[0], kv, kv, pl.BlockSpec((tk, LANES), lambda h, p, jt, ia, ib, vb: (jt[p], 0))]
                     + qa_specs[1:] + qb_specs + [ANY] * len(carry),
            out_specs=[pl.BlockSpec((t, hps * hd), lambda h, p, jt, ia, ib, vb: (0, h)), kv, kv,
                       pl.BlockSpec((hps, tk, LANES), lambda h, p, jt, ia, ib, vb: (h, jt[p], 0)),
                       pl.BlockSpec((hps, nq, 8, tq), lambda h, p, jt, ia, ib, vb: (h, 0, 0, 0))]
                      + [ANY] * len(carry),
            scratch_shapes=[pltpu.VMEM((hps, tk, hd), F32), pltpu.VMEM((hps, tk, hd), F32),
                            pltpu.VMEM((hps, tk, LANES), F32)] + _carry_scratch(carry)),
        out_shape=[jax.ShapeDtypeStruct((t, d), F32), jax.ShapeDtypeStruct((t, d), BF16),
                   jax.ShapeDtypeStruct((t, d), BF16), jax.ShapeDtypeStruct((N_HEADS, t, LANES), F32),
                   jax.ShapeDtypeStruct((N_HEADS, nq, 8, tq), F32)] + _carry_shapes(carry),
        compiler_params=_params(("arbitrary", "arbitrary"), VMEM_BIG),
    )(jt, ia, ib, vb, q, k, v, c, ct, lse, delta, dot_t, *q_args, *[a for _, a in carry])


def _adamw(w, g, m, v, name):
    r, c = w.shape
    tr = r
    for cand in (256, 128, 64, 32, 16, 8):
        if r % cand == 0 and r > cand:
            tr = cand
            break
    bc1 = 1.0 - ADAM_B1 ** ADAM_STEP
    bc2 = 1.0 - ADAM_B2 ** ADAM_STEP

    def body(w_ref, g_ref, m_ref, v_ref, d_ref, nm_ref, nv_ref):
        gg = g_ref[...]
        nm = ADAM_B1 * m_ref[...] + (1.0 - ADAM_B1) * gg
        nv = ADAM_B2 * v_ref[...] + (1.0 - ADAM_B2) * (gg * gg)
        d_ref[...] = -ADAM_LR * ((nm / bc1) / (jnp.sqrt(nv / bc2) + ADAM_EPS) + ADAM_WD * w_ref[...])
        nm_ref[...] = nm
        nv_ref[...] = nv

    blk = pl.BlockSpec((tr, c), lambda i: (i, 0))
    shp = jax.ShapeDtypeStruct((r, c), F32)
    return pl.pallas_call(
        body, name=name, grid=(r // tr,), in_specs=[blk] * 4, out_specs=[blk] * 3,
        out_shape=[shp] * 3, compiler_params=_params(("arbitrary",), VMEM_MID),
    )(w, g, m, v)


def _reduce_adamw(w, m, v, landed, name):
    nl, r, c = w.shape
    tr = next(cand for cand in range(min(r, ADAM_ROWS_MAX), 0, -BF16_ROWS) if r % cand == 0)
    nr = r // tr
    bc1 = 1.0 - ADAM_B1 ** ADAM_STEP
    bc2 = 1.0 - ADAM_B2 ** ADAM_STEP

    def body(*refs):
        w_ref, m_ref, v_ref = refs[:3]
        src_refs = refs[3:3 + nl]
        g_ref, d_ref, nm_ref, nv_ref = refs[3 + nl:]

        def update(src):
            gg = src[0].astype(F32)
            for s in range(1, N_DEV):
                gg = gg + src[s].astype(F32)
            nm = ADAM_B1 * m_ref[0] + (1.0 - ADAM_B1) * gg
            nv = ADAM_B2 * v_ref[0] + (1.0 - ADAM_B2) * (gg * gg)
            g_ref[0] = gg
            d_ref[0] = -ADAM_LR * ((nm / bc1) / (jnp.sqrt(nv / bc2) + ADAM_EPS) + ADAM_WD * w_ref[0])
            nm_ref[0] = nm
            nv_ref[0] = nv

        for idx in range(nl):
            pl.when(pl.program_id(0) == idx)(functools.partial(update, src_refs[idx]))

    def src_spec(idx):
        return pl.BlockSpec((N_DEV, tr, c),
                            lambda l, i: (0, jnp.where(l == idx, i, jnp.where(l < idx, 0, nr - 1)), 0))

    blk = pl.BlockSpec((1, tr, c), lambda l, i: (l, i, 0))
    shp = jax.ShapeDtypeStruct((nl, r, c), F32)
    return pl.pallas_call(
        body, name=name, grid=(nl, nr), in_specs=[blk] * 3 + [src_spec(idx) for idx in range(nl)],
        out_specs=[blk] * 4, out_shape=[shp] * 4,
        compiler_params=_params(("arbitrary", "arbitrary"), VMEM_MID),
    )(w, m, v, *landed)


def _sum_sources(r, name):
    n, rows, c = r.shape
    tr = next(cand for cand in range(min(rows, SUM_ROWS_MAX), 0, -BF16_ROWS) if rows % cand == 0)

    def body(r_ref, o_ref):
        acc = r_ref[0].astype(F32)
        for s in range(1, n):
            acc = acc + r_ref[s].astype(F32)
        o_ref[...] = acc

    return pl.pallas_call(
        body, name=name, grid=(rows // tr,),
        in_specs=[pl.BlockSpec((n, tr, c), lambda i: (0, i, 0))],
        out_specs=pl.BlockSpec((tr, c), lambda i: (i, 0)),
        out_shape=jax.ShapeDtypeStruct((rows, c), F32),
        compiler_params=_params(("arbitrary",), VMEM_MID),
    )(r)


def _all_gather(parts, name):
    n = len(parts)

    def body(*refs):
        x_refs, out_refs = refs[:n], refs[n:2 * n]
        send_sems, recv_sems, local_sems = refs[2 * n:]
        mx, my, mc = lax.axis_index("x"), lax.axis_index("y"), lax.axis_index("c")
        me, sibling = (mx, my, mc), (mx, my, 1 - mc)
        chips = [(1 - mx, my), (mx, 1 - my), (1 - mx, 1 - my)]

        def copy(p, k, block, to, from_input=False):
            px, py, pc = block
            rows = out_refs[p].at[4 * px + 2 * py + pc]
            return pltpu.make_async_remote_copy(
                src_ref=x_refs[p] if from_input else rows, dst_ref=rows,
                send_sem=send_sems.at[7 * p + k], recv_sem=recv_sems.at[7 * p + k],
                device_id=to, device_id_type=MESH)

        mine, sent = [], []
        for p in range(n):
            own = pltpu.make_async_copy(x_refs[p], out_refs[p].at[4 * mx + 2 * my + mc], local_sems.at[p])
            own.start()
            mine.append(own)
            first = [copy(p, 0, me, sibling, True)]
            first += [copy(p, 1 + j, me, (*chip, mc), True) for j, chip in enumerate(chips)]
            for cp in first:
                cp.start()
            sent += first
        for p in range(n):
            for j, chip in enumerate(chips):
                copy(p, 1 + j, (*chip, mc), me).wait_recv()
                fwd = copy(p, 4 + j, (*chip, mc), sibling)
                fwd.start()
                sent.append(fwd)
        for p in range(n):
            copy(p, 0, sibling, me).wait_recv()
            for j, chip in enumerate(chips):
                copy(p, 4 + j, (*chip, 1 - mc), me).wait_recv()
        for cp in sent:
            cp.wait_send()
        for own in mine:
            own.wait()

    return pl.pallas_call(
        body, name=name, in_specs=[ANY] * n, out_specs=[ANY] * n,
        out_shape=[jax.ShapeDtypeStruct((N_DEV,) + a.shape, a.dtype) for a in parts],
        scratch_shapes=[pltpu.SemaphoreType.DMA((7 * n,)), pltpu.SemaphoreType.DMA((7 * n,)),
                        pltpu.SemaphoreType.DMA((n,))],
    )(*parts)


def _pack_rows(parts, width, mult, lead=0):
    out = []
    for a in parts:
        head = a.shape[:lead]
        flat = a.reshape(head + (-1,))
        padn = (-flat.shape[-1]) % (width * mult)
        if padn:
            flat = jnp.pad(flat, [(0, 0)] * lead + [(0, padn)])
        out.append(flat.reshape(head + (-1, width)))
    return jnp.concatenate(out, axis=lead)


def _rows_of(shape, width, mult):
    n = math.prod(shape)
    per = width * mult
    return ((n + per - 1) // per) * mult


def _unpack_rows(buf, shapes, width, mult):
    lead = buf.shape[:-2]
    out, off = [], 0
    for shp in shapes:
        r = _rows_of(shp, width, mult)
        flat = buf[..., off:off + r, :].reshape(lead + (r * width,))
        out.append(flat[..., :math.prod(shp)].reshape(lead + tuple(shp)))
        off += r
    return out


def _cols_from_devices(g):
    nd = g.ndim
    perm = tuple(range(1, nd - 1)) + (0, nd - 1)
    t = jnp.transpose(g, perm)
    return t.reshape(t.shape[:-2] + (t.shape[-2] * t.shape[-1],))


def _cols_to_devices(a):
    c = a.shape[-1] // N_DEV
    t = a.reshape(a.shape[:-1] + (N_DEV, c))
    nd = t.ndim
    perm = (nd - 2,) + tuple(range(0, nd - 2)) + (nd - 1,)
    return jnp.transpose(t, perm)


WIDTH = 1024


def kernel(x, meta, ffn1_wg, ffn1_wu, ffn1_wd, ffn2_wg, ffn2_wu, ffn2_wd, ln_gain, ln_bias, conv_w_in, conv_w, conv_w_out, kv_w, f_bias, attn_w_q, attn_w_o, loss_target, m_meta, m_ffn1_wg, m_ffn1_wu, m_ffn1_wd, m_ffn2_wg, m_ffn2_wu, m_ffn2_wd, m_ln_gain, m_ln_bias, m_conv_w_in, m_conv_w, m_conv_w_out, m_kv_w, m_f_bias, m_attn_w_q, m_attn_w_o, v_meta, v_ffn1_wg, v_ffn1_wu, v_ffn1_wd, v_ffn2_wg, v_ffn2_wu, v_ffn2_wd, v_ln_gain, v_ln_bias, v_conv_w_in, v_conv_w, v_conv_w_out, v_kv_w, v_f_bias, v_attn_w_q, v_attn_w_o):
    depth = ln_gain.shape[0]
    alpha = float((2 * depth) ** 0.25)
    d = x.shape[-1]
    seq = x.shape[1]
    t = ROW0 + seq
    fsh = ffn1_wg.shape[-1]
    f = fsh * N_DEV
    fck = MXU_COLS
    nc = f // fck
    me = 4 * lax.axis_index("x") + 2 * lax.axis_index("y") + lax.axis_index("c")

    def gather_of(parts):
        return [(True, a.astype(BF16)) for a in parts]

    small = [meta, ln_gain, ln_bias, conv_w]
    small_shapes = [a.shape for a in small]
    g1g, g1u, g1d, gcin, gcout, gsmall = _all_gather(
        [a.astype(BF16) for a in (ffn1_wg[0], ffn1_wu[0], ffn1_wd[0], conv_w_in[0], conv_w_out[0])]
        + [_pack_rows(small, WIDTH, F32_ROWS)], "ag_first")
    gmeta, ggain, gbias, gcw = _unpack_rows(gsmall, small_shapes, WIDTH, F32_ROWS)

    def ffn_chunks(gg, gu, gd):
        up = lambda g: jnp.transpose(_cols_from_devices(g).reshape(d, nc, fck), (1, 0, 2))
        return up(gg), up(gu), gd.reshape(nc, fck, d)

    w_in = _cols_from_devices(gcin)
    w_out = gcout.reshape(d, d)
    fb = jnp.pad(f_bias, (0, LANES - N_HEADS)).reshape(1, LANES)
    meta_f = _cols_from_devices(gmeta)
    gain_f = _cols_from_devices(ggain)
    bias_f = _cols_from_devices(gbias)
    cw_f = _cols_from_devices(gcw)[0]

    def gb(l, n):
        return gain_f[l, n].reshape(1, d), bias_f[l, n].reshape(1, d)

    ones = jnp.ones((1, d), F32)
    zeros = jnp.zeros((1, d), F32)

    h0 = jnp.concatenate([jnp.zeros((PAD, d), F32), meta_f, x[0]], axis=0)

    w1 = ffn_chunks(g1g, g1u, g1d)
    g00, b00 = gb(0, 0)
    xh1, rs1, hb1, gg1, uu1, hb0, g2g, g2u = _ffn_fwd(
        h0, ones, zeros, *w1, g00, b00, alpha, "ffn_fwd_0a", carry=gather_of([ffn2_wg[0], ffn2_wu[0]]),
        input_t=True)
    g01, b01 = gb(0, 1)
    xh2, rs2, hb2, pp, mb, g2d, gkv = _conv_fwd(
        xh1, g00, b00, w_in, cw_f, w_out, g01, b01, alpha, "conv_fwd", carry=gather_of([ffn2_wd[0], kv_w.T]))
    w2 = ffn_chunks(g2g, g2u, g2d)
    g02, b02 = gb(0, 2)
    xh3, rs3, hb3, gg3, uu3, g3g, g3u = _ffn_fwd(
        xh2, g01, b01, *w2, g02, b02, alpha, "ffn_fwd_0b", carry=gather_of([ffn1_wg[1], ffn1_wu[1]]))
    kvw = gkv.reshape(gkv.shape[0] * gkv.shape[1], d).T
    wk, wv = kvw[:, :d], kvw[:, d:2 * d]
    wf = jnp.pad(kvw[:, 2 * d:], ((0, 0), (0, LANES - N_HEADS)))
    kk, vv, logit, cc, cct, g3d = _kv_fwd(xh3, g02, b02, wk, wv, wf, fb, "kv_fwd",
                                          carry=gather_of([ffn1_wd[1]]))

    w3 = ffn_chunks(g3g, g3u, g3d)
    g10, b10 = gb(1, 0)
    xh4, rs4, hb4, gg4, uu4, gwq = _ffn_fwd(xh3, g02, b02, *w3, g10, b10, alpha, "ffn_fwd_1a",
                                             carry=gather_of([attn_w_q[0]]))
    w_q = gwq.reshape(d, d)
    qq = _proj(xh4, g10, b10, w_q, "q_proj")
    ot, lse, gwo, g4g, g4u, g4d = _attn_fwd(
        qq, kk, vv, cc, cct, "attn_fwd", carry=gather_of([attn_w_o[0], ffn2_wg[1], ffn2_wu[1], ffn2_wd[1]]))
    w_o = gwo.reshape(d, d)
    g11, b11 = gb(1, 1)
    xh5, rs5, hb5 = _attn_out_fwd(ot, xh4, g10, b10, w_o, g11, b11, alpha, "attn_out_fwd")
    w4 = ffn_chunks(g4g, g4u, g4d)
    g12, b12 = gb(1, 2)
    xh6, rs6, _, gg6, uu6 = _ffn_fwd(xh5, g11, b11, *w4, g12, b12, alpha, "ffn_fwd_1b")


    dgain = [[None] * 3 for _ in range(depth)]
    dbias = [[None] * 3 for _ in range(depth)]

    def to_col_owners(g):
        return (False, _cols_to_devices(g).astype(BF16))

    def to_row_owners(g):
        return (False, g.reshape(N_DEV, g.shape[0] // N_DEV, g.shape[1]).astype(BF16))

    dh5, do6, dg6, du6, a6, dgain[1][2], dbias[1][2], loss_l = _ffn_bwd(
        None, xh6, rs6, g12, gg6, uu6, *w4, alpha, "ffn_bwd_1b", loss_target=loss_target[0], loss_bias=b12)
    dw4g, dw4u = _wgrad(hb5, [dg6, du6], "wgrad_up_1b")
    (dw4dt,) = _wgrad(do6, [a6], "wgrad_down_1b")

    dres4, dmix5, dot_t, delta, dgain[1][1], dbias[1][1] = _attn_out_bwd(dh5, xh5, rs5, g11, ot, w_o, alpha, "attn_out_bwd")
    (dwo,) = _wgrad(ot, [dmix5], "wgrad_wo")
    dq, dkk, dvv, dcs, drow, l4g, l4u, l4d, lwo = _attn_bwd(
        qq, kk, vv, cc, cct, lse, delta, dot_t, "attn_bwd",
        carry=[to_col_owners(dw4g), to_col_owners(dw4u), to_row_owners(dw4dt.T), to_row_owners(dwo)])
    dh4 = _add_proj_nt(dres4, dq, w_q, "q_bwd")
    (dwq,) = _wgrad(hb4, [dq], "wgrad_wq")

    dh3a, do4, dg4, du4, a4, dgain[1][0], dbias[1][0] = _ffn_bwd(dh4, xh4, rs4, g10, gg4, uu4, *w3, alpha, "ffn_bwd_1a")
    dw3g, dw3u = _wgrad(hb3, [dg4, du4], "wgrad_up_1a")
    (dw3dt,) = _wgrad(do4, [a4], "wgrad_down_1a")

    dcq = jnp.pad(drow[:, :, 0, :].reshape(N_HEADS, t).T, ((0, 0), (0, LANES - N_HEADS)))
    dh3, dlogit, dfb = _kv_bwd(dkk, dvv, dcs, dcq, logit, dh3a, wk, wv, wf, "kv_bwd")
    dwk, dwv = _wgrad(hb3, [dkk, dvv], "wgrad_kv")
    (dwf,) = _wgrad(hb3, [dlogit], "wgrad_f")
    dkv = jnp.concatenate([dwk, dwv, dwf[:, :N_HEADS]], axis=1)

    dh2, do3, dg3, du3, a3, dgain[0][2], dbias[0][2], lwq, l3g, l3u, l3d, lkv = _ffn_bwd(
        dh3, xh3, rs3, g02, gg3, uu3, *w2, alpha, "ffn_bwd_0b",
        carry=[to_row_owners(dwq), to_col_owners(dw3g), to_col_owners(dw3u), to_row_owners(dw3dt.T),
               to_row_owners(dkv.T)])
    dw2g, dw2u = _wgrad(hb2, [dg3, du3], "wgrad_up_0b")
    (dw2dt,) = _wgrad(do3, [a3], "wgrad_down_0b")

    dh1, dmix2, dpp, dcw, dgain[0][1], dbias[0][1] = _conv_bwd(dh2, xh2, rs2, g01, pp, cw_f, w_in, w_out, alpha, "conv_bwd")
    (dwin,) = _wgrad(hb1, [dpp], "wgrad_conv_in")
    (dwout,) = _wgrad(mb, [dmix2], "wgrad_conv_out")

    dh0, do1, dg1, du1, a1, dgain[0][0], dbias[0][0], l2g, l2u, l2d, lcin, lcout = _ffn_bwd(
        dh1, xh1, rs1, g00, gg1, uu1, *w1, alpha, "ffn_bwd_0a",
        carry=[to_col_owners(dw2g), to_col_owners(dw2u), to_row_owners(dw2dt.T), to_col_owners(dwin),
               to_row_owners(dwout)])
    (dw1dt,) = _wgrad(do1, [a1], "wgrad_down_0a")
    dw1g, l1d = _wgrad(hb0, [dg1], "wgrad_upg_0a", carry=[to_row_owners(dw1dt.T)])
    dw1u, l1g = _wgrad(hb0, [du1], "wgrad_upu_0a", carry=[to_col_owners(dw1g)])
    dmeta = dh0[PAD:ROW0]
    dgain_f = jnp.stack([jnp.concatenate(r, axis=0) for r in dgain])
    dbias_f = jnp.stack([jnp.concatenate(r, axis=0) for r in dbias])
    small_full = [dmeta, dgain_f, dbias_f, dcw[None], dfb, loss_l]
    small_full_shapes = [a.shape for a in small_full]
    l1u, gsmall_grads = _exchange([to_col_owners(dw1u), (True, _pack_rows(small_full, WIDTH, F32_ROWS))], "rs_last")

    grad_x = dh0[ROW0:].reshape(1, seq, d)
    rsmall = _sum_sources(gsmall_grads, "small_sum")
    smeta, sgain, sbias, scw, sfb, sloss = _unpack_rows(rsmall, small_full_shapes, WIDTH, F32_ROWS)
    loss = sloss[0, 0]
    csh = d // N_DEV

    def my_cols(a):
        return lax.dynamic_slice_in_dim(a, me * csh, csh, axis=a.ndim - 1)

    grads = {"meta": my_cols(smeta), "ln_gain": my_cols(sgain), "ln_bias": my_cols(sbias),
             "conv_w": my_cols(scw), "f_bias": sfb[0, :N_HEADS], "kv_w": _sum_sources(lkv, "kv_sum").T}
    landed = {"ffn1_wg": [l1g, l3g], "ffn1_wu": [l1u, l3u], "ffn1_wd": [l1d, l3d],
              "ffn2_wg": [l2g, l4g], "ffn2_wu": [l2u, l4u], "ffn2_wd": [l2d, l4d],
              "conv_w_in": [lcin], "conv_w_out": [lcout], "attn_w_q": [lwq], "attn_w_o": [lwo]}
    weights = dict(meta=meta, ffn1_wg=ffn1_wg, ffn1_wu=ffn1_wu, ffn1_wd=ffn1_wd, ffn2_wg=ffn2_wg,
                   ffn2_wu=ffn2_wu, ffn2_wd=ffn2_wd, ln_gain=ln_gain, ln_bias=ln_bias,
                   conv_w_in=conv_w_in, conv_w=conv_w, conv_w_out=conv_w_out, kv_w=kv_w,
                   f_bias=f_bias, attn_w_q=attn_w_q, attn_w_o=attn_w_o)
    moms = dict(meta=(m_meta, v_meta), ffn1_wg=(m_ffn1_wg, v_ffn1_wg), ffn1_wu=(m_ffn1_wu, v_ffn1_wu),
                ffn1_wd=(m_ffn1_wd, v_ffn1_wd), ffn2_wg=(m_ffn2_wg, v_ffn2_wg), ffn2_wu=(m_ffn2_wu, v_ffn2_wu),
                ffn2_wd=(m_ffn2_wd, v_ffn2_wd), ln_gain=(m_ln_gain, v_ln_gain), ln_bias=(m_ln_bias, v_ln_bias),
                conv_w_in=(m_conv_w_in, v_conv_w_in), conv_w=(m_conv_w, v_conv_w),
                conv_w_out=(m_conv_w_out, v_conv_w_out), kv_w=(m_kv_w, v_kv_w), f_bias=(m_f_bias, v_f_bias),
                attn_w_q=(m_attn_w_q, v_attn_w_q), attn_w_o=(m_attn_w_o, v_attn_w_o))

    names = list(weights)
    g_out, d_out, m_out, v_out = [], [], [], []
    for n in names:
        w = weights[n]
        shp = w.shape
        mm, vv_ = moms[n]
        if n in landed:
            three = (len(landed[n]),) + shp[-2:]
            g, dl, nm, nv = _reduce_adamw(w.reshape(three), mm.reshape(three), vv_.reshape(three),
                                          landed[n], "adamw_" + n)
            g = g.reshape(shp)
        else:
            two = (1, shp[0]) if w.ndim == 1 else (math.prod(shp[:-1]), shp[-1])
            g = grads[n].reshape(shp)
            dl, nm, nv = _adamw(w.reshape(two), g.reshape(two), mm.reshape(two), vv_.reshape(two), "adamw_" + n)
        g_out.append(g)
        d_out.append(dl.reshape(shp))
        m_out.append(nm.reshape(shp))
        v_out.append(nv.reshape(shp))
    return (loss, grad_x, *g_out, *d_out, *m_out, *v_out)
```

```python
import functools
import math

import jax
import jax.numpy as jnp
from jax import lax
from jax.experimental import pallas as pl
from jax.experimental.pallas import tpu as pltpu

F32 = jnp.float32
BF16 = jnp.bfloat16

N_DEV = 8
N_HEADS = 8
N_META = 16
PAD = 112
ROW0 = PAD + N_META
LN_EPS = 1e-5
NEG_INF = -1e30
LOG2E = 1.4426950408889634
ATTN_HEADS_PER_STEP = 8
ATTN_BWD_HEADS_PER_STEP = 2
LANES = 128
MXU_COLS = 256
FFN_FWD_CHUNKS = 11
FFN_BWD_CHUNKS = 4

ADAM_LR = 0.001
ADAM_B1 = 0.9
ADAM_B2 = 0.999
ADAM_EPS = 1e-08
ADAM_WD = 0.01
ADAM_STEP = 10

ROW_TILES = (640, 128)
LOSS_TILE = 128
BF16_ROWS = 16
F32_ROWS = 8
SUM_ROWS_MAX = 768
ADAM_ROWS_MAX = 256
VMEM_BIG = 56 << 20
VMEM_MID = 40 << 20

ANY = pl.BlockSpec(memory_space=pl.ANY)
MESH = pl.DeviceIdType.MESH


def _row_tile(t):
    for c in ROW_TILES:
        if t % c == 0:
            return c
    raise ValueError(f"no row tile for {t}")


def _dot(a, b):
    return jnp.dot(a, b, preferred_element_type=F32)


def _dot_nt(a, b):
    return lax.dot_general(a, b, (((1,), (1,)), ((), ())), preferred_element_type=F32)


def _dot_tn(a, b):
    return lax.dot_general(a, b, (((0,), (0,)), ((), ())), preferred_element_type=F32)


def _params(sem, vmem):
    return pltpu.CompilerParams(dimension_semantics=sem, vmem_limit_bytes=vmem)


def _ln_fwd(z):
    mu = jnp.mean(z, axis=-1, keepdims=True)
    zc = z - mu
    var = jnp.mean(zc * zc, axis=-1, keepdims=True)
    rstd = lax.rsqrt(var + LN_EPS)
    return zc * rstd, rstd


def _ln_bwd(dh, xhat, rstd, gain):
    dxh = dh * gain
    m1 = jnp.mean(dxh, axis=-1, keepdims=True)
    m2 = jnp.mean(dxh * xhat, axis=-1, keepdims=True)
    dz = rstd * (dxh - m1 - xhat * m2)
    return dz, jnp.sum(dh * xhat, axis=0, keepdims=True), jnp.sum(dh, axis=0, keepdims=True)


def _load_resident(pairs, sems):
    cps = [pltpu.make_async_copy(src, dst, sems.at[k]) for k, (src, dst) in enumerate(pairs)]
    for cp in cps:
        cp.start()
    for cp in cps:
        cp.wait()


def _peer_ids():
    mx, my, mc = lax.axis_index("x"), lax.axis_index("y"), lax.axis_index("c")
    peers = []
    for kk in range(1, N_DEV):
        px = 1 - mx if (kk >> 2) & 1 else mx
        py = 1 - my if (kk >> 1) & 1 else my
        pc = 1 - mc if kk & 1 else mc
        peers.append(((px, py, pc), 4 * px + 2 * py + pc))
    return 4 * mx + 2 * my + mc, peers


def _exchange_copies(jobs, send_sems, recv_sems, local_sems, starting):
    me_id, peers = _peer_ids()
    for n, (gather, src, dst) in enumerate(jobs):
        own = pltpu.make_async_copy(src if gather else src.at[me_id], dst.at[me_id], local_sems.at[n])
        own.start() if starting else own.wait()
        for k, (dev, pid) in enumerate(peers):
            sem = (N_DEV - 1) * n + k
            out = src if gather else src.at[pid]
            send = pltpu.make_async_remote_copy(
                src_ref=out, dst_ref=dst.at[me_id], send_sem=send_sems.at[sem], recv_sem=recv_sems.at[sem],
                device_id=dev, device_id_type=MESH)
            if starting:
                send.start()
            else:
                pltpu.make_async_remote_copy(
                    src_ref=out, dst_ref=dst.at[pid], send_sem=send_sems.at[sem], recv_sem=recv_sems.at[sem],
                    device_id=dev, device_id_type=MESH).wait_recv()
                send.wait_send()


def _carried(body, n_in, n_out, carry, first, last):
    nj = len(carry)
    if nj == 0:
        return body

    def wrapped(*refs):
        ins, srcs = refs[:n_in], refs[n_in:n_in + nj]
        outs = refs[n_in + nj:n_in + nj + n_out]
        dsts = refs[n_in + nj + n_out:n_in + 2 * nj + n_out]
        scratch, sems = refs[n_in + 2 * nj + n_out:-3], refs[-3:]
        jobs = [(g, s, r) for (g, _), s, r in zip(carry, srcs, dsts)]

        @pl.when(first())
        def _():
            _exchange_copies(jobs, *sems, starting=True)

        body(*ins, *outs, *scratch)

        @pl.when(last())
        def _():
            _exchange_copies(jobs, *sems, starting=False)

    return wrapped


def _carry_shapes(carry):
    return [jax.ShapeDtypeStruct((N_DEV,) + a.shape if g else a.shape, a.dtype) for g, a in carry]


def _carry_scratch(carry):
    if not carry:
        return []
    n = len(carry)
    return [pltpu.SemaphoreType.DMA(((N_DEV - 1) * n,)), pltpu.SemaphoreType.DMA(((N_DEV - 1) * n,)),
            pltpu.SemaphoreType.DMA((n,))]


def _exchange(carry, name):
    n = len(carry)

    def body(*refs):
        jobs = [(g, s, r) for (g, _), s, r in zip(carry, refs[:n], refs[n:2 * n])]
        _exchange_copies(jobs, *refs[2 * n:], starting=True)
        _exchange_copies(jobs, *refs[2 * n:], starting=False)

    return pl.pallas_call(
        body, name=name, in_specs=[ANY] * n, out_specs=[ANY] * n, out_shape=_carry_shapes(carry),
        scratch_shapes=_carry_scratch(carry),
    )(*[a for _, a in carry])


def _ffn_fwd(xh, gi, bi, wg, wu, wd, go, bo, alpha, name, carry=(), input_t=False):
    t, d = xh.shape
    nch, _, fc = wg.shape
    f = nch * fc
    per = min(FFN_FWD_CHUNKS, nch)
    nc = -(-nch // per)
    tm = _row_tile(t)
    nt = t // tm

    def body(xh_ref, gi_ref, bi_ref, wg_hbm, wu_hbm, wd_hbm, go_ref, bo_ref,
             xo_ref, rs_ref, hb_ref, g_ref, u_ref, *tail):
        hin_ref = tail[0] if input_t else None
        wg_v, wu_v, wd_v, acc, hbs, sems = tail[1:] if input_t else tail
        i = pl.program_id(0)
        c = pl.program_id(1)

        @pl.when((i == 0) & (c == 0))
        def _():
            _load_resident([(wg_hbm, wg_v), (wu_hbm, wu_v), (wd_hbm, wd_v)], sems)

        @pl.when(c == 0)
        def _():
            h = xh_ref[...] * gi_ref[...] + bi_ref[...]
            hbs[...] = h.astype(BF16)
            acc[...] = jnp.zeros_like(acc)
            if input_t:
                hin_ref[...] = hbs[...].T

        def chunk(k):
            ck = c * per + k
            cols = slice(k * fc, (k + 1) * fc)
            hb = hbs[...]
            g = _dot(hb, wg_v[ck])
            u = _dot(hb, wu_v[ck])
            a = (g * jax.nn.sigmoid(g)) * u
            g_ref[:, cols] = g.astype(BF16)
            u_ref[:, cols] = u.astype(BF16)
            acc[...] += _dot(a.astype(BF16), wd_v[ck])

        for k in range(per):
            if (nc - 1) * per + k < nch:
                chunk(k)
            else:
                pl.when(c * per + k < nch)(functools.partial(chunk, k))

        @pl.when(c == nc - 1)
        def _():
            h = xh_ref[...] * gi_ref[...] + bi_ref[...]
            xhat, rstd = _ln_fwd(alpha * h + 0.5 * acc[...])
            xo_ref[...] = xhat
            rs_ref[...] = rstd
            hb_ref[...] = (xhat * go_ref[...] + bo_ref[...]).astype(BF16).T

    row = pl.BlockSpec((tm, d), lambda i, c: (i, 0))
    vec = pl.BlockSpec((1, d), lambda i, c: (0, 0))
    chunk = pl.BlockSpec((tm, per * fc), lambda i, c: (i, c))
    first = lambda: (pl.program_id(0) == 0) & (pl.program_id(1) == 0)
    last = lambda: (pl.program_id(0) == nt - 1) & (pl.program_id(1) == nc - 1)
    col = pl.BlockSpec((d, tm), lambda i, c: (0, i))
    t_spec, t_shape = ([col], [jax.ShapeDtypeStruct((d, t), BF16)]) if input_t else ([], [])
    return pl.pallas_call(
        _carried(body, 8, 5 + len(t_spec), carry, first, last), name=name, grid=(nt, nc),
        in_specs=[row, vec, vec, ANY, ANY, ANY, vec, vec] + [ANY] * len(carry),
        out_specs=[row, pl.BlockSpec((tm, 1), lambda i, c: (i, 0)), col, chunk, chunk] + t_spec
                  + [ANY] * len(carry),
        out_shape=[jax.ShapeDtypeStruct((t, d), F32), jax.ShapeDtypeStruct((t, 1), F32),
                   jax.ShapeDtypeStruct((d, t), BF16), jax.ShapeDtypeStruct((t, f), BF16),
                   jax.ShapeDtypeStruct((t, f), BF16)] + t_shape + _carry_shapes(carry),
        scratch_shapes=[pltpu.VMEM((nch, d, fc), BF16), pltpu.VMEM((nch, d, fc), BF16),
                        pltpu.VMEM((nch, fc, d), BF16), pltpu.VMEM((tm, d), F32),
                        pltpu.VMEM((tm, d), BF16), pltpu.SemaphoreType.DMA((3,))] + _carry_scratch(carry),
        compiler_params=_params(("arbitrary", "arbitrary"), VMEM_BIG),
    )(xh, gi, bi, wg, wu, wd, go, bo, *[a for _, a in carry])


def _ffn_bwd(dh, xo, rs, go, gs, us, wg, wu, wd, alpha, name, carry=(), loss_target=None, loss_bias=None):
    t, d = xo.shape
    nch, _, fc = wg.shape
    f = nch * fc
    per = min(FFN_BWD_CHUNKS, nch)
    nc = -(-nch // per)
    tm = _row_tile(t)
    nt = t // tm

    with_loss = loss_target is not None
    nsub, lead = tm // LOSS_TILE, ROW0 // LOSS_TILE
    nlead = nsub + 1 if with_loss else 1

    def body(*refs):
        lead_refs = refs[:nlead]
        xo_ref, rs_ref, go_ref, g_ref, u_ref, wg_hbm, wu_hbm, wd_hbm = refs[nlead:nlead + 8]
        dhin_ref, dot_ref, dg_ref, du_ref, a_ref, dgain_ref, dbias_ref = refs[nlead + 8:nlead + 15]
        rest = refs[nlead + 15:]
        loss_ref, rest = (rest[0], rest[1:]) if with_loss else (None, rest)
        wg_v, wu_v, wd_v, do_ref, sems = rest[:5]
        i = pl.program_id(0)
        c = pl.program_id(1)

        @pl.when((i == 0) & (c == 0))
        def _():
            _load_resident([(wg_hbm, wg_v), (wu_hbm, wu_v), (wd_hbm, wd_v)], sems)
            dgain_ref[...] = jnp.zeros_like(dgain_ref)
            dbias_ref[...] = jnp.zeros_like(dbias_ref)
            if with_loss:
                rest[5][...] = jnp.zeros_like(rest[5])

        def tile_dh():
            if not with_loss:
                return lead_refs[0][...]
            part = rest[5]
            for k in range(nsub):
                sl = slice(k * LOSS_TILE, (k + 1) * LOSS_TILE)
                rows = i * tm + k * LOSS_TILE + lax.broadcasted_iota(jnp.int32, (LOSS_TILE, 1), 0)
                y = xo_ref[sl, :] * go_ref[...] + lead_refs[nsub][...]
                e = jnp.where(rows >= ROW0, y - lead_refs[k][...], 0.0)
                part[...] += jnp.sum(e * e, axis=0, keepdims=True)
                dhin_ref[sl, :] = e * (1.0 / d)

            @pl.when(i == nt - 1)
            def _():
                loss_ref[...] = jnp.full((1, LANES), 0.5 / d, F32) * jnp.sum(part[...])

            return dhin_ref[...]

        @pl.when(c == 0)
        def _():
            dz, dgp, dbp = _ln_bwd(tile_dh(), xo_ref[...], rs_ref[...], go_ref[...])
            dgain_ref[...] += dgp
            dbias_ref[...] += dbp
            dob = (0.5 * dz).astype(BF16)
            do_ref[...] = dob
            dot_ref[...] = dob.T
            dhin_ref[...] = alpha * dz

        def chunk(k):
            ck = c * per + k
            cols = slice(k * fc, (k + 1) * fc)
            g = g_ref[:, cols].astype(F32)
            u = u_ref[:, cols].astype(F32)
            sg = jax.nn.sigmoid(g)
            sl = g * sg
            da = _dot_nt(do_ref[...], wd_v[ck])
            dgb = (da * u * (sg * (1.0 + g * (1.0 - sg)))).astype(BF16)
            dub = (da * sl).astype(BF16)
            a_ref[:, cols] = (sl * u).astype(BF16)
            dg_ref[:, cols] = dgb
            du_ref[:, cols] = dub
            dhin_ref[...] += _dot_nt(dgb, wg_v[ck]) + _dot_nt(dub, wu_v[ck])

        for k in range(per):
            if (nc - 1) * per + k < nch:
                chunk(k)
            else:
                pl.when(c * per + k < nch)(functools.partial(chunk, k))

    row = pl.BlockSpec((tm, d), lambda i, c: (i, 0))
    vec = pl.BlockSpec((1, d), lambda i, c: (0, 0))
    chunk = pl.BlockSpec((tm, per * fc), lambda i, c: (i, c))
    first = lambda: (pl.program_id(0) == 0) & (pl.program_id(1) == 0)
    last = lambda: (pl.program_id(0) == nt - 1) & (pl.program_id(1) == nc - 1)
    if with_loss:
        lead_specs = [pl.BlockSpec((LOSS_TILE, d), lambda i, c, k=k: (jnp.maximum(i * nsub + k - lead, 0), 0))
                      for k in range(nsub)] + [vec]
        lead_args = [loss_target] * nsub + [loss_bias]
        loss_spec, loss_shape = [pl.BlockSpec((1, LANES), lambda i, c: (0, 0))], [jax.ShapeDtypeStruct((1, LANES), F32)]
        loss_scratch = [pltpu.VMEM((1, d), F32)]
    else:
        lead_specs, lead_args, loss_spec, loss_shape, loss_scratch = [row], [dh], [], [], []
    return pl.pallas_call(
        _carried(body, nlead + 8, 7 + len(loss_spec), carry, first, last), name=name, grid=(nt, nc),
        in_specs=lead_specs + [row, pl.BlockSpec((tm, 1), lambda i, c: (i, 0)), vec, chunk, chunk,
                               ANY, ANY, ANY] + [ANY] * len(carry),
        out_specs=[row, pl.BlockSpec((d, tm), lambda i, c: (0, i)), chunk, chunk, chunk, vec, vec]
                  + loss_spec + [ANY] * len(carry),
        out_shape=[jax.ShapeDtypeStruct((t, d), F32), jax.ShapeDtypeStruct((d, t), BF16),
                   jax.ShapeDtypeStruct((t, f), BF16), jax.ShapeDtypeStruct((t, f), BF16),
                   jax.ShapeDtypeStruct((t, f), BF16), jax.ShapeDtypeStruct((1, d), F32),
                   jax.ShapeDtypeStruct((1, d), F32)] + loss_shape + _carry_shapes(carry),
        scratch_shapes=[pltpu.VMEM((nch, d, fc), BF16), pltpu.VMEM((nch, d, fc), BF16),
                        pltpu.VMEM((nch, fc, d), BF16), pltpu.VMEM((tm, d), BF16),
                        pltpu.SemaphoreType.DMA((3,))] + loss_scratch + _carry_scratch(carry),
        compiler_params=_params(("arbitrary", "arbitrary"), VMEM_BIG),
    )(*lead_args, xo, rs, go, gs, us, wg, wu, wd, *[a for _, a in carry])


def _wgrad(xt, ys, name, carry=()):
    m, t = xt.shape
    n = ys[0].shape[1]
    tn = min(n, MXU_COLS)
    ny = len(ys)

    def body(*refs):
        x_hbm = refs[0]
        y_refs = refs[1:1 + ny]
        o_refs = refs[1 + ny:1 + 2 * ny]
        xv, sems = refs[1 + 2 * ny:]

        @pl.when(pl.program_id(0) == 0)
        def _():
            _load_resident([(x_hbm, xv)], sems)

        for y_ref, o_ref in zip(y_refs, o_refs):
            o_ref[...] = _dot(xv[...], y_ref[...].astype(BF16)).astype(BF16)

    steps = n // tn
    first = lambda: pl.program_id(0) == 0
    last = lambda: pl.program_id(0) == steps - 1
    return pl.pallas_call(
        _carried(body, 1 + ny, ny, carry, first, last), name=name, grid=(steps,),
        in_specs=[ANY] + [pl.BlockSpec((t, tn), lambda c: (0, c)) for _ in ys] + [ANY] * len(carry),
        out_specs=[pl.BlockSpec((m, tn), lambda c: (0, c)) for _ in ys] + [ANY] * len(carry),
        out_shape=[jax.ShapeDtypeStruct((m, n), BF16) for _ in ys] + _carry_shapes(carry),
        scratch_shapes=[pltpu.VMEM((m, t), BF16), pltpu.SemaphoreType.DMA((1,))] + _carry_scratch(carry),
        compiler_params=_params(("arbitrary",), VMEM_BIG),
    )(xt, *ys, *[a for _, a in carry])


def _shift_rows(u, halo, tm):
    r = lax.broadcasted_iota(jnp.int32, (tm, 1), 0)
    u1 = jnp.where(r == 0, halo[7:8], pltpu.roll(u, 1, 0))
    u2 = jnp.where(r == 0, halo[6:7], jnp.where(r == 1, halo[7:8], pltpu.roll(u, 2, 0)))
    return u1, u2


def _conv_fwd(xh, gi, bi, w_in, cw, w_out, go, bo, alpha, name, carry=()):
    t, d = xh.shape
    tm = _row_tile(t)
    nt = t // tm

    def body(xh_ref, gi_ref, bi_ref, win_ref, cw_ref, wout_ref, go_ref, bo_ref,
             xo_ref, rs_ref, hb_ref, p_ref, m_ref, halo):
        i = pl.program_id(0)

        @pl.when(i == 0)
        def _():
            halo[...] = jnp.zeros_like(halo)

        h = xh_ref[...] * gi_ref[...] + bi_ref[...]
        hb = h.astype(BF16)
        bg = _dot(hb, win_ref[:, 0:d])
        cg = _dot(hb, win_ref[:, d:2 * d])
        val = _dot(hb, win_ref[:, 2 * d:3 * d])
        p_ref[:, 0:d] = bg.astype(BF16)
        p_ref[:, d:2 * d] = cg.astype(BF16)
        p_ref[:, 2 * d:3 * d] = val.astype(BF16)
        rows = i * tm + lax.broadcasted_iota(jnp.int32, (tm, 1), 0)
        u = jnp.where(rows >= PAD, cg * val, 0.0)
        u1, u2 = _shift_rows(u, halo[...], tm)
        halo[...] = u[tm - 8:tm]
        y = cw_ref[0:1] * u2 + cw_ref[1:2] * u1 + cw_ref[2:3] * u
        mb = (bg * y).astype(BF16)
        m_ref[...] = mb.T
        xhat, rstd = _ln_fwd(alpha * h + _dot(mb, wout_ref[...]))
        xo_ref[...] = xhat
        rs_ref[...] = rstd
        hb_ref[...] = (xhat * go_ref[...] + bo_ref[...]).astype(BF16).T

    row = pl.BlockSpec((tm, d), lambda i: (i, 0))
    col = pl.BlockSpec((d, tm), lambda i: (0, i))
    vec = pl.BlockSpec((1, d), lambda i: (0, 0))
    first = lambda: pl.program_id(0) == 0
    last = lambda: pl.program_id(0) == nt - 1
    return pl.pallas_call(
        _carried(body, 8, 5, carry, first, last), name=name, grid=(nt,),
        in_specs=[row, vec, vec, pl.BlockSpec((d, 3 * d), lambda i: (0, 0)),
                  pl.BlockSpec((3, d), lambda i: (0, 0)), pl.BlockSpec((d, d), lambda i: (0, 0)),
                  vec, vec] + [ANY] * len(carry),
        out_specs=[row, pl.BlockSpec((tm, 1), lambda i: (i, 0)), col,
                   pl.BlockSpec((tm, 3 * d), lambda i: (i, 0)), col] + [ANY] * len(carry),
        out_shape=[jax.ShapeDtypeStruct((t, d), F32), jax.ShapeDtypeStruct((t, 1), F32),
                   jax.ShapeDtypeStruct((d, t), BF16), jax.ShapeDtypeStruct((t, 3 * d), BF16),
                   jax.ShapeDtypeStruct((d, t), BF16)] + _carry_shapes(carry),
        scratch_shapes=[pltpu.VMEM((8, d), F32)] + _carry_scratch(carry),
        compiler_params=_params(("arbitrary",), VMEM_BIG),
    )(xh, gi, bi, w_in, cw, w_out, go, bo, *[a for _, a in carry])


def _conv_bwd(dh, xo, rs, go, p, cw, w_in, w_out, alpha, name):
    t, d = dh.shape
    tm = _row_tile(t)
    nt = t // tm
    tb = tm // 8

    def body(dh_ref, xo_ref, rs_ref, go_ref, p_ref, ph_ref, cw_ref, win_ref, wout_ref,
             dhin_ref, dmix_ref, dp_ref, dcw_ref, dgain_ref, dbias_ref, carry):
        i = pl.program_id(0)
        tile = nt - 1 - i

        @pl.when(i == 0)
        def _():
            carry[...] = jnp.zeros_like(carry)
            dcw_ref[...] = jnp.zeros_like(dcw_ref)
            dgain_ref[...] = jnp.zeros_like(dgain_ref)
            dbias_ref[...] = jnp.zeros_like(dbias_ref)

        dz, dgp, dbp = _ln_bwd(dh_ref[...], xo_ref[...], rs_ref[...], go_ref[...])
        dgain_ref[...] += dgp
        dbias_ref[...] += dbp
        dmixb = dz.astype(BF16)
        dmix_ref[...] = dmixb
        dm = _dot_nt(dmixb, wout_ref[...])

        bg = p_ref[:, 0:d].astype(F32)
        cg = p_ref[:, d:2 * d].astype(F32)
        val = p_ref[:, 2 * d:3 * d].astype(F32)
        rows = tile * tm + lax.broadcasted_iota(jnp.int32, (tm, 1), 0)
        valid = rows >= PAD
        u = jnp.where(valid, cg * val, 0.0)
        hrows = tile * tm - 8 + lax.broadcasted_iota(jnp.int32, (8, 1), 0)
        hu = jnp.where((hrows >= PAD) & (tile > 0),
                       ph_ref[:, d:2 * d].astype(F32) * ph_ref[:, 2 * d:3 * d].astype(F32), 0.0)
        u1, u2 = _shift_rows(u, hu, tm)
        w0, w1, w2 = cw_ref[0:1], cw_ref[1:2], cw_ref[2:3]
        y = w0 * u2 + w1 * u1 + w2 * u
        dbg = dm * y
        dy = dm * bg
        dcw_ref[0:1] += jnp.sum(dy * u2, axis=0, keepdims=True)
        dcw_ref[1:2] += jnp.sum(dy * u1, axis=0, keepdims=True)
        dcw_ref[2:3] += jnp.sum(dy * u, axis=0, keepdims=True)

        nxt = carry[...]
        r = lax.broadcasted_iota(jnp.int32, (tm, 1), 0)
        dy1 = jnp.where(r == tm - 1, nxt[0:1], pltpu.roll(dy, tm - 1, 0))
        dy2 = jnp.where(r == tm - 2, nxt[0:1],
                        jnp.where(r == tm - 1, nxt[1:2], pltpu.roll(dy, tm - 2, 0)))
        carry[...] = dy[0:8]
        du = jnp.where(valid, w2 * dy + w1 * dy1 + w0 * dy2, 0.0)
        dbgb = dbg.astype(BF16)
        dcgb = (du * val).astype(BF16)
        dvalb = (du * cg).astype(BF16)
        dp_ref[:, 0:d] = dbgb
        dp_ref[:, d:2 * d] = dcgb
        dp_ref[:, 2 * d:3 * d] = dvalb
        dhin_ref[...] = (alpha * dz + _dot_nt(dbgb, win_ref[:, 0:d])
                         + _dot_nt(dcgb, win_ref[:, d:2 * d]) + _dot_nt(dvalb, win_ref[:, 2 * d:3 * d]))

    row = pl.BlockSpec((tm, d), lambda i: (nt - 1 - i, 0))
    vec = pl.BlockSpec((1, d), lambda i: (0, 0))
    prow = pl.BlockSpec((tm, 3 * d), lambda i: (nt - 1 - i, 0))
    return pl.pallas_call(
        body, name=name, grid=(nt,),
        in_specs=[row, row, pl.BlockSpec((tm, 1), lambda i: (nt - 1 - i, 0)), vec, prow,
                  pl.BlockSpec((8, 3 * d), lambda i: (jnp.maximum((nt - 1 - i) * tb - 1, 0), 0)),
                  pl.BlockSpec((3, d), lambda i: (0, 0)),
                  pl.BlockSpec((d, 3 * d), lambda i: (0, 0)), pl.BlockSpec((d, d), lambda i: (0, 0))],
        out_specs=[row, row, prow, pl.BlockSpec((3, d), lambda i: (0, 0)), vec, vec],
        out_shape=[jax.ShapeDtypeStruct((t, d), F32), jax.ShapeDtypeStruct((t, d), BF16),
                   jax.ShapeDtypeStruct((t, 3 * d), BF16), jax.ShapeDtypeStruct((3, d), F32),
                   jax.ShapeDtypeStruct((1, d), F32), jax.ShapeDtypeStruct((1, d), F32)],
        scratch_shapes=[pltpu.VMEM((8, d), F32)],
        compiler_params=_params(("arbitrary",), VMEM_BIG),
    )(dh, xo, rs, go, p, p, cw, w_in, w_out)


def _kv_fwd(xh, gi, bi, wk, wv, wf, fb, name, carry=()):
    t, d = xh.shape
    tm = _row_tile(t)
    nt = t // tm

    def body(xh_ref, gi_ref, bi_ref, wk_ref, wv_ref, wf_ref, fb_ref,
             k_ref, v_ref, lg_ref, c_ref, ct_ref, run):
        i = pl.program_id(0)

        @pl.when(i == 0)
        def _():
            run[...] = jnp.zeros_like(run)

        x = (xh_ref[...] * gi_ref[...] + bi_ref[...]).astype(BF16)
        k_ref[...] = _dot(x, wk_ref[...]).astype(BF16)
        v_ref[...] = _dot(x, wv_ref[...]).astype(BF16)
        logit = _dot(x, wf_ref[...]) + fb_ref[...]
        lg_ref[...] = logit
        logf = jnp.minimum(logit, 0.0) - jnp.log(1.0 + jnp.exp(-jnp.abs(logit)))
        rows = i * tm + lax.broadcasted_iota(jnp.int32, (tm, 1), 0)
        logf = jnp.where(rows >= PAD, logf, 0.0)
        tri = (lax.broadcasted_iota(jnp.int32, (tm, tm), 0)
               >= lax.broadcasted_iota(jnp.int32, (tm, tm), 1)).astype(F32)
        cs = jnp.dot(tri, logf, precision=lax.Precision.HIGHEST, preferred_element_type=F32) + run[...]
        run[...] = cs[tm - 1:tm]
        c_ref[...] = cs
        ct_ref[...] = cs.T

    row = pl.BlockSpec((tm, d), lambda i: (i, 0))
    vec = pl.BlockSpec((1, d), lambda i: (0, 0))
    gate = pl.BlockSpec((tm, LANES), lambda i: (i, 0))
    sq = pl.BlockSpec((d, d), lambda i: (0, 0))
    first = lambda: pl.program_id(0) == 0
    last = lambda: pl.program_id(0) == nt - 1
    return pl.pallas_call(
        _carried(body, 7, 5, carry, first, last), name=name, grid=(nt,),
        in_specs=[row, vec, vec, sq, sq, pl.BlockSpec((d, LANES), lambda i: (0, 0)),
                  pl.BlockSpec((1, LANES), lambda i: (0, 0))] + [ANY] * len(carry),
        out_specs=[row, row, gate, gate, pl.BlockSpec((LANES, tm), lambda i: (0, i))] + [ANY] * len(carry),
        out_shape=[jax.ShapeDtypeStruct((t, d), BF16), jax.ShapeDtypeStruct((t, d), BF16),
                   jax.ShapeDtypeStruct((t, LANES), F32), jax.ShapeDtypeStruct((t, LANES), F32),
                   jax.ShapeDtypeStruct((LANES, t), F32)] + _carry_shapes(carry),
        scratch_shapes=[pltpu.VMEM((1, LANES), F32)] + _carry_scratch(carry),
        compiler_params=_params(("arbitrary",), VMEM_MID),
    )(xh, gi, bi, wk, wv, wf, fb, *[a for _, a in carry])


def _kv_bwd(dk, dv, dcs, dcq, logit, dh_other, wk, wv, wf, name):
    t, d = dk.shape
    tm = _row_tile(t)
    nt = t // tm

    def body(dk_ref, dv_ref, dcs_ref, dcq_ref, lg_ref, oth_ref, wk_ref, wv_ref, wf_ref,
             dh_ref, dl_ref, dfb_ref, run):
        i = pl.program_id(0)
        tile = nt - 1 - i

        @pl.when(i == 0)
        def _():
            run[...] = jnp.zeros_like(run)
            dfb_ref[...] = jnp.zeros_like(dfb_ref)

        lane = lax.broadcasted_iota(jnp.int32, (tm, LANES), 1)
        dc = dcq_ref[...]
        for hh in range(N_HEADS):
            dc = dc + jnp.where(lane == hh, jnp.sum(dcs_ref[hh], axis=1, keepdims=True), 0.0)
        tri = (lax.broadcasted_iota(jnp.int32, (tm, tm), 0)
               <= lax.broadcasted_iota(jnp.int32, (tm, tm), 1)).astype(F32)
        dlf = jnp.dot(tri, dc, precision=lax.Precision.HIGHEST, preferred_element_type=F32) + run[...]
        run[...] = dlf[0:1]
        rows = tile * tm + lax.broadcasted_iota(jnp.int32, (tm, 1), 0)
        dlogit = jnp.where(rows >= PAD, dlf * jax.nn.sigmoid(-lg_ref[...]), 0.0)
        dfb_ref[...] += jnp.sum(dlogit, axis=0, keepdims=True)
        dlb = dlogit.astype(BF16)
        dl_ref[...] = dlb
        dh_ref[...] = (oth_ref[...] + _dot_nt(dk_ref[...], wk_ref[...])
                       + _dot_nt(dv_ref[...], wv_ref[...]) + _dot_nt(dlb, wf_ref[...]))

    row = pl.BlockSpec((tm, d), lambda i: (nt - 1 - i, 0))
    gate = pl.BlockSpec((tm, LANES), lambda i: (nt - 1 - i, 0))
    sq = pl.BlockSpec((d, d), lambda i: (0, 0))
    return pl.pallas_call(
        body, name=name, grid=(nt,),
        in_specs=[row, row, pl.BlockSpec((N_HEADS, tm, LANES), lambda i: (0, nt - 1 - i, 0)), gate, gate, row,
                  sq, sq, pl.BlockSpec((d, LANES), lambda i: (0, 0))],
        out_specs=[row, gate, pl.BlockSpec((1, LANES), lambda i: (0, 0))],
        out_shape=[jax.ShapeDtypeStruct((t, d), F32), jax.ShapeDtypeStruct((t, LANES), BF16),
                   jax.ShapeDtypeStruct((1, LANES), F32)],
        scratch_shapes=[pltpu.VMEM((1, LANES), F32)],
        compiler_params=_params(("arbitrary",), VMEM_MID),
    )(dk, dv, dcs, dcq, logit, dh_other, wk, wv, wf)


def _proj(xh, gi, bi, w, name):
    t, k = xh.shape
    n = w.shape[1]
    tm = _row_tile(t)

    def body(x_ref, g_ref, b_ref, w_ref, o_ref):
        x = (x_ref[...] * g_ref[...] + b_ref[...]).astype(BF16)
        o_ref[...] = _dot(x, w_ref[...]).astype(BF16)

    vec = pl.BlockSpec((1, k), lambda i: (0, 0))
    return pl.pallas_call(
        body, name=name, grid=(t // tm,),
        in_specs=[pl.BlockSpec((tm, k), lambda i: (i, 0)), vec, vec, pl.BlockSpec((k, n), lambda i: (0, 0))],
        out_specs=pl.BlockSpec((tm, n), lambda i: (i, 0)),
        out_shape=jax.ShapeDtypeStruct((t, n), BF16),
        compiler_params=_params(("arbitrary",), VMEM_MID),
    )(xh, gi, bi, w)


def _add_proj_nt(base, y, w, name):
    t, n = y.shape
    k = w.shape[0]
    tm = _row_tile(t)

    def body(b_ref, y_ref, w_ref, o_ref):
        o_ref[...] = b_ref[...] + _dot_nt(y_ref[...].astype(BF16), w_ref[...])

    return pl.pallas_call(
        body, name=name, grid=(t // tm,),
        in_specs=[pl.BlockSpec((tm, k), lambda i: (i, 0)), pl.BlockSpec((tm, n), lambda i: (i, 0)),
                  pl.BlockSpec((k, n), lambda i: (0, 0))],
        out_specs=pl.BlockSpec((tm, k), lambda i: (i, 0)),
        out_shape=jax.ShapeDtypeStruct((t, k), F32),
        compiler_params=_params(("arbitrary",), VMEM_MID),
    )(base, y, w)


def _attn_out_fwd(ot, xh, gi, bi, w_o, go, bo, alpha, name):
    t, d = xh.shape
    tm = _row_tile(t)

    def body(ot_ref, xh_ref, gi_ref, bi_ref, wo_ref, go_ref, bo_ref, xo_ref, rs_ref, hb_ref):
        h = xh_ref[...] * gi_ref[...] + bi_ref[...]
        xhat, rstd = _ln_fwd(alpha * h + _dot_tn(ot_ref[...], wo_ref[...]))
        xo_ref[...] = xhat
        rs_ref[...] = rstd
        hb_ref[...] = (xhat * go_ref[...] + bo_ref[...]).astype(BF16).T

    row = pl.BlockSpec((tm, d), lambda i: (i, 0))
    col = pl.BlockSpec((d, tm), lambda i: (0, i))
    vec = pl.BlockSpec((1, d), lambda i: (0, 0))
    return pl.pallas_call(
        body, name=name, grid=(t // tm,),
        in_specs=[col, row, vec, vec, pl.BlockSpec((d, d), lambda i: (0, 0)), vec, vec],
        out_specs=[row, pl.BlockSpec((tm, 1), lambda i: (i, 0)), col],
        out_shape=[jax.ShapeDtypeStruct((t, d), F32), jax.ShapeDtypeStruct((t, 1), F32),
                   jax.ShapeDtypeStruct((d, t), BF16)],
        compiler_params=_params(("arbitrary",), VMEM_MID),
    )(ot, xh, gi, bi, w_o, go, bo)


def _attn_out_bwd(dh, xo, rs, go, ot, w_o, alpha, name):
    t, d = dh.shape
    tm = _row_tile(t)
    hd = d // N_HEADS

    def body(dh_ref, xo_ref, rs_ref, go_ref, ot_ref, wo_ref,
             dres_ref, dmix_ref, dot_ref, delta_ref, dgain_ref, dbias_ref):
        @pl.when(pl.program_id(0) == 0)
        def _():
            dgain_ref[...] = jnp.zeros_like(dgain_ref)
            dbias_ref[...] = jnp.zeros_like(dbias_ref)

        dz, dgp, dbp = _ln_bwd(dh_ref[...], xo_ref[...], rs_ref[...], go_ref[...])
        dgain_ref[...] += dgp
        dbias_ref[...] += dbp
        dres_ref[...] = alpha * dz
        dmixb = dz.astype(BF16)
        dmix_ref[...] = dmixb
        dot_t = _dot_nt(wo_ref[...], dmixb)
        dot_ref[...] = dot_t.astype(BF16)
        prod = dot_t * ot_ref[...].astype(F32)
        delta_ref[...] = jnp.sum(prod.reshape(N_HEADS, hd, tm), axis=1)

    row = pl.BlockSpec((tm, d), lambda i: (i, 0))
    vec = pl.BlockSpec((1, d), lambda i: (0, 0))
    col = pl.BlockSpec((d, tm), lambda i: (0, i))
    return pl.pallas_call(
        body, name=name, grid=(t // tm,),
        in_specs=[row, row, pl.BlockSpec((tm, 1), lambda i: (i, 0)), vec, col,
                  pl.BlockSpec((d, d), lambda i: (0, 0))],
        out_specs=[row, row, col, pl.BlockSpec((N_HEADS, tm), lambda i: (0, i)), vec, vec],
        out_shape=[jax.ShapeDtypeStruct((t, d), F32), jax.ShapeDtypeStruct((t, d), BF16),
                   jax.ShapeDtypeStruct((d, t), BF16), jax.ShapeDtypeStruct((N_HEADS, t), F32),
                   jax.ShapeDtypeStruct((1, d), F32), jax.ShapeDtypeStruct((1, d), F32)],
        compiler_params=_params(("arbitrary",), VMEM_MID),
    )(dh, xo, rs, go, ot, w_o)


def _scores_t(k, q, ct_ref, c_ref, h, i, j, tq, tk, scale, masked):
    sub = lax.broadcasted_iota(jnp.int32, (8, tq), 0)
    cq = jnp.sum(jnp.where(sub == h, ct_ref[...], 0.0), axis=0, keepdims=True) * LOG2E
    lane = lax.broadcasted_iota(jnp.int32, (tk, LANES), 1)
    ck = jnp.sum(jnp.where(lane == h, c_ref[...], 0.0), axis=1, keepdims=True) * LOG2E
    st = _dot_nt(k, q) * (scale * LOG2E) - ck
    if masked:
        kpos = j * tk + lax.broadcasted_iota(jnp.int32, (tk, 1), 0)
        qpos = i * tq + lax.broadcasted_iota(jnp.int32, (1, tq), 1)
        st = jnp.where((kpos <= qpos) & (kpos >= PAD), st, NEG_INF)
    return st, cq


def _tri_pairs(n, by_row):
    if by_row:
        pairs = [(i, j) for i in range(n) for j in range(i + 1)]
    else:
        pairs = [(i, j) for j in range(n) for i in range(j, n)]
    return (jnp.asarray([p[0] for p in pairs], jnp.int32), jnp.asarray([p[1] for p in pairs], jnp.int32))


def _attn_fwd(q, k, v, c, ct, name, carry=()):
    t, d = q.shape
    hd = d // N_HEADS
    tq = tk = _row_tile(t)
    nq = t // tq
    scale = 1.0 / math.sqrt(hd)

    hps = ATTN_HEADS_PER_STEP

    def body(it_ref, jt_ref, q_ref, k_ref, v_ref, c_ref, ct_ref, ot_ref, lse_ref, m_s, l_s, acc):
        hp, p_ = pl.program_id(0), pl.program_id(1)
        i, j = it_ref[p_], jt_ref[p_]

        @pl.when(j == 0)
        def _():
            m_s[...] = jnp.full_like(m_s, NEG_INF)
            l_s[...] = jnp.zeros_like(l_s)
            acc[...] = jnp.zeros_like(acc)

        def update(masked):
            scores = []
            for e in range(hps):
                cols = slice(e * hd, (e + 1) * hd)
                scores.append(_scores_t(k_ref[:, cols], q_ref[:, cols], ct_ref, c_ref, hp * hps + e,
                                        i, j, tq, tk, scale, masked))
            probs = []
            for e, (st, cq) in enumerate(scores):
                m_new = jnp.maximum(m_s[e], jnp.max(st, axis=0, keepdims=True) + cq)
                a = jnp.exp2(m_s[e] - m_new)
                p = jnp.exp2(st - (m_new - cq))
                l_s[e] = a * l_s[e] + jnp.sum(p, axis=0, keepdims=True)
                m_s[e] = m_new
                probs.append((a, p.astype(BF16)))
            for e, (a, pb) in enumerate(probs):
                acc[e] = a * acc[e] + _dot_tn(v_ref[:, e * hd:(e + 1) * hd], pb)

        edge = (j == i) | (j == 0)
        pl.when(edge)(lambda: update(True))
        pl.when(jnp.logical_not(edge))(lambda: update(False))

        @pl.when(j == i)
        def _():
            for e in range(hps):
                ot_ref[e * hd:(e + 1) * hd, :] = (acc[e] / l_s[e]).astype(BF16)
                lse_ref[e] = m_s[e] + jnp.log2(l_s[e])

    it, jt = _tri_pairs(nq, by_row=True)
    npairs = it.shape[0]
    nhp = N_HEADS // hps
    kv = pl.BlockSpec((tk, hps * hd), lambda h, p, it, jt: (jt[p], h))
    first = lambda: (pl.program_id(0) == 0) & (pl.program_id(1) == 0)
    last = lambda: (pl.program_id(0) == nhp - 1) & (pl.program_id(1) == npairs - 1)
    return pl.pallas_call(
        _carried(body, 7, 2, carry, first, last), name=name,
        grid_spec=pltpu.PrefetchScalarGridSpec(
            num_scalar_prefetch=2, grid=(nhp, npairs),
            in_specs=[pl.BlockSpec((tq, hps * hd), lambda h, p, it, jt: (it[p], h)), kv, kv,
                      pl.BlockSpec((tk, LANES), lambda h, p, it, jt: (jt[p], 0)),
                      pl.BlockSpec((8, tq), lambda h, p, it, jt: (0, it[p]))] + [ANY] * len(carry),
            out_specs=[pl.BlockSpec((hps * hd, tq), lambda h, p, it, jt: (h, it[p])),
                       pl.BlockSpec((hps, 1, tq), lambda h, p, it, jt: (h, 0, it[p]))] + [ANY] * len(carry),
            scratch_shapes=[pltpu.VMEM((hps, 1, tq), F32), pltpu.VMEM((hps, 1, tq), F32),
                            pltpu.VMEM((hps, hd, tq), F32)] + _carry_scratch(carry)),
        out_shape=[jax.ShapeDtypeStruct((d, t), BF16), jax.ShapeDtypeStruct((N_HEADS, 1, t), F32)]
                  + _carry_shapes(carry),
        compiler_params=_params(("arbitrary", "arbitrary"), VMEM_MID),
    )(it, jt, q, k, v, c, ct, *[a for _, a in carry])


def _attn_bwd(q, k, v, c, ct, lse, delta, dot_t, name, carry=()):
    t, d = q.shape
    hd = d // N_HEADS
    tq = tk = _row_tile(t)
    nq = t // tq
    scale = 1.0 / math.sqrt(hd)
    hps = ATTN_BWD_HEADS_PER_STEP

    steps = [(j, i, min(i + 1, nq - 1), int(i + 1 < nq)) for j in range(nq) for i in range(j, nq, 2)]
    jt, ia, ib, vb = (jnp.asarray([s[n] for s in steps], jnp.int32) for n in range(4))

    def body(jt_ref, ia_ref, ib_ref, vb_ref, qa_ref, k_ref, v_ref, c_ref, cta_ref, lsea_ref, deltaa_ref, dota_ref,
             qb_ref, ctb_ref, lseb_ref, deltab_ref, dotb_ref,
             dq_ref, dk_ref, dv_ref, dcs_ref, drow_ref, dk_acc, dv_acc, dc_acc):
        hp, p_ = pl.program_id(0), pl.program_id(1)
        j, i_a, i_b, has_b = jt_ref[p_], ia_ref[p_], ib_ref[p_], vb_ref[p_] == 1

        @pl.when(p_ == 0)
        def _():
            dq_ref[...] = jnp.zeros_like(dq_ref)
            drow_ref[...] = jnp.zeros_like(drow_ref)

        @pl.when(i_a == j)
        def _():
            dk_acc[...] = jnp.zeros_like(dk_acc)
            dv_acc[...] = jnp.zeros_like(dv_acc)
            dc_acc[...] = jnp.zeros_like(dc_acc)

        def update(q_ref, ct_ref, lse_ref, delta_ref, dot_ref, i, masked):
            sub = lax.broadcasted_iota(jnp.int32, (8, tq), 0)
            rows = pl.ds(pl.multiple_of(i * tq, tq), tq)
            stage = []
            for e in range(hps):
                cols = slice(e * hd, (e + 1) * hd)
                st, cq = _scores_t(k_ref[:, cols], q_ref[:, cols], ct_ref, c_ref, hp * hps + e,
                                   i, j, tq, tk, scale, masked)
                dp = _dot(v_ref[:, cols], dot_ref[cols, :])
                stage.append((st, cq, dp))
            grads = []
            for e, (st, cq, dp) in enumerate(stage):
                p = jnp.exp2(st - (lse_ref[e] - cq))
                dl = jnp.sum(jnp.where(sub == hp * hps + e, delta_ref[...], 0.0), axis=0, keepdims=True)
                ds = p * (dp - dl)
                part = ds[:, 0:LANES]
                for g in range(1, tq // LANES):
                    part = part + ds[:, g * LANES:(g + 1) * LANES]
                dc_acc[e] += part
                drow_ref[e, i] += jnp.broadcast_to(jnp.sum(ds, axis=0, keepdims=True), (8, tq))
                grads.append((p.astype(BF16), ds.astype(BF16)))
            for e, (pb, dsb) in enumerate(grads):
                cols = slice(e * hd, (e + 1) * hd)
                dv_acc[e] += _dot_nt(pb, dot_ref[cols, :])
                dk_acc[e] += _dot(dsb, q_ref[:, cols]) * scale
                dq_ref[rows, cols] += _dot_tn(dsb, k_ref[:, cols]) * scale

        slot_a = (qa_ref, cta_ref, lsea_ref, deltaa_ref, dota_ref, i_a)
        slot_b = (qb_ref, ctb_ref, lseb_ref, deltab_ref, dotb_ref, i_b)
        edge_a = (j == i_a) | (j == 0)
        pl.when(edge_a)(lambda: update(*slot_a, True))
        pl.when(jnp.logical_not(edge_a))(lambda: update(*slot_a, False))
        pl.when(has_b & (j == 0))(lambda: update(*slot_b, True))
        pl.when(has_b & (j != 0))(lambda: update(*slot_b, False))

        @pl.when((i_a == nq - 1) | (has_b & (i_b == nq - 1)))
        def _():
            for e in range(hps):
                cols = slice(e * hd, (e + 1) * hd)
                dk_ref[:, cols] = dk_acc[e].astype(BF16)
                dv_ref[:, cols] = dv_acc[e].astype(BF16)
                dcs_ref[e] = -dc_acc[e]

    nsteps = len(steps)
    nhp = N_HEADS // hps
    kv = pl.BlockSpec((tk, hps * hd), lambda h, p, jt, ia, ib, vb: (jt[p], h))

    def q_side(sel):
        return [pl.BlockSpec((tq, hps * hd), lambda h, p, jt, ia, ib, vb: (sel(ia, ib)[p], h)),
                pl.BlockSpec((8, tq), lambda h, p, jt, ia, ib, vb: (0, sel(ia, ib)[p])),
                pl.BlockSpec((hps, 1, tq), lambda h, p, jt, ia, ib, vb: (h, 0, sel(ia, ib)[p])),
                pl.BlockSpec((N_HEADS, tq), lambda h, p, jt, ia, ib, vb: (0, sel(ia, ib)[p])),
                pl.BlockSpec((hps * hd, tq), lambda h, p, jt, ia, ib, vb: (h, sel(ia, ib)[p]))]

    qa_specs, qb_specs = q_side(lambda ia, ib: ia), q_side(lambda ia, ib: ib)
    first = lambda: (pl.program_id(0) == 0) & (pl.program_id(1) == 0)
    last = lambda: (pl.program_id(0) == nhp - 1) & (pl.program_id(1) == nsteps - 1)
    q_args = (q, ct, lse, delta, dot_t)
    return pl.pallas_call(
        _carried(body, 17, 5, carry, first, last), name=name,
        grid_spec=pltpu.PrefetchScalarGridSpec(
            num_scalar_prefetch=4, grid=(nhp, nsteps),
            in_specs=[qa_specs[0], kv, kv, pl.BlockSpec((tk, LANES), lambda h, p, jt, ia, ib, vb: (jt[p], 0))]
                     + qa_specs[1:] + qb_specs + [ANY] * len(carry),
            out_specs=[pl.BlockSpec((t, hps * hd), lambda h, p, jt, ia, ib, vb: (0, h)), kv, kv,
                       pl.BlockSpec((hps, tk, LANES), lambda h, p, jt, ia, ib, vb: (h, jt[p], 0)),
                       pl.BlockSpec((hps, nq, 8, tq), lambda h, p, jt, ia, ib, vb: (h, 0, 0, 0))]
                      + [ANY] * len(carry),
            scratch_shapes=[pltpu.VMEM((hps, tk, hd), F32), pltpu.VMEM((hps, tk, hd), F32),
                            pltpu.VMEM((hps, tk, LANES), F32)] + _carry_scratch(carry)),
        out_shape=[jax.ShapeDtypeStruct((t, d), F32), jax.ShapeDtypeStruct((t, d), BF16),
                   jax.ShapeDtypeStruct((t, d), BF16), jax.ShapeDtypeStruct((N_HEADS, t, LANES), F32),
                   jax.ShapeDtypeStruct((N_HEADS, nq, 8, tq), F32)] + _carry_shapes(carry),
        compiler_params=_params(("arbitrary", "arbitrary"), VMEM_BIG),
    )(jt, ia, ib, vb, q, k, v, c, ct, lse, delta, dot_t, *q_args, *[a for _, a in carry])


def _adamw(w, g, m, v, name):
    r, c = w.shape
    tr = r
    for cand in (256, 128, 64, 32, 16, 8):
        if r % cand == 0 and r > cand:
            tr = cand
            break
    bc1 = 1.0 - ADAM_B1 ** ADAM_STEP
    bc2 = 1.0 - ADAM_B2 ** ADAM_STEP

    def body(w_ref, g_ref, m_ref, v_ref, d_ref, nm_ref, nv_ref):
        gg = g_ref[...]
        nm = ADAM_B1 * m_ref[...] + (1.0 - ADAM_B1) * gg
        nv = ADAM_B2 * v_ref[...] + (1.0 - ADAM_B2) * (gg * gg)
        d_ref[...] = -ADAM_LR * ((nm / bc1) / (jnp.sqrt(nv / bc2) + ADAM_EPS) + ADAM_WD * w_ref[...])
        nm_ref[...] = nm
        nv_ref[...] = nv

    blk = pl.BlockSpec((tr, c), lambda i: (i, 0))
    shp = jax.ShapeDtypeStruct((r, c), F32)
    return pl.pallas_call(
        body, name=name, grid=(r // tr,), in_specs=[blk] * 4, out_specs=[blk] * 3,
        out_shape=[shp] * 3, compiler_params=_params(("arbitrary",), VMEM_MID),
    )(w, g, m, v)


def _reduce_adamw(w, m, v, landed, name):
    nl, r, c = w.shape
    tr = next(cand for cand in range(min(r, ADAM_ROWS_MAX), 0, -BF16_ROWS) if r % cand == 0)
    nr = r // tr
    bc1 = 1.0 - ADAM_B1 ** ADAM_STEP
    bc2 = 1.0 - ADAM_B2 ** ADAM_STEP

    def body(*refs):
        w_ref, m_ref, v_ref = refs[:3]
        src_refs = refs[3:3 + nl]
        g_ref, d_ref, nm_ref, nv_ref = refs[3 + nl:]

        def update(src):
            gg = src[0].astype(F32)
            for s in range(1, N_DEV):
                gg = gg + src[s].astype(F32)
            nm = ADAM_B1 * m_ref[0] + (1.0 - ADAM_B1) * gg
            nv = ADAM_B2 * v_ref[0] + (1.0 - ADAM_B2) * (gg * gg)
            g_ref[0] = gg
            d_ref[0] = -ADAM_LR * ((nm / bc1) / (jnp.sqrt(nv / bc2) + ADAM_EPS) + ADAM_WD * w_ref[0])
            nm_ref[0] = nm
            nv_ref[0] = nv

        for idx in range(nl):
            pl.when(pl.program_id(0) == idx)(functools.partial(update, src_refs[idx]))

    def src_spec(idx):
        return pl.BlockSpec((N_DEV, tr, c),
                            lambda l, i: (0, jnp.where(l == idx, i, jnp.where(l < idx, 0, nr - 1)), 0))

    blk = pl.BlockSpec((1, tr, c), lambda l, i: (l, i, 0))
    shp = jax.ShapeDtypeStruct((nl, r, c), F32)
    return pl.pallas_call(
        body, name=name, grid=(nl, nr), in_specs=[blk] * 3 + [src_spec(idx) for idx in range(nl)],
        out_specs=[blk] * 4, out_shape=[shp] * 4,
        compiler_params=_params(("arbitrary", "arbitrary"), VMEM_MID),
    )(w, m, v, *landed)


def _sum_sources(r, name):
    n, rows, c = r.shape
    tr = next(cand for cand in range(min(rows, SUM_ROWS_MAX), 0, -BF16_ROWS) if rows % cand == 0)

    def body(r_ref, o_ref):
        acc = r_ref[0].astype(F32)
        for s in range(1, n):
            acc = acc + r_ref[s].astype(F32)
        o_ref[...] = acc

    return pl.pallas_call(
        body, name=name, grid=(rows // tr,),
        in_specs=[pl.BlockSpec((n, tr, c), lambda i: (0, i, 0))],
        out_specs=pl.BlockSpec((tr, c), lambda i: (i, 0)),
        out_shape=jax.ShapeDtypeStruct((rows, c), F32),
        compiler_params=_params(("arbitrary",), VMEM_MID),
    )(r)


def _all_gather(parts, name):
    n = len(parts)

    def body(*refs):
        x_refs, out_refs = refs[:n], refs[n:2 * n]
        send_sems, recv_sems, local_sems = refs[2 * n:]
        mx, my, mc = lax.axis_index("x"), lax.axis_index("y"), lax.axis_index("c")
        me, sibling = (mx, my, mc), (mx, my, 1 - mc)
        chips = [(1 - mx, my), (mx, 1 - my), (1 - mx, 1 - my)]

        def copy(p, k, block, to, from_input=False):
            px, py, pc = block
            rows = out_refs[p].at[4 * px + 2 * py + pc]
            return pltpu.make_async_remote_copy(
                src_ref=x_refs[p] if from_input else rows, dst_ref=rows,
                send_sem=send_sems.at[7 * p + k], recv_sem=recv_sems.at[7 * p + k],
                device_id=to, device_id_type=MESH)

        mine, sent = [], []
        for p in range(n):
            own = pltpu.make_async_copy(x_refs[p], out_refs[p].at[4 * mx + 2 * my + mc], local_sems.at[p])
            own.start()
            mine.append(own)
            first = [copy(p, 0, me, sibling, True)]
            first += [copy(p, 1 + j, me, (*chip, mc), True) for j, chip in enumerate(chips)]
            for cp in first:
                cp.start()
            sent += first
        for p in range(n):
            for j, chip in enumerate(chips):
                copy(p, 1 + j, (*chip, mc), me).wait_recv()
                fwd = copy(p, 4 + j, (*chip, mc), sibling)
                fwd.start()
                sent.append(fwd)
        for p in range(n):
            copy(p, 0, sibling, me).wait_recv()
            for j, chip in enumerate(chips):
                copy(p, 4 + j, (*chip, 1 - mc), me).wait_recv()
        for cp in sent:
            cp.wait_send()
        for own in mine:
            own.wait()

    return pl.pallas_call(
        body, name=name, in_specs=[ANY] * n, out_specs=[ANY] * n,
        out_shape=[jax.ShapeDtypeStruct((N_DEV,) + a.shape, a.dtype) for a in parts],
        scratch_shapes=[pltpu.SemaphoreType.DMA((7 * n,)), pltpu.SemaphoreType.DMA((7 * n,)),
                        pltpu.SemaphoreType.DMA((n,))],
    )(*parts)


def _pack_rows(parts, width, mult, lead=0):
    out = []
    for a in parts:
        head = a.shape[:lead]
        flat = a.reshape(head + (-1,))
        padn = (-flat.shape[-1]) % (width * mult)
        if padn:
            flat = jnp.pad(flat, [(0, 0)] * lead + [(0, padn)])
        out.append(flat.reshape(head + (-1, width)))
    return jnp.concatenate(out, axis=lead)


def _rows_of(shape, width, mult):
    n = math.prod(shape)
    per = width * mult
    return ((n + per - 1) // per) * mult


def _unpack_rows(buf, shapes, width, mult):
    lead = buf.shape[:-2]
    out, off = [], 0
    for shp in shapes:
        r = _rows_of(shp, width, mult)
        flat = buf[..., off:off + r, :].reshape(lead + (r * width,))
        out.append(flat[..., :math.prod(shp)].reshape(lead + tuple(shp)))
        off += r
    return out


def _cols_from_devices(g):
    nd = g.ndim
    perm = tuple(range(1, nd - 1)) + (0, nd - 1)
    t = jnp.transpose(g, perm)
    return t.reshape(t.shape[:-2] + (t.shape[-2] * t.shape[-1],))


def _cols_to_devices(a):
    c = a.shape[-1] // N_DEV
    t = a.reshape(a.shape[:-1] + (N_DEV, c))
    nd = t.ndim
    perm = (nd - 2,) + tuple(range(0, nd - 2)) + (nd - 1,)
    return jnp.transpose(t, perm)


WIDTH = 1024


def kernel(x, meta, ffn1_wg, ffn1_wu, ffn1_wd, ffn2_wg, ffn2_wu, ffn2_wd, ln_gain, ln_bias, conv_w_in, conv_w, conv_w_out, kv_w, f_bias, attn_w_q, attn_w_o, loss_target, m_meta, m_ffn1_wg, m_ffn1_wu, m_ffn1_wd, m_ffn2_wg, m_ffn2_wu, m_ffn2_wd, m_ln_gain, m_ln_bias, m_conv_w_in, m_conv_w, m_conv_w_out, m_kv_w, m_f_bias, m_attn_w_q, m_attn_w_o, v_meta, v_ffn1_wg, v_ffn1_wu, v_ffn1_wd, v_ffn2_wg, v_ffn2_wu, v_ffn2_wd, v_ln_gain, v_ln_bias, v_conv_w_in, v_conv_w, v_conv_w_out, v_kv_w, v_f_bias, v_attn_w_q, v_attn_w_o):
    depth = ln_gain.shape[0]
    alpha = float((2 * depth) ** 0.25)
    d = x.shape[-1]
    seq = x.shape[1]
    t = ROW0 + seq
    fsh = ffn1_wg.shape[-1]
    f = fsh * N_DEV
    fck = MXU_COLS
    nc = f // fck
    me = 4 * lax.axis_index("x") + 2 * lax.axis_index("y") + lax.axis_index("c")

    def gather_of(parts):
        return [(True, a.astype(BF16)) for a in parts]

    small = [meta, ln_gain, ln_bias, conv_w]
    small_shapes = [a.shape for a in small]
    g1g, g1u, g1d, gcin, gcout, gsmall = _all_gather(
        [a.astype(BF16) for a in (ffn1_wg[0], ffn1_wu[0], ffn1_wd[0], conv_w_in[0], conv_w_out[0])]
        + [_pack_rows(small, WIDTH, F32_ROWS)], "ag_first")
    gmeta, ggain, gbias, gcw = _unpack_rows(gsmall, small_shapes, WIDTH, F32_ROWS)

    def ffn_chunks(gg, gu, gd):
        up = lambda g: jnp.transpose(_cols_from_devices(g).reshape(d, nc, fck), (1, 0, 2))
        return up(gg), up(gu), gd.reshape(nc, fck, d)

    w_in = _cols_from_devices(gcin)
    w_out = gcout.reshape(d, d)
    fb = jnp.pad(f_bias, (0, LANES - N_HEADS)).reshape(1, LANES)
    meta_f = _cols_from_devices(gmeta)
    gain_f = _cols_from_devices(ggain)
    bias_f = _cols_from_devices(gbias)
    cw_f = _cols_from_devices(gcw)[0]

    def gb(l, n):
        return gain_f[l, n].reshape(1, d), bias_f[l, n].reshape(1, d)

    ones = jnp.ones((1, d), F32)
    zeros = jnp.zeros((1, d), F32)

    h0 = jnp.concatenate([jnp.zeros((PAD, d), F32), meta_f, x[0]], axis=0)

    w1 = ffn_chunks(g1g, g1u, g1d)
    g00, b00 = gb(0, 0)
    xh1, rs1, hb1, gg1, uu1, hb0, g2g, g2u = _ffn_fwd(
        h0, ones, zeros, *w1, g00, b00, alpha, "ffn_fwd_0a", carry=gather_of([ffn2_wg[0], ffn2_wu[0]]),
        input_t=True)
    g01, b01 = gb(0, 1)
    xh2, rs2, hb2, pp, mb, g2d, gkv = _conv_fwd(
        xh1, g00, b00, w_in, cw_f, w_out, g01, b01, alpha, "conv_fwd", carry=gather_of([ffn2_wd[0], kv_w.T]))
    w2 = ffn_chunks(g2g, g2u, g2d)
    g02, b02 = gb(0, 2)
    xh3, rs3, hb3, gg3, uu3, g3g, g3u = _ffn_fwd(
        xh2, g01, b01, *w2, g02, b02, alpha, "ffn_fwd_0b", carry=gather_of([ffn1_wg[1], ffn1_wu[1]]))
    kvw = gkv.reshape(gkv.shape[0] * gkv.shape[1], d).T
    wk, wv = kvw[:, :d], kvw[:, d:2 * d]
    wf = jnp.pad(kvw[:, 2 * d:], ((0, 0), (0, LANES - N_HEADS)))
    kk, vv, logit, cc, cct, g3d = _kv_fwd(xh3, g02, b02, wk, wv, wf, fb, "kv_fwd",
                                          carry=gather_of([ffn1_wd[1]]))

    w3 = ffn_chunks(g3g, g3u, g3d)
    g10, b10 = gb(1, 0)
    xh4, rs4, hb4, gg4, uu4, gwq, g4g, g4u = _ffn_fwd(
        xh3, g02, b02, *w3, g10, b10, alpha, "ffn_fwd_1a", carry=gather_of([attn_w_q[0], ffn2_wg[1], ffn2_wu[1]]))
    w_q = gwq.reshape(d, d)
    qq = _proj(xh4, g10, b10, w_q, "q_proj")
    ot, lse, gwo, g4d = _attn_fwd(
        qq, kk, vv, cc, cct, "attn_fwd", carry=gather_of([attn_w_o[0], ffn2_wd[1]]))
    w_o = gwo.reshape(d, d)
    g11, b11 = gb(1, 1)
    xh5, rs5, hb5 = _attn_out_fwd(ot, xh4, g10, b10, w_o, g11, b11, alpha, "attn_out_fwd")
    w4 = ffn_chunks(g4g, g4u, g4d)
    g12, b12 = gb(1, 2)
    xh6, rs6, _, gg6, uu6 = _ffn_fwd(xh5, g11, b11, *w4, g12, b12, alpha, "ffn_fwd_1b")


    dgain = [[None] * 3 for _ in range(depth)]
    dbias = [[None] * 3 for _ in range(depth)]

    def to_col_owners(g):
        return (False, _cols_to_devices(g).astype(BF16))

    def to_row_owners(g):
        return (False, g.reshape(N_DEV, g.shape[0] // N_DEV, g.shape[1]).astype(BF16))

    dh5, do6, dg6, du6, a6, dgain[1][2], dbias[1][2], loss_l = _ffn_bwd(
        None, xh6, rs6, g12, gg6, uu6, *w4, alpha, "ffn_bwd_1b", loss_target=loss_target[0], loss_bias=b12)
    loss = lax.psum(loss_l[0, 0], ("x", "y", "c"))
    dw4g, dw4u = _wgrad(hb5, [dg6, du6], "wgrad_up_1b")
    (dw4dt,) = _wgrad(do6, [a6], "wgrad_down_1b")

    dres4, dmix5, dot_t, delta, dgain[1][1], dbias[1][1] = _attn_out_bwd(dh5, xh5, rs5, g11, ot, w_o, alpha, "attn_out_bwd")
    (dwo,) = _wgrad(ot, [dmix5], "wgrad_wo")
    dq, dkk, dvv, dcs, drow, l4g, l4u, l4d, lwo = _attn_bwd(
        qq, kk, vv, cc, cct, lse, delta, dot_t, "attn_bwd",
        carry=[to_col_owners(dw4g), to_col_owners(dw4u), to_row_owners(dw4dt.T), to_row_owners(dwo)])
    dh4 = _add_proj_nt(dres4, dq, w_q, "q_bwd")
    (dwq,) = _wgrad(hb4, [dq], "wgrad_wq")

    dh3a, do4, dg4, du4, a4, dgain[1][0], dbias[1][0] = _ffn_bwd(dh4, xh4, rs4, g10, gg4, uu4, *w3, alpha, "ffn_bwd_1a")
    dw3g, dw3u = _wgrad(hb3, [dg4, du4], "wgrad_up_1a")
    (dw3dt,) = _wgrad(do4, [a4], "wgrad_down_1a")

    dcq = jnp.pad(drow[:, :, 0, :].reshape(N_HEADS, t).T, ((0, 0), (0, LANES - N_HEADS)))
    dh3, dlogit, dfb = _kv_bwd(dkk, dvv, dcs, dcq, logit, dh3a, wk, wv, wf, "kv_bwd")
    dwk, dwv = _wgrad(hb3, [dkk, dvv], "wgrad_kv")
    (dwf,) = _wgrad(hb3, [dlogit], "wgrad_f")
    dkv = jnp.concatenate([dwk, dwv, dwf[:, :N_HEADS]], axis=1)

    dh2, do3, dg3, du3, a3, dgain[0][2], dbias[0][2], lwq, l3g, l3u, l3d, lkv = _ffn_bwd(
        dh3, xh3, rs3, g02, gg3, uu3, *w2, alpha, "ffn_bwd_0b",
        carry=[to_row_owners(dwq), to_col_owners(dw3g), to_col_owners(dw3u), to_row_owners(dw3dt.T),
               to_row_owners(dkv.T)])
    dw2g, dw2u = _wgrad(hb2, [dg3, du3], "wgrad_up_0b")
    (dw2dt,) = _wgrad(do3, [a3], "wgrad_down_0b")

    dh1, dmix2, dpp, dcw, dgain[0][1], dbias[0][1] = _conv_bwd(dh2, xh2, rs2, g01, pp, cw_f, w_in, w_out, alpha, "conv_bwd")
    (dwin,) = _wgrad(hb1, [dpp], "wgrad_conv_in")
    (dwout,) = _wgrad(mb, [dmix2], "wgrad_conv_out")

    dh0, do1, dg1, du1, a1, dgain[0][0], dbias[0][0], l2g, l2u, l2d, lcin, lcout = _ffn_bwd(
        dh1, xh1, rs1, g00, gg1, uu1, *w1, alpha, "ffn_bwd_0a",
        carry=[to_col_owners(dw2g), to_col_owners(dw2u), to_row_owners(dw2dt.T), to_col_owners(dwin),
               to_row_owners(dwout)])
    (dw1dt,) = _wgrad(do1, [a1], "wgrad_down_0a")
    dw1g, l1d = _wgrad(hb0, [dg1], "wgrad_upg_0a", carry=[to_row_owners(dw1dt.T)])
    dw1u, l1g = _wgrad(hb0, [du1], "wgrad_upu_0a", carry=[to_col_owners(dw1g)])
    dmeta = dh0[PAD:ROW0]
    dgain_f = jnp.stack([jnp.concatenate(r, axis=0) for r in dgain])
    dbias_f = jnp.stack([jnp.concatenate(r, axis=0) for r in dbias])
    small_full = [dmeta, dgain_f, dbias_f, dcw[None], dfb]
    small_full_shapes = [a.shape for a in small_full]
    l1u, gsmall_grads = _exchange([to_col_owners(dw1u), (True, _pack_rows(small_full, WIDTH, F32_ROWS))], "rs_last")

    grad_x = dh0[ROW0:].reshape(1, seq, d)
    rsmall = _sum_sources(gsmall_grads, "small_sum")
    smeta, sgain, sbias, scw, sfb = _unpack_rows(rsmall, small_full_shapes, WIDTH, F32_ROWS)
    csh = d // N_DEV

    def my_cols(a):
        return lax.dynamic_slice_in_dim(a, me * csh, csh, axis=a.ndim - 1)

    grads = {"meta": my_cols(smeta), "ln_gain": my_cols(sgain), "ln_bias": my_cols(sbias),
             "conv_w": my_cols(scw), "f_bias": sfb[0, :N_HEADS], "kv_w": _sum_sources(lkv, "kv_sum").T}
    landed = {"ffn1_wg": [l1g, l3g], "ffn1_wu": [l1u, l3u], "ffn1_wd": [l1d, l3d],
              "ffn2_wg": [l2g, l4g], "ffn2_wu": [l2u, l4u], "ffn2_wd": [l2d, l4d],
              "conv_w_in": [lcin], "conv_w_out": [lcout], "attn_w_q": [lwq], "attn_w_o": [lwo]}
    weights = dict(meta=meta, ffn1_wg=ffn1_wg, ffn1_wu=ffn1_wu, ffn1_wd=ffn1_wd, ffn2_wg=ffn2_wg,
                   ffn2_wu=ffn2_wu, ffn2_wd=ffn2_wd, ln_gain=ln_gain, ln_bias=ln_bias,
                   conv_w_in=conv_w_in, conv_w=conv_w, conv_w_out=conv_w_out, kv_w=kv_w,
                   f_bias=f_bias, attn_w_q=attn_w_q, attn_w_o=attn_w_o)
    moms = dict(meta=(m_meta, v_meta), ffn1_wg=(m_ffn1_wg, v_ffn1_wg), ffn1_wu=(m_ffn1_wu, v_ffn1_wu),
                ffn1_wd=(m_ffn1_wd, v_ffn1_wd), ffn2_wg=(m_ffn2_wg, v_ffn2_wg), ffn2_wu=(m_ffn2_wu, v_ffn2_wu),
                ffn2_wd=(m_ffn2_wd, v_ffn2_wd), ln_gain=(m_ln_gain, v_ln_gain), ln_bias=(m_ln_bias, v_ln_bias),
                conv_w_in=(m_conv_w_in, v_conv_w_in), conv_w=(m_conv_w, v_conv_w),
                conv_w_out=(m_conv_w_out, v_conv_w_out), kv_w=(m_kv_w, v_kv_w), f_bias=(m_f_bias, v_f_bias),
                attn_w_q=(m_attn_w_q, v_attn_w_q), attn_w_o=(m_attn_w_o, v_attn_w_o))

    names = list(weights)
    g_out, d_out, m_out, v_out = [], [], [], []
    for n in names:
        w = weights[n]
        shp = w.shape
        mm, vv_ = moms[n]
        if n in landed:
            three = (len(landed[n]),) + shp[-2:]
            g, dl, nm, nv = _reduce_adamw(w.reshape(three), mm.reshape(three), vv_.reshape(three),
                                          landed[n], "adamw_" + n)
            g = g.reshape(shp)
        else:
            two = (1, shp[0]) if w.ndim == 1 else (math.prod(shp[:-1]), shp[-1])
            g = grads[n].reshape(shp)
            dl, nm, nv = _adamw(w.reshape(two), g.reshape(two), mm.reshape(two), vv_.reshape(two), "adamw_" + n)
        g_out.append(g)
        d_out.append(dl.reshape(shp))
        m_out.append(nm.reshape(shp))
        v_out.append(nv.reshape(shp))
    return (loss, grad_x, *g_out, *d_out, *m_out, *v_out)
```

```python
import functools
import math

import jax
import jax.numpy as jnp
from jax import lax
from jax.experimental import pallas as pl
from jax.experimental.pallas import tpu as pltpu

F32 = jnp.float32
BF16 = jnp.bfloat16

N_DEV = 8
N_HEADS = 8
N_META = 16
PAD = 112
ROW0 = PAD + N_META
LN_EPS = 1e-5
NEG_INF = -1e30
LOG2E = 1.4426950408889634
ATTN_HEADS_PER_STEP = 8
ATTN_BWD_HEADS_PER_STEP = 2
LANES = 128
MXU_COLS = 256
FFN_FWD_CHUNKS = 11
FFN_BWD_CHUNKS = 4

ADAM_LR = 0.001
ADAM_B1 = 0.9
ADAM_B2 = 0.999
ADAM_EPS = 1e-08
ADAM_WD = 0.01
ADAM_STEP = 10

ROW_TILES = (640, 128)
LOSS_TILE = 128
BF16_ROWS = 16
F32_ROWS = 8
SUM_ROWS_MAX = 768
ADAM_ROWS_MAX = 256
VMEM_BIG = 56 << 20
VMEM_MID = 40 << 20

ANY = pl.BlockSpec(memory_space=pl.ANY)
MESH = pl.DeviceIdType.MESH


def _row_tile(t):
    for c in ROW_TILES:
        if t % c == 0:
            return c
    raise ValueError(f"no row tile for {t}")


def _dot(a, b):
    return jnp.dot(a, b, preferred_element_type=F32)


def _dot_nt(a, b):
    return lax.dot_general(a, b, (((1,), (1,)), ((), ())), preferred_element_type=F32)


def _dot_tn(a, b):
    return lax.dot_general(a, b, (((0,), (0,)), ((), ())), preferred_element_type=F32)


def _params(sem, vmem):
    return pltpu.CompilerParams(dimension_semantics=sem, vmem_limit_bytes=vmem)


def _ln_fwd(z):
    mu = jnp.mean(z, axis=-1, keepdims=True)
    zc = z - mu
    var = jnp.mean(zc * zc, axis=-1, keepdims=True)
    rstd = lax.rsqrt(var + LN_EPS)
    return zc * rstd, rstd


def _ln_bwd(dh, xhat, rstd, gain):
    dxh = dh * gain
    m1 = jnp.mean(dxh, axis=-1, keepdims=True)
    m2 = jnp.mean(dxh * xhat, axis=-1, keepdims=True)
    dz = rstd * (dxh - m1 - xhat * m2)
    return dz, jnp.sum(dh * xhat, axis=0, keepdims=True), jnp.sum(dh, axis=0, keepdims=True)


def _load_resident(pairs, sems):
    cps = [pltpu.make_async_copy(src, dst, sems.at[k]) for k, (src, dst) in enumerate(pairs)]
    for cp in cps:
        cp.start()
    for cp in cps:
        cp.wait()


def _peer_ids():
    mx, my, mc = lax.axis_index("x"), lax.axis_index("y"), lax.axis_index("c")
    peers = []
    for kk in range(1, N_DEV):
        px = 1 - mx if (kk >> 2) & 1 else mx
        py = 1 - my if (kk >> 1) & 1 else my
        pc = 1 - mc if kk & 1 else mc
        peers.append(((px, py, pc), 4 * px + 2 * py + pc))
    return 4 * mx + 2 * my + mc, peers


def _exchange_copies(jobs, send_sems, recv_sems, local_sems, starting):
    me_id, peers = _peer_ids()
    for n, (gather, src, dst) in enumerate(jobs):
        own = pltpu.make_async_copy(src if gather else src.at[me_id], dst.at[me_id], local_sems.at[n])
        own.start() if starting else own.wait()
        for k, (dev, pid) in enumerate(peers):
            sem = (N_DEV - 1) * n + k
            out = src if gather else src.at[pid]
            send = pltpu.make_async_remote_copy(
                src_ref=out, dst_ref=dst.at[me_id], send_sem=send_sems.at[sem], recv_sem=recv_sems.at[sem],
                device_id=dev, device_id_type=MESH)
            if starting:
                send.start()
            else:
                pltpu.make_async_remote_copy(
                    src_ref=out, dst_ref=dst.at[pid], send_sem=send_sems.at[sem], recv_sem=recv_sems.at[sem],
                    device_id=dev, device_id_type=MESH).wait_recv()
                send.wait_send()


def _carried(body, n_in, n_out, carry, first, last):
    nj = len(carry)
    if nj == 0:
        return body

    def wrapped(*refs):
        ins, srcs = refs[:n_in], refs[n_in:n_in + nj]
        outs = refs[n_in + nj:n_in + nj + n_out]
        dsts = refs[n_in + nj + n_out:n_in + 2 * nj + n_out]
        scratch, sems = refs[n_in + 2 * nj + n_out:-3], refs[-3:]
        jobs = [(g, s, r) for (g, _), s, r in zip(carry, srcs, dsts)]

        @pl.when(first())
        def _():
            _exchange_copies(jobs, *sems, starting=True)

        body(*ins, *outs, *scratch)

        @pl.when(last())
        def _():
            _exchange_copies(jobs, *sems, starting=False)

    return wrapped


def _carry_shapes(carry):
    return [jax.ShapeDtypeStruct((N_DEV,) + a.shape if g else a.shape, a.dtype) for g, a in carry]


def _carry_scratch(carry):
    if not carry:
        return []
    n = len(carry)
    return [pltpu.SemaphoreType.DMA(((N_DEV - 1) * n,)), pltpu.SemaphoreType.DMA(((N_DEV - 1) * n,)),
            pltpu.SemaphoreType.DMA((n,))]


def _exchange(carry, name):
    n = len(carry)

    def body(*refs):
        jobs = [(g, s, r) for (g, _), s, r in zip(carry, refs[:n], refs[n:2 * n])]
        _exchange_copies(jobs, *refs[2 * n:], starting=True)
        _exchange_copies(jobs, *refs[2 * n:], starting=False)

    return pl.pallas_call(
        body, name=name, in_specs=[ANY] * n, out_specs=[ANY] * n, out_shape=_carry_shapes(carry),
        scratch_shapes=_carry_scratch(carry),
    )(*[a for _, a in carry])


def _ffn_fwd(xh, gi, bi, wg, wu, wd, go, bo, alpha, name, carry=(), input_t=False):
    t, d = xh.shape
    nch, _, fc = wg.shape
    f = nch * fc
    per = min(FFN_FWD_CHUNKS, nch)
    nc = -(-nch // per)
    tm = _row_tile(t)
    nt = t // tm

    def body(xh_ref, gi_ref, bi_ref, wg_hbm, wu_hbm, wd_hbm, go_ref, bo_ref,
             xo_ref, rs_ref, hb_ref, g_ref, u_ref, *tail):
        hin_ref = tail[0] if input_t else None
        wg_v, wu_v, wd_v, acc, hbs, sems = tail[1:] if input_t else tail
        i = pl.program_id(0)
        c = pl.program_id(1)

        @pl.when((i == 0) & (c == 0))
        def _():
            _load_resident([(wg_hbm, wg_v), (wu_hbm, wu_v), (wd_hbm, wd_v)], sems)

        @pl.when(c == 0)
        def _():
            h = xh_ref[...] * gi_ref[...] + bi_ref[...]
            hbs[...] = h.astype(BF16)
            acc[...] = jnp.zeros_like(acc)
            if input_t:
                hin_ref[...] = hbs[...].T

        def chunk(k):
            ck = c * per + k
            cols = slice(k * fc, (k + 1) * fc)
            hb = hbs[...]
            g = _dot(hb, wg_v[ck])
            u = _dot(hb, wu_v[ck])
            a = (g * jax.nn.sigmoid(g)) * u
            g_ref[:, cols] = g.astype(BF16)
            u_ref[:, cols] = u.astype(BF16)
            acc[...] += _dot(a.astype(BF16), wd_v[ck])

        for k in range(per):
            if (nc - 1) * per + k < nch:
                chunk(k)
            else:
                pl.when(c * per + k < nch)(functools.partial(chunk, k))

        @pl.when(c == nc - 1)
        def _():
            h = xh_ref[...] * gi_ref[...] + bi_ref[...]
            xhat, rstd = _ln_fwd(alpha * h + 0.5 * acc[...])
            xo_ref[...] = xhat
            rs_ref[...] = rstd
            hb_ref[...] = (xhat * go_ref[...] + bo_ref[...]).astype(BF16).T

    row = pl.BlockSpec((tm, d), lambda i, c: (i, 0))
    vec = pl.BlockSpec((1, d), lambda i, c: (0, 0))
    chunk = pl.BlockSpec((tm, per * fc), lambda i, c: (i, c))
    first = lambda: (pl.program_id(0) == 0) & (pl.program_id(1) == 0)
    last = lambda: (pl.program_id(0) == nt - 1) & (pl.program_id(1) == nc - 1)
    col = pl.BlockSpec((d, tm), lambda i, c: (0, i))
    t_spec, t_shape = ([col], [jax.ShapeDtypeStruct((d, t), BF16)]) if input_t else ([], [])
    return pl.pallas_call(
        _carried(body, 8, 5 + len(t_spec), carry, first, last), name=name, grid=(nt, nc),
        in_specs=[row, vec, vec, ANY, ANY, ANY, vec, vec] + [ANY] * len(carry),
        out_specs=[row, pl.BlockSpec((tm, 1), lambda i, c: (i, 0)), col, chunk, chunk] + t_spec
                  + [ANY] * len(carry),
        out_shape=[jax.ShapeDtypeStruct((t, d), F32), jax.ShapeDtypeStruct((t, 1), F32),
                   jax.ShapeDtypeStruct((d, t), BF16), jax.ShapeDtypeStruct((t, f), BF16),
                   jax.ShapeDtypeStruct((t, f), BF16)] + t_shape + _carry_shapes(carry),
        scratch_shapes=[pltpu.VMEM((nch, d, fc), BF16), pltpu.VMEM((nch, d, fc), BF16),
                        pltpu.VMEM((nch, fc, d), BF16), pltpu.VMEM((tm, d), F32),
                        pltpu.VMEM((tm, d), BF16), pltpu.SemaphoreType.DMA((3,))] + _carry_scratch(carry),
        compiler_params=_params(("arbitrary", "arbitrary"), VMEM_BIG),
    )(xh, gi, bi, wg, wu, wd, go, bo, *[a for _, a in carry])


def _ffn_bwd(dh, xo, rs, go, gs, us, wg, wu, wd, alpha, name, carry=(), loss_target=None, loss_bias=None):
    t, d = xo.shape
    nch, _, fc = wg.shape
    f = nch * fc
    per = min(FFN_BWD_CHUNKS, nch)
    nc = -(-nch // per)
    tm = _row_tile(t)
    nt = t // tm

    with_loss = loss_target is not None
    nsub, lead = tm // LOSS_TILE, ROW0 // LOSS_TILE
    nlead = nsub + 1 if with_loss else 1

    def body(*refs):
        lead_refs = refs[:nlead]
        xo_ref, rs_ref, go_ref, g_ref, u_ref, wg_hbm, wu_hbm, wd_hbm = refs[nlead:nlead + 8]
        dhin_ref, dot_ref, dg_ref, du_ref, a_ref, dgain_ref, dbias_ref = refs[nlead + 8:nlead + 15]
        rest = refs[nlead + 15:]
        loss_ref, rest = (rest[0], rest[1:]) if with_loss else (None, rest)
        wg_v, wu_v, wd_v, do_ref, sems = rest[:5]
        i = pl.program_id(0)
        c = pl.program_id(1)

        @pl.when((i == 0) & (c == 0))
        def _():
            _load_resident([(wg_hbm, wg_v), (wu_hbm, wu_v), (wd_hbm, wd_v)], sems)
            dgain_ref[...] = jnp.zeros_like(dgain_ref)
            dbias_ref[...] = jnp.zeros_like(dbias_ref)
            if with_loss:
                rest[5][...] = jnp.zeros_like(rest[5])

        def tile_dh():
            if not with_loss:
                return lead_refs[0][...]
            part = rest[5]
            for k in range(nsub):
                sl = slice(k * LOSS_TILE, (k + 1) * LOSS_TILE)
                rows = i * tm + k * LOSS_TILE + lax.broadcasted_iota(jnp.int32, (LOSS_TILE, 1), 0)
                y = xo_ref[sl, :] * go_ref[...] + lead_refs[nsub][...]
                e = jnp.where(rows >= ROW0, y - lead_refs[k][...], 0.0)
                part[...] += jnp.sum(e * e, axis=0, keepdims=True)
                dhin_ref[sl, :] = e * (1.0 / d)

            @pl.when(i == nt - 1)
            def _():
                loss_ref[...] = jnp.full((1, LANES), 0.5 / d, F32) * jnp.sum(part[...])

            return dhin_ref[...]

        @pl.when(c == 0)
        def _():
            dz, dgp, dbp = _ln_bwd(tile_dh(), xo_ref[...], rs_ref[...], go_ref[...])
            dgain_ref[...] += dgp
            dbias_ref[...] += dbp
            dob = (0.5 * dz).astype(BF16)
            do_ref[...] = dob
            dot_ref[...] = dob.T
            dhin_ref[...] = alpha * dz

        def chunk(k):
            ck = c * per + k
            cols = slice(k * fc, (k + 1) * fc)
            g = g_ref[:, cols].astype(F32)
            u = u_ref[:, cols].astype(F32)
            sg = jax.nn.sigmoid(g)
            sl = g * sg
            da = _dot_nt(do_ref[...], wd_v[ck])
            dgb = (da * u * (sg * (1.0 + g * (1.0 - sg)))).astype(BF16)
            dub = (da * sl).astype(BF16)
            a_ref[:, cols] = (sl * u).astype(BF16)
            dg_ref[:, cols] = dgb
            du_ref[:, cols] = dub
            dhin_ref[...] += _dot_nt(dgb, wg_v[ck]) + _dot_nt(dub, wu_v[ck])

        for k in range(per):
            if (nc - 1) * per + k < nch:
                chunk(k)
            else:
                pl.when(c * per + k < nch)(functools.partial(chunk, k))

    row = pl.BlockSpec((tm, d), lambda i, c: (i, 0))
    vec = pl.BlockSpec((1, d), lambda i, c: (0, 0))
    chunk = pl.BlockSpec((tm, per * fc), lambda i, c: (i, c))
    first = lambda: (pl.program_id(0) == 0) & (pl.program_id(1) == 0)
    last = lambda: (pl.program_id(0) == nt - 1) & (pl.program_id(1) == nc - 1)
    if with_loss:
        lead_specs = [pl.BlockSpec((LOSS_TILE, d), lambda i, c, k=k: (jnp.maximum(i * nsub + k - lead, 0), 0))
                      for k in range(nsub)] + [vec]
        lead_args = [loss_target] * nsub + [loss_bias]
        loss_spec, loss_shape = [pl.BlockSpec((1, LANES), lambda i, c: (0, 0))], [jax.ShapeDtypeStruct((1, LANES), F32)]
        loss_scratch = [pltpu.VMEM((1, d), F32)]
    else:
        lead_specs, lead_args, loss_spec, loss_shape, loss_scratch = [row], [dh], [], [], []
    return pl.pallas_call(
        _carried(body, nlead + 8, 7 + len(loss_spec), carry, first, last), name=name, grid=(nt, nc),
        in_specs=lead_specs + [row, pl.BlockSpec((tm, 1), lambda i, c: (i, 0)), vec, chunk, chunk,
                               ANY, ANY, ANY] + [ANY] * len(carry),
        out_specs=[row, pl.BlockSpec((d, tm), lambda i, c: (0, i)), chunk, chunk, chunk, vec, vec]
                  + loss_spec + [ANY] * len(carry),
        out_shape=[jax.ShapeDtypeStruct((t, d), F32), jax.ShapeDtypeStruct((d, t), BF16),
                   jax.ShapeDtypeStruct((t, f), BF16), jax.ShapeDtypeStruct((t, f), BF16),
                   jax.ShapeDtypeStruct((t, f), BF16), jax.ShapeDtypeStruct((1, d), F32),
                   jax.ShapeDtypeStruct((1, d), F32)] + loss_shape + _carry_shapes(carry),
        scratch_shapes=[pltpu.VMEM((nch, d, fc), BF16), pltpu.VMEM((nch, d, fc), BF16),
                        pltpu.VMEM((nch, fc, d), BF16), pltpu.VMEM((tm, d), BF16),
                        pltpu.SemaphoreType.DMA((3,))] + loss_scratch + _carry_scratch(carry),
        compiler_params=_params(("arbitrary", "arbitrary"), VMEM_BIG),
    )(*lead_args, xo, rs, go, gs, us, wg, wu, wd, *[a for _, a in carry])


def _wgrad(xt, ys, name, carry=()):
    m, t = xt.shape
    n = ys[0].shape[1]
    tn = min(n, MXU_COLS)
    ny = len(ys)

    def body(*refs):
        x_hbm = refs[0]
        y_refs = refs[1:1 + ny]
        o_refs = refs[1 + ny:1 + 2 * ny]
        xv, sems = refs[1 + 2 * ny:]

        @pl.when(pl.program_id(0) == 0)
        def _():
            _load_resident([(x_hbm, xv)], sems)

        for y_ref, o_ref in zip(y_refs, o_refs):
            o_ref[...] = _dot(xv[...], y_ref[...].astype(BF16)).astype(BF16)

    steps = n // tn
    first = lambda: pl.program_id(0) == 0
    last = lambda: pl.program_id(0) == steps - 1
    return pl.pallas_call(
        _carried(body, 1 + ny, ny, carry, first, last), name=name, grid=(steps,),
        in_specs=[ANY] + [pl.BlockSpec((t, tn), lambda c: (0, c)) for _ in ys] + [ANY] * len(carry),
        out_specs=[pl.BlockSpec((m, tn), lambda c: (0, c)) for _ in ys] + [ANY] * len(carry),
        out_shape=[jax.ShapeDtypeStruct((m, n), BF16) for _ in ys] + _carry_shapes(carry),
        scratch_shapes=[pltpu.VMEM((m, t), BF16), pltpu.SemaphoreType.DMA((1,))] + _carry_scratch(carry),
        compiler_params=_params(("arbitrary",), VMEM_BIG),
    )(xt, *ys, *[a for _, a in carry])


def _shift_rows(u, halo, tm):
    r = lax.broadcasted_iota(jnp.int32, (tm, 1), 0)
    u1 = jnp.where(r == 0, halo[7:8], pltpu.roll(u, 1, 0))
    u2 = jnp.where(r == 0, halo[6:7], jnp.where(r == 1, halo[7:8], pltpu.roll(u, 2, 0)))
    return u1, u2


def _conv_fwd(xh, gi, bi, w_in, cw, w_out, go, bo, alpha, name, carry=()):
    t, d = xh.shape
    tm = _row_tile(t)
    nt = t // tm

    def body(xh_ref, gi_ref, bi_ref, win_ref, cw_ref, wout_ref, go_ref, bo_ref,
             xo_ref, rs_ref, hb_ref, p_ref, m_ref, halo):
        i = pl.program_id(0)

        @pl.when(i == 0)
        def _():
            halo[...] = jnp.zeros_like(halo)

        h = xh_ref[...] * gi_ref[...] + bi_ref[...]
        hb = h.astype(BF16)
        bg = _dot(hb, win_ref[:, 0:d])
        cg = _dot(hb, win_ref[:, d:2 * d])
        val = _dot(hb, win_ref[:, 2 * d:3 * d])
        p_ref[:, 0:d] = bg.astype(BF16)
        p_ref[:, d:2 * d] = cg.astype(BF16)
        p_ref[:, 2 * d:3 * d] = val.astype(BF16)
        rows = i * tm + lax.broadcasted_iota(jnp.int32, (tm, 1), 0)
        u = jnp.where(rows >= PAD, cg * val, 0.0)
        u1, u2 = _shift_rows(u, halo[...], tm)
        halo[...] = u[tm - 8:tm]
        y = cw_ref[0:1] * u2 + cw_ref[1:2] * u1 + cw_ref[2:3] * u
        mb = (bg * y).astype(BF16)
        m_ref[...] = mb.T
        xhat, rstd = _ln_fwd(alpha * h + _dot(mb, wout_ref[...]))
        xo_ref[...] = xhat
        rs_ref[...] = rstd
        hb_ref[...] = (xhat * go_ref[...] + bo_ref[...]).astype(BF16).T

    row = pl.BlockSpec((tm, d), lambda i: (i, 0))
    col = pl.BlockSpec((d, tm), lambda i: (0, i))
    vec = pl.BlockSpec((1, d), lambda i: (0, 0))
    first = lambda: pl.program_id(0) == 0
    last = lambda: pl.program_id(0) == nt - 1
    return pl.pallas_call(
        _carried(body, 8, 5, carry, first, last), name=name, grid=(nt,),
        in_specs=[row, vec, vec, pl.BlockSpec((d, 3 * d), lambda i: (0, 0)),
                  pl.BlockSpec((3, d), lambda i: (0, 0)), pl.BlockSpec((d, d), lambda i: (0, 0)),
                  vec, vec] + [ANY] * len(carry),
        out_specs=[row, pl.BlockSpec((tm, 1), lambda i: (i, 0)), col,
                   pl.BlockSpec((tm, 3 * d), lambda i: (i, 0)), col] + [ANY] * len(carry),
        out_shape=[jax.ShapeDtypeStruct((t, d), F32), jax.ShapeDtypeStruct((t, 1), F32),
                   jax.ShapeDtypeStruct((d, t), BF16), jax.ShapeDtypeStruct((t, 3 * d), BF16),
                   jax.ShapeDtypeStruct((d, t), BF16)] + _carry_shapes(carry),
        scratch_shapes=[pltpu.VMEM((8, d), F32)] + _carry_scratch(carry),
        compiler_params=_params(("arbitrary",), VMEM_BIG),
    )(xh, gi, bi, w_in, cw, w_out, go, bo, *[a for _, a in carry])


def _conv_bwd(dh, xo, rs, go, p, cw, w_in, w_out, alpha, name):
    t, d = dh.shape
    tm = _row_tile(t)
    nt = t // tm
    tb = tm // 8

    def body(dh_ref, xo_ref, rs_ref, go_ref, p_ref, ph_ref, cw_ref, win_ref, wout_ref,
             dhin_ref, dmix_ref, dp_ref, dcw_ref, dgain_ref, dbias_ref, carry):
        i = pl.program_id(0)
        tile = nt - 1 - i

        @pl.when(i == 0)
        def _():
            carry[...] = jnp.zeros_like(carry)
            dcw_ref[...] = jnp.zeros_like(dcw_ref)
            dgain_ref[...] = jnp.zeros_like(dgain_ref)
            dbias_ref[...] = jnp.zeros_like(dbias_ref)

        dz, dgp, dbp = _ln_bwd(dh_ref[...], xo_ref[...], rs_ref[...], go_ref[...])
        dgain_ref[...] += dgp
        dbias_ref[...] += dbp
        dmixb = dz.astype(BF16)
        dmix_ref[...] = dmixb
        dm = _dot_nt(dmixb, wout_ref[...])

        bg = p_ref[:, 0:d].astype(F32)
        cg = p_ref[:, d:2 * d].astype(F32)
        val = p_ref[:, 2 * d:3 * d].astype(F32)
        rows = tile * tm + lax.broadcasted_iota(jnp.int32, (tm, 1), 0)
        valid = rows >= PAD
        u = jnp.where(valid, cg * val, 0.0)
        hrows = tile * tm - 8 + lax.broadcasted_iota(jnp.int32, (8, 1), 0)
        hu = jnp.where((hrows >= PAD) & (tile > 0),
                       ph_ref[:, d:2 * d].astype(F32) * ph_ref[:, 2 * d:3 * d].astype(F32), 0.0)
        u1, u2 = _shift_rows(u, hu, tm)
        w0, w1, w2 = cw_ref[0:1], cw_ref[1:2], cw_ref[2:3]
        y = w0 * u2 + w1 * u1 + w2 * u
        dbg = dm * y
        dy = dm * bg
        dcw_ref[0:1] += jnp.sum(dy * u2, axis=0, keepdims=True)
        dcw_ref[1:2] += jnp.sum(dy * u1, axis=0, keepdims=True)
        dcw_ref[2:3] += jnp.sum(dy * u, axis=0, keepdims=True)

        nxt = carry[...]
        r = lax.broadcasted_iota(jnp.int32, (tm, 1), 0)
        dy1 = jnp.where(r == tm - 1, nxt[0:1], pltpu.roll(dy, tm - 1, 0))
        dy2 = jnp.where(r == tm - 2, nxt[0:1],
                        jnp.where(r == tm - 1, nxt[1:2], pltpu.roll(dy, tm - 2, 0)))
        carry[...] = dy[0:8]
        du = jnp.where(valid, w2 * dy + w1 * dy1 + w0 * dy2, 0.0)
        dbgb = dbg.astype(BF16)
        dcgb = (du * val).astype(BF16)
        dvalb = (du * cg).astype(BF16)
        dp_ref[:, 0:d] = dbgb
        dp_ref[:, d:2 * d] = dcgb
        dp_ref[:, 2 * d:3 * d] = dvalb
        dhin_ref[...] = (alpha * dz + _dot_nt(dbgb, win_ref[:, 0:d])
                         + _dot_nt(dcgb, win_ref[:, d:2 * d]) + _dot_nt(dvalb, win_ref[:, 2 * d:3 * d]))

    row = pl.BlockSpec((tm, d), lambda i: (nt - 1 - i, 0))
    vec = pl.BlockSpec((1, d), lambda i: (0, 0))
    prow = pl.BlockSpec((tm, 3 * d), lambda i: (nt - 1 - i, 0))
    return pl.pallas_call(
        body, name=name, grid=(nt,),
        in_specs=[row, row, pl.BlockSpec((tm, 1), lambda i: (nt - 1 - i, 0)), vec, prow,
                  pl.BlockSpec((8, 3 * d), lambda i: (jnp.maximum((nt - 1 - i) * tb - 1, 0), 0)),
                  pl.BlockSpec((3, d), lambda i: (0, 0)),
                  pl.BlockSpec((d, 3 * d), lambda i: (0, 0)), pl.BlockSpec((d, d), lambda i: (0, 0))],
        out_specs=[row, row, prow, pl.BlockSpec((3, d), lambda i: (0, 0)), vec, vec],
        out_shape=[jax.ShapeDtypeStruct((t, d), F32), jax.ShapeDtypeStruct((t, d), BF16),
                   jax.ShapeDtypeStruct((t, 3 * d), BF16), jax.ShapeDtypeStruct((3, d), F32),
                   jax.ShapeDtypeStruct((1, d), F32), jax.ShapeDtypeStruct((1, d), F32)],
        scratch_shapes=[pltpu.VMEM((8, d), F32)],
        compiler_params=_params(("arbitrary",), VMEM_BIG),
    )(dh, xo, rs, go, p, p, cw, w_in, w_out)


def _kv_fwd(xh, gi, bi, wk, wv, wf, fb, name, carry=()):
    t, d = xh.shape
    tm = _row_tile(t)
    nt = t // tm

    def body(xh_ref, gi_ref, bi_ref, wk_ref, wv_ref, wf_ref, fb_ref,
             k_ref, v_ref, lg_ref, c_ref, ct_ref, run):
        i = pl.program_id(0)

        @pl.when(i == 0)
        def _():
            run[...] = jnp.zeros_like(run)

        x = (xh_ref[...] * gi_ref[...] + bi_ref[...]).astype(BF16)
        k_ref[...] = _dot(x, wk_ref[...]).astype(BF16)
        v_ref[...] = _dot(x, wv_ref[...]).astype(BF16)
        logit = _dot(x, wf_ref[...]) + fb_ref[...]
        lg_ref[...] = logit
        logf = jnp.minimum(logit, 0.0) - jnp.log(1.0 + jnp.exp(-jnp.abs(logit)))
        rows = i * tm + lax.broadcasted_iota(jnp.int32, (tm, 1), 0)
        logf = jnp.where(rows >= PAD, logf, 0.0)
        tri = (lax.broadcasted_iota(jnp.int32, (tm, tm), 0)
               >= lax.broadcasted_iota(jnp.int32, (tm, tm), 1)).astype(F32)
        cs = jnp.dot(tri, logf, precision=lax.Precision.HIGHEST, preferred_element_type=F32) + run[...]
        run[...] = cs[tm - 1:tm]
        c_ref[...] = cs
        ct_ref[...] = cs.T

    row = pl.BlockSpec((tm, d), lambda i: (i, 0))
    vec = pl.BlockSpec((1, d), lambda i: (0, 0))
    gate = pl.BlockSpec((tm, LANES), lambda i: (i, 0))
    sq = pl.BlockSpec((d, d), lambda i: (0, 0))
    first = lambda: pl.program_id(0) == 0
    last = lambda: pl.program_id(0) == nt - 1
    return pl.pallas_call(
        _carried(body, 7, 5, carry, first, last), name=name, grid=(nt,),
        in_specs=[row, vec, vec, sq, sq, pl.BlockSpec((d, LANES), lambda i: (0, 0)),
                  pl.BlockSpec((1, LANES), lambda i: (0, 0))] + [ANY] * len(carry),
        out_specs=[row, row, gate, gate, pl.BlockSpec((LANES, tm), lambda i: (0, i))] + [ANY] * len(carry),
        out_shape=[jax.ShapeDtypeStruct((t, d), BF16), jax.ShapeDtypeStruct((t, d), BF16),
                   jax.ShapeDtypeStruct((t, LANES), F32), jax.ShapeDtypeStruct((t, LANES), F32),
                   jax.ShapeDtypeStruct((LANES, t), F32)] + _carry_shapes(carry),
        scratch_shapes=[pltpu.VMEM((1, LANES), F32)] + _carry_scratch(carry),
        compiler_params=_params(("arbitrary",), VMEM_MID),
    )(xh, gi, bi, wk, wv, wf, fb, *[a for _, a in carry])


def _kv_bwd(dk, dv, dcs, dcq, logit, dh_other, wk, wv, wf, name):
    t, d = dk.shape
    tm = _row_tile(t)
    nt = t // tm

    def body(dk_ref, dv_ref, dcs_ref, dcq_ref, lg_ref, oth_ref, wk_ref, wv_ref, wf_ref,
             dh_ref, dl_ref, dfb_ref, run):
        i = pl.program_id(0)
        tile = nt - 1 - i

        @pl.when(i == 0)
        def _():
            run[...] = jnp.zeros_like(run)
            dfb_ref[...] = jnp.zeros_like(dfb_ref)

        lane = lax.broadcasted_iota(jnp.int32, (tm, LANES), 1)
        dc = dcq_ref[...]
        for hh in range(N_HEADS):
            dc = dc + jnp.where(lane == hh, jnp.sum(dcs_ref[hh], axis=1, keepdims=True), 0.0)
        tri = (lax.broadcasted_iota(jnp.int32, (tm, tm), 0)
               <= lax.broadcasted_iota(jnp.int32, (tm, tm), 1)).astype(F32)
        dlf = jnp.dot(tri, dc, precision=lax.Precision.HIGHEST, preferred_element_type=F32) + run[...]
        run[...] = dlf[0:1]
        rows = tile * tm + lax.broadcasted_iota(jnp.int32, (tm, 1), 0)
        dlogit = jnp.where(rows >= PAD, dlf * jax.nn.sigmoid(-lg_ref[...]), 0.0)
        dfb_ref[...] += jnp.sum(dlogit, axis=0, keepdims=True)
        dlb = dlogit.astype(BF16)
        dl_ref[...] = dlb
        dh_ref[...] = (oth_ref[...] + _dot_nt(dk_ref[...], wk_ref[...])
                       + _dot_nt(dv_ref[...], wv_ref[...]) + _dot_nt(dlb, wf_ref[...]))

    row = pl.BlockSpec((tm, d), lambda i: (nt - 1 - i, 0))
    gate = pl.BlockSpec((tm, LANES), lambda i: (nt - 1 - i, 0))
    sq = pl.BlockSpec((d, d), lambda i: (0, 0))
    return pl.pallas_call(
        body, name=name, grid=(nt,),
        in_specs=[row, row, pl.BlockSpec((N_HEADS, tm, LANES), lambda i: (0, nt - 1 - i, 0)), gate, gate, row,
                  sq, sq, pl.BlockSpec((d, LANES), lambda i: (0, 0))],
        out_specs=[row, gate, pl.BlockSpec((1, LANES), lambda i: (0, 0))],
        out_shape=[jax.ShapeDtypeStruct((t, d), F32), jax.ShapeDtypeStruct((t, LANES), BF16),
                   jax.ShapeDtypeStruct((1, LANES), F32)],
        scratch_shapes=[pltpu.VMEM((1, LANES), F32)],
        compiler_params=_params(("arbitrary",), VMEM_MID),
    )(dk, dv, dcs, dcq, logit, dh_other, wk, wv, wf)


def _proj(xh, gi, bi, w, name):
    t, k = xh.shape
    n = w.shape[1]
    tm = _row_tile(t)

    def body(x_ref, g_ref, b_ref, w_ref, o_ref):
        x = (x_ref[...] * g_ref[...] + b_ref[...]).astype(BF16)
        o_ref[...] = _dot(x, w_ref[...]).astype(BF16)

    vec = pl.BlockSpec((1, k), lambda i: (0, 0))
    return pl.pallas_call(
        body, name=name, grid=(t // tm,),
        in_specs=[pl.BlockSpec((tm, k), lambda i: (i, 0)), vec, vec, pl.BlockSpec((k, n), lambda i: (0, 0))],
        out_specs=pl.BlockSpec((tm, n), lambda i: (i, 0)),
        out_shape=jax.ShapeDtypeStruct((t, n), BF16),
        compiler_params=_params(("arbitrary",), VMEM_MID),
    )(xh, gi, bi, w)


def _add_proj_nt(base, y, w, name):
    t, n = y.shape
    k = w.shape[0]
    tm = _row_tile(t)

    def body(b_ref, y_ref, w_ref, o_ref):
        o_ref[...] = b_ref[...] + _dot_nt(y_ref[...].astype(BF16), w_ref[...])

    return pl.pallas_call(
        body, name=name, grid=(t // tm,),
        in_specs=[pl.BlockSpec((tm, k), lambda i: (i, 0)), pl.BlockSpec((tm, n), lambda i: (i, 0)),
                  pl.BlockSpec((k, n), lambda i: (0, 0))],
        out_specs=pl.BlockSpec((tm, k), lambda i: (i, 0)),
        out_shape=jax.ShapeDtypeStruct((t, k), F32),
        compiler_params=_params(("arbitrary",), VMEM_MID),
    )(base, y, w)


def _attn_out_fwd(ot, xh, gi, bi, w_o, go, bo, alpha, name):
    t, d = xh.shape
    tm = _row_tile(t)

    def body(ot_ref, xh_ref, gi_ref, bi_ref, wo_ref, go_ref, bo_ref, xo_ref, rs_ref, hb_ref):
        h = xh_ref[...] * gi_ref[...] + bi_ref[...]
        xhat, rstd = _ln_fwd(alpha * h + _dot_tn(ot_ref[...], wo_ref[...]))
        xo_ref[...] = xhat
        rs_ref[...] = rstd
        hb_ref[...] = (xhat * go_ref[...] + bo_ref[...]).astype(BF16).T

    row = pl.BlockSpec((tm, d), lambda i: (i, 0))
    col = pl.BlockSpec((d, tm), lambda i: (0, i))
    vec = pl.BlockSpec((1, d), lambda i: (0, 0))
    return pl.pallas_call(
        body, name=name, grid=(t // tm,),
        in_specs=[col, row, vec, vec, pl.BlockSpec((d, d), lambda i: (0, 0)), vec, vec],
        out_specs=[row, pl.BlockSpec((tm, 1), lambda i: (i, 0)), col],
        out_shape=[jax.ShapeDtypeStruct((t, d), F32), jax.ShapeDtypeStruct((t, 1), F32),
                   jax.ShapeDtypeStruct((d, t), BF16)],
        compiler_params=_params(("arbitrary",), VMEM_MID),
    )(ot, xh, gi, bi, w_o, go, bo)


def _attn_out_bwd(dh, xo, rs, go, ot, w_o, alpha, name):
    t, d = dh.shape
    tm = _row_tile(t)
    hd = d // N_HEADS

    def body(dh_ref, xo_ref, rs_ref, go_ref, ot_ref, wo_ref,
             dres_ref, dmix_ref, dot_ref, delta_ref, dgain_ref, dbias_ref):
        @pl.when(pl.program_id(0) == 0)
        def _():
            dgain_ref[...] = jnp.zeros_like(dgain_ref)
            dbias_ref[...] = jnp.zeros_like(dbias_ref)

        dz, dgp, dbp = _ln_bwd(dh_ref[...], xo_ref[...], rs_ref[...], go_ref[...])
        dgain_ref[...] += dgp
        dbias_ref[...] += dbp
        dres_ref[...] = alpha * dz
        dmixb = dz.astype(BF16)
        dmix_ref[...] = dmixb
        dot_t = _dot_nt(wo_ref[...], dmixb)
        dot_ref[...] = dot_t.astype(BF16)
        prod = dot_t * ot_ref[...].astype(F32)
        delta_ref[...] = jnp.sum(prod.reshape(N_HEADS, hd, tm), axis=1)

    row = pl.BlockSpec((tm, d), lambda i: (i, 0))
    vec = pl.BlockSpec((1, d), lambda i: (0, 0))
    col = pl.BlockSpec((d, tm), lambda i: (0, i))
    return pl.pallas_call(
        body, name=name, grid=(t // tm,),
        in_specs=[row, row, pl.BlockSpec((tm, 1), lambda i: (i, 0)), vec, col,
                  pl.BlockSpec((d, d), lambda i: (0, 0))],
        out_specs=[row, row, col, pl.BlockSpec((N_HEADS, tm), lambda i: (0, i)), vec, vec],
        out_shape=[jax.ShapeDtypeStruct((t, d), F32), jax.ShapeDtypeStruct((t, d), BF16),
                   jax.ShapeDtypeStruct((d, t), BF16), jax.ShapeDtypeStruct((N_HEADS, t), F32),
                   jax.ShapeDtypeStruct((1, d), F32), jax.ShapeDtypeStruct((1, d), F32)],
        compiler_params=_params(("arbitrary",), VMEM_MID),
    )(dh, xo, rs, go, ot, w_o)


def _scores_t(k, q, ct_ref, c_ref, h, i, j, tq, tk, scale, masked):
    sub = lax.broadcasted_iota(jnp.int32, (8, tq), 0)
    cq = jnp.sum(jnp.where(sub == h, ct_ref[...], 0.0), axis=0, keepdims=True) * LOG2E
    lane = lax.broadcasted_iota(jnp.int32, (tk, LANES), 1)
    ck = jnp.sum(jnp.where(lane == h, c_ref[...], 0.0), axis=1, keepdims=True) * LOG2E
    st = _dot_nt(k, q) * (scale * LOG2E) - ck
    if masked:
        kpos = j * tk + lax.broadcasted_iota(jnp.int32, (tk, 1), 0)
        qpos = i * tq + lax.broadcasted_iota(jnp.int32, (1, tq), 1)
        st = jnp.where((kpos <= qpos) & (kpos >= PAD), st, NEG_INF)
    return st, cq


def _tri_pairs(n, by_row):
    if by_row:
        pairs = [(i, j) for i in range(n) for j in range(i + 1)]
    else:
        pairs = [(i, j) for j in range(n) for i in range(j, n)]
    return (jnp.asarray([p[0] for p in pairs], jnp.int32), jnp.asarray([p[1] for p in pairs], jnp.int32))


def _attn_fwd(q, k, v, c, ct, name, carry=()):
    t, d = q.shape
    hd = d // N_HEADS
    tq = tk = _row_tile(t)
    nq = t // tq
    scale = 1.0 / math.sqrt(hd)

    hps = ATTN_HEADS_PER_STEP

    def body(it_ref, jt_ref, q_ref, k_ref, v_ref, c_ref, ct_ref, ot_ref, lse_ref, m_s, l_s, acc):
        hp, p_ = pl.program_id(0), pl.program_id(1)
        i, j = it_ref[p_], jt_ref[p_]

        @pl.when(j == 0)
        def _():
            m_s[...] = jnp.full_like(m_s, NEG_INF)
            l_s[...] = jnp.zeros_like(l_s)
            acc[...] = jnp.zeros_like(acc)

        def update(masked):
            scores = []
            for e in range(hps):
                cols = slice(e * hd, (e + 1) * hd)
                scores.append(_scores_t(k_ref[:, cols], q_ref[:, cols], ct_ref, c_ref, hp * hps + e,
                                        i, j, tq, tk, scale, masked))
            probs = []
            for e, (st, cq) in enumerate(scores):
                m_new = jnp.maximum(m_s[e], jnp.max(st, axis=0, keepdims=True) + cq)
                a = jnp.exp2(m_s[e] - m_new)
                p = jnp.exp2(st - (m_new - cq))
                l_s[e] = a * l_s[e] + jnp.sum(p, axis=0, keepdims=True)
                m_s[e] = m_new
                probs.append((a, p.astype(BF16)))
            for e, (a, pb) in enumerate(probs):
                acc[e] = a * acc[e] + _dot_tn(v_ref[:, e * hd:(e + 1) * hd], pb)

        edge = (j == i) | (j == 0)
        pl.when(edge)(lambda: update(True))
        pl.when(jnp.logical_not(edge))(lambda: update(False))

        @pl.when(j == i)
        def _():
            for e in range(hps):
                ot_ref[e * hd:(e + 1) * hd, :] = (acc[e] / l_s[e]).astype(BF16)
                lse_ref[e] = m_s[e] + jnp.log2(l_s[e])

    it, jt = _tri_pairs(nq, by_row=True)
    npairs = it.shape[0]
    nhp = N_HEADS // hps
    kv = pl.BlockSpec((tk, hps * hd), lambda h, p, it, jt: (jt[p], h))
    first = lambda: (pl.program_id(0) == 0) & (pl.program_id(1) == 0)
    last = lambda: (pl.program_id(0) == nhp - 1) & (pl.program_id(1) == npairs - 1)
    return pl.pallas_call(
        _carried(body, 7, 2, carry, first, last), name=name,
        grid_spec=pltpu.PrefetchScalarGridSpec(
            num_scalar_prefetch=2, grid=(nhp, npairs),
            in_specs=[pl.BlockSpec((tq, hps * hd), lambda h, p, it, jt: (it[p], h)), kv, kv,
                      pl.BlockSpec((tk, LANES), lambda h, p, it, jt: (jt[p], 0)),
                      pl.BlockSpec((8, tq), lambda h, p, it, jt: (0, it[p]))] + [ANY] * len(carry),
            out_specs=[pl.BlockSpec((hps * hd, tq), lambda h, p, it, jt: (h, it[p])),
                       pl.BlockSpec((hps, 1, tq), lambda h, p, it, jt: (h, 0, it[p]))] + [ANY] * len(carry),
            scratch_shapes=[pltpu.VMEM((hps, 1, tq), F32), pltpu.VMEM((hps, 1, tq), F32),
                            pltpu.VMEM((hps, hd, tq), F32)] + _carry_scratch(carry)),
        out_shape=[jax.ShapeDtypeStruct((d, t), BF16), jax.ShapeDtypeStruct((N_HEADS, 1, t), F32)]
                  + _carry_shapes(carry),
        compiler_params=_params(("arbitrary", "arbitrary"), VMEM_MID),
    )(it, jt, q, k, v, c, ct, *[a for _, a in carry])


def _attn_bwd(q, k, v, c, ct, lse, delta, dot_t, name, carry=()):
    t, d = q.shape
    hd = d // N_HEADS
    tq = tk = _row_tile(t)
    nq = t // tq
    scale = 1.0 / math.sqrt(hd)
    hps = ATTN_BWD_HEADS_PER_STEP

    steps = [(j, i, min(i + 1, nq - 1), int(i + 1 < nq)) for j in range(nq) for i in range(j, nq, 2)]
    jt, ia, ib, vb = (jnp.asarray([s[n] for s in steps], jnp.int32) for n in range(4))

    def body(jt_ref, ia_ref, ib_ref, vb_ref, qa_ref, k_ref, v_ref, c_ref, cta_ref, lsea_ref, deltaa_ref, dota_ref,
             qb_ref, ctb_ref, lseb_ref, deltab_ref, dotb_ref,
             dq_ref, dk_ref, dv_ref, dcs_ref, drow_ref, dk_acc, dv_acc, dc_acc):
        hp, p_ = pl.program_id(0), pl.program_id(1)
        j, i_a, i_b, has_b = jt_ref[p_], ia_ref[p_], ib_ref[p_], vb_ref[p_] == 1

        @pl.when(p_ == 0)
        def _():
            dq_ref[...] = jnp.zeros_like(dq_ref)
            drow_ref[...] = jnp.zeros_like(drow_ref)

        @pl.when(i_a == j)
        def _():
            dk_acc[...] = jnp.zeros_like(dk_acc)
            dv_acc[...] = jnp.zeros_like(dv_acc)
            dc_acc[...] = jnp.zeros_like(dc_acc)

        def update(q_ref, ct_ref, lse_ref, delta_ref, dot_ref, i, masked):
            sub = lax.broadcasted_iota(jnp.int32, (8, tq), 0)
            rows = pl.ds(pl.multiple_of(i * tq, tq), tq)
            stage = []
            for e in range(hps):
                cols = slice(e * hd, (e + 1) * hd)
                st, cq = _scores_t(k_ref[:, cols], q_ref[:, cols], ct_ref, c_ref, hp * hps + e,
                                   i, j, tq, tk, scale, masked)
                dp = _dot(v_ref[:, cols], dot_ref[cols, :])
                stage.append((st, cq, dp))
            grads = []
            for e, (st, cq, dp) in enumerate(stage):
                p = jnp.exp2(st - (lse_ref[e] - cq))
                dl = jnp.sum(jnp.where(sub == hp * hps + e, delta_ref[...], 0.0), axis=0, keepdims=True)
                ds = p * (dp - dl)
                part = ds[:, 0:LANES]
                for g in range(1, tq // LANES):
                    part = part + ds[:, g * LANES:(g + 1) * LANES]
                dc_acc[e] += part
                drow_ref[e, i] += jnp.broadcast_to(jnp.sum(ds, axis=0, keepdims=True), (8, tq))
                grads.append((p.astype(BF16), ds.astype(BF16)))
            for e, (pb, dsb) in enumerate(grads):
                cols = slice(e * hd, (e + 1) * hd)
                dv_acc[e] += _dot_nt(pb, dot_ref[cols, :])
                dk_acc[e] += _dot(dsb, q_ref[:, cols]) * scale
                dq_ref[rows, cols] += _dot_tn(dsb, k_ref[:, cols]) * scale

        slot_a = (qa_ref, cta_ref, lsea_ref, deltaa_ref, dota_ref, i_a)
        slot_b = (qb_ref, ctb_ref, lseb_ref, deltab_ref, dotb_ref, i_b)
        edge_a = (j == i_a) | (j == 0)
        pl.when(edge_a)(lambda: update(*slot_a, True))
        pl.when(jnp.logical_not(edge_a))(lambda: update(*slot_a, False))
        pl.when(has_b & (j == 0))(lambda: update(*slot_b, True))
        pl.when(has_b & (j != 0))(lambda: update(*slot_b, False))

        @pl.when((i_a == nq - 1) | (has_b & (i_b == nq - 1)))
        def _():
            for e in range(hps):
                cols = slice(e * hd, (e + 1) * hd)
                dk_ref[:, cols] = dk_acc[e].astype(BF16)
                dv_ref[:, cols] = dv_acc[e].astype(BF16)
                dcs_ref[e] = -dc_acc[e]

    nsteps = len(steps)
    nhp = N_HEADS // hps
    kv = pl.BlockSpec((tk, hps * hd), lambda h, p, jt, ia, ib, vb: (jt[p], h))

    def q_side(sel):
        return [pl.BlockSpec((tq, hps * hd), lambda h, p, jt, ia, ib, vb: (sel(ia, ib)[p], h)),
                pl.BlockSpec((8, tq), lambda h, p, jt, ia, ib, vb: (0, sel(ia, ib)[p])),
                pl.BlockSpec((hps, 1, tq), lambda h, p, jt, ia, ib, vb: (h, 0, sel(ia, ib)[p])),
                pl.BlockSpec((N_HEADS, tq), lambda h, p, jt, ia, ib, vb: (0, sel(ia, ib)[p])),
                pl.BlockSpec((hps * hd, tq), lambda h, p, jt, ia, ib, vb: (h, sel(ia, ib)[p]))]

    qa_specs, qb_specs = q_side(lambda ia, ib: ia), q_side(lambda ia, ib: ib)
    first = lambda: (pl.program_id(0) == 0) & (pl.program_id(1) == 0)
    last = lambda: (pl.program_id(0) == nhp - 1) & (pl.program_id(1) == nsteps - 1)
    q_args = (q, ct, lse, delta, dot_t)
    return pl.pallas_call(
        _carried(body, 17, 5, carry, first, last), name=name,
        grid_spec=pltpu.PrefetchScalarGridSpec(
            num_scalar_prefetch=4, grid=(nhp, nsteps),
            in_specs=[qa_specs[0], kv, kv, pl.BlockSpec((tk, LANES), lambda h, p, jt, ia, ib, vb: (jt[p], 0))]
                     + qa_specs[1:] + qb_specs + [ANY] * len(carry),
            out_specs=[pl.BlockSpec((t, hps * hd), lambda h, p, jt, ia, ib, vb: (0, h)), kv, kv,
                       pl.BlockSpec((hps, tk, LANES), lambda h, p, jt, ia, ib, vb: (h, jt[p], 0)),
                       pl.BlockSpec((hps, nq, 8, tq), lambda h, p, jt, ia, ib, vb: (h, 0, 0, 0))]
                      + [ANY] * len(carry),
            scratch_shapes=[pltpu.VMEM((hps, tk, hd), F32), pltpu.VMEM((hps, tk, hd), F32),
                            pltpu.VMEM((hps, tk, LANES), F32)] + _carry_scratch(carry)),
        out_shape=[jax.ShapeDtypeStruct((t, d), F32), jax.ShapeDtypeStruct((t, d), BF16),
                   jax.ShapeDtypeStruct((t, d), BF16), jax.ShapeDtypeStruct((N_HEADS, t, LANES), F32),
                   jax.ShapeDtypeStruct((N_HEADS, nq, 8, tq), F32)] + _carry_shapes(carry),
        compiler_params=_params(("arbitrary", "arbitrary"), VMEM_BIG),
    )(jt, ia, ib, vb, q, k, v, c, ct, lse, delta, dot_t, *q_args, *[a for _, a in carry])


def _adamw(w, g, m, v, name):
    r, c = w.shape
    tr = r
    for cand in (256, 128, 64, 32, 16, 8):
        if r % cand == 0 and r > cand:
            tr = cand
            break
    bc1 = 1.0 - ADAM_B1 ** ADAM_STEP
    bc2 = 1.0 - ADAM_B2 ** ADAM_STEP

    def body(w_ref, g_ref, m_ref, v_ref, d_ref, nm_ref, nv_ref):
        gg = g_ref[...]
        nm = ADAM_B1 * m_ref[...] + (1.0 - ADAM_B1) * gg
        nv = ADAM_B2 * v_ref[...] + (1.0 - ADAM_B2) * (gg * gg)
        d_ref[...] = -ADAM_LR * ((nm / bc1) / (jnp.sqrt(nv / bc2) + ADAM_EPS) + ADAM_WD * w_ref[...])
        nm_ref[...] = nm
        nv_ref[...] = nv

    blk = pl.BlockSpec((tr, c), lambda i: (i, 0))
    shp = jax.ShapeDtypeStruct((r, c), F32)
    return pl.pallas_call(
        body, name=name, grid=(r // tr,), in_specs=[blk] * 4, out_specs=[blk] * 3,
        out_shape=[shp] * 3, compiler_params=_params(("arbitrary",), VMEM_MID),
    )(w, g, m, v)


def _reduce_adamw(w, m, v, landed, name):
    nl, r, c = w.shape
    tr = next(cand for cand in range(min(r, ADAM_ROWS_MAX), 0, -BF16_ROWS) if r % cand == 0)
    nr = r // tr
    bc1 = 1.0 - ADAM_B1 ** ADAM_STEP
    bc2 = 1.0 - ADAM_B2 ** ADAM_STEP

    def body(*refs):
        w_ref, m_ref, v_ref = refs[:3]
        src_refs = refs[3:3 + nl]
        g_ref, d_ref, nm_ref, nv_ref = refs[3 + nl:]

        def update(src):
            gg = src[0].astype(F32)
            for s in range(1, N_DEV):
                gg = gg + src[s].astype(F32)
            nm = ADAM_B1 * m_ref[0] + (1.0 - ADAM_B1) * gg
            nv = ADAM_B2 * v_ref[0] + (1.0 - ADAM_B2) * (gg * gg)
            g_ref[0] = gg
            d_ref[0] = -ADAM_LR * ((nm / bc1) / (jnp.sqrt(nv / bc2) + ADAM_EPS) + ADAM_WD * w_ref[0])
            nm_ref[0] = nm
            nv_ref[0] = nv

        for idx in range(nl):
            pl.when(pl.program_id(0) == idx)(functools.partial(update, src_refs[idx]))

    def src_spec(idx):
        return pl.BlockSpec((N_DEV, tr, c),
                            lambda l, i: (0, jnp.where(l == idx, i, jnp.where(l < idx, 0, nr - 1)), 0))

    blk = pl.BlockSpec((1, tr, c), lambda l, i: (l, i, 0))
    shp = jax.ShapeDtypeStruct((nl, r, c), F32)
    return pl.pallas_call(
        body, name=name, grid=(nl, nr), in_specs=[blk] * 3 + [src_spec(idx) for idx in range(nl)],
        out_specs=[blk] * 4, out_shape=[shp] * 4,
        compiler_params=_params(("arbitrary", "arbitrary"), VMEM_MID),
    )(w, m, v, *landed)


def _sum_sources(r, name):
    n, rows, c = r.shape
    tr = next(cand for cand in range(min(rows, SUM_ROWS_MAX), 0, -BF16_ROWS) if rows % cand == 0)

    def body(r_ref, o_ref):
        acc = r_ref[0].astype(F32)
        for s in range(1, n):
            acc = acc + r_ref[s].astype(F32)
        o_ref[...] = acc

    return pl.pallas_call(
        body, name=name, grid=(rows // tr,),
        in_specs=[pl.BlockSpec((n, tr, c), lambda i: (0, i, 0))],
        out_specs=pl.BlockSpec((tr, c), lambda i: (i, 0)),
        out_shape=jax.ShapeDtypeStruct((rows, c), F32),
        compiler_params=_params(("arbitrary",), VMEM_MID),
    )(r)


def _all_gather(parts, name):
    n = len(parts)

    def body(*refs):
        x_refs, out_refs = refs[:n], refs[n:2 * n]
        send_sems, recv_sems, local_sems = refs[2 * n:]
        mx, my, mc = lax.axis_index("x"), lax.axis_index("y"), lax.axis_index("c")
        me, sibling = (mx, my, mc), (mx, my, 1 - mc)
        chips = [(1 - mx, my), (mx, 1 - my), (1 - mx, 1 - my)]

        def copy(p, k, block, to, from_input=False):
            px, py, pc = block
            rows = out_refs[p].at[4 * px + 2 * py + pc]
            return pltpu.make_async_remote_copy(
                src_ref=x_refs[p] if from_input else rows, dst_ref=rows,
                send_sem=send_sems.at[7 * p + k], recv_sem=recv_sems.at[7 * p + k],
                device_id=to, device_id_type=MESH)

        mine, sent = [], []
        for p in range(n):
            own = pltpu.make_async_copy(x_refs[p], out_refs[p].at[4 * mx + 2 * my + mc], local_sems.at[p])
            own.start()
            mine.append(own)
            first = [copy(p, 0, me, sibling, True)]
            first += [copy(p, 1 + j, me, (*chip, mc), True) for j, chip in enumerate(chips)]
            for cp in first:
                cp.start()
            sent += first
        for p in range(n):
            for j, chip in enumerate(chips):
                copy(p, 1 + j, (*chip, mc), me).wait_recv()
                fwd = copy(p, 4 + j, (*chip, mc), sibling)
                fwd.start()
                sent.append(fwd)
        for p in range(n):
            copy(p, 0, sibling, me).wait_recv()
            for j, chip in enumerate(chips):
                copy(p, 4 + j, (*chip, 1 - mc), me).wait_recv()
        for cp in sent:
            cp.wait_send()
        for own in mine:
            own.wait()

    return pl.pallas_call(
        body, name=name, in_specs=[ANY] * n, out_specs=[ANY] * n,
        out_shape=[jax.ShapeDtypeStruct((N_DEV,) + a.shape, a.dtype) for a in parts],
        scratch_shapes=[pltpu.SemaphoreType.DMA((7 * n,)), pltpu.SemaphoreType.DMA((7 * n,)),
                        pltpu.SemaphoreType.DMA((n,))],
    )(*parts)


def _pack_rows(parts, width, mult, lead=0):
    out = []
    for a in parts:
        head = a.shape[:lead]
        flat = a.reshape(head + (-1,))
        padn = (-flat.shape[-1]) % (width * mult)
        if padn:
            flat = jnp.pad(flat, [(0, 0)] * lead + [(0, padn)])
        out.append(flat.reshape(head + (-1, width)))
    return jnp.concatenate(out, axis=lead)


def _rows_of(shape, width, mult):
    n = math.prod(shape)
    per = width * mult
    return ((n + per - 1) // per) * mult


def _unpack_rows(buf, shapes, width, mult):
    lead = buf.shape[:-2]
    out, off = [], 0
    for shp in shapes:
        r = _rows_of(shp, width, mult)
        flat = buf[..., off:off + r, :].reshape(lead + (r * width,))
        out.append(flat[..., :math.prod(shp)].reshape(lead + tuple(shp)))
        off += r
    return out


def _cols_from_devices(g):
    nd = g.ndim
    perm = tuple(range(1, nd - 1)) + (0, nd - 1)
    t = jnp.transpose(g, perm)
    return t.reshape(t.shape[:-2] + (t.shape[-2] * t.shape[-1],))


def _cols_to_devices(a):
    c = a.shape[-1] // N_DEV
    t = a.reshape(a.shape[:-1] + (N_DEV, c))
    nd = t.ndim
    perm = (nd - 2,) + tuple(range(0, nd - 2)) + (nd - 1,)
    return jnp.transpose(t, perm)


WIDTH = 1024


def kernel(x, meta, ffn1_wg, ffn1_wu, ffn1_wd, ffn2_wg, ffn2_wu, ffn2_wd, ln_gain, ln_bias, conv_w_in, conv_w, conv_w_out, kv_w, f_bias, attn_w_q, attn_w_o, loss_target, m_meta, m_ffn1_wg, m_ffn1_wu, m_ffn1_wd, m_ffn2_wg, m_ffn2_wu, m_ffn2_wd, m_ln_gain, m_ln_bias, m_conv_w_in, m_conv_w, m_conv_w_out, m_kv_w, m_f_bias, m_attn_w_q, m_attn_w_o, v_meta, v_ffn1_wg, v_ffn1_wu, v_ffn1_wd, v_ffn2_wg, v_ffn2_wu, v_ffn2_wd, v_ln_gain, v_ln_bias, v_conv_w_in, v_conv_w, v_conv_w_out, v_kv_w, v_f_bias, v_attn_w_q, v_attn_w_o):
    depth = ln_gain.shape[0]
    alpha = float((2 * depth) ** 0.25)
    d = x.shape[-1]
    seq = x.shape[1]
    t = ROW0 + seq
    fsh = ffn1_wg.shape[-1]
    f = fsh * N_DEV
    fck = MXU_COLS
    nc = f // fck
    me = 4 * lax.axis_index("x") + 2 * lax.axis_index("y") + lax.axis_index("c")

    def gather_of(parts):
        return [(True, a.astype(BF16)) for a in parts]

    small = [meta, ln_gain, ln_bias, conv_w]
    small_shapes = [a.shape for a in small]
    g1g, g1u, g1d, gcin, gsmall = _all_gather(
        [a.astype(BF16) for a in (ffn1_wg[0], ffn1_wu[0], ffn1_wd[0], conv_w_in[0])]
        + [_pack_rows(small, WIDTH, F32_ROWS)], "ag_first")
    gmeta, ggain, gbias, gcw = _unpack_rows(gsmall, small_shapes, WIDTH, F32_ROWS)

    def ffn_chunks(gg, gu, gd):
        up = lambda g: jnp.transpose(_cols_from_devices(g).reshape(d, nc, fck), (1, 0, 2))
        return up(gg), up(gu), gd.reshape(nc, fck, d)

    w_in = _cols_from_devices(gcin)
    fb =jnp.pad(f_bias, (0, LANES - N_HEADS)).reshape(1, LANES)
    meta_f = _cols_from_devices(gmeta)
    gain_f = _cols_from_devices(ggain)
    bias_f = _cols_from_devices(gbias)
    cw_f = _cols_from_devices(gcw)[0]

    def gb(l, n):
        return gain_f[l, n].reshape(1, d), bias_f[l, n].reshape(1, d)

    ones = jnp.ones((1, d), F32)
    zeros = jnp.zeros((1, d), F32)

    h0 = jnp.concatenate([jnp.zeros((PAD, d), F32), meta_f, x[0]], axis=0)

    w1 = ffn_chunks(g1g, g1u, g1d)
    g00, b00 = gb(0, 0)
    xh1, rs1, hb1, gg1, uu1, hb0, gcout, g2g, g2u = _ffn_fwd(
        h0, ones, zeros, *w1, g00, b00, alpha, "ffn_fwd_0a",
        carry=gather_of([conv_w_out[0], ffn2_wg[0], ffn2_wu[0]]), input_t=True)
    w_out = gcout.reshape(d, d)
    g01, b01 = gb(0, 1)
    xh2, rs2, hb2, pp, mb, g2d, gkv = _conv_fwd(
        xh1, g00, b00, w_in, cw_f, w_out, g01, b01, alpha, "conv_fwd", carry=gather_of([ffn2_wd[0], kv_w.T]))
    w2 = ffn_chunks(g2g, g2u, g2d)
    g02, b02 = gb(0, 2)
    xh3, rs3, hb3, gg3, uu3, g3g, g3u = _ffn_fwd(
        xh2, g01, b01, *w2, g02, b02, alpha, "ffn_fwd_0b", carry=gather_of([ffn1_wg[1], ffn1_wu[1]]))
    kvw = gkv.reshape(gkv.shape[0] * gkv.shape[1], d).T
    wk, wv = kvw[:, :d], kvw[:, d:2 * d]
    wf = jnp.pad(kvw[:, 2 * d:], ((0, 0), (0, LANES - N_HEADS)))
    kk, vv, logit, cc, cct, g3d = _kv_fwd(xh3, g02, b02, wk, wv, wf, fb, "kv_fwd",
                                          carry=gather_of([ffn1_wd[1]]))

    w3 = ffn_chunks(g3g, g3u, g3d)
    g10, b10 = gb(1, 0)
    xh4, rs4, hb4, gg4, uu4, gwq, g4g, g4u = _ffn_fwd(
        xh3, g02, b02, *w3, g10, b10, alpha, "ffn_fwd_1a", carry=gather_of([attn_w_q[0], ffn2_wg[1], ffn2_wu[1]]))
    w_q = gwq.reshape(d, d)
    qq = _proj(xh4, g10, b10, w_q, "q_proj")
    ot, lse, gwo, g4d = _attn_fwd(
        qq, kk, vv, cc, cct, "attn_fwd", carry=gather_of([attn_w_o[0], ffn2_wd[1]]))
    w_o = gwo.reshape(d, d)
    g11, b11 = gb(1, 1)
    xh5, rs5, hb5 = _attn_out_fwd(ot, xh4, g10, b10, w_o, g11, b11, alpha, "attn_out_fwd")
    w4 = ffn_chunks(g4g, g4u, g4d)
    g12, b12 = gb(1, 2)
    xh6, rs6, _, gg6, uu6 = _ffn_fwd(xh5, g11, b11, *w4, g12, b12, alpha, "ffn_fwd_1b")


    dgain = [[None] * 3 for _ in range(depth)]
    dbias = [[None] * 3 for _ in range(depth)]

    def to_col_owners(g):
        return (False, _cols_to_devices(g).astype(BF16))

    def to_row_owners(g):
        return (False, g.reshape(N_DEV, g.shape[0] // N_DEV, g.shape[1]).astype(BF16))

    dh5, do6, dg6, du6, a6, dgain[1][2], dbias[1][2], loss_l = _ffn_bwd(
        None, xh6, rs6, g12, gg6, uu6, *w4, alpha, "ffn_bwd_1b", loss_target=loss_target[0], loss_bias=b12)
    loss = lax.psum(loss_l[0, 0], ("x", "y", "c"))
    dw4g, dw4u = _wgrad(hb5, [dg6, du6], "wgrad_up_1b")
    (dw4dt,) = _wgrad(do6, [a6], "wgrad_down_1b")

    dres4, dmix5, dot_t, delta, dgain[1][1], dbias[1][1] = _attn_out_bwd(dh5, xh5, rs5, g11, ot, w_o, alpha, "attn_out_bwd")
    (dwo,) = _wgrad(ot, [dmix5], "wgrad_wo")
    dq, dkk, dvv, dcs, drow = _attn_bwd(qq, kk, vv, cc, cct, lse, delta, dot_t, "attn_bwd")
    dh4 = _add_proj_nt(dres4, dq, w_q, "q_bwd")
    (dwq,) = _wgrad(hb4, [dq], "wgrad_wq")

    dh3a, do4, dg4, du4, a4, dgain[1][0], dbias[1][0], l4g, l4u, l4d, lwo = _ffn_bwd(
        dh4, xh4, rs4, g10, gg4, uu4, *w3, alpha, "ffn_bwd_1a",
        carry=[to_col_owners(dw4g), to_col_owners(dw4u), to_row_owners(dw4dt.T), to_row_owners(dwo)])
    dw3g, dw3u = _wgrad(hb3, [dg4, du4], "wgrad_up_1a")
    (dw3dt,) = _wgrad(do4, [a4], "wgrad_down_1a")

    dcq = jnp.pad(drow[:, :, 0, :].reshape(N_HEADS, t).T, ((0, 0), (0, LANES - N_HEADS)))
    dh3, dlogit, dfb = _kv_bwd(dkk, dvv, dcs, dcq, logit, dh3a, wk, wv, wf, "kv_bwd")
    dwk, dwv = _wgrad(hb3, [dkk, dvv], "wgrad_kv")
    (dwf,) = _wgrad(hb3, [dlogit], "wgrad_f")
    dkv = jnp.concatenate([dwk, dwv, dwf[:, :N_HEADS]], axis=1)

    dh2, do3, dg3, du3, a3, dgain[0][2], dbias[0][2], lwq, l3g, l3u, l3d, lkv = _ffn_bwd(
        dh3, xh3, rs3, g02, gg3, uu3, *w2, alpha, "ffn_bwd_0b",
        carry=[to_row_owners(dwq), to_col_owners(dw3g), to_col_owners(dw3u), to_row_owners(dw3dt.T),
               to_row_owners(dkv.T)])
    dw2g, dw2u = _wgrad(hb2, [dg3, du3], "wgrad_up_0b")
    (dw2dt,) = _wgrad(do3, [a3], "wgrad_down_0b")

    dh1, dmix2, dpp, dcw, dgain[0][1], dbias[0][1] = _conv_bwd(dh2, xh2, rs2, g01, pp, cw_f, w_in, w_out, alpha, "conv_bwd")
    (dwin,) = _wgrad(hb1, [dpp], "wgrad_conv_in")
    (dwout,) = _wgrad(mb, [dmix2], "wgrad_conv_out")

    dh0, do1, dg1, du1, a1, dgain[0][0], dbias[0][0], l2g, l2u, l2d, lcin, lcout = _ffn_bwd(
        dh1, xh1, rs1, g00, gg1, uu1, *w1, alpha, "ffn_bwd_0a",
        carry=[to_col_owners(dw2g), to_col_owners(dw2u), to_row_owners(dw2dt.T), to_col_owners(dwin),
               to_row_owners(dwout)])
    (dw1dt,) = _wgrad(do1, [a1], "wgrad_down_0a")
    dw1g, l1d = _wgrad(hb0, [dg1], "wgrad_upg_0a", carry=[to_row_owners(dw1dt.T)])
    dw1u, l1g = _wgrad(hb0, [du1], "wgrad_upu_0a", carry=[to_col_owners(dw1g)])
    dmeta = dh0[PAD:ROW0]
    dgain_f = jnp.stack([jnp.concatenate(r, axis=0) for r in dgain])
    dbias_f = jnp.stack([jnp.concatenate(r, axis=0) for r in dbias])
    small_full = [dmeta, dgain_f, dbias_f, dcw[None], dfb]
    small_full_shapes = [a.shape for a in small_full]
    l1u, gsmall_grads = _exchange([to_col_owners(dw1u), (True, _pack_rows(small_full, WIDTH, F32_ROWS))], "rs_last")

    grad_x = dh0[ROW0:].reshape(1, seq, d)
    rsmall = _sum_sources(gsmall_grads, "small_sum")
    smeta, sgain, sbias, scw, sfb = _unpack_rows(rsmall, small_full_shapes, WIDTH, F32_ROWS)
    csh = d // N_DEV

    def my_cols(a):
        return lax.dynamic_slice_in_dim(a, me * csh, csh, axis=a.ndim - 1)

    grads = {"meta": my_cols(smeta), "ln_gain": my_cols(sgain), "ln_bias": my_cols(sbias),
             "conv_w": my_cols(scw), "f_bias": sfb[0, :N_HEADS], "kv_w": _sum_sources(lkv, "kv_sum").T}
    landed = {"ffn1_wg": [l1g, l3g], "ffn1_wu": [l1u, l3u], "ffn1_wd": [l1d, l3d],
              "ffn2_wg": [l2g, l4g], "ffn2_wu": [l2u, l4u], "ffn2_wd": [l2d, l4d],
              "conv_w_in": [lcin], "conv_w_out": [lcout], "attn_w_q": [lwq], "attn_w_o": [lwo]}
    weights = dict(meta=meta, ffn1_wg=ffn1_wg, ffn1_wu=ffn1_wu, ffn1_wd=ffn1_wd, ffn2_wg=ffn2_wg,
                   ffn2_wu=ffn2_wu, ffn2_wd=ffn2_wd, ln_gain=ln_gain, ln_bias=ln_bias,
                   conv_w_in=conv_w_in, conv_w=conv_w, conv_w_out=conv_w_out, kv_w=kv_w,
                   f_bias=f_bias, attn_w_q=attn_w_q, attn_w_o=attn_w_o)
    moms = dict(meta=(m_meta, v_meta), ffn1_wg=(m_ffn1_wg, v_ffn1_wg), ffn1_wu=(m_ffn1_wu, v_ffn1_wu),
                ffn1_wd=(m_ffn1_wd, v_ffn1_wd), ffn2_wg=(m_ffn2_wg, v_ffn2_wg), ffn2_wu=(m_ffn2_wu, v_ffn2_wu),
                ffn2_wd=(m_ffn2_wd, v_ffn2_wd), ln_gain=(m_ln_gain, v_ln_gain), ln_bias=(m_ln_bias, v_ln_bias),
                conv_w_in=(m_conv_w_in, v_conv_w_in), conv_w=(m_conv_w, v_conv_w),
                conv_w_out=(m_conv_w_out, v_conv_w_out), kv_w=(m_kv_w, v_kv_w), f_bias=(m_f_bias, v_f_bias),
                attn_w_q=(m_attn_w_q, v_attn_w_q), attn_w_o=(m_attn_w_o, v_attn_w_o))

    names = list(weights)
    g_out, d_out, m_out, v_out = [], [], [], []
    for n in names:
        w = weights[n]
        shp = w.shape
        mm, vv_ = moms[n]
        if n in landed:
            three = (len(landed[n]),) + shp[-2:]
            g, dl, nm, nv = _reduce_adamw(w.reshape(three), mm.reshape(three), vv_.reshape(three),
                                          landed[n], "adamw_" + n)
            g = g.reshape(shp)
        else:
            two = (1, shp[0]) if w.ndim == 1 else (math.prod(shp[:-1]), shp[-1])
            g = grads[n].reshape(shp)
            dl, nm, nv = _adamw(w.reshape(two), g.reshape(two), mm.reshape(two), vv_.reshape(two), "adamw_" + n)
        g_out.append(g)
        d_out.append(dl.reshape(shp))
        m_out.append(nm.reshape(shp))
        v_out.append(nv.reshape(shp))
    return (loss, grad_x, *g_out, *d_out, *m_out, *v_out)
```

```python
import functools
import math

import jax
import jax.numpy as jnp
from jax import lax
from jax.experimental import pallas as pl
from jax.experimental.pallas import tpu as pltpu

F32 = jnp.float32
BF16 = jnp.bfloat16

N_DEV = 8
N_HEADS = 8
N_META = 16
PAD = 112
ROW0 = PAD + N_META
LN_EPS = 1e-5
NEG_INF = -1e30
LOG2E = 1.4426950408889634
ATTN_HEADS_PER_STEP = 8
ATTN_BWD_HEADS_PER_STEP = 2
LANES = 128
MXU_COLS = 256
FFN_FWD_CHUNKS = 11
FFN_BWD_CHUNKS = 4

ADAM_LR = 0.001
ADAM_B1 = 0.9
ADAM_B2 = 0.999
ADAM_EPS = 1e-08
ADAM_WD = 0.01
ADAM_STEP = 10

ROW_TILES = (640, 128)
LOSS_TILE = 128
BF16_ROWS = 16
F32_ROWS = 8
SUM_ROWS_MAX = 768
ADAM_ROWS_MAX = 256
VMEM_BIG = 56 << 20
VMEM_MID = 40 << 20

ANY = pl.BlockSpec(memory_space=pl.ANY)
MESH = pl.DeviceIdType.MESH


def _row_tile(t):
    for c in ROW_TILES:
        if t % c == 0:
            return c
    raise ValueError(f"no row tile for {t}")


def _dot(a, b):
    return jnp.dot(a, b, preferred_element_type=F32)


def _dot_nt(a, b):
    return lax.dot_general(a, b, (((1,), (1,)), ((), ())), preferred_element_type=F32)


def _dot_tn(a, b):
    return lax.dot_general(a, b, (((0,), (0,)), ((), ())), preferred_element_type=F32)


def _params(sem, vmem):
    return pltpu.CompilerParams(dimension_semantics=sem, vmem_limit_bytes=vmem)


def _ln_fwd(z):
    mu = jnp.mean(z, axis=-1, keepdims=True)
    zc = z - mu
    var = jnp.mean(zc * zc, axis=-1, keepdims=True)
    rstd = lax.rsqrt(var + LN_EPS)
    return zc * rstd, rstd


def _ln_bwd(dh, xhat, rstd, gain):
    dxh = dh * gain
    m1 = jnp.mean(dxh, axis=-1, keepdims=True)
    m2 = jnp.mean(dxh * xhat, axis=-1, keepdims=True)
    dz = rstd * (dxh - m1 - xhat * m2)
    return dz, jnp.sum(dh * xhat, axis=0, keepdims=True), jnp.sum(dh, axis=0, keepdims=True)


def _load_resident(pairs, sems):
    cps = [pltpu.make_async_copy(src, dst, sems.at[k]) for k, (src, dst) in enumerate(pairs)]
    for cp in cps:
        cp.start()
    for cp in cps:
        cp.wait()


def _peer_ids():
    mx, my, mc = lax.axis_index("x"), lax.axis_index("y"), lax.axis_index("c")
    peers = []
    for kk in range(1, N_DEV):
        px = 1 - mx if (kk >> 2) & 1 else mx
        py = 1 - my if (kk >> 1) & 1 else my
        pc = 1 - mc if kk & 1 else mc
        peers.append(((px, py, pc), 4 * px + 2 * py + pc))
    return 4 * mx + 2 * my + mc, peers


def _exchange_copies(jobs, send_sems, recv_sems, local_sems, starting):
    me_id, peers = _peer_ids()
    for n, (gather, src, dst) in enumerate(jobs):
        own = pltpu.make_async_copy(src if gather else src.at[me_id], dst.at[me_id], local_sems.at[n])
        own.start() if starting else own.wait()
        for k, (dev, pid) in enumerate(peers):
            sem = (N_DEV - 1) * n + k
            out = src if gather else src.at[pid]
            send = pltpu.make_async_remote_copy(
                src_ref=out, dst_ref=dst.at[me_id], send_sem=send_sems.at[sem], recv_sem=recv_sems.at[sem],
                device_id=dev, device_id_type=MESH)
            if starting:
                send.start()
            else:
                pltpu.make_async_remote_copy(
                    src_ref=out, dst_ref=dst.at[pid], send_sem=send_sems.at[sem], recv_sem=recv_sems.at[sem],
                    device_id=dev, device_id_type=MESH).wait_recv()
                send.wait_send()


def _carried(body, n_in, n_out, carry, first, last):
    nj = len(carry)
    if nj == 0:
        return body

    def wrapped(*refs):
        ins, srcs = refs[:n_in], refs[n_in:n_in + nj]
        outs = refs[n_in + nj:n_in + nj + n_out]
        dsts = refs[n_in + nj + n_out:n_in + 2 * nj + n_out]
        scratch, sems = refs[n_in + 2 * nj + n_out:-3], refs[-3:]
        jobs = [(g, s, r) for (g, _), s, r in zip(carry, srcs, dsts)]

        @pl.when(first())
        def _():
            _exchange_copies(jobs, *sems, starting=True)

        body(*ins, *outs, *scratch)

        @pl.when(last())
        def _():
            _exchange_copies(jobs, *sems, starting=False)

    return wrapped


def _carry_shapes(carry):
    return [jax.ShapeDtypeStruct((N_DEV,) + a.shape if g else a.shape, a.dtype) for g, a in carry]


def _carry_scratch(carry):
    if not carry:
        return []
    n = len(carry)
    return [pltpu.SemaphoreType.DMA(((N_DEV - 1) * n,)), pltpu.SemaphoreType.DMA(((N_DEV - 1) * n,)),
            pltpu.SemaphoreType.DMA((n,))]


def _exchange(carry, name):
    n = len(carry)

    def body(*refs):
        jobs = [(g, s, r) for (g, _), s, r in zip(carry, refs[:n], refs[n:2 * n])]
        _exchange_copies(jobs, *refs[2 * n:], starting=True)
        _exchange_copies(jobs, *refs[2 * n:], starting=False)

    return pl.pallas_call(
        body, name=name, in_specs=[ANY] * n, out_specs=[ANY] * n, out_shape=_carry_shapes(carry),
        scratch_shapes=_carry_scratch(carry),
    )(*[a for _, a in carry])


def _ffn_fwd(xh, gi, bi, wg, wu, wd, go, bo, alpha, name, carry=(), input_t=False):
    t, d = xh.shape
    nch, _, fc = wg.shape
    f = nch * fc
    per = min(FFN_FWD_CHUNKS, nch)
    nc = -(-nch // per)
    tm = _row_tile(t)
    nt = t // tm

    def body(xh_ref, gi_ref, bi_ref, wg_hbm, wu_hbm, wd_hbm, go_ref, bo_ref,
             xo_ref, rs_ref, hb_ref, g_ref, u_ref, *tail):
        hin_ref = tail[0] if input_t else None
        wg_v, wu_v, wd_v, acc, hbs, sems = tail[1:] if input_t else tail
        i = pl.program_id(0)
        c = pl.program_id(1)

        @pl.when((i == 0) & (c == 0))
        def _():
            _load_resident([(wg_hbm, wg_v), (wu_hbm, wu_v), (wd_hbm, wd_v)], sems)

        @pl.when(c == 0)
        def _():
            h = xh_ref[...] * gi_ref[...] + bi_ref[...]
            hbs[...] = h.astype(BF16)
            acc[...] = jnp.zeros_like(acc)
            if input_t:
                hin_ref[...] = hbs[...].T

        def chunk(k):
            ck = c * per + k
            cols = slice(k * fc, (k + 1) * fc)
            hb = hbs[...]
            g = _dot(hb, wg_v[ck])
            u = _dot(hb, wu_v[ck])
            a = (g * jax.nn.sigmoid(g)) * u
            g_ref[:, cols] = g.astype(BF16)
            u_ref[:, cols] = u.astype(BF16)
            acc[...] += _dot(a.astype(BF16), wd_v[ck])

        for k in range(per):
            if (nc - 1) * per + k < nch:
                chunk(k)
            else:
                pl.when(c * per + k < nch)(functools.partial(chunk, k))

        @pl.when(c == nc - 1)
        def _():
            h = xh_ref[...] * gi_ref[...] + bi_ref[...]
            xhat, rstd = _ln_fwd(alpha * h + 0.5 * acc[...])
            xo_ref[...] = xhat
            rs_ref[...] = rstd
            hb_ref[...] = (xhat * go_ref[...] + bo_ref[...]).astype(BF16).T

    row = pl.BlockSpec((tm, d), lambda i, c: (i, 0))
    vec = pl.BlockSpec((1, d), lambda i, c: (0, 0))
    chunk = pl.BlockSpec((tm, per * fc), lambda i, c: (i, c))
    first = lambda: (pl.program_id(0) == 0) & (pl.program_id(1) == 0)
    last = lambda: (pl.program_id(0) == nt - 1) & (pl.program_id(1) == nc - 1)
    col = pl.BlockSpec((d, tm), lambda i, c: (0, i))
    t_spec, t_shape = ([col], [jax.ShapeDtypeStruct((d, t), BF16)]) if input_t else ([], [])
    return pl.pallas_call(
        _carried(body, 8, 5 + len(t_spec), carry, first, last), name=name, grid=(nt, nc),
        in_specs=[row, vec, vec, ANY, ANY, ANY, vec, vec] + [ANY] * len(carry),
        out_specs=[row, pl.BlockSpec((tm, 1), lambda i, c: (i, 0)), col, chunk, chunk] + t_spec
                  + [ANY] * len(carry),
        out_shape=[jax.ShapeDtypeStruct((t, d), F32), jax.ShapeDtypeStruct((t, 1), F32),
                   jax.ShapeDtypeStruct((d, t), BF16), jax.ShapeDtypeStruct((t, f), BF16),
                   jax.ShapeDtypeStruct((t, f), BF16)] + t_shape + _carry_shapes(carry),
        scratch_shapes=[pltpu.VMEM((nch, d, fc), BF16), pltpu.VMEM((nch, d, fc), BF16),
                        pltpu.VMEM((nch, fc, d), BF16), pltpu.VMEM((tm, d), F32),
                        pltpu.VMEM((tm, d), BF16), pltpu.SemaphoreType.DMA((3,))] + _carry_scratch(carry),
        compiler_params=_params(("arbitrary", "arbitrary"), VMEM_BIG),
    )(xh, gi, bi, wg, wu, wd, go, bo, *[a for _, a in carry])


def _ffn_bwd(dh, xo, rs, go, gs, us, wg, wu, wd, alpha, name, carry=(), loss_target=None, loss_bias=None):
    t, d = xo.shape
    nch, _, fc = wg.shape
    f = nch * fc
    per = min(FFN_BWD_CHUNKS, nch)
    nc = -(-nch // per)
    tm = _row_tile(t)
    nt = t // tm

    with_loss = loss_target is not None
    nsub, lead = tm // LOSS_TILE, ROW0 // LOSS_TILE
    nlead = nsub + 1 if with_loss else 1

    def body(*refs):
        lead_refs = refs[:nlead]
        xo_ref, rs_ref, go_ref, g_ref, u_ref, wg_hbm, wu_hbm, wd_hbm = refs[nlead:nlead + 8]
        dhin_ref, dot_ref, dg_ref, du_ref, a_ref, dgain_ref, dbias_ref = refs[nlead + 8:nlead + 15]
        rest = refs[nlead + 15:]
        loss_ref, rest = (rest[0], rest[1:]) if with_loss else (None, rest)
        wg_v, wu_v, wd_v, do_ref, sems = rest[:5]
        i = pl.program_id(0)
        c = pl.program_id(1)

        @pl.when((i == 0) & (c == 0))
        def _():
            _load_resident([(wg_hbm, wg_v), (wu_hbm, wu_v), (wd_hbm, wd_v)], sems)
            dgain_ref[...] = jnp.zeros_like(dgain_ref)
            dbias_ref[...] = jnp.zeros_like(dbias_ref)
            if with_loss:
                rest[5][...] = jnp.zeros_like(rest[5])

        def tile_dh():
            if not with_loss:
                return lead_refs[0][...]
            part = rest[5]
            for k in range(nsub):
                sl = slice(k * LOSS_TILE, (k + 1) * LOSS_TILE)
                rows = i * tm + k * LOSS_TILE + lax.broadcasted_iota(jnp.int32, (LOSS_TILE, 1), 0)
                y = xo_ref[sl, :] * go_ref[...] + lead_refs[nsub][...]
                e = jnp.where(rows >= ROW0, y - lead_refs[k][...], 0.0)
                part[...] += jnp.sum(e * e, axis=0, keepdims=True)
                dhin_ref[sl, :] = e * (1.0 / d)

            @pl.when(i == nt - 1)
            def _():
                loss_ref[...] = jnp.full((1, LANES), 0.5 / d, F32) * jnp.sum(part[...])

            return dhin_ref[...]

        @pl.when(c == 0)
        def _():
            dz, dgp, dbp = _ln_bwd(tile_dh(), xo_ref[...], rs_ref[...], go_ref[...])
            dgain_ref[...] += dgp
            dbias_ref[...] += dbp
            dob = (0.5 * dz).astype(BF16)
            do_ref[...] = dob
            dot_ref[...] = dob.T
            dhin_ref[...] = alpha * dz

        def chunk(k):
            ck = c * per + k
            cols = slice(k * fc, (k + 1) * fc)
            g = g_ref[:, cols].astype(F32)
            u = u_ref[:, cols].astype(F32)
            sg = jax.nn.sigmoid(g)
            sl = g * sg
            da = _dot_nt(do_ref[...], wd_v[ck])
            dgb = (da * u * (sg * (1.0 + g * (1.0 - sg)))).astype(BF16)
            dub = (da * sl).astype(BF16)
            a_ref[:, cols] = (sl * u).astype(BF16)
            dg_ref[:, cols] = dgb
            du_ref[:, cols] = dub
            dhin_ref[...] += _dot_nt(dgb, wg_v[ck]) + _dot_nt(dub, wu_v[ck])

        for k in range(per):
            if (nc - 1) * per + k < nch:
                chunk(k)
            else:
                pl.when(c * per + k < nch)(functools.partial(chunk, k))

    row = pl.BlockSpec((tm, d), lambda i, c: (i, 0))
    vec = pl.BlockSpec((1, d), lambda i, c: (0, 0))
    chunk = pl.BlockSpec((tm, per * fc), lambda i, c: (i, c))
    first = lambda: (pl.program_id(0) == 0) & (pl.program_id(1) == 0)
    last = lambda: (pl.program_id(0) == nt - 1) & (pl.program_id(1) == nc - 1)
    if with_loss:
        lead_specs = [pl.BlockSpec((LOSS_TILE, d), lambda i, c, k=k: (jnp.maximum(i * nsub + k - lead, 0), 0))
                      for k in range(nsub)] + [vec]
        lead_args = [loss_target] * nsub + [loss_bias]
        loss_spec, loss_shape = [pl.BlockSpec((1, LANES), lambda i, c: (0, 0))], [jax.ShapeDtypeStruct((1, LANES), F32)]
        loss_scratch = [pltpu.VMEM((1, d), F32)]
    else:
        lead_specs, lead_args, loss_spec, loss_shape, loss_scratch = [row], [dh], [], [], []
    return pl.pallas_call(
        _carried(body, nlead + 8, 7 + len(loss_spec), carry, first, last), name=name, grid=(nt, nc),
        in_specs=lead_specs + [row, pl.BlockSpec((tm, 1), lambda i, c: (i, 0)), vec, chunk, chunk,
                               ANY, ANY, ANY] + [ANY] * len(carry),
        out_specs=[row, pl.BlockSpec((d, tm), lambda i, c: (0, i)), chunk, chunk, chunk, vec, vec]
                  + loss_spec + [ANY] * len(carry),
        out_shape=[jax.ShapeDtypeStruct((t, d), F32), jax.ShapeDtypeStruct((d, t), BF16),
                   jax.ShapeDtypeStruct((t, f), BF16), jax.ShapeDtypeStruct((t, f), BF16),
                   jax.ShapeDtypeStruct((t, f), BF16), jax.ShapeDtypeStruct((1, d), F32),
                   jax.ShapeDtypeStruct((1, d), F32)] + loss_shape + _carry_shapes(carry),
        scratch_shapes=[pltpu.VMEM((nch, d, fc), BF16), pltpu.VMEM((nch, d, fc), BF16),
                        pltpu.VMEM((nch, fc, d), BF16), pltpu.VMEM((tm, d), BF16),
                        pltpu.SemaphoreType.DMA((3,))] + loss_scratch + _carry_scratch(carry),
        compiler_params=_params(("arbitrary", "arbitrary"), VMEM_BIG),
    )(*lead_args, xo, rs, go, gs, us, wg, wu, wd, *[a for _, a in carry])


def _wgrad(xt, ys, name, carry=()):
    m, t = xt.shape
    n = ys[0].shape[1]
    tn = min(n, MXU_COLS)
    ny = len(ys)

    def body(*refs):
        x_hbm = refs[0]
        y_refs = refs[1:1 + ny]
        o_refs = refs[1 + ny:1 + 2 * ny]
        xv, sems = refs[1 + 2 * ny:]

        @pl.when(pl.program_id(0) == 0)
        def _():
            _load_resident([(x_hbm, xv)], sems)

        for y_ref, o_ref in zip(y_refs, o_refs):
            o_ref[...] = _dot(xv[...], y_ref[...].astype(BF16)).astype(BF16)

    steps = n // tn
    first = lambda: pl.program_id(0) == 0
    last = lambda: pl.program_id(0) == steps - 1
    return pl.pallas_call(
        _carried(body, 1 + ny, ny, carry, first, last), name=name, grid=(steps,),
        in_specs=[ANY] + [pl.BlockSpec((t, tn), lambda c: (0, c)) for _ in ys] + [ANY] * len(carry),
        out_specs=[pl.BlockSpec((m, tn), lambda c: (0, c)) for _ in ys] + [ANY] * len(carry),
        out_shape=[jax.ShapeDtypeStruct((m, n), BF16) for _ in ys] + _carry_shapes(carry),
        scratch_shapes=[pltpu.VMEM((m, t), BF16), pltpu.SemaphoreType.DMA((1,))] + _carry_scratch(carry),
        compiler_params=_params(("arbitrary",), VMEM_BIG),
    )(xt, *ys, *[a for _, a in carry])


def _shift_rows(u, halo, tm):
    r = lax.broadcasted_iota(jnp.int32, (tm, 1), 0)
    u1 = jnp.where(r == 0, halo[7:8], pltpu.roll(u, 1, 0))
    u2 = jnp.where(r == 0, halo[6:7], jnp.where(r == 1, halo[7:8], pltpu.roll(u, 2, 0)))
    return u1, u2


def _conv_fwd(xh, gi, bi, w_in, cw, w_out, go, bo, alpha, name, carry=()):
    t, d = xh.shape
    tm = _row_tile(t)
    nt = t // tm

    def body(xh_ref, gi_ref, bi_ref, win_ref, cw_ref, wout_ref, go_ref, bo_ref,
             xo_ref, rs_ref, hb_ref, p_ref, m_ref, halo):
        i = pl.program_id(0)

        @pl.when(i == 0)
        def _():
            halo[...] = jnp.zeros_like(halo)

        h = xh_ref[...] * gi_ref[...] + bi_ref[...]
        hb = h.astype(BF16)
        bg = _dot(hb, win_ref[:, 0:d])
        cg = _dot(hb, win_ref[:, d:2 * d])
        val = _dot(hb, win_ref[:, 2 * d:3 * d])
        p_ref[:, 0:d] = bg.astype(BF16)
        p_ref[:, d:2 * d] = cg.astype(BF16)
        p_ref[:, 2 * d:3 * d] = val.astype(BF16)
        rows = i * tm + lax.broadcasted_iota(jnp.int32, (tm, 1), 0)
        u = jnp.where(rows >= PAD, cg * val, 0.0)
        u1, u2 = _shift_rows(u, halo[...], tm)
        halo[...] = u[tm - 8:tm]
        y = cw_ref[0:1] * u2 + cw_ref[1:2] * u1 + cw_ref[2:3] * u
        mb = (bg * y).astype(BF16)
        m_ref[...] = mb.T
        xhat, rstd = _ln_fwd(alpha * h + _dot(mb, wout_ref[...]))
        xo_ref[...] = xhat
        rs_ref[...] = rstd
        hb_ref[...] = (xhat * go_ref[...] + bo_ref[...]).astype(BF16).T

    row = pl.BlockSpec((tm, d), lambda i: (i, 0))
    col = pl.BlockSpec((d, tm), lambda i: (0, i))
    vec = pl.BlockSpec((1, d), lambda i: (0, 0))
    first = lambda: pl.program_id(0) == 0
    last = lambda: pl.program_id(0) == nt - 1
    return pl.pallas_call(
        _carried(body, 8, 5, carry, first, last), name=name, grid=(nt,),
        in_specs=[row, vec, vec, pl.BlockSpec((d, 3 * d), lambda i: (0, 0)),
                  pl.BlockSpec((3, d), lambda i: (0, 0)), pl.BlockSpec((d, d), lambda i: (0, 0)),
                  vec, vec] + [ANY] * len(carry),
        out_specs=[row, pl.BlockSpec((tm, 1), lambda i: (i, 0)), col,
                   pl.BlockSpec((tm, 3 * d), lambda i: (i, 0)), col] + [ANY] * len(carry),
        out_shape=[jax.ShapeDtypeStruct((t, d), F32), jax.ShapeDtypeStruct((t, 1), F32),
                   jax.ShapeDtypeStruct((d, t), BF16), jax.ShapeDtypeStruct((t, 3 * d), BF16),
                   jax.ShapeDtypeStruct((d, t), BF16)] + _carry_shapes(carry),
        scratch_shapes=[pltpu.VMEM((8, d), F32)] + _carry_scratch(carry),
        compiler_params=_params(("arbitrary",), VMEM_BIG),
    )(xh, gi, bi, w_in, cw, w_out, go, bo, *[a for _, a in carry])


def _conv_bwd(dh, xo, rs, go, p, cw, w_in, w_out, alpha, name):
    t, d = dh.shape
    tm = _row_tile(t)
    nt = t // tm
    tb = tm // 8

    def body(dh_ref, xo_ref, rs_ref, go_ref, p_ref, ph_ref, cw_ref, win_ref, wout_ref,
             dhin_ref, dmix_ref, dp_ref, dcw_ref, dgain_ref, dbias_ref, carry):
        i = pl.program_id(0)
        tile = nt - 1 - i

        @pl.when(i == 0)
        def _():
            carry[...] = jnp.zeros_like(carry)
            dcw_ref[...] = jnp.zeros_like(dcw_ref)
            dgain_ref[...] = jnp.zeros_like(dgain_ref)
            dbias_ref[...] = jnp.zeros_like(dbias_ref)

        dz, dgp, dbp = _ln_bwd(dh_ref[...], xo_ref[...], rs_ref[...], go_ref[...])
        dgain_ref[...] += dgp
        dbias_ref[...] += dbp
        dmixb = dz.astype(BF16)
        dmix_ref[...] = dmixb
        dm = _dot_nt(dmixb, wout_ref[...])

        bg = p_ref[:, 0:d].astype(F32)
        cg = p_ref[:, d:2 * d].astype(F32)
        val = p_ref[:, 2 * d:3 * d].astype(F32)
        rows = tile * tm + lax.broadcasted_iota(jnp.int32, (tm, 1), 0)
        valid = rows >= PAD
        u = jnp.where(valid, cg * val, 0.0)
        hrows = tile * tm - 8 + lax.broadcasted_iota(jnp.int32, (8, 1), 0)
        hu = jnp.where((hrows >= PAD) & (tile > 0),
                       ph_ref[:, d:2 * d].astype(F32) * ph_ref[:, 2 * d:3 * d].astype(F32), 0.0)
        u1, u2 = _shift_rows(u, hu, tm)
        w0, w1, w2 = cw_ref[0:1], cw_ref[1:2], cw_ref[2:3]
        y = w0 * u2 + w1 * u1 + w2 * u
        dbg = dm * y
        dy = dm * bg
        dcw_ref[0:1] += jnp.sum(dy * u2, axis=0, keepdims=True)
        dcw_ref[1:2] += jnp.sum(dy * u1, axis=0, keepdims=True)
        dcw_ref[2:3] += jnp.sum(dy * u, axis=0, keepdims=True)

        nxt = carry[...]
        r = lax.broadcasted_iota(jnp.int32, (tm, 1), 0)
        dy1 = jnp.where(r == tm - 1, nxt[0:1], pltpu.roll(dy, tm - 1, 0))
        dy2 = jnp.where(r == tm - 2, nxt[0:1],
                        jnp.where(r == tm - 1, nxt[1:2], pltpu.roll(dy, tm - 2, 0)))
        carry[...] = dy[0:8]
        du = jnp.where(valid, w2 * dy + w1 * dy1 + w0 * dy2, 0.0)
        dbgb = dbg.astype(BF16)
        dcgb = (du * val).astype(BF16)
        dvalb = (du * cg).astype(BF16)
        dp_ref[:, 0:d] = dbgb
        dp_ref[:, d:2 * d] = dcgb
        dp_ref[:, 2 * d:3 * d] = dvalb
        dhin_ref[...] = (alpha * dz + _dot_nt(dbgb, win_ref[:, 0:d])
                         + _dot_nt(dcgb, win_ref[:, d:2 * d]) + _dot_nt(dvalb, win_ref[:, 2 * d:3 * d]))

    row = pl.BlockSpec((tm, d), lambda i: (nt - 1 - i, 0))
    vec = pl.BlockSpec((1, d), lambda i: (0, 0))
    prow = pl.BlockSpec((tm, 3 * d), lambda i: (nt - 1 - i, 0))
    return pl.pallas_call(
        body, name=name, grid=(nt,),
        in_specs=[row, row, pl.BlockSpec((tm, 1), lambda i: (nt - 1 - i, 0)), vec, prow,
                  pl.BlockSpec((8, 3 * d), lambda i: (jnp.maximum((nt - 1 - i) * tb - 1, 0), 0)),
                  pl.BlockSpec((3, d), lambda i: (0, 0)),
                  pl.BlockSpec((d, 3 * d), lambda i: (0, 0)), pl.BlockSpec((d, d), lambda i: (0, 0))],
        out_specs=[row, row, prow, pl.BlockSpec((3, d), lambda i: (0, 0)), vec, vec],
        out_shape=[jax.ShapeDtypeStruct((t, d), F32), jax.ShapeDtypeStruct((t, d), BF16),
                   jax.ShapeDtypeStruct((t, 3 * d), BF16), jax.ShapeDtypeStruct((3, d), F32),
                   jax.ShapeDtypeStruct((1, d), F32), jax.ShapeDtypeStruct((1, d), F32)],
        scratch_shapes=[pltpu.VMEM((8, d), F32)],
        compiler_params=_params(("arbitrary",), VMEM_BIG),
    )(dh, xo, rs, go, p, p, cw, w_in, w_out)


def _kv_fwd(xh, gi, bi, wk, wv, wf, fb, name, carry=()):
    t, d = xh.shape
    tm = _row_tile(t)
    nt = t // tm

    def body(xh_ref, gi_ref, bi_ref, wk_ref, wv_ref, wf_ref, fb_ref,
             k_ref, v_ref, lg_ref, c_ref, ct_ref, run):
        i = pl.program_id(0)

        @pl.when(i == 0)
        def _():
            run[...] = jnp.zeros_like(run)

        x = (xh_ref[...] * gi_ref[...] + bi_ref[...]).astype(BF16)
        k_ref[...] = _dot(x, wk_ref[...]).astype(BF16)
        v_ref[...] = _dot(x, wv_ref[...]).astype(BF16)
        logit = _dot(x, wf_ref[...]) + fb_ref[...]
        lg_ref[...] = logit
        logf = jnp.minimum(logit, 0.0) - jnp.log(1.0 + jnp.exp(-jnp.abs(logit)))
        rows = i * tm + lax.broadcasted_iota(jnp.int32, (tm, 1), 0)
        logf = jnp.where(rows >= PAD, logf, 0.0)
        tri = (lax.broadcasted_iota(jnp.int32, (tm, tm), 0)
               >= lax.broadcasted_iota(jnp.int32, (tm, tm), 1)).astype(F32)
        cs = jnp.dot(tri, logf, precision=lax.Precision.HIGHEST, preferred_element_type=F32) + run[...]
        run[...] = cs[tm - 1:tm]
        c_ref[...] = cs
        ct_ref[...] = cs.T

    row = pl.BlockSpec((tm, d), lambda i: (i, 0))
    vec = pl.BlockSpec((1, d), lambda i: (0, 0))
    gate = pl.BlockSpec((tm, LANES), lambda i: (i, 0))
    sq = pl.BlockSpec((d, d), lambda i: (0, 0))
    first = lambda: pl.program_id(0) == 0
    last = lambda: pl.program_id(0) == nt - 1
    return pl.pallas_call(
        _carried(body, 7, 5, carry, first, last), name=name, grid=(nt,),
        in_specs=[row, vec, vec, sq, sq, pl.BlockSpec((d, LANES), lambda i: (0, 0)),
                  pl.BlockSpec((1, LANES), lambda i: (0, 0))] + [ANY] * len(carry),
        out_specs=[row, row, gate, gate, pl.BlockSpec((LANES, tm), lambda i: (0, i))] + [ANY] * len(carry),
        out_shape=[jax.ShapeDtypeStruct((t, d), BF16), jax.ShapeDtypeStruct((t, d), BF16),
                   jax.ShapeDtypeStruct((t, LANES), F32), jax.ShapeDtypeStruct((t, LANES), F32),
                   jax.ShapeDtypeStruct((LANES, t), F32)] + _carry_shapes(carry),
        scratch_shapes=[pltpu.VMEM((1, LANES), F32)] + _carry_scratch(carry),
        compiler_params=_params(("arbitrary",), VMEM_MID),
    )(xh, gi, bi, wk, wv, wf, fb, *[a for _, a in carry])


def _kv_bwd(dk, dv, dcs, dcq, logit, dh_other, wk, wv, wf, name):
    t, d = dk.shape
    tm = _row_tile(t)
    nt = t // tm

    def body(dk_ref, dv_ref, dcs_ref, dcq_ref, lg_ref, oth_ref, wk_ref, wv_ref, wf_ref,
             dh_ref, dl_ref, dfb_ref, run):
        i = pl.program_id(0)
        tile = nt - 1 - i

        @pl.when(i == 0)
        def _():
            run[...] = jnp.zeros_like(run)
            dfb_ref[...] = jnp.zeros_like(dfb_ref)

        lane = lax.broadcasted_iota(jnp.int32, (tm, LANES), 1)
        dc = dcq_ref[...]
        for hh in range(N_HEADS):
            dc = dc + jnp.where(lane == hh, jnp.sum(dcs_ref[hh], axis=1, keepdims=True), 0.0)
        tri = (lax.broadcasted_iota(jnp.int32, (tm, tm), 0)
               <= lax.broadcasted_iota(jnp.int32, (tm, tm), 1)).astype(F32)
        dlf = jnp.dot(tri, dc, precision=lax.Precision.HIGHEST, preferred_element_type=F32) + run[...]
        run[...] = dlf[0:1]
        rows = tile * tm + lax.broadcasted_iota(jnp.int32, (tm, 1), 0)
        dlogit = jnp.where(rows >= PAD, dlf * jax.nn.sigmoid(-lg_ref[...]), 0.0)
        dfb_ref[...] += jnp.sum(dlogit, axis=0, keepdims=True)
        dlb = dlogit.astype(BF16)
        dl_ref[...] = dlb
        dh_ref[...] = (oth_ref[...] + _dot_nt(dk_ref[...], wk_ref[...])
                       + _dot_nt(dv_ref[...], wv_ref[...]) + _dot_nt(dlb, wf_ref[...]))

    row = pl.BlockSpec((tm, d), lambda i: (nt - 1 - i, 0))
    gate = pl.BlockSpec((tm, LANES), lambda i: (nt - 1 - i, 0))
    sq = pl.BlockSpec((d, d), lambda i: (0, 0))
    return pl.pallas_call(
        body, name=name, grid=(nt,),
        in_specs=[row, row, pl.BlockSpec((N_HEADS, tm, LANES), lambda i: (0, nt - 1 - i, 0)), gate, gate, row,
                  sq, sq, pl.BlockSpec((d, LANES), lambda i: (0, 0))],
        out_specs=[row, gate, pl.BlockSpec((1, LANES), lambda i: (0, 0))],
        out_shape=[jax.ShapeDtypeStruct((t, d), F32), jax.ShapeDtypeStruct((t, LANES), BF16),
                   jax.ShapeDtypeStruct((1, LANES), F32)],
        scratch_shapes=[pltpu.VMEM((1, LANES), F32)],
        compiler_params=_params(("arbitrary",), VMEM_MID),
    )(dk, dv, dcs, dcq, logit, dh_other, wk, wv, wf)


def _proj(xh, gi, bi, w, name):
    t, k = xh.shape
    n = w.shape[1]
    tm = _row_tile(t)

    def body(x_ref, g_ref, b_ref, w_ref, o_ref):
        x = (x_ref[...] * g_ref[...] + b_ref[...]).astype(BF16)
        o_ref[...] = _dot(x, w_ref[...]).astype(BF16)

    vec = pl.BlockSpec((1, k), lambda i: (0, 0))
    return pl.pallas_call(
        body, name=name, grid=(t // tm,),
        in_specs=[pl.BlockSpec((tm, k), lambda i: (i, 0)), vec, vec, pl.BlockSpec((k, n), lambda i: (0, 0))],
        out_specs=pl.BlockSpec((tm, n), lambda i: (i, 0)),
        out_shape=jax.ShapeDtypeStruct((t, n), BF16),
        compiler_params=_params(("arbitrary",), VMEM_MID),
    )(xh, gi, bi, w)


def _add_proj_nt(base, y, w, name):
    t, n = y.shape
    k = w.shape[0]
    tm = _row_tile(t)

    def body(b_ref, y_ref, w_ref, o_ref):
        o_ref[...] = b_ref[...] + _dot_nt(y_ref[...].astype(BF16), w_ref[...])

    return pl.pallas_call(
        body, name=name, grid=(t // tm,),
        in_specs=[pl.BlockSpec((tm, k), lambda i: (i, 0)), pl.BlockSpec((tm, n), lambda i: (i, 0)),
                  pl.BlockSpec((k, n), lambda i: (0, 0))],
        out_specs=pl.BlockSpec((tm, k), lambda i: (i, 0)),
        out_shape=jax.ShapeDtypeStruct((t, k), F32),
        compiler_params=_params(("arbitrary",), VMEM_MID),
    )(base, y, w)


def _attn_out_fwd(ot, xh, gi, bi, w_o, go, bo, alpha, name):
    t, d = xh.shape
    tm = _row_tile(t)

    def body(ot_ref, xh_ref, gi_ref, bi_ref, wo_ref, go_ref, bo_ref, xo_ref, rs_ref, hb_ref):
        h = xh_ref[...] * gi_ref[...] + bi_ref[...]
        xhat, rstd = _ln_fwd(alpha * h + _dot_tn(ot_ref[...], wo_ref[...]))
        xo_ref[...] = xhat
        rs_ref[...] = rstd
        hb_ref[...] = (xhat * go_ref[...] + bo_ref[...]).astype(BF16).T

    row = pl.BlockSpec((tm, d), lambda i: (i, 0))
    col = pl.BlockSpec((d, tm), lambda i: (0, i))
    vec = pl.BlockSpec((1, d), lambda i: (0, 0))
    return pl.pallas_call(
        body, name=name, grid=(t // tm,),
        in_specs=[col, row, vec, vec, pl.BlockSpec((d, d), lambda i: (0, 0)), vec, vec],
        out_specs=[row, pl.BlockSpec((tm, 1), lambda i: (i, 0)), col],
        out_shape=[jax.ShapeDtypeStruct((t, d), F32), jax.ShapeDtypeStruct((t, 1), F32),
                   jax.ShapeDtypeStruct((d, t), BF16)],
        compiler_params=_params(("arbitrary",), VMEM_MID),
    )(ot, xh, gi, bi, w_o, go, bo)


def _attn_out_bwd(dh, xo, rs, go, ot, w_o, alpha, name):
    t, d = dh.shape
    tm = _row_tile(t)
    hd = d // N_HEADS

    def body(dh_ref, xo_ref, rs_ref, go_ref, ot_ref, wo_ref,
             dres_ref, dmix_ref, dot_ref, delta_ref, dgain_ref, dbias_ref):
        @pl.when(pl.program_id(0) == 0)
        def _():
            dgain_ref[...] = jnp.zeros_like(dgain_ref)
            dbias_ref[...] = jnp.zeros_like(dbias_ref)

        dz, dgp, dbp = _ln_bwd(dh_ref[...], xo_ref[...], rs_ref[...], go_ref[...])
        dgain_ref[...] += dgp
        dbias_ref[...] += dbp
        dres_ref[...] = alpha * dz
        dmixb = dz.astype(BF16)
        dmix_ref[...] = dmixb
        dot_t = _dot_nt(wo_ref[...], dmixb)
        dot_ref[...] = dot_t.astype(BF16)
        prod = dot_t * ot_ref[...].astype(F32)
        delta_ref[...] = jnp.sum(prod.reshape(N_HEADS, hd, tm), axis=1)

    row = pl.BlockSpec((tm, d), lambda i: (i, 0))
    vec = pl.BlockSpec((1, d), lambda i: (0, 0))
    col = pl.BlockSpec((d, tm), lambda i: (0, i))
    return pl.pallas_call(
        body, name=name, grid=(t // tm,),
        in_specs=[row, row, pl.BlockSpec((tm, 1), lambda i: (i, 0)), vec, col,
                  pl.BlockSpec((d, d), lambda i: (0, 0))],
        out_specs=[row, row, col, pl.BlockSpec((N_HEADS, tm), lambda i: (0, i)), vec, vec],
        out_shape=[jax.ShapeDtypeStruct((t, d), F32), jax.ShapeDtypeStruct((t, d), BF16),
                   jax.ShapeDtypeStruct((d, t), BF16), jax.ShapeDtypeStruct((N_HEADS, t), F32),
                   jax.ShapeDtypeStruct((1, d), F32), jax.ShapeDtypeStruct((1, d), F32)],
        compiler_params=_params(("arbitrary",), VMEM_MID),
    )(dh, xo, rs, go, ot, w_o)


def _scores_t(k, q, ct_ref, c_ref, h, i, j, tq, tk, scale, masked):
    sub = lax.broadcasted_iota(jnp.int32, (8, tq), 0)
    cq = jnp.sum(jnp.where(sub == h, ct_ref[...], 0.0), axis=0, keepdims=True) * LOG2E
    lane = lax.broadcasted_iota(jnp.int32, (tk, LANES), 1)
    ck = jnp.sum(jnp.where(lane == h, c_ref[...], 0.0), axis=1, keepdims=True) * LOG2E
    st = _dot_nt(k, q) * (scale * LOG2E) - ck
    if masked:
        kpos = j * tk + lax.broadcasted_iota(jnp.int32, (tk, 1), 0)
        qpos = i * tq + lax.broadcasted_iota(jnp.int32, (1, tq), 1)
        st = jnp.where((kpos <= qpos) & (kpos >= PAD), st, NEG_INF)
    return st, cq


def _tri_pairs(n, by_row):
    if by_row:
        pairs = [(i, j) for i in range(n) for j in range(i + 1)]
    else:
        pairs = [(i, j) for j in range(n) for i in range(j, n)]
    return (jnp.asarray([p[0] for p in pairs], jnp.int32), jnp.asarray([p[1] for p in pairs], jnp.int32))


def _attn_fwd(q, k, v, c, ct, name, carry=()):
    t, d = q.shape
    hd = d // N_HEADS
    tq = tk = _row_tile(t)
    nq = t // tq
    scale = 1.0 / math.sqrt(hd)

    hps = ATTN_HEADS_PER_STEP

    def body(it_ref, jt_ref, q_ref, k_ref, v_ref, c_ref, ct_ref, ot_ref, lse_ref, m_s, l_s, acc):
        hp, p_ = pl.program_id(0), pl.program_id(1)
        i, j = it_ref[p_], jt_ref[p_]

        @pl.when(j == 0)
        def _():
            m_s[...] = jnp.full_like(m_s, NEG_INF)
            l_s[...] = jnp.zeros_like(l_s)
            acc[...] = jnp.zeros_like(acc)

        def update(masked):
            scores = []
            for e in range(hps):
                cols = slice(e * hd, (e + 1) * hd)
                scores.append(_scores_t(k_ref[:, cols], q_ref[:, cols], ct_ref, c_ref, hp * hps + e,
                                        i, j, tq, tk, scale, masked))
            probs = []
            for e, (st, cq) in enumerate(scores):
                m_new = jnp.maximum(m_s[e], jnp.max(st, axis=0, keepdims=True) + cq)
                a = jnp.exp2(m_s[e] - m_new)
                p = jnp.exp2(st - (m_new - cq))
                l_s[e] = a * l_s[e] + jnp.sum(p, axis=0, keepdims=True)
                m_s[e] = m_new
                probs.append((a, p.astype(BF16)))
            for e, (a, pb) in enumerate(probs):
                acc[e] = a * acc[e] + _dot_tn(v_ref[:, e * hd:(e + 1) * hd], pb)

        edge = (j == i) | (j == 0)
        pl.when(edge)(lambda: update(True))
        pl.when(jnp.logical_not(edge))(lambda: update(False))

        @pl.when(j == i)
        def _():
            for e in range(hps):
                ot_ref[e * hd:(e + 1) * hd, :] = (acc[e] / l_s[e]).astype(BF16)
                lse_ref[e] = m_s[e] + jnp.log2(l_s[e])

    it, jt = _tri_pairs(nq, by_row=True)
    npairs = it.shape[0]
    nhp = N_HEADS // hps
    kv = pl.BlockSpec((tk, hps * hd), lambda h, p, it, jt: (jt[p], h))
    first = lambda: (pl.program_id(0) == 0) & (pl.program_id(1) == 0)
    last = lambda: (pl.program_id(0) == nhp - 1) & (pl.program_id(1) == npairs - 1)
    return pl.pallas_call(
        _carried(body, 7, 2, carry, first, last), name=name,
        grid_spec=pltpu.PrefetchScalarGridSpec(
            num_scalar_prefetch=2, grid=(nhp, npairs),
            in_specs=[pl.BlockSpec((tq, hps * hd), lambda h, p, it, jt: (it[p], h)), kv, kv,
                      pl.BlockSpec((tk, LANES), lambda h, p, it, jt: (jt[p], 0)),
                      pl.BlockSpec((8, tq), lambda h, p, it, jt: (0, it[p]))] + [ANY] * len(carry),
            out_specs=[pl.BlockSpec((hps * hd, tq), lambda h, p, it, jt: (h, it[p])),
                       pl.BlockSpec((hps, 1, tq), lambda h, p, it, jt: (h, 0, it[p]))] + [ANY] * len(carry),
            scratch_shapes=[pltpu.VMEM((hps, 1, tq), F32), pltpu.VMEM((hps, 1, tq), F32),
                            pltpu.VMEM((hps, hd, tq), F32)] + _carry_scratch(carry)),
        out_shape=[jax.ShapeDtypeStruct((d, t), BF16), jax.ShapeDtypeStruct((N_HEADS, 1, t), F32)]
                  + _carry_shapes(carry),
        compiler_params=_params(("arbitrary", "arbitrary"), VMEM_MID),
    )(it, jt, q, k, v, c, ct, *[a for _, a in carry])


def _attn_bwd(q, k, v, c, ct, lse, delta, dot_t, name, carry=()):
    t, d = q.shape
    hd = d // N_HEADS
    tq = tk = _row_tile(t)
    nq = t // tq
    scale = 1.0 / math.sqrt(hd)
    hps = ATTN_BWD_HEADS_PER_STEP

    steps = [(j, i, min(i + 1, nq - 1), int(i + 1 < nq)) for j in range(nq) for i in range(j, nq, 2)]
    jt, ia, ib, vb = (jnp.asarray([s[n] for s in steps], jnp.int32) for n in range(4))

    def body(jt_ref, ia_ref, ib_ref, vb_ref, qa_ref, k_ref, v_ref, c_ref, cta_ref, lsea_ref, deltaa_ref, dota_ref,
             qb_ref, ctb_ref, lseb_ref, deltab_ref, dotb_ref,
             dq_ref, dk_ref, dv_ref, dcs_ref, drow_ref, dk_acc, dv_acc, dc_acc):
        hp, p_ = pl.program_id(0), pl.program_id(1)
        j, i_a, i_b, has_b = jt_ref[p_], ia_ref[p_], ib_ref[p_], vb_ref[p_] == 1

        @pl.when(p_ == 0)
        def _():
            dq_ref[...] = jnp.zeros_like(dq_ref)
            drow_ref[...] = jnp.zeros_like(drow_ref)

        @pl.when(i_a == j)
        def _():
            dk_acc[...] = jnp.zeros_like(dk_acc)
            dv_acc[...] = jnp.zeros_like(dv_acc)
            dc_acc[...] = jnp.zeros_like(dc_acc)

        def update(q_ref, ct_ref, lse_ref, delta_ref, dot_ref, i, masked):
            sub = lax.broadcasted_iota(jnp.int32, (8, tq), 0)
            rows = pl.ds(pl.multiple_of(i * tq, tq), tq)
            stage = []
            for e in range(hps):
                cols = slice(e * hd, (e + 1) * hd)
                st, cq = _scores_t(k_ref[:, cols], q_ref[:, cols], ct_ref, c_ref, hp * hps + e,
                                   i, j, tq, tk, scale, masked)
                dp = _dot(v_ref[:, cols], dot_ref[cols, :])
                stage.append((st, cq, dp))
            grads = []
            for e, (st, cq, dp) in enumerate(stage):
                p = jnp.exp2(st - (lse_ref[e] - cq))
                dl = jnp.sum(jnp.where(sub == hp * hps + e, delta_ref[...], 0.0), axis=0, keepdims=True)
                ds = p * (dp - dl)
                part = ds[:, 0:LANES]
                for g in range(1, tq // LANES):
                    part = part + ds[:, g * LANES:(g + 1) * LANES]
                dc_acc[e] += part
                drow_ref[e, i] += jnp.broadcast_to(jnp.sum(ds, axis=0, keepdims=True), (8, tq))
                grads.append((p.astype(BF16), ds.astype(BF16)))
            for e, (pb, dsb) in enumerate(grads):
                cols = slice(e * hd, (e + 1) * hd)
                dv_acc[e] += _dot_nt(pb, dot_ref[cols, :])
                dk_acc[e] += _dot(dsb, q_ref[:, cols]) * scale
                dq_ref[rows, cols] += _dot_tn(dsb, k_ref[:, cols]) * scale

        slot_a = (qa_ref, cta_ref, lsea_ref, deltaa_ref, dota_ref, i_a)
        slot_b = (qb_ref, ctb_ref, lseb_ref, deltab_ref, dotb_ref, i_b)
        edge_a = (j == i_a) | (j == 0)
        pl.when(edge_a)(lambda: update(*slot_a, True))
        pl.when(jnp.logical_not(edge_a))(lambda: update(*slot_a, False))
        pl.when(has_b & (j == 0))(lambda: update(*slot_b, True))
        pl.when(has_b & (j != 0))(lambda: update(*slot_b, False))

        @pl.when((i_a == nq - 1) | (has_b & (i_b == nq - 1)))
        def _():
            for e in range(hps):
                cols = slice(e * hd, (e + 1) * hd)
                dk_ref[:, cols] = dk_acc[e].astype(BF16)
                dv_ref[:, cols] = dv_acc[e].astype(BF16)
                dcs_ref[e] = -dc_acc[e]

    nsteps = len(steps)
    nhp = N_HEADS // hps
    kv = pl.BlockSpec((tk, hps * hd), lambda h, p, jt, ia, ib, vb: (jt[p], h))

    def q_side(sel):
        return [pl.BlockSpec((tq, hps * hd), lambda h, p, jt, ia, ib, vb: (sel(ia, ib)[p], h)),
                pl.BlockSpec((8, tq), lambda h, p, jt, ia, ib, vb: (0, sel(ia, ib)[p])),
                pl.BlockSpec((hps, 1, tq), lambda h, p, jt, ia, ib, vb: (h, 0, sel(ia, ib)[p])),
                pl.BlockSpec((N_HEADS, tq), lambda h, p, jt, ia, ib, vb: (0, sel(ia, ib)[p])),
                pl.BlockSpec((hps * hd, tq), lambda h, p, jt, ia, ib, vb: (h, sel(ia, ib)[p]))]

    qa_specs, qb_specs = q_side(lambda ia, ib: ia), q_side(lambda ia, ib: ib)
    first = lambda: (pl.program_id(0) == 0) & (pl.program_id(1) == 0)
    last = lambda: (pl.program_id(0) == nhp - 1) & (pl.program_id(1) == nsteps - 1)
    q_args = (q, ct, lse, delta, dot_t)
    return pl.pallas_call(
        _carried(body, 17, 5, carry, first, last), name=name,
        grid_spec=pltpu.PrefetchScalarGridSpec(
            num_scalar_prefetch=4, grid=(nhp, nsteps),
            in_specs=[qa_specs[0], kv, kv, pl.BlockSpec((tk, LANES), lambda h, p, jt, ia, ib, vb: (jt[p], 0))]
                     + qa_specs[1:] + qb_specs + [ANY] * len(carry),
            out_specs=[pl.BlockSpec((t, hps * hd), lambda h, p, jt, ia, ib, vb: (0, h)), kv, kv,
                       pl.BlockSpec((hps, tk, LANES), lambda h, p, jt, ia, ib, vb: (h, jt[p], 0)),
                       pl.BlockSpec((hps, nq, 8, tq), lambda h, p, jt, ia, ib, vb: (h, 0, 0, 0))]
                      + [ANY] * len(carry),
            scratch_shapes=[pltpu.VMEM((hps, tk, hd), F32), pltpu.VMEM((hps, tk, hd), F32),
                            pltpu.VMEM((hps, tk, LANES), F32)] + _carry_scratch(carry)),
        out_shape=[jax.ShapeDtypeStruct((t, d), F32), jax.ShapeDtypeStruct((t, d), BF16),
                   jax.ShapeDtypeStruct((t, d), BF16), jax.ShapeDtypeStruct((N_HEADS, t, LANES), F32),
                   jax.ShapeDtypeStruct((N_HEADS, nq, 8, tq), F32)] + _carry_shapes(carry),
        compiler_params=_params(("arbitrary", "arbitrary"), VMEM_BIG),
    )(jt, ia, ib, vb, q, k, v, c, ct, lse, delta, dot_t, *q_args, *[a for _, a in carry])


def _adamw(w, g, m, v, name):
    r, c = w.shape
    tr = r
    for cand in (256, 128, 64, 32, 16, 8):
        if r % cand == 0 and r > cand:
            tr = cand
            break
    bc1 = 1.0 - ADAM_B1 ** ADAM_STEP
    bc2 = 1.0 - ADAM_B2 ** ADAM_STEP

    def body(w_ref, g_ref, m_ref, v_ref, d_ref, nm_ref, nv_ref):
        gg = g_ref[...]
        nm = ADAM_B1 * m_ref[...] + (1.0 - ADAM_B1) * gg
        nv = ADAM_B2 * v_ref[...] + (1.0 - ADAM_B2) * (gg * gg)
        d_ref[...] = -ADAM_LR * ((nm / bc1) / (jnp.sqrt(nv / bc2) + ADAM_EPS) + ADAM_WD * w_ref[...])
        nm_ref[...] = nm
        nv_ref[...] = nv

    blk = pl.BlockSpec((tr, c), lambda i: (i, 0))
    shp = jax.ShapeDtypeStruct((r, c), F32)
    return pl.pallas_call(
        body, name=name, grid=(r // tr,), in_specs=[blk] * 4, out_specs=[blk] * 3,
        out_shape=[shp] * 3, compiler_params=_params(("arbitrary",), VMEM_MID),
    )(w, g, m, v)


def _reduce_adamw(w, m, v, landed, name):
    nl, r, c = w.shape
    tr = next(cand for cand in range(min(r, ADAM_ROWS_MAX), 0, -BF16_ROWS) if r % cand == 0)
    nr = r // tr
    bc1 = 1.0 - ADAM_B1 ** ADAM_STEP
    bc2 = 1.0 - ADAM_B2 ** ADAM_STEP

    def body(*refs):
        w_ref, m_ref, v_ref = refs[:3]
        src_refs = refs[3:3 + nl]
        g_ref, d_ref, nm_ref, nv_ref = refs[3 + nl:]

        def update(src):
            gg = src[0].astype(F32)
            for s in range(1, N_DEV):
                gg = gg + src[s].astype(F32)
            nm = ADAM_B1 * m_ref[0] + (1.0 - ADAM_B1) * gg
            nv = ADAM_B2 * v_ref[0] + (1.0 - ADAM_B2) * (gg * gg)
            g_ref[0] = gg
            d_ref[0] = -ADAM_LR * ((nm / bc1) / (jnp.sqrt(nv / bc2) + ADAM_EPS) + ADAM_WD * w_ref[0])
            nm_ref[0] = nm
            nv_ref[0] = nv

        for idx in range(nl):
            pl.when(pl.program_id(0) == idx)(functools.partial(update, src_refs[idx]))

    def src_spec(idx):
        return pl.BlockSpec((N_DEV, tr, c),
                            lambda l, i: (0, jnp.where(l == idx, i, jnp.where(l < idx, 0, nr - 1)), 0))

    blk = pl.BlockSpec((1, tr, c), lambda l, i: (l, i, 0))
    shp = jax.ShapeDtypeStruct((nl, r, c), F32)
    return pl.pallas_call(
        body, name=name, grid=(nl, nr), in_specs=[blk] * 3 + [src_spec(idx) for idx in range(nl)],
        out_specs=[blk] * 4, out_shape=[shp] * 4,
        compiler_params=_params(("arbitrary", "arbitrary"), VMEM_MID),
    )(w, m, v, *landed)


def _sum_sources(r, name):
    n, rows, c = r.shape
    tr = next(cand for cand in range(min(rows, SUM_ROWS_MAX), 0, -BF16_ROWS) if rows % cand == 0)

    def body(r_ref, o_ref):
        acc = r_ref[0].astype(F32)
        for s in range(1, n):
            acc = acc + r_ref[s].astype(F32)
        o_ref[...] = acc

    return pl.pallas_call(
        body, name=name, grid=(rows // tr,),
        in_specs=[pl.BlockSpec((n, tr, c), lambda i: (0, i, 0))],
        out_specs=pl.BlockSpec((tr, c), lambda i: (i, 0)),
        out_shape=jax.ShapeDtypeStruct((rows, c), F32),
        compiler_params=_params(("arbitrary",), VMEM_MID),
    )(r)


def _all_gather(parts, name):
    n = len(parts)

    def body(*refs):
        x_refs, out_refs = refs[:n], refs[n:2 * n]
        send_sems, recv_sems, local_sems = refs[2 * n:]
        mx, my, mc = lax.axis_index("x"), lax.axis_index("y"), lax.axis_index("c")
        me, sibling = (mx, my, mc), (mx, my, 1 - mc)
        chips = [(1 - mx, my), (mx, 1 - my), (1 - mx, 1 - my)]

        def copy(p, k, block, to, from_input=False):
            px, py, pc = block
            rows = out_refs[p].at[4 * px + 2 * py + pc]
            return pltpu.make_async_remote_copy(
                src_ref=x_refs[p] if from_input else rows, dst_ref=rows,
                send_sem=send_sems.at[7 * p + k], recv_sem=recv_sems.at[7 * p + k],
                device_id=to, device_id_type=MESH)

        mine, sent = [], []
        for p in range(n):
            own = pltpu.make_async_copy(x_refs[p], out_refs[p].at[4 * mx + 2 * my + mc], local_sems.at[p])
            own.start()
            mine.append(own)
            first = [copy(p, 0, me, sibling, True)]
            first += [copy(p, 1 + j, me, (*chip, mc), True) for j, chip in enumerate(chips)]
            for cp in first:
                cp.start()
            sent += first
        for p in range(n):
            for j, chip in enumerate(chips):
                copy(p, 1 + j, (*chip, mc), me).wait_recv()
                fwd = copy(p, 4 + j, (*chip, mc), sibling)
                fwd.start()
                sent.append(fwd)
        for p in range(n):
            copy(p, 0, sibling, me).wait_recv()
            for j, chip in enumerate(chips):
                copy(p, 4 + j, (*chip, 1 - mc), me).wait_recv()
        for cp in sent:
            cp.wait_send()
        for own in mine:
            own.wait()

    return pl.pallas_call(
        body, name=name, in_specs=[ANY] * n, out_specs=[ANY] * n,
        out_shape=[jax.ShapeDtypeStruct((N_DEV,) + a.shape, a.dtype) for a in parts],
        scratch_shapes=[pltpu.SemaphoreType.DMA((7 * n,)), pltpu.SemaphoreType.DMA((7 * n,)),
                        pltpu.SemaphoreType.DMA((n,))],
    )(*parts)


def _pack_rows(parts, width, mult, lead=0):
    out = []
    for a in parts:
        head = a.shape[:lead]
        flat = a.reshape(head + (-1,))
        padn = (-flat.shape[-1]) % (width * mult)
        if padn:
            flat = jnp.pad(flat, [(0, 0)] * lead + [(0, padn)])
        out.append(flat.reshape(head + (-1, width)))
    return jnp.concatenate(out, axis=lead)


def _rows_of(shape, width, mult):
    n = math.prod(shape)
    per = width * mult
    return ((n + per - 1) // per) * mult


def _unpack_rows(buf, shapes, width, mult):
    lead = buf.shape[:-2]
    out, off = [], 0
    for shp in shapes:
        r = _rows_of(shp, width, mult)
        flat = buf[..., off:off + r, :].reshape(lead + (r * width,))
        out.append(flat[..., :math.prod(shp)].reshape(lead + tuple(shp)))
        off += r
    return out


def _cols_from_devices(g):
    nd = g.ndim
    perm = tuple(range(1, nd - 1)) + (0, nd - 1)
    t = jnp.transpose(g, perm)
    return t.reshape(t.shape[:-2] + (t.shape[-2] * t.shape[-1],))


def _cols_to_devices(a):
    c = a.shape[-1] // N_DEV
    t = a.reshape(a.shape[:-1] + (N_DEV, c))
    nd = t.ndim
    perm = (nd - 2,) + tuple(range(0, nd - 2)) + (nd - 1,)
    return jnp.transpose(t, perm)


WIDTH = 1024


def kernel(x, meta, ffn1_wg, ffn1_wu, ffn1_wd, ffn2_wg, ffn2_wu, ffn2_wd, ln_gain, ln_bias, conv_w_in, conv_w, conv_w_out, kv_w, f_bias, attn_w_q, attn_w_o, loss_target, m_meta, m_ffn1_wg, m_ffn1_wu, m_ffn1_wd, m_ffn2_wg, m_ffn2_wu, m_ffn2_wd, m_ln_gain, m_ln_bias, m_conv_w_in, m_conv_w, m_conv_w_out, m_kv_w, m_f_bias, m_attn_w_q, m_attn_w_o, v_meta, v_ffn1_wg, v_ffn1_wu, v_ffn1_wd, v_ffn2_wg, v_ffn2_wu, v_ffn2_wd, v_ln_gain, v_ln_bias, v_conv_w_in, v_conv_w, v_conv_w_out, v_kv_w, v_f_bias, v_attn_w_q, v_attn_w_o):
    depth = ln_gain.shape[0]
    alpha = float((2 * depth) ** 0.25)
    d = x.shape[-1]
    seq = x.shape[1]
    t = ROW0 + seq
    fsh = ffn1_wg.shape[-1]
    f = fsh * N_DEV
    fck = MXU_COLS
    nc = f // fck
    me = 4 * lax.axis_index("x") + 2 * lax.axis_index("y") + lax.axis_index("c")

    def gather_of(parts):
        return [(True, a.astype(BF16)) for a in parts]

    small = [meta, ln_gain, ln_bias, conv_w]
    small_shapes = [a.shape for a in small]
    g1g, g1u, g1d, gcin, gsmall = _all_gather(
        [a.astype(BF16) for a in (ffn1_wg[0], ffn1_wu[0], ffn1_wd[0], conv_w_in[0])]
        + [_pack_rows(small, WIDTH, F32_ROWS)], "ag_first")
    gmeta, ggain, gbias, gcw = _unpack_rows(gsmall, small_shapes, WIDTH, F32_ROWS)

    def ffn_chunks(gg, gu, gd):
        up = lambda g: jnp.transpose(_cols_from_devices(g).reshape(d, nc, fck), (1, 0, 2))
        return up(gg), up(gu), gd.reshape(nc, fck, d)

    w_in = _cols_from_devices(gcin)
    fb =jnp.pad(f_bias, (0, LANES - N_HEADS)).reshape(1, LANES)
    meta_f = _cols_from_devices(gmeta)
    gain_f = _cols_from_devices(ggain)
    bias_f = _cols_from_devices(gbias)
    cw_f = _cols_from_devices(gcw)[0]

    def gb(l, n):
        return gain_f[l, n].reshape(1, d), bias_f[l, n].reshape(1, d)

    ones = jnp.ones((1, d), F32)
    zeros = jnp.zeros((1, d), F32)

    h0 = jnp.concatenate([jnp.zeros((PAD, d), F32), meta_f, x[0]], axis=0)

    w1 = ffn_chunks(g1g, g1u, g1d)
    g00, b00 = gb(0, 0)
    xh1, rs1, hb1, gg1, uu1, hb0, gcout, g2g, g2u = _ffn_fwd(
        h0, ones, zeros, *w1, g00, b00, alpha, "ffn_fwd_0a",
        carry=gather_of([conv_w_out[0], ffn2_wg[0], ffn2_wu[0]]), input_t=True)
    w_out = gcout.reshape(d, d)
    g01, b01 = gb(0, 1)
    xh2, rs2, hb2, pp, mb, g2d, gkv = _conv_fwd(
        xh1, g00, b00, w_in, cw_f, w_out, g01, b01, alpha, "conv_fwd", carry=gather_of([ffn2_wd[0], kv_w.T]))
    w2 = ffn_chunks(g2g, g2u, g2d)
    g02, b02 = gb(0, 2)
    xh3, rs3, hb3, gg3, uu3, g3g, g3u, gwo = _ffn_fwd(
        xh2, g01, b01, *w2, g02, b02, alpha, "ffn_fwd_0b", carry=gather_of([ffn1_wg[1], ffn1_wu[1], attn_w_o[0]]))
    kvw = gkv.reshape(gkv.shape[0] * gkv.shape[1], d).T
    wk, wv = kvw[:, :d], kvw[:, d:2 * d]
    wf = jnp.pad(kvw[:, 2 * d:], ((0, 0), (0, LANES - N_HEADS)))
    kk, vv, logit, cc, cct, g3d = _kv_fwd(xh3, g02, b02, wk, wv, wf, fb, "kv_fwd",
                                          carry=gather_of([ffn1_wd[1]]))

    w3 = ffn_chunks(g3g, g3u, g3d)
    g10, b10 = gb(1, 0)
    xh4, rs4, hb4, gg4, uu4, gwq, g4g, g4u = _ffn_fwd(
        xh3, g02, b02, *w3, g10, b10, alpha, "ffn_fwd_1a", carry=gather_of([attn_w_q[0], ffn2_wg[1], ffn2_wu[1]]))
    w_q = gwq.reshape(d, d)
    qq = _proj(xh4, g10, b10, w_q, "q_proj")
    ot, lse, g4d = _attn_fwd(qq, kk, vv, cc, cct, "attn_fwd", carry=gather_of([ffn2_wd[1]]))
    w_o = gwo.reshape(d, d)
    g11, b11 = gb(1, 1)
    xh5, rs5, hb5 = _attn_out_fwd(ot, xh4, g10, b10, w_o, g11, b11, alpha, "attn_out_fwd")
    w4 = ffn_chunks(g4g, g4u, g4d)
    g12, b12 = gb(1, 2)
    xh6, rs6, _, gg6, uu6 = _ffn_fwd(xh5, g11, b11, *w4, g12, b12, alpha, "ffn_fwd_1b")


    dgain = [[None] * 3 for _ in range(depth)]
    dbias = [[None] * 3 for _ in range(depth)]

    def to_col_owners(g):
        return (False, _cols_to_devices(g).astype(BF16))

    def to_row_owners(g):
        return (False, g.reshape(N_DEV, g.shape[0] // N_DEV, g.shape[1]).astype(BF16))

    dh5, do6, dg6, du6, a6, dgain[1][2], dbias[1][2], loss_l = _ffn_bwd(
        None, xh6, rs6, g12, gg6, uu6, *w4, alpha, "ffn_bwd_1b", loss_target=loss_target[0], loss_bias=b12)
    loss = lax.psum(loss_l[0, 0], ("x", "y", "c"))
    dw4g, dw4u = _wgrad(hb5, [dg6, du6], "wgrad_up_1b")
    (dw4dt,) = _wgrad(do6, [a6], "wgrad_down_1b")

    dres4, dmix5, dot_t, delta, dgain[1][1], dbias[1][1] = _attn_out_bwd(dh5, xh5, rs5, g11, ot, w_o, alpha, "attn_out_bwd")
    (dwo,) = _wgrad(ot, [dmix5], "wgrad_wo")
    dq, dkk, dvv, dcs, drow = _attn_bwd(qq, kk, vv, cc, cct, lse, delta, dot_t, "attn_bwd")
    dh4 = _add_proj_nt(dres4, dq, w_q, "q_bwd")
    (dwq,) = _wgrad(hb4, [dq], "wgrad_wq")

    dh3a, do4, dg4, du4, a4, dgain[1][0], dbias[1][0], l4g, l4u, l4d, lwo = _ffn_bwd(
        dh4, xh4, rs4, g10, gg4, uu4, *w3, alpha, "ffn_bwd_1a",
        carry=[to_col_owners(dw4g), to_col_owners(dw4u), to_row_owners(dw4dt.T), to_row_owners(dwo)])
    dw3g, dw3u = _wgrad(hb3, [dg4, du4], "wgrad_up_1a")
    (dw3dt,) = _wgrad(do4, [a4], "wgrad_down_1a")

    dcq = jnp.pad(drow[:, :, 0, :].reshape(N_HEADS, t).T, ((0, 0), (0, LANES - N_HEADS)))
    dh3, dlogit, dfb = _kv_bwd(dkk, dvv, dcs, dcq, logit, dh3a, wk, wv, wf, "kv_bwd")
    dwk, dwv = _wgrad(hb3, [dkk, dvv], "wgrad_kv")
    (dwf,) = _wgrad(hb3, [dlogit], "wgrad_f")
    dkv = jnp.concatenate([dwk, dwv, dwf[:, :N_HEADS]], axis=1)

    dh2, do3, dg3, du3, a3, dgain[0][2], dbias[0][2], lwq, l3g, l3u, l3d, lkv = _ffn_bwd(
        dh3, xh3, rs3, g02, gg3, uu3, *w2, alpha, "ffn_bwd_0b",
        carry=[to_row_owners(dwq), to_col_owners(dw3g), to_col_owners(dw3u), to_row_owners(dw3dt.T),
               to_row_owners(dkv.T)])
    dw2g, dw2u = _wgrad(hb2, [dg3, du3], "wgrad_up_0b")
    (dw2dt,) = _wgrad(do3, [a3], "wgrad_down_0b")

    dh1, dmix2, dpp, dcw, dgain[0][1], dbias[0][1] = _conv_bwd(dh2, xh2, rs2, g01, pp, cw_f, w_in, w_out, alpha, "conv_bwd")
    (dwin,) = _wgrad(hb1, [dpp], "wgrad_conv_in")
    (dwout,) = _wgrad(mb, [dmix2], "wgrad_conv_out")

    dh0, do1, dg1, du1, a1, dgain[0][0], dbias[0][0], l2g, l2u, l2d, lcin, lcout = _ffn_bwd(
        dh1, xh1, rs1, g00, gg1, uu1, *w1, alpha, "ffn_bwd_0a",
        carry=[to_col_owners(dw2g), to_col_owners(dw2u), to_row_owners(dw2dt.T), to_col_owners(dwin),
               to_row_owners(dwout)])
    (dw1dt,) = _wgrad(do1, [a1], "wgrad_down_0a")
    dw1g, l1d = _wgrad(hb0, [dg1], "wgrad_upg_0a", carry=[to_row_owners(dw1dt.T)])
    dw1u, l1g = _wgrad(hb0, [du1], "wgrad_upu_0a", carry=[to_col_owners(dw1g)])
    dmeta = dh0[PAD:ROW0]
    dgain_f = jnp.stack([jnp.concatenate(r, axis=0) for r in dgain])
    dbias_f = jnp.stack([jnp.concatenate(r, axis=0) for r in dbias])
    small_full = [dmeta, dgain_f, dbias_f, dcw[None], dfb]
    small_full_shapes = [a.shape for a in small_full]
    l1u, gsmall_grads = _exchange([to_col_owners(dw1u), (True, _pack_rows(small_full, WIDTH, F32_ROWS))], "rs_last")

    grad_x = dh0[ROW0:].reshape(1, seq, d)
    rsmall = _sum_sources(gsmall_grads, "small_sum")
    smeta, sgain, sbias, scw, sfb = _unpack_rows(rsmall, small_full_shapes, WIDTH, F32_ROWS)
    csh = d // N_DEV

    def my_cols(a):
        return lax.dynamic_slice_in_dim(a, me * csh, csh, axis=a.ndim - 1)

    grads = {"meta": my_cols(smeta), "ln_gain": my_cols(sgain), "ln_bias": my_cols(sbias),
             "conv_w": my_cols(scw), "f_bias": sfb[0, :N_HEADS], "kv_w": _sum_sources(lkv, "kv_sum").T}
    landed = {"ffn1_wg": [l1g, l3g], "ffn1_wu": [l1u, l3u], "ffn1_wd": [l1d, l3d],
              "ffn2_wg": [l2g, l4g], "ffn2_wu": [l2u, l4u], "ffn2_wd": [l2d, l4d],
              "conv_w_in": [lcin], "conv_w_out": [lcout], "attn_w_q": [lwq], "attn_w_o": [lwo]}
    weights = dict(meta=meta, ffn1_wg=ffn1_wg, ffn1_wu=ffn1_wu, ffn1_wd=ffn1_wd, ffn2_wg=ffn2_wg,
                   ffn2_wu=ffn2_wu, ffn2_wd=ffn2_wd, ln_gain=ln_gain, ln_bias=ln_bias,
                   conv_w_in=conv_w_in, conv_w=conv_w, conv_w_out=conv_w_out, kv_w=kv_w,
                   f_bias=f_bias, attn_w_q=attn_w_q, attn_w_o=attn_w_o)
    moms = dict(meta=(m_meta, v_meta), ffn1_wg=(m_ffn1_wg, v_ffn1_wg), ffn1_wu=(m_ffn1_wu, v_ffn1_wu),
                ffn1_wd=(m_ffn1_wd, v_ffn1_wd), ffn2_wg=(m_ffn2_wg, v_ffn2_wg), ffn2_wu=(m_ffn2_wu, v_ffn2_wu),
                ffn2_wd=(m_ffn2_wd, v_ffn2_wd), ln_gain=(m_ln_gain, v_ln_gain), ln_bias=(m_ln_bias, v_ln_bias),
                conv_w_in=(m_conv_w_in, v_conv_w_in), conv_w=(m_conv_w, v_conv_w),
                conv_w_out=(m_conv_w_out, v_conv_w_out), kv_w=(m_kv_w, v_kv_w), f_bias=(m_f_bias, v_f_bias),
                attn_w_q=(m_attn_w_q, v_attn_w_q), attn_w_o=(m_attn_w_o, v_attn_w_o))

    names = list(weights)
    g_out, d_out, m_out, v_out = [], [], [], []
    for n in names:
        w = weights[n]
        shp = w.shape
        mm, vv_ = moms[n]
        if n in landed:
            three = (len(landed[n]),) + shp[-2:]
            g, dl, nm, nv = _reduce_adamw(w.reshape(three), mm.reshape(three), vv_.reshape(three),
                                          landed[n], "adamw_" + n)
            g = g.reshape(shp)
        else:
            two = (1, shp[0]) if w.ndim == 1 else (math.prod(shp[:-1]), shp[-1])
            g = grads[n].reshape(shp)
            dl, nm, nv = _adamw(w.reshape(two), g.reshape(two), mm.reshape(two), vv_.reshape(two), "adamw_" + n)
        g_out.append(g)
        d_out.append(dl.reshape(shp))
        m_out.append(nm.reshape(shp))
        v_out.append(nv.reshape(shp))
    return (loss, grad_x, *g_out, *d_out, *m_out, *v_out)
```

```python
import functools
import math

import jax
import jax.numpy as jnp
from jax import lax
from jax.experimental import pallas as pl
from jax.experimental.pallas import tpu as pltpu

F32 = jnp.float32
BF16 = jnp.bfloat16

N_DEV = 8
N_HEADS = 8
N_META = 16
PAD = 112
ROW0 = PAD + N_META
LN_EPS = 1e-5
NEG_INF = -1e30
LOG2E = 1.4426950408889634
ATTN_HEADS_PER_STEP = 8
ATTN_BWD_HEADS_PER_STEP = 2
LANES = 128
MXU_COLS = 256
FFN_FWD_CHUNKS = 11
FFN_BWD_CHUNKS = 4

ADAM_LR = 0.001
ADAM_B1 = 0.9
ADAM_B2 = 0.999
ADAM_EPS = 1e-08
ADAM_WD = 0.01
ADAM_STEP = 10

ROW_TILES = (640, 128)
LOSS_TILE = 128
BF16_ROWS = 16
F32_ROWS = 8
SUM_ROWS_MAX = 768
ADAM_ROWS_MAX = 256
VMEM_BIG = 56 << 20
VMEM_MID = 40 << 20

ANY = pl.BlockSpec(memory_space=pl.ANY)
MESH = pl.DeviceIdType.MESH


def _row_tile(t):
    for c in ROW_TILES:
        if t % c == 0:
            return c
    raise ValueError(f"no row tile for {t}")


def _dot(a, b):
    return jnp.dot(a, b, preferred_element_type=F32)


def _dot_nt(a, b):
    return lax.dot_general(a, b, (((1,), (1,)), ((), ())), preferred_element_type=F32)


def _dot_tn(a, b):
    return lax.dot_general(a, b, (((0,), (0,)), ((), ())), preferred_element_type=F32)


def _params(sem, vmem):
    return pltpu.CompilerParams(dimension_semantics=sem, vmem_limit_bytes=vmem)


def _ln_fwd(z):
    mu = jnp.mean(z, axis=-1, keepdims=True)
    zc = z - mu
    var = jnp.mean(zc * zc, axis=-1, keepdims=True)
    rstd = lax.rsqrt(var + LN_EPS)
    return zc * rstd, rstd


def _ln_bwd(dh, xhat, rstd, gain):
    dxh = dh * gain
    m1 = jnp.mean(dxh, axis=-1, keepdims=True)
    m2 = jnp.mean(dxh * xhat, axis=-1, keepdims=True)
    dz = rstd * (dxh - m1 - xhat * m2)
    return dz, jnp.sum(dh * xhat, axis=0, keepdims=True), jnp.sum(dh, axis=0, keepdims=True)


def _load_resident(pairs, sems):
    cps = [pltpu.make_async_copy(src, dst, sems.at[k]) for k, (src, dst) in enumerate(pairs)]
    for cp in cps:
        cp.start()
    for cp in cps:
        cp.wait()


def _peer_ids():
    mx, my, mc = lax.axis_index("x"), lax.axis_index("y"), lax.axis_index("c")
    peers = []
    for kk in range(1, N_DEV):
        px = 1 - mx if (kk >> 2) & 1 else mx
        py = 1 - my if (kk >> 1) & 1 else my
        pc = 1 - mc if kk & 1 else mc
        peers.append(((px, py, pc), 4 * px + 2 * py + pc))
    return 4 * mx + 2 * my + mc, peers


def _exchange_copies(jobs, send_sems, recv_sems, local_sems, starting):
    me_id, peers = _peer_ids()
    for n, (gather, src, dst) in enumerate(jobs):
        own = pltpu.make_async_copy(src if gather else src.at[me_id], dst.at[me_id], local_sems.at[n])
        own.start() if starting else own.wait()
        for k, (dev, pid) in enumerate(peers):
            sem = (N_DEV - 1) * n + k
            out = src if gather else src.at[pid]
            send = pltpu.make_async_remote_copy(
                src_ref=out, dst_ref=dst.at[me_id], send_sem=send_sems.at[sem], recv_sem=recv_sems.at[sem],
                device_id=dev, device_id_type=MESH)
            if starting:
                send.start()
            else:
                pltpu.make_async_remote_copy(
                    src_ref=out, dst_ref=dst.at[pid], send_sem=send_sems.at[sem], recv_sem=recv_sems.at[sem],
                    device_id=dev, device_id_type=MESH).wait_recv()
                send.wait_send()


def _carried(body, n_in, n_out, carry, first, last):
    nj = len(carry)
    if nj == 0:
        return body

    def wrapped(*refs):
        ins, srcs = refs[:n_in], refs[n_in:n_in + nj]
        outs = refs[n_in + nj:n_in + nj + n_out]
        dsts = refs[n_in + nj + n_out:n_in + 2 * nj + n_out]
        scratch, sems = refs[n_in + 2 * nj + n_out:-3], refs[-3:]
        jobs = [(g, s, r) for (g, _), s, r in zip(carry, srcs, dsts)]

        @pl.when(first())
        def _():
            _exchange_copies(jobs, *sems, starting=True)

        body(*ins, *outs, *scratch)

        @pl.when(last())
        def _():
            _exchange_copies(jobs, *sems, starting=False)

    return wrapped


def _carry_shapes(carry):
    return [jax.ShapeDtypeStruct((N_DEV,) + a.shape if g else a.shape, a.dtype) for g, a in carry]


def _carry_scratch(carry):
    if not carry:
        return []
    n = len(carry)
    return [pltpu.SemaphoreType.DMA(((N_DEV - 1) * n,)), pltpu.SemaphoreType.DMA(((N_DEV - 1) * n,)),
            pltpu.SemaphoreType.DMA((n,))]


def _exchange(carry, name):
    n = len(carry)

    def body(*refs):
        jobs = [(g, s, r) for (g, _), s, r in zip(carry, refs[:n], refs[n:2 * n])]
        _exchange_copies(jobs, *refs[2 * n:], starting=True)
        _exchange_copies(jobs, *refs[2 * n:], starting=False)

    return pl.pallas_call(
        body, name=name, in_specs=[ANY] * n, out_specs=[ANY] * n, out_shape=_carry_shapes(carry),
        scratch_shapes=_carry_scratch(carry),
    )(*[a for _, a in carry])


def _ffn_fwd(xh, gi, bi, wg, wu, wd, go, bo, alpha, name, carry=(), input_t=False):
    t, d = xh.shape
    nch, _, fc = wg.shape
    f = nch * fc
    per = min(FFN_FWD_CHUNKS, nch)
    nc = -(-nch // per)
    tm = _row_tile(t)
    nt = t // tm

    def body(xh_ref, gi_ref, bi_ref, wg_hbm, wu_hbm, wd_hbm, go_ref, bo_ref,
             xo_ref, rs_ref, hb_ref, g_ref, u_ref, *tail):
        hin_ref = tail[0] if input_t else None
        wg_v, wu_v, wd_v, acc, hbs, sems = tail[1:] if input_t else tail
        i = pl.program_id(0)
        c = pl.program_id(1)

        def weight_copies(ck):
            return [pltpu.make_async_copy(src.at[ck], dst.at[ck], sems.at[3 * ck + m])
                    for m, (src, dst) in enumerate(((wg_hbm, wg_v), (wu_hbm, wu_v), (wd_hbm, wd_v)))]

        @pl.when((i == 0) & (c == 0))
        def _():
            for ck in range(nch):
                for cp in weight_copies(ck):
                    cp.start()

        @pl.when(c == 0)
        def _():
            h = xh_ref[...] * gi_ref[...] + bi_ref[...]
            hbs[...] = h.astype(BF16)
            acc[...] = jnp.zeros_like(acc)
            if input_t:
                hin_ref[...] = hbs[...].T

        def chunk(k):
            ck = c * per + k
            cols = slice(k * fc, (k + 1) * fc)

            @pl.when(i == 0)
            def _():
                for cp in weight_copies(ck):
                    cp.wait()

            hb = hbs[...]
            g = _dot(hb, wg_v[ck])
            u = _dot(hb, wu_v[ck])
            a = (g * jax.nn.sigmoid(g)) * u
            g_ref[:, cols] = g.astype(BF16)
            u_ref[:, cols] = u.astype(BF16)
            acc[...] += _dot(a.astype(BF16), wd_v[ck])

        for k in range(per):
            if (nc - 1) * per + k < nch:
                chunk(k)
            else:
                pl.when(c * per + k < nch)(functools.partial(chunk, k))

        @pl.when(c == nc - 1)
        def _():
            h = xh_ref[...] * gi_ref[...] + bi_ref[...]
            xhat, rstd = _ln_fwd(alpha * h + 0.5 * acc[...])
            xo_ref[...] = xhat
            rs_ref[...] = rstd
            hb_ref[...] = (xhat * go_ref[...] + bo_ref[...]).astype(BF16).T

    row = pl.BlockSpec((tm, d), lambda i, c: (i, 0))
    vec = pl.BlockSpec((1, d), lambda i, c: (0, 0))
    chunk = pl.BlockSpec((tm, per * fc), lambda i, c: (i, c))
    first = lambda: (pl.program_id(0) == 0) & (pl.program_id(1) == 0)
    last = lambda: (pl.program_id(0) == nt - 1) & (pl.program_id(1) == nc - 1)
    col = pl.BlockSpec((d, tm), lambda i, c: (0, i))
    t_spec, t_shape = ([col], [jax.ShapeDtypeStruct((d, t), BF16)]) if input_t else ([], [])
    return pl.pallas_call(
        _carried(body, 8, 5 + len(t_spec), carry, first, last), name=name, grid=(nt, nc),
        in_specs=[row, vec, vec, ANY, ANY, ANY, vec, vec] + [ANY] * len(carry),
        out_specs=[row, pl.BlockSpec((tm, 1), lambda i, c: (i, 0)), col, chunk, chunk] + t_spec
                  + [ANY] * len(carry),
        out_shape=[jax.ShapeDtypeStruct((t, d), F32), jax.ShapeDtypeStruct((t, 1), F32),
                   jax.ShapeDtypeStruct((d, t), BF16), jax.ShapeDtypeStruct((t, f), BF16),
                   jax.ShapeDtypeStruct((t, f), BF16)] + t_shape + _carry_shapes(carry),
        scratch_shapes=[pltpu.VMEM((nch, d, fc), BF16), pltpu.VMEM((nch, d, fc), BF16),
                        pltpu.VMEM((nch, fc, d), BF16), pltpu.VMEM((tm, d), F32),
                        pltpu.VMEM((tm, d), BF16), pltpu.SemaphoreType.DMA((3 * nch,))] + _carry_scratch(carry),
        compiler_params=_params(("arbitrary", "arbitrary"), VMEM_BIG),
    )(xh, gi, bi, wg, wu, wd, go, bo, *[a for _, a in carry])


def _ffn_bwd(dh, xo, rs, go, gs, us, wg, wu, wd, alpha, name, carry=(), loss_target=None, loss_bias=None):
    t, d = xo.shape
    nch, _, fc = wg.shape
    f = nch * fc
    per = min(FFN_BWD_CHUNKS, nch)
    nc = -(-nch // per)
    tm = _row_tile(t)
    nt = t // tm

    with_loss = loss_target is not None
    nsub, lead = tm // LOSS_TILE, ROW0 // LOSS_TILE
    nlead = nsub + 1 if with_loss else 1

    def body(*refs):
        lead_refs = refs[:nlead]
        xo_ref, rs_ref, go_ref, g_ref, u_ref, wg_hbm, wu_hbm, wd_hbm = refs[nlead:nlead + 8]
        dhin_ref, dot_ref, dg_ref, du_ref, a_ref, dgain_ref, dbias_ref = refs[nlead + 8:nlead + 15]
        rest = refs[nlead + 15:]
        loss_ref, rest = (rest[0], rest[1:]) if with_loss else (None, rest)
        wg_v, wu_v, wd_v, do_ref, sems = rest[:5]
        i = pl.program_id(0)
        c = pl.program_id(1)

        @pl.when((i == 0) & (c == 0))
        def _():
            _load_resident([(wg_hbm, wg_v), (wu_hbm, wu_v), (wd_hbm, wd_v)], sems)
            dgain_ref[...] = jnp.zeros_like(dgain_ref)
            dbias_ref[...] = jnp.zeros_like(dbias_ref)
            if with_loss:
                rest[5][...] = jnp.zeros_like(rest[5])

        def tile_dh():
            if not with_loss:
                return lead_refs[0][...]
            part = rest[5]
            for k in range(nsub):
                sl = slice(k * LOSS_TILE, (k + 1) * LOSS_TILE)
                rows = i * tm + k * LOSS_TILE + lax.broadcasted_iota(jnp.int32, (LOSS_TILE, 1), 0)
                y = xo_ref[sl, :] * go_ref[...] + lead_refs[nsub][...]
                e = jnp.where(rows >= ROW0, y - lead_refs[k][...], 0.0)
                part[...] += jnp.sum(e * e, axis=0, keepdims=True)
                dhin_ref[sl, :] = e * (1.0 / d)

            @pl.when(i == nt - 1)
            def _():
                loss_ref[...] = jnp.full((1, LANES), 0.5 / d, F32) * jnp.sum(part[...])

            return dhin_ref[...]

        @pl.when(c == 0)
        def _():
            dz, dgp, dbp = _ln_bwd(tile_dh(), xo_ref[...], rs_ref[...], go_ref[...])
            dgain_ref[...] += dgp
            dbias_ref[...] += dbp
            dob = (0.5 * dz).astype(BF16)
            do_ref[...] = dob
            dot_ref[...] = dob.T
            dhin_ref[...] = alpha * dz

        def chunk(k):
            ck = c * per + k
            cols = slice(k * fc, (k + 1) * fc)
            g = g_ref[:, cols].astype(F32)
            u = u_ref[:, cols].astype(F32)
            sg = jax.nn.sigmoid(g)
            sl = g * sg
            da = _dot_nt(do_ref[...], wd_v[ck])
            dgb = (da * u * (sg * (1.0 + g * (1.0 - sg)))).astype(BF16)
            dub = (da * sl).astype(BF16)
            a_ref[:, cols] = (sl * u).astype(BF16)
            dg_ref[:, cols] = dgb
            du_ref[:, cols] = dub
            dhin_ref[...] += _dot_nt(dgb, wg_v[ck]) + _dot_nt(dub, wu_v[ck])

        for k in range(per):
            if (nc - 1) * per + k < nch:
                chunk(k)
            else:
                pl.when(c * per + k < nch)(functools.partial(chunk, k))

    row = pl.BlockSpec((tm, d), lambda i, c: (i, 0))
    vec = pl.BlockSpec((1, d), lambda i, c: (0, 0))
    chunk = pl.BlockSpec((tm, per * fc), lambda i, c: (i, c))
    first = lambda: (pl.program_id(0) == 0) & (pl.program_id(1) == 0)
    last = lambda: (pl.program_id(0) == nt - 1) & (pl.program_id(1) == nc - 1)
    if with_loss:
        lead_specs = [pl.BlockSpec((LOSS_TILE, d), lambda i, c, k=k: (jnp.maximum(i * nsub + k - lead, 0), 0))
                      for k in range(nsub)] + [vec]
        lead_args = [loss_target] * nsub + [loss_bias]
        loss_spec, loss_shape = [pl.BlockSpec((1, LANES), lambda i, c: (0, 0))], [jax.ShapeDtypeStruct((1, LANES), F32)]
        loss_scratch = [pltpu.VMEM((1, d), F32)]
    else:
        lead_specs, lead_args, loss_spec, loss_shape, loss_scratch = [row], [dh], [], [], []
    return pl.pallas_call(
        _carried(body, nlead + 8, 7 + len(loss_spec), carry, first, last), name=name, grid=(nt, nc),
        in_specs=lead_specs + [row, pl.BlockSpec((tm, 1), lambda i, c: (i, 0)), vec, chunk, chunk,
                               ANY, ANY, ANY] + [ANY] * len(carry),
        out_specs=[row, pl.BlockSpec((d, tm), lambda i, c: (0, i)), chunk, chunk, chunk, vec, vec]
                  + loss_spec + [ANY] * len(carry),
        out_shape=[jax.ShapeDtypeStruct((t, d), F32), jax.ShapeDtypeStruct((d, t), BF16),
                   jax.ShapeDtypeStruct((t, f), BF16), jax.ShapeDtypeStruct((t, f), BF16),
                   jax.ShapeDtypeStruct((t, f), BF16), jax.ShapeDtypeStruct((1, d), F32),
                   jax.ShapeDtypeStruct((1, d), F32)] + loss_shape + _carry_shapes(carry),
        scratch_shapes=[pltpu.VMEM((nch, d, fc), BF16), pltpu.VMEM((nch, d, fc), BF16),
                        pltpu.VMEM((nch, fc, d), BF16), pltpu.VMEM((tm, d), BF16),
                        pltpu.SemaphoreType.DMA((3,))] + loss_scratch + _carry_scratch(carry),
        compiler_params=_params(("arbitrary", "arbitrary"), VMEM_BIG),
    )(*lead_args, xo, rs, go, gs, us, wg, wu, wd, *[a for _, a in carry])


def _wgrad(xt, ys, name, carry=()):
    m, t = xt.shape
    n = ys[0].shape[1]
    tn = min(n, MXU_COLS)
    ny = len(ys)

    def body(*refs):
        x_hbm = refs[0]
        y_refs = refs[1:1 + ny]
        o_refs = refs[1 + ny:1 + 2 * ny]
        xv, sems = refs[1 + 2 * ny:]

        @pl.when(pl.program_id(0) == 0)
        def _():
            _load_resident([(x_hbm, xv)], sems)

        for y_ref, o_ref in zip(y_refs, o_refs):
            o_ref[...] = _dot(xv[...], y_ref[...].astype(BF16)).astype(BF16)

    steps = n // tn
    first = lambda: pl.program_id(0) == 0
    last = lambda: pl.program_id(0) == steps - 1
    return pl.pallas_call(
        _carried(body, 1 + ny, ny, carry, first, last), name=name, grid=(steps,),
        in_specs=[ANY] + [pl.BlockSpec((t, tn), lambda c: (0, c)) for _ in ys] + [ANY] * len(carry),
        out_specs=[pl.BlockSpec((m, tn), lambda c: (0, c)) for _ in ys] + [ANY] * len(carry),
        out_shape=[jax.ShapeDtypeStruct((m, n), BF16) for _ in ys] + _carry_shapes(carry),
        scratch_shapes=[pltpu.VMEM((m, t), BF16), pltpu.SemaphoreType.DMA((1,))] + _carry_scratch(carry),
        compiler_params=_params(("arbitrary",), VMEM_BIG),
    )(xt, *ys, *[a for _, a in carry])


def _shift_rows(u, halo, tm):
    r = lax.broadcasted_iota(jnp.int32, (tm, 1), 0)
    u1 = jnp.where(r == 0, halo[7:8], pltpu.roll(u, 1, 0))
    u2 = jnp.where(r == 0, halo[6:7], jnp.where(r == 1, halo[7:8], pltpu.roll(u, 2, 0)))
    return u1, u2


def _conv_fwd(xh, gi, bi, w_in, cw, w_out, go, bo, alpha, name, carry=()):
    t, d = xh.shape
    tm = _row_tile(t)
    nt = t // tm

    def body(xh_ref, gi_ref, bi_ref, win_ref, cw_ref, wout_ref, go_ref, bo_ref,
             xo_ref, rs_ref, hb_ref, p_ref, m_ref, halo):
        i = pl.program_id(0)

        @pl.when(i == 0)
        def _():
            halo[...] = jnp.zeros_like(halo)

        h = xh_ref[...] * gi_ref[...] + bi_ref[...]
        hb = h.astype(BF16)
        bg = _dot(hb, win_ref[:, 0:d])
        cg = _dot(hb, win_ref[:, d:2 * d])
        val = _dot(hb, win_ref[:, 2 * d:3 * d])
        p_ref[:, 0:d] = bg.astype(BF16)
        p_ref[:, d:2 * d] = cg.astype(BF16)
        p_ref[:, 2 * d:3 * d] = val.astype(BF16)
        rows = i * tm + lax.broadcasted_iota(jnp.int32, (tm, 1), 0)
        u = jnp.where(rows >= PAD, cg * val, 0.0)
        u1, u2 = _shift_rows(u, halo[...], tm)
        halo[...] = u[tm - 8:tm]
        y = cw_ref[0:1] * u2 + cw_ref[1:2] * u1 + cw_ref[2:3] * u
        mb = (bg * y).astype(BF16)
        m_ref[...] = mb.T
        xhat, rstd = _ln_fwd(alpha * h + _dot(mb, wout_ref[...]))
        xo_ref[...] = xhat
        rs_ref[...] = rstd
        hb_ref[...] = (xhat * go_ref[...] + bo_ref[...]).astype(BF16).T

    row = pl.BlockSpec((tm, d), lambda i: (i, 0))
    col = pl.BlockSpec((d, tm), lambda i: (0, i))
    vec = pl.BlockSpec((1, d), lambda i: (0, 0))
    first = lambda: pl.program_id(0) == 0
    last = lambda: pl.program_id(0) == nt - 1
    return pl.pallas_call(
        _carried(body, 8, 5, carry, first, last), name=name, grid=(nt,),
        in_specs=[row, vec, vec, pl.BlockSpec((d, 3 * d), lambda i: (0, 0)),
                  pl.BlockSpec((3, d), lambda i: (0, 0)), pl.BlockSpec((d, d), lambda i: (0, 0)),
                  vec, vec] + [ANY] * len(carry),
        out_specs=[row, pl.BlockSpec((tm, 1), lambda i: (i, 0)), col,
                   pl.BlockSpec((tm, 3 * d), lambda i: (i, 0)), col] + [ANY] * len(carry),
        out_shape=[jax.ShapeDtypeStruct((t, d), F32), jax.ShapeDtypeStruct((t, 1), F32),
                   jax.ShapeDtypeStruct((d, t), BF16), jax.ShapeDtypeStruct((t, 3 * d), BF16),
                   jax.ShapeDtypeStruct((d, t), BF16)] + _carry_shapes(carry),
        scratch_shapes=[pltpu.VMEM((8, d), F32)] + _carry_scratch(carry),
        compiler_params=_params(("arbitrary",), VMEM_BIG),
    )(xh, gi, bi, w_in, cw, w_out, go, bo, *[a for _, a in carry])


def _conv_bwd(dh, xo, rs, go, p, cw, w_in, w_out, alpha, name):
    t, d = dh.shape
    tm = _row_tile(t)
    nt = t // tm
    tb = tm // 8

    def body(dh_ref, xo_ref, rs_ref, go_ref, p_ref, ph_ref, cw_ref, win_ref, wout_ref,
             dhin_ref, dmix_ref, dp_ref, dcw_ref, dgain_ref, dbias_ref, carry):
        i = pl.program_id(0)
        tile = nt - 1 - i

        @pl.when(i == 0)
        def _():
            carry[...] = jnp.zeros_like(carry)
            dcw_ref[...] = jnp.zeros_like(dcw_ref)
            dgain_ref[...] = jnp.zeros_like(dgain_ref)
            dbias_ref[...] = jnp.zeros_like(dbias_ref)

        dz, dgp, dbp = _ln_bwd(dh_ref[...], xo_ref[...], rs_ref[...], go_ref[...])
        dgain_ref[...] += dgp
        dbias_ref[...] += dbp
        dmixb = dz.astype(BF16)
        dmix_ref[...] = dmixb
        dm = _dot_nt(dmixb, wout_ref[...])

        bg = p_ref[:, 0:d].astype(F32)
        cg = p_ref[:, d:2 * d].astype(F32)
        val = p_ref[:, 2 * d:3 * d].astype(F32)
        rows = tile * tm + lax.broadcasted_iota(jnp.int32, (tm, 1), 0)
        valid = rows >= PAD
        u = jnp.where(valid, cg * val, 0.0)
        hrows = tile * tm - 8 + lax.broadcasted_iota(jnp.int32, (8, 1), 0)
        hu = jnp.where((hrows >= PAD) & (tile > 0),
                       ph_ref[:, d:2 * d].astype(F32) * ph_ref[:, 2 * d:3 * d].astype(F32), 0.0)
        u1, u2 = _shift_rows(u, hu, tm)
        w0, w1, w2 = cw_ref[0:1], cw_ref[1:2], cw_ref[2:3]
        y = w0 * u2 + w1 * u1 + w2 * u
        dbg = dm * y
        dy = dm * bg
        dcw_ref[0:1] += jnp.sum(dy * u2, axis=0, keepdims=True)
        dcw_ref[1:2] += jnp.sum(dy * u1, axis=0, keepdims=True)
        dcw_ref[2:3] += jnp.sum(dy * u, axis=0, keepdims=True)

        nxt = carry[...]
        r = lax.broadcasted_iota(jnp.int32, (tm, 1), 0)
        dy1 = jnp.where(r == tm - 1, nxt[0:1], pltpu.roll(dy, tm - 1, 0))
        dy2 = jnp.where(r == tm - 2, nxt[0:1],
                        jnp.where(r == tm - 1, nxt[1:2], pltpu.roll(dy, tm - 2, 0)))
        carry[...] = dy[0:8]
        du = jnp.where(valid, w2 * dy + w1 * dy1 + w0 * dy2, 0.0)
        dbgb = dbg.astype(BF16)
        dcgb = (du * val).astype(BF16)
        dvalb = (du * cg).astype(BF16)
        dp_ref[:, 0:d] = dbgb
        dp_ref[:, d:2 * d] = dcgb
        dp_ref[:, 2 * d:3 * d] = dvalb
        dhin_ref[...] = (alpha * dz + _dot_nt(dbgb, win_ref[:, 0:d])
                         + _dot_nt(dcgb, win_ref[:, d:2 * d]) + _dot_nt(dvalb, win_ref[:, 2 * d:3 * d]))

    row = pl.BlockSpec((tm, d), lambda i: (nt - 1 - i, 0))
    vec = pl.BlockSpec((1, d), lambda i: (0, 0))
    prow = pl.BlockSpec((tm, 3 * d), lambda i: (nt - 1 - i, 0))
    return pl.pallas_call(
        body, name=name, grid=(nt,),
        in_specs=[row, row, pl.BlockSpec((tm, 1), lambda i: (nt - 1 - i, 0)), vec, prow,
                  pl.BlockSpec((8, 3 * d), lambda i: (jnp.maximum((nt - 1 - i) * tb - 1, 0), 0)),
                  pl.BlockSpec((3, d), lambda i: (0, 0)),
                  pl.BlockSpec((d, 3 * d), lambda i: (0, 0)), pl.BlockSpec((d, d), lambda i: (0, 0))],
        out_specs=[row, row, prow, pl.BlockSpec((3, d), lambda i: (0, 0)), vec, vec],
        out_shape=[jax.ShapeDtypeStruct((t, d), F32), jax.ShapeDtypeStruct((t, d), BF16),
                   jax.ShapeDtypeStruct((t, 3 * d), BF16), jax.ShapeDtypeStruct((3, d), F32),
                   jax.ShapeDtypeStruct((1, d), F32), jax.ShapeDtypeStruct((1, d), F32)],
        scratch_shapes=[pltpu.VMEM((8, d), F32)],
        compiler_params=_params(("arbitrary",), VMEM_BIG),
    )(dh, xo, rs, go, p, p, cw, w_in, w_out)


def _kv_fwd(xh, gi, bi, wk, wv, wf, fb, name, carry=()):
    t, d = xh.shape
    tm = _row_tile(t)
    nt = t // tm

    def body(xh_ref, gi_ref, bi_ref, wk_ref, wv_ref, wf_ref, fb_ref,
             k_ref, v_ref, lg_ref, c_ref, ct_ref, run):
        i = pl.program_id(0)

        @pl.when(i == 0)
        def _():
            run[...] = jnp.zeros_like(run)

        x = (xh_ref[...] * gi_ref[...] + bi_ref[...]).astype(BF16)
        k_ref[...] = _dot(x, wk_ref[...]).astype(BF16)
        v_ref[...] = _dot(x, wv_ref[...]).astype(BF16)
        logit = _dot(x, wf_ref[...]) + fb_ref[...]
        lg_ref[...] = logit
        logf = jnp.minimum(logit, 0.0) - jnp.log(1.0 + jnp.exp(-jnp.abs(logit)))
        rows = i * tm + lax.broadcasted_iota(jnp.int32, (tm, 1), 0)
        logf = jnp.where(rows >= PAD, logf, 0.0)
        tri = (lax.broadcasted_iota(jnp.int32, (tm, tm), 0)
               >= lax.broadcasted_iota(jnp.int32, (tm, tm), 1)).astype(F32)
        cs = jnp.dot(tri, logf, precision=lax.Precision.HIGHEST, preferred_element_type=F32) + run[...]
        run[...] = cs[tm - 1:tm]
        c_ref[...] = cs
        ct_ref[...] = cs.T

    row = pl.BlockSpec((tm, d), lambda i: (i, 0))
    vec = pl.BlockSpec((1, d), lambda i: (0, 0))
    gate = pl.BlockSpec((tm, LANES), lambda i: (i, 0))
    sq = pl.BlockSpec((d, d), lambda i: (0, 0))
    first = lambda: pl.program_id(0) == 0
    last = lambda: pl.program_id(0) == nt - 1
    return pl.pallas_call(
        _carried(body, 7, 5, carry, first, last), name=name, grid=(nt,),
        in_specs=[row, vec, vec, sq, sq, pl.BlockSpec((d, LANES), lambda i: (0, 0)),
                  pl.BlockSpec((1, LANES), lambda i: (0, 0))] + [ANY] * len(carry),
        out_specs=[row, row, gate, gate, pl.BlockSpec((LANES, tm), lambda i: (0, i))] + [ANY] * len(carry),
        out_shape=[jax.ShapeDtypeStruct((t, d), BF16), jax.ShapeDtypeStruct((t, d), BF16),
                   jax.ShapeDtypeStruct((t, LANES), F32), jax.ShapeDtypeStruct((t, LANES), F32),
                   jax.ShapeDtypeStruct((LANES, t), F32)] + _carry_shapes(carry),
        scratch_shapes=[pltpu.VMEM((1, LANES), F32)] + _carry_scratch(carry),
        compiler_params=_params(("arbitrary",), VMEM_MID),
    )(xh, gi, bi, wk, wv, wf, fb, *[a for _, a in carry])


def _kv_bwd(dk, dv, dcs, dcq, logit, dh_other, wk, wv, wf, name):
    t, d = dk.shape
    tm = _row_tile(t)
    nt = t // tm

    def body(dk_ref, dv_ref, dcs_ref, dcq_ref, lg_ref, oth_ref, wk_ref, wv_ref, wf_ref,
             dh_ref, dl_ref, dfb_ref, run):
        i = pl.program_id(0)
        tile = nt - 1 - i

        @pl.when(i == 0)
        def _():
            run[...] = jnp.zeros_like(run)
            dfb_ref[...] = jnp.zeros_like(dfb_ref)

        lane = lax.broadcasted_iota(jnp.int32, (tm, LANES), 1)
        dc = dcq_ref[...]
        for hh in range(N_HEADS):
            dc = dc + jnp.where(lane == hh, jnp.sum(dcs_ref[hh], axis=1, keepdims=True), 0.0)
        tri = (lax.broadcasted_iota(jnp.int32, (tm, tm), 0)
               <= lax.broadcasted_iota(jnp.int32, (tm, tm), 1)).astype(F32)
        dlf = jnp.dot(tri, dc, precision=lax.Precision.HIGHEST, preferred_element_type=F32) + run[...]
        run[...] = dlf[0:1]
        rows = tile * tm + lax.broadcasted_iota(jnp.int32, (tm, 1), 0)
        dlogit = jnp.where(rows >= PAD, dlf * jax.nn.sigmoid(-lg_ref[...]), 0.0)
        dfb_ref[...] += jnp.sum(dlogit, axis=0, keepdims=True)
        dlb = dlogit.astype(BF16)
        dl_ref[...] = dlb
        dh_ref[...] = (oth_ref[...] + _dot_nt(dk_ref[...], wk_ref[...])
                       + _dot_nt(dv_ref[...], wv_ref[...]) + _dot_nt(dlb, wf_ref[...]))

    row = pl.BlockSpec((tm, d), lambda i: (nt - 1 - i, 0))
    gate = pl.BlockSpec((tm, LANES), lambda i: (nt - 1 - i, 0))
    sq = pl.BlockSpec((d, d), lambda i: (0, 0))
    return pl.pallas_call(
        body, name=name, grid=(nt,),
        in_specs=[row, row, pl.BlockSpec((N_HEADS, tm, LANES), lambda i: (0, nt - 1 - i, 0)), gate, gate, row,
                  sq, sq, pl.BlockSpec((d, LANES), lambda i: (0, 0))],
        out_specs=[row, gate, pl.BlockSpec((1, LANES), lambda i: (0, 0))],
        out_shape=[jax.ShapeDtypeStruct((t, d), F32), jax.ShapeDtypeStruct((t, LANES), BF16),
                   jax.ShapeDtypeStruct((1, LANES), F32)],
        scratch_shapes=[pltpu.VMEM((1, LANES), F32)],
        compiler_params=_params(("arbitrary",), VMEM_MID),
    )(dk, dv, dcs, dcq, logit, dh_other, wk, wv, wf)


def _proj(xh, gi, bi, w, name):
    t, k = xh.shape
    n = w.shape[1]
    tm = _row_tile(t)

    def body(x_ref, g_ref, b_ref, w_ref, o_ref):
        x = (x_ref[...] * g_ref[...] + b_ref[...]).astype(BF16)
        o_ref[...] = _dot(x, w_ref[...]).astype(BF16)

    vec = pl.BlockSpec((1, k), lambda i: (0, 0))
    return pl.pallas_call(
        body, name=name, grid=(t // tm,),
        in_specs=[pl.BlockSpec((tm, k), lambda i: (i, 0)), vec, vec, pl.BlockSpec((k, n), lambda i: (0, 0))],
        out_specs=pl.BlockSpec((tm, n), lambda i: (i, 0)),
        out_shape=jax.ShapeDtypeStruct((t, n), BF16),
        compiler_params=_params(("arbitrary",), VMEM_MID),
    )(xh, gi, bi, w)


def _add_proj_nt(base, y, w, name):
    t, n = y.shape
    k = w.shape[0]
    tm = _row_tile(t)

    def body(b_ref, y_ref, w_ref, o_ref):
        o_ref[...] = b_ref[...] + _dot_nt(y_ref[...].astype(BF16), w_ref[...])

    return pl.pallas_call(
        body, name=name, grid=(t // tm,),
        in_specs=[pl.BlockSpec((tm, k), lambda i: (i, 0)), pl.BlockSpec((tm, n), lambda i: (i, 0)),
                  pl.BlockSpec((k, n), lambda i: (0, 0))],
        out_specs=pl.BlockSpec((tm, k), lambda i: (i, 0)),
        out_shape=jax.ShapeDtypeStruct((t, k), F32),
        compiler_params=_params(("arbitrary",), VMEM_MID),
    )(base, y, w)


def _attn_out_fwd(ot, xh, gi, bi, w_o, go, bo, alpha, name):
    t, d = xh.shape
    tm = _row_tile(t)

    def body(ot_ref, xh_ref, gi_ref, bi_ref, wo_ref, go_ref, bo_ref, xo_ref, rs_ref, hb_ref):
        h = xh_ref[...] * gi_ref[...] + bi_ref[...]
        xhat, rstd = _ln_fwd(alpha * h + _dot_tn(ot_ref[...], wo_ref[...]))
        xo_ref[...] = xhat
        rs_ref[...] = rstd
        hb_ref[...] = (xhat * go_ref[...] + bo_ref[...]).astype(BF16).T

    row = pl.BlockSpec((tm, d), lambda i: (i, 0))
    col = pl.BlockSpec((d, tm), lambda i: (0, i))
    vec = pl.BlockSpec((1, d), lambda i: (0, 0))
    return pl.pallas_call(
        body, name=name, grid=(t // tm,),
        in_specs=[col, row, vec, vec, pl.BlockSpec((d, d), lambda i: (0, 0)), vec, vec],
        out_specs=[row, pl.BlockSpec((tm, 1), lambda i: (i, 0)), col],
        out_shape=[jax.ShapeDtypeStruct((t, d), F32), jax.ShapeDtypeStruct((t, 1), F32),
                   jax.ShapeDtypeStruct((d, t), BF16)],
        compiler_params=_params(("arbitrary",), VMEM_MID),
    )(ot, xh, gi, bi, w_o, go, bo)


def _attn_out_bwd(dh, xo, rs, go, ot, w_o, alpha, name):
    t, d = dh.shape
    tm = _row_tile(t)
    hd = d // N_HEADS

    def body(dh_ref, xo_ref, rs_ref, go_ref, ot_ref, wo_ref,
             dres_ref, dmix_ref, dot_ref, delta_ref, dgain_ref, dbias_ref):
        @pl.when(pl.program_id(0) == 0)
        def _():
            dgain_ref[...] = jnp.zeros_like(dgain_ref)
            dbias_ref[...] = jnp.zeros_like(dbias_ref)

        dz, dgp, dbp = _ln_bwd(dh_ref[...], xo_ref[...], rs_ref[...], go_ref[...])
        dgain_ref[...] += dgp
        dbias_ref[...] += dbp
        dres_ref[...] = alpha * dz
        dmixb = dz.astype(BF16)
        dmix_ref[...] = dmixb
        dot_t = _dot_nt(wo_ref[...], dmixb)
        dot_ref[...] = dot_t.astype(BF16)
        prod = dot_t * ot_ref[...].astype(F32)
        delta_ref[...] = jnp.sum(prod.reshape(N_HEADS, hd, tm), axis=1)

    row = pl.BlockSpec((tm, d), lambda i: (i, 0))
    vec = pl.BlockSpec((1, d), lambda i: (0, 0))
    col = pl.BlockSpec((d, tm), lambda i: (0, i))
    return pl.pallas_call(
        body, name=name, grid=(t // tm,),
        in_specs=[row, row, pl.BlockSpec((tm, 1), lambda i: (i, 0)), vec, col,
                  pl.BlockSpec((d, d), lambda i: (0, 0))],
        out_specs=[row, row, col, pl.BlockSpec((N_HEADS, tm), lambda i: (0, i)), vec, vec],
        out_shape=[jax.ShapeDtypeStruct((t, d), F32), jax.ShapeDtypeStruct((t, d), BF16),
                   jax.ShapeDtypeStruct((d, t), BF16), jax.ShapeDtypeStruct((N_HEADS, t), F32),
                   jax.ShapeDtypeStruct((1, d), F32), jax.ShapeDtypeStruct((1, d), F32)],
        compiler_params=_params(("arbitrary",), VMEM_MID),
    )(dh, xo, rs, go, ot, w_o)


def _scores_t(k, q, ct_ref, c_ref, h, i, j, tq, tk, scale, masked):
    sub = lax.broadcasted_iota(jnp.int32, (8, tq), 0)
    cq = jnp.sum(jnp.where(sub == h, ct_ref[...], 0.0), axis=0, keepdims=True) * LOG2E
    lane = lax.broadcasted_iota(jnp.int32, (tk, LANES), 1)
    ck = jnp.sum(jnp.where(lane == h, c_ref[...], 0.0), axis=1, keepdims=True) * LOG2E
    st = _dot_nt(k, q) * (scale * LOG2E) - ck
    if masked:
        kpos = j * tk + lax.broadcasted_iota(jnp.int32, (tk, 1), 0)
        qpos = i * tq + lax.broadcasted_iota(jnp.int32, (1, tq), 1)
        st = jnp.where((kpos <= qpos) & (kpos >= PAD), st, NEG_INF)
    return st, cq


def _tri_pairs(n, by_row):
    if by_row:
        pairs = [(i, j) for i in range(n) for j in range(i + 1)]
    else:
        pairs = [(i, j) for j in range(n) for i in range(j, n)]
    return (jnp.asarray([p[0] for p in pairs], jnp.int32), jnp.asarray([p[1] for p in pairs], jnp.int32))


def _attn_fwd(q, k, v, c, ct, name, carry=()):
    t, d = q.shape
    hd = d // N_HEADS
    tq = tk = _row_tile(t)
    nq = t // tq
    scale = 1.0 / math.sqrt(hd)

    hps = ATTN_HEADS_PER_STEP

    def body(it_ref, jt_ref, q_ref, k_ref, v_ref, c_ref, ct_ref, ot_ref, lse_ref, m_s, l_s, acc):
        hp, p_ = pl.program_id(0), pl.program_id(1)
        i, j = it_ref[p_], jt_ref[p_]

        @pl.when(j == 0)
        def _():
            m_s[...] = jnp.full_like(m_s, NEG_INF)
            l_s[...] = jnp.zeros_like(l_s)
            acc[...] = jnp.zeros_like(acc)

        def update(masked):
            scores = []
            for e in range(hps):
                cols = slice(e * hd, (e + 1) * hd)
                scores.append(_scores_t(k_ref[:, cols], q_ref[:, cols], ct_ref, c_ref, hp * hps + e,
                                        i, j, tq, tk, scale, masked))
            probs = []
            for e, (st, cq) in enumerate(scores):
                m_new = jnp.maximum(m_s[e], jnp.max(st, axis=0, keepdims=True) + cq)
                a = jnp.exp2(m_s[e] - m_new)
                p = jnp.exp2(st - (m_new - cq))
                l_s[e] = a * l_s[e] + jnp.sum(p, axis=0, keepdims=True)
                m_s[e] = m_new
                probs.append((a, p.astype(BF16)))
            for e, (a, pb) in enumerate(probs):
                acc[e] = a * acc[e] + _dot_tn(v_ref[:, e * hd:(e + 1) * hd], pb)

        edge = (j == i) | (j == 0)
        pl.when(edge)(lambda: update(True))
        pl.when(jnp.logical_not(edge))(lambda: update(False))

        @pl.when(j == i)
        def _():
            for e in range(hps):
                ot_ref[e * hd:(e + 1) * hd, :] = (acc[e] / l_s[e]).astype(BF16)
                lse_ref[e] = m_s[e] + jnp.log2(l_s[e])

    it, jt = _tri_pairs(nq, by_row=True)
    npairs = it.shape[0]
    nhp = N_HEADS // hps
    kv = pl.BlockSpec((tk, hps * hd), lambda h, p, it, jt: (jt[p], h))
    first = lambda: (pl.program_id(0) == 0) & (pl.program_id(1) == 0)
    last = lambda: (pl.program_id(0) == nhp - 1) & (pl.program_id(1) == npairs - 1)
    return pl.pallas_call(
        _carried(body, 7, 2, carry, first, last), name=name,
        grid_spec=pltpu.PrefetchScalarGridSpec(
            num_scalar_prefetch=2, grid=(nhp, npairs),
            in_specs=[pl.BlockSpec((tq, hps * hd), lambda h, p, it, jt: (it[p], h)), kv, kv,
                      pl.BlockSpec((tk, LANES), lambda h, p, it, jt: (jt[p], 0)),
                      pl.BlockSpec((8, tq), lambda h, p, it, jt: (0, it[p]))] + [ANY] * len(carry),
            out_specs=[pl.BlockSpec((hps * hd, tq), lambda h, p, it, jt: (h, it[p])),
                       pl.BlockSpec((hps, 1, tq), lambda h, p, it, jt: (h, 0, it[p]))] + [ANY] * len(carry),
            scratch_shapes=[pltpu.VMEM((hps, 1, tq), F32), pltpu.VMEM((hps, 1, tq), F32),
                            pltpu.VMEM((hps, hd, tq), F32)] + _carry_scratch(carry)),
        out_shape=[jax.ShapeDtypeStruct((d, t), BF16), jax.ShapeDtypeStruct((N_HEADS, 1, t), F32)]
                  + _carry_shapes(carry),
        compiler_params=_params(("arbitrary", "arbitrary"), VMEM_MID),
    )(it, jt, q, k, v, c, ct, *[a for _, a in carry])


def _attn_bwd(q, k, v, c, ct, lse, delta, dot_t, name, carry=()):
    t, d = q.shape
    hd = d // N_HEADS
    tq = tk = _row_tile(t)
    nq = t // tq
    scale = 1.0 / math.sqrt(hd)
    hps = ATTN_BWD_HEADS_PER_STEP

    steps = [(j, i, min(i + 1, nq - 1), int(i + 1 < nq)) for j in range(nq) for i in range(j, nq, 2)]
    jt, ia, ib, vb = (jnp.asarray([s[n] for s in steps], jnp.int32) for n in range(4))

    def body(jt_ref, ia_ref, ib_ref, vb_ref, qa_ref, k_ref, v_ref, c_ref, cta_ref, lsea_ref, deltaa_ref, dota_ref,
             qb_ref, ctb_ref, lseb_ref, deltab_ref, dotb_ref,
             dq_ref, dk_ref, dv_ref, dcs_ref, drow_ref, dk_acc, dv_acc, dc_acc):
        hp, p_ = pl.program_id(0), pl.program_id(1)
        j, i_a, i_b, has_b = jt_ref[p_], ia_ref[p_], ib_ref[p_], vb_ref[p_] == 1

        @pl.when(p_ == 0)
        def _():
            dq_ref[...] = jnp.zeros_like(dq_ref)
            drow_ref[...] = jnp.zeros_like(drow_ref)

        @pl.when(i_a == j)
        def _():
            dk_acc[...] = jnp.zeros_like(dk_acc)
            dv_acc[...] = jnp.zeros_like(dv_acc)
            dc_acc[...] = jnp.zeros_like(dc_acc)

        def update(q_ref, ct_ref, lse_ref, delta_ref, dot_ref, i, masked):
            sub = lax.broadcasted_iota(jnp.int32, (8, tq), 0)
            rows = pl.ds(pl.multiple_of(i * tq, tq), tq)
            stage = []
            for e in range(hps):
                cols = slice(e * hd, (e + 1) * hd)
                st, cq = _scores_t(k_ref[:, cols], q_ref[:, cols], ct_ref, c_ref, hp * hps + e,
                                   i, j, tq, tk, scale, masked)
                dp = _dot(v_ref[:, cols], dot_ref[cols, :])
                stage.append((st, cq, dp))
            grads = []
            for e, (st, cq, dp) in enumerate(stage):
                p = jnp.exp2(st - (lse_ref[e] - cq))
                dl = jnp.sum(jnp.where(sub == hp * hps + e, delta_ref[...], 0.0), axis=0, keepdims=True)
                ds = p * (dp - dl)
                part = ds[:, 0:LANES]
                for g in range(1, tq // LANES):
                    part = part + ds[:, g * LANES:(g + 1) * LANES]
                dc_acc[e] += part
                drow_ref[e, i] += jnp.broadcast_to(jnp.sum(ds, axis=0, keepdims=True), (8, tq))
                grads.append((p.astype(BF16), ds.astype(BF16)))
            for e, (pb, dsb) in enumerate(grads):
                cols = slice(e * hd, (e + 1) * hd)
                dv_acc[e] += _dot_nt(pb, dot_ref[cols, :])
                dk_acc[e] += _dot(dsb, q_ref[:, cols]) * scale
                dq_ref[rows, cols] += _dot_tn(dsb, k_ref[:, cols]) * scale

        slot_a = (qa_ref, cta_ref, lsea_ref, deltaa_ref, dota_ref, i_a)
        slot_b = (qb_ref, ctb_ref, lseb_ref, deltab_ref, dotb_ref, i_b)
        edge_a = (j == i_a) | (j == 0)
        pl.when(edge_a)(lambda: update(*slot_a, True))
        pl.when(jnp.logical_not(edge_a))(lambda: update(*slot_a, False))
        pl.when(has_b & (j == 0))(lambda: update(*slot_b, True))
        pl.when(has_b & (j != 0))(lambda: update(*slot_b, False))

        @pl.when((i_a == nq - 1) | (has_b & (i_b == nq - 1)))
        def _():
            for e in range(hps):
                cols = slice(e * hd, (e + 1) * hd)
                dk_ref[:, cols] = dk_acc[e].astype(BF16)
                dv_ref[:, cols] = dv_acc[e].astype(BF16)
                dcs_ref[e] = -dc_acc[e]

    nsteps = len(steps)
    nhp = N_HEADS // hps
    kv = pl.BlockSpec((tk, hps * hd), lambda h, p, jt, ia, ib, vb: (jt[p], h))

    def q_side(sel):
        return [pl.BlockSpec((tq, hps * hd), lambda h, p, jt, ia, ib, vb: (sel(ia, ib)[p], h)),
                pl.BlockSpec((8, tq), lambda h, p, jt, ia, ib, vb: (0, sel(ia, ib)[p])),
                pl.BlockSpec((hps, 1, tq), lambda h, p, jt, ia, ib, vb: (h, 0, sel(ia, ib)[p])),
                pl.BlockSpec((N_HEADS, tq), lambda h, p, jt, ia, ib, vb: (0, sel(ia, ib)[p])),
                pl.BlockSpec((hps * hd, tq), lambda h, p, jt, ia, ib, vb: (h, sel(ia, ib)[p]))]

    qa_specs, qb_specs = q_side(lambda ia, ib: ia), q_side(lambda ia, ib: ib)
    first = lambda: (pl.program_id(0) == 0) & (pl.program_id(1) == 0)
    last = lambda: (pl.program_id(0) == nhp - 1) & (pl.program_id(1) == nsteps - 1)
    q_args = (q, ct, lse, delta, dot_t)
    return pl.pallas_call(
        _carried(body, 17, 5, carry, first, last), name=name,
        grid_spec=pltpu.PrefetchScalarGridSpec(
            num_scalar_prefetch=4, grid=(nhp, nsteps),
            in_specs=[qa_specs[0], kv, kv, pl.BlockSpec((tk, LANES), lambda h, p, jt, ia, ib, vb: (jt[p], 0))]
                     + qa_specs[1:] + qb_specs + [ANY] * len(carry),
            out_specs=[pl.BlockSpec((t, hps * hd), lambda h, p, jt, ia, ib, vb: (0, h)), kv, kv,
                       pl.BlockSpec((hps, tk, LANES), lambda h, p, jt, ia, ib, vb: (h, jt[p], 0)),
                       pl.BlockSpec((hps, nq, 8, tq), lambda h, p, jt, ia, ib, vb: (h, 0, 0, 0))]
                      + [ANY] * len(carry),
            scratch_shapes=[pltpu.VMEM((hps, tk, hd), F32), pltpu.VMEM((hps, tk, hd), F32),
                            pltpu.VMEM((hps, tk, LANES), F32)] + _carry_scratch(carry)),
        out_shape=[jax.ShapeDtypeStruct((t, d), F32), jax.ShapeDtypeStruct((t, d), BF16),
                   jax.ShapeDtypeStruct((t, d), BF16), jax.ShapeDtypeStruct((N_HEADS, t, LANES), F32),
                   jax.ShapeDtypeStruct((N_HEADS, nq, 8, tq), F32)] + _carry_shapes(carry),
        compiler_params=_params(("arbitrary", "arbitrary"), VMEM_BIG),
    )(jt, ia, ib, vb, q, k, v, c, ct, lse, delta, dot_t, *q_args, *[a for _, a in carry])


def _adamw(w, g, m, v, name):
    r, c = w.shape
    tr = r
    for cand in (256, 128, 64, 32, 16, 8):
        if r % cand == 0 and r > cand:
            tr = cand
            break
    bc1 = 1.0 - ADAM_B1 ** ADAM_STEP
    bc2 = 1.0 - ADAM_B2 ** ADAM_STEP

    def body(w_ref, g_ref, m_ref, v_ref, d_ref, nm_ref, nv_ref):
        gg = g_ref[...]
        nm = ADAM_B1 * m_ref[...] + (1.0 - ADAM_B1) * gg
        nv = ADAM_B2 * v_ref[...] + (1.0 - ADAM_B2) * (gg * gg)
        d_ref[...] = -ADAM_LR * ((nm / bc1) / (jnp.sqrt(nv / bc2) + ADAM_EPS) + ADAM_WD * w_ref[...])
        nm_ref[...] = nm
        nv_ref[...] = nv

    blk = pl.BlockSpec((tr, c), lambda i: (i, 0))
    shp = jax.ShapeDtypeStruct((r, c), F32)
    return pl.pallas_call(
        body, name=name, grid=(r // tr,), in_specs=[blk] * 4, out_specs=[blk] * 3,
        out_shape=[shp] * 3, compiler_params=_params(("arbitrary",), VMEM_MID),
    )(w, g, m, v)


def _reduce_adamw(w, m, v, landed, name):
    nl, r, c = w.shape
    tr = next(cand for cand in range(min(r, ADAM_ROWS_MAX), 0, -BF16_ROWS) if r % cand == 0)
    nr = r // tr
    bc1 = 1.0 - ADAM_B1 ** ADAM_STEP
    bc2 = 1.0 - ADAM_B2 ** ADAM_STEP

    def body(*refs):
        w_ref, m_ref, v_ref = refs[:3]
        src_refs = refs[3:3 + nl]
        g_ref, d_ref, nm_ref, nv_ref = refs[3 + nl:]

        def update(src):
            gg = src[0].astype(F32)
            for s in range(1, N_DEV):
                gg = gg + src[s].astype(F32)
            nm = ADAM_B1 * m_ref[0] + (1.0 - ADAM_B1) * gg
            nv = ADAM_B2 * v_ref[0] + (1.0 - ADAM_B2) * (gg * gg)
            g_ref[0] = gg
            d_ref[0] = -ADAM_LR * ((nm / bc1) / (jnp.sqrt(nv / bc2) + ADAM_EPS) + ADAM_WD * w_ref[0])
            nm_ref[0] = nm
            nv_ref[0] = nv

        for idx in range(nl):
            pl.when(pl.program_id(0) == idx)(functools.partial(update, src_refs[idx]))

    def src_spec(idx):
        return pl.BlockSpec((N_DEV, tr, c),
                            lambda l, i: (0, jnp.where(l == idx, i, jnp.where(l < idx, 0, nr - 1)), 0))

    blk = pl.BlockSpec((1, tr, c), lambda l, i: (l, i, 0))
    shp = jax.ShapeDtypeStruct((nl, r, c), F32)
    return pl.pallas_call(
        body, name=name, grid=(nl, nr), in_specs=[blk] * 3 + [src_spec(idx) for idx in range(nl)],
        out_specs=[blk] * 4, out_shape=[shp] * 4,
        compiler_params=_params(("arbitrary", "arbitrary"), VMEM_MID),
    )(w, m, v, *landed)


def _sum_sources(r, name):
    n, rows, c = r.shape
    tr = next(cand for cand in range(min(rows, SUM_ROWS_MAX), 0, -BF16_ROWS) if rows % cand == 0)

    def body(r_ref, o_ref):
        acc = r_ref[0].astype(F32)
        for s in range(1, n):
            acc = acc + r_ref[s].astype(F32)
        o_ref[...] = acc

    return pl.pallas_call(
        body, name=name, grid=(rows // tr,),
        in_specs=[pl.BlockSpec((n, tr, c), lambda i: (0, i, 0))],
        out_specs=pl.BlockSpec((tr, c), lambda i: (i, 0)),
        out_shape=jax.ShapeDtypeStruct((rows, c), F32),
        compiler_params=_params(("arbitrary",), VMEM_MID),
    )(r)


def _all_gather(parts, name):
    n = len(parts)

    def body(*refs):
        x_refs, out_refs = refs[:n], refs[n:2 * n]
        send_sems, recv_sems, local_sems = refs[2 * n:]
        mx, my, mc = lax.axis_index("x"), lax.axis_index("y"), lax.axis_index("c")
        me, sibling = (mx, my, mc), (mx, my, 1 - mc)
        chips = [(1 - mx, my), (mx, 1 - my), (1 - mx, 1 - my)]

        def copy(p, k, block, to, from_input=False):
            px, py, pc = block
            rows = out_refs[p].at[4 * px + 2 * py + pc]
            return pltpu.make_async_remote_copy(
                src_ref=x_refs[p] if from_input else rows, dst_ref=rows,
                send_sem=send_sems.at[7 * p + k], recv_sem=recv_sems.at[7 * p + k],
                device_id=to, device_id_type=MESH)

        mine, sent = [], []
        for p in range(n):
            own = pltpu.make_async_copy(x_refs[p], out_refs[p].at[4 * mx + 2 * my + mc], local_sems.at[p])
            own.start()
            mine.append(own)
            first = [copy(p, 0, me, sibling, True)]
            first += [copy(p, 1 + j, me, (*chip, mc), True) for j, chip in enumerate(chips)]
            for cp in first:
                cp.start()
            sent += first
        for p in range(n):
            for j, chip in enumerate(chips):
                copy(p, 1 + j, (*chip, mc), me).wait_recv()
                fwd = copy(p, 4 + j, (*chip, mc), sibling)
                fwd.start()
                sent.append(fwd)
        for p in range(n):
            copy(p, 0, sibling, me).wait_recv()
            for j, chip in enumerate(chips):
                copy(p, 4 + j, (*chip, 1 - mc), me).wait_recv()
        for cp in sent:
            cp.wait_send()
        for own in mine:
            own.wait()

    return pl.pallas_call(
        body, name=name, in_specs=[ANY] * n, out_specs=[ANY] * n,
        out_shape=[jax.ShapeDtypeStruct((N_DEV,) + a.shape, a.dtype) for a in parts],
        scratch_shapes=[pltpu.SemaphoreType.DMA((7 * n,)), pltpu.SemaphoreType.DMA((7 * n,)),
                        pltpu.SemaphoreType.DMA((n,))],
    )(*parts)


def _pack_rows(parts, width, mult, lead=0):
    out = []
    for a in parts:
        head = a.shape[:lead]
        flat = a.reshape(head + (-1,))
        padn = (-flat.shape[-1]) % (width * mult)
        if padn:
            flat = jnp.pad(flat, [(0, 0)] * lead + [(0, padn)])
        out.append(flat.reshape(head + (-1, width)))
    return jnp.concatenate(out, axis=lead)


def _rows_of(shape, width, mult):
    n = math.prod(shape)
    per = width * mult
    return ((n + per - 1) // per) * mult


def _unpack_rows(buf, shapes, width, mult):
    lead = buf.shape[:-2]
    out, off = [], 0
    for shp in shapes:
        r = _rows_of(shp, width, mult)
        flat = buf[..., off:off + r, :].reshape(lead + (r * width,))
        out.append(flat[..., :math.prod(shp)].reshape(lead + tuple(shp)))
        off += r
    return out


def _cols_from_devices(g):
    nd = g.ndim
    perm = tuple(range(1, nd - 1)) + (0, nd - 1)
    t = jnp.transpose(g, perm)
    return t.reshape(t.shape[:-2] + (t.shape[-2] * t.shape[-1],))


def _cols_to_devices(a):
    c = a.shape[-1] // N_DEV
    t = a.reshape(a.shape[:-1] + (N_DEV, c))
    nd = t.ndim
    perm = (nd - 2,) + tuple(range(0, nd - 2)) + (nd - 1,)
    return jnp.transpose(t, perm)


WIDTH = 1024


def kernel(x, meta, ffn1_wg, ffn1_wu, ffn1_wd, ffn2_wg, ffn2_wu, ffn2_wd, ln_gain, ln_bias, conv_w_in, conv_w, conv_w_out, kv_w, f_bias, attn_w_q, attn_w_o, loss_target, m_meta, m_ffn1_wg, m_ffn1_wu, m_ffn1_wd, m_ffn2_wg, m_ffn2_wu, m_ffn2_wd, m_ln_gain, m_ln_bias, m_conv_w_in, m_conv_w, m_conv_w_out, m_kv_w, m_f_bias, m_attn_w_q, m_attn_w_o, v_meta, v_ffn1_wg, v_ffn1_wu, v_ffn1_wd, v_ffn2_wg, v_ffn2_wu, v_ffn2_wd, v_ln_gain, v_ln_bias, v_conv_w_in, v_conv_w, v_conv_w_out, v_kv_w, v_f_bias, v_attn_w_q, v_attn_w_o):
    depth = ln_gain.shape[0]
    alpha = float((2 * depth) ** 0.25)
    d = x.shape[-1]
    seq = x.shape[1]
    t = ROW0 + seq
    fsh = ffn1_wg.shape[-1]
    f = fsh * N_DEV
    fck = MXU_COLS
    nc = f // fck
    me = 4 * lax.axis_index("x") + 2 * lax.axis_index("y") + lax.axis_index("c")

    def gather_of(parts):
        return [(True, a.astype(BF16)) for a in parts]

    small = [meta, ln_gain, ln_bias, conv_w]
    small_shapes = [a.shape for a in small]
    g1g, g1u, g1d, gcin, gsmall = _all_gather(
        [a.astype(BF16) for a in (ffn1_wg[0], ffn1_wu[0], ffn1_wd[0], conv_w_in[0])]
        + [_pack_rows(small, WIDTH, F32_ROWS)], "ag_first")
    gmeta, ggain, gbias, gcw = _unpack_rows(gsmall, small_shapes, WIDTH, F32_ROWS)

    def ffn_chunks(gg, gu, gd):
        up = lambda g: jnp.transpose(_cols_from_devices(g).reshape(d, nc, fck), (1, 0, 2))
        return up(gg), up(gu), gd.reshape(nc, fck, d)

    w_in = _cols_from_devices(gcin)
    fb =jnp.pad(f_bias, (0, LANES - N_HEADS)).reshape(1, LANES)
    meta_f = _cols_from_devices(gmeta)
    gain_f = _cols_from_devices(ggain)
    bias_f = _cols_from_devices(gbias)
    cw_f = _cols_from_devices(gcw)[0]

    def gb(l, n):
        return gain_f[l, n].reshape(1, d), bias_f[l, n].reshape(1, d)

    ones = jnp.ones((1, d), F32)
    zeros = jnp.zeros((1, d), F32)

    h0 = jnp.concatenate([jnp.zeros((PAD, d), F32), meta_f, x[0]], axis=0)

    w1 = ffn_chunks(g1g, g1u, g1d)
    g00, b00 = gb(0, 0)
    xh1, rs1, hb1, gg1, uu1, hb0, gcout, g2g, g2u = _ffn_fwd(
        h0, ones, zeros, *w1, g00, b00, alpha, "ffn_fwd_0a",
        carry=gather_of([conv_w_out[0], ffn2_wg[0], ffn2_wu[0]]), input_t=True)
    w_out = gcout.reshape(d, d)
    g01, b01 = gb(0, 1)
    xh2, rs2, hb2, pp, mb, g2d, gkv = _conv_fwd(
        xh1, g00, b00, w_in, cw_f, w_out, g01, b01, alpha, "conv_fwd", carry=gather_of([ffn2_wd[0], kv_w.T]))
    w2 = ffn_chunks(g2g, g2u, g2d)
    g02, b02 = gb(0, 2)
    xh3, rs3, hb3, gg3, uu3, g3g, g3u = _ffn_fwd(
        xh2, g01, b01, *w2, g02, b02, alpha, "ffn_fwd_0b", carry=gather_of([ffn1_wg[1], ffn1_wu[1]]))
    kvw = gkv.reshape(gkv.shape[0] * gkv.shape[1], d).T
    wk, wv = kvw[:, :d], kvw[:, d:2 * d]
    wf = jnp.pad(kvw[:, 2 * d:], ((0, 0), (0, LANES - N_HEADS)))
    kk, vv, logit, cc, cct, g3d = _kv_fwd(xh3, g02, b02, wk, wv, wf, fb, "kv_fwd",
                                          carry=gather_of([ffn1_wd[1]]))

    w3 = ffn_chunks(g3g, g3u, g3d)
    g10, b10 = gb(1, 0)
    xh4, rs4, hb4, gg4, uu4, gwq, g4g, g4u = _ffn_fwd(
        xh3, g02, b02, *w3, g10, b10, alpha, "ffn_fwd_1a", carry=gather_of([attn_w_q[0], ffn2_wg[1], ffn2_wu[1]]))
    w_q = gwq.reshape(d, d)
    qq = _proj(xh4, g10, b10, w_q, "q_proj")
    ot, lse, gwo, g4d = _attn_fwd(
        qq, kk, vv, cc, cct, "attn_fwd", carry=gather_of([attn_w_o[0], ffn2_wd[1]]))
    w_o = gwo.reshape(d, d)
    g11, b11 = gb(1, 1)
    xh5, rs5, hb5 = _attn_out_fwd(ot, xh4, g10, b10, w_o, g11, b11, alpha, "attn_out_fwd")
    w4 = ffn_chunks(g4g, g4u, g4d)
    g12, b12 = gb(1, 2)
    xh6, rs6, _, gg6, uu6 = _ffn_fwd(xh5, g11, b11, *w4, g12, b12, alpha, "ffn_fwd_1b")


    dgain = [[None] * 3 for _ in range(depth)]
    dbias = [[None] * 3 for _ in range(depth)]

    def to_col_owners(g):
        return (False, _cols_to_devices(g).astype(BF16))

    def to_row_owners(g):
        return (False, g.reshape(N_DEV, g.shape[0] // N_DEV, g.shape[1]).astype(BF16))

    dh5, do6, dg6, du6, a6, dgain[1][2], dbias[1][2], loss_l = _ffn_bwd(
        None, xh6, rs6, g12, gg6, uu6, *w4, alpha, "ffn_bwd_1b", loss_target=loss_target[0], loss_bias=b12)
    loss = lax.psum(loss_l[0, 0], ("x", "y", "c"))
    dw4g, dw4u = _wgrad(hb5, [dg6, du6], "wgrad_up_1b")
    (dw4dt,) = _wgrad(do6, [a6], "wgrad_down_1b")

    dres4, dmix5, dot_t, delta, dgain[1][1], dbias[1][1] = _attn_out_bwd(dh5, xh5, rs5, g11, ot, w_o, alpha, "attn_out_bwd")
    (dwo,) = _wgrad(ot, [dmix5], "wgrad_wo")
    dq, dkk, dvv, dcs, drow = _attn_bwd(qq, kk, vv, cc, cct, lse, delta, dot_t, "attn_bwd")
    dh4 = _add_proj_nt(dres4, dq, w_q, "q_bwd")
    (dwq,) = _wgrad(hb4, [dq], "wgrad_wq")

    dh3a, do4, dg4, du4, a4, dgain[1][0], dbias[1][0], l4g, l4u, l4d, lwo = _ffn_bwd(
        dh4, xh4, rs4, g10, gg4, uu4, *w3, alpha, "ffn_bwd_1a",
        carry=[to_col_owners(dw4g), to_col_owners(dw4u), to_row_owners(dw4dt.T), to_row_owners(dwo)])
    dw3g, dw3u = _wgrad(hb3, [dg4, du4], "wgrad_up_1a")
    (dw3dt,) = _wgrad(do4, [a4], "wgrad_down_1a")

    dcq = jnp.pad(drow[:, :, 0, :].reshape(N_HEADS, t).T, ((0, 0), (0, LANES - N_HEADS)))
    dh3, dlogit, dfb = _kv_bwd(dkk, dvv, dcs, dcq, logit, dh3a, wk, wv, wf, "kv_bwd")
    dwk, dwv = _wgrad(hb3, [dkk, dvv], "wgrad_kv")
    (dwf,) = _wgrad(hb3, [dlogit], "wgrad_f")
    dkv = jnp.concatenate([dwk, dwv, dwf[:, :N_HEADS]], axis=1)

    dh2, do3, dg3, du3, a3, dgain[0][2], dbias[0][2], lwq, l3g, l3u, l3d, lkv = _ffn_bwd(
        dh3, xh3, rs3, g02, gg3, uu3, *w2, alpha, "ffn_bwd_0b",
        carry=[to_row_owners(dwq), to_col_owners(dw3g), to_col_owners(dw3u), to_row_owners(dw3dt.T),
               to_row_owners(dkv.T)])
    dw2g, dw2u = _wgrad(hb2, [dg3, du3], "wgrad_up_0b")
    (dw2dt,) = _wgrad(do3, [a3], "wgrad_down_0b")

    dh1, dmix2, dpp, dcw, dgain[0][1], dbias[0][1] = _conv_bwd(dh2, xh2, rs2, g01, pp, cw_f, w_in, w_out, alpha, "conv_bwd")
    (dwin,) = _wgrad(hb1, [dpp], "wgrad_conv_in")
    (dwout,) = _wgrad(mb, [dmix2], "wgrad_conv_out")

    dh0, do1, dg1, du1, a1, dgain[0][0], dbias[0][0], l2g, l2u, l2d, lcin, lcout = _ffn_bwd(
        dh1, xh1, rs1, g00, gg1, uu1, *w1, alpha, "ffn_bwd_0a",
        carry=[to_col_owners(dw2g), to_col_owners(dw2u), to_row_owners(dw2dt.T), to_col_owners(dwin),
               to_row_owners(dwout)])
    (dw1dt,) = _wgrad(do1, [a1], "wgrad_down_0a")
    dw1g, l1d = _wgrad(hb0, [dg1], "wgrad_upg_0a", carry=[to_row_owners(dw1dt.T)])
    dw1u, l1g = _wgrad(hb0, [du1], "wgrad_upu_0a", carry=[to_col_owners(dw1g)])
    dmeta = dh0[PAD:ROW0]
    dgain_f = jnp.stack([jnp.concatenate(r, axis=0) for r in dgain])
    dbias_f = jnp.stack([jnp.concatenate(r, axis=0) for r in dbias])
    small_full = [dmeta, dgain_f, dbias_f, dcw[None], dfb]
    small_full_shapes = [a.shape for a in small_full]
    l1u, gsmall_grads = _exchange([to_col_owners(dw1u), (True, _pack_rows(small_full, WIDTH, F32_ROWS))], "rs_last")

    grad_x = dh0[ROW0:].reshape(1, seq, d)
    rsmall = _sum_sources(gsmall_grads, "small_sum")
    smeta, sgain, sbias, scw, sfb = _unpack_rows(rsmall, small_full_shapes, WIDTH, F32_ROWS)
    csh = d // N_DEV

    def my_cols(a):
        return lax.dynamic_slice_in_dim(a, me * csh, csh, axis=a.ndim - 1)

    grads = {"meta": my_cols(smeta), "ln_gain": my_cols(sgain), "ln_bias": my_cols(sbias),
             "conv_w": my_cols(scw), "f_bias": sfb[0, :N_HEADS], "kv_w": _sum_sources(lkv, "kv_sum").T}
    landed = {"ffn1_wg": [l1g, l3g], "ffn1_wu": [l1u, l3u], "ffn1_wd": [l1d, l3d],
              "ffn2_wg": [l2g, l4g], "ffn2_wu": [l2u, l4u], "ffn2_wd": [l2d, l4d],
              "conv_w_in": [lcin], "conv_w_out": [lcout], "attn_w_q": [lwq], "attn_w_o": [lwo]}
    weights = dict(meta=meta, ffn1_wg=ffn1_wg, ffn1_wu=ffn1_wu, ffn1_wd=ffn1_wd, ffn2_wg=ffn2_wg,
                   ffn2_wu=ffn2_wu, ffn2_wd=ffn2_wd, ln_gain=ln_gain, ln_bias=ln_bias,
                   conv_w_in=conv_w_in, conv_w=conv_w, conv_w_out=conv_w_out, kv_w=kv_w,
                   f_bias=f_bias, attn_w_q=attn_w_q, attn_w_o=attn_w_o)
    moms = dict(meta=(m_meta, v_meta), ffn1_wg=(m_ffn1_wg, v_ffn1_wg), ffn1_wu=(m_ffn1_wu, v_ffn1_wu),
                ffn1_wd=(m_ffn1_wd, v_ffn1_wd), ffn2_wg=(m_ffn2_wg, v_ffn2_wg), ffn2_wu=(m_ffn2_wu, v_ffn2_wu),
                ffn2_wd=(m_ffn2_wd, v_ffn2_wd), ln_gain=(m_ln_gain, v_ln_gain), ln_bias=(m_ln_bias, v_ln_bias),
                conv_w_in=(m_conv_w_in, v_conv_w_in), conv_w=(m_conv_w, v_conv_w),
                conv_w_out=(m_conv_w_out, v_conv_w_out), kv_w=(m_kv_w, v_kv_w), f_bias=(m_f_bias, v_f_bias),
                attn_w_q=(m_attn_w_q, v_attn_w_q), attn_w_o=(m_attn_w_o, v_attn_w_o))

    names = list(weights)
    g_out, d_out, m_out, v_out = [], [], [], []
    for n in names:
        w = weights[n]
        shp = w.shape
        mm, vv_ = moms[n]
        if n in landed:
            three = (len(landed[n]),) + shp[-2:]
            g, dl, nm, nv = _reduce_adamw(w.reshape(three), mm.reshape(three), vv_.reshape(three),
                                          landed[n], "adamw_" + n)
            g = g.reshape(shp)
        else:
            two = (1, shp[0]) if w.ndim == 1 else (math.prod(shp[:-1]), shp[-1])
            g = grads[n].reshape(shp)
            dl, nm, nv = _adamw(w.reshape(two), g.reshape(two), mm.reshape(two), vv_.reshape(two), "adamw_" + n)
        g_out.append(g)
        d_out.append(dl.reshape(shp))
        m_out.append(nm.reshape(shp))
        v_out.append(nv.reshape(shp))
    return (loss, grad_x, *g_out, *d_out, *m_out, *v_out)
```
